```python
import math
import jax, jax.numpy as jnp
from jax import lax
import numpy as np

D_MODEL = 2048
BATCH = 1
SEQ = 8192
DEPTH = 1
DEC_BATCH = 32
DEC_SEQ = 1
PAST_LEN = 8192
PAGE_SIZE = 128

HEAD_A = 64
H_A = 16
D_A = H_A * HEAD_A
D_DECAY_LORA = 96
D_AAA_LORA = 96
D_GATE_LORA = 64
D_SHIFT = 3 * D_A + D_DECAY_LORA + D_AAA_LORA + D_GATE_LORA
EPS_GN = 64e-5
HEAD_B = 64
H_G = 8
WINDOWS = (128, 512, 2048)
DILATIONS = (1, 4, 16)
N_GROUPS = 3
H_B = H_G * N_GROUPS
D_B = H_B * HEAD_B
D_B_OUT = H_G * HEAD_B
BLK = 128
N_BUCKETS = 32
MAX_DISTANCE = 2048
D_IN = D_SHIFT + 3 * D_B + 2 * D_MODEL
D_FF = 5632
CONV_W = 3
EPS_RMS = 1e-6

kernel_name = 'hybrid_rwkv7_dilated_window_step'


def rms_norm(x, g):
    xf = x.astype(jnp.float32)
    y = xf * lax.rsqrt(jnp.mean(xf * xf, axis=-1, keepdims=True) + EPS_RMS)
    return (y * g.astype(jnp.float32)).astype(x.dtype)


def rel_bucket(dist):
    max_exact = N_BUCKETS // 2
    d_f = jnp.maximum(dist, 1).astype(jnp.float32)
    large = max_exact + (jnp.log(d_f / max_exact) / math.log(MAX_DISTANCE / max_exact)
                         * (N_BUCKETS - max_exact)).astype(jnp.int32)
    large = jnp.minimum(large, N_BUCKETS - 1)
    return jnp.where(dist < max_exact, dist, large)


def wkv7_scan(r, decay, k, v, kk, a, state0):
    def step(S, inp):
        r_t, w_t, k_t, v_t, kk_t, a_t = inp
        sa = jnp.einsum('bhij,bhj->bhi', S, -kk_t)
        S = (S * w_t[:, :, None, :] + sa[..., None] * (kk_t * a_t)[:, :, None, :]
             + v_t[..., None] * k_t[:, :, None, :])
        return S, jnp.einsum('bhij,bhj->bhi', S, r_t)
    xs = tuple(jnp.moveaxis(t.astype(jnp.float32), 1, 0) for t in (r, decay, k, v, kk, a))
    S, o = lax.scan(step, state0.astype(jnp.float32), xs)
    return jnp.moveaxis(o, 0, 1), S


def rwkv7_branch(p_shift, shift_prev, state0, mu_shift, w0, w_up_decay, a0, w_up_aaa,
                 w_up_gate, k_k, k_a, r_k, gn_g, gn_b):
    B, T, _ = p_shift.shape
    dt = p_shift.dtype
    prev = jnp.concatenate([shift_prev.astype(dt), p_shift], axis=1)[:, :T]
    xm = p_shift + mu_shift * (prev - p_shift)
    o1, o2, o3 = D_A, 2 * D_A, 3 * D_A
    o4 = o3 + D_DECAY_LORA
    o5 = o4 + D_AAA_LORA
    r, k, v = xm[..., :o1], xm[..., o1:o2], xm[..., o2:o3]
    xw, xa, xg = xm[..., o3:o4], xm[..., o4:o5], xm[..., o5:]
    w_raw = -jax.nn.softplus(-(w0 + jnp.tanh(xw) @ w_up_decay).astype(jnp.float32)) - 0.5
    decay = jnp.exp(-jnp.exp(w_raw))
    a = jax.nn.sigmoid(a0 + xa @ w_up_aaa)
    g = jax.nn.sigmoid(xg) @ w_up_gate
    heads = lambda t: t.reshape(B, T, H_A, HEAD_A)
    kk = heads(k * k_k).astype(jnp.float32)
    kk = kk / jnp.maximum(jnp.sqrt(jnp.sum(kk * kk, axis=-1, keepdims=True)), 1e-12)
    k = k * (1 + (a - 1) * k_a)
    o, S = wkv7_scan(heads(r), heads(decay), heads(k), heads(v), kk, heads(a), state0)
    mu = jnp.mean(o, axis=-1, keepdims=True)
    var = jnp.mean(jnp.square(o - mu), axis=-1, keepdims=True)
    o_n = ((o - mu) * lax.rsqrt(var + EPS_GN)).reshape(B, T, D_A) * gn_g + gn_b
    bonus = (jnp.sum(heads(r) * heads(k) * r_k, axis=-1, keepdims=True) * heads(v)).reshape(B, T, D_A)
    y = (o_n + bonus) * g
    return y.astype(dt), p_shift[:, -1:], S


def dilated_attn_prompt(q, k, v, bias_tab, win, dil):
    B, S, H, E = q.shape
    span = dil * BLK
    Lp = -(-S // span) * span
    L = Lp // dil
    nb = L // BLK
    reach = win // dil

    def blocks(t):
        t = jnp.pad(t, ((0, 0), (0, Lp - S), (0, 0), (0, 0)))
        return t.reshape(B, L, dil, H, E).transpose(0, 2, 1, 3, 4).reshape(B, dil, nb, BLK, H, E)

    def with_prev(t):
        prev = jnp.pad(t, ((0, 0), (0, 0), (1, 0), (0, 0), (0, 0), (0, 0)))[:, :, :nb]
        return jnp.concatenate([prev, t], axis=3)

    qb = blocks(q)
    kw, vw = with_prev(blocks(k)), with_prev(blocks(v))
    qi = jnp.arange(BLK)[:, None]
    ki = jnp.arange(2 * BLK)[None, :]
    rel = qi + BLK - ki
    bias = bias_tab[rel_bucket(dil * jnp.maximum(rel, 0))].transpose(2, 0, 1)
    key_pos = jnp.arange(nb)[:, None] * BLK + ki - BLK
    mask = ((rel >= 0) & (rel <= reach))[None] & (key_pos >= 0)[:, None, :]
    logits = jnp.einsum('bdnqhe,bdnkhe->bdnhqk', qb, kw,
                        preferred_element_type=jnp.float32) * (HEAD_B ** -0.5) + bias
    logits = jnp.where(mask[:, None], logits, -jnp.inf)
    m = jnp.max(logits, axis=-1, keepdims=True)
    p = jnp.exp(logits - m)
    s = jnp.sum(p, axis=-1, keepdims=True)
    out = jnp.einsum('bdnhqk,bdnkhe->bdnqhe', p, vw.astype(jnp.float32)) / jnp.swapaxes(s, 3, 4)
    lse = jnp.swapaxes((m + jnp.log(s))[..., 0], 3, 4)

    def unblock(t):
        rest = t.shape[4:]
        t = t.reshape((B, dil, L) + rest)
        return jnp.swapaxes(t, 1, 2).reshape((B, Lp) + rest)[:, :S]

    return unblock(out).astype(q.dtype), unblock(lse)


def dilated_attn_step(q, k_new, v_new, cache, bias_tab, win, dil):
    Lb = cache.shape[1]
    T = q.shape[1]
    k_all = jnp.concatenate([cache[:, :, 0].astype(k_new.dtype), k_new], axis=1)
    v_all = jnp.concatenate([cache[:, :, 1].astype(v_new.dtype), v_new], axis=1)
    steps = jnp.arange(win // dil + 1)
    idx = Lb + jnp.arange(T)[:, None] - dil * steps[None, :]
    valid = idx >= 0
    idx = jnp.maximum(idx, 0)
    kg, vg = k_all[:, idx], v_all[:, idx]
    bias = bias_tab[rel_bucket(dil * steps)].T
    logits = jnp.einsum('bthe,btmhe->bthm', q, kg,
                        preferred_element_type=jnp.float32) * (HEAD_B ** -0.5) + bias
    logits = jnp.where(valid[None, :, None, :], logits, -jnp.inf)
    m = jnp.max(logits, axis=-1, keepdims=True)
    p = jnp.exp(logits - m)
    s = jnp.sum(p, axis=-1, keepdims=True)
    out = jnp.einsum('bthm,btmhe->bthe', p, vg.astype(jnp.float32)) / s
    return out.astype(q.dtype), (m + jnp.log(s))[..., 0]


def merge_groups(outs, lses):
    o = jnp.stack(outs, axis=0).astype(jnp.float32)
    w = jax.nn.softmax(jnp.stack(lses, axis=0), axis=0)
    return jnp.einsum('gbth,gbthe->bthe', w, o).astype(outs[0].dtype)


def trunk_layer(x, shift_prev, wkv_prev, conv_prev, win_caches, rel_bias,
                norm1_g, w_in, gate_b, mu_shift, w0, w_up_decay, a0, w_up_aaa, w_up_gate,
                k_k, k_a, r_k, gn_g, gn_b, w_out_a, w_out_b, w_o, norm2_g, w_up, conv_w,
                conv_b, w_down):
    B, T, _ = x.shape
    xn = rms_norm(x, norm1_g)
    proj = xn @ w_in
    p_shift = proj[..., :D_SHIFT]
    qkv = proj[..., D_SHIFT:D_SHIFT + 3 * D_B]
    gates = jax.nn.sigmoid(proj[..., D_SHIFT + 3 * D_B:] + gate_b)
    g_a, g_b = gates[..., :D_MODEL], gates[..., D_MODEL:]
    ya, shift_new, wkv_new = rwkv7_branch(p_shift, shift_prev, wkv_prev, mu_shift, w0, w_up_decay,
                                          a0, w_up_aaa, w_up_gate, k_k, k_a, r_k, gn_g, gn_b)
    q = qkv[..., :D_B].reshape(B, T, N_GROUPS, H_G, HEAD_B)
    k = qkv[..., D_B:2 * D_B].reshape(B, T, N_GROUPS, H_G, HEAD_B)
    v = qkv[..., 2 * D_B:].reshape(B, T, N_GROUPS, H_G, HEAD_B)
    outs, lses, kv_new = [], [], []
    for gi in range(N_GROUPS):
        tab = rel_bias[:, gi * H_G:(gi + 1) * H_G]
        qg, kg, vg = q[:, :, gi], k[:, :, gi], v[:, :, gi]
        if win_caches is None:
            o, l = dilated_attn_prompt(qg, kg, vg, tab, WINDOWS[gi], DILATIONS[gi])
            keep = min(WINDOWS[gi], T)
            kv_new.append(jnp.stack([kg[:, T - keep:], vg[:, T - keep:]], axis=2))
        else:
            o, l = dilated_attn_step(qg, kg, vg, win_caches[gi], tab, WINDOWS[gi], DILATIONS[gi])
            kv_new.append(jnp.stack([kg, vg], axis=2))
        outs.append(o)
        lses.append(l)
    yb = merge_groups(outs, lses).reshape(B, T, D_B_OUT)
    mixed = g_a * (ya @ w_out_a) + g_b * (yb @ w_out_b)
    h = x + mixed @ w_o
    hn = rms_norm(h, norm2_g)
    up = hn @ w_up
    up_ext = jnp.concatenate([conv_prev.astype(up.dtype), up], axis=1)
    conv = conv_b + conv_w[0] * up_ext[:, 0:T]
    for i in range(1, CONV_W):
        conv = conv + conv_w[i] * up_ext[:, i:i + T]
    ffn = (jax.nn.gelu(conv[..., :D_FF], approximate=False) * conv[..., D_FF:]) @ w_down
    return h + ffn, shift_new, wkv_new, up_ext[:, T:], kv_new


def setup_inputs(seed: int = 0) -> dict:
    key = jax.random.key(seed)
    ks = jax.random.split(key, 32)
    f32 = jnp.float32
    nrm = lambda k, shape, s: jax.random.normal(k, shape, f32) * s
    L = DEPTH
    return {
        'x_prompt': nrm(ks[0], (BATCH, SEQ, D_MODEL), 1.0),
        'x_sample': nrm(ks[1], (DEC_BATCH, DEC_SEQ, D_MODEL), 1.0),
        'state_wkv': nrm(ks[2], (L, DEC_BATCH, H_A, HEAD_A, HEAD_A), 0.3),
        'state_shift': nrm(ks[3], (L, DEC_BATCH, 1, D_SHIFT), 1.0),
        'state_ffn_conv': nrm(ks[4], (L, DEC_BATCH, CONV_W - 1, 2 * D_FF), 0.6),
        'cache_win1': nrm(ks[5], (L, DEC_BATCH, min(WINDOWS[0], PAST_LEN), 2, H_G, HEAD_B), 1.0),
        'cache_win2': nrm(ks[6], (L, DEC_BATCH, min(WINDOWS[1], PAST_LEN), 2, H_G, HEAD_B), 1.0),
        'cache_win3': nrm(ks[7], (L, DEC_BATCH, min(WINDOWS[2], PAST_LEN), 2, H_G, HEAD_B), 1.0),
        'rel_bias': nrm(ks[8], (N_BUCKETS, H_B), 0.5),
        'norm1_g': 1.0 + nrm(ks[9], (L, D_MODEL), 0.02),
        'w_in': nrm(ks[10], (L, D_MODEL, D_IN), D_MODEL ** -0.5),
        'gate_b': nrm(ks[11], (L, 2 * D_MODEL), 0.02),
        'mu_shift': jax.random.uniform(ks[12], (L, D_SHIFT), f32, 0.0, 1.0),
        'w0': jax.random.uniform(ks[13], (L, D_A), f32, -6.0, -1.0),
        'w_up_decay': nrm(ks[14], (L, D_DECAY_LORA, D_A), 0.5 * D_DECAY_LORA ** -0.5),
        'a0': nrm(ks[15], (L, D_A), 0.1),
        'w_up_aaa': nrm(ks[16], (L, D_AAA_LORA, D_A), D_AAA_LORA ** -0.5),
        'w_up_gate': nrm(ks[17], (L, D_GATE_LORA, D_A), D_GATE_LORA ** -0.5),
        'k_k': 0.85 + nrm(ks[18], (L, D_A), 0.05),
        'k_a': 1.0 + nrm(ks[19], (L, D_A), 0.05),
        'r_k': nrm(ks[20], (L, H_A, HEAD_A), 0.1),
        'gn_g': 1.0 + nrm(ks[21], (L, D_A), 0.02),
        'gn_b': nrm(ks[22], (L, D_A), 0.02),
        'w_out_a': nrm(ks[23], (L, D_A, D_MODEL), D_A ** -0.5),
        'w_out_b': nrm(ks[24], (L, D_B_OUT, D_MODEL), D_B_OUT ** -0.5),
        'w_o': nrm(ks[25], (L, D_MODEL, D_MODEL), D_MODEL ** -0.5),
        'norm2_g': 1.0 + nrm(ks[26], (L, D_MODEL), 0.02),
        'w_up': nrm(ks[27], (L, D_MODEL, 2 * D_FF), D_MODEL ** -0.5),
        'conv_w': nrm(ks[28], (L, CONV_W, 2 * D_FF), CONV_W ** -0.5),
        'conv_b': nrm(ks[29], (L, 2 * D_FF), 0.02),
        'w_down': nrm(ks[30], (L, D_FF, D_MODEL), D_FF ** -0.5),
        'normf_g': 1.0 + nrm(ks[31], (D_MODEL,), 0.02),
    }


def reference(x_prompt, x_sample, state_wkv, state_shift, state_ffn_conv, cache_win1, cache_win2,
              cache_win3, rel_bias, norm1_g, w_in, gate_b, mu_shift, w0, w_up_decay, a0, w_up_aaa,
              w_up_gate, k_k, k_a, r_k, gn_g, gn_b, w_out_a, w_out_b, w_o, norm2_g, w_up, conv_w,
              conv_b, w_down, normf_g):
    B = x_prompt.shape[0]
    dt = x_prompt.dtype
    hp, hs = x_prompt, x_sample
    wkv_p, wkv_s, sh_p, sh_s, cv_p, cv_s = [], [], [], [], [], []
    w1_p, w1_s, w2_p, w2_s, w3_p, w3_s = [], [], [], [], [], []
    for l in range(DEPTH):
        lw = (norm1_g[l], w_in[l], gate_b[l], mu_shift[l], w0[l], w_up_decay[l], a0[l], w_up_aaa[l],
              w_up_gate[l], k_k[l], k_a[l], r_k[l], gn_g[l], gn_b[l], w_out_a[l], w_out_b[l], w_o[l],
              norm2_g[l], w_up[l], conv_w[l], conv_b[l], w_down[l])
        hp, s_sh, s_wkv, s_cv, s_kv = trunk_layer(
            hp, jnp.zeros((B, 1, D_SHIFT), dt), jnp.zeros((B, H_A, HEAD_A, HEAD_A), jnp.float32),
            jnp.zeros((B, CONV_W - 1, 2 * D_FF), dt), None, rel_bias, *lw)
        wkv_p.append(s_wkv); sh_p.append(s_sh); cv_p.append(s_cv)
        w1_p.append(s_kv[0]); w2_p.append(s_kv[1]); w3_p.append(s_kv[2])
        hs, s_sh, s_wkv, s_cv, s_kv = trunk_layer(
            hs, state_shift[l], state_wkv[l], state_ffn_conv[l],
            (cache_win1[l], cache_win2[l], cache_win3[l]), rel_bias, *lw)
        wkv_s.append(s_wkv); sh_s.append(s_sh); cv_s.append(s_cv)
        w1_s.append(s_kv[0]); w2_s.append(s_kv[1]); w3_s.append(s_kv[2])
    y_prompt = rms_norm(hp, normf_g)
    y_sample = rms_norm(hs, normf_g)
    return (y_prompt, y_sample,
            jnp.stack(wkv_p), jnp.stack(wkv_s),
            jnp.stack(sh_p), jnp.stack(sh_s),
            jnp.stack(cv_p), jnp.stack(cv_s),
            jnp.stack(w1_p), jnp.stack(w1_s),
            jnp.stack(w2_p), jnp.stack(w2_s),
            jnp.stack(w3_p), jnp.stack(w3_s))
```

```python
import functools
import math

import numpy as np
import jax
import jax.numpy as jnp
from jax import lax
from jax.experimental import pallas as pl
from jax.experimental.pallas import tpu as pltpu

F32 = jnp.float32
BF16 = jnp.bfloat16
HIGHEST = lax.Precision.HIGHEST

D_MODEL = 2048
SEQ = 8192
DEC_BATCH = 32
HEAD_A = 64
H_A = 16
D_A = H_A * HEAD_A
D_DECAY_LORA = 96
D_AAA_LORA = 96
D_GATE_LORA = 64
D_LORA = D_DECAY_LORA + D_AAA_LORA + D_GATE_LORA
D_SHIFT = 3 * D_A + D_LORA
EPS_GN = 64e-5
HEAD_B = 64
H_G = 8
D_G = H_G * HEAD_B
WINDOWS = (128, 512, 2048)
DILATIONS = (1, 4, 16)
N_GROUPS = 3
D_B = N_GROUPS * D_G
BLK = 128
N_BUCKETS = 32
MAX_DISTANCE = 2048
COL_QKV = D_SHIFT
COL_GATE = D_SHIFT + 3 * D_B
D_IN = COL_GATE + 2 * D_MODEL
D_FF = 5632
CONV_W = 3
EPS_RMS = 1e-6
CHUNK = 64
LANES = 128
VMEM_LIMIT = 56 * 1024 * 1024


def _params(sem, vmem=VMEM_LIMIT):
    return pltpu.CompilerParams(dimension_semantics=sem, vmem_limit_bytes=vmem)


def _sigmoid(x):
    return 1.0 / (1.0 + jnp.exp(-x))


def _dot_bf16(a, b):
    return jnp.dot(a, b, preferred_element_type=F32)


def _dot_f32(a, b):
    return jnp.dot(a, b, precision=HIGHEST, preferred_element_type=F32)


def _split3(x):
    hi = x.astype(BF16)
    r1 = x - hi.astype(F32)
    mid = r1.astype(BF16)
    lo = (r1 - mid.astype(F32)).astype(BF16)
    return hi, mid, lo


def _dot_exact_rhs(x, m):
    hi, mid, lo = _split3(x)
    return _dot_bf16(hi, m) + _dot_bf16(mid, m) + _dot_bf16(lo, m)


def _dot_exact_lhs(m, x):
    hi, mid, lo = _split3(x)
    return _dot_bf16(m, hi) + _dot_bf16(m, mid) + _dot_bf16(m, lo)


def _rms_kernel(x_ref, g_ref, o_ref):
    x = x_ref[...]
    inv = lax.rsqrt(jnp.mean(x * x, axis=-1, keepdims=True) + EPS_RMS)
    o_ref[...] = (x * inv * g_ref[...]).astype(o_ref.dtype)


def _rmsnorm(x, g, tm, name):
    m, d = x.shape
    return pl.pallas_call(
        _rms_kernel,
        out_shape=jax.ShapeDtypeStruct((m, d), BF16),
        grid=(m // tm,),
        in_specs=[pl.BlockSpec((tm, d), lambda i: (i, 0)), pl.BlockSpec((1, d), lambda i: (0, 0))],
        out_specs=pl.BlockSpec((tm, d), lambda i: (i, 0)),
        compiler_params=_params(("parallel",)),
        name=name,
    )(x, g.reshape(1, d))


def _proj_kernel(a_ref, w_ref, b_ref, o_ref, *, gate_block):
    acc = _dot_bf16(a_ref[...], w_ref[...])
    j = pl.program_id(1)

    @pl.when(j < gate_block)
    def _():
        o_ref[...] = acc

    @pl.when(j >= gate_block)
    def _():
        o_ref[...] = _sigmoid(acc + b_ref[...])


def _matmul(a, w, bias, tm, tn, gate_col, name):
    m, k = a.shape
    n = w.shape[1]
    return pl.pallas_call(
        functools.partial(_proj_kernel, gate_block=gate_col // tn),
        out_shape=jax.ShapeDtypeStruct((m, n), F32),
        grid=(m // tm, n // tn),
        in_specs=[
            pl.BlockSpec((tm, k), lambda i, j: (i, 0)),
            pl.BlockSpec((k, tn), lambda i, j: (0, j)),
            pl.BlockSpec((1, tn), lambda i, j: (0, j)),
        ],
        out_specs=pl.BlockSpec((tm, tn), lambda i, j: (i, j)),
        compiler_params=_params(("parallel", "arbitrary")),
        name=name,
    )(a, w, bias)


def _prep_kernel(p_ref, prev_ref, mu_ref, wl_ref, w0_ref, a0_ref, kk_ref, ka_ref, rk_ref, lmat_ref, sel_ref,
                 bd_ref, rt_ref, kt_ref, qt_ref, pt_ref, v_ref, bonus_ref, g_ref, gc_ref, *, rows_are_time):
    p = p_ref[...]
    tm = p.shape[0]
    if rows_are_time:
        last = jnp.where(pl.program_id(0) == 0, 0.0, prev_ref[7:8, :])
        row = lax.broadcasted_iota(jnp.int32, (tm, 1), 0)
        prev = jnp.where(row == 0, last, pltpu.roll(p, 1, axis=0))
    else:
        prev = prev_ref[...]
    xm = p + mu_ref[...] * (prev - p)
    r = xm[:, 0:D_A]
    k = xm[:, D_A:2 * D_A]
    v = xm[:, 2 * D_A:3 * D_A]
    xl = xm[:, 3 * D_A:D_SHIFT]
    lane = lax.broadcasted_iota(jnp.int32, xl.shape, 1)
    act = jnp.where(lane < D_DECAY_LORA, jnp.tanh(xl),
                    jnp.where(lane < D_DECAY_LORA + D_AAA_LORA, xl, _sigmoid(xl)))
    lora = _dot_f32(act, wl_ref[...])
    y = -(w0_ref[...] + lora[:, 0:D_A])
    softplus = jnp.maximum(y, 0.0) + jnp.log(1.0 + jnp.exp(-jnp.abs(y)))
    logw = -jnp.exp(-softplus - 0.5)
    a = _sigmoid(a0_ref[...] + lora[:, D_A:2 * D_A])
    g = lora[:, 2 * D_A:3 * D_A]
    kkr = k * kk_ref[...]
    kp = k * (1.0 + (a - 1.0) * ka_ref[...])
    seg = _dot_exact_rhs(jnp.concatenate([kkr * kkr, r * kp * rk_ref[...]], axis=0), bd_ref[...])
    kk = kkr / jnp.maximum(jnp.sqrt(seg[0:tm]), 1e-12)
    cum = _dot_exact_lhs(lmat_ref[...], logw)
    e_out = jnp.exp(-cum)
    rt_ref[...] = r * jnp.exp(cum)
    kt_ref[...] = kp * e_out
    qt_ref[...] = kk * a * e_out
    pt_ref[...] = -kk * jnp.exp(cum - logw)
    v_ref[...] = v
    bonus_ref[...] = seg[tm:2 * tm] * v
    g_ref[...] = g
    gc_ref[...] = jnp.exp(_dot_exact_lhs(sel_ref[...], cum))


def _rwkv_prep(proj, prev, tm, chunk, lw, name):
    m = proj.shape[0]
    rows_are_time = prev is None
    n_tiles = m // tm
    t = np.arange(tm)
    lmat = ((t[:, None] // chunk == t[None, :] // chunk) & (t[None, :] <= t[:, None])).astype(np.float32)
    sel_stride = 1 if chunk == 1 else 8
    n_sel = sel_stride * (tm // chunk)
    sel = np.zeros((n_sel, tm), np.float32)
    for c in range(tm // chunk):
        sel[sel_stride * c, (c + 1) * chunk - 1] = 1.0
    if rows_are_time:
        prev_arr = proj
        prev_spec = pl.BlockSpec((8, D_SHIFT), lambda i: (jnp.maximum(i * (tm // 8) - 1, 0), 0))
    else:
        prev_arr = prev
        prev_spec = pl.BlockSpec((tm, D_SHIFT), lambda i: (i, 0))
    vec = lambda d: pl.BlockSpec((1, d), lambda i: (0, 0))
    full = lambda a: pl.BlockSpec(a.shape, lambda i: (0, 0))
    big = pl.BlockSpec((tm, D_A), lambda i: (i, 0))
    out = jax.ShapeDtypeStruct((m, D_A), F32)
    lmat_b = jnp.asarray(lmat, BF16)
    sel_b = jnp.asarray(sel, BF16)
    return pl.pallas_call(
        functools.partial(_prep_kernel, rows_are_time=rows_are_time),
        out_shape=[out] * 7 + [jax.ShapeDtypeStruct((n_tiles * n_sel, D_A), F32)],
        grid=(n_tiles,),
        in_specs=[pl.BlockSpec((tm, D_SHIFT), lambda i: (i, 0)), prev_spec, vec(D_SHIFT), full(lw["w_lora"]),
                  vec(D_A), vec(D_A), vec(D_A), vec(D_A), vec(D_A), full(lmat_b), full(sel_b), full(lw["bd"])],
        out_specs=[big] * 7 + [pl.BlockSpec((n_sel, D_A), lambda i: (i, 0))],
        compiler_params=_params(("parallel",)),
        name=name,
    )(proj, prev_arr, lw["mu"], lw["w_lora"], lw["w0"], lw["a0"], lw["k_k"], lw["k_a"], lw["r_k"], lmat_b, sel_b,
      lw["bd"])


def _scan_kernel(pt_ref, rt_ref, qt_ref, kt_ref, v_ref, gc_ref, o_ref, sfin_ref, s_scr):
    @pl.when(pl.program_id(0) == 0)
    def _():
        s_scr[...] = jnp.zeros_like(s_scr)

    ri = lax.broadcasted_iota(jnp.int32, (CHUNK, CHUNK), 0)
    ci = lax.broadcasted_iota(jnp.int32, (CHUNK, CHUNK), 1)
    strict = ri > ci
    incl = ri >= ci
    eye = (ri == ci).astype(F32)
    outs = []
    for hp in range(H_A // 2):
        lanes = slice(hp * LANES, (hp + 1) * LANES)
        pt2, rt2, qt2, kt2, v2 = (ref[:, lanes] for ref in (pt_ref, rt_ref, qt_ref, kt_ref, v_ref))
        gc2 = gc_ref[0:1, lanes]
        pair = []
        for sub in range(2):
            h = 2 * hp + sub
            ls = slice(sub * HEAD_A, (sub + 1) * HEAD_A)
            vv = v2[:, ls]
            pr = jnp.concatenate([pt2[:, ls], rt2[:, ls]], axis=0)
            qk = jnp.concatenate([qt2[:, ls], kt2[:, ls]], axis=0)
            gram = lax.dot_general(pr, qk, (((1,), (1,)), ((), ())), precision=HIGHEST,
                                   preferred_element_type=F32)
            a_qp = jnp.where(strict, gram[0:CHUNK, 0:CHUNK], 0.0)
            a_kp = jnp.where(strict, gram[0:CHUNK, CHUNK:], 0.0)
            a_rq = jnp.where(incl, gram[CHUNK:, 0:CHUNK], 0.0)
            a_rk = jnp.where(incl, gram[CHUNK:, CHUNK:], 0.0)
            tinv = eye + a_qp
            pw = a_qp
            for _ in range(5):
                pw = _dot_f32(pw, pw)
                tinv = tinv + _dot_f32(tinv, pw)
            s0 = s_scr[h]
            base = lax.dot_general(pr, s0, (((1,), (1,)), ((), ())), precision=HIGHEST,
                                   preferred_element_type=F32)
            u = _dot_f32(tinv, base[0:CHUNK] + _dot_f32(a_kp, vv))
            uv = jnp.concatenate([u, vv], axis=0)
            o = base[CHUNK:] + _dot_f32(jnp.concatenate([a_rq, a_rk], axis=1), uv)
            s_scr[h] = (s0 + _dot_f32(uv.T, qk)) * gc2[:, ls]
            pair.append(o)
        outs.append(jnp.concatenate(pair, axis=1))
    o_ref[...] = jnp.concatenate(outs, axis=1)

    @pl.when(pl.program_id(0) == pl.num_programs(0) - 1)
    def _():
        sfin_ref[...] = s_scr[...]


def _wkv_scan(pt, rt, qt, kt, v, gc):
    t = pt.shape[0]
    blk = pl.BlockSpec((CHUNK, D_A), lambda c: (c, 0))
    return pl.pallas_call(
        _scan_kernel,
        out_shape=[jax.ShapeDtypeStruct((t, D_A), F32), jax.ShapeDtypeStruct((H_A, HEAD_A, HEAD_A), F32)],
        grid=(t // CHUNK,),
        in_specs=[blk] * 5 + [pl.BlockSpec((8, D_A), lambda c: (c, 0))],
        out_specs=[blk, pl.BlockSpec((H_A, HEAD_A, HEAD_A), lambda c: (0, 0, 0))],
        scratch_shapes=[pltpu.VMEM((H_A, HEAD_A, HEAD_A), F32)],
        compiler_params=_params(("arbitrary",)),
        name="wkv_scan_prompt",
    )(pt, rt, qt, kt, v, gc)


def _wkv_step_kernel(s_ref, pt_ref, rt_ref, qt_ref, kt_ref, v_ref, gc_ref, sn_ref, o_ref):
    ri = lax.broadcasted_iota(jnp.int32, (HEAD_A, HEAD_A), 0)
    ci = lax.broadcasted_iota(jnp.int32, (HEAD_A, HEAD_A), 1)
    eye = ri == ci
    pt, rt, qt, kt, v, gc = (ref[0] for ref in (pt_ref, rt_ref, qt_ref, kt_ref, v_ref, gc_ref))
    outs = []
    for h in range(H_A):
        ls = slice(h * HEAD_A, (h + 1) * HEAD_A)
        s = s_ref[0, 0, h]
        u = jnp.sum(s * pt[:, ls], axis=1, keepdims=True)
        v_col = jnp.sum(jnp.where(eye, v[:, ls], 0.0), axis=1, keepdims=True)
        m = s + u * qt[:, ls] + v_col * kt[:, ls]
        sn_ref[0, 0, h] = m * gc[:, ls]
        o_col = jnp.sum(m * rt[:, ls], axis=1, keepdims=True)
        outs.append(jnp.sum(jnp.where(eye, o_col, 0.0), axis=0, keepdims=True))
    o_ref[0] = jnp.concatenate(outs, axis=1)


def _wkv_step(state, pt, rt, qt, kt, v, gc):
    b = pt.shape[0]
    rows = [x.reshape(b, 1, D_A) for x in (pt, rt, qt, kt, v, gc)]
    row_spec = pl.BlockSpec((1, 1, D_A), lambda i: (i, 0, 0))
    st_spec = pl.BlockSpec((1, 1, H_A, HEAD_A, HEAD_A), lambda i: (0, i, 0, 0, 0))
    sn, o = pl.pallas_call(
        _wkv_step_kernel,
        out_shape=[jax.ShapeDtypeStruct(state.shape, F32), jax.ShapeDtypeStruct((b, 1, D_A), F32)],
        grid=(b,),
        in_specs=[st_spec] + [row_spec] * 6,
        out_specs=[st_spec, row_spec],
        compiler_params=_params(("parallel",)),
        name="wkv_step_sample",
    )(state, *rows)
    return sn, o.reshape(b, D_A)


def _post_kernel(o_ref, bonus_ref, g_ref, gng_ref, gnb_ref, bd_ref, y_ref):
    o = o_ref[...]
    bd = bd_ref[...]
    mu = _dot_exact_rhs(o, bd) * (1.0 / HEAD_A)
    d = o - mu
    var = _dot_exact_rhs(d * d, bd) * (1.0 / HEAD_A)
    o_n = d * lax.rsqrt(var + EPS_GN) * gng_ref[...] + gnb_ref[...]
    y_ref[...] = ((o_n + bonus_ref[...]) * g_ref[...]).astype(y_ref.dtype)


def _rwkv_post(o, bonus, g, lw, tm, name):
    m = o.shape[0]
    big = pl.BlockSpec((tm, D_A), lambda i: (i, 0))
    vec = pl.BlockSpec((1, D_A), lambda i: (0, 0))
    return pl.pallas_call(
        _post_kernel,
        out_shape=jax.ShapeDtypeStruct((m, D_A), BF16),
        grid=(m // tm,),
        in_specs=[big, big, big, vec, vec, pl.BlockSpec((D_A, D_A), lambda i: (0, 0))],
        out_specs=big,
        compiler_params=_params(("parallel",)),
        name=name,
    )(o, bonus, g, lw["gn_g"], lw["gn_b"], lw["bd"])


def _attn_kernel(q_ref, kc_ref, vc_ref, kh_ref, vh_ref, bias_ref, o_ref, l_ref, *, dil, m_blocks):
    scale = HEAD_B ** -0.5
    col = lax.broadcasted_iota(jnp.int32, (BLK, 2 * BLK), 1)
    no_prev = jnp.logical_and(pl.program_id(0) == 0, col < BLK)

    def rows(ref, start, size):
        if dil == 1:
            return ref[pl.ds(start, size), :]
        return ref[pl.ds(start, size, stride=dil), :]

    for r in range(dil):
        for mb in range(m_blocks):
            base = r + dil * BLK * mb
            q = rows(q_ref, base, BLK)
            if mb == 0:
                k = jnp.concatenate([rows(kh_ref, r, BLK), rows(kc_ref, r, BLK)], axis=0)
                v = jnp.concatenate([rows(vh_ref, r, BLK), rows(vc_ref, r, BLK)], axis=0)
            else:
                k = rows(kc_ref, base - dil * BLK, 2 * BLK)
                v = rows(vc_ref, base - dil * BLK, 2 * BLK)
            outs, lses = [], []
            for sub in range(2):
                ls = slice(sub * HEAD_B, (sub + 1) * HEAD_B)
                s = lax.dot_general(q[:, ls].astype(BF16), k[:, ls].astype(BF16), (((1,), (1,)), ((), ())),
                                    preferred_element_type=F32) * scale + bias_ref[sub]
                if mb == 0:
                    s = jnp.where(no_prev, -jnp.inf, s)
                m = jnp.max(s, axis=-1, keepdims=True)
                p = jnp.exp(s - m)
                l = jnp.sum(p, axis=-1, keepdims=True)
                outs.append(_dot_bf16(p.astype(BF16), v[:, ls].astype(BF16)) / l)
                lses.append(jnp.broadcast_to(m + jnp.log(l), (BLK, HEAD_B)))
            o_blk = jnp.concatenate(outs, axis=1)
            l_blk = jnp.concatenate(lses, axis=1)
            if dil == 1:
                o_ref[pl.ds(base, BLK), :] = o_blk
                l_ref[pl.ds(base, BLK), :] = l_blk
            else:
                o_ref[pl.ds(base, BLK, stride=dil), :] = o_blk
                l_ref[pl.ds(base, BLK, stride=dil), :] = l_blk


def _attn_prompt(proj, bias, gi, rows_per_step):
    t = proj.shape[0]
    dil = DILATIONS[gi]
    span = BLK * dil
    m_blocks = rows_per_step // span
    cq = (COL_QKV + gi * D_G) // LANES
    ck = (COL_QKV + D_B + gi * D_G) // LANES
    cv = (COL_QKV + 2 * D_B + gi * D_G) // LANES
    cur = lambda c0: pl.BlockSpec((rows_per_step, LANES), lambda i, hp: (i, c0 + hp))
    halo = lambda c0: pl.BlockSpec((span, LANES), lambda i, hp: (jnp.maximum(i * m_blocks - 1, 0), c0 + hp))
    out_spec = pl.BlockSpec((rows_per_step, LANES), lambda i, hp: (i, hp))
    out = jax.ShapeDtypeStruct((t, D_G), F32)
    return pl.pallas_call(
        functools.partial(_attn_kernel, dil=dil, m_blocks=m_blocks),
        out_shape=[out, out],
        grid=(t // rows_per_step, H_G // 2),
        in_specs=[cur(cq), cur(ck), cur(cv), halo(ck), halo(cv),
                  pl.BlockSpec((2, BLK, 2 * BLK), lambda i, hp: (hp, 0, 0))],
        out_specs=[out_spec, out_spec],
        compiler_params=_params(("parallel", "parallel")),
        name=f"attn_prompt_g{gi}",
    )(proj, proj, proj, proj, proj, bias)


def _attn_step_kernel(q_ref, kn_ref, vn_ref, c_ref, bias_ref, o_ref, l_ref):
    scale = HEAD_B ** -0.5
    q = q_ref[...]
    kc = jnp.concatenate([c_ref[:, 0], kn_ref[...]], axis=0)
    vc = jnp.concatenate([c_ref[:, 1], vn_ref[...]], axis=0)
    s = jnp.sum(kc * q, axis=-1, keepdims=True) * scale + bias_ref[...]
    m = jnp.max(s, axis=0, keepdims=True)
    p = jnp.exp(s - m)
    l = jnp.sum(p, axis=0, keepdims=True)
    o_ref[...] = jnp.sum(p * vc, axis=0, keepdims=True) / l
    l_ref[...] = jnp.broadcast_to(m + jnp.log(l), (1, H_G, HEAD_B))


def _attn_step(q, k_new, v_new, cache, bias, gi):
    b = q.shape[0]
    dil = DILATIONS[gi]
    n_rows = cache.shape[2] // dil
    cache7 = cache.reshape(1, b, n_rows, dil, 2, H_G, HEAD_B)
    vec = pl.BlockSpec((1, H_G, HEAD_B), lambda i: (i, 0, 0))
    out = jax.ShapeDtypeStruct((b, H_G, HEAD_B), F32)
    o, l = pl.pallas_call(
        _attn_step_kernel,
        out_shape=[out, out],
        grid=(b,),
        in_specs=[vec, vec, vec,
                  pl.BlockSpec((None, None, n_rows, None, 2, H_G, HEAD_B), lambda i: (0, i, 0, 0, 0, 0, 0)),
                  pl.BlockSpec((n_rows + 1, H_G, 1), lambda i: (0, 0, 0))],
        out_specs=[vec, vec],
        compiler_params=_params(("parallel",)),
        name=f"attn_step_g{gi}",
    )(q, k_new, v_new, cache7, bias)
    return o.reshape(b, D_G), l.reshape(b, D_G)


def _merge_kernel(o1_ref, o2_ref, o3_ref, l1_ref, l2_ref, l3_ref, y_ref):
    l1, l2, l3 = l1_ref[...], l2_ref[...], l3_ref[...]
    m = jnp.maximum(jnp.maximum(l1, l2), l3)
    e1, e2, e3 = jnp.exp(l1 - m), jnp.exp(l2 - m), jnp.exp(l3 - m)
    den = e1 + e2 + e3
    y = (e1 / den) * o1_ref[...] + (e2 / den) * o2_ref[...] + (e3 / den) * o3_ref[...]
    y_ref[...] = y.astype(y_ref.dtype)


def _merge(outs, lses, tm, name):
    m = outs[0].shape[0]
    blk = pl.BlockSpec((tm, D_G), lambda i: (i, 0))
    return pl.pallas_call(
        _merge_kernel,
        out_shape=jax.ShapeDtypeStruct((m, D_G), BF16),
        grid=(m // tm,),
        in_specs=[blk] * 6,
        out_specs=blk,
        compiler_params=_params(("parallel",)),
        name=name,
    )(*outs, *lses)


def _mix_kernel(ya_ref, yb_ref, wa_ref, wb_ref, ga_ref, gb_ref, o_ref):
    mixed = ga_ref[...] * _dot_bf16(ya_ref[...], wa_ref[...]) + gb_ref[...] * _dot_bf16(yb_ref[...], wb_ref[...])
    o_ref[...] = mixed.astype(o_ref.dtype)


def _mix(ya, yb, wa, wb, proj, tm, tn, name):
    m = ya.shape[0]
    ga0 = COL_GATE // tn
    gb0 = (COL_GATE + D_MODEL) // tn
    return pl.pallas_call(
        _mix_kernel,
        out_shape=jax.ShapeDtypeStruct((m, D_MODEL), BF16),
        grid=(m // tm, D_MODEL // tn),
        in_specs=[
            pl.BlockSpec((tm, D_A), lambda i, j: (i, 0)),
            pl.BlockSpec((tm, D_G), lambda i, j: (i, 0)),
            pl.BlockSpec((D_A, tn), lambda i, j: (0, j)),
            pl.BlockSpec((D_G, tn), lambda i, j: (0, j)),
            pl.BlockSpec((tm, tn), lambda i, j: (i, ga0 + j)),
            pl.BlockSpec((tm, tn), lambda i, j: (i, gb0 + j)),
        ],
        out_specs=pl.BlockSpec((tm, tn), lambda i, j: (i, j)),
        compiler_params=_params(("parallel", "arbitrary")),
        name=name,
    )(ya, yb, wa, wb, proj, proj)


def _resid_kernel(a_ref, w_ref, x_ref, o_ref):
    o_ref[...] = x_ref[...] + _dot_bf16(a_ref[...], w_ref[...])


def _matmul_residual(a, w, x, tm, tn, name):
    m, k = a.shape
    n = w.shape[1]
    return pl.pallas_call(
        _resid_kernel,
        out_shape=jax.ShapeDtypeStruct((m, n), F32),
        grid=(m // tm, n // tn),
        in_specs=[
            pl.BlockSpec((tm, k), lambda i, j: (i, 0)),
            pl.BlockSpec((k, tn), lambda i, j: (0, j)),
            pl.BlockSpec((tm, tn), lambda i, j: (i, j)),
        ],
        out_specs=pl.BlockSpec((tm, tn), lambda i, j: (i, j)),
        compiler_params=_params(("parallel", "arbitrary")),
        name=name,
    )(a, w, x)


def _gelu(x):
    return 0.5 * x * (1.0 + lax.erf(x * (1.0 / math.sqrt(2.0))))


def _ffn_up_kernel(h_ref, w1_ref, w2_ref, cw1_ref, cw2_ref, cb1_ref, cb2_ref, act_ref, t1_ref, t2_ref,
                   c1_scr, c2_scr):
    tm = h_ref.shape[0]

    @pl.when(pl.program_id(1) == 0)
    def _():
        c1_scr[...] = jnp.zeros_like(c1_scr)
        c2_scr[...] = jnp.zeros_like(c2_scr)

    a = h_ref[...]
    row = lax.broadcasted_iota(jnp.int32, (tm, 1), 0)

    def conv(u, carry_ref, cw_ref, cb_ref):
        m1 = jnp.where(row == 0, carry_ref[7:8, :], pltpu.roll(u, 1, axis=0))
        m2 = jnp.where(row == 0, carry_ref[6:7, :], jnp.where(row == 1, carry_ref[7:8, :], pltpu.roll(u, 2, axis=0)))
        return cb_ref[...] + cw_ref[0:1, :] * m2 + cw_ref[1:2, :] * m1 + cw_ref[2:3, :] * u

    u1 = _dot_bf16(a, w1_ref[...])
    u2 = _dot_bf16(a, w2_ref[...])
    c1 = conv(u1, c1_scr, cw1_ref, cb1_ref)
    c2 = conv(u2, c2_scr, cw2_ref, cb2_ref)
    act_ref[...] = (_gelu(c1) * c2).astype(act_ref.dtype)
    c1_scr[...] = u1[tm - 8:tm]
    c2_scr[...] = u2[tm - 8:tm]
    t1_ref[...] = u1[tm - 8:tm]
    t2_ref[...] = u2[tm - 8:tm]


def _ffn_up_prompt(hn, w_up, conv_w, conv_b, tm, tn):
    m = hn.shape[0]
    nj = D_FF // tn
    tail = jax.ShapeDtypeStruct((8, D_FF), F32)
    return pl.pallas_call(
        _ffn_up_kernel,
        out_shape=[jax.ShapeDtypeStruct((m, D_FF), BF16), tail, tail],
        grid=(nj, m // tm),
        in_specs=[
            pl.BlockSpec((tm, D_MODEL), lambda j, i: (i, 0)),
            pl.BlockSpec((D_MODEL, tn), lambda j, i: (0, j)),
            pl.BlockSpec((D_MODEL, tn), lambda j, i: (0, nj + j)),
            pl.BlockSpec((CONV_W, tn), lambda j, i: (0, j)),
            pl.BlockSpec((CONV_W, tn), lambda j, i: (0, nj + j)),
            pl.BlockSpec((1, tn), lambda j, i: (0, j)),
            pl.BlockSpec((1, tn), lambda j, i: (0, nj + j)),
        ],
        out_specs=[pl.BlockSpec((tm, tn), lambda j, i: (i, j)),
                   pl.BlockSpec((8, tn), lambda j, i: (0, j)),
                   pl.BlockSpec((8, tn), lambda j, i: (0, j))],
        scratch_shapes=[pltpu.VMEM((8, tn), F32), pltpu.VMEM((8, tn), F32)],
        compiler_params=_params(("arbitrary", "arbitrary")),
        name="ffn_up_prompt",
    )(hn, w_up, w_up, conv_w, conv_w, conv_b, conv_b)


def _ffn_act_step_kernel(up_ref, prev_ref, cw_ref, cb_ref, act_ref):
    up = up_ref[...]
    w = 2 * D_FF
    c = cb_ref[...] + cw_ref[0:1, :] * prev_ref[:, 0:w] + cw_ref[1:2, :] * prev_ref[:, w:2 * w] + cw_ref[2:3, :] * up
    act_ref[...] = (_gelu(c[:, 0:D_FF]) * c[:, D_FF:w]).astype(act_ref.dtype)


def _ffn_act_step(up, conv_prev, conv_w, conv_b):
    b = up.shape[0]
    return pl.pallas_call(
        _ffn_act_step_kernel,
        out_shape=jax.ShapeDtypeStruct((b, D_FF), BF16),
        compiler_params=pltpu.CompilerParams(vmem_limit_bytes=VMEM_LIMIT),
        name="ffn_act_sample",
    )(up, conv_prev.reshape(b, (CONV_W - 1) * 2 * D_FF), conv_w, conv_b)


def _down_kernel(a_ref, w_ref, h_ref, g_ref, o_ref, acc_ref, *, nk):
    kstep = pl.program_id(1)

    @pl.when(kstep == 0)
    def _():
        acc_ref[...] = jnp.zeros_like(acc_ref)

    acc_ref[...] += _dot_bf16(a_ref[...], w_ref[...])

    @pl.when(kstep == nk - 1)
    def _():
        out = h_ref[...] + acc_ref[...]
        inv = lax.rsqrt(jnp.mean(out * out, axis=-1, keepdims=True) + EPS_RMS)
        o_ref[...] = out * inv * g_ref[...]


def _ffn_down(act, w_down, h, g, tm, tk, name):
    m = act.shape[0]
    nk = D_FF // tk
    return pl.pallas_call(
        functools.partial(_down_kernel, nk=nk),
        out_shape=jax.ShapeDtypeStruct((m, D_MODEL), F32),
        grid=(m // tm, nk),
        in_specs=[
            pl.BlockSpec((tm, tk), lambda i, k: (i, k)),
            pl.BlockSpec((tk, D_MODEL), lambda i, k: (k, 0)),
            pl.BlockSpec((tm, D_MODEL), lambda i, k: (i, 0)),
            pl.BlockSpec((1, D_MODEL), lambda i, k: (0, 0)),
        ],
        out_specs=pl.BlockSpec((tm, D_MODEL), lambda i, k: (i, 0)),
        scratch_shapes=[pltpu.VMEM((tm, D_MODEL), F32)],
        compiler_params=_params(("parallel", "arbitrary")),
        name=name,
    )(act, w_down, h, g.reshape(1, D_MODEL))


def _rel_bucket(dist):
    max_exact = N_BUCKETS // 2
    d_f = jnp.maximum(dist, 1).astype(F32)
    large = max_exact + (jnp.log(d_f / max_exact) / math.log(MAX_DISTANCE / max_exact)
                         * (N_BUCKETS - max_exact)).astype(jnp.int32)
    large = jnp.minimum(large, N_BUCKETS - 1)
    return jnp.where(dist < max_exact, dist, large)


def _bias_tables(rel_bias, gi):
    dil = DILATIONS[gi]
    reach = WINDOWS[gi] // dil
    tab = rel_bias[:, gi * H_G:(gi + 1) * H_G]
    qi = jnp.arange(BLK)[:, None]
    ki = jnp.arange(2 * BLK)[None, :]
    rel = qi + BLK - ki
    blk = tab[_rel_bucket(dil * jnp.maximum(rel, 0))].transpose(2, 0, 1)
    blk = jnp.where(((rel >= 0) & (rel <= reach))[None], blk, -jnp.inf)
    steps = jnp.arange(reach, -1, -1)
    step = tab[_rel_bucket(dil * steps)][:, :, None]
    return blk, step


def _layer(x, tiles, lw, bias_blk, prompt, state=None):
    m = x.shape[0]
    xn = _rmsnorm(x, lw["norm1_g"], tiles["rms"], "rms1_" + tiles["tag"])
    proj = _matmul(xn, lw["w_in"], lw["in_bias"], tiles["proj_m"], 256, COL_GATE, "proj_" + tiles["tag"])

    if prompt:
        rt, kt, qt, pt, v, bonus, g, gc = _rwkv_prep(proj, None, 128, CHUNK, lw, "rwkv_prep_prompt")
        o, wkv_new = _wkv_scan(pt, rt, qt, kt, v, gc)
        wkv_new = wkv_new[None, None]
    else:
        rt, kt, qt, pt, v, bonus, g, gc = _rwkv_prep(proj, state["shift"], m, 1, lw, "rwkv_prep_sample")
        wkv_new, o = _wkv_step(state["wkv"], pt, rt, qt, kt, v, gc)
    ya = _rwkv_post(o, bonus, g, lw, tiles["post"], "rwkv_post_" + tiles["tag"])

    outs, lses = [], []
    for gi in range(N_GROUPS):
        if prompt:
            o_g, l_g = _attn_prompt(proj, bias_blk[gi][0], gi, 2048)
        else:
            sl = lambda c0: proj[:, c0 + gi * D_G:c0 + (gi + 1) * D_G].reshape(m, H_G, HEAD_B)
            o_g, l_g = _attn_step(sl(COL_QKV), sl(COL_QKV + D_B), sl(COL_QKV + 2 * D_B), state["win"][gi],
                                  bias_blk[gi][1], gi)
        outs.append(o_g)
        lses.append(l_g)
    yb = _merge(outs, lses, tiles["merge"], "merge_" + tiles["tag"])

    mixed = _mix(ya, yb, lw["w_out_a"], lw["w_out_b"], proj, tiles["mix_m"], 256, "mix_" + tiles["tag"])
    h = _matmul_residual(mixed, lw["w_o"], x, tiles["wo_m"], 512, "wo_" + tiles["tag"])

    hn = _rmsnorm(h, lw["norm2_g"], tiles["rms"], "rms2_" + tiles["tag"])
    if prompt:
        act, t1, t2 = _ffn_up_prompt(hn, lw["w_up"], lw["conv_w"], lw["conv_b"], 1024, 512)
        conv_new = jnp.concatenate([t1[6:8], t2[6:8]], axis=1)[None, None]
    else:
        up = _matmul(hn, lw["w_up"], lw["up_bias"], m, 512, 2 * D_FF, "up_sample")
        act = _ffn_act_step(up, state["conv"], lw["conv_w"], lw["conv_b"])
        conv_new = jnp.concatenate([state["conv"][:, 1:], up[:, None, :]], axis=1)[None]
    y = _ffn_down(act, lw["w_down"], h, lw["normf_g"], tiles["down_m"], 1408, "down_" + tiles["tag"])
    return y, proj, wkv_new, conv_new


def kernel(x_prompt, x_sample, state_wkv, state_shift, state_ffn_conv, cache_win1, cache_win2, cache_win3, rel_bias,
           norm1_g, w_in, gate_b, mu_shift, w0, w_up_decay, a0, w_up_aaa, w_up_gate, k_k, k_a, r_k, gn_g, gn_b,
           w_out_a, w_out_b, w_o, norm2_g, w_up, conv_w, conv_b, w_down, normf_g):
    row = lambda a: a.reshape(1, -1)
    w_lora = jnp.zeros((D_LORA, 3 * D_A), F32)
    w_lora = w_lora.at[0:D_DECAY_LORA, 0:D_A].set(w_up_decay[0])
    w_lora = w_lora.at[D_DECAY_LORA:D_DECAY_LORA + D_AAA_LORA, D_A:2 * D_A].set(w_up_aaa[0])
    w_lora = w_lora.at[D_DECAY_LORA + D_AAA_LORA:, 2 * D_A:].set(w_up_gate[0])
    head = np.arange(D_A) // HEAD_A
    lw = dict(
        norm1_g=norm1_g[0], norm2_g=norm2_g[0], normf_g=normf_g,
        w_in=w_in[0].astype(BF16),
        in_bias=jnp.concatenate([jnp.zeros((1, COL_GATE), F32), row(gate_b[0])], axis=1),
        up_bias=jnp.zeros((1, 2 * D_FF), F32),
        mu=row(mu_shift[0]), w_lora=w_lora, w0=row(w0[0]), a0=row(a0[0]), k_k=row(k_k[0]), k_a=row(k_a[0]),
        r_k=row(r_k[0]), gn_g=row(gn_g[0]), gn_b=row(gn_b[0]),
        bd=jnp.asarray(head[:, None] == head[None, :], BF16),
        w_out_a=w_out_a[0].astype(BF16), w_out_b=w_out_b[0].astype(BF16), w_o=w_o[0].astype(BF16),
        w_up=w_up[0].astype(BF16), conv_w=conv_w[0], conv_b=row(conv_b[0]), w_down=w_down[0].astype(BF16),
    )
    bias = [_bias_tables(rel_bias, gi) for gi in range(N_GROUPS)]

    tiles_p = dict(tag="prompt", rms=512, proj_m=2048, post=256, merge=1024, mix_m=1024, wo_m=1024, down_m=512)
    y_p, proj_p, wkv_p, conv_p = _layer(x_prompt[0], tiles_p, lw, bias, prompt=True)

    b = DEC_BATCH
    tiles_s = dict(tag="sample", rms=b, proj_m=b, post=b, merge=b, mix_m=b, wo_m=b, down_m=b)
    state = dict(wkv=state_wkv, shift=state_shift.reshape(b, D_SHIFT), conv=state_ffn_conv[0],
                 win=(cache_win1, cache_win2, cache_win3))
    y_s, proj_s, wkv_s, conv_s = _layer(x_sample[:, 0], tiles_s, lw, bias, prompt=False, state=state)

    def kv_rows(proj, lo, gi):
        k = proj[lo:, COL_QKV + D_B + gi * D_G:COL_QKV + D_B + (gi + 1) * D_G]
        v = proj[lo:, COL_QKV + 2 * D_B + gi * D_G:COL_QKV + 2 * D_B + (gi + 1) * D_G]
        n = k.shape[0]
        return jnp.stack([k.reshape(n, H_G, HEAD_B), v.reshape(n, H_G, HEAD_B)], axis=1)

    win_p = [kv_rows(proj_p, SEQ - min(WINDOWS[gi], SEQ), gi)[None, None] for gi in range(N_GROUPS)]
    win_s = [kv_rows(proj_s, 0, gi)[None, :, None] for gi in range(N_GROUPS)]
    return (y_p[None], y_s[:, None],
            wkv_p, wkv_s,
            proj_p[SEQ - 1:, 0:D_SHIFT][None, None], proj_s[:, 0:D_SHIFT][None, :, None],
            conv_p, conv_s,
            win_p[0], win_s[0], win_p[1], win_s[1], win_p[2], win_s[2])
```

```python
import functools
import math

import numpy as np
import jax
import jax.numpy as jnp
from jax import lax
from jax.experimental import pallas as pl
from jax.experimental.pallas import tpu as pltpu

F32 = jnp.float32
BF16 = jnp.bfloat16
HIGHEST = lax.Precision.HIGHEST

D_MODEL = 2048
SEQ = 8192
DEC_BATCH = 32
HEAD_A = 64
H_A = 16
D_A = H_A * HEAD_A
D_DECAY_LORA = 96
D_AAA_LORA = 96
D_GATE_LORA = 64
D_LORA = D_DECAY_LORA + D_AAA_LORA + D_GATE_LORA
D_SHIFT = 3 * D_A + D_LORA
EPS_GN = 64e-5
HEAD_B = 64
H_G = 8
D_G = H_G * HEAD_B
WINDOWS = (128, 512, 2048)
DILATIONS = (1, 4, 16)
N_GROUPS = 3
D_B = N_GROUPS * D_G
BLK = 128
N_BUCKETS = 32
MAX_DISTANCE = 2048
COL_QKV = D_SHIFT
COL_GATE = D_SHIFT + 3 * D_B
D_IN = COL_GATE + 2 * D_MODEL
D_FF = 5632
CONV_W = 3
EPS_RMS = 1e-6
CHUNK = 64
LANES = 128
VMEM_LIMIT = 56 * 1024 * 1024


def _params(sem, vmem=VMEM_LIMIT):
    return pltpu.CompilerParams(dimension_semantics=sem, vmem_limit_bytes=vmem)


def _sigmoid(x):
    return 1.0 / (1.0 + jnp.exp(-x))


def _dot_bf16(a, b):
    return jnp.dot(a, b, preferred_element_type=F32)


def _split2(x):
    hi = x.astype(BF16)
    return hi, (x - hi.astype(F32)).astype(BF16)


def _dot3(a, b):
    ah, al = _split2(a)
    bh, bl = _split2(b)
    return _dot_bf16(ah, bh) + _dot_bf16(ah, bl) + _dot_bf16(al, bh)


def _split3(x):
    hi = x.astype(BF16)
    r1 = x - hi.astype(F32)
    mid = r1.astype(BF16)
    lo = (r1 - mid.astype(F32)).astype(BF16)
    return hi, mid, lo


def _dot_exact_rhs(x, m):
    hi, mid, lo = _split3(x)
    return _dot_bf16(hi, m) + _dot_bf16(mid, m) + _dot_bf16(lo, m)


def _dot_exact_lhs(m, x):
    hi, mid, lo = _split3(x)
    return _dot_bf16(m, hi) + _dot_bf16(m, mid) + _dot_bf16(m, lo)


def _rms_kernel(x_ref, g_ref, o_ref):
    x = x_ref[...]
    inv = lax.rsqrt(jnp.mean(x * x, axis=-1, keepdims=True) + EPS_RMS)
    o_ref[...] = (x * inv * g_ref[...]).astype(o_ref.dtype)


def _rmsnorm(x, g, tm, name):
    m, d = x.shape
    return pl.pallas_call(
        _rms_kernel,
        out_shape=jax.ShapeDtypeStruct((m, d), BF16),
        grid=(m // tm,),
        in_specs=[pl.BlockSpec((tm, d), lambda i: (i, 0)), pl.BlockSpec((1, d), lambda i: (0, 0))],
        out_specs=pl.BlockSpec((tm, d), lambda i: (i, 0)),
        compiler_params=_params(("parallel",)),
        name=name,
    )(x, g.reshape(1, d))


def _proj_kernel(a_ref, w_ref, b_ref, o_ref, *, gate_block):
    acc = _dot_bf16(a_ref[...], w_ref[...])
    j = pl.program_id(1)

    @pl.when(j < gate_block)
    def _():
        o_ref[...] = acc

    @pl.when(j >= gate_block)
    def _():
        o_ref[...] = _sigmoid(acc + b_ref[...])


def _matmul(a, w, bias, tm, tn, gate_col, name):
    m, k = a.shape
    n = w.shape[1]
    return pl.pallas_call(
        functools.partial(_proj_kernel, gate_block=gate_col // tn),
        out_shape=jax.ShapeDtypeStruct((m, n), F32),
        grid=(m // tm, n // tn),
        in_specs=[
            pl.BlockSpec((tm, k), lambda i, j: (i, 0)),
            pl.BlockSpec((k, tn), lambda i, j: (0, j)),
            pl.BlockSpec((1, tn), lambda i, j: (0, j)),
        ],
        out_specs=pl.BlockSpec((tm, tn), lambda i, j: (i, j)),
        compiler_params=_params(("parallel", "arbitrary")),
        name=name,
    )(a, w, bias)


def _prep_kernel(p_ref, prev_ref, mu_ref, wl_ref, w0_ref, a0_ref, kk_ref, ka_ref, rk_ref, lmat_ref, sel_ref,
                 bd_ref, rt_ref, kt_ref, qt_ref, pt_ref, v_ref, bonus_ref, g_ref, gc_ref, *, rows_are_time):
    p = p_ref[...]
    tm = p.shape[0]
    if rows_are_time:
        last = jnp.where(pl.program_id(0) == 0, 0.0, prev_ref[7:8, :])
        row = lax.broadcasted_iota(jnp.int32, (tm, 1), 0)
        prev = jnp.where(row == 0, last, pltpu.roll(p, 1, axis=0))
    else:
        prev = prev_ref[...]
    xm = p + mu_ref[...] * (prev - p)
    r = xm[:, 0:D_A]
    k = xm[:, D_A:2 * D_A]
    v = xm[:, 2 * D_A:3 * D_A]
    xl = xm[:, 3 * D_A:D_SHIFT]
    lane = lax.broadcasted_iota(jnp.int32, xl.shape, 1)
    act = jnp.where(lane < D_DECAY_LORA, jnp.tanh(xl),
                    jnp.where(lane < D_DECAY_LORA + D_AAA_LORA, xl, _sigmoid(xl)))
    lora = _dot3(act, wl_ref[...])
    y = -(w0_ref[...] + lora[:, 0:D_A])
    softplus = jnp.maximum(y, 0.0) + jnp.log(1.0 + jnp.exp(-jnp.abs(y)))
    logw = -jnp.exp(-softplus - 0.5)
    a = _sigmoid(a0_ref[...] + lora[:, D_A:2 * D_A])
    g = lora[:, 2 * D_A:3 * D_A]
    kkr = k * kk_ref[...]
    kp = k * (1.0 + (a - 1.0) * ka_ref[...])
    seg = _dot_exact_rhs(jnp.concatenate([kkr * kkr, r * kp * rk_ref[...]], axis=0), bd_ref[...])
    kk = kkr / jnp.maximum(jnp.sqrt(seg[0:tm]), 1e-12)
    cum = _dot_exact_lhs(lmat_ref[...], logw)
    e_out = jnp.exp(-cum)
    rt_ref[...] = r * jnp.exp(cum)
    kt_ref[...] = kp * e_out
    qt_ref[...] = kk * a * e_out
    pt_ref[...] = -kk * jnp.exp(cum - logw)
    v_ref[...] = v
    bonus_ref[...] = seg[tm:2 * tm] * v
    g_ref[...] = g
    gc_ref[...] = jnp.exp(_dot_exact_lhs(sel_ref[...], cum))


def _rwkv_prep(proj, prev, tm, chunk, lw, name):
    m = proj.shape[0]
    rows_are_time = prev is None
    n_tiles = m // tm
    t = np.arange(tm)
    lmat = ((t[:, None] // chunk == t[None, :] // chunk) & (t[None, :] <= t[:, None])).astype(np.float32)
    sel_stride = 1 if chunk == 1 else 8
    n_sel = sel_stride * (tm // chunk)
    sel = np.zeros((n_sel, tm), np.float32)
    for c in range(tm // chunk):
        sel[sel_stride * c, (c + 1) * chunk - 1] = 1.0
    if rows_are_time:
        prev_arr = proj
        prev_spec = pl.BlockSpec((8, D_SHIFT), lambda i: (jnp.maximum(i * (tm // 8) - 1, 0), 0))
    else:
        prev_arr = prev
        prev_spec = pl.BlockSpec((tm, D_SHIFT), lambda i: (i, 0))
    vec = lambda d: pl.BlockSpec((1, d), lambda i: (0, 0))
    full = lambda a: pl.BlockSpec(a.shape, lambda i: (0, 0))
    big = pl.BlockSpec((tm, D_A), lambda i: (i, 0))
    out = jax.ShapeDtypeStruct((m, D_A), F32)
    lmat_b = jnp.asarray(lmat, BF16)
    sel_b = jnp.asarray(sel, BF16)
    return pl.pallas_call(
        functools.partial(_prep_kernel, rows_are_time=rows_are_time),
        out_shape=[out] * 7 + [jax.ShapeDtypeStruct((n_tiles * n_sel, D_A), F32)],
        grid=(n_tiles,),
        in_specs=[pl.BlockSpec((tm, D_SHIFT), lambda i: (i, 0)), prev_spec, vec(D_SHIFT), full(lw["w_lora"]),
                  vec(D_A), vec(D_A), vec(D_A), vec(D_A), vec(D_A), full(lmat_b), full(sel_b), full(lw["bd"])],
        out_specs=[big] * 7 + [pl.BlockSpec((n_sel, D_A), lambda i: (i, 0))],
        compiler_params=_params(("parallel",)),
        name=name,
    )(proj, prev_arr, lw["mu"], lw["w_lora"], lw["w0"], lw["a0"], lw["k_k"], lw["k_a"], lw["r_k"], lmat_b, sel_b,
      lw["bd"])


def _bdot3(a, b, dims):
    ah, al = _split2(a)
    bh, bl = _split2(b)
    dg = lambda x, y: lax.dot_general(x, y, dims, preferred_element_type=F32)
    return dg(ah, bh) + dg(ah, bl) + dg(al, bh)


_NN = (((2,), (1,)), ((0,), (0,)))
_NT = (((2,), (2,)), ((0,), (0,)))


def _scan_kernel(pt_ref, rt_ref, qt_ref, kt_ref, v_ref, gc_ref, o_ref, sfin_ref, s_scr):
    @pl.when(pl.program_id(0) == 0)
    def _():
        s_scr[...] = jnp.zeros_like(s_scr)

    def heads(x):
        return jnp.stack([x[:, h * HEAD_A:(h + 1) * HEAD_A] for h in range(H_A)], axis=0)

    p, r, q, k, v = (heads(ref[...]) for ref in (pt_ref, rt_ref, qt_ref, kt_ref, v_ref))
    gam = heads(gc_ref[0:1, :])
    ri = lax.broadcasted_iota(jnp.int32, (1, CHUNK, CHUNK), 1)
    ci = lax.broadcasted_iota(jnp.int32, (1, CHUNK, CHUNK), 2)
    strict = ri > ci
    incl = ri >= ci
    eye = (ri == ci).astype(F32)

    gram = _bdot3(jnp.concatenate([p, r], axis=1), jnp.concatenate([q, k], axis=1), _NT)
    a_qp = jnp.where(strict, gram[:, 0:CHUNK, 0:CHUNK], 0.0)
    a_kp = jnp.where(strict, gram[:, 0:CHUNK, CHUNK:], 0.0)
    a_rq = jnp.where(incl, gram[:, CHUNK:, 0:CHUNK], 0.0)
    a_rk = jnp.where(incl, gram[:, CHUNK:, CHUNK:], 0.0)
    same = lambda log2_bs: (ri >> log2_bs) == (ci >> log2_bs)
    pw = jnp.where(same(3), a_qp, 0.0)
    tinv = eye + pw
    for _ in range(2):
        pw = _bdot3(pw, pw, _NN)
        tinv = tinv + _bdot3(tinv, pw, _NN)
    for log2_bs in (3, 4, 5):
        off = jnp.where(jnp.logical_and(same(log2_bs + 1), jnp.logical_not(same(log2_bs))), a_qp, 0.0)
        tinv = tinv + _bdot3(tinv, _bdot3(off, tinv, _NN), _NN)
    pw_hat = _bdot3(tinv, jnp.concatenate([p, _bdot3(a_kp, v, _NN)], axis=2), _NN)
    x = _bdot3(a_rq, pw_hat, _NN)
    r_hat = r + x[:, :, 0:HEAD_A]
    o_loc = x[:, :, HEAD_A:] + _bdot3(a_rk, v, _NN)
    y = _bdot3(jnp.swapaxes(pw_hat, 1, 2), q, _NN)
    m_mat = (eye + y[:, 0:HEAD_A]) * gam
    n_mat = (y[:, HEAD_A:] + _bdot3(jnp.swapaxes(v, 1, 2), k, _NN)) * gam

    s0 = s_scr[...]
    o = _bdot3(r_hat, s0, _NT) + o_loc
    s_scr[...] = _bdot3(s0, m_mat, _NN) + n_mat
    o_ref[...] = jnp.concatenate([o[h] for h in range(H_A)], axis=1)

    @pl.when(pl.program_id(0) == pl.num_programs(0) - 1)
    def _():
        sfin_ref[...] = s_scr[...]


def _wkv_scan(pt, rt, qt, kt, v, gc):
    t = pt.shape[0]
    blk = pl.BlockSpec((CHUNK, D_A), lambda c: (c, 0))
    return pl.pallas_call(
        _scan_kernel,
        out_shape=[jax.ShapeDtypeStruct((t, D_A), F32), jax.ShapeDtypeStruct((H_A, HEAD_A, HEAD_A), F32)],
        grid=(t // CHUNK,),
        in_specs=[blk] * 5 + [pl.BlockSpec((8, D_A), lambda c: (c, 0))],
        out_specs=[blk, pl.BlockSpec((H_A, HEAD_A, HEAD_A), lambda c: (0, 0, 0))],
        scratch_shapes=[pltpu.VMEM((H_A, HEAD_A, HEAD_A), F32)],
        compiler_params=_params(("arbitrary",)),
        name="wkv_scan_prompt",
    )(pt, rt, qt, kt, v, gc)


def _wkv_step_kernel(s_ref, pt_ref, rt_ref, qt_ref, kt_ref, v_ref, gc_ref, sn_ref, o_ref):
    ri = lax.broadcasted_iota(jnp.int32, (HEAD_A, HEAD_A), 0)
    ci = lax.broadcasted_iota(jnp.int32, (HEAD_A, HEAD_A), 1)
    eye = ri == ci
    pt, rt, qt, kt, v, gc = (ref[0] for ref in (pt_ref, rt_ref, qt_ref, kt_ref, v_ref, gc_ref))
    outs = []
    for h in range(H_A):
        ls = slice(h * HEAD_A, (h + 1) * HEAD_A)
        s = s_ref[0, 0, h]
        u = jnp.sum(s * pt[:, ls], axis=1, keepdims=True)
        v_col = jnp.sum(jnp.where(eye, v[:, ls], 0.0), axis=1, keepdims=True)
        m = s + u * qt[:, ls] + v_col * kt[:, ls]
        sn_ref[0, 0, h] = m * gc[:, ls]
        o_col = jnp.sum(m * rt[:, ls], axis=1, keepdims=True)
        outs.append(jnp.sum(jnp.where(eye, o_col, 0.0), axis=0, keepdims=True))
    o_ref[0] = jnp.concatenate(outs, axis=1)


def _wkv_step(state, pt, rt, qt, kt, v, gc):
    b = pt.shape[0]
    rows = [x.reshape(b, 1, D_A) for x in (pt, rt, qt, kt, v, gc)]
    row_spec = pl.BlockSpec((1, 1, D_A), lambda i: (i, 0, 0))
    st_spec = pl.BlockSpec((1, 1, H_A, HEAD_A, HEAD_A), lambda i: (0, i, 0, 0, 0))
    sn, o = pl.pallas_call(
        _wkv_step_kernel,
        out_shape=[jax.ShapeDtypeStruct(state.shape, F32), jax.ShapeDtypeStruct((b, 1, D_A), F32)],
        grid=(b,),
        in_specs=[st_spec] + [row_spec] * 6,
        out_specs=[st_spec, row_spec],
        compiler_params=_params(("parallel",)),
        name="wkv_step_sample",
    )(state, *rows)
    return sn, o.reshape(b, D_A)


def _post_kernel(o_ref, bonus_ref, g_ref, gng_ref, gnb_ref, bd_ref, y_ref):
    o = o_ref[...]
    bd = bd_ref[...]
    mu = _dot_exact_rhs(o, bd) * (1.0 / HEAD_A)
    d = o - mu
    var = _dot_exact_rhs(d * d, bd) * (1.0 / HEAD_A)
    o_n = d * lax.rsqrt(var + EPS_GN) * gng_ref[...] + gnb_ref[...]
    y_ref[...] = ((o_n + bonus_ref[...]) * g_ref[...]).astype(y_ref.dtype)


def _rwkv_post(o, bonus, g, lw, tm, name):
    m = o.shape[0]
    big = pl.BlockSpec((tm, D_A), lambda i: (i, 0))
    vec = pl.BlockSpec((1, D_A), lambda i: (0, 0))
    return pl.pallas_call(
        _post_kernel,
        out_shape=jax.ShapeDtypeStruct((m, D_A), BF16),
        grid=(m // tm,),
        in_specs=[big, big, big, vec, vec, pl.BlockSpec((D_A, D_A), lambda i: (0, 0))],
        out_specs=big,
        compiler_params=_params(("parallel",)),
        name=name,
    )(o, bonus, g, lw["gn_g"], lw["gn_b"], lw["bd"])


def _attn_kernel(q_ref, kc_ref, vc_ref, kh_ref, vh_ref, bias_ref, o_ref, l_ref, *, dil, m_blocks):
    scale = HEAD_B ** -0.5
    col = lax.broadcasted_iota(jnp.int32, (BLK, 2 * BLK), 1)
    no_prev = jnp.logical_and(pl.program_id(0) == 0, col < BLK)

    def rows(ref, start, size):
        if dil == 1:
            return ref[pl.ds(start, size), :]
        return ref[pl.ds(start, size, stride=dil), :]

    for r in range(dil):
        for mb in range(m_blocks):
            base = r + dil * BLK * mb
            q = rows(q_ref, base, BLK)
            if mb == 0:
                k = jnp.concatenate([rows(kh_ref, r, BLK), rows(kc_ref, r, BLK)], axis=0)
                v = jnp.concatenate([rows(vh_ref, r, BLK), rows(vc_ref, r, BLK)], axis=0)
            else:
                k = rows(kc_ref, base - dil * BLK, 2 * BLK)
                v = rows(vc_ref, base - dil * BLK, 2 * BLK)
            outs, lses = [], []
            for sub in range(2):
                ls = slice(sub * HEAD_B, (sub + 1) * HEAD_B)
                s = lax.dot_general(q[:, ls].astype(BF16), k[:, ls].astype(BF16), (((1,), (1,)), ((), ())),
                                    preferred_element_type=F32) * scale + bias_ref[sub]
                if mb == 0:
                    s = jnp.where(no_prev, -jnp.inf, s)
                m = jnp.max(s, axis=-1, keepdims=True)
                p = jnp.exp(s - m)
                l = jnp.sum(p, axis=-1, keepdims=True)
                outs.append(_dot_bf16(p.astype(BF16), v[:, ls].astype(BF16)) / l)
                lses.append(jnp.broadcast_to(m + jnp.log(l), (BLK, HEAD_B)))
            o_blk = jnp.concatenate(outs, axis=1)
            l_blk = jnp.concatenate(lses, axis=1)
            if dil == 1:
                o_ref[pl.ds(base, BLK), :] = o_blk
                l_ref[pl.ds(base, BLK), :] = l_blk
            else:
                o_ref[pl.ds(base, BLK, stride=dil), :] = o_blk
                l_ref[pl.ds(base, BLK, stride=dil), :] = l_blk


def _attn_prompt(proj, bias, gi, rows_per_step):
    t = proj.shape[0]
    dil = DILATIONS[gi]
    span = BLK * dil
    m_blocks = rows_per_step // span
    cq = (COL_QKV + gi * D_G) // LANES
    ck = (COL_QKV + D_B + gi * D_G) // LANES
    cv = (COL_QKV + 2 * D_B + gi * D_G) // LANES
    cur = lambda c0: pl.BlockSpec((rows_per_step, LANES), lambda i, hp: (i, c0 + hp))
    halo = lambda c0: pl.BlockSpec((span, LANES), lambda i, hp: (jnp.maximum(i * m_blocks - 1, 0), c0 + hp))
    out_spec = pl.BlockSpec((rows_per_step, LANES), lambda i, hp: (i, hp))
    out = jax.ShapeDtypeStruct((t, D_G), F32)
    return pl.pallas_call(
        functools.partial(_attn_kernel, dil=dil, m_blocks=m_blocks),
        out_shape=[out, out],
        grid=(t // rows_per_step, H_G // 2),
        in_specs=[cur(cq), cur(ck), cur(cv), halo(ck), halo(cv),
                  pl.BlockSpec((2, BLK, 2 * BLK), lambda i, hp: (hp, 0, 0))],
        out_specs=[out_spec, out_spec],
        compiler_params=_params(("parallel", "parallel")),
        name=f"attn_prompt_g{gi}",
    )(proj, proj, proj, proj, proj, bias)


def _attn_step_kernel(q_ref, kn_ref, vn_ref, c_ref, bias_ref, bias0_ref, o_ref, l_ref):
    scale = HEAD_B ** -0.5
    ri = lax.broadcasted_iota(jnp.int32, (1, HEAD_B, HEAD_B), 1)
    ci = lax.broadcasted_iota(jnp.int32, (1, HEAD_B, HEAD_B), 2)
    eye = ri == ci
    q = q_ref[0]
    q_col = jnp.sum(jnp.where(eye, q, 0.0), axis=2, keepdims=True)
    s = jnp.sum(c_ref[0] * q_col, axis=1, keepdims=True) * scale + bias_ref[...]
    s_new = jnp.sum(q * kn_ref[0], axis=2, keepdims=True) * scale + bias0_ref[...]
    m = jnp.maximum(jnp.max(s, axis=2, keepdims=True), s_new)
    p = jnp.exp(s - m)
    p_new = jnp.exp(s_new - m)
    l = jnp.sum(p, axis=2, keepdims=True) + p_new
    o_col = jnp.sum(c_ref[1] * p, axis=2, keepdims=True)
    o_row = jnp.sum(jnp.where(eye, o_col, 0.0), axis=1, keepdims=True)
    o_ref[0] = (o_row + p_new * vn_ref[0]) / l
    l_ref[0] = jnp.broadcast_to(m + jnp.log(l), (H_G, 1, HEAD_B))


def _attn_step(q, k_new, v_new, cache, bias, bias0, gi):
    b = q.shape[0]
    w = cache.shape[2]
    cache_t = jnp.transpose(cache, (0, 1, 3, 4, 5, 2))
    vec = pl.BlockSpec((1, H_G, 1, HEAD_B), lambda i: (i, 0, 0, 0))
    out = jax.ShapeDtypeStruct((b, H_G, 1, HEAD_B), F32)
    o, l = pl.pallas_call(
        _attn_step_kernel,
        out_shape=[out, out],
        grid=(b,),
        in_specs=[vec, vec, vec,
                  pl.BlockSpec((None, None, 2, H_G, HEAD_B, w), lambda i: (0, i, 0, 0, 0, 0)),
                  pl.BlockSpec((H_G, 1, w), lambda i: (0, 0, 0)),
                  pl.BlockSpec((H_G, 1, 1), lambda i: (0, 0, 0))],
        out_specs=[vec, vec],
        compiler_params=_params(("parallel",)),
        name=f"attn_step_g{gi}",
    )(q, k_new, v_new, cache_t, bias, bias0)
    return o.reshape(b, D_G), l.reshape(b, D_G)


def _merge_kernel(o1_ref, o2_ref, o3_ref, l1_ref, l2_ref, l3_ref, y_ref):
    l1, l2, l3 = l1_ref[...], l2_ref[...], l3_ref[...]
    m = jnp.maximum(jnp.maximum(l1, l2), l3)
    e1, e2, e3 = jnp.exp(l1 - m), jnp.exp(l2 - m), jnp.exp(l3 - m)
    den = e1 + e2 + e3
    y = (e1 / den) * o1_ref[...] + (e2 / den) * o2_ref[...] + (e3 / den) * o3_ref[...]
    y_ref[...] = y.astype(y_ref.dtype)


def _merge(outs, lses, tm, name):
    m = outs[0].shape[0]
    blk = pl.BlockSpec((tm, D_G), lambda i: (i, 0))
    return pl.pallas_call(
        _merge_kernel,
        out_shape=jax.ShapeDtypeStruct((m, D_G), BF16),
        grid=(m // tm,),
        in_specs=[blk] * 6,
        out_specs=blk,
        compiler_params=_params(("parallel",)),
        name=name,
    )(*outs, *lses)


def _mix_kernel(ya_ref, yb_ref, wa_ref, wb_ref, ga_ref, gb_ref, o_ref):
    mixed = ga_ref[...] * _dot_bf16(ya_ref[...], wa_ref[...]) + gb_ref[...] * _dot_bf16(yb_ref[...], wb_ref[...])
    o_ref[...] = mixed.astype(o_ref.dtype)


def _mix(ya, yb, wa, wb, proj, tm, tn, name):
    m = ya.shape[0]
    ga0 = COL_GATE // tn
    gb0 = (COL_GATE + D_MODEL) // tn
    return pl.pallas_call(
        _mix_kernel,
        out_shape=jax.ShapeDtypeStruct((m, D_MODEL), BF16),
        grid=(m // tm, D_MODEL // tn),
        in_specs=[
            pl.BlockSpec((tm, D_A), lambda i, j: (i, 0)),
            pl.BlockSpec((tm, D_G), lambda i, j: (i, 0)),
            pl.BlockSpec((D_A, tn), lambda i, j: (0, j)),
            pl.BlockSpec((D_G, tn), lambda i, j: (0, j)),
            pl.BlockSpec((tm, tn), lambda i, j: (i, ga0 + j)),
            pl.BlockSpec((tm, tn), lambda i, j: (i, gb0 + j)),
        ],
        out_specs=pl.BlockSpec((tm, tn), lambda i, j: (i, j)),
        compiler_params=_params(("parallel", "arbitrary")),
        name=name,
    )(ya, yb, wa, wb, proj, proj)


def _resid_kernel(a_ref, w_ref, x_ref, o_ref):
    o_ref[...] = x_ref[...] + _dot_bf16(a_ref[...], w_ref[...])


def _matmul_residual(a, w, x, tm, tn, name):
    m, k = a.shape
    n = w.shape[1]
    return pl.pallas_call(
        _resid_kernel,
        out_shape=jax.ShapeDtypeStruct((m, n), F32),
        grid=(m // tm, n // tn),
        in_specs=[
            pl.BlockSpec((tm, k), lambda i, j: (i, 0)),
            pl.BlockSpec((k, tn), lambda i, j: (0, j)),
            pl.BlockSpec((tm, tn), lambda i, j: (i, j)),
        ],
        out_specs=pl.BlockSpec((tm, tn), lambda i, j: (i, j)),
        compiler_params=_params(("parallel", "arbitrary")),
        name=name,
    )(a, w, x)


def _gelu(x):
    return 0.5 * x * (1.0 + lax.erf(x * (1.0 / math.sqrt(2.0))))


def _ffn_up_kernel(h_ref, w1_ref, w2_ref, cw1_ref, cw2_ref, cb1_ref, cb2_ref, act_ref, t1_ref, t2_ref,
                   c1_scr, c2_scr):
    tm = h_ref.shape[0]

    @pl.when(pl.program_id(1) == 0)
    def _():
        c1_scr[...] = jnp.zeros_like(c1_scr)
        c2_scr[...] = jnp.zeros_like(c2_scr)

    a = h_ref[...]
    row = lax.broadcasted_iota(jnp.int32, (tm, 1), 0)

    def conv(u, carry_ref, cw_ref, cb_ref):
        m1 = jnp.where(row == 0, carry_ref[7:8, :], pltpu.roll(u, 1, axis=0))
        m2 = jnp.where(row == 0, carry_ref[6:7, :], jnp.where(row == 1, carry_ref[7:8, :], pltpu.roll(u, 2, axis=0)))
        return cb_ref[...] + cw_ref[0:1, :] * m2 + cw_ref[1:2, :] * m1 + cw_ref[2:3, :] * u

    u1 = _dot_bf16(a, w1_ref[...])
    u2 = _dot_bf16(a, w2_ref[...])
    c1 = conv(u1, c1_scr, cw1_ref, cb1_ref)
    c2 = conv(u2, c2_scr, cw2_ref, cb2_ref)
    act_ref[...] = (_gelu(c1) * c2).astype(act_ref.dtype)
    c1_scr[...] = u1[tm - 8:tm]
    c2_scr[...] = u2[tm - 8:tm]
    t1_ref[...] = u1[tm - 8:tm]
    t2_ref[...] = u2[tm - 8:tm]


def _ffn_up_prompt(hn, w_up, conv_w, conv_b, tm, tn):
    m = hn.shape[0]
    nj = D_FF // tn
    tail = jax.ShapeDtypeStruct((8, D_FF), F32)
    return pl.pallas_call(
        _ffn_up_kernel,
        out_shape=[jax.ShapeDtypeStruct((m, D_FF), BF16), tail, tail],
        grid=(nj, m // tm),
        in_specs=[
            pl.BlockSpec((tm, D_MODEL), lambda j, i: (i, 0)),
            pl.BlockSpec((D_MODEL, tn), lambda j, i: (0, j)),
            pl.BlockSpec((D_MODEL, tn), lambda j, i: (0, nj + j)),
            pl.BlockSpec((CONV_W, tn), lambda j, i: (0, j)),
            pl.BlockSpec((CONV_W, tn), lambda j, i: (0, nj + j)),
            pl.BlockSpec((1, tn), lambda j, i: (0, j)),
            pl.BlockSpec((1, tn), lambda j, i: (0, nj + j)),
        ],
        out_specs=[pl.BlockSpec((tm, tn), lambda j, i: (i, j)),
                   pl.BlockSpec((8, tn), lambda j, i: (0, j)),
                   pl.BlockSpec((8, tn), lambda j, i: (0, j))],
        scratch_shapes=[pltpu.VMEM((8, tn), F32), pltpu.VMEM((8, tn), F32)],
        compiler_params=_params(("arbitrary", "arbitrary")),
        name="ffn_up_prompt",
    )(hn, w_up, w_up, conv_w, conv_w, conv_b, conv_b)


def _ffn_act_step_kernel(up_ref, prev_ref, cw_ref, cb_ref, act_ref):
    up = up_ref[...]
    w = 2 * D_FF
    c = cb_ref[...] + cw_ref[0:1, :] * prev_ref[:, 0:w] + cw_ref[1:2, :] * prev_ref[:, w:2 * w] + cw_ref[2:3, :] * up
    act_ref[...] = (_gelu(c[:, 0:D_FF]) * c[:, D_FF:w]).astype(act_ref.dtype)


def _ffn_act_step(up, conv_prev, conv_w, conv_b):
    b = up.shape[0]
    return pl.pallas_call(
        _ffn_act_step_kernel,
        out_shape=jax.ShapeDtypeStruct((b, D_FF), BF16),
        compiler_params=pltpu.CompilerParams(vmem_limit_bytes=VMEM_LIMIT),
        name="ffn_act_sample",
    )(up, conv_prev.reshape(b, (CONV_W - 1) * 2 * D_FF), conv_w, conv_b)


def _down_kernel(a_ref, w_ref, h_ref, g_ref, o_ref, acc_ref, *, nk):
    kstep = pl.program_id(1)

    @pl.when(kstep == 0)
    def _():
        acc_ref[...] = jnp.zeros_like(acc_ref)

    acc_ref[...] += _dot_bf16(a_ref[...], w_ref[...])

    @pl.when(kstep == nk - 1)
    def _():
        out = h_ref[...] + acc_ref[...]
        inv = lax.rsqrt(jnp.mean(out * out, axis=-1, keepdims=True) + EPS_RMS)
        o_ref[...] = out * inv * g_ref[...]


def _ffn_down(act, w_down, h, g, tm, tk, name):
    m = act.shape[0]
    nk = D_FF // tk
    return pl.pallas_call(
        functools.partial(_down_kernel, nk=nk),
        out_shape=jax.ShapeDtypeStruct((m, D_MODEL), F32),
        grid=(m // tm, nk),
        in_specs=[
            pl.BlockSpec((tm, tk), lambda i, k: (i, k)),
            pl.BlockSpec((tk, D_MODEL), lambda i, k: (k, 0)),
            pl.BlockSpec((tm, D_MODEL), lambda i, k: (i, 0)),
            pl.BlockSpec((1, D_MODEL), lambda i, k: (0, 0)),
        ],
        out_specs=pl.BlockSpec((tm, D_MODEL), lambda i, k: (i, 0)),
        scratch_shapes=[pltpu.VMEM((tm, D_MODEL), F32)],
        compiler_params=_params(("parallel", "arbitrary")),
        name=name,
    )(act, w_down, h, g.reshape(1, D_MODEL))


def _rel_bucket(dist):
    max_exact = N_BUCKETS // 2
    d_f = jnp.maximum(dist, 1).astype(F32)
    large = max_exact + (jnp.log(d_f / max_exact) / math.log(MAX_DISTANCE / max_exact)
                         * (N_BUCKETS - max_exact)).astype(jnp.int32)
    large = jnp.minimum(large, N_BUCKETS - 1)
    return jnp.where(dist < max_exact, dist, large)


def _bias_rows(tab, dist):
    onehot = (_rel_bucket(dist)[None, :] == jnp.arange(N_BUCKETS)[:, None]).astype(F32)
    return jnp.dot(tab.T, onehot, precision=HIGHEST)


def _bias_tables(rel_bias, gi):
    dil = DILATIONS[gi]
    win = WINDOWS[gi]
    reach = win // dil
    tab = rel_bias[:, gi * H_G:(gi + 1) * H_G]
    qi = jnp.arange(BLK)[:, None]
    ki = jnp.arange(2 * BLK)[None, :]
    rel = qi + BLK - ki
    blk = _bias_rows(tab, (dil * jnp.maximum(rel, 0)).reshape(-1)).reshape(H_G, BLK, 2 * BLK)
    blk = jnp.where(((rel >= 0) & (rel <= reach))[None], blk, -jnp.inf)
    back = win - jnp.arange(win)
    step = jnp.where((back % dil == 0)[None, :], _bias_rows(tab, back), -jnp.inf)[:, None, :]
    step0 = _bias_rows(tab, jnp.zeros((1,), jnp.int32))[:, None, :]
    return blk, step, step0


def _layer(x, tiles, lw, bias_blk, prompt, state=None):
    m = x.shape[0]
    xn = _rmsnorm(x, lw["norm1_g"], tiles["rms"], "rms1_" + tiles["tag"])
    proj = _matmul(xn, lw["w_in"], lw["in_bias"], tiles["proj_m"], 256, COL_GATE, "proj_" + tiles["tag"])

    if prompt:
        rt, kt, qt, pt, v, bonus, g, gc = _rwkv_prep(proj, None, 128, CHUNK, lw, "rwkv_prep_prompt")
        o, wkv_new = _wkv_scan(pt, rt, qt, kt, v, gc)
        wkv_new = wkv_new[None, None]
    else:
        rt, kt, qt, pt, v, bonus, g, gc = _rwkv_prep(proj, state["shift"], m, 1, lw, "rwkv_prep_sample")
        wkv_new, o = _wkv_step(state["wkv"], pt, rt, qt, kt, v, gc)
    ya = _rwkv_post(o, bonus, g, lw, tiles["post"], "rwkv_post_" + tiles["tag"])

    outs, lses = [], []
    for gi in range(N_GROUPS):
        if prompt:
            o_g, l_g = _attn_prompt(proj, bias_blk[gi][0], gi, 2048)
        else:
            sl = lambda c0: proj[:, c0 + gi * D_G:c0 + (gi + 1) * D_G].reshape(m, H_G, 1, HEAD_B)
            o_g, l_g = _attn_step(sl(COL_QKV), sl(COL_QKV + D_B), sl(COL_QKV + 2 * D_B), state["win"][gi],
                                  bias_blk[gi][1], bias_blk[gi][2], gi)
        outs.append(o_g)
        lses.append(l_g)
    yb = _merge(outs, lses, tiles["merge"], "merge_" + tiles["tag"])

    mixed = _mix(ya, yb, lw["w_out_a"], lw["w_out_b"], proj, tiles["mix_m"], 256, "mix_" + tiles["tag"])
    h = _matmul_residual(mixed, lw["w_o"], x, tiles["wo_m"], 512, "wo_" + tiles["tag"])

    hn = _rmsnorm(h, lw["norm2_g"], tiles["rms"], "rms2_" + tiles["tag"])
    if prompt:
        act, t1, t2 = _ffn_up_prompt(hn, lw["w_up"], lw["conv_w"], lw["conv_b"], 1024, 512)
        conv_new = jnp.concatenate([t1[6:8], t2[6:8]], axis=1)[None, None]
    else:
        up = _matmul(hn, lw["w_up"], lw["up_bias"], m, 512, 2 * D_FF, "up_sample")
        act = _ffn_act_step(up, state["conv"], lw["conv_w"], lw["conv_b"])
        conv_new = jnp.concatenate([state["conv"][:, 1:], up[:, None, :]], axis=1)[None]
    y = _ffn_down(act, lw["w_down"], h, lw["normf_g"], tiles["down_m"], 1408, "down_" + tiles["tag"])
    return y, proj, wkv_new, conv_new


def kernel(x_prompt, x_sample, state_wkv, state_shift, state_ffn_conv, cache_win1, cache_win2, cache_win3, rel_bias,
           norm1_g, w_in, gate_b, mu_shift, w0, w_up_decay, a0, w_up_aaa, w_up_gate, k_k, k_a, r_k, gn_g, gn_b,
           w_out_a, w_out_b, w_o, norm2_g, w_up, conv_w, conv_b, w_down, normf_g):
    row = lambda a: a.reshape(1, -1)
    w_lora = jnp.zeros((D_LORA, 3 * D_A), F32)
    w_lora = w_lora.at[0:D_DECAY_LORA, 0:D_A].set(w_up_decay[0])
    w_lora = w_lora.at[D_DECAY_LORA:D_DECAY_LORA + D_AAA_LORA, D_A:2 * D_A].set(w_up_aaa[0])
    w_lora = w_lora.at[D_DECAY_LORA + D_AAA_LORA:, 2 * D_A:].set(w_up_gate[0])
    head = np.arange(D_A) // HEAD_A
    lw = dict(
        norm1_g=norm1_g[0], norm2_g=norm2_g[0], normf_g=normf_g,
        w_in=w_in[0].astype(BF16),
        in_bias=jnp.concatenate([jnp.zeros((1, COL_GATE), F32), row(gate_b[0])], axis=1),
        up_bias=jnp.zeros((1, 2 * D_FF), F32),
        mu=row(mu_shift[0]), w_lora=w_lora, w0=row(w0[0]), a0=row(a0[0]), k_k=row(k_k[0]), k_a=row(k_a[0]),
        r_k=row(r_k[0]), gn_g=row(gn_g[0]), gn_b=row(gn_b[0]),
        bd=jnp.asarray(head[:, None] == head[None, :], BF16),
        w_out_a=w_out_a[0].astype(BF16), w_out_b=w_out_b[0].astype(BF16), w_o=w_o[0].astype(BF16),
        w_up=w_up[0].astype(BF16), conv_w=conv_w[0], conv_b=row(conv_b[0]), w_down=w_down[0].astype(BF16),
    )
    bias = [_bias_tables(rel_bias, gi) for gi in range(N_GROUPS)]

    tiles_p = dict(tag="prompt", rms=512, proj_m=2048, post=256, merge=1024, mix_m=1024, wo_m=1024, down_m=512)
    y_p, proj_p, wkv_p, conv_p = _layer(x_prompt[0], tiles_p, lw, bias, prompt=True)

    b = DEC_BATCH
    tiles_s = dict(tag="sample", rms=b, proj_m=b, post=b, merge=b, mix_m=b, wo_m=b, down_m=b)
    state = dict(wkv=state_wkv, shift=state_shift.reshape(b, D_SHIFT), conv=state_ffn_conv[0],
                 win=(cache_win1, cache_win2, cache_win3))
    y_s, proj_s, wkv_s, conv_s = _layer(x_sample[:, 0], tiles_s, lw, bias, prompt=False, state=state)

    def kv_rows(proj, lo, gi):
        k = proj[lo:, COL_QKV + D_B + gi * D_G:COL_QKV + D_B + (gi + 1) * D_G]
        v = proj[lo:, COL_QKV + 2 * D_B + gi * D_G:COL_QKV + 2 * D_B + (gi + 1) * D_G]
        n = k.shape[0]
        return jnp.stack([k.reshape(n, H_G, HEAD_B), v.reshape(n, H_G, HEAD_B)], axis=1)

    win_p = [kv_rows(proj_p, SEQ - min(WINDOWS[gi], SEQ), gi)[None, None] for gi in range(N_GROUPS)]
    win_s = [kv_rows(proj_s, 0, gi)[None, :, None] for gi in range(N_GROUPS)]
    return (y_p[None], y_s[:, None],
            wkv_p, wkv_s,
            proj_p[SEQ - 1:, 0:D_SHIFT][None, None], proj_s[:, 0:D_SHIFT][None, :, None],
            conv_p, conv_s,
            win_p[0], win_s[0], win_p[1], win_s[1], win_p[2], win_s[2])
```

```python
import functools
import math

import numpy as np
import jax
import jax.numpy as jnp
from jax import lax
from jax.experimental import pallas as pl
from jax.experimental.pallas import tpu as pltpu

F32 = jnp.float32
BF16 = jnp.bfloat16
HIGHEST = lax.Precision.HIGHEST

D_MODEL = 2048
SEQ = 8192
DEC_BATCH = 32
HEAD_A = 64
H_A = 16
D_A = H_A * HEAD_A
D_DECAY_LORA = 96
D_AAA_LORA = 96
D_GATE_LORA = 64
D_LORA = D_DECAY_LORA + D_AAA_LORA + D_GATE_LORA
D_SHIFT = 3 * D_A + D_LORA
EPS_GN = 64e-5
HEAD_B = 64
H_G = 8
D_G = H_G * HEAD_B
WINDOWS = (128, 512, 2048)
DILATIONS = (1, 4, 16)
N_GROUPS = 3
D_B = N_GROUPS * D_G
BLK = 128
N_BUCKETS = 32
MAX_DISTANCE = 2048
COL_QKV = D_SHIFT
COL_GATE = D_SHIFT + 3 * D_B
D_IN = COL_GATE + 2 * D_MODEL
PROJ_TN = 1024
D_IN_PAD = -(-D_IN // PROJ_TN) * PROJ_TN
D_FF = 5632
CONV_W = 3
EPS_RMS = 1e-6
CHUNK = 64
LANES = 128
VMEM_LIMIT = 56 * 1024 * 1024


def _params(sem, vmem=VMEM_LIMIT):
    return pltpu.CompilerParams(dimension_semantics=sem, vmem_limit_bytes=vmem)


def _sigmoid(x):
    return 1.0 / (1.0 + jnp.exp(-x))


def _dot_bf16(a, b):
    return jnp.dot(a, b, preferred_element_type=F32)


def _split2(x):
    hi = x.astype(BF16)
    return hi, (x - hi.astype(F32)).astype(BF16)


def _dot3(a, b):
    ah, al = _split2(a)
    bh, bl = _split2(b)
    return _dot_bf16(ah, bh) + _dot_bf16(ah, bl) + _dot_bf16(al, bh)


def _split3(x):
    hi = x.astype(BF16)
    r1 = x - hi.astype(F32)
    mid = r1.astype(BF16)
    lo = (r1 - mid.astype(F32)).astype(BF16)
    return hi, mid, lo


def _dot_exact_rhs(x, m):
    hi, mid, lo = _split3(x)
    return _dot_bf16(hi, m) + _dot_bf16(mid, m) + _dot_bf16(lo, m)


def _head_sums(x, hsum, hbc):
    return _dot_exact_rhs(_dot_exact_rhs(x, hsum), hbc)


def _dot_exact_lhs(m, x):
    hi, mid, lo = _split3(x)
    return _dot_bf16(m, hi) + _dot_bf16(m, mid) + _dot_bf16(m, lo)


def _rms_kernel(x_ref, g_ref, o_ref):
    x = x_ref[...]
    inv = lax.rsqrt(jnp.mean(x * x, axis=-1, keepdims=True) + EPS_RMS)
    o_ref[...] = (x * inv * g_ref[...]).astype(o_ref.dtype)


def _rmsnorm(x, g, tm, name):
    m, d = x.shape
    return pl.pallas_call(
        _rms_kernel,
        out_shape=jax.ShapeDtypeStruct((m, d), BF16),
        grid=(m // tm,),
        in_specs=[pl.BlockSpec((tm, d), lambda i: (i, 0)), pl.BlockSpec((1, d), lambda i: (0, 0))],
        out_specs=pl.BlockSpec((tm, d), lambda i: (i, 0)),
        compiler_params=_params(("parallel",)),
        name=name,
    )(x, g.reshape(1, d))


def _proj_kernel(a_ref, w_ref, b_ref, o_ref, *, gate_col):
    acc = _dot_bf16(a_ref[...], w_ref[...])
    tn = o_ref.shape[1]
    col = lax.broadcasted_iota(jnp.int32, (1, tn), 1) + pl.program_id(1) * tn
    o_ref[...] = jnp.where(col >= gate_col, _sigmoid(acc + b_ref[...]), acc)


def _matmul(a, w, bias, tm, tn, gate_col, name):
    m, k = a.shape
    n = w.shape[1]
    return pl.pallas_call(
        functools.partial(_proj_kernel, gate_col=gate_col),
        out_shape=jax.ShapeDtypeStruct((m, n), F32),
        grid=(m // tm, n // tn),
        in_specs=[
            pl.BlockSpec((tm, k), lambda i, j: (i, 0)),
            pl.BlockSpec((k, tn), lambda i, j: (0, j)),
            pl.BlockSpec((1, tn), lambda i, j: (0, j)),
        ],
        out_specs=pl.BlockSpec((tm, tn), lambda i, j: (i, j)),
        compiler_params=_params(("parallel", "arbitrary")),
        name=name,
    )(a, w, bias)


def _prep_kernel(p_ref, prev_ref, mu_ref, wl_ref, w0_ref, a0_ref, kk_ref, ka_ref, rk_ref, lmat_ref, sel_ref,
                 hs_ref, hb_ref, rt_ref, kt_ref, qt_ref, pt_ref, v_ref, bonus_ref, g_ref, gc_ref, *, rows_are_time):
    p = p_ref[...]
    tm = p.shape[0]
    if rows_are_time:
        last = jnp.where(pl.program_id(0) == 0, 0.0, prev_ref[7:8, :])
        row = lax.broadcasted_iota(jnp.int32, (tm, 1), 0)
        prev = jnp.where(row == 0, last, pltpu.roll(p, 1, axis=0))
    else:
        prev = prev_ref[...]
    xm = p + mu_ref[...] * (prev - p)
    r = xm[:, 0:D_A]
    k = xm[:, D_A:2 * D_A]
    v = xm[:, 2 * D_A:3 * D_A]
    xl = xm[:, 3 * D_A:D_SHIFT]
    lane = lax.broadcasted_iota(jnp.int32, xl.shape, 1)
    act = jnp.where(lane < D_DECAY_LORA, jnp.tanh(xl),
                    jnp.where(lane < D_DECAY_LORA + D_AAA_LORA, xl, _sigmoid(xl)))
    lora = _dot3(act, wl_ref[...])
    y = -(w0_ref[...] + lora[:, 0:D_A])
    softplus = jnp.maximum(y, 0.0) + jnp.log(1.0 + jnp.exp(-jnp.abs(y)))
    logw = -jnp.exp(-softplus - 0.5)
    a = _sigmoid(a0_ref[...] + lora[:, D_A:2 * D_A])
    g = lora[:, 2 * D_A:3 * D_A]
    kkr = k * kk_ref[...]
    kp = k * (1.0 + (a - 1.0) * ka_ref[...])
    seg = _head_sums(jnp.concatenate([kkr * kkr, r * kp * rk_ref[...]], axis=0), hs_ref[...], hb_ref[...])
    kk = kkr / jnp.maximum(jnp.sqrt(seg[0:tm]), 1e-12)
    cum = _dot_exact_lhs(lmat_ref[...], logw)
    e_out = jnp.exp(-cum)
    rt_ref[...] = r * jnp.exp(cum)
    kt_ref[...] = kp * e_out
    qt_ref[...] = kk * a * e_out
    pt_ref[...] = -kk * jnp.exp(cum - logw)
    v_ref[...] = v
    bonus_ref[...] = seg[tm:2 * tm] * v
    g_ref[...] = g
    gc_ref[...] = jnp.exp(_dot_exact_lhs(sel_ref[...], cum))


def _rwkv_prep(proj, prev, tm, chunk, lw, name):
    m = proj.shape[0]
    rows_are_time = prev is None
    n_tiles = m // tm
    t = np.arange(tm)
    lmat = ((t[:, None] // chunk == t[None, :] // chunk) & (t[None, :] <= t[:, None])).astype(np.float32)
    sel_stride = 1 if chunk == 1 else 8
    n_sel = sel_stride * (tm // chunk)
    sel = np.zeros((n_sel, tm), np.float32)
    for c in range(tm // chunk):
        sel[sel_stride * c, (c + 1) * chunk - 1] = 1.0
    if rows_are_time:
        prev_arr = proj
        prev_spec = pl.BlockSpec((8, D_SHIFT), lambda i: (jnp.maximum(i * (tm // 8) - 1, 0), 0))
    else:
        prev_arr = prev
        prev_spec = pl.BlockSpec((tm, D_SHIFT), lambda i: (i, 0))
    vec = lambda d: pl.BlockSpec((1, d), lambda i: (0, 0))
    full = lambda a: pl.BlockSpec(a.shape, lambda i: (0, 0))
    big = pl.BlockSpec((tm, D_A), lambda i: (i, 0))
    out = jax.ShapeDtypeStruct((m, D_A), F32)
    lmat_b = jnp.asarray(lmat, BF16)
    sel_b = jnp.asarray(sel, BF16)
    return pl.pallas_call(
        functools.partial(_prep_kernel, rows_are_time=rows_are_time),
        out_shape=[out] * 7 + [jax.ShapeDtypeStruct((n_tiles * n_sel, D_A), F32)],
        grid=(n_tiles,),
        in_specs=[pl.BlockSpec((tm, D_SHIFT), lambda i: (i, 0)), prev_spec, vec(D_SHIFT), full(lw["w_lora"]),
                  vec(D_A), vec(D_A), vec(D_A), vec(D_A), vec(D_A), full(lmat_b), full(sel_b), full(lw["hsum"]),
                  full(lw["hbc"])],
        out_specs=[big] * 7 + [pl.BlockSpec((n_sel, D_A), lambda i: (i, 0))],
        compiler_params=_params(("parallel",)),
        name=name,
    )(proj, prev_arr, lw["mu"], lw["w_lora"], lw["w0"], lw["a0"], lw["k_k"], lw["k_a"], lw["r_k"], lmat_b, sel_b,
      lw["hsum"], lw["hbc"])


def _bdot3(a, b, dims):
    ah, al = _split2(a)
    bh, bl = _split2(b)
    dg = lambda x, y: lax.dot_general(x, y, dims, preferred_element_type=F32)
    return dg(ah, bh) + dg(ah, bl) + dg(al, bh)


_NN = (((2,), (1,)), ((0,), (0,)))
_NT = (((2,), (2,)), ((0,), (0,)))


def _scan_kernel(pt_ref, rt_ref, qt_ref, kt_ref, v_ref, gc_ref, o_ref, sfin_ref, s_scr):
    @pl.when(pl.program_id(0) == 0)
    def _():
        s_scr[...] = jnp.zeros_like(s_scr)

    def heads(x):
        return jnp.stack([x[:, h * HEAD_A:(h + 1) * HEAD_A] for h in range(H_A)], axis=0)

    p, r, q, k, v = (heads(ref[...]) for ref in (pt_ref, rt_ref, qt_ref, kt_ref, v_ref))
    gam = heads(gc_ref[0:1, :])
    ri = lax.broadcasted_iota(jnp.int32, (1, CHUNK, CHUNK), 1)
    ci = lax.broadcasted_iota(jnp.int32, (1, CHUNK, CHUNK), 2)
    strict = ri > ci
    incl = ri >= ci
    eye = (ri == ci).astype(F32)

    gram = _bdot3(jnp.concatenate([p, r], axis=1), jnp.concatenate([q, k], axis=1), _NT)
    a_qp = jnp.where(strict, gram[:, 0:CHUNK, 0:CHUNK], 0.0)
    a_kp = jnp.where(strict, gram[:, 0:CHUNK, CHUNK:], 0.0)
    a_rq = jnp.where(incl, gram[:, CHUNK:, 0:CHUNK], 0.0)
    a_rk = jnp.where(incl, gram[:, CHUNK:, CHUNK:], 0.0)
    same = lambda log2_bs: (ri >> log2_bs) == (ci >> log2_bs)
    a_d = jnp.where(same(4), a_qp, 0.0)
    tinv = eye + a_d
    pw = _bdot3(a_d, a_d, _NN)
    for _ in range(2):
        both = _bdot3(jnp.concatenate([pw, tinv], axis=1), pw, _NN)
        pw = both[:, 0:CHUNK]
        tinv = tinv + both[:, CHUNK:]
    tinv = tinv + _bdot3(tinv, pw, _NN)
    for log2_bs in (4, 5):
        off = jnp.where(jnp.logical_and(same(log2_bs + 1), jnp.logical_not(same(log2_bs))), a_qp, 0.0)
        tinv = tinv + _bdot3(tinv, _bdot3(off, tinv, _NN), _NN)
    pw_hat = _bdot3(tinv, jnp.concatenate([p, _bdot3(a_kp, v, _NN)], axis=2), _NN)
    x = _bdot3(a_rq, pw_hat, _NN)
    r_hat = r + x[:, :, 0:HEAD_A]
    o_loc = x[:, :, HEAD_A:] + _bdot3(a_rk, v, _NN)
    y = _bdot3(jnp.swapaxes(pw_hat, 1, 2), q, _NN)
    m_mat = (eye + y[:, 0:HEAD_A]) * gam
    n_mat = (y[:, HEAD_A:] + _bdot3(jnp.swapaxes(v, 1, 2), k, _NN)) * gam

    s0 = s_scr[...]
    o = _bdot3(r_hat, s0, _NT) + o_loc
    s_scr[...] = _bdot3(s0, m_mat, _NN) + n_mat
    o_ref[...] = jnp.concatenate([o[h] for h in range(H_A)], axis=1)

    @pl.when(pl.program_id(0) == pl.num_programs(0) - 1)
    def _():
        sfin_ref[...] = s_scr[...]


def _wkv_scan(pt, rt, qt, kt, v, gc):
    t = pt.shape[0]
    blk = pl.BlockSpec((CHUNK, D_A), lambda c: (c, 0))
    return pl.pallas_call(
        _scan_kernel,
        out_shape=[jax.ShapeDtypeStruct((t, D_A), F32), jax.ShapeDtypeStruct((H_A, HEAD_A, HEAD_A), F32)],
        grid=(t // CHUNK,),
        in_specs=[blk] * 5 + [pl.BlockSpec((8, D_A), lambda c: (c, 0))],
        out_specs=[blk, pl.BlockSpec((H_A, HEAD_A, HEAD_A), lambda c: (0, 0, 0))],
        scratch_shapes=[pltpu.VMEM((H_A, HEAD_A, HEAD_A), F32)],
        compiler_params=_params(("arbitrary",)),
        name="wkv_scan_prompt",
    )(pt, rt, qt, kt, v, gc)


def _wkv_step_kernel(s_ref, rows_ref, sn_ref, o_ref):
    ri = lax.broadcasted_iota(jnp.int32, (1, HEAD_A, HEAD_A), 1)
    ci = lax.broadcasted_iota(jnp.int32, (1, HEAD_A, HEAD_A), 2)
    eye = ri == ci
    pt, rt, qt, kt, v, gc = (rows_ref[0, n] for n in range(6))
    s = s_ref[0, 0]
    u = jnp.sum(s * pt, axis=2, keepdims=True)
    v_col = jnp.sum(jnp.where(eye, v, 0.0), axis=2, keepdims=True)
    m = s + u * qt + v_col * kt
    sn_ref[0, 0] = m * gc
    o_col = jnp.sum(m * rt, axis=2, keepdims=True)
    o_ref[0] = jnp.sum(jnp.where(eye, o_col, 0.0), axis=1, keepdims=True)


def _wkv_step(state, pt, rt, qt, kt, v, gc):
    b = pt.shape[0]
    rows = jnp.stack([pt, rt, qt, kt, v, gc], axis=1).reshape(b, 6, H_A, 1, HEAD_A)
    st_spec = pl.BlockSpec((1, 1, H_A, HEAD_A, HEAD_A), lambda i: (0, i, 0, 0, 0))
    sn, o = pl.pallas_call(
        _wkv_step_kernel,
        out_shape=[jax.ShapeDtypeStruct(state.shape, F32), jax.ShapeDtypeStruct((b, H_A, 1, HEAD_A), F32)],
        grid=(b,),
        in_specs=[st_spec, pl.BlockSpec((1, 6, H_A, 1, HEAD_A), lambda i: (i, 0, 0, 0, 0))],
        out_specs=[st_spec, pl.BlockSpec((1, H_A, 1, HEAD_A), lambda i: (i, 0, 0, 0))],
        compiler_params=_params(("parallel",)),
        name="wkv_step_sample",
    )(state, rows)
    return sn, o.reshape(b, D_A)


def _post_kernel(o_ref, bonus_ref, g_ref, gng_ref, gnb_ref, hs_ref, hb_ref, y_ref):
    o = o_ref[...]
    hsum, hbc = hs_ref[...], hb_ref[...]
    mu = _head_sums(o, hsum, hbc) * (1.0 / HEAD_A)
    d = o - mu
    var = _head_sums(d * d, hsum, hbc) * (1.0 / HEAD_A)
    o_n = d * lax.rsqrt(var + EPS_GN) * gng_ref[...] + gnb_ref[...]
    y_ref[...] = ((o_n + bonus_ref[...]) * g_ref[...]).astype(y_ref.dtype)


def _rwkv_post(o, bonus, g, lw, tm, name):
    m = o.shape[0]
    big = pl.BlockSpec((tm, D_A), lambda i: (i, 0))
    vec = pl.BlockSpec((1, D_A), lambda i: (0, 0))
    return pl.pallas_call(
        _post_kernel,
        out_shape=jax.ShapeDtypeStruct((m, D_A), BF16),
        grid=(m // tm,),
        in_specs=[big, big, big, vec, vec, pl.BlockSpec((D_A, LANES), lambda i: (0, 0)),
                  pl.BlockSpec((LANES, D_A), lambda i: (0, 0))],
        out_specs=big,
        compiler_params=_params(("parallel",)),
        name=name,
    )(o, bonus, g, lw["gn_g"], lw["gn_b"], lw["hsum"], lw["hbc"])


def _attn_kernel(q_ref, kc_ref, vc_ref, kh_ref, vh_ref, bias_ref, o_ref, l_ref, *, dil, m_blocks):
    scale = HEAD_B ** -0.5
    n_units = dil * m_blocks

    def rows(ref, start, size):
        if dil == 1:
            return ref[pl.ds(start, size), :]
        return ref[pl.ds(start, size, stride=dil), :]

    qs, ks, vs, bases = [], [], [], []
    for r in range(dil):
        for mb in range(m_blocks):
            base = r + dil * BLK * mb
            bases.append(base)
            qs.append(rows(q_ref, base, BLK))
            if mb == 0:
                ks.append(jnp.concatenate([rows(kh_ref, r, BLK), rows(kc_ref, r, BLK)], axis=0))
                vs.append(jnp.concatenate([rows(vh_ref, r, BLK), rows(vc_ref, r, BLK)], axis=0))
            else:
                ks.append(rows(kc_ref, base - dil * BLK, 2 * BLK))
                vs.append(rows(vc_ref, base - dil * BLK, 2 * BLK))

    def batch(xs):
        return jnp.stack([x[:, sub * HEAD_B:(sub + 1) * HEAD_B].astype(BF16) for sub in range(2) for x in xs], axis=0)

    qb, kb, vb = batch(qs), batch(ks), batch(vs)
    s = lax.dot_general(qb, kb, (((2,), (2,)), ((0,), (0,))), preferred_element_type=F32) * scale
    s = s.reshape(2, n_units, BLK, 2 * BLK) + bias_ref[...][:, None]
    unit = lax.broadcasted_iota(jnp.int32, (1, n_units, 1, 2 * BLK), 1)
    col = lax.broadcasted_iota(jnp.int32, (1, n_units, 1, 2 * BLK), 3)
    no_prev = jnp.logical_and(jnp.logical_and(pl.program_id(0) == 0, (unit & (m_blocks - 1)) == 0), col < BLK)
    s = jnp.where(no_prev, -jnp.inf, s)
    m = jnp.max(s, axis=-1, keepdims=True)
    p = jnp.exp(s - m)
    l = jnp.sum(p, axis=-1, keepdims=True)
    pv = lax.dot_general(p.astype(BF16).reshape(2 * n_units, BLK, 2 * BLK), vb, (((2,), (1,)), ((0,), (0,))),
                         preferred_element_type=F32)
    o = pv.reshape(2, n_units, BLK, HEAD_B) / l
    lse = jnp.broadcast_to(m + jnp.log(l), (2, n_units, BLK, HEAD_B))
    for u, base in enumerate(bases):
        o_blk = jnp.concatenate([o[0, u], o[1, u]], axis=1)
        l_blk = jnp.concatenate([lse[0, u], lse[1, u]], axis=1)
        if dil == 1:
            o_ref[pl.ds(base, BLK), :] = o_blk
            l_ref[pl.ds(base, BLK), :] = l_blk
        else:
            o_ref[pl.ds(base, BLK, stride=dil), :] = o_blk
            l_ref[pl.ds(base, BLK, stride=dil), :] = l_blk


def _attn_prompt(proj, bias, gi, rows_per_step):
    t = proj.shape[0]
    dil = DILATIONS[gi]
    span = BLK * dil
    m_blocks = rows_per_step // span
    cq = (COL_QKV + gi * D_G) // LANES
    ck = (COL_QKV + D_B + gi * D_G) // LANES
    cv = (COL_QKV + 2 * D_B + gi * D_G) // LANES
    cur = lambda c0: pl.BlockSpec((rows_per_step, LANES), lambda i, hp: (i, c0 + hp))
    halo = lambda c0: pl.BlockSpec((span, LANES), lambda i, hp: (jnp.maximum(i * m_blocks - 1, 0), c0 + hp))
    out_spec = pl.BlockSpec((rows_per_step, LANES), lambda i, hp: (i, hp))
    out = jax.ShapeDtypeStruct((t, D_G), F32)
    return pl.pallas_call(
        functools.partial(_attn_kernel, dil=dil, m_blocks=m_blocks),
        out_shape=[out, out],
        grid=(t // rows_per_step, H_G // 2),
        in_specs=[cur(cq), cur(ck), cur(cv), halo(ck), halo(cv),
                  pl.BlockSpec((2, BLK, 2 * BLK), lambda i, hp: (hp, 0, 0))],
        out_specs=[out_spec, out_spec],
        compiler_params=_params(("parallel", "parallel")),
        name=f"attn_prompt_g{gi}",
    )(proj, proj, proj, proj, proj, bias)


def _attn_step_kernel(q_ref, kn_ref, vn_ref, c_ref, bias_ref, bias0_ref, o_ref, l_ref):
    scale = HEAD_B ** -0.5
    ri = lax.broadcasted_iota(jnp.int32, (1, HEAD_B, HEAD_B), 1)
    ci = lax.broadcasted_iota(jnp.int32, (1, HEAD_B, HEAD_B), 2)
    eye = ri == ci
    q = q_ref[0]
    q_col = jnp.sum(jnp.where(eye, q, 0.0), axis=2, keepdims=True)
    s = jnp.sum(c_ref[0] * q_col, axis=1, keepdims=True) * scale + bias_ref[...]
    s_new = jnp.sum(q * kn_ref[0], axis=2, keepdims=True) * scale + bias0_ref[...]
    m = jnp.maximum(jnp.max(s, axis=2, keepdims=True), s_new)
    p = jnp.exp(s - m)
    p_new = jnp.exp(s_new - m)
    l = jnp.sum(p, axis=2, keepdims=True) + p_new
    o_col = jnp.sum(c_ref[1] * p, axis=2, keepdims=True)
    o_row = jnp.sum(jnp.where(eye, o_col, 0.0), axis=1, keepdims=True)
    o_ref[0] = (o_row + p_new * vn_ref[0]) / l
    l_ref[0] = jnp.broadcast_to(m + jnp.log(l), (H_G, 1, HEAD_B))


def _attn_step(q, k_new, v_new, cache, bias, bias0, gi):
    b = q.shape[0]
    w = cache.shape[2]
    cache_t = jnp.transpose(cache, (0, 1, 3, 4, 5, 2))
    vec = pl.BlockSpec((1, H_G, 1, HEAD_B), lambda i: (i, 0, 0, 0))
    out = jax.ShapeDtypeStruct((b, H_G, 1, HEAD_B), F32)
    o, l = pl.pallas_call(
        _attn_step_kernel,
        out_shape=[out, out],
        grid=(b,),
        in_specs=[vec, vec, vec,
                  pl.BlockSpec((None, None, 2, H_G, HEAD_B, w), lambda i: (0, i, 0, 0, 0, 0)),
                  pl.BlockSpec((H_G, 1, w), lambda i: (0, 0, 0)),
                  pl.BlockSpec((H_G, 1, 1), lambda i: (0, 0, 0))],
        out_specs=[vec, vec],
        compiler_params=_params(("parallel",)),
        name=f"attn_step_g{gi}",
    )(q, k_new, v_new, cache_t, bias, bias0)
    return o.reshape(b, D_G), l.reshape(b, D_G)


def _merge_kernel(o1_ref, o2_ref, o3_ref, l1_ref, l2_ref, l3_ref, y_ref):
    l1, l2, l3 = l1_ref[...], l2_ref[...], l3_ref[...]
    m = jnp.maximum(jnp.maximum(l1, l2), l3)
    e1, e2, e3 = jnp.exp(l1 - m), jnp.exp(l2 - m), jnp.exp(l3 - m)
    den = e1 + e2 + e3
    y = (e1 / den) * o1_ref[...] + (e2 / den) * o2_ref[...] + (e3 / den) * o3_ref[...]
    y_ref[...] = y.astype(y_ref.dtype)


def _merge(outs, lses, tm, name):
    m = outs[0].shape[0]
    blk = pl.BlockSpec((tm, D_G), lambda i: (i, 0))
    return pl.pallas_call(
        _merge_kernel,
        out_shape=jax.ShapeDtypeStruct((m, D_G), BF16),
        grid=(m // tm,),
        in_specs=[blk] * 6,
        out_specs=blk,
        compiler_params=_params(("parallel",)),
        name=name,
    )(*outs, *lses)


def _mix_kernel(ya_ref, yb_ref, wa_ref, wb_ref, ga_ref, gb_ref, o_ref):
    mixed = ga_ref[...] * _dot_bf16(ya_ref[...], wa_ref[...]) + gb_ref[...] * _dot_bf16(yb_ref[...], wb_ref[...])
    o_ref[...] = mixed.astype(o_ref.dtype)


def _mix(ya, yb, wa, wb, proj, tm, tn, name):
    m = ya.shape[0]
    ga0 = COL_GATE // tn
    gb0 = (COL_GATE + D_MODEL) // tn
    return pl.pallas_call(
        _mix_kernel,
        out_shape=jax.ShapeDtypeStruct((m, D_MODEL), BF16),
        grid=(m // tm, D_MODEL // tn),
        in_specs=[
            pl.BlockSpec((tm, D_A), lambda i, j: (i, 0)),
            pl.BlockSpec((tm, D_G), lambda i, j: (i, 0)),
            pl.BlockSpec((D_A, tn), lambda i, j: (0, j)),
            pl.BlockSpec((D_G, tn), lambda i, j: (0, j)),
            pl.BlockSpec((tm, tn), lambda i, j: (i, ga0 + j)),
            pl.BlockSpec((tm, tn), lambda i, j: (i, gb0 + j)),
        ],
        out_specs=pl.BlockSpec((tm, tn), lambda i, j: (i, j)),
        compiler_params=_params(("parallel", "arbitrary")),
        name=name,
    )(ya, yb, wa, wb, proj, proj)


def _resid_kernel(a_ref, w_ref, x_ref, o_ref):
    o_ref[...] = x_ref[...] + _dot_bf16(a_ref[...], w_ref[...])


def _matmul_residual(a, w, x, tm, tn, name):
    m, k = a.shape
    n = w.shape[1]
    return pl.pallas_call(
        _resid_kernel,
        out_shape=jax.ShapeDtypeStruct((m, n), F32),
        grid=(m // tm, n // tn),
        in_specs=[
            pl.BlockSpec((tm, k), lambda i, j: (i, 0)),
            pl.BlockSpec((k, tn), lambda i, j: (0, j)),
            pl.BlockSpec((tm, tn), lambda i, j: (i, j)),
        ],
        out_specs=pl.BlockSpec((tm, tn), lambda i, j: (i, j)),
        compiler_params=_params(("parallel", "arbitrary")),
        name=name,
    )(a, w, x)


def _gelu(x):
    return 0.5 * x * (1.0 + lax.erf(x * (1.0 / math.sqrt(2.0))))


def _ffn_up_kernel(h_ref, w1_ref, w2_ref, cw1_ref, cw2_ref, cb1_ref, cb2_ref, act_ref, t1_ref, t2_ref,
                   c1_scr, c2_scr):
    tm = h_ref.shape[0]

    @pl.when(pl.program_id(1) == 0)
    def _():
        c1_scr[...] = jnp.zeros_like(c1_scr)
        c2_scr[...] = jnp.zeros_like(c2_scr)

    a = h_ref[...]
    row = lax.broadcasted_iota(jnp.int32, (tm, 1), 0)

    def conv(u, carry_ref, cw_ref, cb_ref):
        m1 = jnp.where(row == 0, carry_ref[7:8, :], pltpu.roll(u, 1, axis=0))
        m2 = jnp.where(row == 0, carry_ref[6:7, :], jnp.where(row == 1, carry_ref[7:8, :], pltpu.roll(u, 2, axis=0)))
        return cb_ref[...] + cw_ref[0:1, :] * m2 + cw_ref[1:2, :] * m1 + cw_ref[2:3, :] * u

    u1 = _dot_bf16(a, w1_ref[...])
    u2 = _dot_bf16(a, w2_ref[...])
    c1 = conv(u1, c1_scr, cw1_ref, cb1_ref)
    c2 = conv(u2, c2_scr, cw2_ref, cb2_ref)
    act_ref[...] = (_gelu(c1) * c2).astype(act_ref.dtype)
    c1_scr[...] = u1[tm - 8:tm]
    c2_scr[...] = u2[tm - 8:tm]
    t1_ref[...] = u1[tm - 8:tm]
    t2_ref[...] = u2[tm - 8:tm]


def _ffn_up_prompt(hn, w_up, conv_w, conv_b, tm, tn):
    m = hn.shape[0]
    nj = D_FF // tn
    tail = jax.ShapeDtypeStruct((8, D_FF), F32)
    return pl.pallas_call(
        _ffn_up_kernel,
        out_shape=[jax.ShapeDtypeStruct((m, D_FF), BF16), tail, tail],
        grid=(nj, m // tm),
        in_specs=[
            pl.BlockSpec((tm, D_MODEL), lambda j, i: (i, 0)),
            pl.BlockSpec((D_MODEL, tn), lambda j, i: (0, j)),
            pl.BlockSpec((D_MODEL, tn), lambda j, i: (0, nj + j)),
            pl.BlockSpec((CONV_W, tn), lambda j, i: (0, j)),
            pl.BlockSpec((CONV_W, tn), lambda j, i: (0, nj + j)),
            pl.BlockSpec((1, tn), lambda j, i: (0, j)),
            pl.BlockSpec((1, tn), lambda j, i: (0, nj + j)),
        ],
        out_specs=[pl.BlockSpec((tm, tn), lambda j, i: (i, j)),
                   pl.BlockSpec((8, tn), lambda j, i: (0, j)),
                   pl.BlockSpec((8, tn), lambda j, i: (0, j))],
        scratch_shapes=[pltpu.VMEM((8, tn), F32), pltpu.VMEM((8, tn), F32)],
        compiler_params=_params(("arbitrary", "arbitrary")),
        name="ffn_up_prompt",
    )(hn, w_up, w_up, conv_w, conv_w, conv_b, conv_b)


def _ffn_act_step_kernel(up_ref, prev_ref, cw_ref, cb_ref, act_ref):
    up = up_ref[...]
    w = 2 * D_FF
    c = cb_ref[...] + cw_ref[0:1, :] * prev_ref[:, 0:w] + cw_ref[1:2, :] * prev_ref[:, w:2 * w] + cw_ref[2:3, :] * up
    act_ref[...] = (_gelu(c[:, 0:D_FF]) * c[:, D_FF:w]).astype(act_ref.dtype)


def _ffn_act_step(up, conv_prev, conv_w, conv_b):
    b = up.shape[0]
    return pl.pallas_call(
        _ffn_act_step_kernel,
        out_shape=jax.ShapeDtypeStruct((b, D_FF), BF16),
        compiler_params=pltpu.CompilerParams(vmem_limit_bytes=VMEM_LIMIT),
        name="ffn_act_sample",
    )(up, conv_prev.reshape(b, (CONV_W - 1) * 2 * D_FF), conv_w, conv_b)


def _down_kernel(a_ref, w_ref, h_ref, g_ref, o_ref, acc_ref):
    @pl.when(pl.program_id(1) == 0)
    def _():
        acc_ref[...] = h_ref[...]

    out = acc_ref[...] + _dot_bf16(a_ref[...], w_ref[...])
    acc_ref[...] = out
    inv = lax.rsqrt(jnp.mean(out * out, axis=-1, keepdims=True) + EPS_RMS)
    o_ref[...] = out * inv * g_ref[...]


def _ffn_down(act, w_down, h, g, tm, tk, name):
    m = act.shape[0]
    nk = D_FF // tk
    return pl.pallas_call(
        _down_kernel,
        out_shape=jax.ShapeDtypeStruct((m, D_MODEL), F32),
        grid=(m // tm, nk),
        in_specs=[
            pl.BlockSpec((tm, tk), lambda i, k: (i, k)),
            pl.BlockSpec((tk, D_MODEL), lambda i, k: (k, 0)),
            pl.BlockSpec((tm, D_MODEL), lambda i, k: (i, 0)),
            pl.BlockSpec((1, D_MODEL), lambda i, k: (0, 0)),
        ],
        out_specs=pl.BlockSpec((tm, D_MODEL), lambda i, k: (i, 0)),
        scratch_shapes=[pltpu.VMEM((tm, D_MODEL), F32)],
        compiler_params=_params(("parallel", "arbitrary")),
        name=name,
    )(act, w_down, h, g.reshape(1, D_MODEL))


def _rel_bucket(dist):
    max_exact = N_BUCKETS // 2
    d_f = jnp.maximum(dist, 1).astype(F32)
    large = max_exact + (jnp.log(d_f / max_exact) / math.log(MAX_DISTANCE / max_exact)
                         * (N_BUCKETS - max_exact)).astype(jnp.int32)
    large = jnp.minimum(large, N_BUCKETS - 1)
    return jnp.where(dist < max_exact, dist, large)


def _bias_rows(tab, dist):
    onehot = (_rel_bucket(dist)[None, :] == jnp.arange(N_BUCKETS)[:, None]).astype(F32)
    return jnp.dot(tab.T, onehot, precision=HIGHEST)


def _bias_tables(rel_bias, gi):
    dil = DILATIONS[gi]
    win = WINDOWS[gi]
    reach = win // dil
    tab = rel_bias[:, gi * H_G:(gi + 1) * H_G]
    qi = jnp.arange(BLK)[:, None]
    ki = jnp.arange(2 * BLK)[None, :]
    rel = qi + BLK - ki
    blk = _bias_rows(tab, (dil * jnp.maximum(rel, 0)).reshape(-1)).reshape(H_G, BLK, 2 * BLK)
    blk = jnp.where(((rel >= 0) & (rel <= reach))[None], blk, -jnp.inf)
    back = win - jnp.arange(win)
    step = jnp.where((back % dil == 0)[None, :], _bias_rows(tab, back), -jnp.inf)[:, None, :]
    step0 = _bias_rows(tab, jnp.zeros((1,), jnp.int32))[:, None, :]
    return blk, step, step0


def _layer(x, tiles, lw, bias_blk, prompt, state=None):
    m = x.shape[0]
    xn = _rmsnorm(x, lw["norm1_g"], tiles["rms"], "rms1_" + tiles["tag"])
    proj = _matmul(xn, lw["w_in"], lw["in_bias"], tiles["proj_m"], PROJ_TN, COL_GATE, "proj_" + tiles["tag"])

    if prompt:
        rt, kt, qt, pt, v, bonus, g, gc = _rwkv_prep(proj, None, 128, CHUNK, lw, "rwkv_prep_prompt")
        o, wkv_new = _wkv_scan(pt, rt, qt, kt, v, gc)
        wkv_new = wkv_new[None, None]
    else:
        rt, kt, qt, pt, v, bonus, g, gc = _rwkv_prep(proj, state["shift"], m, 1, lw, "rwkv_prep_sample")
        wkv_new, o = _wkv_step(state["wkv"], pt, rt, qt, kt, v, gc)
    ya = _rwkv_post(o, bonus, g, lw, tiles["post"], "rwkv_post_" + tiles["tag"])

    outs, lses = [], []
    for gi in range(N_GROUPS):
        if prompt:
            o_g, l_g = _attn_prompt(proj, bias_blk[gi][0], gi, 2048)
        else:
            sl = lambda c0: proj[:, c0 + gi * D_G:c0 + (gi + 1) * D_G].reshape(m, H_G, 1, HEAD_B)
            o_g, l_g = _attn_step(sl(COL_QKV), sl(COL_QKV + D_B), sl(COL_QKV + 2 * D_B), state["win"][gi],
                                  bias_blk[gi][1], bias_blk[gi][2], gi)
        outs.append(o_g)
        lses.append(l_g)
    yb = _merge(outs, lses, tiles["merge"], "merge_" + tiles["tag"])

    mixed = _mix(ya, yb, lw["w_out_a"], lw["w_out_b"], proj, tiles["mix_m"], 256, "mix_" + tiles["tag"])
    h = _matmul_residual(mixed, lw["w_o"], x, tiles["wo_m"], 512, "wo_" + tiles["tag"])

    hn = _rmsnorm(h, lw["norm2_g"], tiles["rms"], "rms2_" + tiles["tag"])
    if prompt:
        act, t1, t2 = _ffn_up_prompt(hn, lw["w_up"], lw["conv_w"], lw["conv_b"], 1024, 512)
        conv_new = jnp.concatenate([t1[6:8], t2[6:8]], axis=1)[None, None]
    else:
        up = _matmul(hn, lw["w_up"], lw["up_bias"], m, 512, 2 * D_FF, "up_sample")
        act = _ffn_act_step(up, state["conv"], lw["conv_w"], lw["conv_b"])
        conv_new = jnp.concatenate([state["conv"][:, 1:], up[:, None, :]], axis=1)[None]
    y = _ffn_down(act, lw["w_down"], h, lw["normf_g"], tiles["down_m"], 2816, "down_" + tiles["tag"])
    return y, proj, wkv_new, conv_new


def kernel(x_prompt, x_sample, state_wkv, state_shift, state_ffn_conv, cache_win1, cache_win2, cache_win3, rel_bias,
           norm1_g, w_in, gate_b, mu_shift, w0, w_up_decay, a0, w_up_aaa, w_up_gate, k_k, k_a, r_k, gn_g, gn_b,
           w_out_a, w_out_b, w_o, norm2_g, w_up, conv_w, conv_b, w_down, normf_g):
    row = lambda a: a.reshape(1, -1)
    w_lora = jnp.zeros((D_LORA, 3 * D_A), F32)
    w_lora = w_lora.at[0:D_DECAY_LORA, 0:D_A].set(w_up_decay[0])
    w_lora = w_lora.at[D_DECAY_LORA:D_DECAY_LORA + D_AAA_LORA, D_A:2 * D_A].set(w_up_aaa[0])
    w_lora = w_lora.at[D_DECAY_LORA + D_AAA_LORA:, 2 * D_A:].set(w_up_gate[0])
    head = np.arange(D_A) // HEAD_A
    lw = dict(
        norm1_g=norm1_g[0], norm2_g=norm2_g[0], normf_g=normf_g,
        w_in=jnp.pad(w_in[0].astype(BF16), ((0, 0), (0, D_IN_PAD - D_IN))),
        in_bias=jnp.concatenate([jnp.zeros((1, COL_GATE), F32), row(gate_b[0]),
                                 jnp.zeros((1, D_IN_PAD - D_IN), F32)], axis=1),
        up_bias=jnp.zeros((1, 2 * D_FF), F32),
        mu=row(mu_shift[0]), w_lora=w_lora, w0=row(w0[0]), a0=row(a0[0]), k_k=row(k_k[0]), k_a=row(k_a[0]),
        r_k=row(r_k[0]), gn_g=row(gn_g[0]), gn_b=row(gn_b[0]),
        hsum=jnp.asarray(head[:, None] == np.arange(LANES)[None, :], BF16),
        hbc=jnp.asarray(np.arange(LANES)[:, None] == head[None, :], BF16),
        w_out_a=w_out_a[0].astype(BF16), w_out_b=w_out_b[0].astype(BF16), w_o=w_o[0].astype(BF16),
        w_up=w_up[0].astype(BF16), conv_w=conv_w[0], conv_b=row(conv_b[0]), w_down=w_down[0].astype(BF16),
    )
    bias = [_bias_tables(rel_bias, gi) for gi in range(N_GROUPS)]

    tiles_p = dict(tag="prompt", rms=512, proj_m=1024, post=256, merge=1024, mix_m=1024, wo_m=1024, down_m=512)
    y_p, proj_p, wkv_p, conv_p = _layer(x_prompt[0], tiles_p, lw, bias, prompt=True)

    b = DEC_BATCH
    tiles_s = dict(tag="sample", rms=b, proj_m=b, post=b, merge=b, mix_m=b, wo_m=b, down_m=b)
    state = dict(wkv=state_wkv, shift=state_shift.reshape(b, D_SHIFT), conv=state_ffn_conv[0],
                 win=(cache_win1, cache_win2, cache_win3))
    y_s, proj_s, wkv_s, conv_s = _layer(x_sample[:, 0], tiles_s, lw, bias, prompt=False, state=state)

    def kv_rows(proj, lo, gi):
        k = proj[lo:, COL_QKV + D_B + gi * D_G:COL_QKV + D_B + (gi + 1) * D_G]
        v = proj[lo:, COL_QKV + 2 * D_B + gi * D_G:COL_QKV + 2 * D_B + (gi + 1) * D_G]
        n = k.shape[0]
        return jnp.stack([k.reshape(n, H_G, HEAD_B), v.reshape(n, H_G, HEAD_B)], axis=1)

    win_p = [kv_rows(proj_p, SEQ - min(WINDOWS[gi], SEQ), gi)[None, None] for gi in range(N_GROUPS)]
    win_s = [kv_rows(proj_s, 0, gi)[None, :, None] for gi in range(N_GROUPS)]
    return (y_p[None], y_s[:, None],
            wkv_p, wkv_s,
            proj_p[SEQ - 1:, 0:D_SHIFT][None, None], proj_s[:, 0:D_SHIFT][None, :, None],
            conv_p, conv_s,
            win_p[0], win_s[0], win_p[1], win_s[1], win_p[2], win_s[2])
```

```python
import functools
import math

import numpy as np
import jax
import jax.numpy as jnp
from jax import lax
from jax.experimental import pallas as pl
from jax.experimental.pallas import tpu as pltpu

F32 = jnp.float32
BF16 = jnp.bfloat16
HIGHEST = lax.Precision.HIGHEST

D_MODEL = 2048
SEQ = 8192
DEC_BATCH = 32
HEAD_A = 64
H_A = 16
D_A = H_A * HEAD_A
D_DECAY_LORA = 96
D_AAA_LORA = 96
D_GATE_LORA = 64
D_LORA = D_DECAY_LORA + D_AAA_LORA + D_GATE_LORA
D_SHIFT = 3 * D_A + D_LORA
EPS_GN = 64e-5
HEAD_B = 64
H_G = 8
D_G = H_G * HEAD_B
WINDOWS = (128, 512, 2048)
DILATIONS = (1, 4, 16)
N_GROUPS = 3
D_B = N_GROUPS * D_G
BLK = 128
N_BUCKETS = 32
MAX_DISTANCE = 2048
COL_QKV = D_SHIFT
COL_GATE_SRC = D_SHIFT + 3 * D_B
D_IN = COL_GATE_SRC + 2 * D_MODEL
PROJ_TN = 1024
COL_GATE = -(-COL_GATE_SRC // PROJ_TN) * PROJ_TN
D_IN_PAD = COL_GATE + 2 * D_MODEL
D_FF = 5632
CONV_W = 3
EPS_RMS = 1e-6
CHUNK = 64
LANES = 128
VMEM_LIMIT = 56 * 1024 * 1024


def _params(sem, vmem=VMEM_LIMIT):
    return pltpu.CompilerParams(dimension_semantics=sem, vmem_limit_bytes=vmem)


def _sigmoid(x):
    return 1.0 / (1.0 + jnp.exp(-x))


def _dot_bf16(a, b):
    return jnp.dot(a, b, preferred_element_type=F32)


def _split2(x):
    hi = x.astype(BF16)
    return hi, (x - hi.astype(F32)).astype(BF16)


def _dot3(a, b):
    ah, al = _split2(a)
    bh, bl = _split2(b)
    return _dot_bf16(ah, bh) + _dot_bf16(ah, bl) + _dot_bf16(al, bh)


def _split3(x):
    hi = x.astype(BF16)
    r1 = x - hi.astype(F32)
    mid = r1.astype(BF16)
    lo = (r1 - mid.astype(F32)).astype(BF16)
    return hi, mid, lo


def _dot_exact_rhs(x, m):
    hi, mid, lo = _split3(x)
    return _dot_bf16(hi, m) + _dot_bf16(mid, m) + _dot_bf16(lo, m)


def _head_sums(x, hsum, hbc):
    return _dot_exact_rhs(_dot_exact_rhs(x, hsum), hbc)


def _dot_exact_lhs(m, x):
    hi, mid, lo = _split3(x)
    return _dot_bf16(m, hi) + _dot_bf16(m, mid) + _dot_bf16(m, lo)


def _rms(x, g):
    return x * lax.rsqrt(jnp.mean(x * x, axis=-1, keepdims=True) + EPS_RMS) * g


def _proj_kernel(x_ref, g_ref, w_ref, b_ref, o_ref, xn_scr):
    @pl.when(pl.program_id(1) == 0)
    def _():
        xn_scr[...] = _rms(x_ref[...], g_ref[...]).astype(BF16)

    acc = _dot_bf16(xn_scr[...], w_ref[...])
    tn = o_ref.shape[1]
    col = lax.broadcasted_iota(jnp.int32, (1, tn), 1) + pl.program_id(1) * tn
    o_ref[...] = jnp.where(col >= COL_GATE, _sigmoid(acc + b_ref[...]), acc)


def _proj(x, g, w, bias, tm, name):
    m, k = x.shape
    n = w.shape[1]
    return pl.pallas_call(
        _proj_kernel,
        out_shape=jax.ShapeDtypeStruct((m, n), F32),
        grid=(m // tm, n // PROJ_TN),
        in_specs=[
            pl.BlockSpec((tm, k), lambda i, j: (i, 0)),
            pl.BlockSpec((1, k), lambda i, j: (0, 0)),
            pl.BlockSpec((k, PROJ_TN), lambda i, j: (0, j)),
            pl.BlockSpec((1, PROJ_TN), lambda i, j: (0, j)),
        ],
        out_specs=pl.BlockSpec((tm, PROJ_TN), lambda i, j: (i, j)),
        scratch_shapes=[pltpu.VMEM((tm, k), BF16)],
        compiler_params=_params(("parallel", "arbitrary")),
        name=name,
    )(x, g.reshape(1, k), w, bias)


def _matmul_kernel(a_ref, w_ref, o_ref):
    o_ref[...] = _dot_bf16(a_ref[...], w_ref[...])


def _matmul(a, w, tm, tn, name):
    m, k = a.shape
    n = w.shape[1]
    return pl.pallas_call(
        _matmul_kernel,
        out_shape=jax.ShapeDtypeStruct((m, n), F32),
        grid=(m // tm, n // tn),
        in_specs=[pl.BlockSpec((tm, k), lambda i, j: (i, 0)), pl.BlockSpec((k, tn), lambda i, j: (0, j))],
        out_specs=pl.BlockSpec((tm, tn), lambda i, j: (i, j)),
        compiler_params=_params(("parallel", "arbitrary")),
        name=name,
    )(a, w)


def _prep_kernel(p_ref, prev_ref, mu_ref, wl_ref, w0_ref, a0_ref, kk_ref, ka_ref, rk_ref, lmat_ref, sel_ref,
                 hs_ref, hb_ref, rt_ref, kt_ref, qt_ref, pt_ref, v_ref, bonus_ref, g_ref, gc_ref, *, rows_are_time):
    p = p_ref[...]
    tm = p.shape[0]
    if rows_are_time:
        last = jnp.where(pl.program_id(0) == 0, 0.0, prev_ref[7:8, :])
        row = lax.broadcasted_iota(jnp.int32, (tm, 1), 0)
        prev = jnp.where(row == 0, last, pltpu.roll(p, 1, axis=0))
    else:
        prev = prev_ref[...]
    xm = p + mu_ref[...] * (prev - p)
    r = xm[:, 0:D_A]
    k = xm[:, D_A:2 * D_A]
    v = xm[:, 2 * D_A:3 * D_A]
    xl = xm[:, 3 * D_A:D_SHIFT]
    lane = lax.broadcasted_iota(jnp.int32, xl.shape, 1)
    act = jnp.where(lane < D_DECAY_LORA, jnp.tanh(xl),
                    jnp.where(lane < D_DECAY_LORA + D_AAA_LORA, xl, _sigmoid(xl)))
    lora = _dot3(act, wl_ref[...])
    y = -(w0_ref[...] + lora[:, 0:D_A])
    softplus = jnp.maximum(y, 0.0) + jnp.log(1.0 + jnp.exp(-jnp.abs(y)))
    logw = -jnp.exp(-softplus - 0.5)
    a = _sigmoid(a0_ref[...] + lora[:, D_A:2 * D_A])
    g = lora[:, 2 * D_A:3 * D_A]
    kkr = k * kk_ref[...]
    kp = k * (1.0 + (a - 1.0) * ka_ref[...])
    seg = _head_sums(jnp.concatenate([kkr * kkr, r * kp * rk_ref[...]], axis=0), hs_ref[...], hb_ref[...])
    kk = kkr / jnp.maximum(jnp.sqrt(seg[0:tm]), 1e-12)
    cum = _dot_exact_lhs(lmat_ref[...], logw)
    e_out = jnp.exp(-cum)
    rt_ref[...] = r * jnp.exp(cum)
    kt_ref[...] = kp * e_out
    qt_ref[...] = kk * a * e_out
    pt_ref[...] = -kk * jnp.exp(cum - logw)
    v_ref[...] = v
    bonus_ref[...] = seg[tm:2 * tm] * v
    g_ref[...] = g
    gc_ref[...] = jnp.exp(_dot_exact_lhs(sel_ref[...], cum))


def _rwkv_prep(proj, prev, tm, chunk, lw, name):
    m = proj.shape[0]
    rows_are_time = prev is None
    n_tiles = m // tm
    t = np.arange(tm)
    lmat = ((t[:, None] // chunk == t[None, :] // chunk) & (t[None, :] <= t[:, None])).astype(np.float32)
    sel_stride = 1 if chunk == 1 else 8
    n_sel = sel_stride * (tm // chunk)
    sel = np.zeros((n_sel, tm), np.float32)
    for c in range(tm // chunk):
        sel[sel_stride * c, (c + 1) * chunk - 1] = 1.0
    if rows_are_time:
        prev_arr = proj
        prev_spec = pl.BlockSpec((8, D_SHIFT), lambda i: (jnp.maximum(i * (tm // 8) - 1, 0), 0))
    else:
        prev_arr = prev
        prev_spec = pl.BlockSpec((tm, D_SHIFT), lambda i: (i, 0))
    vec = lambda d: pl.BlockSpec((1, d), lambda i: (0, 0))
    full = lambda a: pl.BlockSpec(a.shape, lambda i: (0, 0))
    big = pl.BlockSpec((tm, D_A), lambda i: (i, 0))
    out = jax.ShapeDtypeStruct((m, D_A), F32)
    lmat_b = jnp.asarray(lmat, BF16)
    sel_b = jnp.asarray(sel, BF16)
    return pl.pallas_call(
        functools.partial(_prep_kernel, rows_are_time=rows_are_time),
        out_shape=[out] * 7 + [jax.ShapeDtypeStruct((n_tiles * n_sel, D_A), F32)],
        grid=(n_tiles,),
        in_specs=[pl.BlockSpec((tm, D_SHIFT), lambda i: (i, 0)), prev_spec, vec(D_SHIFT), full(lw["w_lora"]),
                  vec(D_A), vec(D_A), vec(D_A), vec(D_A), vec(D_A), full(lmat_b), full(sel_b), full(lw["hsum"]),
                  full(lw["hbc"])],
        out_specs=[big] * 7 + [pl.BlockSpec((n_sel, D_A), lambda i: (i, 0))],
        compiler_params=_params(("parallel",)),
        name=name,
    )(proj, prev_arr, lw["mu"], lw["w_lora"], lw["w0"], lw["a0"], lw["k_k"], lw["k_a"], lw["r_k"], lmat_b, sel_b,
      lw["hsum"], lw["hbc"])


def _bdot3(a, b, dims):
    ah, al = _split2(a)
    bh, bl = _split2(b)
    dg = lambda x, y: lax.dot_general(x, y, dims, preferred_element_type=F32)
    return dg(ah, bh) + dg(ah, bl) + dg(al, bh)


_NN = (((2,), (1,)), ((0,), (0,)))
_NT = (((2,), (2,)), ((0,), (0,)))


def _scan_kernel(pt_ref, rt_ref, qt_ref, kt_ref, v_ref, gc_ref, o_ref, sfin_ref, s_scr):
    @pl.when(pl.program_id(0) == 0)
    def _():
        s_scr[...] = jnp.zeros_like(s_scr)

    def heads(x):
        return jnp.stack([x[:, h * HEAD_A:(h + 1) * HEAD_A] for h in range(H_A)], axis=0)

    p, r, q, k, v = (heads(ref[...]) for ref in (pt_ref, rt_ref, qt_ref, kt_ref, v_ref))
    gam = heads(gc_ref[0:1, :])
    ri = lax.broadcasted_iota(jnp.int32, (1, CHUNK, CHUNK), 1)
    ci = lax.broadcasted_iota(jnp.int32, (1, CHUNK, CHUNK), 2)
    strict = ri > ci
    incl = ri >= ci
    eye = (ri == ci).astype(F32)

    gram = _bdot3(jnp.concatenate([p, r], axis=1), jnp.concatenate([q, k], axis=1), _NT)
    a_qp = jnp.where(strict, gram[:, 0:CHUNK, 0:CHUNK], 0.0)
    a_kp = jnp.where(strict, gram[:, 0:CHUNK, CHUNK:], 0.0)
    a_rq = jnp.where(incl, gram[:, CHUNK:, 0:CHUNK], 0.0)
    a_rk = jnp.where(incl, gram[:, CHUNK:, CHUNK:], 0.0)
    same = lambda log2_bs: (ri >> log2_bs) == (ci >> log2_bs)
    a_d = jnp.where(same(4), a_qp, 0.0)
    tinv = eye + a_d
    pw = _bdot3(a_d, a_d, _NN)
    for _ in range(2):
        both = _bdot3(jnp.concatenate([pw, tinv], axis=1), pw, _NN)
        pw = both[:, 0:CHUNK]
        tinv = tinv + both[:, CHUNK:]
    tinv = tinv + _bdot3(tinv, pw, _NN)
    for log2_bs in (4, 5):
        off = jnp.where(jnp.logical_and(same(log2_bs + 1), jnp.logical_not(same(log2_bs))), a_qp, 0.0)
        tinv = tinv + _bdot3(tinv, _bdot3(off, tinv, _NN), _NN)
    pw_hat = _bdot3(tinv, jnp.concatenate([p, _bdot3(a_kp, v, _NN)], axis=2), _NN)
    x = _bdot3(a_rq, pw_hat, _NN)
    r_hat = r + x[:, :, 0:HEAD_A]
    o_loc = x[:, :, HEAD_A:] + _bdot3(a_rk, v, _NN)
    y = _bdot3(jnp.swapaxes(pw_hat, 1, 2), q, _NN)
    m_mat = (eye + y[:, 0:HEAD_A]) * gam
    n_mat = (y[:, HEAD_A:] + _bdot3(jnp.swapaxes(v, 1, 2), k, _NN)) * gam

    s0 = s_scr[...]
    o = _bdot3(r_hat, s0, _NT) + o_loc
    s_scr[...] = _bdot3(s0, m_mat, _NN) + n_mat
    o_ref[...] = jnp.concatenate([o[h] for h in range(H_A)], axis=1)

    @pl.when(pl.program_id(0) == pl.num_programs(0) - 1)
    def _():
        sfin_ref[...] = s_scr[...]


def _wkv_scan(pt, rt, qt, kt, v, gc):
    t = pt.shape[0]
    blk = pl.BlockSpec((CHUNK, D_A), lambda c: (c, 0))
    return pl.pallas_call(
        _scan_kernel,
        out_shape=[jax.ShapeDtypeStruct((t, D_A), F32), jax.ShapeDtypeStruct((H_A, HEAD_A, HEAD_A), F32)],
        grid=(t // CHUNK,),
        in_specs=[blk] * 5 + [pl.BlockSpec((8, D_A), lambda c: (c, 0))],
        out_specs=[blk, pl.BlockSpec((H_A, HEAD_A, HEAD_A), lambda c: (0, 0, 0))],
        scratch_shapes=[pltpu.VMEM((H_A, HEAD_A, HEAD_A), F32)],
        compiler_params=_params(("arbitrary",)),
        name="wkv_scan_prompt",
    )(pt, rt, qt, kt, v, gc)


def _wkv_step_kernel(s_ref, rows_ref, sn_ref, o_ref):
    ri = lax.broadcasted_iota(jnp.int32, (1, HEAD_A, HEAD_A), 1)
    ci = lax.broadcasted_iota(jnp.int32, (1, HEAD_A, HEAD_A), 2)
    eye = ri == ci
    pt, rt, qt, kt, v, gc = (rows_ref[0, n] for n in range(6))
    s = s_ref[0, 0]
    u = jnp.sum(s * pt, axis=2, keepdims=True)
    v_col = jnp.sum(jnp.where(eye, v, 0.0), axis=2, keepdims=True)
    m = s + u * qt + v_col * kt
    sn_ref[0, 0] = m * gc
    o_col = jnp.sum(m * rt, axis=2, keepdims=True)
    o_ref[0] = jnp.sum(jnp.where(eye, o_col, 0.0), axis=1, keepdims=True)


def _wkv_step(state, pt, rt, qt, kt, v, gc):
    b = pt.shape[0]
    rows = jnp.stack([pt, rt, qt, kt, v, gc], axis=1).reshape(b, 6, H_A, 1, HEAD_A)
    st_spec = pl.BlockSpec((1, 1, H_A, HEAD_A, HEAD_A), lambda i: (0, i, 0, 0, 0))
    sn, o = pl.pallas_call(
        _wkv_step_kernel,
        out_shape=[jax.ShapeDtypeStruct(state.shape, F32), jax.ShapeDtypeStruct((b, H_A, 1, HEAD_A), F32)],
        grid=(b,),
        in_specs=[st_spec, pl.BlockSpec((1, 6, H_A, 1, HEAD_A), lambda i: (i, 0, 0, 0, 0))],
        out_specs=[st_spec, pl.BlockSpec((1, H_A, 1, HEAD_A), lambda i: (i, 0, 0, 0))],
        compiler_params=_params(("parallel",)),
        name="wkv_step_sample",
    )(state, rows)
    return sn, o.reshape(b, D_A)


def _post_kernel(o_ref, bonus_ref, g_ref, gng_ref, gnb_ref, hs_ref, hb_ref, y_ref):
    o = o_ref[...]
    hsum, hbc = hs_ref[...], hb_ref[...]
    mu = _head_sums(o, hsum, hbc) * (1.0 / HEAD_A)
    d = o - mu
    var = _head_sums(d * d, hsum, hbc) * (1.0 / HEAD_A)
    o_n = d * lax.rsqrt(var + EPS_GN) * gng_ref[...] + gnb_ref[...]
    y_ref[...] = ((o_n + bonus_ref[...]) * g_ref[...]).astype(y_ref.dtype)


def _rwkv_post(o, bonus, g, lw, tm, name):
    m = o.shape[0]
    big = pl.BlockSpec((tm, D_A), lambda i: (i, 0))
    vec = pl.BlockSpec((1, D_A), lambda i: (0, 0))
    return pl.pallas_call(
        _post_kernel,
        out_shape=jax.ShapeDtypeStruct((m, D_A), BF16),
        grid=(m // tm,),
        in_specs=[big, big, big, vec, vec, pl.BlockSpec((D_A, LANES), lambda i: (0, 0)),
                  pl.BlockSpec((LANES, D_A), lambda i: (0, 0))],
        out_specs=big,
        compiler_params=_params(("parallel",)),
        name=name,
    )(o, bonus, g, lw["gn_g"], lw["gn_b"], lw["hsum"], lw["hbc"])


def _attn_kernel(q_ref, kc_ref, vc_ref, kh_ref, vh_ref, bias_ref, o_ref, l_ref, *, dil, m_blocks):
    scale = HEAD_B ** -0.5
    n_units = dil * m_blocks

    def rows(ref, start, size):
        if dil == 1:
            return ref[pl.ds(start, size), :]
        return ref[pl.ds(start, size, stride=dil), :]

    qs, ks, vs, bases = [], [], [], []
    for r in range(dil):
        for mb in range(m_blocks):
            base = r + dil * BLK * mb
            bases.append(base)
            qs.append(rows(q_ref, base, BLK))
            if mb == 0:
                ks.append(jnp.concatenate([rows(kh_ref, r, BLK), rows(kc_ref, r, BLK)], axis=0))
                vs.append(jnp.concatenate([rows(vh_ref, r, BLK), rows(vc_ref, r, BLK)], axis=0))
            else:
                ks.append(rows(kc_ref, base - dil * BLK, 2 * BLK))
                vs.append(rows(vc_ref, base - dil * BLK, 2 * BLK))

    def batch(xs):
        return jnp.stack([x[:, sub * HEAD_B:(sub + 1) * HEAD_B].astype(BF16) for sub in range(2) for x in xs], axis=0)

    qb, kb, vb = batch(qs), batch(ks), batch(vs)
    s = lax.dot_general(qb, kb, (((2,), (2,)), ((0,), (0,))), preferred_element_type=F32) * scale
    s = s.reshape(2, n_units, BLK, 2 * BLK) + bias_ref[...][:, None]
    unit = lax.broadcasted_iota(jnp.int32, (1, n_units, 1, 2 * BLK), 1)
    col = lax.broadcasted_iota(jnp.int32, (1, n_units, 1, 2 * BLK), 3)
    no_prev = jnp.logical_and(jnp.logical_and(pl.program_id(0) == 0, (unit & (m_blocks - 1)) == 0), col < BLK)
    s = jnp.where(no_prev, -jnp.inf, s)
    m = jnp.max(s, axis=-1, keepdims=True)
    p = jnp.exp(s - m)
    l = jnp.sum(p, axis=-1, keepdims=True)
    pv = lax.dot_general(p.astype(BF16).reshape(2 * n_units, BLK, 2 * BLK), vb, (((2,), (1,)), ((0,), (0,))),
                         preferred_element_type=F32)
    o = pv.reshape(2, n_units, BLK, HEAD_B) / l
    lse = jnp.broadcast_to(m + jnp.log(l), (2, n_units, BLK, HEAD_B))
    for u, base in enumerate(bases):
        o_blk = jnp.concatenate([o[0, u], o[1, u]], axis=1)
        l_blk = jnp.concatenate([lse[0, u], lse[1, u]], axis=1)
        if dil == 1:
            o_ref[pl.ds(base, BLK), :] = o_blk
            l_ref[pl.ds(base, BLK), :] = l_blk
        else:
            o_ref[pl.ds(base, BLK, stride=dil), :] = o_blk
            l_ref[pl.ds(base, BLK, stride=dil), :] = l_blk


def _attn_prompt(proj, bias, gi, rows_per_step):
    t = proj.shape[0]
    dil = DILATIONS[gi]
    span = BLK * dil
    m_blocks = rows_per_step // span
    cq = (COL_QKV + gi * D_G) // LANES
    ck = (COL_QKV + D_B + gi * D_G) // LANES
    cv = (COL_QKV + 2 * D_B + gi * D_G) // LANES
    cur = lambda c0: pl.BlockSpec((rows_per_step, LANES), lambda i, hp: (i, c0 + hp))
    halo = lambda c0: pl.BlockSpec((span, LANES), lambda i, hp: (jnp.maximum(i * m_blocks - 1, 0), c0 + hp))
    out_spec = pl.BlockSpec((rows_per_step, LANES), lambda i, hp: (i, hp))
    out = jax.ShapeDtypeStruct((t, D_G), F32)
    return pl.pallas_call(
        functools.partial(_attn_kernel, dil=dil, m_blocks=m_blocks),
        out_shape=[out, out],
        grid=(t // rows_per_step, H_G // 2),
        in_specs=[cur(cq), cur(ck), cur(cv), halo(ck), halo(cv),
                  pl.BlockSpec((2, BLK, 2 * BLK), lambda i, hp: (hp, 0, 0))],
        out_specs=[out_spec, out_spec],
        compiler_params=_params(("parallel", "parallel")),
        name=f"attn_prompt_g{gi}",
    )(proj, proj, proj, proj, proj, bias)


def _attn_step_kernel(q_ref, kn_ref, vn_ref, c_ref, bias_ref, bias0_ref, o_ref, l_ref):
    scale = HEAD_B ** -0.5
    ri = lax.broadcasted_iota(jnp.int32, (1, HEAD_B, HEAD_B), 1)
    ci = lax.broadcasted_iota(jnp.int32, (1, HEAD_B, HEAD_B), 2)
    eye = ri == ci
    q = q_ref[0]
    q_col = jnp.sum(jnp.where(eye, q, 0.0), axis=2, keepdims=True)
    s = jnp.sum(c_ref[0] * q_col, axis=1, keepdims=True) * scale + bias_ref[...]
    s_new = jnp.sum(q * kn_ref[0], axis=2, keepdims=True) * scale + bias0_ref[...]
    m = jnp.maximum(jnp.max(s, axis=2, keepdims=True), s_new)
    p = jnp.exp(s - m)
    p_new = jnp.exp(s_new - m)
    l = jnp.sum(p, axis=2, keepdims=True) + p_new
    o_col = jnp.sum(c_ref[1] * p, axis=2, keepdims=True)
    o_row = jnp.sum(jnp.where(eye, o_col, 0.0), axis=1, keepdims=True)
    o_ref[0] = (o_row + p_new * vn_ref[0]) / l
    l_ref[0] = jnp.broadcast_to(m + jnp.log(l), (H_G, 1, HEAD_B))


def _attn_step(q, k_new, v_new, cache, bias, bias0, gi):
    b = q.shape[0]
    w = cache.shape[2]
    cache_t = jnp.transpose(cache, (0, 1, 3, 4, 5, 2))
    vec = pl.BlockSpec((1, H_G, 1, HEAD_B), lambda i: (i, 0, 0, 0))
    out = jax.ShapeDtypeStruct((b, H_G, 1, HEAD_B), F32)
    o, l = pl.pallas_call(
        _attn_step_kernel,
        out_shape=[out, out],
        grid=(b,),
        in_specs=[vec, vec, vec,
                  pl.BlockSpec((None, None, 2, H_G, HEAD_B, w), lambda i: (0, i, 0, 0, 0, 0)),
                  pl.BlockSpec((H_G, 1, w), lambda i: (0, 0, 0)),
                  pl.BlockSpec((H_G, 1, 1), lambda i: (0, 0, 0))],
        out_specs=[vec, vec],
        compiler_params=_params(("parallel",)),
        name=f"attn_step_g{gi}",
    )(q, k_new, v_new, cache_t, bias, bias0)
    return o.reshape(b, D_G), l.reshape(b, D_G)


def _mix_kernel(ya_ref, o1_ref, o2_ref, o3_ref, l1_ref, l2_ref, l3_ref, wa_ref, wb_ref, ga_ref, gb_ref, o_ref,
                yb_scr):
    @pl.when(pl.program_id(1) == 0)
    def _():
        l1, l2, l3 = l1_ref[...], l2_ref[...], l3_ref[...]
        m = jnp.maximum(jnp.maximum(l1, l2), l3)
        e1, e2, e3 = jnp.exp(l1 - m), jnp.exp(l2 - m), jnp.exp(l3 - m)
        den = e1 + e2 + e3
        yb = (e1 / den) * o1_ref[...] + (e2 / den) * o2_ref[...] + (e3 / den) * o3_ref[...]
        yb_scr[...] = yb.astype(BF16)

    mixed = ga_ref[...] * _dot_bf16(ya_ref[...], wa_ref[...]) + gb_ref[...] * _dot_bf16(yb_scr[...], wb_ref[...])
    o_ref[...] = mixed.astype(o_ref.dtype)


def _mix(ya, outs, lses, wa, wb, proj, tm, tn, name):
    m = ya.shape[0]
    ga0 = COL_GATE // tn
    gb0 = (COL_GATE + D_MODEL) // tn
    grp = pl.BlockSpec((tm, D_G), lambda i, j: (i, 0))
    return pl.pallas_call(
        _mix_kernel,
        out_shape=jax.ShapeDtypeStruct((m, D_MODEL), BF16),
        grid=(m // tm, D_MODEL // tn),
        in_specs=[pl.BlockSpec((tm, D_A), lambda i, j: (i, 0))] + [grp] * 6 + [
            pl.BlockSpec((D_A, tn), lambda i, j: (0, j)),
            pl.BlockSpec((D_G, tn), lambda i, j: (0, j)),
            pl.BlockSpec((tm, tn), lambda i, j: (i, ga0 + j)),
            pl.BlockSpec((tm, tn), lambda i, j: (i, gb0 + j)),
        ],
        out_specs=pl.BlockSpec((tm, tn), lambda i, j: (i, j)),
        scratch_shapes=[pltpu.VMEM((tm, D_G), BF16)],
        compiler_params=_params(("parallel", "arbitrary")),
        name=name,
    )(ya, *outs, *lses, wa, wb, proj, proj)


def _wo_kernel(a_ref, w_ref, x_ref, g_ref, h_ref, hn_ref):
    h = x_ref[...] + _dot_bf16(a_ref[...], w_ref[...])
    h_ref[...] = h
    hn_ref[...] = _rms(h, g_ref[...]).astype(BF16)


def _wo(a, w, x, g, tm, name):
    m, k = a.shape
    n = w.shape[1]
    row = pl.BlockSpec((tm, n), lambda i: (i, 0))
    return pl.pallas_call(
        _wo_kernel,
        out_shape=[jax.ShapeDtypeStruct((m, n), F32), jax.ShapeDtypeStruct((m, n), BF16)],
        grid=(m // tm,),
        in_specs=[pl.BlockSpec((tm, k), lambda i: (i, 0)), pl.BlockSpec((k, n), lambda i: (0, 0)), row,
                  pl.BlockSpec((1, n), lambda i: (0, 0))],
        out_specs=[row, row],
        compiler_params=_params(("parallel",)),
        name=name,
    )(a, w, x, g.reshape(1, n))


def _gelu(x):
    return 0.5 * x * (1.0 + lax.erf(x * (1.0 / math.sqrt(2.0))))


def _ffn_up_kernel(h_ref, w1_ref, w2_ref, cw1_ref, cw2_ref, cb1_ref, cb2_ref, act_ref, t1_ref, t2_ref,
                   c1_scr, c2_scr):
    tm = h_ref.shape[0]

    @pl.when(pl.program_id(1) == 0)
    def _():
        c1_scr[...] = jnp.zeros_like(c1_scr)
        c2_scr[...] = jnp.zeros_like(c2_scr)

    a = h_ref[...]
    row = lax.broadcasted_iota(jnp.int32, (tm, 1), 0)

    def conv(u, carry_ref, cw_ref, cb_ref):
        m1 = jnp.where(row == 0, carry_ref[7:8, :], pltpu.roll(u, 1, axis=0))
        m2 = jnp.where(row == 0, carry_ref[6:7, :], jnp.where(row == 1, carry_ref[7:8, :], pltpu.roll(u, 2, axis=0)))
        return cb_ref[...] + cw_ref[0:1, :] * m2 + cw_ref[1:2, :] * m1 + cw_ref[2:3, :] * u

    u1 = _dot_bf16(a, w1_ref[...])
    u2 = _dot_bf16(a, w2_ref[...])
    c1 = conv(u1, c1_scr, cw1_ref, cb1_ref)
    c2 = conv(u2, c2_scr, cw2_ref, cb2_ref)
    act_ref[...] = (_gelu(c1) * c2).astype(act_ref.dtype)
    c1_scr[...] = u1[tm - 8:tm]
    c2_scr[...] = u2[tm - 8:tm]
    t1_ref[...] = u1[tm - 8:tm]
    t2_ref[...] = u2[tm - 8:tm]


def _ffn_up_prompt(hn, w_up, conv_w, conv_b, tm, tn):
    m = hn.shape[0]
    nj = D_FF // tn
    tail = jax.ShapeDtypeStruct((8, D_FF), F32)
    return pl.pallas_call(
        _ffn_up_kernel,
        out_shape=[jax.ShapeDtypeStruct((m, D_FF), BF16), tail, tail],
        grid=(nj, m // tm),
        in_specs=[
            pl.BlockSpec((tm, D_MODEL), lambda j, i: (i, 0)),
            pl.BlockSpec((D_MODEL, tn), lambda j, i: (0, j)),
            pl.BlockSpec((D_MODEL, tn), lambda j, i: (0, nj + j)),
            pl.BlockSpec((CONV_W, tn), lambda j, i: (0, j)),
            pl.BlockSpec((CONV_W, tn), lambda j, i: (0, nj + j)),
            pl.BlockSpec((1, tn), lambda j, i: (0, j)),
            pl.BlockSpec((1, tn), lambda j, i: (0, nj + j)),
        ],
        out_specs=[pl.BlockSpec((tm, tn), lambda j, i: (i, j)),
                   pl.BlockSpec((8, tn), lambda j, i: (0, j)),
                   pl.BlockSpec((8, tn), lambda j, i: (0, j))],
        scratch_shapes=[pltpu.VMEM((8, tn), F32), pltpu.VMEM((8, tn), F32)],
        compiler_params=_params(("arbitrary", "arbitrary")),
        name="ffn_up_prompt",
    )(hn, w_up, w_up, conv_w, conv_w, conv_b, conv_b)


def _ffn_act_step_kernel(up_ref, prev_ref, cw_ref, cb_ref, act_ref):
    up = up_ref[...]
    w = 2 * D_FF
    c = cb_ref[...] + cw_ref[0:1, :] * prev_ref[:, 0:w] + cw_ref[1:2, :] * prev_ref[:, w:2 * w] + cw_ref[2:3, :] * up
    act_ref[...] = (_gelu(c[:, 0:D_FF]) * c[:, D_FF:w]).astype(act_ref.dtype)


def _ffn_act_step(up, conv_prev, conv_w, conv_b):
    b = up.shape[0]
    return pl.pallas_call(
        _ffn_act_step_kernel,
        out_shape=jax.ShapeDtypeStruct((b, D_FF), BF16),
        compiler_params=pltpu.CompilerParams(vmem_limit_bytes=VMEM_LIMIT),
        name="ffn_act_sample",
    )(up, conv_prev.reshape(b, (CONV_W - 1) * 2 * D_FF), conv_w, conv_b)


def _down_kernel(a_ref, w_ref, h_ref, g_ref, o_ref, acc_ref):
    @pl.when(pl.program_id(1) == 0)
    def _():
        acc_ref[...] = h_ref[...]

    out = acc_ref[...] + _dot_bf16(a_ref[...], w_ref[...])
    acc_ref[...] = out
    inv = lax.rsqrt(jnp.mean(out * out, axis=-1, keepdims=True) + EPS_RMS)
    o_ref[...] = out * inv * g_ref[...]


def _ffn_down(act, w_down, h, g, tm, tk, name):
    m = act.shape[0]
    nk = D_FF // tk
    return pl.pallas_call(
        _down_kernel,
        out_shape=jax.ShapeDtypeStruct((m, D_MODEL), F32),
        grid=(m // tm, nk),
        in_specs=[
            pl.BlockSpec((tm, tk), lambda i, k: (i, k)),
            pl.BlockSpec((tk, D_MODEL), lambda i, k: (k, 0)),
            pl.BlockSpec((tm, D_MODEL), lambda i, k: (i, 0)),
            pl.BlockSpec((1, D_MODEL), lambda i, k: (0, 0)),
        ],
        out_specs=pl.BlockSpec((tm, D_MODEL), lambda i, k: (i, 0)),
        scratch_shapes=[pltpu.VMEM((tm, D_MODEL), F32)],
        compiler_params=_params(("parallel", "arbitrary")),
        name=name,
    )(act, w_down, h, g.reshape(1, D_MODEL))


def _rel_bucket(dist):
    max_exact = N_BUCKETS // 2
    d_f = jnp.maximum(dist, 1).astype(F32)
    large = max_exact + (jnp.log(d_f / max_exact) / math.log(MAX_DISTANCE / max_exact)
                         * (N_BUCKETS - max_exact)).astype(jnp.int32)
    large = jnp.minimum(large, N_BUCKETS - 1)
    return jnp.where(dist < max_exact, dist, large)


def _bias_rows(tab, dist):
    onehot = (_rel_bucket(dist)[None, :] == jnp.arange(N_BUCKETS)[:, None]).astype(F32)
    return jnp.dot(tab.T, onehot, precision=HIGHEST)


def _bias_tables(rel_bias, gi):
    dil = DILATIONS[gi]
    win = WINDOWS[gi]
    reach = win // dil
    tab = rel_bias[:, gi * H_G:(gi + 1) * H_G]
    qi = jnp.arange(BLK)[:, None]
    ki = jnp.arange(2 * BLK)[None, :]
    rel = qi + BLK - ki
    blk = _bias_rows(tab, (dil * jnp.maximum(rel, 0)).reshape(-1)).reshape(H_G, BLK, 2 * BLK)
    blk = jnp.where(((rel >= 0) & (rel <= reach))[None], blk, -jnp.inf)
    back = win - jnp.arange(win)
    step = jnp.where((back % dil == 0)[None, :], _bias_rows(tab, back), -jnp.inf)[:, None, :]
    step0 = _bias_rows(tab, jnp.zeros((1,), jnp.int32))[:, None, :]
    return blk, step, step0


def _layer(x, tiles, lw, bias_blk, prompt, state=None):
    m = x.shape[0]
    proj = _proj(x, lw["norm1_g"], lw["w_in"], lw["in_bias"], tiles["proj_m"], "proj_" + tiles["tag"])

    if prompt:
        rt, kt, qt, pt, v, bonus, g, gc = _rwkv_prep(proj, None, 128, CHUNK, lw, "rwkv_prep_prompt")
        o, wkv_new = _wkv_scan(pt, rt, qt, kt, v, gc)
        wkv_new = wkv_new[None, None]
    else:
        rt, kt, qt, pt, v, bonus, g, gc = _rwkv_prep(proj, state["shift"], m, 1, lw, "rwkv_prep_sample")
        wkv_new, o = _wkv_step(state["wkv"], pt, rt, qt, kt, v, gc)
    ya = _rwkv_post(o, bonus, g, lw, tiles["post"], "rwkv_post_" + tiles["tag"])

    outs, lses = [], []
    for gi in range(N_GROUPS):
        if prompt:
            o_g, l_g = _attn_prompt(proj, bias_blk[gi][0], gi, 2048)
        else:
            sl = lambda c0: proj[:, c0 + gi * D_G:c0 + (gi + 1) * D_G].reshape(m, H_G, 1, HEAD_B)
            o_g, l_g = _attn_step(sl(COL_QKV), sl(COL_QKV + D_B), sl(COL_QKV + 2 * D_B), state["win"][gi],
                                  bias_blk[gi][1], bias_blk[gi][2], gi)
        outs.append(o_g)
        lses.append(l_g)
    mixed = _mix(ya, outs, lses, lw["w_out_a"], lw["w_out_b"], proj, tiles["mix_m"], 1024, "mix_" + tiles["tag"])
    h, hn = _wo(mixed, lw["w_o"], x, lw["norm2_g"], tiles["wo_m"], "wo_" + tiles["tag"])

    if prompt:
        act, t1, t2 = _ffn_up_prompt(hn, lw["w_up"], lw["conv_w"], lw["conv_b"], 1024, 512)
        conv_new = jnp.concatenate([t1[6:8], t2[6:8]], axis=1)[None, None]
    else:
        up = _matmul(hn, lw["w_up"], m, 512, "up_sample")
        act = _ffn_act_step(up, state["conv"], lw["conv_w"], lw["conv_b"])
        conv_new = jnp.concatenate([state["conv"][:, 1:], up[:, None, :]], axis=1)[None]
    y = _ffn_down(act, lw["w_down"], h, lw["normf_g"], tiles["down_m"], 2816, "down_" + tiles["tag"])
    return y, proj, wkv_new, conv_new


def kernel(x_prompt, x_sample, state_wkv, state_shift, state_ffn_conv, cache_win1, cache_win2, cache_win3, rel_bias,
           norm1_g, w_in, gate_b, mu_shift, w0, w_up_decay, a0, w_up_aaa, w_up_gate, k_k, k_a, r_k, gn_g, gn_b,
           w_out_a, w_out_b, w_o, norm2_g, w_up, conv_w, conv_b, w_down, normf_g):
    row = lambda a: a.reshape(1, -1)
    w_lora = jnp.zeros((D_LORA, 3 * D_A), F32)
    w_lora = w_lora.at[0:D_DECAY_LORA, 0:D_A].set(w_up_decay[0])
    w_lora = w_lora.at[D_DECAY_LORA:D_DECAY_LORA + D_AAA_LORA, D_A:2 * D_A].set(w_up_aaa[0])
    w_lora = w_lora.at[D_DECAY_LORA + D_AAA_LORA:, 2 * D_A:].set(w_up_gate[0])
    head = np.arange(D_A) // HEAD_A
    lw = dict(
        norm1_g=norm1_g[0], norm2_g=norm2_g[0], normf_g=normf_g,
        w_in=jnp.concatenate([w_in[0][:, 0:COL_GATE_SRC], jnp.zeros((D_MODEL, COL_GATE - COL_GATE_SRC), F32),
                              w_in[0][:, COL_GATE_SRC:]], axis=1).astype(BF16),
        in_bias=jnp.concatenate([jnp.zeros((1, COL_GATE), F32), row(gate_b[0])], axis=1),
        mu=row(mu_shift[0]), w_lora=w_lora, w0=row(w0[0]), a0=row(a0[0]), k_k=row(k_k[0]), k_a=row(k_a[0]),
        r_k=row(r_k[0]), gn_g=row(gn_g[0]), gn_b=row(gn_b[0]),
        hsum=jnp.asarray(head[:, None] == np.arange(LANES)[None, :], BF16),
        hbc=jnp.asarray(np.arange(LANES)[:, None] == head[None, :], BF16),
        w_out_a=w_out_a[0].astype(BF16), w_out_b=w_out_b[0].astype(BF16), w_o=w_o[0].astype(BF16),
        w_up=w_up[0].astype(BF16), conv_w=conv_w[0], conv_b=row(conv_b[0]), w_down=w_down[0].astype(BF16),
    )
    bias = [_bias_tables(rel_bias, gi) for gi in range(N_GROUPS)]

    tiles_p = dict(tag="prompt", proj_m=1024, post=256, mix_m=512, wo_m=512, down_m=512)
    y_p, proj_p, wkv_p, conv_p = _layer(x_prompt[0], tiles_p, lw, bias, prompt=True)

    b = DEC_BATCH
    tiles_s = dict(tag="sample", proj_m=b, post=b, mix_m=b, wo_m=b, down_m=b)
    state = dict(wkv=state_wkv, shift=state_shift.reshape(b, D_SHIFT), conv=state_ffn_conv[0],
                 win=(cache_win1, cache_win2, cache_win3))
    y_s, proj_s, wkv_s, conv_s = _layer(x_sample[:, 0], tiles_s, lw, bias, prompt=False, state=state)

    def kv_rows(proj, lo, gi):
        k = proj[lo:, COL_QKV + D_B + gi * D_G:COL_QKV + D_B + (gi + 1) * D_G]
        v = proj[lo:, COL_QKV + 2 * D_B + gi * D_G:COL_QKV + 2 * D_B + (gi + 1) * D_G]
        n = k.shape[0]
        return jnp.stack([k.reshape(n, H_G, HEAD_B), v.reshape(n, H_G, HEAD_B)], axis=1)

    win_p = [kv_rows(proj_p, SEQ - min(WINDOWS[gi], SEQ), gi)[None, None] for gi in range(N_GROUPS)]
    win_s = [kv_rows(proj_s, 0, gi)[None, :, None] for gi in range(N_GROUPS)]
    return (y_p[None], y_s[:, None],
            wkv_p, wkv_s,
            proj_p[SEQ - 1:, 0:D_SHIFT][None, None], proj_s[:, 0:D_SHIFT][None, :, None],
            conv_p, conv_s,
            win_p[0], win_s[0], win_p[1], win_s[1], win_p[2], win_s[2])
```

```python
import functools
import math

import numpy as np
import jax
import jax.numpy as jnp
from jax import lax
from jax.experimental import pallas as pl
from jax.experimental.pallas import tpu as pltpu

F32 = jnp.float32
BF16 = jnp.bfloat16
HIGHEST = lax.Precision.HIGHEST

D_MODEL = 2048
SEQ = 8192
DEC_BATCH = 32
HEAD_A = 64
H_A = 16
D_A = H_A * HEAD_A
D_DECAY_LORA = 96
D_AAA_LORA = 96
D_GATE_LORA = 64
D_LORA = D_DECAY_LORA + D_AAA_LORA + D_GATE_LORA
D_SHIFT = 3 * D_A + D_LORA
EPS_GN = 64e-5
HEAD_B = 64
H_G = 8
D_G = H_G * HEAD_B
WINDOWS = (128, 512, 2048)
DILATIONS = (1, 4, 16)
N_GROUPS = 3
D_B = N_GROUPS * D_G
BLK = 128
N_BUCKETS = 32
MAX_DISTANCE = 2048
COL_QKV = D_SHIFT
COL_GATE_SRC = D_SHIFT + 3 * D_B
D_IN = COL_GATE_SRC + 2 * D_MODEL
PROJ_TN = 1024
COL_GATE = -(-COL_GATE_SRC // PROJ_TN) * PROJ_TN
D_IN_PAD = COL_GATE + 2 * D_MODEL
D_FF = 5632
CONV_W = 3
EPS_RMS = 1e-6
CHUNK = 64
LANES = 128
VMEM_LIMIT = 56 * 1024 * 1024


def _params(sem, vmem=VMEM_LIMIT):
    return pltpu.CompilerParams(dimension_semantics=sem, vmem_limit_bytes=vmem)


def _sigmoid(x):
    return 1.0 / (1.0 + jnp.exp(-x))


def _dot_bf16(a, b):
    return jnp.dot(a, b, preferred_element_type=F32)


def _split2(x):
    hi = x.astype(BF16)
    return hi, (x - hi.astype(F32)).astype(BF16)


def _dot3(a, b):
    ah, al = _split2(a)
    bh, bl = _split2(b)
    return _dot_bf16(ah, bh) + _dot_bf16(ah, bl) + _dot_bf16(al, bh)


def _split3(x):
    hi = x.astype(BF16)
    r1 = x - hi.astype(F32)
    mid = r1.astype(BF16)
    lo = (r1 - mid.astype(F32)).astype(BF16)
    return hi, mid, lo


def _dot_exact_rhs(x, m):
    hi, mid, lo = _split3(x)
    return _dot_bf16(hi, m) + _dot_bf16(mid, m) + _dot_bf16(lo, m)


def _head_sums(x, hsum, hbc):
    return _dot_exact_rhs(_dot_exact_rhs(x, hsum), hbc)


def _dot_exact_lhs(m, x):
    hi, mid, lo = _split3(x)
    return _dot_bf16(m, hi) + _dot_bf16(m, mid) + _dot_bf16(m, lo)


def _rms(x, g):
    return x * lax.rsqrt(jnp.mean(x * x, axis=-1, keepdims=True) + EPS_RMS) * g


def _cast_w_in_kernel(w_ref, o_ref, *, gap_block):
    keep = pl.program_id(0) != gap_block
    o_ref[...] = jnp.where(keep, w_ref[...], 0.0).astype(BF16)


def _cast_w_in(w):
    k = w.shape[0]
    tn = COL_GATE - COL_GATE_SRC
    assert tn % LANES == 0 and COL_GATE_SRC % tn == 0
    gap_block = COL_GATE_SRC // tn
    return pl.pallas_call(
        functools.partial(_cast_w_in_kernel, gap_block=gap_block),
        out_shape=jax.ShapeDtypeStruct((k, D_IN_PAD), BF16),
        grid=(D_IN_PAD // tn,),
        in_specs=[pl.BlockSpec((k, tn), lambda j: (0, jnp.where(j < gap_block, j, jnp.maximum(j - 1, 0))))],
        out_specs=pl.BlockSpec((k, tn), lambda j: (0, j)),
        compiler_params=_params(("parallel",)),
        name="cast_w_in",
    )(w)


def _proj_kernel(x_ref, g_ref, w_ref, b_ref, o_ref, xn_scr):
    @pl.when(pl.program_id(1) == 0)
    def _():
        xn_scr[...] = _rms(x_ref[...], g_ref[...]).astype(BF16)

    acc = _dot_bf16(xn_scr[...], w_ref[...])
    tn = o_ref.shape[1]
    col = lax.broadcasted_iota(jnp.int32, (1, tn), 1) + pl.program_id(1) * tn
    o_ref[...] = jnp.where(col >= COL_GATE, _sigmoid(acc + b_ref[...]), acc)


def _proj(x, g, w, bias, tm, name):
    m, k = x.shape
    n = w.shape[1]
    return pl.pallas_call(
        _proj_kernel,
        out_shape=jax.ShapeDtypeStruct((m, n), F32),
        grid=(m // tm, n // PROJ_TN),
        in_specs=[
            pl.BlockSpec((tm, k), lambda i, j: (i, 0)),
            pl.BlockSpec((1, k), lambda i, j: (0, 0)),
            pl.BlockSpec((k, PROJ_TN), lambda i, j: (0, j)),
            pl.BlockSpec((1, PROJ_TN), lambda i, j: (0, j)),
        ],
        out_specs=pl.BlockSpec((tm, PROJ_TN), lambda i, j: (i, j)),
        scratch_shapes=[pltpu.VMEM((tm, k), BF16)],
        compiler_params=_params(("parallel", "arbitrary")),
        name=name,
    )(x, g.reshape(1, k), w, bias)


def _matmul_kernel(a_ref, w_ref, o_ref):
    o_ref[...] = _dot_bf16(a_ref[...], w_ref[...])


def _matmul(a, w, tm, tn, name):
    m, k = a.shape
    n = w.shape[1]
    return pl.pallas_call(
        _matmul_kernel,
        out_shape=jax.ShapeDtypeStruct((m, n), F32),
        grid=(m // tm, n // tn),
        in_specs=[pl.BlockSpec((tm, k), lambda i, j: (i, 0)), pl.BlockSpec((k, tn), lambda i, j: (0, j))],
        out_specs=pl.BlockSpec((tm, tn), lambda i, j: (i, j)),
        compiler_params=_params(("parallel", "arbitrary")),
        name=name,
    )(a, w)


def _prep_kernel(p_ref, prev_ref, mu_ref, wl_ref, w0_ref, a0_ref, kk_ref, ka_ref, rk_ref, lmat_ref, sel_ref,
                 hs_ref, hb_ref, rt_ref, kt_ref, qt_ref, pt_ref, v_ref, bonus_ref, g_ref, gc_ref, *, rows_are_time):
    p = p_ref[...]
    tm = p.shape[0]
    if rows_are_time:
        last = jnp.where(pl.program_id(0) == 0, 0.0, prev_ref[7:8, :])
        row = lax.broadcasted_iota(jnp.int32, (tm, 1), 0)
        prev = jnp.where(row == 0, last, pltpu.roll(p, 1, axis=0))
    else:
        prev = prev_ref[...]
    xm = p + mu_ref[...] * (prev - p)
    r = xm[:, 0:D_A]
    k = xm[:, D_A:2 * D_A]
    v = xm[:, 2 * D_A:3 * D_A]
    xl = xm[:, 3 * D_A:D_SHIFT]
    lane = lax.broadcasted_iota(jnp.int32, xl.shape, 1)
    act = jnp.where(lane < D_DECAY_LORA, jnp.tanh(xl),
                    jnp.where(lane < D_DECAY_LORA + D_AAA_LORA, xl, _sigmoid(xl)))
    lora = _dot3(act, wl_ref[...])
    y = -(w0_ref[...] + lora[:, 0:D_A])
    softplus = jnp.maximum(y, 0.0) + jnp.log(1.0 + jnp.exp(-jnp.abs(y)))
    logw = -jnp.exp(-softplus - 0.5)
    a = _sigmoid(a0_ref[...] + lora[:, D_A:2 * D_A])
    g = lora[:, 2 * D_A:3 * D_A]
    kkr = k * kk_ref[...]
    kp = k * (1.0 + (a - 1.0) * ka_ref[...])
    seg = _head_sums(jnp.concatenate([kkr * kkr, r * kp * rk_ref[...]], axis=0), hs_ref[...], hb_ref[...])
    kk = kkr / jnp.maximum(jnp.sqrt(seg[0:tm]), 1e-12)
    cum = _dot_exact_lhs(lmat_ref[...], logw)
    e_out = jnp.exp(-cum)
    rt_ref[...] = r * jnp.exp(cum)
    kt_ref[...] = kp * e_out
    qt_ref[...] = kk * a * e_out
    pt_ref[...] = -kk * jnp.exp(cum - logw)
    v_ref[...] = v
    bonus_ref[...] = seg[tm:2 * tm] * v
    g_ref[...] = g
    gc_ref[...] = jnp.exp(_dot_exact_lhs(sel_ref[...], cum))


def _rwkv_prep(proj, prev, tm, chunk, lw, name):
    m = proj.shape[0]
    rows_are_time = prev is None
    n_tiles = m // tm
    t = np.arange(tm)
    lmat = ((t[:, None] // chunk == t[None, :] // chunk) & (t[None, :] <= t[:, None])).astype(np.float32)
    sel_stride = 1 if chunk == 1 else 8
    n_sel = sel_stride * (tm // chunk)
    sel = np.zeros((n_sel, tm), np.float32)
    for c in range(tm // chunk):
        sel[sel_stride * c, (c + 1) * chunk - 1] = 1.0
    if rows_are_time:
        prev_arr = proj
        prev_spec = pl.BlockSpec((8, D_SHIFT), lambda i: (jnp.maximum(i * (tm // 8) - 1, 0), 0))
    else:
        prev_arr = prev
        prev_spec = pl.BlockSpec((tm, D_SHIFT), lambda i: (i, 0))
    vec = lambda d: pl.BlockSpec((1, d), lambda i: (0, 0))
    full = lambda a: pl.BlockSpec(a.shape, lambda i: (0, 0))
    big = pl.BlockSpec((tm, D_A), lambda i: (i, 0))
    out = jax.ShapeDtypeStruct((m, D_A), F32)
    lmat_b = jnp.asarray(lmat, BF16)
    sel_b = jnp.asarray(sel, BF16)
    return pl.pallas_call(
        functools.partial(_prep_kernel, rows_are_time=rows_are_time),
        out_shape=[out] * 7 + [jax.ShapeDtypeStruct((n_tiles * n_sel, D_A), F32)],
        grid=(n_tiles,),
        in_specs=[pl.BlockSpec((tm, D_SHIFT), lambda i: (i, 0)), prev_spec, vec(D_SHIFT), full(lw["w_lora"]),
                  vec(D_A), vec(D_A), vec(D_A), vec(D_A), vec(D_A), full(lmat_b), full(sel_b), full(lw["hsum"]),
                  full(lw["hbc"])],
        out_specs=[big] * 7 + [pl.BlockSpec((n_sel, D_A), lambda i: (i, 0))],
        compiler_params=_params(("parallel",)),
        name=name,
    )(proj, prev_arr, lw["mu"], lw["w_lora"], lw["w0"], lw["a0"], lw["k_k"], lw["k_a"], lw["r_k"], lmat_b, sel_b,
      lw["hsum"], lw["hbc"])


def _bdot3(a, b, dims):
    ah, al = _split2(a)
    bh, bl = _split2(b)
    (ca,), (cb,) = dims[0]
    a_cat = jnp.concatenate([ah, ah, al, jnp.zeros_like(al)], axis=ca)
    b_cat = jnp.concatenate([bh, bl, bh, jnp.zeros_like(bl)], axis=cb)
    return lax.dot_general(a_cat, b_cat, dims, preferred_element_type=F32)


_NN = (((2,), (1,)), ((0,), (0,)))
_NT = (((2,), (2,)), ((0,), (0,)))


def _scan_kernel(pt_ref, rt_ref, qt_ref, kt_ref, v_ref, gc_ref, o_ref, sfin_ref, s_scr):
    @pl.when(pl.program_id(0) == 0)
    def _():
        s_scr[...] = jnp.zeros_like(s_scr)

    def heads(x):
        return jnp.stack([x[:, h * HEAD_A:(h + 1) * HEAD_A] for h in range(H_A)], axis=0)

    p, r, q, k, v = (heads(ref[...]) for ref in (pt_ref, rt_ref, qt_ref, kt_ref, v_ref))
    gam = heads(gc_ref[0:1, :])
    ri = lax.broadcasted_iota(jnp.int32, (1, CHUNK, CHUNK), 1)
    ci = lax.broadcasted_iota(jnp.int32, (1, CHUNK, CHUNK), 2)
    strict = ri > ci
    incl = ri >= ci
    eye = (ri == ci).astype(F32)

    gram = _bdot3(jnp.concatenate([p, r], axis=1), jnp.concatenate([q, k], axis=1), _NT)
    a_qp = jnp.where(strict, gram[:, 0:CHUNK, 0:CHUNK], 0.0)
    a_kp = jnp.where(strict, gram[:, 0:CHUNK, CHUNK:], 0.0)
    a_rq = jnp.where(incl, gram[:, CHUNK:, 0:CHUNK], 0.0)
    a_rk = jnp.where(incl, gram[:, CHUNK:, CHUNK:], 0.0)
    same = lambda log2_bs: (ri >> log2_bs) == (ci >> log2_bs)
    a_d = jnp.where(same(4), a_qp, 0.0)
    tinv = eye + a_d
    pw = _bdot3(a_d, a_d, _NN)
    for _ in range(2):
        both = _bdot3(jnp.concatenate([pw, tinv], axis=1), pw, _NN)
        pw = both[:, 0:CHUNK]
        tinv = tinv + both[:, CHUNK:]
    tinv = tinv + _bdot3(tinv, pw, _NN)
    for log2_bs in (4, 5):
        off = jnp.where(jnp.logical_and(same(log2_bs + 1), jnp.logical_not(same(log2_bs))), a_qp, 0.0)
        tinv = tinv + _bdot3(tinv, _bdot3(off, tinv, _NN), _NN)
    av = _bdot3(jnp.concatenate([a_kp, a_rk], axis=1), v, _NN)
    pw_hat = _bdot3(tinv, jnp.concatenate([p, av[:, 0:CHUNK]], axis=2), _NN)
    x = _bdot3(a_rq, pw_hat, _NN)
    r_hat = r + x[:, :, 0:HEAD_A]
    o_loc = x[:, :, HEAD_A:] + av[:, CHUNK:]
    y = _bdot3(jnp.swapaxes(pw_hat, 1, 2), q, _NN)
    m_mat = (eye + y[:, 0:HEAD_A]) * gam
    n_mat = (y[:, HEAD_A:] + _bdot3(jnp.swapaxes(v, 1, 2), k, _NN)) * gam

    s0 = s_scr[...]
    o = _bdot3(r_hat, s0, _NT) + o_loc
    s_scr[...] = _bdot3(s0, m_mat, _NN) + n_mat
    o_ref[...] = jnp.concatenate([o[h] for h in range(H_A)], axis=1)

    @pl.when(pl.program_id(0) == pl.num_programs(0) - 1)
    def _():
        sfin_ref[...] = s_scr[...]


def _wkv_scan(pt, rt, qt, kt, v, gc):
    t = pt.shape[0]
    blk = pl.BlockSpec((CHUNK, D_A), lambda c: (c, 0))
    return pl.pallas_call(
        _scan_kernel,
        out_shape=[jax.ShapeDtypeStruct((t, D_A), F32), jax.ShapeDtypeStruct((H_A, HEAD_A, HEAD_A), F32)],
        grid=(t // CHUNK,),
        in_specs=[blk] * 5 + [pl.BlockSpec((8, D_A), lambda c: (c, 0))],
        out_specs=[blk, pl.BlockSpec((H_A, HEAD_A, HEAD_A), lambda c: (0, 0, 0))],
        scratch_shapes=[pltpu.VMEM((H_A, HEAD_A, HEAD_A), F32)],
        compiler_params=_params(("arbitrary",)),
        name="wkv_scan_prompt",
    )(pt, rt, qt, kt, v, gc)


def _wkv_step_kernel(s_ref, rows_ref, sn_ref, o_ref):
    ri = lax.broadcasted_iota(jnp.int32, (1, HEAD_A, HEAD_A), 1)
    ci = lax.broadcasted_iota(jnp.int32, (1, HEAD_A, HEAD_A), 2)
    eye = ri == ci
    pt, rt, qt, kt, v, gc = (rows_ref[0, n] for n in range(6))
    s = s_ref[0, 0]
    u = jnp.sum(s * pt, axis=2, keepdims=True)
    v_col = jnp.sum(jnp.where(eye, v, 0.0), axis=2, keepdims=True)
    m = s + u * qt + v_col * kt
    sn_ref[0, 0] = m * gc
    o_col = jnp.sum(m * rt, axis=2, keepdims=True)
    o_ref[0] = jnp.sum(jnp.where(eye, o_col, 0.0), axis=1, keepdims=True)


def _wkv_step(state, pt, rt, qt, kt, v, gc):
    b = pt.shape[0]
    rows = jnp.stack([pt, rt, qt, kt, v, gc], axis=1).reshape(b, 6, H_A, 1, HEAD_A)
    st_spec = pl.BlockSpec((1, 1, H_A, HEAD_A, HEAD_A), lambda i: (0, i, 0, 0, 0))
    sn, o = pl.pallas_call(
        _wkv_step_kernel,
        out_shape=[jax.ShapeDtypeStruct(state.shape, F32), jax.ShapeDtypeStruct((b, H_A, 1, HEAD_A), F32)],
        grid=(b,),
        in_specs=[st_spec, pl.BlockSpec((1, 6, H_A, 1, HEAD_A), lambda i: (i, 0, 0, 0, 0))],
        out_specs=[st_spec, pl.BlockSpec((1, H_A, 1, HEAD_A), lambda i: (i, 0, 0, 0))],
        compiler_params=_params(("parallel",)),
        name="wkv_step_sample",
    )(state, rows)
    return sn, o.reshape(b, D_A)


def _post_kernel(o_ref, bonus_ref, g_ref, gng_ref, gnb_ref, hs_ref, hb_ref, y_ref):
    o = o_ref[...]
    hsum, hbc = hs_ref[...], hb_ref[...]
    mu = _head_sums(o, hsum, hbc) * (1.0 / HEAD_A)
    d = o - mu
    var = _head_sums(d * d, hsum, hbc) * (1.0 / HEAD_A)
    o_n = d * lax.rsqrt(var + EPS_GN) * gng_ref[...] + gnb_ref[...]
    y_ref[...] = ((o_n + bonus_ref[...]) * g_ref[...]).astype(y_ref.dtype)


def _rwkv_post(o, bonus, g, lw, tm, name):
    m = o.shape[0]
    big = pl.BlockSpec((tm, D_A), lambda i: (i, 0))
    vec = pl.BlockSpec((1, D_A), lambda i: (0, 0))
    return pl.pallas_call(
        _post_kernel,
        out_shape=jax.ShapeDtypeStruct((m, D_A), BF16),
        grid=(m // tm,),
        in_specs=[big, big, big, vec, vec, pl.BlockSpec((D_A, LANES), lambda i: (0, 0)),
                  pl.BlockSpec((LANES, D_A), lambda i: (0, 0))],
        out_specs=big,
        compiler_params=_params(("parallel",)),
        name=name,
    )(o, bonus, g, lw["gn_g"], lw["gn_b"], lw["hsum"], lw["hbc"])


def _attn_kernel(q_ref, kc_ref, vc_ref, kh_ref, vh_ref, bias_ref, o_ref, l_ref, *, dil, m_blocks):
    scale = HEAD_B ** -0.5
    n_units = dil * m_blocks

    def rows(ref, start, size):
        if dil == 1:
            return ref[pl.ds(start, size), :]
        return ref[pl.ds(start, size, stride=dil), :]

    qs, ks, vs, bases = [], [], [], []
    for r in range(dil):
        for mb in range(m_blocks):
            base = r + dil * BLK * mb
            bases.append(base)
            qs.append(rows(q_ref, base, BLK))
            if mb == 0:
                ks.append(jnp.concatenate([rows(kh_ref, r, BLK), rows(kc_ref, r, BLK)], axis=0))
                vs.append(jnp.concatenate([rows(vh_ref, r, BLK), rows(vc_ref, r, BLK)], axis=0))
            else:
                ks.append(rows(kc_ref, base - dil * BLK, 2 * BLK))
                vs.append(rows(vc_ref, base - dil * BLK, 2 * BLK))

    def batch(xs):
        return jnp.stack([x[:, sub * HEAD_B:(sub + 1) * HEAD_B].astype(BF16) for sub in range(2) for x in xs], axis=0)

    qb, kb, vb = batch(qs), batch(ks), batch(vs)
    s = lax.dot_general(qb, kb, (((2,), (2,)), ((0,), (0,))), preferred_element_type=F32) * scale
    s = s.reshape(2, n_units, BLK, 2 * BLK) + bias_ref[...][:, None]
    unit = lax.broadcasted_iota(jnp.int32, (1, n_units, 1, 2 * BLK), 1)
    col = lax.broadcasted_iota(jnp.int32, (1, n_units, 1, 2 * BLK), 3)
    no_prev = jnp.logical_and(jnp.logical_and(pl.program_id(0) == 0, (unit & (m_blocks - 1)) == 0), col < BLK)
    s = jnp.where(no_prev, -jnp.inf, s)
    m = jnp.max(s, axis=-1, keepdims=True)
    p = jnp.exp(s - m)
    l = jnp.sum(p, axis=-1, keepdims=True)
    pv = lax.dot_general(p.astype(BF16).reshape(2 * n_units, BLK, 2 * BLK), vb, (((2,), (1,)), ((0,), (0,))),
                         preferred_element_type=F32)
    o = pv.reshape(2, n_units, BLK, HEAD_B) / l
    lse = jnp.broadcast_to(m + jnp.log(l), (2, n_units, BLK, HEAD_B))
    for u, base in enumerate(bases):
        o_blk = jnp.concatenate([o[0, u], o[1, u]], axis=1)
        l_blk = jnp.concatenate([lse[0, u], lse[1, u]], axis=1)
        if dil == 1:
            o_ref[pl.ds(base, BLK), :] = o_blk
            l_ref[pl.ds(base, BLK), :] = l_blk
        else:
            o_ref[pl.ds(base, BLK, stride=dil), :] = o_blk
            l_ref[pl.ds(base, BLK, stride=dil), :] = l_blk


def _attn_prompt(proj, bias, gi, rows_per_step):
    t = proj.shape[0]
    dil = DILATIONS[gi]
    span = BLK * dil
    m_blocks = rows_per_step // span
    cq = (COL_QKV + gi * D_G) // LANES
    ck = (COL_QKV + D_B + gi * D_G) // LANES
    cv = (COL_QKV + 2 * D_B + gi * D_G) // LANES
    cur = lambda c0: pl.BlockSpec((rows_per_step, LANES), lambda i, hp: (i, c0 + hp))
    halo = lambda c0: pl.BlockSpec((span, LANES), lambda i, hp: (jnp.maximum(i * m_blocks - 1, 0), c0 + hp))
    out_spec = pl.BlockSpec((rows_per_step, LANES), lambda i, hp: (i, hp))
    out = jax.ShapeDtypeStruct((t, D_G), F32)
    return pl.pallas_call(
        functools.partial(_attn_kernel, dil=dil, m_blocks=m_blocks),
        out_shape=[out, out],
        grid=(t // rows_per_step, H_G // 2),
        in_specs=[cur(cq), cur(ck), cur(cv), halo(ck), halo(cv),
                  pl.BlockSpec((2, BLK, 2 * BLK), lambda i, hp: (hp, 0, 0))],
        out_specs=[out_spec, out_spec],
        compiler_params=_params(("parallel", "parallel")),
        name=f"attn_prompt_g{gi}",
    )(proj, proj, proj, proj, proj, bias)


def _attn_step_kernel(q_ref, kn_ref, vn_ref, c_ref, bias_ref, bias0_ref, o_ref, l_ref):
    scale = HEAD_B ** -0.5
    ri = lax.broadcasted_iota(jnp.int32, (1, HEAD_B, HEAD_B), 1)
    ci = lax.broadcasted_iota(jnp.int32, (1, HEAD_B, HEAD_B), 2)
    eye = ri == ci
    q = q_ref[0]
    q_col = jnp.sum(jnp.where(eye, q, 0.0), axis=2, keepdims=True)
    s = jnp.sum(c_ref[0] * q_col, axis=1, keepdims=True) * scale + bias_ref[...]
    s_new = jnp.sum(q * kn_ref[0], axis=2, keepdims=True) * scale + bias0_ref[...]
    m = jnp.maximum(jnp.max(s, axis=2, keepdims=True), s_new)
    p = jnp.exp(s - m)
    p_new = jnp.exp(s_new - m)
    l = jnp.sum(p, axis=2, keepdims=True) + p_new
    o_col = jnp.sum(c_ref[1] * p, axis=2, keepdims=True)
    o_row = jnp.sum(jnp.where(eye, o_col, 0.0), axis=1, keepdims=True)
    o_ref[0] = (o_row + p_new * vn_ref[0]) / l
    l_ref[0] = jnp.broadcast_to(m + jnp.log(l), (H_G, 1, HEAD_B))


def _attn_step(q, k_new, v_new, cache, bias, bias0, gi):
    b = q.shape[0]
    w = cache.shape[2]
    cache_t = jnp.transpose(cache, (0, 1, 3, 4, 5, 2))
    vec = pl.BlockSpec((1, H_G, 1, HEAD_B), lambda i: (i, 0, 0, 0))
    out = jax.ShapeDtypeStruct((b, H_G, 1, HEAD_B), F32)
    o, l = pl.pallas_call(
        _attn_step_kernel,
        out_shape=[out, out],
        grid=(b,),
        in_specs=[vec, vec, vec,
                  pl.BlockSpec((None, None, 2, H_G, HEAD_B, w), lambda i: (0, i, 0, 0, 0, 0)),
                  pl.BlockSpec((H_G, 1, w), lambda i: (0, 0, 0)),
                  pl.BlockSpec((H_G, 1, 1), lambda i: (0, 0, 0))],
        out_specs=[vec, vec],
        compiler_params=_params(("parallel",)),
        name=f"attn_step_g{gi}",
    )(q, k_new, v_new, cache_t, bias, bias0)
    return o.reshape(b, D_G), l.reshape(b, D_G)


def _mix_kernel(ya_ref, o1_ref, o2_ref, o3_ref, l1_ref, l2_ref, l3_ref, wa_ref, wb_ref, ga_ref, gb_ref, o_ref,
                yb_scr):
    @pl.when(pl.program_id(1) == 0)
    def _():
        l1, l2, l3 = l1_ref[...], l2_ref[...], l3_ref[...]
        m = jnp.maximum(jnp.maximum(l1, l2), l3)
        e1, e2, e3 = jnp.exp(l1 - m), jnp.exp(l2 - m), jnp.exp(l3 - m)
        den = e1 + e2 + e3
        yb = (e1 / den) * o1_ref[...] + (e2 / den) * o2_ref[...] + (e3 / den) * o3_ref[...]
        yb_scr[...] = yb.astype(BF16)

    mixed = ga_ref[...] * _dot_bf16(ya_ref[...], wa_ref[...]) + gb_ref[...] * _dot_bf16(yb_scr[...], wb_ref[...])
    o_ref[...] = mixed.astype(o_ref.dtype)


def _mix(ya, outs, lses, wa, wb, proj, tm, tn, name):
    m = ya.shape[0]
    ga0 = COL_GATE // tn
    gb0 = (COL_GATE + D_MODEL) // tn
    grp = pl.BlockSpec((tm, D_G), lambda i, j: (i, 0))
    return pl.pallas_call(
        _mix_kernel,
        out_shape=jax.ShapeDtypeStruct((m, D_MODEL), BF16),
        grid=(m // tm, D_MODEL // tn),
        in_specs=[pl.BlockSpec((tm, D_A), lambda i, j: (i, 0))] + [grp] * 6 + [
            pl.BlockSpec((D_A, tn), lambda i, j: (0, j)),
            pl.BlockSpec((D_G, tn), lambda i, j: (0, j)),
            pl.BlockSpec((tm, tn), lambda i, j: (i, ga0 + j)),
            pl.BlockSpec((tm, tn), lambda i, j: (i, gb0 + j)),
        ],
        out_specs=pl.BlockSpec((tm, tn), lambda i, j: (i, j)),
        scratch_shapes=[pltpu.VMEM((tm, D_G), BF16)],
        compiler_params=_params(("parallel", "arbitrary")),
        name=name,
    )(ya, *outs, *lses, wa, wb, proj, proj)


def _wo_kernel(a_ref, w_ref, x_ref, g_ref, h_ref, hn_ref):
    h = x_ref[...] + _dot_bf16(a_ref[...], w_ref[...])
    h_ref[...] = h
    hn_ref[...] = _rms(h, g_ref[...]).astype(BF16)


def _wo(a, w, x, g, tm, name):
    m, k = a.shape
    n = w.shape[1]
    row = pl.BlockSpec((tm, n), lambda i: (i, 0))
    return pl.pallas_call(
        _wo_kernel,
        out_shape=[jax.ShapeDtypeStruct((m, n), F32), jax.ShapeDtypeStruct((m, n), BF16)],
        grid=(m // tm,),
        in_specs=[pl.BlockSpec((tm, k), lambda i: (i, 0)), pl.BlockSpec((k, n), lambda i: (0, 0)), row,
                  pl.BlockSpec((1, n), lambda i: (0, 0))],
        out_specs=[row, row],
        compiler_params=_params(("parallel",)),
        name=name,
    )(a, w, x, g.reshape(1, n))


def _gelu(x):
    return 0.5 * x * (1.0 + lax.erf(x * (1.0 / math.sqrt(2.0))))


def _ffn_up_kernel(h_ref, w1_ref, w2_ref, cw1_ref, cw2_ref, cb1_ref, cb2_ref, act_ref, t1_ref, t2_ref,
                   c1_scr, c2_scr):
    tm = h_ref.shape[0]

    @pl.when(pl.program_id(1) == 0)
    def _():
        c1_scr[...] = jnp.zeros_like(c1_scr)
        c2_scr[...] = jnp.zeros_like(c2_scr)

    a = h_ref[...]
    row = lax.broadcasted_iota(jnp.int32, (tm, 1), 0)

    def conv(u, carry_ref, cw_ref, cb_ref):
        m1 = jnp.where(row == 0, carry_ref[7:8, :], pltpu.roll(u, 1, axis=0))
        m2 = jnp.where(row == 0, carry_ref[6:7, :], jnp.where(row == 1, carry_ref[7:8, :], pltpu.roll(u, 2, axis=0)))
        return cb_ref[...] + cw_ref[0:1, :] * m2 + cw_ref[1:2, :] * m1 + cw_ref[2:3, :] * u

    u1 = _dot_bf16(a, w1_ref[...])
    u2 = _dot_bf16(a, w2_ref[...])
    c1 = conv(u1, c1_scr, cw1_ref, cb1_ref)
    c2 = conv(u2, c2_scr, cw2_ref, cb2_ref)
    act_ref[...] = (_gelu(c1) * c2).astype(act_ref.dtype)
    c1_scr[...] = u1[tm - 8:tm]
    c2_scr[...] = u2[tm - 8:tm]
    t1_ref[...] = u1[tm - 8:tm]
    t2_ref[...] = u2[tm - 8:tm]


def _ffn_up_prompt(hn, w_up, conv_w, conv_b, tm, tn):
    m = hn.shape[0]
    nj = D_FF // tn
    tail = jax.ShapeDtypeStruct((8, D_FF), F32)
    return pl.pallas_call(
        _ffn_up_kernel,
        out_shape=[jax.ShapeDtypeStruct((m, D_FF), BF16), tail, tail],
        grid=(nj, m // tm),
        in_specs=[
            pl.BlockSpec((tm, D_MODEL), lambda j, i: (i, 0)),
            pl.BlockSpec((D_MODEL, tn), lambda j, i: (0, j)),
            pl.BlockSpec((D_MODEL, tn), lambda j, i: (0, nj + j)),
            pl.BlockSpec((CONV_W, tn), lambda j, i: (0, j)),
            pl.BlockSpec((CONV_W, tn), lambda j, i: (0, nj + j)),
            pl.BlockSpec((1, tn), lambda j, i: (0, j)),
            pl.BlockSpec((1, tn), lambda j, i: (0, nj + j)),
        ],
        out_specs=[pl.BlockSpec((tm, tn), lambda j, i: (i, j)),
                   pl.BlockSpec((8, tn), lambda j, i: (0, j)),
                   pl.BlockSpec((8, tn), lambda j, i: (0, j))],
        scratch_shapes=[pltpu.VMEM((8, tn), F32), pltpu.VMEM((8, tn), F32)],
        compiler_params=_params(("arbitrary", "arbitrary")),
        name="ffn_up_prompt",
    )(hn, w_up, w_up, conv_w, conv_w, conv_b, conv_b)


def _ffn_act_step_kernel(up_ref, prev_ref, cw_ref, cb_ref, act_ref):
    up = up_ref[...]
    w = 2 * D_FF
    c = cb_ref[...] + cw_ref[0:1, :] * prev_ref[:, 0:w] + cw_ref[1:2, :] * prev_ref[:, w:2 * w] + cw_ref[2:3, :] * up
    act_ref[...] = (_gelu(c[:, 0:D_FF]) * c[:, D_FF:w]).astype(act_ref.dtype)


def _ffn_act_step(up, conv_prev, conv_w, conv_b):
    b = up.shape[0]
    return pl.pallas_call(
        _ffn_act_step_kernel,
        out_shape=jax.ShapeDtypeStruct((b, D_FF), BF16),
        compiler_params=pltpu.CompilerParams(vmem_limit_bytes=VMEM_LIMIT),
        name="ffn_act_sample",
    )(up, conv_prev.reshape(b, (CONV_W - 1) * 2 * D_FF), conv_w, conv_b)


def _down_kernel(a_ref, w_ref, h_ref, g_ref, o_ref, acc_ref):
    @pl.when(pl.program_id(1) == 0)
    def _():
        acc_ref[...] = h_ref[...]

    out = acc_ref[...] + _dot_bf16(a_ref[...], w_ref[...])
    acc_ref[...] = out
    inv = lax.rsqrt(jnp.mean(out * out, axis=-1, keepdims=True) + EPS_RMS)
    o_ref[...] = out * inv * g_ref[...]


def _ffn_down(act, w_down, h, g, tm, tk, name):
    m = act.shape[0]
    nk = D_FF // tk
    return pl.pallas_call(
        _down_kernel,
        out_shape=jax.ShapeDtypeStruct((m, D_MODEL), F32),
        grid=(m // tm, nk),
        in_specs=[
            pl.BlockSpec((tm, tk), lambda i, k: (i, k)),
            pl.BlockSpec((tk, D_MODEL), lambda i, k: (k, 0)),
            pl.BlockSpec((tm, D_MODEL), lambda i, k: (i, 0)),
            pl.BlockSpec((1, D_MODEL), lambda i, k: (0, 0)),
        ],
        out_specs=pl.BlockSpec((tm, D_MODEL), lambda i, k: (i, 0)),
        scratch_shapes=[pltpu.VMEM((tm, D_MODEL), F32)],
        compiler_params=_params(("parallel", "arbitrary")),
        name=name,
    )(act, w_down, h, g.reshape(1, D_MODEL))


def _rel_bucket(dist):
    max_exact = N_BUCKETS // 2
    d_f = jnp.maximum(dist, 1).astype(F32)
    large = max_exact + (jnp.log(d_f / max_exact) / math.log(MAX_DISTANCE / max_exact)
                         * (N_BUCKETS - max_exact)).astype(jnp.int32)
    large = jnp.minimum(large, N_BUCKETS - 1)
    return jnp.where(dist < max_exact, dist, large)


def _bias_rows(tab, dist):
    onehot = (_rel_bucket(dist)[None, :] == jnp.arange(N_BUCKETS)[:, None]).astype(F32)
    return jnp.dot(tab.T, onehot, precision=HIGHEST)


def _bias_tables(rel_bias, gi):
    dil = DILATIONS[gi]
    win = WINDOWS[gi]
    reach = win // dil
    tab = rel_bias[:, gi * H_G:(gi + 1) * H_G]
    qi = jnp.arange(BLK)[:, None]
    ki = jnp.arange(2 * BLK)[None, :]
    rel = qi + BLK - ki
    blk = _bias_rows(tab, (dil * jnp.maximum(rel, 0)).reshape(-1)).reshape(H_G, BLK, 2 * BLK)
    blk = jnp.where(((rel >= 0) & (rel <= reach))[None], blk, -jnp.inf)
    back = win - jnp.arange(win)
    step = jnp.where((back % dil == 0)[None, :], _bias_rows(tab, back), -jnp.inf)[:, None, :]
    step0 = _bias_rows(tab, jnp.zeros((1,), jnp.int32))[:, None, :]
    return blk, step, step0


def _layer(x, tiles, lw, bias_blk, prompt, state=None):
    m = x.shape[0]
    proj = _proj(x, lw["norm1_g"], lw["w_in"], lw["in_bias"], tiles["proj_m"], "proj_" + tiles["tag"])

    if prompt:
        rt, kt, qt, pt, v, bonus, g, gc = _rwkv_prep(proj, None, 128, CHUNK, lw, "rwkv_prep_prompt")
        o, wkv_new = _wkv_scan(pt, rt, qt, kt, v, gc)
        wkv_new = wkv_new[None, None]
    else:
        rt, kt, qt, pt, v, bonus, g, gc = _rwkv_prep(proj, state["shift"], m, 1, lw, "rwkv_prep_sample")
        wkv_new, o = _wkv_step(state["wkv"], pt, rt, qt, kt, v, gc)
    ya = _rwkv_post(o, bonus, g, lw, tiles["post"], "rwkv_post_" + tiles["tag"])

    outs, lses = [], []
    for gi in range(N_GROUPS):
        if prompt:
            o_g, l_g = _attn_prompt(proj, bias_blk[gi][0], gi, 2048)
        else:
            sl = lambda c0: proj[:, c0 + gi * D_G:c0 + (gi + 1) * D_G].reshape(m, H_G, 1, HEAD_B)
            o_g, l_g = _attn_step(sl(COL_QKV), sl(COL_QKV + D_B), sl(COL_QKV + 2 * D_B), state["win"][gi],
                                  bias_blk[gi][1], bias_blk[gi][2], gi)
        outs.append(o_g)
        lses.append(l_g)
    mixed = _mix(ya, outs, lses, lw["w_out_a"], lw["w_out_b"], proj, tiles["mix_m"], 1024, "mix_" + tiles["tag"])
    h, hn = _wo(mixed, lw["w_o"], x, lw["norm2_g"], tiles["wo_m"], "wo_" + tiles["tag"])

    if prompt:
        act, t1, t2 = _ffn_up_prompt(hn, lw["w_up"], lw["conv_w"], lw["conv_b"], 1024, 512)
        conv_new = jnp.concatenate([t1[6:8], t2[6:8]], axis=1)[None, None]
    else:
        up = _matmul(hn, lw["w_up"], m, 512, "up_sample")
        act = _ffn_act_step(up, state["conv"], lw["conv_w"], lw["conv_b"])
        conv_new = jnp.concatenate([state["conv"][:, 1:], up[:, None, :]], axis=1)[None]
    y = _ffn_down(act, lw["w_down"], h, lw["normf_g"], tiles["down_m"], 2816, "down_" + tiles["tag"])
    return y, proj, wkv_new, conv_new


def kernel(x_prompt, x_sample, state_wkv, state_shift, state_ffn_conv, cache_win1, cache_win2, cache_win3, rel_bias,
           norm1_g, w_in, gate_b, mu_shift, w0, w_up_decay, a0, w_up_aaa, w_up_gate, k_k, k_a, r_k, gn_g, gn_b,
           w_out_a, w_out_b, w_o, norm2_g, w_up, conv_w, conv_b, w_down, normf_g):
    row = lambda a: a.reshape(1, -1)
    w_lora = jnp.zeros((D_LORA, 3 * D_A), F32)
    w_lora = w_lora.at[0:D_DECAY_LORA, 0:D_A].set(w_up_decay[0])
    w_lora = w_lora.at[D_DECAY_LORA:D_DECAY_LORA + D_AAA_LORA, D_A:2 * D_A].set(w_up_aaa[0])
    w_lora = w_lora.at[D_DECAY_LORA + D_AAA_LORA:, 2 * D_A:].set(w_up_gate[0])
    head = np.arange(D_A) // HEAD_A
    lw = dict(
        norm1_g=norm1_g[0], norm2_g=norm2_g[0], normf_g=normf_g,
        w_in=_cast_w_in(w_in[0]),
        in_bias=jnp.concatenate([jnp.zeros((1, COL_GATE), F32), row(gate_b[0])], axis=1),
        mu=row(mu_shift[0]), w_lora=w_lora, w0=row(w0[0]), a0=row(a0[0]), k_k=row(k_k[0]), k_a=row(k_a[0]),
        r_k=row(r_k[0]), gn_g=row(gn_g[0]), gn_b=row(gn_b[0]),
        hsum=jnp.asarray(head[:, None] == np.arange(LANES)[None, :], BF16),
        hbc=jnp.asarray(np.arange(LANES)[:, None] == head[None, :], BF16),
        w_out_a=w_out_a[0].astype(BF16), w_out_b=w_out_b[0].astype(BF16), w_o=w_o[0].astype(BF16),
        w_up=w_up[0].astype(BF16), conv_w=conv_w[0], conv_b=row(conv_b[0]), w_down=w_down[0].astype(BF16),
    )
    bias = [_bias_tables(rel_bias, gi) for gi in range(N_GROUPS)]

    tiles_p = dict(tag="prompt", proj_m=1024, post=256, mix_m=512, wo_m=512, down_m=512)
    y_p, proj_p, wkv_p, conv_p = _layer(x_prompt[0], tiles_p, lw, bias, prompt=True)

    b = DEC_BATCH
    tiles_s = dict(tag="sample", proj_m=b, post=b, mix_m=b, wo_m=b, down_m=b)
    state = dict(wkv=state_wkv, shift=state_shift.reshape(b, D_SHIFT), conv=state_ffn_conv[0],
                 win=(cache_win1, cache_win2, cache_win3))
    y_s, proj_s, wkv_s, conv_s = _layer(x_sample[:, 0], tiles_s, lw, bias, prompt=False, state=state)

    def kv_rows(proj, lo, gi):
        k = proj[lo:, COL_QKV + D_B + gi * D_G:COL_QKV + D_B + (gi + 1) * D_G]
        v = proj[lo:, COL_QKV + 2 * D_B + gi * D_G:COL_QKV + 2 * D_B + (gi + 1) * D_G]
        n = k.shape[0]
        return jnp.stack([k.reshape(n, H_G, HEAD_B), v.reshape(n, H_G, HEAD_B)], axis=1)

    win_p = [kv_rows(proj_p, SEQ - min(WINDOWS[gi], SEQ), gi)[None, None] for gi in range(N_GROUPS)]
    win_s = [kv_rows(proj_s, 0, gi)[None, :, None] for gi in range(N_GROUPS)]
    return (y_p[None], y_s[:, None],
            wkv_p, wkv_s,
            proj_p[SEQ - 1:, 0:D_SHIFT][None, None], proj_s[:, 0:D_SHIFT][None, :, None],
            conv_p, conv_s,
            win_p[0], win_s[0], win_p[1], win_s[1], win_p[2], win_s[2])
```

```python
import functools
import math

import numpy as np
import jax
import jax.numpy as jnp
from jax import lax
from jax.experimental import pallas as pl
from jax.experimental.pallas import tpu as pltpu

F32 = jnp.float32
BF16 = jnp.bfloat16
HIGHEST = lax.Precision.HIGHEST

D_MODEL = 2048
SEQ = 8192
DEC_BATCH = 32
HEAD_A = 64
H_A = 16
D_A = H_A * HEAD_A
D_DECAY_LORA = 96
D_AAA_LORA = 96
D_GATE_LORA = 64
D_LORA = D_DECAY_LORA + D_AAA_LORA + D_GATE_LORA
D_SHIFT = 3 * D_A + D_LORA
EPS_GN = 64e-5
HEAD_B = 64
H_G = 8
D_G = H_G * HEAD_B
WINDOWS = (128, 512, 2048)
DILATIONS = (1, 4, 16)
N_GROUPS = 3
D_B = N_GROUPS * D_G
BLK = 128
N_BUCKETS = 32
MAX_DISTANCE = 2048
COL_QKV = D_SHIFT
COL_GATE_SRC = D_SHIFT + 3 * D_B
D_IN = COL_GATE_SRC + 2 * D_MODEL
PROJ_TN = 1024
COL_GATE = -(-COL_GATE_SRC // PROJ_TN) * PROJ_TN
D_IN_PAD = COL_GATE + 2 * D_MODEL
D_FF = 5632
CONV_W = 3
EPS_RMS = 1e-6
CHUNK = 64
LANES = 128
VMEM_LIMIT = 56 * 1024 * 1024
STEP_CACHE_WORDS = 2 * 1024 * 1024


def _params(sem, vmem=VMEM_LIMIT):
    return pltpu.CompilerParams(dimension_semantics=sem, vmem_limit_bytes=vmem)


def _sigmoid(x):
    return 1.0 / (1.0 + jnp.exp(-x))


def _dot_bf16(a, b):
    return jnp.dot(a, b, preferred_element_type=F32)


def _split2(x):
    hi = x.astype(BF16)
    return hi, (x - hi.astype(F32)).astype(BF16)


def _split3(x):
    hi = x.astype(BF16)
    r1 = x - hi.astype(F32)
    mid = r1.astype(BF16)
    lo = (r1 - mid.astype(F32)).astype(BF16)
    return hi, mid, lo


def _dot_exact_rhs(x, m):
    hi, mid, lo = _split3(x)
    return _dot_bf16(hi, m) + _dot_bf16(mid, m) + _dot_bf16(lo, m)


def _head_sums(x, hsum, hbc):
    return _dot_exact_rhs(_dot_exact_rhs(x, hsum), hbc)


def _dot_exact_lhs(m, x):
    hi, mid, lo = _split3(x)
    return _dot_bf16(m, hi) + _dot_bf16(m, mid) + _dot_bf16(m, lo)


def _rms(x, g):
    return x * lax.rsqrt(jnp.mean(x * x, axis=-1, keepdims=True) + EPS_RMS) * g


def _cast_w_in_kernel(w_ref, o_ref, *, gap_block):
    keep = pl.program_id(0) != gap_block
    o_ref[...] = jnp.where(keep, w_ref[...], 0.0).astype(BF16)


def _cast_w_in(w):
    k = w.shape[0]
    tn = COL_GATE - COL_GATE_SRC
    assert tn % LANES == 0 and COL_GATE_SRC % tn == 0
    gap_block = COL_GATE_SRC // tn
    return pl.pallas_call(
        functools.partial(_cast_w_in_kernel, gap_block=gap_block),
        out_shape=jax.ShapeDtypeStruct((k, D_IN_PAD), BF16),
        grid=(D_IN_PAD // tn,),
        in_specs=[pl.BlockSpec((k, tn), lambda j: (0, jnp.where(j < gap_block, j, jnp.maximum(j - 1, 0))))],
        out_specs=pl.BlockSpec((k, tn), lambda j: (0, j)),
        compiler_params=_params(("parallel",)),
        name="cast_w_in",
    )(w)


def _proj_kernel(x_ref, g_ref, w_ref, b_ref, o_ref, xn_scr):
    @pl.when(pl.program_id(1) == 0)
    def _():
        xn_scr[...] = _rms(x_ref[...], g_ref[...]).astype(BF16)

    acc = _dot_bf16(xn_scr[...], w_ref[...])
    tn = o_ref.shape[1]
    col = lax.broadcasted_iota(jnp.int32, (1, tn), 1) + pl.program_id(1) * tn
    o_ref[...] = jnp.where(col >= COL_GATE, _sigmoid(acc + b_ref[...]), acc)


def _proj(x, g, w, bias, tm, name):
    m, k = x.shape
    n = w.shape[1]
    return pl.pallas_call(
        _proj_kernel,
        out_shape=jax.ShapeDtypeStruct((m, n), F32),
        grid=(m // tm, n // PROJ_TN),
        in_specs=[
            pl.BlockSpec((tm, k), lambda i, j: (i, 0)),
            pl.BlockSpec((1, k), lambda i, j: (0, 0)),
            pl.BlockSpec((k, PROJ_TN), lambda i, j: (0, j)),
            pl.BlockSpec((1, PROJ_TN), lambda i, j: (0, j)),
        ],
        out_specs=pl.BlockSpec((tm, PROJ_TN), lambda i, j: (i, j)),
        scratch_shapes=[pltpu.VMEM((tm, k), BF16)],
        compiler_params=_params(("parallel", "arbitrary")),
        name=name,
    )(x, g.reshape(1, k), w, bias)


def _matmul_kernel(a_ref, w_ref, o_ref):
    o_ref[...] = _dot_bf16(a_ref[...], w_ref[...].astype(BF16))


def _matmul(a, w, tm, tn, name):
    m, k = a.shape
    n = w.shape[1]
    return pl.pallas_call(
        _matmul_kernel,
        out_shape=jax.ShapeDtypeStruct((m, n), F32),
        grid=(m // tm, n // tn),
        in_specs=[pl.BlockSpec((tm, k), lambda i, j: (i, 0)), pl.BlockSpec((k, tn), lambda i, j: (0, j))],
        out_specs=pl.BlockSpec((tm, tn), lambda i, j: (i, j)),
        compiler_params=_params(("parallel", "arbitrary")),
        name=name,
    )(a, w)


def _prep_kernel(p_ref, prev_ref, mu_ref, wl_ref, w0_ref, a0_ref, kk_ref, ka_ref, rk_ref, lmat_ref, sel_ref,
                 hs_ref, hb_ref, rt_ref, kt_ref, qt_ref, pt_ref, v_ref, bonus_ref, g_ref, gc_ref, *, rows_are_time):
    p = p_ref[...]
    tm = p.shape[0]
    if rows_are_time:
        last = jnp.where(pl.program_id(0) == 0, 0.0, prev_ref[7:8, :])
        row = lax.broadcasted_iota(jnp.int32, (tm, 1), 0)
        prev = jnp.where(row == 0, last, pltpu.roll(p, 1, axis=0))
    else:
        prev = prev_ref[...]
    xm = p + mu_ref[...] * (prev - p)
    r = xm[:, 0:D_A]
    k = xm[:, D_A:2 * D_A]
    v = xm[:, 2 * D_A:3 * D_A]
    xl = xm[:, 3 * D_A:D_SHIFT]
    lane = lax.broadcasted_iota(jnp.int32, xl.shape, 1)
    act = jnp.where(lane < D_DECAY_LORA, jnp.tanh(xl),
                    jnp.where(lane < D_DECAY_LORA + D_AAA_LORA, xl, _sigmoid(xl)))
    act_hi, act_lo = _split2(act)
    w_hi, w_lo = wl_ref[0], wl_ref[1]
    lora = _dot_bf16(act_hi, w_hi) + _dot_bf16(act_hi, w_lo) + _dot_bf16(act_lo, w_hi)
    y = -(w0_ref[...] + lora[:, 0:D_A])
    softplus = jnp.maximum(y, 0.0) + jnp.log(1.0 + jnp.exp(-jnp.abs(y)))
    logw = -jnp.exp(-softplus - 0.5)
    a = _sigmoid(a0_ref[...] + lora[:, D_A:2 * D_A])
    g = lora[:, 2 * D_A:3 * D_A]
    kkr = k * kk_ref[...]
    kp = k * (1.0 + (a - 1.0) * ka_ref[...])
    seg = _head_sums(jnp.concatenate([kkr * kkr, r * kp * rk_ref[...]], axis=0), hs_ref[...], hb_ref[...])
    kk = kkr / jnp.maximum(jnp.sqrt(seg[0:tm]), 1e-12)
    cum = _dot_exact_lhs(lmat_ref[...], logw)
    e_out = jnp.exp(-cum)
    rt_ref[...] = r * jnp.exp(cum)
    kt_ref[...] = kp * e_out
    qt_ref[...] = kk * a * e_out
    pt_ref[...] = -kk * jnp.exp(cum - logw)
    v_ref[...] = v
    bonus_ref[...] = seg[tm:2 * tm] * v
    g_ref[...] = g
    gc_ref[...] = jnp.exp(_dot_exact_lhs(sel_ref[...], cum))


def _rwkv_prep(proj, prev, tm, chunk, lw, name):
    m = proj.shape[0]
    rows_are_time = prev is None
    n_tiles = m // tm
    t = np.arange(tm)
    lmat = ((t[:, None] // chunk == t[None, :] // chunk) & (t[None, :] <= t[:, None])).astype(np.float32)
    sel_stride = 1 if chunk == 1 else 8
    n_sel = sel_stride * (tm // chunk)
    sel = np.zeros((n_sel, tm), np.float32)
    for c in range(tm // chunk):
        sel[sel_stride * c, (c + 1) * chunk - 1] = 1.0
    if rows_are_time:
        prev_arr = proj
        prev_spec = pl.BlockSpec((8, D_SHIFT), lambda i: (jnp.maximum(i * (tm // 8) - 1, 0), 0))
    else:
        prev_arr = prev
        prev_spec = pl.BlockSpec((tm, D_SHIFT), lambda i: (i, 0))
    vec = lambda d: pl.BlockSpec((1, d), lambda i: (0, 0))
    full = lambda a: pl.BlockSpec(a.shape, lambda i: (0, 0))
    big = pl.BlockSpec((tm, D_A), lambda i: (i, 0))
    out = jax.ShapeDtypeStruct((m, D_A), F32)
    lmat_b = jnp.asarray(lmat, BF16)
    sel_b = jnp.asarray(sel, BF16)
    return pl.pallas_call(
        functools.partial(_prep_kernel, rows_are_time=rows_are_time),
        out_shape=[out] * 7 + [jax.ShapeDtypeStruct((n_tiles * n_sel, D_A), F32)],
        grid=(n_tiles,),
        in_specs=[pl.BlockSpec((tm, D_SHIFT), lambda i: (i, 0)), prev_spec, vec(D_SHIFT),
                  pl.BlockSpec(lw["w_lora"].shape, lambda i: (0, 0, 0)),
                  vec(D_A), vec(D_A), vec(D_A), vec(D_A), vec(D_A), full(lmat_b), full(sel_b), full(lw["hsum"]),
                  full(lw["hbc"])],
        out_specs=[big] * 7 + [pl.BlockSpec((n_sel, D_A), lambda i: (i, 0))],
        compiler_params=_params(("parallel",)),
        name=name,
    )(proj, prev_arr, lw["mu"], lw["w_lora"], lw["w0"], lw["a0"], lw["k_k"], lw["k_a"], lw["r_k"], lmat_b, sel_b,
      lw["hsum"], lw["hbc"])


def _bdot3(a, b, dims):
    ah, al = _split2(a)
    bh, bl = _split2(b)
    (ca,), (cb,) = dims[0]
    a_cat = jnp.concatenate([ah, ah, al, jnp.zeros_like(al)], axis=ca)
    b_cat = jnp.concatenate([bh, bl, bh, jnp.zeros_like(bl)], axis=cb)
    return lax.dot_general(a_cat, b_cat, dims, preferred_element_type=F32)


_NN = (((2,), (1,)), ((0,), (0,)))
_NT = (((2,), (2,)), ((0,), (0,)))


def _scan_kernel(pt_ref, rt_ref, qt_ref, kt_ref, v_ref, gc_ref, o_ref, sfin_ref, s_scr):
    @pl.when(pl.program_id(0) == 0)
    def _():
        s_scr[...] = jnp.zeros_like(s_scr)

    def heads(x):
        return jnp.stack([x[:, h * HEAD_A:(h + 1) * HEAD_A] for h in range(H_A)], axis=0)

    p, r, q, k, v = (heads(ref[...]) for ref in (pt_ref, rt_ref, qt_ref, kt_ref, v_ref))
    gam = heads(gc_ref[0:1, :])
    ri = lax.broadcasted_iota(jnp.int32, (1, CHUNK, CHUNK), 1)
    ci = lax.broadcasted_iota(jnp.int32, (1, CHUNK, CHUNK), 2)
    strict = ri > ci
    incl = ri >= ci
    eye = (ri == ci).astype(F32)

    gram = _bdot3(jnp.concatenate([p, r], axis=1), jnp.concatenate([q, k], axis=1), _NT)
    a_qp = jnp.where(strict, gram[:, 0:CHUNK, 0:CHUNK], 0.0)
    a_kp = jnp.where(strict, gram[:, 0:CHUNK, CHUNK:], 0.0)
    a_rq = jnp.where(incl, gram[:, CHUNK:, 0:CHUNK], 0.0)
    a_rk = jnp.where(incl, gram[:, CHUNK:, CHUNK:], 0.0)
    same = lambda log2_bs: (ri >> log2_bs) == (ci >> log2_bs)
    a_d = jnp.where(same(4), a_qp, 0.0)
    tinv = eye + a_d
    pw = _bdot3(a_d, a_d, _NN)
    for _ in range(2):
        both = _bdot3(jnp.concatenate([pw, tinv], axis=1), pw, _NN)
        pw = both[:, 0:CHUNK]
        tinv = tinv + both[:, CHUNK:]
    tinv = tinv + _bdot3(tinv, pw, _NN)
    for log2_bs in (4, 5):
        off = jnp.where(jnp.logical_and(same(log2_bs + 1), jnp.logical_not(same(log2_bs))), a_qp, 0.0)
        tinv = tinv + _bdot3(tinv, _bdot3(off, tinv, _NN), _NN)
    av = _bdot3(jnp.concatenate([a_kp, a_rk], axis=1), v, _NN)
    pw_hat = _bdot3(tinv, jnp.concatenate([p, av[:, 0:CHUNK]], axis=2), _NN)
    x = _bdot3(a_rq, pw_hat, _NN)
    r_hat = r + x[:, :, 0:HEAD_A]
    o_loc = x[:, :, HEAD_A:] + av[:, CHUNK:]
    y = _bdot3(jnp.swapaxes(pw_hat, 1, 2), q, _NN)
    m_mat = (eye + y[:, 0:HEAD_A]) * gam
    n_mat = (y[:, HEAD_A:] + _bdot3(jnp.swapaxes(v, 1, 2), k, _NN)) * gam

    s0 = s_scr[...]
    o = _bdot3(r_hat, s0, _NT) + o_loc
    s_scr[...] = _bdot3(s0, m_mat, _NN) + n_mat
    o_ref[...] = jnp.concatenate([o[h] for h in range(H_A)], axis=1)

    @pl.when(pl.program_id(0) == pl.num_programs(0) - 1)
    def _():
        sfin_ref[...] = s_scr[...]


def _wkv_scan(pt, rt, qt, kt, v, gc):
    t = pt.shape[0]
    blk = pl.BlockSpec((CHUNK, D_A), lambda c: (c, 0))
    return pl.pallas_call(
        _scan_kernel,
        out_shape=[jax.ShapeDtypeStruct((t, D_A), F32), jax.ShapeDtypeStruct((H_A, HEAD_A, HEAD_A), F32)],
        grid=(t // CHUNK,),
        in_specs=[blk] * 5 + [pl.BlockSpec((8, D_A), lambda c: (c, 0))],
        out_specs=[blk, pl.BlockSpec((H_A, HEAD_A, HEAD_A), lambda c: (0, 0, 0))],
        scratch_shapes=[pltpu.VMEM((H_A, HEAD_A, HEAD_A), F32)],
        compiler_params=_params(("arbitrary",)),
        name="wkv_scan_prompt",
    )(pt, rt, qt, kt, v, gc)


def _wkv_step_kernel(s_ref, rows_ref, sn_ref, o_ref):
    ri = lax.broadcasted_iota(jnp.int32, (1, HEAD_A, HEAD_A), 1)
    ci = lax.broadcasted_iota(jnp.int32, (1, HEAD_A, HEAD_A), 2)
    eye = ri == ci
    pt, rt, qt, kt, v, gc = (rows_ref[0, n] for n in range(6))
    s = s_ref[0, 0]
    u = jnp.sum(s * pt, axis=2, keepdims=True)
    v_col = jnp.sum(jnp.where(eye, v, 0.0), axis=2, keepdims=True)
    m = s + u * qt + v_col * kt
    sn_ref[0, 0] = m * gc
    o_col = jnp.sum(m * rt, axis=2, keepdims=True)
    o_ref[0] = jnp.sum(jnp.where(eye, o_col, 0.0), axis=1, keepdims=True)


def _wkv_step(state, pt, rt, qt, kt, v, gc):
    b = pt.shape[0]
    rows = jnp.stack([pt, rt, qt, kt, v, gc], axis=1).reshape(b, 6, H_A, 1, HEAD_A)
    st_spec = pl.BlockSpec((1, 1, H_A, HEAD_A, HEAD_A), lambda i: (0, i, 0, 0, 0))
    sn, o = pl.pallas_call(
        _wkv_step_kernel,
        out_shape=[jax.ShapeDtypeStruct(state.shape, F32), jax.ShapeDtypeStruct((b, H_A, 1, HEAD_A), F32)],
        grid=(b,),
        in_specs=[st_spec, pl.BlockSpec((1, 6, H_A, 1, HEAD_A), lambda i: (i, 0, 0, 0, 0))],
        out_specs=[st_spec, pl.BlockSpec((1, H_A, 1, HEAD_A), lambda i: (i, 0, 0, 0))],
        compiler_params=_params(("parallel",)),
        name="wkv_step_sample",
    )(state, rows)
    return sn, o.reshape(b, D_A)


def _post_kernel(o_ref, bonus_ref, g_ref, gng_ref, gnb_ref, hs_ref, hb_ref, y_ref):
    o = o_ref[...]
    hsum, hbc = hs_ref[...], hb_ref[...]
    mu = _head_sums(o, hsum, hbc) * (1.0 / HEAD_A)
    d = o - mu
    var = _head_sums(d * d, hsum, hbc) * (1.0 / HEAD_A)
    o_n = d * lax.rsqrt(var + EPS_GN) * gng_ref[...] + gnb_ref[...]
    y_ref[...] = ((o_n + bonus_ref[...]) * g_ref[...]).astype(y_ref.dtype)


def _rwkv_post(o, bonus, g, lw, tm, name):
    m = o.shape[0]
    big = pl.BlockSpec((tm, D_A), lambda i: (i, 0))
    vec = pl.BlockSpec((1, D_A), lambda i: (0, 0))
    return pl.pallas_call(
        _post_kernel,
        out_shape=jax.ShapeDtypeStruct((m, D_A), BF16),
        grid=(m // tm,),
        in_specs=[big, big, big, vec, vec, pl.BlockSpec((D_A, LANES), lambda i: (0, 0)),
                  pl.BlockSpec((LANES, D_A), lambda i: (0, 0))],
        out_specs=big,
        compiler_params=_params(("parallel",)),
        name=name,
    )(o, bonus, g, lw["gn_g"], lw["gn_b"], lw["hsum"], lw["hbc"])


def _attn_kernel(q_ref, kc_ref, vc_ref, kh_ref, vh_ref, bias_ref, o_ref, l_ref, *, dil, m_blocks):
    scale = HEAD_B ** -0.5
    n_units = dil * m_blocks

    def rows(ref, start, size):
        if dil == 1:
            return ref[pl.ds(start, size), :]
        return ref[pl.ds(start, size, stride=dil), :]

    qs, ks, vs, bases = [], [], [], []
    for r in range(dil):
        for mb in range(m_blocks):
            base = r + dil * BLK * mb
            bases.append(base)
            qs.append(rows(q_ref, base, BLK))
            if mb == 0:
                ks.append(jnp.concatenate([rows(kh_ref, r, BLK), rows(kc_ref, r, BLK)], axis=0))
                vs.append(jnp.concatenate([rows(vh_ref, r, BLK), rows(vc_ref, r, BLK)], axis=0))
            else:
                ks.append(rows(kc_ref, base - dil * BLK, 2 * BLK))
                vs.append(rows(vc_ref, base - dil * BLK, 2 * BLK))

    def batch(xs):
        return jnp.stack([x[:, sub * HEAD_B:(sub + 1) * HEAD_B].astype(BF16) for sub in range(2) for x in xs], axis=0)

    qb, kb, vb = batch(qs), batch(ks), batch(vs)
    s = lax.dot_general(qb, kb, (((2,), (2,)), ((0,), (0,))), preferred_element_type=F32) * scale
    s = s.reshape(2, n_units, BLK, 2 * BLK) + bias_ref[...][:, None]
    unit = lax.broadcasted_iota(jnp.int32, (1, n_units, 1, 2 * BLK), 1)
    col = lax.broadcasted_iota(jnp.int32, (1, n_units, 1, 2 * BLK), 3)
    no_prev = jnp.logical_and(jnp.logical_and(pl.program_id(0) == 0, (unit & (m_blocks - 1)) == 0), col < BLK)
    s = jnp.where(no_prev, -jnp.inf, s)
    m = jnp.max(s, axis=-1, keepdims=True)
    p = jnp.exp(s - m)
    l = jnp.sum(p, axis=-1, keepdims=True)
    pv = lax.dot_general(p.astype(BF16).reshape(2 * n_units, BLK, 2 * BLK), vb, (((2,), (1,)), ((0,), (0,))),
                         preferred_element_type=F32)
    o = pv.reshape(2, n_units, BLK, HEAD_B) / l
    lse = jnp.broadcast_to(m + jnp.log(l), (2, n_units, BLK, HEAD_B))
    for u, base in enumerate(bases):
        o_blk = jnp.concatenate([o[0, u], o[1, u]], axis=1)
        l_blk = jnp.concatenate([lse[0, u], lse[1, u]], axis=1)
        if dil == 1:
            o_ref[pl.ds(base, BLK), :] = o_blk
            l_ref[pl.ds(base, BLK), :] = l_blk
        else:
            o_ref[pl.ds(base, BLK, stride=dil), :] = o_blk
            l_ref[pl.ds(base, BLK, stride=dil), :] = l_blk


def _attn_prompt(proj, bias, gi, rows_per_step):
    t = proj.shape[0]
    dil = DILATIONS[gi]
    span = BLK * dil
    m_blocks = rows_per_step // span
    cq = (COL_QKV + gi * D_G) // LANES
    ck = (COL_QKV + D_B + gi * D_G) // LANES
    cv = (COL_QKV + 2 * D_B + gi * D_G) // LANES
    cur = lambda c0: pl.BlockSpec((rows_per_step, LANES), lambda i, hp: (i, c0 + hp))
    halo = lambda c0: pl.BlockSpec((span, LANES), lambda i, hp: (jnp.maximum(i * m_blocks - 1, 0), c0 + hp))
    out_spec = pl.BlockSpec((rows_per_step, LANES), lambda i, hp: (i, hp))
    out = jax.ShapeDtypeStruct((t, D_G), F32)
    return pl.pallas_call(
        functools.partial(_attn_kernel, dil=dil, m_blocks=m_blocks),
        out_shape=[out, out],
        grid=(t // rows_per_step, H_G // 2),
        in_specs=[cur(cq), cur(ck), cur(cv), halo(ck), halo(cv),
                  pl.BlockSpec((2, BLK, 2 * BLK), lambda i, hp: (hp, 0, 0))],
        out_specs=[out_spec, out_spec],
        compiler_params=_params(("parallel", "parallel")),
        name=f"attn_prompt_g{gi}",
    )(proj, proj, proj, proj, proj, bias)


def _attn_step_kernel(q_ref, kn_ref, vn_ref, c_ref, bias_ref, bias0_ref, o_ref, l_ref):
    scale = HEAD_B ** -0.5
    ri = lax.broadcasted_iota(jnp.int32, (1, 1, HEAD_B, HEAD_B), 2)
    ci = lax.broadcasted_iota(jnp.int32, (1, 1, HEAD_B, HEAD_B), 3)
    eye = ri == ci
    q = q_ref[...]
    q_col = jnp.sum(jnp.where(eye, q, 0.0), axis=3, keepdims=True)
    s = jnp.sum(c_ref[:, 0] * q_col, axis=2, keepdims=True) * scale + bias_ref[...]
    s_new = jnp.sum(q * kn_ref[...], axis=3, keepdims=True) * scale + bias0_ref[...]
    m = jnp.maximum(jnp.max(s, axis=3, keepdims=True), s_new)
    p = jnp.exp(s - m)
    p_new = jnp.exp(s_new - m)
    l = jnp.sum(p, axis=3, keepdims=True) + p_new
    o_col = jnp.sum(c_ref[:, 1] * p, axis=3, keepdims=True)
    o_row = jnp.sum(jnp.where(eye, o_col, 0.0), axis=2, keepdims=True)
    o_ref[...] = (o_row + p_new * vn_ref[...]) / l
    l_ref[...] = jnp.broadcast_to(m + jnp.log(l), o_ref.shape)


def _attn_step(q, k_new, v_new, cache, bias, bias0, gi):
    b = q.shape[0]
    w = cache.shape[2]
    bt = max(1, min(b, STEP_CACHE_WORDS // (2 * D_G * w)))
    cache_t = jnp.transpose(cache, (0, 1, 3, 4, 5, 2))
    vec = pl.BlockSpec((bt, H_G, 1, HEAD_B), lambda i: (i, 0, 0, 0))
    out = jax.ShapeDtypeStruct((b, H_G, 1, HEAD_B), F32)
    o, l = pl.pallas_call(
        _attn_step_kernel,
        out_shape=[out, out],
        grid=(b // bt,),
        in_specs=[vec, vec, vec,
                  pl.BlockSpec((None, bt, 2, H_G, HEAD_B, w), lambda i: (0, i, 0, 0, 0, 0)),
                  pl.BlockSpec((H_G, 1, w), lambda i: (0, 0, 0)),
                  pl.BlockSpec((H_G, 1, 1), lambda i: (0, 0, 0))],
        out_specs=[vec, vec],
        compiler_params=_params(("parallel",)),
        name=f"attn_step_g{gi}",
    )(q, k_new, v_new, cache_t, bias, bias0)
    return o.reshape(b, D_G), l.reshape(b, D_G)


def _mix_kernel(ya_ref, o1_ref, o2_ref, o3_ref, l1_ref, l2_ref, l3_ref, wa_ref, wb_ref, ga_ref, gb_ref, o_ref,
                yb_scr):
    @pl.when(pl.program_id(1) == 0)
    def _():
        l1, l2, l3 = l1_ref[...], l2_ref[...], l3_ref[...]
        m = jnp.maximum(jnp.maximum(l1, l2), l3)
        e1, e2, e3 = jnp.exp(l1 - m), jnp.exp(l2 - m), jnp.exp(l3 - m)
        den = e1 + e2 + e3
        yb = (e1 / den) * o1_ref[...] + (e2 / den) * o2_ref[...] + (e3 / den) * o3_ref[...]
        yb_scr[...] = yb.astype(BF16)

    mixed = ga_ref[...] * _dot_bf16(ya_ref[...], wa_ref[...]) + gb_ref[...] * _dot_bf16(yb_scr[...], wb_ref[...])
    o_ref[...] = mixed.astype(o_ref.dtype)


def _mix(ya, outs, lses, wa, wb, proj, tm, tn, name):
    m = ya.shape[0]
    ga0 = COL_GATE // tn
    gb0 = (COL_GATE + D_MODEL) // tn
    grp = pl.BlockSpec((tm, D_G), lambda i, j: (i, 0))
    return pl.pallas_call(
        _mix_kernel,
        out_shape=jax.ShapeDtypeStruct((m, D_MODEL), BF16),
        grid=(m // tm, D_MODEL // tn),
        in_specs=[pl.BlockSpec((tm, D_A), lambda i, j: (i, 0))] + [grp] * 6 + [
            pl.BlockSpec((D_A, tn), lambda i, j: (0, j)),
            pl.BlockSpec((D_G, tn), lambda i, j: (0, j)),
            pl.BlockSpec((tm, tn), lambda i, j: (i, ga0 + j)),
            pl.BlockSpec((tm, tn), lambda i, j: (i, gb0 + j)),
        ],
        out_specs=pl.BlockSpec((tm, tn), lambda i, j: (i, j)),
        scratch_shapes=[pltpu.VMEM((tm, D_G), BF16)],
        compiler_params=_params(("parallel", "arbitrary")),
        name=name,
    )(ya, *outs, *lses, wa, wb, proj, proj)


def _wo_kernel(a_ref, w_ref, x_ref, g_ref, h_ref, hn_ref):
    h = x_ref[...] + _dot_bf16(a_ref[...], w_ref[...])
    h_ref[...] = h
    hn_ref[...] = _rms(h, g_ref[...]).astype(BF16)


def _wo(a, w, x, g, tm, name):
    m, k = a.shape
    n = w.shape[1]
    row = pl.BlockSpec((tm, n), lambda i: (i, 0))
    return pl.pallas_call(
        _wo_kernel,
        out_shape=[jax.ShapeDtypeStruct((m, n), F32), jax.ShapeDtypeStruct((m, n), BF16)],
        grid=(m // tm,),
        in_specs=[pl.BlockSpec((tm, k), lambda i: (i, 0)), pl.BlockSpec((k, n), lambda i: (0, 0)), row,
                  pl.BlockSpec((1, n), lambda i: (0, 0))],
        out_specs=[row, row],
        compiler_params=_params(("parallel",)),
        name=name,
    )(a, w, x, g.reshape(1, n))


def _gelu(x):
    return 0.5 * x * (1.0 + lax.erf(x * (1.0 / math.sqrt(2.0))))


def _ffn_up_kernel(h_ref, w1_ref, w2_ref, cw1_ref, cw2_ref, cb1_ref, cb2_ref, act_ref, t1_ref, t2_ref,
                   c1_scr, c2_scr, w1_scr, w2_scr):
    tm = h_ref.shape[0]

    @pl.when(pl.program_id(1) == 0)
    def _():
        c1_scr[...] = jnp.zeros_like(c1_scr)
        c2_scr[...] = jnp.zeros_like(c2_scr)
        w1_scr[...] = w1_ref[...].astype(BF16)
        w2_scr[...] = w2_ref[...].astype(BF16)

    a = h_ref[...]
    row = lax.broadcasted_iota(jnp.int32, (tm, 1), 0)

    def conv(u, carry_ref, cw_ref, cb_ref):
        m1 = jnp.where(row == 0, carry_ref[7:8, :], pltpu.roll(u, 1, axis=0))
        m2 = jnp.where(row == 0, carry_ref[6:7, :], jnp.where(row == 1, carry_ref[7:8, :], pltpu.roll(u, 2, axis=0)))
        return cb_ref[...] + cw_ref[0:1, :] * m2 + cw_ref[1:2, :] * m1 + cw_ref[2:3, :] * u

    u1 = _dot_bf16(a, w1_scr[...])
    u2 = _dot_bf16(a, w2_scr[...])
    c1 = conv(u1, c1_scr, cw1_ref, cb1_ref)
    c2 = conv(u2, c2_scr, cw2_ref, cb2_ref)
    act_ref[...] = (_gelu(c1) * c2).astype(act_ref.dtype)
    c1_scr[...] = u1[tm - 8:tm]
    c2_scr[...] = u2[tm - 8:tm]
    t1_ref[...] = u1[tm - 8:tm]
    t2_ref[...] = u2[tm - 8:tm]


def _ffn_up_prompt(hn, w_up, conv_w, conv_b, tm, tn):
    m = hn.shape[0]
    nj = D_FF // tn
    tail = jax.ShapeDtypeStruct((8, D_FF), F32)
    return pl.pallas_call(
        _ffn_up_kernel,
        out_shape=[jax.ShapeDtypeStruct((m, D_FF), BF16), tail, tail],
        grid=(nj, m // tm),
        in_specs=[
            pl.BlockSpec((tm, D_MODEL), lambda j, i: (i, 0)),
            pl.BlockSpec((D_MODEL, tn), lambda j, i: (0, j)),
            pl.BlockSpec((D_MODEL, tn), lambda j, i: (0, nj + j)),
            pl.BlockSpec((CONV_W, tn), lambda j, i: (0, j)),
            pl.BlockSpec((CONV_W, tn), lambda j, i: (0, nj + j)),
            pl.BlockSpec((1, tn), lambda j, i: (0, j)),
            pl.BlockSpec((1, tn), lambda j, i: (0, nj + j)),
        ],
        out_specs=[pl.BlockSpec((tm, tn), lambda j, i: (i, j)),
                   pl.BlockSpec((8, tn), lambda j, i: (0, j)),
                   pl.BlockSpec((8, tn), lambda j, i: (0, j))],
        scratch_shapes=[pltpu.VMEM((8, tn), F32), pltpu.VMEM((8, tn), F32),
                        pltpu.VMEM((D_MODEL, tn), BF16), pltpu.VMEM((D_MODEL, tn), BF16)],
        compiler_params=_params(("arbitrary", "arbitrary")),
        name="ffn_up_prompt",
    )(hn, w_up, w_up, conv_w, conv_w, conv_b, conv_b)


def _ffn_act_step_kernel(up_ref, prev_ref, cw_ref, cb_ref, act_ref):
    up = up_ref[...]
    w = 2 * D_FF
    c = cb_ref[...] + cw_ref[0:1, :] * prev_ref[:, 0:w] + cw_ref[1:2, :] * prev_ref[:, w:2 * w] + cw_ref[2:3, :] * up
    act_ref[...] = (_gelu(c[:, 0:D_FF]) * c[:, D_FF:w]).astype(act_ref.dtype)


def _ffn_act_step(up, conv_prev, conv_w, conv_b):
    b = up.shape[0]
    return pl.pallas_call(
        _ffn_act_step_kernel,
        out_shape=jax.ShapeDtypeStruct((b, D_FF), BF16),
        compiler_params=pltpu.CompilerParams(vmem_limit_bytes=VMEM_LIMIT),
        name="ffn_act_sample",
    )(up, conv_prev.reshape(b, (CONV_W - 1) * 2 * D_FF), conv_w, conv_b)


def _down_kernel(a_ref, w_ref, h_ref, g_ref, o_ref, acc_ref):
    @pl.when(pl.program_id(1) == 0)
    def _():
        acc_ref[...] = h_ref[...]

    out = acc_ref[...] + _dot_bf16(a_ref[...], w_ref[...])
    acc_ref[...] = out
    inv = lax.rsqrt(jnp.mean(out * out, axis=-1, keepdims=True) + EPS_RMS)
    o_ref[...] = out * inv * g_ref[...]


def _ffn_down(act, w_down, h, g, tm, tk, name):
    m = act.shape[0]
    nk = D_FF // tk
    return pl.pallas_call(
        _down_kernel,
        out_shape=jax.ShapeDtypeStruct((m, D_MODEL), F32),
        grid=(m // tm, nk),
        in_specs=[
            pl.BlockSpec((tm, tk), lambda i, k: (i, k)),
            pl.BlockSpec((tk, D_MODEL), lambda i, k: (k, 0)),
            pl.BlockSpec((tm, D_MODEL), lambda i, k: (i, 0)),
            pl.BlockSpec((1, D_MODEL), lambda i, k: (0, 0)),
        ],
        out_specs=pl.BlockSpec((tm, D_MODEL), lambda i, k: (i, 0)),
        scratch_shapes=[pltpu.VMEM((tm, D_MODEL), F32)],
        compiler_params=_params(("parallel", "arbitrary")),
        name=name,
    )(act, w_down, h, g.reshape(1, D_MODEL))


def _rel_bucket(dist):
    max_exact = N_BUCKETS // 2
    d_f = jnp.maximum(dist, 1).astype(F32)
    large = max_exact + (jnp.log(d_f / max_exact) / math.log(MAX_DISTANCE / max_exact)
                         * (N_BUCKETS - max_exact)).astype(jnp.int32)
    large = jnp.minimum(large, N_BUCKETS - 1)
    return jnp.where(dist < max_exact, dist, large)


def _bias_rows(tab, dist):
    onehot = (_rel_bucket(dist)[None, :] == jnp.arange(N_BUCKETS)[:, None]).astype(F32)
    return jnp.dot(tab.T, onehot, precision=HIGHEST)


def _bias_tables(rel_bias, gi):
    dil = DILATIONS[gi]
    win = WINDOWS[gi]
    reach = win // dil
    tab = rel_bias[:, gi * H_G:(gi + 1) * H_G]
    qi = jnp.arange(BLK)[:, None]
    ki = jnp.arange(2 * BLK)[None, :]
    rel = qi + BLK - ki
    blk = _bias_rows(tab, (dil * jnp.maximum(rel, 0)).reshape(-1)).reshape(H_G, BLK, 2 * BLK)
    blk = jnp.where(((rel >= 0) & (rel <= reach))[None], blk, -jnp.inf)
    back = win - jnp.arange(win)
    step = jnp.where((back % dil == 0)[None, :], _bias_rows(tab, back), -jnp.inf)[:, None, :]
    step0 = _bias_rows(tab, jnp.zeros((1,), jnp.int32))[:, None, :]
    return blk, step, step0


def _layer(x, tiles, lw, bias_blk, prompt, state=None):
    m = x.shape[0]
    proj = _proj(x, lw["norm1_g"], lw["w_in"], lw["in_bias"], tiles["proj_m"], "proj_" + tiles["tag"])

    if prompt:
        rt, kt, qt, pt, v, bonus, g, gc = _rwkv_prep(proj, None, 128, CHUNK, lw, "rwkv_prep_prompt")
        o, wkv_new = _wkv_scan(pt, rt, qt, kt, v, gc)
        wkv_new = wkv_new[None, None]
    else:
        rt, kt, qt, pt, v, bonus, g, gc = _rwkv_prep(proj, state["shift"], m, 1, lw, "rwkv_prep_sample")
        wkv_new, o = _wkv_step(state["wkv"], pt, rt, qt, kt, v, gc)
    ya = _rwkv_post(o, bonus, g, lw, tiles["post"], "rwkv_post_" + tiles["tag"])

    outs, lses = [], []
    for gi in range(N_GROUPS):
        if prompt:
            o_g, l_g = _attn_prompt(proj, bias_blk[gi][0], gi, 2048)
        else:
            sl = lambda c0: proj[:, c0 + gi * D_G:c0 + (gi + 1) * D_G].reshape(m, H_G, 1, HEAD_B)
            o_g, l_g = _attn_step(sl(COL_QKV), sl(COL_QKV + D_B), sl(COL_QKV + 2 * D_B), state["win"][gi],
                                  bias_blk[gi][1], bias_blk[gi][2], gi)
        outs.append(o_g)
        lses.append(l_g)
    mixed = _mix(ya, outs, lses, lw["w_out_a"], lw["w_out_b"], proj, tiles["mix_m"], 1024, "mix_" + tiles["tag"])
    h, hn = _wo(mixed, lw["w_o"], x, lw["norm2_g"], tiles["wo_m"], "wo_" + tiles["tag"])

    if prompt:
        act, t1, t2 = _ffn_up_prompt(hn, lw["w_up"], lw["conv_w"], lw["conv_b"], 1024, 512)
        conv_new = jnp.concatenate([t1[6:8], t2[6:8]], axis=1)[None, None]
    else:
        up = _matmul(hn, lw["w_up"], m, 512, "up_sample")
        act = _ffn_act_step(up, state["conv"], lw["conv_w"], lw["conv_b"])
        conv_new = jnp.concatenate([state["conv"][:, 1:], up[:, None, :]], axis=1)[None]
    y = _ffn_down(act, lw["w_down"], h, lw["normf_g"], tiles["down_m"], 2816, "down_" + tiles["tag"])
    return y, proj, wkv_new, conv_new


def kernel(x_prompt, x_sample, state_wkv, state_shift, state_ffn_conv, cache_win1, cache_win2, cache_win3, rel_bias,
           norm1_g, w_in, gate_b, mu_shift, w0, w_up_decay, a0, w_up_aaa, w_up_gate, k_k, k_a, r_k, gn_g, gn_b,
           w_out_a, w_out_b, w_o, norm2_g, w_up, conv_w, conv_b, w_down, normf_g):
    row = lambda a: a.reshape(1, -1)
    w_lora = jnp.zeros((D_LORA, 3 * D_A), F32)
    w_lora = w_lora.at[0:D_DECAY_LORA, 0:D_A].set(w_up_decay[0])
    w_lora = w_lora.at[D_DECAY_LORA:D_DECAY_LORA + D_AAA_LORA, D_A:2 * D_A].set(w_up_aaa[0])
    w_lora = w_lora.at[D_DECAY_LORA + D_AAA_LORA:, 2 * D_A:].set(w_up_gate[0])
    head = np.arange(D_A) // HEAD_A
    lw = dict(
        norm1_g=norm1_g[0], norm2_g=norm2_g[0], normf_g=normf_g,
        w_in=_cast_w_in(w_in[0]),
        in_bias=jnp.concatenate([jnp.zeros((1, COL_GATE), F32), row(gate_b[0])], axis=1),
        mu=row(mu_shift[0]), w_lora=jnp.stack(_split2(w_lora)), w0=row(w0[0]), a0=row(a0[0]), k_k=row(k_k[0]), k_a=row(k_a[0]),
        r_k=row(r_k[0]), gn_g=row(gn_g[0]), gn_b=row(gn_b[0]),
        hsum=jnp.asarray(head[:, None] == np.arange(LANES)[None, :], BF16),
        hbc=jnp.asarray(np.arange(LANES)[:, None] == head[None, :], BF16),
        w_out_a=w_out_a[0].astype(BF16), w_out_b=w_out_b[0].astype(BF16), w_o=w_o[0].astype(BF16),
        w_up=w_up[0], conv_w=conv_w[0], conv_b=row(conv_b[0]), w_down=w_down[0].astype(BF16),
    )
    bias = [_bias_tables(rel_bias, gi) for gi in range(N_GROUPS)]

    tiles_p = dict(tag="prompt", proj_m=1024, post=256, mix_m=512, wo_m=512, down_m=512)
    y_p, proj_p, wkv_p, conv_p = _layer(x_prompt[0], tiles_p, lw, bias, prompt=True)

    b = DEC_BATCH
    tiles_s = dict(tag="sample", proj_m=b, post=b, mix_m=b, wo_m=b, down_m=b)
    state = dict(wkv=state_wkv, shift=state_shift.reshape(b, D_SHIFT), conv=state_ffn_conv[0],
                 win=(cache_win1, cache_win2, cache_win3))
    y_s, proj_s, wkv_s, conv_s = _layer(x_sample[:, 0], tiles_s, lw, bias, prompt=False, state=state)

    def kv_rows(proj, lo, gi):
        k = proj[lo:, COL_QKV + D_B + gi * D_G:COL_QKV + D_B + (gi + 1) * D_G]
        v = proj[lo:, COL_QKV + 2 * D_B + gi * D_G:COL_QKV + 2 * D_B + (gi + 1) * D_G]
        n = k.shape[0]
        return jnp.stack([k.reshape(n, H_G, HEAD_B), v.reshape(n, H_G, HEAD_B)], axis=1)

    win_p = [kv_rows(proj_p, SEQ - min(WINDOWS[gi], SEQ), gi)[None, None] for gi in range(N_GROUPS)]
    win_s = [kv_rows(proj_s, 0, gi)[None, :, None] for gi in range(N_GROUPS)]
    return (y_p[None], y_s[:, None],
            wkv_p, wkv_s,
            proj_p[SEQ - 1:, 0:D_SHIFT][None, None], proj_s[:, 0:D_SHIFT][None, :, None],
            conv_p, conv_s,
            win_p[0], win_s[0], win_p[1], win_s[1], win_p[2], win_s[2])
```

```python
import functools
import math

import numpy as np
import jax
import jax.numpy as jnp
from jax import lax
from jax.experimental import pallas as pl
from jax.experimental.pallas import tpu as pltpu

F32 = jnp.float32
BF16 = jnp.bfloat16
HIGHEST = lax.Precision.HIGHEST

D_MODEL = 2048
SEQ = 8192
DEC_BATCH = 32
HEAD_A = 64
H_A = 16
D_A = H_A * HEAD_A
D_DECAY_LORA = 96
D_AAA_LORA = 96
D_GATE_LORA = 64
D_LORA = D_DECAY_LORA + D_AAA_LORA + D_GATE_LORA
D_SHIFT = 3 * D_A + D_LORA
EPS_GN = 64e-5
HEAD_B = 64
H_G = 8
D_G = H_G * HEAD_B
WINDOWS = (128, 512, 2048)
DILATIONS = (1, 4, 16)
N_GROUPS = 3
D_B = N_GROUPS * D_G
BLK = 128
N_BUCKETS = 32
MAX_DISTANCE = 2048
COL_QKV = D_SHIFT
COL_GATE_SRC = D_SHIFT + 3 * D_B
D_IN = COL_GATE_SRC + 2 * D_MODEL
PROJ_TN = 1024
COL_GATE = -(-COL_GATE_SRC // PROJ_TN) * PROJ_TN
D_IN_PAD = COL_GATE + 2 * D_MODEL
D_FF = 5632
CONV_W = 3
EPS_RMS = 1e-6
CHUNK = 64
LANES = 128
SUBLANES = 8
VMEM_LIMIT = 56 * 1024 * 1024
STEP_CACHE_WORDS = 2 * 1024 * 1024


def _params(sem, vmem=VMEM_LIMIT):
    return pltpu.CompilerParams(dimension_semantics=sem, vmem_limit_bytes=vmem)


def _sigmoid(x):
    return 1.0 / (1.0 + jnp.exp(-x))


def _dot_bf16(a, b):
    return jnp.dot(a, b, preferred_element_type=F32)


def _split2(x):
    hi = x.astype(BF16)
    return hi, (x - hi.astype(F32)).astype(BF16)


def _split3(x):
    hi = x.astype(BF16)
    r1 = x - hi.astype(F32)
    mid = r1.astype(BF16)
    lo = (r1 - mid.astype(F32)).astype(BF16)
    return hi, mid, lo


def _dot_exact_rhs(x, m):
    hi, mid, lo = _split3(x)
    return _dot_bf16(hi, m) + _dot_bf16(mid, m) + _dot_bf16(lo, m)


def _head_sums(x, hsum, hbc):
    return _dot_exact_rhs(_dot_exact_rhs(x, hsum), hbc)


def _dot_exact_lhs(m, x):
    hi, mid, lo = _split3(x)
    return _dot_bf16(m, hi) + _dot_bf16(m, mid) + _dot_bf16(m, lo)


def _rms(x, g):
    return x * lax.rsqrt(jnp.mean(x * x, axis=-1, keepdims=True) + EPS_RMS) * g


def _cast_w_in_kernel(w_ref, o_ref, *, gap_block):
    keep = pl.program_id(0) != gap_block
    o_ref[...] = jnp.where(keep, w_ref[...], 0.0).astype(BF16)


def _cast_w_in(w):
    k = w.shape[0]
    tn = COL_GATE - COL_GATE_SRC
    assert tn % LANES == 0 and COL_GATE_SRC % tn == 0
    gap_block = COL_GATE_SRC // tn
    return pl.pallas_call(
        functools.partial(_cast_w_in_kernel, gap_block=gap_block),
        out_shape=jax.ShapeDtypeStruct((k, D_IN_PAD), BF16),
        grid=(D_IN_PAD // tn,),
        in_specs=[pl.BlockSpec((k, tn), lambda j: (0, jnp.where(j < gap_block, j, jnp.maximum(j - 1, 0))))],
        out_specs=pl.BlockSpec((k, tn), lambda j: (0, j)),
        compiler_params=_params(("parallel",)),
        name="cast_w_in",
    )(w)


def _proj_kernel(x_ref, g_ref, w_ref, b_ref, o_ref, xn_scr):
    @pl.when(pl.program_id(1) == 0)
    def _():
        xn_scr[...] = _rms(x_ref[...], g_ref[...]).astype(BF16)

    acc = _dot_bf16(xn_scr[...], w_ref[...])
    tn = o_ref.shape[1]
    col = lax.broadcasted_iota(jnp.int32, (1, tn), 1) + pl.program_id(1) * tn
    o_ref[...] = jnp.where(col >= COL_GATE, _sigmoid(acc + b_ref[...]), acc)


def _proj(x, g, w, bias, tm, name):
    m, k = x.shape
    n = w.shape[1]
    return pl.pallas_call(
        _proj_kernel,
        out_shape=jax.ShapeDtypeStruct((m, n), F32),
        grid=(m // tm, n // PROJ_TN),
        in_specs=[
            pl.BlockSpec((tm, k), lambda i, j: (i, 0)),
            pl.BlockSpec((1, k), lambda i, j: (0, 0)),
            pl.BlockSpec((k, PROJ_TN), lambda i, j: (0, j)),
            pl.BlockSpec((1, PROJ_TN), lambda i, j: (0, j)),
        ],
        out_specs=pl.BlockSpec((tm, PROJ_TN), lambda i, j: (i, j)),
        scratch_shapes=[pltpu.VMEM((tm, k), BF16)],
        compiler_params=_params(("parallel", "arbitrary")),
        name=name,
    )(x, g.reshape(1, k), w, bias)


def _matmul_kernel(a_ref, w_ref, o_ref):
    o_ref[...] = _dot_bf16(a_ref[...], w_ref[...].astype(BF16))


def _matmul(a, w, tm, tn, name):
    m, k = a.shape
    n = w.shape[1]
    return pl.pallas_call(
        _matmul_kernel,
        out_shape=jax.ShapeDtypeStruct((m, n), F32),
        grid=(m // tm, n // tn),
        in_specs=[pl.BlockSpec((tm, k), lambda i, j: (i, 0)), pl.BlockSpec((k, tn), lambda i, j: (0, j))],
        out_specs=pl.BlockSpec((tm, tn), lambda i, j: (i, j)),
        compiler_params=_params(("parallel", "arbitrary")),
        name=name,
    )(a, w)


def _prep_kernel(p_ref, prev_ref, mu_ref, wl_ref, w0_ref, a0_ref, kk_ref, ka_ref, rk_ref, lmat_ref, sel_ref,
                 hs_ref, hb_ref, rt_ref, kt_ref, qt_ref, pt_ref, v_ref, bonus_ref, g_ref, gc_ref, *, rows_are_time):
    p = p_ref[...]
    tm = p.shape[0]
    if rows_are_time:
        last = jnp.where(pl.program_id(0) == 0, 0.0, prev_ref[7:8, :])
        row = lax.broadcasted_iota(jnp.int32, (tm, 1), 0)
        prev = jnp.where(row == 0, last, pltpu.roll(p, 1, axis=0))
    else:
        prev = prev_ref[...]
    xm = p + mu_ref[...] * (prev - p)
    r = xm[:, 0:D_A]
    k = xm[:, D_A:2 * D_A]
    v = xm[:, 2 * D_A:3 * D_A]
    xl = xm[:, 3 * D_A:D_SHIFT]
    lane = lax.broadcasted_iota(jnp.int32, xl.shape, 1)
    act = jnp.where(lane < D_DECAY_LORA, jnp.tanh(xl),
                    jnp.where(lane < D_DECAY_LORA + D_AAA_LORA, xl, _sigmoid(xl)))
    act_hi, act_lo = _split2(act)
    w_hi, w_lo = wl_ref[0], wl_ref[1]
    lora = _dot_bf16(act_hi, w_hi) + _dot_bf16(act_hi, w_lo) + _dot_bf16(act_lo, w_hi)
    y = -(w0_ref[...] + lora[:, 0:D_A])
    softplus = jnp.maximum(y, 0.0) + jnp.log(1.0 + jnp.exp(-jnp.abs(y)))
    logw = -jnp.exp(-softplus - 0.5)
    a = _sigmoid(a0_ref[...] + lora[:, D_A:2 * D_A])
    g = lora[:, 2 * D_A:3 * D_A]
    kkr = k * kk_ref[...]
    kp = k * (1.0 + (a - 1.0) * ka_ref[...])
    seg = _head_sums(jnp.concatenate([kkr * kkr, r * kp * rk_ref[...]], axis=0), hs_ref[...], hb_ref[...])
    kk = kkr / jnp.maximum(jnp.sqrt(seg[0:tm]), 1e-12)
    cum = _dot_exact_lhs(lmat_ref[...], logw)
    e_out = jnp.exp(-cum)
    rt_ref[...] = r * jnp.exp(cum)
    kt_ref[...] = kp * e_out
    qt_ref[...] = kk * a * e_out
    pt_ref[...] = -kk * jnp.exp(cum - logw)
    v_ref[...] = v
    bonus_ref[...] = seg[tm:2 * tm] * v
    g_ref[...] = g
    gc_ref[...] = jnp.exp(_dot_exact_lhs(sel_ref[...], cum))


def _rwkv_prep(proj, prev, tm, chunk, lw, name):
    m = proj.shape[0]
    rows_are_time = prev is None
    n_tiles = m // tm
    t = np.arange(tm)
    lmat = ((t[:, None] // chunk == t[None, :] // chunk) & (t[None, :] <= t[:, None])).astype(np.float32)
    sel_stride = 1 if chunk == 1 else 8
    n_sel = sel_stride * (tm // chunk)
    sel = np.zeros((n_sel, tm), np.float32)
    for c in range(tm // chunk):
        sel[sel_stride * c, (c + 1) * chunk - 1] = 1.0
    if rows_are_time:
        prev_arr = proj
        prev_spec = pl.BlockSpec((8, D_SHIFT), lambda i: (jnp.maximum(i * (tm // 8) - 1, 0), 0))
    else:
        prev_arr = prev
        prev_spec = pl.BlockSpec((tm, D_SHIFT), lambda i: (i, 0))
    vec = lambda d: pl.BlockSpec((1, d), lambda i: (0, 0))
    full = lambda a: pl.BlockSpec(a.shape, lambda i: (0, 0))
    big = pl.BlockSpec((tm, D_A), lambda i: (i, 0))
    out = jax.ShapeDtypeStruct((m, D_A), F32)
    lmat_b = jnp.asarray(lmat, BF16)
    sel_b = jnp.asarray(sel, BF16)
    return pl.pallas_call(
        functools.partial(_prep_kernel, rows_are_time=rows_are_time),
        out_shape=[out] * 7 + [jax.ShapeDtypeStruct((n_tiles * n_sel, D_A), F32)],
        grid=(n_tiles,),
        in_specs=[pl.BlockSpec((tm, D_SHIFT), lambda i: (i, 0)), prev_spec, vec(D_SHIFT),
                  pl.BlockSpec(lw["w_lora"].shape, lambda i: (0, 0, 0)),
                  vec(D_A), vec(D_A), vec(D_A), vec(D_A), vec(D_A), full(lmat_b), full(sel_b), full(lw["hsum"]),
                  full(lw["hbc"])],
        out_specs=[big] * 7 + [pl.BlockSpec((n_sel, D_A), lambda i: (i, 0))],
        compiler_params=_params(("parallel",)),
        name=name,
    )(proj, prev_arr, lw["mu"], lw["w_lora"], lw["w0"], lw["a0"], lw["k_k"], lw["k_a"], lw["r_k"], lmat_b, sel_b,
      lw["hsum"], lw["hbc"])


def _bdot3(a, b, dims):
    ah, al = _split2(a)
    bh, bl = _split2(b)
    (ca,), (cb,) = dims[0]
    a_cat = jnp.concatenate([ah, ah, al, jnp.zeros_like(al)], axis=ca)
    b_cat = jnp.concatenate([bh, bl, bh, jnp.zeros_like(bl)], axis=cb)
    return lax.dot_general(a_cat, b_cat, dims, preferred_element_type=F32)


_NN = (((2,), (1,)), ((0,), (0,)))
_NT = (((2,), (2,)), ((0,), (0,)))


def _scan_kernel(pt_ref, rt_ref, qt_ref, kt_ref, v_ref, gc_ref, o_ref, sfin_ref, s_scr):
    @pl.when(pl.program_id(0) == 0)
    def _():
        s_scr[...] = jnp.zeros_like(s_scr)

    def heads(x):
        return jnp.stack([x[:, h * HEAD_A:(h + 1) * HEAD_A] for h in range(H_A)], axis=0)

    p, r, q, k, v = (heads(ref[...]) for ref in (pt_ref, rt_ref, qt_ref, kt_ref, v_ref))
    gam = heads(gc_ref[0:1, :])
    ri = lax.broadcasted_iota(jnp.int32, (1, CHUNK, CHUNK), 1)
    ci = lax.broadcasted_iota(jnp.int32, (1, CHUNK, CHUNK), 2)
    strict = ri > ci
    incl = ri >= ci
    eye = (ri == ci).astype(F32)

    gram = _bdot3(jnp.concatenate([p, r], axis=1), jnp.concatenate([q, k], axis=1), _NT)
    a_qp = jnp.where(strict, gram[:, 0:CHUNK, 0:CHUNK], 0.0)
    a_kp = jnp.where(strict, gram[:, 0:CHUNK, CHUNK:], 0.0)
    a_rq = jnp.where(incl, gram[:, CHUNK:, 0:CHUNK], 0.0)
    a_rk = jnp.where(incl, gram[:, CHUNK:, CHUNK:], 0.0)
    same = lambda log2_bs: (ri >> log2_bs) == (ci >> log2_bs)
    a_d = jnp.where(same(4), a_qp, 0.0)
    tinv = eye + a_d
    pw = _bdot3(a_d, a_d, _NN)
    for _ in range(2):
        both = _bdot3(jnp.concatenate([pw, tinv], axis=1), pw, _NN)
        pw = both[:, 0:CHUNK]
        tinv = tinv + both[:, CHUNK:]
    tinv = tinv + _bdot3(tinv, pw, _NN)
    for log2_bs in (4, 5):
        off = jnp.where(jnp.logical_and(same(log2_bs + 1), jnp.logical_not(same(log2_bs))), a_qp, 0.0)
        tinv = tinv + _bdot3(tinv, _bdot3(off, tinv, _NN), _NN)
    av = _bdot3(jnp.concatenate([a_kp, a_rk], axis=1), v, _NN)
    pw_hat = _bdot3(tinv, jnp.concatenate([p, av[:, 0:CHUNK]], axis=2), _NN)
    x = _bdot3(a_rq, pw_hat, _NN)
    r_hat = r + x[:, :, 0:HEAD_A]
    o_loc = x[:, :, HEAD_A:] + av[:, CHUNK:]
    y = _bdot3(jnp.swapaxes(pw_hat, 1, 2), q, _NN)
    m_mat = (eye + y[:, 0:HEAD_A]) * gam
    n_mat = (y[:, HEAD_A:] + _bdot3(jnp.swapaxes(v, 1, 2), k, _NN)) * gam

    s0 = s_scr[...]
    o = _bdot3(r_hat, s0, _NT) + o_loc
    s_scr[...] = _bdot3(s0, m_mat, _NN) + n_mat
    o_ref[...] = jnp.concatenate([o[h] for h in range(H_A)], axis=1)

    @pl.when(pl.program_id(0) == pl.num_programs(0) - 1)
    def _():
        sfin_ref[...] = s_scr[...]


def _wkv_scan(pt, rt, qt, kt, v, gc):
    t = pt.shape[0]
    blk = pl.BlockSpec((CHUNK, D_A), lambda c: (c, 0))
    return pl.pallas_call(
        _scan_kernel,
        out_shape=[jax.ShapeDtypeStruct((t, D_A), F32), jax.ShapeDtypeStruct((H_A, HEAD_A, HEAD_A), F32)],
        grid=(t // CHUNK,),
        in_specs=[blk] * 5 + [pl.BlockSpec((8, D_A), lambda c: (c, 0))],
        out_specs=[blk, pl.BlockSpec((H_A, HEAD_A, HEAD_A), lambda c: (0, 0, 0))],
        scratch_shapes=[pltpu.VMEM((H_A, HEAD_A, HEAD_A), F32)],
        compiler_params=_params(("arbitrary",)),
        name="wkv_scan_prompt",
    )(pt, rt, qt, kt, v, gc)


def _wkv_step_kernel(s_ref, rows_ref, sn_ref, o_ref):
    ri = lax.broadcasted_iota(jnp.int32, (1, HEAD_A, HEAD_A), 1)
    ci = lax.broadcasted_iota(jnp.int32, (1, HEAD_A, HEAD_A), 2)
    eye = ri == ci
    pt, rt, qt, kt, v, gc = (rows_ref[0, n] for n in range(6))
    s = s_ref[0, 0]
    u = jnp.sum(s * pt, axis=2, keepdims=True)
    v_col = jnp.sum(jnp.where(eye, v, 0.0), axis=2, keepdims=True)
    m = s + u * qt + v_col * kt
    sn_ref[0, 0] = m * gc
    o_col = jnp.sum(m * rt, axis=2, keepdims=True)
    o_ref[0] = jnp.sum(jnp.where(eye, o_col, 0.0), axis=1, keepdims=True)


def _wkv_step(state, pt, rt, qt, kt, v, gc):
    b = pt.shape[0]
    rows = jnp.stack([pt, rt, qt, kt, v, gc], axis=1).reshape(b, 6, H_A, 1, HEAD_A)
    st_spec = pl.BlockSpec((1, 1, H_A, HEAD_A, HEAD_A), lambda i: (0, i, 0, 0, 0))
    sn, o = pl.pallas_call(
        _wkv_step_kernel,
        out_shape=[jax.ShapeDtypeStruct(state.shape, F32), jax.ShapeDtypeStruct((b, H_A, 1, HEAD_A), F32)],
        grid=(b,),
        in_specs=[st_spec, pl.BlockSpec((1, 6, H_A, 1, HEAD_A), lambda i: (i, 0, 0, 0, 0))],
        out_specs=[st_spec, pl.BlockSpec((1, H_A, 1, HEAD_A), lambda i: (i, 0, 0, 0))],
        compiler_params=_params(("parallel",)),
        name="wkv_step_sample",
    )(state, rows)
    return sn, o.reshape(b, D_A)


def _post_kernel(o_ref, bonus_ref, g_ref, gng_ref, gnb_ref, hs_ref, hb_ref, y_ref):
    o = o_ref[...]
    hsum, hbc = hs_ref[...], hb_ref[...]
    mu = _head_sums(o, hsum, hbc) * (1.0 / HEAD_A)
    d = o - mu
    var = _head_sums(d * d, hsum, hbc) * (1.0 / HEAD_A)
    o_n = d * lax.rsqrt(var + EPS_GN) * gng_ref[...] + gnb_ref[...]
    y_ref[...] = ((o_n + bonus_ref[...]) * g_ref[...]).astype(y_ref.dtype)


def _rwkv_post(o, bonus, g, lw, tm, name):
    m = o.shape[0]
    big = pl.BlockSpec((tm, D_A), lambda i: (i, 0))
    vec = pl.BlockSpec((1, D_A), lambda i: (0, 0))
    return pl.pallas_call(
        _post_kernel,
        out_shape=jax.ShapeDtypeStruct((m, D_A), BF16),
        grid=(m // tm,),
        in_specs=[big, big, big, vec, vec, pl.BlockSpec((D_A, LANES), lambda i: (0, 0)),
                  pl.BlockSpec((LANES, D_A), lambda i: (0, 0))],
        out_specs=big,
        compiler_params=_params(("parallel",)),
        name=name,
    )(o, bonus, g, lw["gn_g"], lw["gn_b"], lw["hsum"], lw["hbc"])


def _attn_kernel(q_ref, kc_ref, vc_ref, kh_ref, vh_ref, bias_ref, o_ref, l_ref, *, dil, m_blocks, tiled):
    scale = HEAD_B ** -0.5
    n_streams = SUBLANES if tiled else dil
    n_units = n_streams * m_blocks

    stride = SUBLANES if tiled else dil
    if tiled:
        flat = lambda ref: ref.reshape(ref.shape[0] * SUBLANES, LANES)
        q_ref, kc_ref, vc_ref, kh_ref, vh_ref, o_ref, l_ref = map(flat, (q_ref, kc_ref, vc_ref, kh_ref, vh_ref,
                                                                          o_ref, l_ref))

    def rows(ref, r, start, size):
        if stride == 1:
            return ref[pl.ds(start, size), :]
        return ref[pl.ds(r + stride * start, size, stride=stride), :]

    def put(ref, r, start, val):
        if stride == 1:
            ref[pl.ds(start, BLK), :] = val
        else:
            ref[pl.ds(r + stride * start, BLK, stride=stride), :] = val

    qs, ks, vs, units = [], [], [], []
    for r in range(n_streams):
        for mb in range(m_blocks):
            units.append((r, mb))
            qs.append(rows(q_ref, r, BLK * mb, BLK))
            if mb == 0:
                ks.append(jnp.concatenate([rows(kh_ref, r, 0, BLK), rows(kc_ref, r, 0, BLK)], axis=0))
                vs.append(jnp.concatenate([rows(vh_ref, r, 0, BLK), rows(vc_ref, r, 0, BLK)], axis=0))
            else:
                ks.append(rows(kc_ref, r, BLK * (mb - 1), 2 * BLK))
                vs.append(rows(vc_ref, r, BLK * (mb - 1), 2 * BLK))

    def batch(xs, mul=None):
        pre = (lambda x: x) if mul is None else (lambda x: x * mul)
        return jnp.stack([pre(x[:, sub * HEAD_B:(sub + 1) * HEAD_B]).astype(BF16) for sub in range(2) for x in xs],
                         axis=0)

    assert math.frexp(scale)[0] == 0.5
    qb, kb, vb = batch(qs, scale), batch(ks), batch(vs)
    s = lax.dot_general(qb, kb, (((2,), (2,)), ((0,), (0,))), preferred_element_type=F32)
    s = s.reshape(2, n_units, BLK, 2 * BLK) + bias_ref[...][:, None]
    unit = lax.broadcasted_iota(jnp.int32, (1, n_units, 1, 2 * BLK), 1)
    col = lax.broadcasted_iota(jnp.int32, (1, n_units, 1, 2 * BLK), 3)
    no_prev = jnp.logical_and(jnp.logical_and(pl.program_id(0) == 0, (unit & (m_blocks - 1)) == 0), col < BLK)
    s = jnp.where(no_prev, -jnp.inf, s)
    m = jnp.max(s, axis=-1, keepdims=True)
    p = jnp.exp(s - m)
    l = jnp.sum(p, axis=-1, keepdims=True)
    pv = lax.dot_general(p.astype(BF16).reshape(2 * n_units, BLK, 2 * BLK), vb, (((2,), (1,)), ((0,), (0,))),
                         preferred_element_type=F32)
    o = pv.reshape(2, n_units, BLK, HEAD_B) / l
    lse = jnp.broadcast_to(m + jnp.log(l), (2, n_units, BLK, HEAD_B))
    for u, (r, mb) in enumerate(units):
        put(o_ref, r, BLK * mb, jnp.concatenate([o[0, u], o[1, u]], axis=1))
        put(l_ref, r, BLK * mb, jnp.concatenate([lse[0, u], lse[1, u]], axis=1))


def _attn_prompt(proj, bias, gi, rows_per_step):
    t, n_cols = proj.shape
    dil = DILATIONS[gi]
    span = BLK * dil
    tiled = dil % SUBLANES == 0
    if tiled:
        rows_per_step *= dil // SUBLANES
    m_blocks = rows_per_step // span
    cq = (COL_QKV + gi * D_G) // LANES
    ck = (COL_QKV + D_B + gi * D_G) // LANES
    cv = (COL_QKV + 2 * D_B + gi * D_G) // LANES
    bias_block = (2, BLK, 2 * BLK)
    if tiled:
        n_hi = dil // SUBLANES
        src = proj.reshape(t // dil, n_hi, SUBLANES, n_cols)
        grid = (t // rows_per_step, n_hi, H_G // 2)
        l_rows = rows_per_step // dil
        cur = lambda c0: pl.BlockSpec((l_rows, None, SUBLANES, LANES), lambda i, rh, hp: (i, rh, 0, c0 + hp))
        halo = lambda c0: pl.BlockSpec((BLK, None, SUBLANES, LANES),
                                       lambda i, rh, hp: (jnp.maximum(i * m_blocks - 1, 0), rh, 0, c0 + hp))
        out_spec = pl.BlockSpec((l_rows, None, SUBLANES, LANES), lambda i, rh, hp: (i, rh, 0, hp))
        out = jax.ShapeDtypeStruct((t // dil, n_hi, SUBLANES, D_G), F32)
        bias_spec = pl.BlockSpec(bias_block, lambda i, rh, hp: (hp, 0, 0))
    else:
        src = proj
        grid = (t // rows_per_step, H_G // 2)
        cur = lambda c0: pl.BlockSpec((rows_per_step, LANES), lambda i, hp: (i, c0 + hp))
        halo = lambda c0: pl.BlockSpec((span, LANES), lambda i, hp: (jnp.maximum(i * m_blocks - 1, 0), c0 + hp))
        out_spec = pl.BlockSpec((rows_per_step, LANES), lambda i, hp: (i, hp))
        out = jax.ShapeDtypeStruct((t, D_G), F32)
        bias_spec = pl.BlockSpec(bias_block, lambda i, hp: (hp, 0, 0))
    o, l = pl.pallas_call(
        functools.partial(_attn_kernel, dil=dil, m_blocks=m_blocks, tiled=tiled),
        out_shape=[out, out],
        grid=grid,
        in_specs=[cur(cq), cur(ck), cur(cv), halo(ck), halo(cv), bias_spec],
        out_specs=[out_spec, out_spec],
        compiler_params=_params(("parallel",) * len(grid)),
        name=f"attn_prompt_g{gi}",
    )(src, src, src, src, src, bias)
    return o.reshape(t, D_G), l.reshape(t, D_G)


def _attn_step_kernel(q_ref, kn_ref, vn_ref, c_ref, bias_ref, bias0_ref, o_ref, l_ref):
    scale = HEAD_B ** -0.5
    ri = lax.broadcasted_iota(jnp.int32, (1, 1, HEAD_B, HEAD_B), 2)
    ci = lax.broadcasted_iota(jnp.int32, (1, 1, HEAD_B, HEAD_B), 3)
    eye = ri == ci
    q = q_ref[...]
    q_col = jnp.sum(jnp.where(eye, q, 0.0), axis=3, keepdims=True)
    s = jnp.sum(c_ref[:, 0] * q_col, axis=2, keepdims=True) * scale + bias_ref[...]
    s_new = jnp.sum(q * kn_ref[...], axis=3, keepdims=True) * scale + bias0_ref[...]
    m = jnp.maximum(jnp.max(s, axis=3, keepdims=True), s_new)
    p = jnp.exp(s - m)
    p_new = jnp.exp(s_new - m)
    l = jnp.sum(p, axis=3, keepdims=True) + p_new
    o_col = jnp.sum(c_ref[:, 1] * p, axis=3, keepdims=True)
    o_row = jnp.sum(jnp.where(eye, o_col, 0.0), axis=2, keepdims=True)
    o_ref[...] = (o_row + p_new * vn_ref[...]) / l
    l_ref[...] = jnp.broadcast_to(m + jnp.log(l), o_ref.shape)


def _attn_step(q, k_new, v_new, cache, bias, bias0, gi):
    b = q.shape[0]
    w = cache.shape[2]
    bt = max(1, min(b, STEP_CACHE_WORDS // (2 * D_G * w)))
    cache_t = jnp.transpose(cache, (0, 1, 3, 4, 5, 2))
    vec = pl.BlockSpec((bt, H_G, 1, HEAD_B), lambda i: (i, 0, 0, 0))
    out = jax.ShapeDtypeStruct((b, H_G, 1, HEAD_B), F32)
    o, l = pl.pallas_call(
        _attn_step_kernel,
        out_shape=[out, out],
        grid=(b // bt,),
        in_specs=[vec, vec, vec,
                  pl.BlockSpec((None, bt, 2, H_G, HEAD_B, w), lambda i: (0, i, 0, 0, 0, 0)),
                  pl.BlockSpec((H_G, 1, w), lambda i: (0, 0, 0)),
                  pl.BlockSpec((H_G, 1, 1), lambda i: (0, 0, 0))],
        out_specs=[vec, vec],
        compiler_params=_params(("parallel",)),
        name=f"attn_step_g{gi}",
    )(q, k_new, v_new, cache_t, bias, bias0)
    return o.reshape(b, D_G), l.reshape(b, D_G)


def _mix_kernel(ya_ref, o1_ref, o2_ref, o3_ref, l1_ref, l2_ref, l3_ref, wa_ref, wb_ref, ga_ref, gb_ref, o_ref,
                yb_scr):
    @pl.when(pl.program_id(1) == 0)
    def _():
        l1, l2, l3 = l1_ref[...], l2_ref[...], l3_ref[...]
        m = jnp.maximum(jnp.maximum(l1, l2), l3)
        e1, e2, e3 = jnp.exp(l1 - m), jnp.exp(l2 - m), jnp.exp(l3 - m)
        den = e1 + e2 + e3
        yb = (e1 / den) * o1_ref[...] + (e2 / den) * o2_ref[...] + (e3 / den) * o3_ref[...]
        yb_scr[...] = yb.astype(BF16)

    mixed = ga_ref[...] * _dot_bf16(ya_ref[...], wa_ref[...]) + gb_ref[...] * _dot_bf16(yb_scr[...], wb_ref[...])
    o_ref[...] = mixed.astype(o_ref.dtype)


def _mix(ya, outs, lses, wa, wb, proj, tm, tn, name):
    m = ya.shape[0]
    ga0 = COL_GATE // tn
    gb0 = (COL_GATE + D_MODEL) // tn
    grp = pl.BlockSpec((tm, D_G), lambda i, j: (i, 0))
    return pl.pallas_call(
        _mix_kernel,
        out_shape=jax.ShapeDtypeStruct((m, D_MODEL), BF16),
        grid=(m // tm, D_MODEL // tn),
        in_specs=[pl.BlockSpec((tm, D_A), lambda i, j: (i, 0))] + [grp] * 6 + [
            pl.BlockSpec((D_A, tn), lambda i, j: (0, j)),
            pl.BlockSpec((D_G, tn), lambda i, j: (0, j)),
            pl.BlockSpec((tm, tn), lambda i, j: (i, ga0 + j)),
            pl.BlockSpec((tm, tn), lambda i, j: (i, gb0 + j)),
        ],
        out_specs=pl.BlockSpec((tm, tn), lambda i, j: (i, j)),
        scratch_shapes=[pltpu.VMEM((tm, D_G), BF16)],
        compiler_params=_params(("parallel", "arbitrary")),
        name=name,
    )(ya, *outs, *lses, wa, wb, proj, proj)


def _wo_kernel(a_ref, w_ref, x_ref, g_ref, h_ref, hn_ref):
    h = x_ref[...] + _dot_bf16(a_ref[...], w_ref[...])
    h_ref[...] = h
    hn_ref[...] = _rms(h, g_ref[...]).astype(BF16)


def _wo(a, w, x, g, tm, name):
    m, k = a.shape
    n = w.shape[1]
    row = pl.BlockSpec((tm, n), lambda i: (i, 0))
    return pl.pallas_call(
        _wo_kernel,
        out_shape=[jax.ShapeDtypeStruct((m, n), F32), jax.ShapeDtypeStruct((m, n), BF16)],
        grid=(m // tm,),
        in_specs=[pl.BlockSpec((tm, k), lambda i: (i, 0)), pl.BlockSpec((k, n), lambda i: (0, 0)), row,
                  pl.BlockSpec((1, n), lambda i: (0, 0))],
        out_specs=[row, row],
        compiler_params=_params(("parallel",)),
        name=name,
    )(a, w, x, g.reshape(1, n))


def _gelu(x):
    return 0.5 * x * (1.0 + lax.erf(x * (1.0 / math.sqrt(2.0))))


def _ffn_up_kernel(h_ref, w1_ref, w2_ref, cw1_ref, cw2_ref, cb1_ref, cb2_ref, act_ref, t1_ref, t2_ref,
                   c1_scr, c2_scr, w1_scr, w2_scr):
    tm = h_ref.shape[0]

    @pl.when(pl.program_id(1) == 0)
    def _():
        c1_scr[...] = jnp.zeros_like(c1_scr)
        c2_scr[...] = jnp.zeros_like(c2_scr)
        w1_scr[...] = w1_ref[...].astype(BF16)
        w2_scr[...] = w2_ref[...].astype(BF16)

    a = h_ref[...]
    row = lax.broadcasted_iota(jnp.int32, (tm, 1), 0)

    def conv(u, carry_ref, cw_ref, cb_ref):
        m1 = jnp.where(row == 0, carry_ref[7:8, :], pltpu.roll(u, 1, axis=0))
        m2 = jnp.where(row == 0, carry_ref[6:7, :], jnp.where(row == 1, carry_ref[7:8, :], pltpu.roll(u, 2, axis=0)))
        return cb_ref[...] + cw_ref[0:1, :] * m2 + cw_ref[1:2, :] * m1 + cw_ref[2:3, :] * u

    u1 = _dot_bf16(a, w1_scr[...])
    u2 = _dot_bf16(a, w2_scr[...])
    c1 = conv(u1, c1_scr, cw1_ref, cb1_ref)
    c2 = conv(u2, c2_scr, cw2_ref, cb2_ref)
    act_ref[...] = (_gelu(c1) * c2).astype(act_ref.dtype)
    c1_scr[...] = u1[tm - 8:tm]
    c2_scr[...] = u2[tm - 8:tm]
    t1_ref[...] = u1[tm - 8:tm]
    t2_ref[...] = u2[tm - 8:tm]


def _ffn_up_prompt(hn, w_up, conv_w, conv_b, tm, tn):
    m = hn.shape[0]
    nj = D_FF // tn
    tail = jax.ShapeDtypeStruct((8, D_FF), F32)
    return pl.pallas_call(
        _ffn_up_kernel,
        out_shape=[jax.ShapeDtypeStruct((m, D_FF), BF16), tail, tail],
        grid=(nj, m // tm),
        in_specs=[
            pl.BlockSpec((tm, D_MODEL), lambda j, i: (i, 0)),
            pl.BlockSpec((D_MODEL, tn), lambda j, i: (0, j)),
            pl.BlockSpec((D_MODEL, tn), lambda j, i: (0, nj + j)),
            pl.BlockSpec((CONV_W, tn), lambda j, i: (0, j)),
            pl.BlockSpec((CONV_W, tn), lambda j, i: (0, nj + j)),
            pl.BlockSpec((1, tn), lambda j, i: (0, j)),
            pl.BlockSpec((1, tn), lambda j, i: (0, nj + j)),
        ],
        out_specs=[pl.BlockSpec((tm, tn), lambda j, i: (i, j)),
                   pl.BlockSpec((8, tn), lambda j, i: (0, j)),
                   pl.BlockSpec((8, tn), lambda j, i: (0, j))],
        scratch_shapes=[pltpu.VMEM((8, tn), F32), pltpu.VMEM((8, tn), F32),
                        pltpu.VMEM((D_MODEL, tn), BF16), pltpu.VMEM((D_MODEL, tn), BF16)],
        compiler_params=_params(("arbitrary", "arbitrary")),
        name="ffn_up_prompt",
    )(hn, w_up, w_up, conv_w, conv_w, conv_b, conv_b)


def _ffn_act_step_kernel(up_ref, prev_ref, cw_ref, cb_ref, act_ref):
    up = up_ref[...]
    w = 2 * D_FF
    c = cb_ref[...] + cw_ref[0:1, :] * prev_ref[:, 0:w] + cw_ref[1:2, :] * prev_ref[:, w:2 * w] + cw_ref[2:3, :] * up
    act_ref[...] = (_gelu(c[:, 0:D_FF]) * c[:, D_FF:w]).astype(act_ref.dtype)


def _ffn_act_step(up, conv_prev, conv_w, conv_b):
    b = up.shape[0]
    return pl.pallas_call(
        _ffn_act_step_kernel,
        out_shape=jax.ShapeDtypeStruct((b, D_FF), BF16),
        compiler_params=pltpu.CompilerParams(vmem_limit_bytes=VMEM_LIMIT),
        name="ffn_act_sample",
    )(up, conv_prev.reshape(b, (CONV_W - 1) * 2 * D_FF), conv_w, conv_b)


def _down_kernel(a_ref, w_ref, h_ref, g_ref, o_ref, acc_ref):
    @pl.when(pl.program_id(1) == 0)
    def _():
        acc_ref[...] = h_ref[...]

    out = acc_ref[...] + _dot_bf16(a_ref[...], w_ref[...])
    acc_ref[...] = out
    inv = lax.rsqrt(jnp.mean(out * out, axis=-1, keepdims=True) + EPS_RMS)
    o_ref[...] = out * inv * g_ref[...]


def _ffn_down(act, w_down, h, g, tm, tk, name):
    m = act.shape[0]
    nk = D_FF // tk
    return pl.pallas_call(
        _down_kernel,
        out_shape=jax.ShapeDtypeStruct((m, D_MODEL), F32),
        grid=(m // tm, nk),
        in_specs=[
            pl.BlockSpec((tm, tk), lambda i, k: (i, k)),
            pl.BlockSpec((tk, D_MODEL), lambda i, k: (k, 0)),
            pl.BlockSpec((tm, D_MODEL), lambda i, k: (i, 0)),
            pl.BlockSpec((1, D_MODEL), lambda i, k: (0, 0)),
        ],
        out_specs=pl.BlockSpec((tm, D_MODEL), lambda i, k: (i, 0)),
        scratch_shapes=[pltpu.VMEM((tm, D_MODEL), F32)],
        compiler_params=_params(("parallel", "arbitrary")),
        name=name,
    )(act, w_down, h, g.reshape(1, D_MODEL))


def _rel_bucket(dist):
    max_exact = N_BUCKETS // 2
    d_f = jnp.maximum(dist, 1).astype(F32)
    large = max_exact + (jnp.log(d_f / max_exact) / math.log(MAX_DISTANCE / max_exact)
                         * (N_BUCKETS - max_exact)).astype(jnp.int32)
    large = jnp.minimum(large, N_BUCKETS - 1)
    return jnp.where(dist < max_exact, dist, large)


def _bias_rows(tab, dist):
    onehot = (_rel_bucket(dist)[None, :] == jnp.arange(N_BUCKETS)[:, None]).astype(F32)
    return jnp.dot(tab.T, onehot, precision=HIGHEST)


def _bias_tables(rel_bias, gi):
    dil = DILATIONS[gi]
    win = WINDOWS[gi]
    reach = win // dil
    tab = rel_bias[:, gi * H_G:(gi + 1) * H_G]
    qi = jnp.arange(BLK)[:, None]
    ki = jnp.arange(2 * BLK)[None, :]
    rel = qi + BLK - ki
    blk = _bias_rows(tab, (dil * jnp.maximum(rel, 0)).reshape(-1)).reshape(H_G, BLK, 2 * BLK)
    blk = jnp.where(((rel >= 0) & (rel <= reach))[None], blk, -jnp.inf)
    back = win - jnp.arange(win)
    step = jnp.where((back % dil == 0)[None, :], _bias_rows(tab, back), -jnp.inf)[:, None, :]
    step0 = _bias_rows(tab, jnp.zeros((1,), jnp.int32))[:, None, :]
    return blk, step, step0


def _tiles(m, prompt):
    if prompt:
        return dict(tag="prompt", proj_m=1024, prep_m=2 * CHUNK, post_m=256, attn_rows=2048, mix_m=512, mix_n=1024,
                    wo_m=512, up_m=1024, up_n=512, down_m=512, down_k=D_FF // 2)
    return dict(tag="sample", proj_m=m, prep_m=m, post_m=m, mix_m=m, mix_n=1024, wo_m=m, up_n=512, down_m=m,
                down_k=D_FF // 2)


def _layer(x, lw, bias_blk, prompt, state=None):
    m = x.shape[0]
    tiles = _tiles(m, prompt)
    proj = _proj(x, lw["norm1_g"], lw["w_in"], lw["in_bias"], tiles["proj_m"], "proj_" + tiles["tag"])

    if prompt:
        rt, kt, qt, pt, v, bonus, g, gc = _rwkv_prep(proj, None, tiles["prep_m"], CHUNK, lw, "rwkv_prep_prompt")
        o, wkv_new = _wkv_scan(pt, rt, qt, kt, v, gc)
        wkv_new = wkv_new[None, None]
    else:
        rt, kt, qt, pt, v, bonus, g, gc = _rwkv_prep(proj, state["shift"], tiles["prep_m"], 1, lw, "rwkv_prep_sample")
        wkv_new, o = _wkv_step(state["wkv"], pt, rt, qt, kt, v, gc)
    ya = _rwkv_post(o, bonus, g, lw, tiles["post_m"], "rwkv_post_" + tiles["tag"])

    outs, lses = [], []
    for gi in range(N_GROUPS):
        if prompt:
            o_g, l_g = _attn_prompt(proj, bias_blk[gi][0], gi, tiles["attn_rows"])
        else:
            sl = lambda c0: proj[:, c0 + gi * D_G:c0 + (gi + 1) * D_G].reshape(m, H_G, 1, HEAD_B)
            o_g, l_g = _attn_step(sl(COL_QKV), sl(COL_QKV + D_B), sl(COL_QKV + 2 * D_B), state["win"][gi],
                                  bias_blk[gi][1], bias_blk[gi][2], gi)
        outs.append(o_g)
        lses.append(l_g)
    mixed = _mix(ya, outs, lses, lw["w_out_a"], lw["w_out_b"], proj, tiles["mix_m"], tiles["mix_n"],
                 "mix_" + tiles["tag"])
    h, hn = _wo(mixed, lw["w_o"], x, lw["norm2_g"], tiles["wo_m"], "wo_" + tiles["tag"])

    if prompt:
        act, t1, t2 = _ffn_up_prompt(hn, lw["w_up"], lw["conv_w"], lw["conv_b"], tiles["up_m"], tiles["up_n"])
        conv_new = jnp.concatenate([t1[6:8], t2[6:8]], axis=1)[None, None]
    else:
        up = _matmul(hn, lw["w_up"], m, tiles["up_n"], "up_sample")
        act = _ffn_act_step(up, state["conv"], lw["conv_w"], lw["conv_b"])
        conv_new = jnp.concatenate([state["conv"][:, 1:], up[:, None, :]], axis=1)[None]
    y = _ffn_down(act, lw["w_down"], h, lw["normf_g"], tiles["down_m"], tiles["down_k"], "down_" + tiles["tag"])
    return y, proj, wkv_new, conv_new


def kernel(x_prompt, x_sample, state_wkv, state_shift, state_ffn_conv, cache_win1, cache_win2, cache_win3, rel_bias,
           norm1_g, w_in, gate_b, mu_shift, w0, w_up_decay, a0, w_up_aaa, w_up_gate, k_k, k_a, r_k, gn_g, gn_b,
           w_out_a, w_out_b, w_o, norm2_g, w_up, conv_w, conv_b, w_down, normf_g):
    row = lambda a: a.reshape(1, -1)
    w_lora = jnp.zeros((D_LORA, 3 * D_A), F32)
    w_lora = w_lora.at[0:D_DECAY_LORA, 0:D_A].set(w_up_decay[0])
    w_lora = w_lora.at[D_DECAY_LORA:D_DECAY_LORA + D_AAA_LORA, D_A:2 * D_A].set(w_up_aaa[0])
    w_lora = w_lora.at[D_DECAY_LORA + D_AAA_LORA:, 2 * D_A:].set(w_up_gate[0])
    head = np.arange(D_A) // HEAD_A
    lw = dict(
        norm1_g=norm1_g[0], norm2_g=norm2_g[0], normf_g=normf_g,
        w_in=_cast_w_in(w_in[0]),
        in_bias=jnp.concatenate([jnp.zeros((1, COL_GATE), F32), row(gate_b[0])], axis=1),
        mu=row(mu_shift[0]), w_lora=jnp.stack(_split2(w_lora)), w0=row(w0[0]), a0=row(a0[0]), k_k=row(k_k[0]), k_a=row(k_a[0]),
        r_k=row(r_k[0]), gn_g=row(gn_g[0]), gn_b=row(gn_b[0]),
        hsum=jnp.asarray(head[:, None] == np.arange(LANES)[None, :], BF16),
        hbc=jnp.asarray(np.arange(LANES)[:, None] == head[None, :], BF16),
        w_out_a=w_out_a[0].astype(BF16), w_out_b=w_out_b[0].astype(BF16), w_o=w_o[0].astype(BF16),
        w_up=w_up[0], conv_w=conv_w[0], conv_b=row(conv_b[0]), w_down=w_down[0].astype(BF16),
    )
    bias = [_bias_tables(rel_bias, gi) for gi in range(N_GROUPS)]

    y_p, proj_p, wkv_p, conv_p = _layer(x_prompt[0], lw, bias, prompt=True)

    b = DEC_BATCH
    state = dict(wkv=state_wkv, shift=state_shift.reshape(b, D_SHIFT), conv=state_ffn_conv[0],
                 win=(cache_win1, cache_win2, cache_win3))
    y_s, proj_s, wkv_s, conv_s = _layer(x_sample[:, 0], lw, bias, prompt=False, state=state)

    def kv_rows(proj, lo, gi):
        k = proj[lo:, COL_QKV + D_B + gi * D_G:COL_QKV + D_B + (gi + 1) * D_G]
        v = proj[lo:, COL_QKV + 2 * D_B + gi * D_G:COL_QKV + 2 * D_B + (gi + 1) * D_G]
        n = k.shape[0]
        return jnp.stack([k.reshape(n, H_G, HEAD_B), v.reshape(n, H_G, HEAD_B)], axis=1)

    win_p = [kv_rows(proj_p, SEQ - min(WINDOWS[gi], SEQ), gi)[None, None] for gi in range(N_GROUPS)]
    win_s = [kv_rows(proj_s, 0, gi)[None, :, None] for gi in range(N_GROUPS)]
    return (y_p[None], y_s[:, None],
            wkv_p, wkv_s,
            proj_p[SEQ - 1:, 0:D_SHIFT][None, None], proj_s[:, 0:D_SHIFT][None, :, None],
            conv_p, conv_s,
            win_p[0], win_s[0], win_p[1], win_s[1], win_p[2], win_s[2])
```

```python
import functools
import math

import numpy as np
import jax
import jax.numpy as jnp
from jax import lax
from jax.experimental import pallas as pl
from jax.experimental.pallas import tpu as pltpu

F32 = jnp.float32
BF16 = jnp.bfloat16
HIGHEST = lax.Precision.HIGHEST

D_MODEL = 2048
SEQ = 8192
DEC_BATCH = 32
HEAD_A = 64
H_A = 16
D_A = H_A * HEAD_A
D_DECAY_LORA = 96
D_AAA_LORA = 96
D_GATE_LORA = 64
D_LORA = D_DECAY_LORA + D_AAA_LORA + D_GATE_LORA
D_SHIFT = 3 * D_A + D_LORA
EPS_GN = 64e-5
HEAD_B = 64
H_G = 8
D_G = H_G * HEAD_B
WINDOWS = (128, 512, 2048)
DILATIONS = (1, 4, 16)
N_GROUPS = 3
D_B = N_GROUPS * D_G
BLK = 128
N_BUCKETS = 32
MAX_DISTANCE = 2048
COL_QKV = D_SHIFT
COL_GATE_SRC = D_SHIFT + 3 * D_B
D_IN = COL_GATE_SRC + 2 * D_MODEL
PROJ_TN = 1024
COL_GATE = -(-COL_GATE_SRC // PROJ_TN) * PROJ_TN
D_IN_PAD = COL_GATE + 2 * D_MODEL
D_FF = 5632
CONV_W = 3
EPS_RMS = 1e-6
CHUNK = 64
LANES = 128
SUBLANES = 8
VMEM_LIMIT = 56 * 1024 * 1024
STEP_CACHE_WORDS = 2 * 1024 * 1024


def _params(sem, vmem=VMEM_LIMIT):
    return pltpu.CompilerParams(dimension_semantics=sem, vmem_limit_bytes=vmem)


def _sigmoid(x):
    return 1.0 / (1.0 + jnp.exp(-x))


def _dot_bf16(a, b):
    return jnp.dot(a, b, preferred_element_type=F32)


def _split2(x):
    hi = x.astype(BF16)
    return hi, (x - hi.astype(F32)).astype(BF16)


def _split3(x):
    hi = x.astype(BF16)
    r1 = x - hi.astype(F32)
    mid = r1.astype(BF16)
    lo = (r1 - mid.astype(F32)).astype(BF16)
    return hi, mid, lo


def _dot_exact_rhs(x, m):
    hi, mid, lo = _split3(x)
    return _dot_bf16(hi, m) + _dot_bf16(mid, m) + _dot_bf16(lo, m)


def _head_sums(x, hsum, hbc):
    return _dot_exact_rhs(_dot_exact_rhs(x, hsum), hbc)


def _dot_exact_lhs(m, x):
    hi, mid, lo = _split3(x)
    return _dot_bf16(m, hi) + _dot_bf16(m, mid) + _dot_bf16(m, lo)


def _rms(x, g):
    return x * lax.rsqrt(jnp.mean(x * x, axis=-1, keepdims=True) + EPS_RMS) * g


def _proj_cast_kernel(x_ref, g_ref, w_ref, b_ref, o_ref, wb_ref, xn_scr, *, gap_block):
    j = pl.program_id(0)

    @pl.when(j == 0)
    def _():
        xn_scr[...] = _rms(x_ref[...], g_ref[...]).astype(BF16)

    wb = jnp.where(j != gap_block, w_ref[...], 0.0).astype(BF16)
    wb_ref[...] = wb
    acc = _dot_bf16(xn_scr[...], wb)
    tn = o_ref.shape[1]
    col = lax.broadcasted_iota(jnp.int32, (1, tn), 1) + j * tn
    o_ref[...] = jnp.where(col >= COL_GATE, _sigmoid(acc + b_ref[...]), acc)


def _proj_cast(x, g, w, bias):
    m, k = x.shape
    tn = COL_GATE - COL_GATE_SRC
    assert tn % LANES == 0 and COL_GATE_SRC % tn == 0
    gap_block = COL_GATE_SRC // tn
    return pl.pallas_call(
        functools.partial(_proj_cast_kernel, gap_block=gap_block),
        out_shape=[jax.ShapeDtypeStruct((m, D_IN_PAD), F32), jax.ShapeDtypeStruct((k, D_IN_PAD), BF16)],
        grid=(D_IN_PAD // tn,),
        in_specs=[
            pl.BlockSpec((m, k), lambda j: (0, 0)),
            pl.BlockSpec((1, k), lambda j: (0, 0)),
            pl.BlockSpec((k, tn), lambda j: (0, jnp.where(j < gap_block, j, jnp.maximum(j - 1, 0)))),
            pl.BlockSpec((1, tn), lambda j: (0, j)),
        ],
        out_specs=[pl.BlockSpec((m, tn), lambda j: (0, j)), pl.BlockSpec((k, tn), lambda j: (0, j))],
        scratch_shapes=[pltpu.VMEM((m, k), BF16)],
        compiler_params=_params(("arbitrary",)),
        name="proj_sample_cast_w_in",
    )(x, g.reshape(1, k), w, bias)


def _proj_kernel(x_ref, g_ref, w_ref, b_ref, o_ref, xn_scr):
    @pl.when(pl.program_id(1) == 0)
    def _():
        xn_scr[...] = _rms(x_ref[...], g_ref[...]).astype(BF16)

    acc = _dot_bf16(xn_scr[...], w_ref[...])
    tn = o_ref.shape[1]
    col = lax.broadcasted_iota(jnp.int32, (1, tn), 1) + pl.program_id(1) * tn
    o_ref[...] = jnp.where(col >= COL_GATE, _sigmoid(acc + b_ref[...]), acc)


def _proj(x, g, w, bias, tm, name):
    m, k = x.shape
    n = w.shape[1]
    return pl.pallas_call(
        _proj_kernel,
        out_shape=jax.ShapeDtypeStruct((m, n), F32),
        grid=(m // tm, n // PROJ_TN),
        in_specs=[
            pl.BlockSpec((tm, k), lambda i, j: (i, 0)),
            pl.BlockSpec((1, k), lambda i, j: (0, 0)),
            pl.BlockSpec((k, PROJ_TN), lambda i, j: (0, j)),
            pl.BlockSpec((1, PROJ_TN), lambda i, j: (0, j)),
        ],
        out_specs=pl.BlockSpec((tm, PROJ_TN), lambda i, j: (i, j)),
        scratch_shapes=[pltpu.VMEM((tm, k), BF16)],
        compiler_params=_params(("parallel", "arbitrary")),
        name=name,
    )(x, g.reshape(1, k), w, bias)


def _matmul_kernel(a_ref, w_ref, o_ref):
    o_ref[...] = _dot_bf16(a_ref[...], w_ref[...].astype(BF16))


def _matmul(a, w, tm, tn, name):
    m, k = a.shape
    n = w.shape[1]
    return pl.pallas_call(
        _matmul_kernel,
        out_shape=jax.ShapeDtypeStruct((m, n), F32),
        grid=(m // tm, n // tn),
        in_specs=[pl.BlockSpec((tm, k), lambda i, j: (i, 0)), pl.BlockSpec((k, tn), lambda i, j: (0, j))],
        out_specs=pl.BlockSpec((tm, tn), lambda i, j: (i, j)),
        compiler_params=_params(("parallel", "arbitrary")),
        name=name,
    )(a, w)


def _prep_kernel(p_ref, prev_ref, mu_ref, wl_ref, w0_ref, a0_ref, kk_ref, ka_ref, rk_ref, lmat_ref, sel_ref,
                 hs_ref, hb_ref, rt_ref, kt_ref, qt_ref, pt_ref, v_ref, bonus_ref, g_ref, gc_ref, *, rows_are_time):
    p = p_ref[...]
    tm = p.shape[0]
    if rows_are_time:
        last = jnp.where(pl.program_id(0) == 0, 0.0, prev_ref[7:8, :])
        row = lax.broadcasted_iota(jnp.int32, (tm, 1), 0)
        prev = jnp.where(row == 0, last, pltpu.roll(p, 1, axis=0))
    else:
        prev = prev_ref[...]
    xm = p + mu_ref[...] * (prev - p)
    r = xm[:, 0:D_A]
    k = xm[:, D_A:2 * D_A]
    v = xm[:, 2 * D_A:3 * D_A]
    xl = xm[:, 3 * D_A:D_SHIFT]
    lane = lax.broadcasted_iota(jnp.int32, xl.shape, 1)
    act = jnp.where(lane < D_DECAY_LORA, jnp.tanh(xl),
                    jnp.where(lane < D_DECAY_LORA + D_AAA_LORA, xl, _sigmoid(xl)))
    act_hi, act_lo = _split2(act)
    w_hi, w_lo = wl_ref[0], wl_ref[1]
    lora = _dot_bf16(act_hi, w_hi) + _dot_bf16(act_hi, w_lo) + _dot_bf16(act_lo, w_hi)
    y = -(w0_ref[...] + lora[:, 0:D_A])
    softplus = jnp.maximum(y, 0.0) + jnp.log(1.0 + jnp.exp(-jnp.abs(y)))
    logw = -jnp.exp(-softplus - 0.5)
    a = _sigmoid(a0_ref[...] + lora[:, D_A:2 * D_A])
    g = lora[:, 2 * D_A:3 * D_A]
    kkr = k * kk_ref[...]
    kp = k * (1.0 + (a - 1.0) * ka_ref[...])
    seg = _head_sums(jnp.concatenate([kkr * kkr, r * kp * rk_ref[...]], axis=0), hs_ref[...], hb_ref[...])
    kk = kkr / jnp.maximum(jnp.sqrt(seg[0:tm]), 1e-12)
    cum = _dot_exact_lhs(lmat_ref[...], logw)
    e_out = jnp.exp(-cum)
    rt_ref[...] = r * jnp.exp(cum)
    kt_ref[...] = kp * e_out
    qt_ref[...] = kk * a * e_out
    pt_ref[...] = -kk * jnp.exp(cum - logw)
    v_ref[...] = v
    bonus_ref[...] = seg[tm:2 * tm] * v
    g_ref[...] = g
    gc_ref[...] = jnp.exp(_dot_exact_lhs(sel_ref[...], cum))


def _rwkv_prep(proj, prev, tm, chunk, lw, name):
    m = proj.shape[0]
    rows_are_time = prev is None
    n_tiles = m // tm
    t = np.arange(tm)
    lmat = ((t[:, None] // chunk == t[None, :] // chunk) & (t[None, :] <= t[:, None])).astype(np.float32)
    sel_stride = 1 if chunk == 1 else 8
    n_sel = sel_stride * (tm // chunk)
    sel = np.zeros((n_sel, tm), np.float32)
    for c in range(tm // chunk):
        sel[sel_stride * c, (c + 1) * chunk - 1] = 1.0
    if rows_are_time:
        prev_arr = proj
        prev_spec = pl.BlockSpec((8, D_SHIFT), lambda i: (jnp.maximum(i * (tm // 8) - 1, 0), 0))
    else:
        prev_arr = prev
        prev_spec = pl.BlockSpec((tm, D_SHIFT), lambda i: (i, 0))
    vec = lambda d: pl.BlockSpec((1, d), lambda i: (0, 0))
    full = lambda a: pl.BlockSpec(a.shape, lambda i: (0, 0))
    big = pl.BlockSpec((tm, D_A), lambda i: (i, 0))
    out = jax.ShapeDtypeStruct((m, D_A), F32)
    lmat_b = jnp.asarray(lmat, BF16)
    sel_b = jnp.asarray(sel, BF16)
    return pl.pallas_call(
        functools.partial(_prep_kernel, rows_are_time=rows_are_time),
        out_shape=[out] * 7 + [jax.ShapeDtypeStruct((n_tiles * n_sel, D_A), F32)],
        grid=(n_tiles,),
        in_specs=[pl.BlockSpec((tm, D_SHIFT), lambda i: (i, 0)), prev_spec, vec(D_SHIFT),
                  pl.BlockSpec(lw["w_lora"].shape, lambda i: (0, 0, 0)),
                  vec(D_A), vec(D_A), vec(D_A), vec(D_A), vec(D_A), full(lmat_b), full(sel_b), full(lw["hsum"]),
                  full(lw["hbc"])],
        out_specs=[big] * 7 + [pl.BlockSpec((n_sel, D_A), lambda i: (i, 0))],
        compiler_params=_params(("parallel",)),
        name=name,
    )(proj, prev_arr, lw["mu"], lw["w_lora"], lw["w0"], lw["a0"], lw["k_k"], lw["k_a"], lw["r_k"], lmat_b, sel_b,
      lw["hsum"], lw["hbc"])


def _bdot3(a, b, dims):
    ah, al = _split2(a)
    bh, bl = _split2(b)
    (ca,), (cb,) = dims[0]
    a_cat = jnp.concatenate([ah, ah, al, jnp.zeros_like(al)], axis=ca)
    b_cat = jnp.concatenate([bh, bl, bh, jnp.zeros_like(bl)], axis=cb)
    return lax.dot_general(a_cat, b_cat, dims, preferred_element_type=F32)


_NN = (((2,), (1,)), ((0,), (0,)))
_NT = (((2,), (2,)), ((0,), (0,)))


def _scan_kernel(pt_ref, rt_ref, qt_ref, kt_ref, v_ref, gc_ref, o_ref, sfin_ref, s_scr):
    @pl.when(pl.program_id(0) == 0)
    def _():
        s_scr[...] = jnp.zeros_like(s_scr)

    def heads(x):
        return jnp.stack([x[:, h * HEAD_A:(h + 1) * HEAD_A] for h in range(H_A)], axis=0)

    p, r, q, k, v = (heads(ref[...]) for ref in (pt_ref, rt_ref, qt_ref, kt_ref, v_ref))
    gam = heads(gc_ref[0:1, :])
    ri = lax.broadcasted_iota(jnp.int32, (1, CHUNK, CHUNK), 1)
    ci = lax.broadcasted_iota(jnp.int32, (1, CHUNK, CHUNK), 2)
    strict = ri > ci
    incl = ri >= ci
    eye = (ri == ci).astype(F32)

    gram = _bdot3(jnp.concatenate([p, r], axis=1), jnp.concatenate([q, k], axis=1), _NT)
    a_qp = jnp.where(strict, gram[:, 0:CHUNK, 0:CHUNK], 0.0)
    a_kp = jnp.where(strict, gram[:, 0:CHUNK, CHUNK:], 0.0)
    a_rq = jnp.where(incl, gram[:, CHUNK:, 0:CHUNK], 0.0)
    a_rk = jnp.where(incl, gram[:, CHUNK:, CHUNK:], 0.0)
    same = lambda log2_bs: (ri >> log2_bs) == (ci >> log2_bs)
    a_d = jnp.where(same(4), a_qp, 0.0)
    tinv = eye + a_d
    pw = _bdot3(a_d, a_d, _NN)
    for _ in range(2):
        both = _bdot3(jnp.concatenate([pw, tinv], axis=1), pw, _NN)
        pw = both[:, 0:CHUNK]
        tinv = tinv + both[:, CHUNK:]
    tinv = tinv + _bdot3(tinv, pw, _NN)
    for log2_bs in (4, 5):
        off = jnp.where(jnp.logical_and(same(log2_bs + 1), jnp.logical_not(same(log2_bs))), a_qp, 0.0)
        tinv = tinv + _bdot3(tinv, _bdot3(off, tinv, _NN), _NN)
    av = _bdot3(jnp.concatenate([a_kp, a_rk], axis=1), v, _NN)
    pw_hat = _bdot3(tinv, jnp.concatenate([p, av[:, 0:CHUNK]], axis=2), _NN)
    x = _bdot3(a_rq, pw_hat, _NN)
    r_hat = r + x[:, :, 0:HEAD_A]
    o_loc = x[:, :, HEAD_A:] + av[:, CHUNK:]
    y = _bdot3(jnp.swapaxes(pw_hat, 1, 2), q, _NN)
    m_mat = (eye + y[:, 0:HEAD_A]) * gam
    n_mat = (y[:, HEAD_A:] + _bdot3(jnp.swapaxes(v, 1, 2), k, _NN)) * gam

    s0 = s_scr[...]
    o = _bdot3(r_hat, s0, _NT) + o_loc
    s_scr[...] = _bdot3(s0, m_mat, _NN) + n_mat
    o_ref[...] = jnp.concatenate([o[h] for h in range(H_A)], axis=1)

    @pl.when(pl.program_id(0) == pl.num_programs(0) - 1)
    def _():
        sfin_ref[...] = s_scr[...]


def _wkv_scan(pt, rt, qt, kt, v, gc):
    t = pt.shape[0]
    blk = pl.BlockSpec((CHUNK, D_A), lambda c: (c, 0))
    return pl.pallas_call(
        _scan_kernel,
        out_shape=[jax.ShapeDtypeStruct((t, D_A), F32), jax.ShapeDtypeStruct((H_A, HEAD_A, HEAD_A), F32)],
        grid=(t // CHUNK,),
        in_specs=[blk] * 5 + [pl.BlockSpec((8, D_A), lambda c: (c, 0))],
        out_specs=[blk, pl.BlockSpec((H_A, HEAD_A, HEAD_A), lambda c: (0, 0, 0))],
        scratch_shapes=[pltpu.VMEM((H_A, HEAD_A, HEAD_A), F32)],
        compiler_params=_params(("arbitrary",)),
        name="wkv_scan_prompt",
    )(pt, rt, qt, kt, v, gc)


def _wkv_step_kernel(s_ref, rows_ref, sn_ref, o_ref):
    ri = lax.broadcasted_iota(jnp.int32, (1, 1, HEAD_A, HEAD_A), 2)
    ci = lax.broadcasted_iota(jnp.int32, (1, 1, HEAD_A, HEAD_A), 3)
    eye = ri == ci
    pt, rt, qt, kt, v, gc = (rows_ref[:, n] for n in range(6))
    s = s_ref[0]
    u = jnp.sum(s * pt, axis=3, keepdims=True)
    v_col = jnp.sum(jnp.where(eye, v, 0.0), axis=3, keepdims=True)
    m = s + u * qt + v_col * kt
    sn_ref[0] = m * gc
    o_col = jnp.sum(m * rt, axis=3, keepdims=True)
    o_ref[...] = jnp.sum(jnp.where(eye, o_col, 0.0), axis=2, keepdims=True)


def _wkv_step(state, pt, rt, qt, kt, v, gc, bt):
    b = pt.shape[0]
    rows = jnp.stack([pt, rt, qt, kt, v, gc], axis=1).reshape(b, 6, H_A, 1, HEAD_A)
    st_spec = pl.BlockSpec((1, bt, H_A, HEAD_A, HEAD_A), lambda i: (0, i, 0, 0, 0))
    sn, o = pl.pallas_call(
        _wkv_step_kernel,
        out_shape=[jax.ShapeDtypeStruct(state.shape, F32), jax.ShapeDtypeStruct((b, H_A, 1, HEAD_A), F32)],
        grid=(b // bt,),
        in_specs=[st_spec, pl.BlockSpec((bt, 6, H_A, 1, HEAD_A), lambda i: (i, 0, 0, 0, 0))],
        out_specs=[st_spec, pl.BlockSpec((bt, H_A, 1, HEAD_A), lambda i: (i, 0, 0, 0))],
        compiler_params=_params(("parallel",)),
        name="wkv_step_sample",
    )(state, rows)
    return sn, o.reshape(b, D_A)


def _post_kernel(o_ref, bonus_ref, g_ref, gng_ref, gnb_ref, hs_ref, hb_ref, y_ref):
    o = o_ref[...]
    hsum, hbc = hs_ref[...], hb_ref[...]
    mu = _head_sums(o, hsum, hbc) * (1.0 / HEAD_A)
    d = o - mu
    var = _head_sums(d * d, hsum, hbc) * (1.0 / HEAD_A)
    o_n = d * lax.rsqrt(var + EPS_GN) * gng_ref[...] + gnb_ref[...]
    y_ref[...] = ((o_n + bonus_ref[...]) * g_ref[...]).astype(y_ref.dtype)


def _rwkv_post(o, bonus, g, lw, tm, name):
    m = o.shape[0]
    big = pl.BlockSpec((tm, D_A), lambda i: (i, 0))
    vec = pl.BlockSpec((1, D_A), lambda i: (0, 0))
    return pl.pallas_call(
        _post_kernel,
        out_shape=jax.ShapeDtypeStruct((m, D_A), BF16),
        grid=(m // tm,),
        in_specs=[big, big, big, vec, vec, pl.BlockSpec((D_A, LANES), lambda i: (0, 0)),
                  pl.BlockSpec((LANES, D_A), lambda i: (0, 0))],
        out_specs=big,
        compiler_params=_params(("parallel",)),
        name=name,
    )(o, bonus, g, lw["gn_g"], lw["gn_b"], lw["hsum"], lw["hbc"])


def _attn_kernel(q_ref, kc_ref, vc_ref, kh_ref, vh_ref, bias_ref, o_ref, l_ref, *, dil, m_blocks, tiled):
    scale = HEAD_B ** -0.5
    n_streams = SUBLANES if tiled else dil
    n_units = n_streams * m_blocks

    stride = SUBLANES if tiled else dil
    if tiled:
        flat = lambda ref: ref.reshape(ref.shape[0] * SUBLANES, LANES)
        q_ref, kc_ref, vc_ref, kh_ref, vh_ref, o_ref, l_ref = map(flat, (q_ref, kc_ref, vc_ref, kh_ref, vh_ref,
                                                                          o_ref, l_ref))

    def rows(ref, r, start, size):
        if stride == 1:
            return ref[pl.ds(start, size), :]
        return ref[pl.ds(r + stride * start, size, stride=stride), :]

    def put(ref, r, start, val):
        if stride == 1:
            ref[pl.ds(start, BLK), :] = val
        else:
            ref[pl.ds(r + stride * start, BLK, stride=stride), :] = val

    qs, ks, vs, units = [], [], [], []
    for r in range(n_streams):
        for mb in range(m_blocks):
            units.append((r, mb))
            qs.append(rows(q_ref, r, BLK * mb, BLK))
            if mb == 0:
                ks.append(jnp.concatenate([rows(kh_ref, r, 0, BLK), rows(kc_ref, r, 0, BLK)], axis=0))
                vs.append(jnp.concatenate([rows(vh_ref, r, 0, BLK), rows(vc_ref, r, 0, BLK)], axis=0))
            else:
                ks.append(rows(kc_ref, r, BLK * (mb - 1), 2 * BLK))
                vs.append(rows(vc_ref, r, BLK * (mb - 1), 2 * BLK))

    def batch(xs, mul=None):
        pre = (lambda x: x) if mul is None else (lambda x: x * mul)
        return jnp.stack([pre(x[:, sub * HEAD_B:(sub + 1) * HEAD_B]).astype(BF16) for sub in range(2) for x in xs],
                         axis=0)

    assert math.frexp(scale)[0] == 0.5
    qb, kb, vb = batch(qs, scale), batch(ks), batch(vs)
    s = lax.dot_general(qb, kb, (((2,), (2,)), ((0,), (0,))), preferred_element_type=F32)
    s = s.reshape(2, n_units, BLK, 2 * BLK) + bias_ref[...][:, None]
    unit = lax.broadcasted_iota(jnp.int32, (1, n_units, 1, 2 * BLK), 1)
    col = lax.broadcasted_iota(jnp.int32, (1, n_units, 1, 2 * BLK), 3)
    no_prev = jnp.logical_and(jnp.logical_and(pl.program_id(0) == 0, (unit & (m_blocks - 1)) == 0), col < BLK)
    s = jnp.where(no_prev, -jnp.inf, s)
    m = jnp.max(s, axis=-1, keepdims=True)
    p = jnp.exp(s - m)
    l = jnp.sum(p, axis=-1, keepdims=True)
    pv = lax.dot_general(p.astype(BF16).reshape(2 * n_units, BLK, 2 * BLK), vb, (((2,), (1,)), ((0,), (0,))),
                         preferred_element_type=F32)
    o = pv.reshape(2, n_units, BLK, HEAD_B) / l
    lse = jnp.broadcast_to(m + jnp.log(l), (2, n_units, BLK, HEAD_B))
    for u, (r, mb) in enumerate(units):
        put(o_ref, r, BLK * mb, jnp.concatenate([o[0, u], o[1, u]], axis=1))
        put(l_ref, r, BLK * mb, jnp.concatenate([lse[0, u], lse[1, u]], axis=1))


def _attn_prompt(proj, bias, gi, rows_per_step):
    t, n_cols = proj.shape
    dil = DILATIONS[gi]
    span = BLK * dil
    tiled = dil % SUBLANES == 0
    if tiled:
        rows_per_step *= dil // SUBLANES
    m_blocks = rows_per_step // span
    cq = (COL_QKV + gi * D_G) // LANES
    ck = (COL_QKV + D_B + gi * D_G) // LANES
    cv = (COL_QKV + 2 * D_B + gi * D_G) // LANES
    bias_block = (2, BLK, 2 * BLK)
    if tiled:
        n_hi = dil // SUBLANES
        src = proj.reshape(t // dil, n_hi, SUBLANES, n_cols)
        grid = (t // rows_per_step, n_hi, H_G // 2)
        l_rows = rows_per_step // dil
        cur = lambda c0: pl.BlockSpec((l_rows, None, SUBLANES, LANES), lambda i, rh, hp: (i, rh, 0, c0 + hp))
        halo = lambda c0: pl.BlockSpec((BLK, None, SUBLANES, LANES),
                                       lambda i, rh, hp: (jnp.maximum(i * m_blocks - 1, 0), rh, 0, c0 + hp))
        out_spec = pl.BlockSpec((l_rows, None, SUBLANES, LANES), lambda i, rh, hp: (i, rh, 0, hp))
        out = jax.ShapeDtypeStruct((t // dil, n_hi, SUBLANES, D_G), F32)
        bias_spec = pl.BlockSpec(bias_block, lambda i, rh, hp: (hp, 0, 0))
    else:
        src = proj
        grid = (t // rows_per_step, H_G // 2)
        cur = lambda c0: pl.BlockSpec((rows_per_step, LANES), lambda i, hp: (i, c0 + hp))
        halo = lambda c0: pl.BlockSpec((span, LANES), lambda i, hp: (jnp.maximum(i * m_blocks - 1, 0), c0 + hp))
        out_spec = pl.BlockSpec((rows_per_step, LANES), lambda i, hp: (i, hp))
        out = jax.ShapeDtypeStruct((t, D_G), F32)
        bias_spec = pl.BlockSpec(bias_block, lambda i, hp: (hp, 0, 0))
    o, l = pl.pallas_call(
        functools.partial(_attn_kernel, dil=dil, m_blocks=m_blocks, tiled=tiled),
        out_shape=[out, out],
        grid=grid,
        in_specs=[cur(cq), cur(ck), cur(cv), halo(ck), halo(cv), bias_spec],
        out_specs=[out_spec, out_spec],
        compiler_params=_params(("parallel",) * len(grid)),
        name=f"attn_prompt_g{gi}",
    )(src, src, src, src, src, bias)
    return o.reshape(t, D_G), l.reshape(t, D_G)


def _attn_step_kernel(q_ref, kn_ref, vn_ref, c_ref, bias_ref, bias0_ref, o_ref, l_ref):
    scale = HEAD_B ** -0.5
    ri = lax.broadcasted_iota(jnp.int32, (1, 1, HEAD_B, HEAD_B), 2)
    ci = lax.broadcasted_iota(jnp.int32, (1, 1, HEAD_B, HEAD_B), 3)
    eye = ri == ci
    q = q_ref[...]
    q_col = jnp.sum(jnp.where(eye, q, 0.0), axis=3, keepdims=True)
    s = jnp.sum(c_ref[:, 0] * q_col, axis=2, keepdims=True) * scale + bias_ref[...]
    s_new = jnp.sum(q * kn_ref[...], axis=3, keepdims=True) * scale + bias0_ref[...]
    m = jnp.maximum(jnp.max(s, axis=3, keepdims=True), s_new)
    p = jnp.exp(s - m)
    p_new = jnp.exp(s_new - m)
    l = jnp.sum(p, axis=3, keepdims=True) + p_new
    o_col = jnp.sum(c_ref[:, 1] * p, axis=3, keepdims=True)
    o_row = jnp.sum(jnp.where(eye, o_col, 0.0), axis=2, keepdims=True)
    o_ref[...] = (o_row + p_new * vn_ref[...]) / l
    l_ref[...] = jnp.broadcast_to(m + jnp.log(l), o_ref.shape)


def _attn_step(q, k_new, v_new, cache, bias, bias0, gi):
    b = q.shape[0]
    w = cache.shape[2]
    bt = max(1, min(b, STEP_CACHE_WORDS // (2 * D_G * w)))
    cache_t = jnp.transpose(cache, (0, 1, 3, 4, 5, 2))
    vec = pl.BlockSpec((bt, H_G, 1, HEAD_B), lambda i: (i, 0, 0, 0))
    out = jax.ShapeDtypeStruct((b, H_G, 1, HEAD_B), F32)
    o, l = pl.pallas_call(
        _attn_step_kernel,
        out_shape=[out, out],
        grid=(b // bt,),
        in_specs=[vec, vec, vec,
                  pl.BlockSpec((None, bt, 2, H_G, HEAD_B, w), lambda i: (0, i, 0, 0, 0, 0)),
                  pl.BlockSpec((H_G, 1, w), lambda i: (0, 0, 0)),
                  pl.BlockSpec((H_G, 1, 1), lambda i: (0, 0, 0))],
        out_specs=[vec, vec],
        compiler_params=_params(("parallel",)),
        name=f"attn_step_g{gi}",
    )(q, k_new, v_new, cache_t, bias, bias0)
    return o.reshape(b, D_G), l.reshape(b, D_G)


def _mix_kernel(ya_ref, o1_ref, o2_ref, o3_ref, l1_ref, l2_ref, l3_ref, wa_ref, wb_ref, ga_ref, gb_ref, o_ref,
                yb_scr):
    @pl.when(pl.program_id(1) == 0)
    def _():
        l1, l2, l3 = l1_ref[...], l2_ref[...], l3_ref[...]
        m = jnp.maximum(jnp.maximum(l1, l2), l3)
        e1, e2, e3 = jnp.exp(l1 - m), jnp.exp(l2 - m), jnp.exp(l3 - m)
        den = e1 + e2 + e3
        yb = (e1 / den) * o1_ref[...] + (e2 / den) * o2_ref[...] + (e3 / den) * o3_ref[...]
        yb_scr[...] = yb.astype(BF16)

    mixed = ga_ref[...] * _dot_bf16(ya_ref[...], wa_ref[...]) + gb_ref[...] * _dot_bf16(yb_scr[...], wb_ref[...])
    o_ref[...] = mixed.astype(o_ref.dtype)


def _mix(ya, outs, lses, wa, wb, proj, tm, tn, name):
    m = ya.shape[0]
    ga0 = COL_GATE // tn
    gb0 = (COL_GATE + D_MODEL) // tn
    grp = pl.BlockSpec((tm, D_G), lambda i, j: (i, 0))
    return pl.pallas_call(
        _mix_kernel,
        out_shape=jax.ShapeDtypeStruct((m, D_MODEL), BF16),
        grid=(m // tm, D_MODEL // tn),
        in_specs=[pl.BlockSpec((tm, D_A), lambda i, j: (i, 0))] + [grp] * 6 + [
            pl.BlockSpec((D_A, tn), lambda i, j: (0, j)),
            pl.BlockSpec((D_G, tn), lambda i, j: (0, j)),
            pl.BlockSpec((tm, tn), lambda i, j: (i, ga0 + j)),
            pl.BlockSpec((tm, tn), lambda i, j: (i, gb0 + j)),
        ],
        out_specs=pl.BlockSpec((tm, tn), lambda i, j: (i, j)),
        scratch_shapes=[pltpu.VMEM((tm, D_G), BF16)],
        compiler_params=_params(("parallel", "arbitrary")),
        name=name,
    )(ya, *outs, *lses, wa, wb, proj, proj)


def _wo_kernel(a_ref, w_ref, x_ref, g_ref, h_ref, hn_ref):
    h = x_ref[...] + _dot_bf16(a_ref[...], w_ref[...])
    h_ref[...] = h
    hn_ref[...] = _rms(h, g_ref[...]).astype(BF16)


def _wo(a, w, x, g, tm, name):
    m, k = a.shape
    n = w.shape[1]
    row = pl.BlockSpec((tm, n), lambda i: (i, 0))
    return pl.pallas_call(
        _wo_kernel,
        out_shape=[jax.ShapeDtypeStruct((m, n), F32), jax.ShapeDtypeStruct((m, n), BF16)],
        grid=(m // tm,),
        in_specs=[pl.BlockSpec((tm, k), lambda i: (i, 0)), pl.BlockSpec((k, n), lambda i: (0, 0)), row,
                  pl.BlockSpec((1, n), lambda i: (0, 0))],
        out_specs=[row, row],
        compiler_params=_params(("parallel",)),
        name=name,
    )(a, w, x, g.reshape(1, n))


def _gelu(x):
    return 0.5 * x * (1.0 + lax.erf(x * (1.0 / math.sqrt(2.0))))


def _ffn_up_kernel(h_ref, w1_ref, w2_ref, cw1_ref, cw2_ref, cb1_ref, cb2_ref, act_ref, t1_ref, t2_ref,
                   c1_scr, c2_scr, w1_scr, w2_scr):
    tm = h_ref.shape[0]

    @pl.when(pl.program_id(1) == 0)
    def _():
        c1_scr[...] = jnp.zeros_like(c1_scr)
        c2_scr[...] = jnp.zeros_like(c2_scr)
        w1_scr[...] = w1_ref[...].astype(BF16)
        w2_scr[...] = w2_ref[...].astype(BF16)

    a = h_ref[...]
    row = lax.broadcasted_iota(jnp.int32, (tm, 1), 0)

    def conv(u, carry_ref, cw_ref, cb_ref):
        m1 = jnp.where(row == 0, carry_ref[7:8, :], pltpu.roll(u, 1, axis=0))
        m2 = jnp.where(row == 0, carry_ref[6:7, :], jnp.where(row == 1, carry_ref[7:8, :], pltpu.roll(u, 2, axis=0)))
        return cb_ref[...] + cw_ref[0:1, :] * m2 + cw_ref[1:2, :] * m1 + cw_ref[2:3, :] * u

    u1 = _dot_bf16(a, w1_scr[...])
    u2 = _dot_bf16(a, w2_scr[...])
    c1 = conv(u1, c1_scr, cw1_ref, cb1_ref)
    c2 = conv(u2, c2_scr, cw2_ref, cb2_ref)
    act_ref[...] = (_gelu(c1) * c2).astype(act_ref.dtype)
    c1_scr[...] = u1[tm - 8:tm]
    c2_scr[...] = u2[tm - 8:tm]
    t1_ref[...] = u1[tm - 8:tm]
    t2_ref[...] = u2[tm - 8:tm]


def _ffn_up_prompt(hn, w_up, conv_w, conv_b, tm, tn):
    m = hn.shape[0]
    nj = D_FF // tn
    tail = jax.ShapeDtypeStruct((8, D_FF), F32)
    return pl.pallas_call(
        _ffn_up_kernel,
        out_shape=[jax.ShapeDtypeStruct((m, D_FF), BF16), tail, tail],
        grid=(nj, m // tm),
        in_specs=[
            pl.BlockSpec((tm, D_MODEL), lambda j, i: (i, 0)),
            pl.BlockSpec((D_MODEL, tn), lambda j, i: (0, j)),
            pl.BlockSpec((D_MODEL, tn), lambda j, i: (0, nj + j)),
            pl.BlockSpec((CONV_W, tn), lambda j, i: (0, j)),
            pl.BlockSpec((CONV_W, tn), lambda j, i: (0, nj + j)),
            pl.BlockSpec((1, tn), lambda j, i: (0, j)),
            pl.BlockSpec((1, tn), lambda j, i: (0, nj + j)),
        ],
        out_specs=[pl.BlockSpec((tm, tn), lambda j, i: (i, j)),
                   pl.BlockSpec((8, tn), lambda j, i: (0, j)),
                   pl.BlockSpec((8, tn), lambda j, i: (0, j))],
        scratch_shapes=[pltpu.VMEM((8, tn), F32), pltpu.VMEM((8, tn), F32),
                        pltpu.VMEM((D_MODEL, tn), BF16), pltpu.VMEM((D_MODEL, tn), BF16)],
        compiler_params=_params(("arbitrary", "arbitrary")),
        name="ffn_up_prompt",
    )(hn, w_up, w_up, conv_w, conv_w, conv_b, conv_b)


def _ffn_act_step_kernel(up_ref, prev_ref, cw_ref, cb_ref, act_ref):
    up = up_ref[...]
    w = 2 * D_FF
    c = cb_ref[...] + cw_ref[0:1, :] * prev_ref[:, 0:w] + cw_ref[1:2, :] * prev_ref[:, w:2 * w] + cw_ref[2:3, :] * up
    act_ref[...] = (_gelu(c[:, 0:D_FF]) * c[:, D_FF:w]).astype(act_ref.dtype)


def _ffn_act_step(up, conv_prev, conv_w, conv_b):
    b = up.shape[0]
    return pl.pallas_call(
        _ffn_act_step_kernel,
        out_shape=jax.ShapeDtypeStruct((b, D_FF), BF16),
        compiler_params=pltpu.CompilerParams(vmem_limit_bytes=VMEM_LIMIT),
        name="ffn_act_sample",
    )(up, conv_prev.reshape(b, (CONV_W - 1) * 2 * D_FF), conv_w, conv_b)


def _down_kernel(a_ref, w_ref, h_ref, g_ref, o_ref, acc_ref):
    @pl.when(pl.program_id(1) == 0)
    def _():
        acc_ref[...] = h_ref[...]

    out = acc_ref[...] + _dot_bf16(a_ref[...], w_ref[...])
    acc_ref[...] = out
    inv = lax.rsqrt(jnp.mean(out * out, axis=-1, keepdims=True) + EPS_RMS)
    o_ref[...] = out * inv * g_ref[...]


def _ffn_down(act, w_down, h, g, tm, tk, name):
    m = act.shape[0]
    nk = D_FF // tk
    return pl.pallas_call(
        _down_kernel,
        out_shape=jax.ShapeDtypeStruct((m, D_MODEL), F32),
        grid=(m // tm, nk),
        in_specs=[
            pl.BlockSpec((tm, tk), lambda i, k: (i, k)),
            pl.BlockSpec((tk, D_MODEL), lambda i, k: (k, 0)),
            pl.BlockSpec((tm, D_MODEL), lambda i, k: (i, 0)),
            pl.BlockSpec((1, D_MODEL), lambda i, k: (0, 0)),
        ],
        out_specs=pl.BlockSpec((tm, D_MODEL), lambda i, k: (i, 0)),
        scratch_shapes=[pltpu.VMEM((tm, D_MODEL), F32)],
        compiler_params=_params(("parallel", "arbitrary")),
        name=name,
    )(act, w_down, h, g.reshape(1, D_MODEL))


def _rel_bucket(dist):
    max_exact = N_BUCKETS // 2
    d_f = jnp.maximum(dist, 1).astype(F32)
    large = max_exact + (jnp.log(d_f / max_exact) / math.log(MAX_DISTANCE / max_exact)
                         * (N_BUCKETS - max_exact)).astype(jnp.int32)
    large = jnp.minimum(large, N_BUCKETS - 1)
    return jnp.where(dist < max_exact, dist, large)


def _bias_rows(tab, dist):
    onehot = (_rel_bucket(dist)[None, :] == jnp.arange(N_BUCKETS)[:, None]).astype(F32)
    return jnp.dot(tab.T, onehot, precision=HIGHEST)


def _bias_tables(rel_bias, gi):
    dil = DILATIONS[gi]
    win = WINDOWS[gi]
    reach = win // dil
    tab = rel_bias[:, gi * H_G:(gi + 1) * H_G]
    qi = jnp.arange(BLK)[:, None]
    ki = jnp.arange(2 * BLK)[None, :]
    rel = qi + BLK - ki
    blk = _bias_rows(tab, (dil * jnp.maximum(rel, 0)).reshape(-1)).reshape(H_G, BLK, 2 * BLK)
    blk = jnp.where(((rel >= 0) & (rel <= reach))[None], blk, -jnp.inf)
    back = win - jnp.arange(win)
    step = jnp.where((back % dil == 0)[None, :], _bias_rows(tab, back), -jnp.inf)[:, None, :]
    step0 = _bias_rows(tab, jnp.zeros((1,), jnp.int32))[:, None, :]
    return blk, step, step0


def _tiles(m, prompt):
    if prompt:
        return dict(tag="prompt", proj_m=1024, prep_m=2 * CHUNK, post_m=256, attn_rows=2048, mix_m=512, mix_n=1024,
                    wo_m=512, up_m=1024, up_n=512, down_m=512, down_k=D_FF // 2)
    return dict(tag="sample", proj_m=m, prep_m=m, step_b=4, post_m=m, mix_m=m, mix_n=1024, wo_m=m, up_n=512, down_m=m,
                down_k=D_FF // 2)


def _layer(x, lw, bias_blk, prompt, state=None, proj=None):
    m = x.shape[0]
    tiles = _tiles(m, prompt)
    if proj is None:
        proj = _proj(x, lw["norm1_g"], lw["w_in"], lw["in_bias"], tiles["proj_m"], "proj_" + tiles["tag"])

    if prompt:
        rt, kt, qt, pt, v, bonus, g, gc = _rwkv_prep(proj, None, tiles["prep_m"], CHUNK, lw, "rwkv_prep_prompt")
        o, wkv_new = _wkv_scan(pt, rt, qt, kt, v, gc)
        wkv_new = wkv_new[None, None]
    else:
        rt, kt, qt, pt, v, bonus, g, gc = _rwkv_prep(proj, state["shift"], tiles["prep_m"], 1, lw, "rwkv_prep_sample")
        wkv_new, o = _wkv_step(state["wkv"], pt, rt, qt, kt, v, gc, tiles["step_b"])
    ya = _rwkv_post(o, bonus, g, lw, tiles["post_m"], "rwkv_post_" + tiles["tag"])

    outs, lses = [], []
    for gi in range(N_GROUPS):
        if prompt:
            o_g, l_g = _attn_prompt(proj, bias_blk[gi][0], gi, tiles["attn_rows"])
        else:
            sl = lambda c0: proj[:, c0 + gi * D_G:c0 + (gi + 1) * D_G].reshape(m, H_G, 1, HEAD_B)
            o_g, l_g = _attn_step(sl(COL_QKV), sl(COL_QKV + D_B), sl(COL_QKV + 2 * D_B), state["win"][gi],
                                  bias_blk[gi][1], bias_blk[gi][2], gi)
        outs.append(o_g)
        lses.append(l_g)
    mixed = _mix(ya, outs, lses, lw["w_out_a"], lw["w_out_b"], proj, tiles["mix_m"], tiles["mix_n"],
                 "mix_" + tiles["tag"])
    h, hn = _wo(mixed, lw["w_o"], x, lw["norm2_g"], tiles["wo_m"], "wo_" + tiles["tag"])

    if prompt:
        act, t1, t2 = _ffn_up_prompt(hn, lw["w_up"], lw["conv_w"], lw["conv_b"], tiles["up_m"], tiles["up_n"])
        conv_new = jnp.concatenate([t1[6:8], t2[6:8]], axis=1)[None, None]
    else:
        up = _matmul(hn, lw["w_up"], m, tiles["up_n"], "up_sample")
        act = _ffn_act_step(up, state["conv"], lw["conv_w"], lw["conv_b"])
        conv_new = jnp.concatenate([state["conv"][:, 1:], up[:, None, :]], axis=1)[None]
    y = _ffn_down(act, lw["w_down"], h, lw["normf_g"], tiles["down_m"], tiles["down_k"], "down_" + tiles["tag"])
    return y, proj, wkv_new, conv_new


def kernel(x_prompt, x_sample, state_wkv, state_shift, state_ffn_conv, cache_win1, cache_win2, cache_win3, rel_bias,
           norm1_g, w_in, gate_b, mu_shift, w0, w_up_decay, a0, w_up_aaa, w_up_gate, k_k, k_a, r_k, gn_g, gn_b,
           w_out_a, w_out_b, w_o, norm2_g, w_up, conv_w, conv_b, w_down, normf_g):
    row = lambda a: a.reshape(1, -1)
    w_lora = jnp.zeros((D_LORA, 3 * D_A), F32)
    w_lora = w_lora.at[0:D_DECAY_LORA, 0:D_A].set(w_up_decay[0])
    w_lora = w_lora.at[D_DECAY_LORA:D_DECAY_LORA + D_AAA_LORA, D_A:2 * D_A].set(w_up_aaa[0])
    w_lora = w_lora.at[D_DECAY_LORA + D_AAA_LORA:, 2 * D_A:].set(w_up_gate[0])
    head = np.arange(D_A) // HEAD_A
    lw = dict(
        norm1_g=norm1_g[0], norm2_g=norm2_g[0], normf_g=normf_g,
        in_bias=jnp.concatenate([jnp.zeros((1, COL_GATE), F32), row(gate_b[0])], axis=1),
        mu=row(mu_shift[0]), w_lora=jnp.stack(_split2(w_lora)), w0=row(w0[0]), a0=row(a0[0]), k_k=row(k_k[0]), k_a=row(k_a[0]),
        r_k=row(r_k[0]), gn_g=row(gn_g[0]), gn_b=row(gn_b[0]),
        hsum=jnp.asarray(head[:, None] == np.arange(LANES)[None, :], BF16),
        hbc=jnp.asarray(np.arange(LANES)[:, None] == head[None, :], BF16),
        w_out_a=w_out_a[0].astype(BF16), w_out_b=w_out_b[0].astype(BF16), w_o=w_o[0].astype(BF16),
        w_up=w_up[0], conv_w=conv_w[0], conv_b=row(conv_b[0]), w_down=w_down[0].astype(BF16),
    )
    bias = [_bias_tables(rel_bias, gi) for gi in range(N_GROUPS)]

    proj_s, lw["w_in"] = _proj_cast(x_sample[:, 0], norm1_g[0], w_in[0], lw["in_bias"])
    y_p, proj_p, wkv_p, conv_p = _layer(x_prompt[0], lw, bias, prompt=True)

    b = DEC_BATCH
    state = dict(wkv=state_wkv, shift=state_shift.reshape(b, D_SHIFT), conv=state_ffn_conv[0],
                 win=(cache_win1, cache_win2, cache_win3))
    y_s, proj_s, wkv_s, conv_s = _layer(x_sample[:, 0], lw, bias, prompt=False, state=state, proj=proj_s)

    def kv_rows(proj, lo, gi):
        k = proj[lo:, COL_QKV + D_B + gi * D_G:COL_QKV + D_B + (gi + 1) * D_G]
        v = proj[lo:, COL_QKV + 2 * D_B + gi * D_G:COL_QKV + 2 * D_B + (gi + 1) * D_G]
        n = k.shape[0]
        return jnp.stack([k.reshape(n, H_G, HEAD_B), v.reshape(n, H_G, HEAD_B)], axis=1)

    win_p = [kv_rows(proj_p, SEQ - min(WINDOWS[gi], SEQ), gi)[None, None] for gi in range(N_GROUPS)]
    win_s = [kv_rows(proj_s, 0, gi)[None, :, None] for gi in range(N_GROUPS)]
    return (y_p[None], y_s[:, None],
            wkv_p, wkv_s,
            proj_p[SEQ - 1:, 0:D_SHIFT][None, None], proj_s[:, 0:D_SHIFT][None, :, None],
            conv_p, conv_s,
            win_p[0], win_s[0], win_p[1], win_s[1], win_p[2], win_s[2])
```

```python
import functools
import math

import numpy as np
import jax
import jax.numpy as jnp
from jax import lax
from jax.experimental import pallas as pl
from jax.experimental.pallas import tpu as pltpu

F32 = jnp.float32
BF16 = jnp.bfloat16
HIGHEST = lax.Precision.HIGHEST

D_MODEL = 2048
SEQ = 8192
DEC_BATCH = 32
HEAD_A = 64
H_A = 16
D_A = H_A * HEAD_A
D_DECAY_LORA = 96
D_AAA_LORA = 96
D_GATE_LORA = 64
D_LORA = D_DECAY_LORA + D_AAA_LORA + D_GATE_LORA
D_SHIFT = 3 * D_A + D_LORA
EPS_GN = 64e-5
HEAD_B = 64
H_G = 8
D_G = H_G * HEAD_B
WINDOWS = (128, 512, 2048)
DILATIONS = (1, 4, 16)
N_GROUPS = 3
D_B = N_GROUPS * D_G
BLK = 128
N_BUCKETS = 32
MAX_DISTANCE = 2048
COL_QKV = D_SHIFT
COL_GATE_SRC = D_SHIFT + 3 * D_B
D_IN = COL_GATE_SRC + 2 * D_MODEL
PROJ_TN = 1024
COL_GATE = -(-COL_GATE_SRC // PROJ_TN) * PROJ_TN
D_IN_PAD = COL_GATE + 2 * D_MODEL
D_FF = 5632
CONV_W = 3
EPS_RMS = 1e-6
CHUNK = 64
LANES = 128
SUBLANES = 8
VMEM_LIMIT = 56 * 1024 * 1024
STEP_CACHE_WORDS = 2 * 1024 * 1024


def _params(sem, vmem=VMEM_LIMIT):
    return pltpu.CompilerParams(dimension_semantics=sem, vmem_limit_bytes=vmem)


def _sigmoid(x):
    return 1.0 / (1.0 + jnp.exp(-x))


def _dot_bf16(a, b):
    return jnp.dot(a, b, preferred_element_type=F32)


def _split2(x):
    hi = x.astype(BF16)
    return hi, (x - hi.astype(F32)).astype(BF16)


def _split3(x):
    hi = x.astype(BF16)
    r1 = x - hi.astype(F32)
    mid = r1.astype(BF16)
    lo = (r1 - mid.astype(F32)).astype(BF16)
    return hi, mid, lo


def _dot_exact_rhs(x, m):
    hi, mid, lo = _split3(x)
    return _dot_bf16(hi, m) + _dot_bf16(mid, m) + _dot_bf16(lo, m)


def _head_sums(x, hsum, hbc):
    return _dot_exact_rhs(_dot_exact_rhs(x, hsum), hbc)


def _dot_exact_lhs(m, x):
    hi, mid, lo = _split3(x)
    return _dot_bf16(m, hi) + _dot_bf16(m, mid) + _dot_bf16(m, lo)


def _rms(x, g):
    return x * lax.rsqrt(jnp.mean(x * x, axis=-1, keepdims=True) + EPS_RMS) * g


def _proj_cast_kernel(x_ref, g_ref, w_ref, b_ref, o_ref, wb_ref, xn_scr, *, gap_block):
    j = pl.program_id(0)

    @pl.when(j == 0)
    def _():
        xn_scr[...] = _rms(x_ref[...], g_ref[...]).astype(BF16)

    wb = jnp.where(j != gap_block, w_ref[...], 0.0).astype(BF16)
    wb_ref[...] = wb
    acc = _dot_bf16(xn_scr[...], wb)
    tn = o_ref.shape[1]
    col = lax.broadcasted_iota(jnp.int32, (1, tn), 1) + j * tn
    o_ref[...] = jnp.where(col >= COL_GATE, _sigmoid(acc + b_ref[...]), acc)


def _proj_cast(x, g, w, bias):
    m, k = x.shape
    tn = COL_GATE - COL_GATE_SRC
    assert tn % LANES == 0 and COL_GATE_SRC % tn == 0
    gap_block = COL_GATE_SRC // tn
    return pl.pallas_call(
        functools.partial(_proj_cast_kernel, gap_block=gap_block),
        out_shape=[jax.ShapeDtypeStruct((m, D_IN_PAD), F32), jax.ShapeDtypeStruct((k, D_IN_PAD), BF16)],
        grid=(D_IN_PAD // tn,),
        in_specs=[
            pl.BlockSpec((m, k), lambda j: (0, 0)),
            pl.BlockSpec((1, k), lambda j: (0, 0)),
            pl.BlockSpec((k, tn), lambda j: (0, jnp.where(j < gap_block, j, jnp.maximum(j - 1, 0)))),
            pl.BlockSpec((1, tn), lambda j: (0, j)),
        ],
        out_specs=[pl.BlockSpec((m, tn), lambda j: (0, j)), pl.BlockSpec((k, tn), lambda j: (0, j))],
        scratch_shapes=[pltpu.VMEM((m, k), BF16)],
        compiler_params=_params(("arbitrary",)),
        name="proj_sample_cast_w_in",
    )(x, g.reshape(1, k), w, bias)


def _proj_kernel(x_ref, g_ref, w_ref, b_ref, o_ref, xn_scr):
    @pl.when(pl.program_id(1) == 0)
    def _():
        xn_scr[...] = _rms(x_ref[...], g_ref[...]).astype(BF16)

    acc = _dot_bf16(xn_scr[...], w_ref[...])
    tn = o_ref.shape[1]
    col = lax.broadcasted_iota(jnp.int32, (1, tn), 1) + pl.program_id(1) * tn
    o_ref[...] = jnp.where(col >= COL_GATE, _sigmoid(acc + b_ref[...]), acc)


def _proj(x, g, w, bias, tm, name):
    m, k = x.shape
    n = w.shape[1]
    return pl.pallas_call(
        _proj_kernel,
        out_shape=jax.ShapeDtypeStruct((m, n), F32),
        grid=(m // tm, n // PROJ_TN),
        in_specs=[
            pl.BlockSpec((tm, k), lambda i, j: (i, 0)),
            pl.BlockSpec((1, k), lambda i, j: (0, 0)),
            pl.BlockSpec((k, PROJ_TN), lambda i, j: (0, j)),
            pl.BlockSpec((1, PROJ_TN), lambda i, j: (0, j)),
        ],
        out_specs=pl.BlockSpec((tm, PROJ_TN), lambda i, j: (i, j)),
        scratch_shapes=[pltpu.VMEM((tm, k), BF16)],
        compiler_params=_params(("parallel", "arbitrary")),
        name=name,
    )(x, g.reshape(1, k), w, bias)


def _matmul_kernel(a_ref, w_ref, o_ref):
    o_ref[...] = _dot_bf16(a_ref[...], w_ref[...].astype(BF16))


def _matmul(a, w, tm, tn, name):
    m, k = a.shape
    n = w.shape[1]
    return pl.pallas_call(
        _matmul_kernel,
        out_shape=jax.ShapeDtypeStruct((m, n), F32),
        grid=(m // tm, n // tn),
        in_specs=[pl.BlockSpec((tm, k), lambda i, j: (i, 0)), pl.BlockSpec((k, tn), lambda i, j: (0, j))],
        out_specs=pl.BlockSpec((tm, tn), lambda i, j: (i, j)),
        compiler_params=_params(("parallel", "arbitrary")),
        name=name,
    )(a, w)


def _prep_kernel(p_ref, prev_ref, mu_ref, wl_ref, w0_ref, a0_ref, kk_ref, ka_ref, rk_ref, lmat_ref, sel_ref,
                 hs_ref, hb_ref, rt_ref, kt_ref, qt_ref, pt_ref, v_ref, bonus_ref, g_ref, gc_ref, *, rows_are_time):
    p = p_ref[...]
    tm = p.shape[0]
    if rows_are_time:
        last = jnp.where(pl.program_id(0) == 0, 0.0, prev_ref[7:8, :])
        row = lax.broadcasted_iota(jnp.int32, (tm, 1), 0)
        prev = jnp.where(row == 0, last, pltpu.roll(p, 1, axis=0))
    else:
        prev = prev_ref[...]
    xm = p + mu_ref[...] * (prev - p)
    r = xm[:, 0:D_A]
    k = xm[:, D_A:2 * D_A]
    v = xm[:, 2 * D_A:3 * D_A]
    xl = xm[:, 3 * D_A:D_SHIFT]
    lane = lax.broadcasted_iota(jnp.int32, xl.shape, 1)
    act = jnp.where(lane < D_DECAY_LORA, jnp.tanh(xl),
                    jnp.where(lane < D_DECAY_LORA + D_AAA_LORA, xl, _sigmoid(xl)))
    act_hi, act_lo = _split2(act)
    w_hi, w_lo = wl_ref[0], wl_ref[1]
    lora = _dot_bf16(act_hi, w_hi) + _dot_bf16(act_hi, w_lo) + _dot_bf16(act_lo, w_hi)
    y = -(w0_ref[...] + lora[:, 0:D_A])
    softplus = jnp.maximum(y, 0.0) + jnp.log(1.0 + jnp.exp(-jnp.abs(y)))
    logw = -jnp.exp(-softplus - 0.5)
    a = _sigmoid(a0_ref[...] + lora[:, D_A:2 * D_A])
    g = lora[:, 2 * D_A:3 * D_A]
    kkr = k * kk_ref[...]
    kp = k * (1.0 + (a - 1.0) * ka_ref[...])
    seg = _head_sums(jnp.concatenate([kkr * kkr, r * kp * rk_ref[...]], axis=0), hs_ref[...], hb_ref[...])
    kk = kkr / jnp.maximum(jnp.sqrt(seg[0:tm]), 1e-12)
    cum = _dot_exact_lhs(lmat_ref[...], logw)
    e_out = jnp.exp(-cum)
    rt_ref[...] = r * jnp.exp(cum)
    kt_ref[...] = kp * e_out
    qt_ref[...] = kk * a * e_out
    pt_ref[...] = -kk * jnp.exp(cum - logw)
    v_ref[...] = v
    bonus_ref[...] = seg[tm:2 * tm] * v
    g_ref[...] = g
    gc_ref[...] = jnp.exp(_dot_exact_lhs(sel_ref[...], cum))


def _rwkv_prep(proj, prev, tm, chunk, lw, name):
    m = proj.shape[0]
    rows_are_time = prev is None
    n_tiles = m // tm
    t = np.arange(tm)
    lmat = ((t[:, None] // chunk == t[None, :] // chunk) & (t[None, :] <= t[:, None])).astype(np.float32)
    sel_stride = 1 if chunk == 1 else 8
    n_sel = sel_stride * (tm // chunk)
    sel = np.zeros((n_sel, tm), np.float32)
    for c in range(tm // chunk):
        sel[sel_stride * c, (c + 1) * chunk - 1] = 1.0
    if rows_are_time:
        prev_arr = proj
        prev_spec = pl.BlockSpec((8, D_SHIFT), lambda i: (jnp.maximum(i * (tm // 8) - 1, 0), 0))
    else:
        prev_arr = prev
        prev_spec = pl.BlockSpec((tm, D_SHIFT), lambda i: (i, 0))
    vec = lambda d: pl.BlockSpec((1, d), lambda i: (0, 0))
    full = lambda a: pl.BlockSpec(a.shape, lambda i: (0, 0))
    big = pl.BlockSpec((tm, D_A), lambda i: (i, 0))
    out = jax.ShapeDtypeStruct((m, D_A), F32)
    lmat_b = jnp.asarray(lmat, BF16)
    sel_b = jnp.asarray(sel, BF16)
    return pl.pallas_call(
        functools.partial(_prep_kernel, rows_are_time=rows_are_time),
        out_shape=[out] * 7 + [jax.ShapeDtypeStruct((n_tiles * n_sel, D_A), F32)],
        grid=(n_tiles,),
        in_specs=[pl.BlockSpec((tm, D_SHIFT), lambda i: (i, 0)), prev_spec, vec(D_SHIFT),
                  pl.BlockSpec(lw["w_lora"].shape, lambda i: (0, 0, 0)),
                  vec(D_A), vec(D_A), vec(D_A), vec(D_A), vec(D_A), full(lmat_b), full(sel_b), full(lw["hsum"]),
                  full(lw["hbc"])],
        out_specs=[big] * 7 + [pl.BlockSpec((n_sel, D_A), lambda i: (i, 0))],
        compiler_params=_params(("parallel",)),
        name=name,
    )(proj, prev_arr, lw["mu"], lw["w_lora"], lw["w0"], lw["a0"], lw["k_k"], lw["k_a"], lw["r_k"], lmat_b, sel_b,
      lw["hsum"], lw["hbc"])


def _bdot3(a, b, dims):
    ah, al = _split2(a)
    bh, bl = _split2(b)
    (ca,), (cb,) = dims[0]
    a_cat = jnp.concatenate([ah, ah, al, jnp.zeros_like(al)], axis=ca)
    b_cat = jnp.concatenate([bh, bl, bh, jnp.zeros_like(bl)], axis=cb)
    return lax.dot_general(a_cat, b_cat, dims, preferred_element_type=F32)


_NN = (((2,), (1,)), ((0,), (0,)))
_NT = (((2,), (2,)), ((0,), (0,)))


def _scan_kernel(pt_ref, rt_ref, qt_ref, kt_ref, v_ref, gc_ref, bonus_ref, g_ref, gng_ref, gnb_ref, ya_ref, sfin_ref,
                 s_scr):
    @pl.when(pl.program_id(0) == 0)
    def _():
        s_scr[...] = jnp.zeros_like(s_scr)

    def heads(x):
        return jnp.stack([x[:, h * HEAD_A:(h + 1) * HEAD_A] for h in range(H_A)], axis=0)

    p, r, q, k, v = (heads(ref[...]) for ref in (pt_ref, rt_ref, qt_ref, kt_ref, v_ref))
    gam = heads(gc_ref[0:1, :])
    ri = lax.broadcasted_iota(jnp.int32, (1, CHUNK, CHUNK), 1)
    ci = lax.broadcasted_iota(jnp.int32, (1, CHUNK, CHUNK), 2)
    strict = ri > ci
    incl = ri >= ci
    eye = (ri == ci).astype(F32)

    gram = _bdot3(jnp.concatenate([p, r], axis=1), jnp.concatenate([q, k], axis=1), _NT)
    a_qp = jnp.where(strict, gram[:, 0:CHUNK, 0:CHUNK], 0.0)
    a_kp = jnp.where(strict, gram[:, 0:CHUNK, CHUNK:], 0.0)
    a_rq = jnp.where(incl, gram[:, CHUNK:, 0:CHUNK], 0.0)
    a_rk = jnp.where(incl, gram[:, CHUNK:, CHUNK:], 0.0)
    same = lambda log2_bs: (ri >> log2_bs) == (ci >> log2_bs)
    a_d = jnp.where(same(4), a_qp, 0.0)
    tinv = eye + a_d
    pw = _bdot3(a_d, a_d, _NN)
    for _ in range(2):
        both = _bdot3(jnp.concatenate([pw, tinv], axis=1), pw, _NN)
        pw = both[:, 0:CHUNK]
        tinv = tinv + both[:, CHUNK:]
    tinv = tinv + _bdot3(tinv, pw, _NN)
    for log2_bs in (4, 5):
        off = jnp.where(jnp.logical_and(same(log2_bs + 1), jnp.logical_not(same(log2_bs))), a_qp, 0.0)
        tinv = tinv + _bdot3(tinv, _bdot3(off, tinv, _NN), _NN)
    av = _bdot3(jnp.concatenate([a_kp, a_rk], axis=1), v, _NN)
    pw_hat = _bdot3(tinv, jnp.concatenate([p, av[:, 0:CHUNK]], axis=2), _NN)
    x = _bdot3(a_rq, pw_hat, _NN)
    r_hat = r + x[:, :, 0:HEAD_A]
    o_loc = x[:, :, HEAD_A:] + av[:, CHUNK:]
    y = _bdot3(jnp.swapaxes(pw_hat, 1, 2), q, _NN)
    m_mat = (eye + y[:, 0:HEAD_A]) * gam
    n_mat = (y[:, HEAD_A:] + _bdot3(jnp.swapaxes(v, 1, 2), k, _NN)) * gam

    s0 = s_scr[...]
    o = _bdot3(r_hat, s0, _NT) + o_loc
    s_scr[...] = _bdot3(s0, m_mat, _NN) + n_mat

    mu = jnp.mean(o, axis=2, keepdims=True)
    d = o - mu
    o_n = d * lax.rsqrt(jnp.mean(d * d, axis=2, keepdims=True) + EPS_GN)
    o_n = jnp.concatenate([o_n[h] for h in range(H_A)], axis=1)
    ya_ref[...] = ((o_n * gng_ref[...] + gnb_ref[...] + bonus_ref[...]) * g_ref[...]).astype(ya_ref.dtype)

    @pl.when(pl.program_id(0) == pl.num_programs(0) - 1)
    def _():
        sfin_ref[...] = s_scr[...]


def _wkv_scan(pt, rt, qt, kt, v, gc, bonus, g, lw):
    t = pt.shape[0]
    blk = pl.BlockSpec((CHUNK, D_A), lambda c: (c, 0))
    vec = pl.BlockSpec((1, D_A), lambda c: (0, 0))
    return pl.pallas_call(
        _scan_kernel,
        out_shape=[jax.ShapeDtypeStruct((t, D_A), BF16), jax.ShapeDtypeStruct((H_A, HEAD_A, HEAD_A), F32)],
        grid=(t // CHUNK,),
        in_specs=[blk] * 5 + [pl.BlockSpec((8, D_A), lambda c: (c, 0)), blk, blk, vec, vec],
        out_specs=[blk, pl.BlockSpec((H_A, HEAD_A, HEAD_A), lambda c: (0, 0, 0))],
        scratch_shapes=[pltpu.VMEM((H_A, HEAD_A, HEAD_A), F32)],
        compiler_params=_params(("arbitrary",)),
        name="wkv_scan_prompt",
    )(pt, rt, qt, kt, v, gc, bonus, g, lw["gn_g"], lw["gn_b"])


def _wkv_step_kernel(s_ref, rows_ref, sn_ref, o_ref):
    ri = lax.broadcasted_iota(jnp.int32, (1, 1, HEAD_A, HEAD_A), 2)
    ci = lax.broadcasted_iota(jnp.int32, (1, 1, HEAD_A, HEAD_A), 3)
    eye = ri == ci
    pt, rt, qt, kt, v, gc = (rows_ref[:, n] for n in range(6))
    s = s_ref[0]
    u = jnp.sum(s * pt, axis=3, keepdims=True)
    v_col = jnp.sum(jnp.where(eye, v, 0.0), axis=3, keepdims=True)
    m = s + u * qt + v_col * kt
    sn_ref[0] = m * gc
    o_col = jnp.sum(m * rt, axis=3, keepdims=True)
    o_ref[...] = jnp.sum(jnp.where(eye, o_col, 0.0), axis=2, keepdims=True)


def _wkv_step(state, pt, rt, qt, kt, v, gc, bt):
    b = pt.shape[0]
    rows = jnp.stack([pt, rt, qt, kt, v, gc], axis=1).reshape(b, 6, H_A, 1, HEAD_A)
    st_spec = pl.BlockSpec((1, bt, H_A, HEAD_A, HEAD_A), lambda i: (0, i, 0, 0, 0))
    sn, o = pl.pallas_call(
        _wkv_step_kernel,
        out_shape=[jax.ShapeDtypeStruct(state.shape, F32), jax.ShapeDtypeStruct((b, H_A, 1, HEAD_A), F32)],
        grid=(b // bt,),
        in_specs=[st_spec, pl.BlockSpec((bt, 6, H_A, 1, HEAD_A), lambda i: (i, 0, 0, 0, 0))],
        out_specs=[st_spec, pl.BlockSpec((bt, H_A, 1, HEAD_A), lambda i: (i, 0, 0, 0))],
        compiler_params=_params(("parallel",)),
        name="wkv_step_sample",
    )(state, rows)
    return sn, o.reshape(b, D_A)


def _post_kernel(o_ref, bonus_ref, g_ref, gng_ref, gnb_ref, hs_ref, hb_ref, y_ref):
    o = o_ref[...]
    hsum, hbc = hs_ref[...], hb_ref[...]
    mu = _head_sums(o, hsum, hbc) * (1.0 / HEAD_A)
    d = o - mu
    var = _head_sums(d * d, hsum, hbc) * (1.0 / HEAD_A)
    o_n = d * lax.rsqrt(var + EPS_GN) * gng_ref[...] + gnb_ref[...]
    y_ref[...] = ((o_n + bonus_ref[...]) * g_ref[...]).astype(y_ref.dtype)


def _rwkv_post(o, bonus, g, lw, tm, name):
    m = o.shape[0]
    big = pl.BlockSpec((tm, D_A), lambda i: (i, 0))
    vec = pl.BlockSpec((1, D_A), lambda i: (0, 0))
    return pl.pallas_call(
        _post_kernel,
        out_shape=jax.ShapeDtypeStruct((m, D_A), BF16),
        grid=(m // tm,),
        in_specs=[big, big, big, vec, vec, pl.BlockSpec((D_A, LANES), lambda i: (0, 0)),
                  pl.BlockSpec((LANES, D_A), lambda i: (0, 0))],
        out_specs=big,
        compiler_params=_params(("parallel",)),
        name=name,
    )(o, bonus, g, lw["gn_g"], lw["gn_b"], lw["hsum"], lw["hbc"])


def _attn_kernel(q_ref, kc_ref, vc_ref, kh_ref, vh_ref, bias_ref, o_ref, l_ref, *, dil, m_blocks, tiled):
    scale = HEAD_B ** -0.5
    n_streams = SUBLANES if tiled else dil
    n_units = n_streams * m_blocks

    stride = SUBLANES if tiled else dil
    if tiled:
        flat = lambda ref: ref.reshape(ref.shape[0] * SUBLANES, LANES)
        q_ref, kc_ref, vc_ref, kh_ref, vh_ref, o_ref, l_ref = map(flat, (q_ref, kc_ref, vc_ref, kh_ref, vh_ref,
                                                                          o_ref, l_ref))

    def rows(ref, r, start, size):
        if stride == 1:
            return ref[pl.ds(start, size), :]
        return ref[pl.ds(r + stride * start, size, stride=stride), :]

    def put(ref, r, start, val):
        if stride == 1:
            ref[pl.ds(start, BLK), :] = val
        else:
            ref[pl.ds(r + stride * start, BLK, stride=stride), :] = val

    qs, ks, vs, units = [], [], [], []
    for r in range(n_streams):
        for mb in range(m_blocks):
            units.append((r, mb))
            qs.append(rows(q_ref, r, BLK * mb, BLK))
            if mb == 0:
                ks.append(jnp.concatenate([rows(kh_ref, r, 0, BLK), rows(kc_ref, r, 0, BLK)], axis=0))
                vs.append(jnp.concatenate([rows(vh_ref, r, 0, BLK), rows(vc_ref, r, 0, BLK)], axis=0))
            else:
                ks.append(rows(kc_ref, r, BLK * (mb - 1), 2 * BLK))
                vs.append(rows(vc_ref, r, BLK * (mb - 1), 2 * BLK))

    def batch(xs, mul=None):
        pre = (lambda x: x) if mul is None else (lambda x: x * mul)
        return jnp.stack([pre(x[:, sub * HEAD_B:(sub + 1) * HEAD_B]).astype(BF16) for sub in range(2) for x in xs],
                         axis=0)

    assert math.frexp(scale)[0] == 0.5
    qb, kb, vb = batch(qs, scale), batch(ks), batch(vs)
    s = lax.dot_general(qb, kb, (((2,), (2,)), ((0,), (0,))), preferred_element_type=F32)
    s = s.reshape(2, n_units, BLK, 2 * BLK) + bias_ref[...][:, None]
    unit = lax.broadcasted_iota(jnp.int32, (1, n_units, 1, 2 * BLK), 1)
    col = lax.broadcasted_iota(jnp.int32, (1, n_units, 1, 2 * BLK), 3)
    no_prev = jnp.logical_and(jnp.logical_and(pl.program_id(0) == 0, (unit & (m_blocks - 1)) == 0), col < BLK)
    s = jnp.where(no_prev, -jnp.inf, s)
    m = jnp.max(s, axis=-1, keepdims=True)
    p = jnp.exp(s - m)
    l = jnp.sum(p, axis=-1, keepdims=True)
    pv = lax.dot_general(p.astype(BF16).reshape(2 * n_units, BLK, 2 * BLK), vb, (((2,), (1,)), ((0,), (0,))),
                         preferred_element_type=F32)
    o = pv.reshape(2, n_units, BLK, HEAD_B) / l
    lse = jnp.broadcast_to(m + jnp.log(l), (2, n_units, BLK, HEAD_B))
    for u, (r, mb) in enumerate(units):
        put(o_ref, r, BLK * mb, jnp.concatenate([o[0, u], o[1, u]], axis=1))
        put(l_ref, r, BLK * mb, jnp.concatenate([lse[0, u], lse[1, u]], axis=1))


def _attn_prompt(proj, bias, gi, rows_per_step):
    t, n_cols = proj.shape
    dil = DILATIONS[gi]
    span = BLK * dil
    tiled = dil % SUBLANES == 0
    if tiled:
        rows_per_step *= dil // SUBLANES
    m_blocks = rows_per_step // span
    cq = (COL_QKV + gi * D_G) // LANES
    ck = (COL_QKV + D_B + gi * D_G) // LANES
    cv = (COL_QKV + 2 * D_B + gi * D_G) // LANES
    bias_block = (2, BLK, 2 * BLK)
    if tiled:
        n_hi = dil // SUBLANES
        src = proj.reshape(t // dil, n_hi, SUBLANES, n_cols)
        grid = (t // rows_per_step, n_hi, H_G // 2)
        l_rows = rows_per_step // dil
        cur = lambda c0: pl.BlockSpec((l_rows, None, SUBLANES, LANES), lambda i, rh, hp: (i, rh, 0, c0 + hp))
        halo = lambda c0: pl.BlockSpec((BLK, None, SUBLANES, LANES),
                                       lambda i, rh, hp: (jnp.maximum(i * m_blocks - 1, 0), rh, 0, c0 + hp))
        out_spec = pl.BlockSpec((l_rows, None, SUBLANES, LANES), lambda i, rh, hp: (i, rh, 0, hp))
        out = jax.ShapeDtypeStruct((t // dil, n_hi, SUBLANES, D_G), F32)
        bias_spec = pl.BlockSpec(bias_block, lambda i, rh, hp: (hp, 0, 0))
    else:
        src = proj
        grid = (t // rows_per_step, H_G // 2)
        cur = lambda c0: pl.BlockSpec((rows_per_step, LANES), lambda i, hp: (i, c0 + hp))
        halo = lambda c0: pl.BlockSpec((span, LANES), lambda i, hp: (jnp.maximum(i * m_blocks - 1, 0), c0 + hp))
        out_spec = pl.BlockSpec((rows_per_step, LANES), lambda i, hp: (i, hp))
        out = jax.ShapeDtypeStruct((t, D_G), F32)
        bias_spec = pl.BlockSpec(bias_block, lambda i, hp: (hp, 0, 0))
    o, l = pl.pallas_call(
        functools.partial(_attn_kernel, dil=dil, m_blocks=m_blocks, tiled=tiled),
        out_shape=[out, out],
        grid=grid,
        in_specs=[cur(cq), cur(ck), cur(cv), halo(ck), halo(cv), bias_spec],
        out_specs=[out_spec, out_spec],
        compiler_params=_params(("parallel",) * len(grid)),
        name=f"attn_prompt_g{gi}",
    )(src, src, src, src, src, bias)
    return o.reshape(t, D_G), l.reshape(t, D_G)


def _attn_step_kernel(q_ref, kn_ref, vn_ref, c_ref, bias_ref, bias0_ref, o_ref, l_ref):
    scale = HEAD_B ** -0.5
    ri = lax.broadcasted_iota(jnp.int32, (1, 1, HEAD_B, HEAD_B), 2)
    ci = lax.broadcasted_iota(jnp.int32, (1, 1, HEAD_B, HEAD_B), 3)
    eye = ri == ci
    q = q_ref[...]
    q_col = jnp.sum(jnp.where(eye, q, 0.0), axis=3, keepdims=True)
    s = jnp.sum(c_ref[:, 0] * q_col, axis=2, keepdims=True) * scale + bias_ref[...]
    s_new = jnp.sum(q * kn_ref[...], axis=3, keepdims=True) * scale + bias0_ref[...]
    m = jnp.maximum(jnp.max(s, axis=3, keepdims=True), s_new)
    p = jnp.exp(s - m)
    p_new = jnp.exp(s_new - m)
    l = jnp.sum(p, axis=3, keepdims=True) + p_new
    o_col = jnp.sum(c_ref[:, 1] * p, axis=3, keepdims=True)
    o_row = jnp.sum(jnp.where(eye, o_col, 0.0), axis=2, keepdims=True)
    o_ref[...] = (o_row + p_new * vn_ref[...]) / l
    l_ref[...] = jnp.broadcast_to(m + jnp.log(l), o_ref.shape)


def _attn_step(q, k_new, v_new, cache, bias, bias0, gi):
    b = q.shape[0]
    w = cache.shape[2]
    bt = max(1, min(b, STEP_CACHE_WORDS // (2 * D_G * w)))
    cache_t = jnp.transpose(cache, (0, 1, 3, 4, 5, 2))
    vec = pl.BlockSpec((bt, H_G, 1, HEAD_B), lambda i: (i, 0, 0, 0))
    out = jax.ShapeDtypeStruct((b, H_G, 1, HEAD_B), F32)
    o, l = pl.pallas_call(
        _attn_step_kernel,
        out_shape=[out, out],
        grid=(b // bt,),
        in_specs=[vec, vec, vec,
                  pl.BlockSpec((None, bt, 2, H_G, HEAD_B, w), lambda i: (0, i, 0, 0, 0, 0)),
                  pl.BlockSpec((H_G, 1, w), lambda i: (0, 0, 0)),
                  pl.BlockSpec((H_G, 1, 1), lambda i: (0, 0, 0))],
        out_specs=[vec, vec],
        compiler_params=_params(("parallel",)),
        name=f"attn_step_g{gi}",
    )(q, k_new, v_new, cache_t, bias, bias0)
    return o.reshape(b, D_G), l.reshape(b, D_G)


def _mix_kernel(ya_ref, o1_ref, o2_ref, o3_ref, l1_ref, l2_ref, l3_ref, wa_ref, wb_ref, ga_ref, gb_ref, o_ref,
                yb_scr):
    @pl.when(pl.program_id(1) == 0)
    def _():
        l1, l2, l3 = l1_ref[...], l2_ref[...], l3_ref[...]
        m = jnp.maximum(jnp.maximum(l1, l2), l3)
        e1, e2, e3 = jnp.exp(l1 - m), jnp.exp(l2 - m), jnp.exp(l3 - m)
        den = e1 + e2 + e3
        yb = (e1 / den) * o1_ref[...] + (e2 / den) * o2_ref[...] + (e3 / den) * o3_ref[...]
        yb_scr[...] = yb.astype(BF16)

    mixed = ga_ref[...] * _dot_bf16(ya_ref[...], wa_ref[...]) + gb_ref[...] * _dot_bf16(yb_scr[...], wb_ref[...])
    o_ref[...] = mixed.astype(o_ref.dtype)


def _mix(ya, outs, lses, wa, wb, proj, tm, tn, name):
    m = ya.shape[0]
    ga0 = COL_GATE // tn
    gb0 = (COL_GATE + D_MODEL) // tn
    grp = pl.BlockSpec((tm, D_G), lambda i, j: (i, 0))
    return pl.pallas_call(
        _mix_kernel,
        out_shape=jax.ShapeDtypeStruct((m, D_MODEL), BF16),
        grid=(m // tm, D_MODEL // tn),
        in_specs=[pl.BlockSpec((tm, D_A), lambda i, j: (i, 0))] + [grp] * 6 + [
            pl.BlockSpec((D_A, tn), lambda i, j: (0, j)),
            pl.BlockSpec((D_G, tn), lambda i, j: (0, j)),
            pl.BlockSpec((tm, tn), lambda i, j: (i, ga0 + j)),
            pl.BlockSpec((tm, tn), lambda i, j: (i, gb0 + j)),
        ],
        out_specs=pl.BlockSpec((tm, tn), lambda i, j: (i, j)),
        scratch_shapes=[pltpu.VMEM((tm, D_G), BF16)],
        compiler_params=_params(("parallel", "arbitrary")),
        name=name,
    )(ya, *outs, *lses, wa, wb, proj, proj)


def _wo_kernel(a_ref, w_ref, x_ref, g_ref, h_ref, hn_ref):
    h = x_ref[...] + _dot_bf16(a_ref[...], w_ref[...])
    h_ref[...] = h
    hn_ref[...] = _rms(h, g_ref[...]).astype(BF16)


def _wo(a, w, x, g, tm, name):
    m, k = a.shape
    n = w.shape[1]
    row = pl.BlockSpec((tm, n), lambda i: (i, 0))
    return pl.pallas_call(
        _wo_kernel,
        out_shape=[jax.ShapeDtypeStruct((m, n), F32), jax.ShapeDtypeStruct((m, n), BF16)],
        grid=(m // tm,),
        in_specs=[pl.BlockSpec((tm, k), lambda i: (i, 0)), pl.BlockSpec((k, n), lambda i: (0, 0)), row,
                  pl.BlockSpec((1, n), lambda i: (0, 0))],
        out_specs=[row, row],
        compiler_params=_params(("parallel",)),
        name=name,
    )(a, w, x, g.reshape(1, n))


def _gelu(x):
    return 0.5 * x * (1.0 + lax.erf(x * (1.0 / math.sqrt(2.0))))


def _ffn_up_kernel(h_ref, w1_ref, w2_ref, cw1_ref, cw2_ref, cb1_ref, cb2_ref, act_ref, t1_ref, t2_ref,
                   c1_scr, c2_scr, w1_scr, w2_scr):
    tm = h_ref.shape[0]

    @pl.when(pl.program_id(1) == 0)
    def _():
        c1_scr[...] = jnp.zeros_like(c1_scr)
        c2_scr[...] = jnp.zeros_like(c2_scr)
        w1_scr[...] = w1_ref[...].astype(BF16)
        w2_scr[...] = w2_ref[...].astype(BF16)

    a = h_ref[...]
    row = lax.broadcasted_iota(jnp.int32, (tm, 1), 0)

    def conv(u, carry_ref, cw_ref, cb_ref):
        m1 = jnp.where(row == 0, carry_ref[7:8, :], pltpu.roll(u, 1, axis=0))
        m2 = jnp.where(row == 0, carry_ref[6:7, :], jnp.where(row == 1, carry_ref[7:8, :], pltpu.roll(u, 2, axis=0)))
        return cb_ref[...] + cw_ref[0:1, :] * m2 + cw_ref[1:2, :] * m1 + cw_ref[2:3, :] * u

    u1 = _dot_bf16(a, w1_scr[...])
    u2 = _dot_bf16(a, w2_scr[...])
    c1 = conv(u1, c1_scr, cw1_ref, cb1_ref)
    c2 = conv(u2, c2_scr, cw2_ref, cb2_ref)
    act_ref[...] = (_gelu(c1) * c2).astype(act_ref.dtype)
    c1_scr[...] = u1[tm - 8:tm]
    c2_scr[...] = u2[tm - 8:tm]
    t1_ref[...] = u1[tm - 8:tm]
    t2_ref[...] = u2[tm - 8:tm]


def _ffn_up_prompt(hn, w_up, conv_w, conv_b, tm, tn):
    m = hn.shape[0]
    nj = D_FF // tn
    tail = jax.ShapeDtypeStruct((8, D_FF), F32)
    return pl.pallas_call(
        _ffn_up_kernel,
        out_shape=[jax.ShapeDtypeStruct((m, D_FF), BF16), tail, tail],
        grid=(nj, m // tm),
        in_specs=[
            pl.BlockSpec((tm, D_MODEL), lambda j, i: (i, 0)),
            pl.BlockSpec((D_MODEL, tn), lambda j, i: (0, j)),
            pl.BlockSpec((D_MODEL, tn), lambda j, i: (0, nj + j)),
            pl.BlockSpec((CONV_W, tn), lambda j, i: (0, j)),
            pl.BlockSpec((CONV_W, tn), lambda j, i: (0, nj + j)),
            pl.BlockSpec((1, tn), lambda j, i: (0, j)),
            pl.BlockSpec((1, tn), lambda j, i: (0, nj + j)),
        ],
        out_specs=[pl.BlockSpec((tm, tn), lambda j, i: (i, j)),
                   pl.BlockSpec((8, tn), lambda j, i: (0, j)),
                   pl.BlockSpec((8, tn), lambda j, i: (0, j))],
        scratch_shapes=[pltpu.VMEM((8, tn), F32), pltpu.VMEM((8, tn), F32),
                        pltpu.VMEM((D_MODEL, tn), BF16), pltpu.VMEM((D_MODEL, tn), BF16)],
        compiler_params=_params(("arbitrary", "arbitrary")),
        name="ffn_up_prompt",
    )(hn, w_up, w_up, conv_w, conv_w, conv_b, conv_b)


def _ffn_act_step_kernel(up_ref, prev_ref, cw_ref, cb_ref, act_ref):
    up = up_ref[...]
    w = 2 * D_FF
    c = cb_ref[...] + cw_ref[0:1, :] * prev_ref[:, 0:w] + cw_ref[1:2, :] * prev_ref[:, w:2 * w] + cw_ref[2:3, :] * up
    act_ref[...] = (_gelu(c[:, 0:D_FF]) * c[:, D_FF:w]).astype(act_ref.dtype)


def _ffn_act_step(up, conv_prev, conv_w, conv_b):
    b = up.shape[0]
    return pl.pallas_call(
        _ffn_act_step_kernel,
        out_shape=jax.ShapeDtypeStruct((b, D_FF), BF16),
        compiler_params=pltpu.CompilerParams(vmem_limit_bytes=VMEM_LIMIT),
        name="ffn_act_sample",
    )(up, conv_prev.reshape(b, (CONV_W - 1) * 2 * D_FF), conv_w, conv_b)


def _down_kernel(a_ref, w_ref, h_ref, g_ref, o_ref, acc_ref):
    @pl.when(pl.program_id(1) == 0)
    def _():
        acc_ref[...] = h_ref[...]

    out = acc_ref[...] + _dot_bf16(a_ref[...], w_ref[...])
    acc_ref[...] = out
    inv = lax.rsqrt(jnp.mean(out * out, axis=-1, keepdims=True) + EPS_RMS)
    o_ref[...] = out * inv * g_ref[...]


def _ffn_down(act, w_down, h, g, tm, tk, name):
    m = act.shape[0]
    nk = D_FF // tk
    return pl.pallas_call(
        _down_kernel,
        out_shape=jax.ShapeDtypeStruct((m, D_MODEL), F32),
        grid=(m // tm, nk),
        in_specs=[
            pl.BlockSpec((tm, tk), lambda i, k: (i, k)),
            pl.BlockSpec((tk, D_MODEL), lambda i, k: (k, 0)),
            pl.BlockSpec((tm, D_MODEL), lambda i, k: (i, 0)),
            pl.BlockSpec((1, D_MODEL), lambda i, k: (0, 0)),
        ],
        out_specs=pl.BlockSpec((tm, D_MODEL), lambda i, k: (i, 0)),
        scratch_shapes=[pltpu.VMEM((tm, D_MODEL), F32)],
        compiler_params=_params(("parallel", "arbitrary")),
        name=name,
    )(act, w_down, h, g.reshape(1, D_MODEL))


def _rel_bucket(dist):
    max_exact = N_BUCKETS // 2
    d_f = jnp.maximum(dist, 1).astype(F32)
    large = max_exact + (jnp.log(d_f / max_exact) / math.log(MAX_DISTANCE / max_exact)
                         * (N_BUCKETS - max_exact)).astype(jnp.int32)
    large = jnp.minimum(large, N_BUCKETS - 1)
    return jnp.where(dist < max_exact, dist, large)


def _bias_rows(tab, dist):
    onehot = (_rel_bucket(dist)[None, :] == jnp.arange(N_BUCKETS)[:, None]).astype(F32)
    return jnp.dot(tab.T, onehot, precision=HIGHEST)


def _bias_tables(rel_bias, gi):
    dil = DILATIONS[gi]
    win = WINDOWS[gi]
    reach = win // dil
    tab = rel_bias[:, gi * H_G:(gi + 1) * H_G]
    qi = jnp.arange(BLK)[:, None]
    ki = jnp.arange(2 * BLK)[None, :]
    rel = qi + BLK - ki
    blk = _bias_rows(tab, (dil * jnp.maximum(rel, 0)).reshape(-1)).reshape(H_G, BLK, 2 * BLK)
    blk = jnp.where(((rel >= 0) & (rel <= reach))[None], blk, -jnp.inf)
    back = win - jnp.arange(win)
    step = jnp.where((back % dil == 0)[None, :], _bias_rows(tab, back), -jnp.inf)[:, None, :]
    step0 = _bias_rows(tab, jnp.zeros((1,), jnp.int32))[:, None, :]
    return blk, step, step0


def _tiles(m, prompt):
    if prompt:
        return dict(tag="prompt", proj_m=1024, prep_m=2 * CHUNK, post_m=256, attn_rows=2048, mix_m=512, mix_n=1024,
                    wo_m=512, up_m=1024, up_n=512, down_m=512, down_k=D_FF // 2)
    return dict(tag="sample", proj_m=m, prep_m=m, step_b=4, post_m=m, mix_m=m, mix_n=1024, wo_m=m, up_n=512, down_m=m,
                down_k=D_FF // 2)


def _layer(x, lw, bias_blk, prompt, state=None, proj=None):
    m = x.shape[0]
    tiles = _tiles(m, prompt)
    if proj is None:
        proj = _proj(x, lw["norm1_g"], lw["w_in"], lw["in_bias"], tiles["proj_m"], "proj_" + tiles["tag"])

    if prompt:
        rt, kt, qt, pt, v, bonus, g, gc = _rwkv_prep(proj, None, tiles["prep_m"], CHUNK, lw, "rwkv_prep_prompt")
        ya, wkv_new = _wkv_scan(pt, rt, qt, kt, v, gc, bonus, g, lw)
        wkv_new = wkv_new[None, None]
    else:
        rt, kt, qt, pt, v, bonus, g, gc = _rwkv_prep(proj, state["shift"], tiles["prep_m"], 1, lw, "rwkv_prep_sample")
        wkv_new, o = _wkv_step(state["wkv"], pt, rt, qt, kt, v, gc, tiles["step_b"])
        ya = _rwkv_post(o, bonus, g, lw, tiles["post_m"], "rwkv_post_" + tiles["tag"])

    outs, lses = [], []
    for gi in range(N_GROUPS):
        if prompt:
            o_g, l_g = _attn_prompt(proj, bias_blk[gi][0], gi, tiles["attn_rows"])
        else:
            sl = lambda c0: proj[:, c0 + gi * D_G:c0 + (gi + 1) * D_G].reshape(m, H_G, 1, HEAD_B)
            o_g, l_g = _attn_step(sl(COL_QKV), sl(COL_QKV + D_B), sl(COL_QKV + 2 * D_B), state["win"][gi],
                                  bias_blk[gi][1], bias_blk[gi][2], gi)
        outs.append(o_g)
        lses.append(l_g)
    mixed = _mix(ya, outs, lses, lw["w_out_a"], lw["w_out_b"], proj, tiles["mix_m"], tiles["mix_n"],
                 "mix_" + tiles["tag"])
    h, hn = _wo(mixed, lw["w_o"], x, lw["norm2_g"], tiles["wo_m"], "wo_" + tiles["tag"])

    if prompt:
        act, t1, t2 = _ffn_up_prompt(hn, lw["w_up"], lw["conv_w"], lw["conv_b"], tiles["up_m"], tiles["up_n"])
        conv_new = jnp.concatenate([t1[6:8], t2[6:8]], axis=1)[None, None]
    else:
        up = _matmul(hn, lw["w_up"], m, tiles["up_n"], "up_sample")
        act = _ffn_act_step(up, state["conv"], lw["conv_w"], lw["conv_b"])
        conv_new = jnp.concatenate([state["conv"][:, 1:], up[:, None, :]], axis=1)[None]
    y = _ffn_down(act, lw["w_down"], h, lw["normf_g"], tiles["down_m"], tiles["down_k"], "down_" + tiles["tag"])
    return y, proj, wkv_new, conv_new


def kernel(x_prompt, x_sample, state_wkv, state_shift, state_ffn_conv, cache_win1, cache_win2, cache_win3, rel_bias,
           norm1_g, w_in, gate_b, mu_shift, w0, w_up_decay, a0, w_up_aaa, w_up_gate, k_k, k_a, r_k, gn_g, gn_b,
           w_out_a, w_out_b, w_o, norm2_g, w_up, conv_w, conv_b, w_down, normf_g):
    row = lambda a: a.reshape(1, -1)
    w_lora = jnp.zeros((D_LORA, 3 * D_A), F32)
    w_lora = w_lora.at[0:D_DECAY_LORA, 0:D_A].set(w_up_decay[0])
    w_lora = w_lora.at[D_DECAY_LORA:D_DECAY_LORA + D_AAA_LORA, D_A:2 * D_A].set(w_up_aaa[0])
    w_lora = w_lora.at[D_DECAY_LORA + D_AAA_LORA:, 2 * D_A:].set(w_up_gate[0])
    head = np.arange(D_A) // HEAD_A
    lw = dict(
        norm1_g=norm1_g[0], norm2_g=norm2_g[0], normf_g=normf_g,
        in_bias=jnp.concatenate([jnp.zeros((1, COL_GATE), F32), row(gate_b[0])], axis=1),
        mu=row(mu_shift[0]), w_lora=jnp.stack(_split2(w_lora)), w0=row(w0[0]), a0=row(a0[0]), k_k=row(k_k[0]), k_a=row(k_a[0]),
        r_k=row(r_k[0]), gn_g=row(gn_g[0]), gn_b=row(gn_b[0]),
        hsum=jnp.asarray(head[:, None] == np.arange(LANES)[None, :], BF16),
        hbc=jnp.asarray(np.arange(LANES)[:, None] == head[None, :], BF16),
        w_out_a=w_out_a[0].astype(BF16), w_out_b=w_out_b[0].astype(BF16), w_o=w_o[0].astype(BF16),
        w_up=w_up[0], conv_w=conv_w[0], conv_b=row(conv_b[0]), w_down=w_down[0].astype(BF16),
    )
    bias = [_bias_tables(rel_bias, gi) for gi in range(N_GROUPS)]

    proj_s, lw["w_in"] = _proj_cast(x_sample[:, 0], norm1_g[0], w_in[0], lw["in_bias"])
    y_p, proj_p, wkv_p, conv_p = _layer(x_prompt[0], lw, bias, prompt=True)

    b = DEC_BATCH
    state = dict(wkv=state_wkv, shift=state_shift.reshape(b, D_SHIFT), conv=state_ffn_conv[0],
                 win=(cache_win1, cache_win2, cache_win3))
    y_s, proj_s, wkv_s, conv_s = _layer(x_sample[:, 0], lw, bias, prompt=False, state=state, proj=proj_s)

    def kv_rows(proj, lo, gi):
        k = proj[lo:, COL_QKV + D_B + gi * D_G:COL_QKV + D_B + (gi + 1) * D_G]
        v = proj[lo:, COL_QKV + 2 * D_B + gi * D_G:COL_QKV + 2 * D_B + (gi + 1) * D_G]
        n = k.shape[0]
        return jnp.stack([k.reshape(n, H_G, HEAD_B), v.reshape(n, H_G, HEAD_B)], axis=1)

    win_p = [kv_rows(proj_p, SEQ - min(WINDOWS[gi], SEQ), gi)[None, None] for gi in range(N_GROUPS)]
    win_s = [kv_rows(proj_s, 0, gi)[None, :, None] for gi in range(N_GROUPS)]
    return (y_p[None], y_s[:, None],
            wkv_p, wkv_s,
            proj_p[SEQ - 1:, 0:D_SHIFT][None, None], proj_s[:, 0:D_SHIFT][None, :, None],
            conv_p, conv_s,
            win_p[0], win_s[0], win_p[1], win_s[1], win_p[2], win_s[2])
```

```python
import functools
import math

import numpy as np
import jax
import jax.numpy as jnp
from jax import lax
from jax.experimental import pallas as pl
from jax.experimental.pallas import tpu as pltpu

F32 = jnp.float32
BF16 = jnp.bfloat16
HIGHEST = lax.Precision.HIGHEST

D_MODEL = 2048
SEQ = 8192
DEC_BATCH = 32
HEAD_A = 64
H_A = 16
D_A = H_A * HEAD_A
D_DECAY_LORA = 96
D_AAA_LORA = 96
D_GATE_LORA = 64
D_LORA = D_DECAY_LORA + D_AAA_LORA + D_GATE_LORA
D_SHIFT = 3 * D_A + D_LORA
EPS_GN = 64e-5
HEAD_B = 64
H_G = 8
D_G = H_G * HEAD_B
WINDOWS = (128, 512, 2048)
DILATIONS = (1, 4, 16)
N_GROUPS = 3
D_B = N_GROUPS * D_G
BLK = 128
N_BUCKETS = 32
MAX_DISTANCE = 2048
COL_QKV = D_SHIFT
COL_GATE_SRC = D_SHIFT + 3 * D_B
D_IN = COL_GATE_SRC + 2 * D_MODEL
PROJ_TN = 1024
COL_GATE = -(-COL_GATE_SRC // PROJ_TN) * PROJ_TN
D_IN_PAD = COL_GATE + 2 * D_MODEL
D_FF = 5632
CONV_W = 3
EPS_RMS = 1e-6
CHUNK = 64
LANES = 128
SUBLANES = 8
VMEM_LIMIT = 56 * 1024 * 1024
STEP_CACHE_WORDS = 2 * 1024 * 1024


def _params(sem, vmem=VMEM_LIMIT):
    return pltpu.CompilerParams(dimension_semantics=sem, vmem_limit_bytes=vmem)


def _sigmoid(x):
    return 1.0 / (1.0 + jnp.exp(-x))


def _dot_bf16(a, b):
    return jnp.dot(a, b, preferred_element_type=F32)


def _split2(x):
    hi = x.astype(BF16)
    return hi, (x - hi.astype(F32)).astype(BF16)


def _split3(x):
    hi = x.astype(BF16)
    r1 = x - hi.astype(F32)
    mid = r1.astype(BF16)
    lo = (r1 - mid.astype(F32)).astype(BF16)
    return hi, mid, lo


def _dot_exact_rhs(x, m):
    hi, mid, lo = _split3(x)
    return _dot_bf16(hi, m) + _dot_bf16(mid, m) + _dot_bf16(lo, m)


def _head_sums(x, hsum, hbc):
    return _dot_exact_rhs(_dot_exact_rhs(x, hsum), hbc)


def _dot_exact_lhs(m, x):
    hi, mid, lo = _split3(x)
    return _dot_bf16(m, hi) + _dot_bf16(m, mid) + _dot_bf16(m, lo)


def _rms(x, g):
    return x * lax.rsqrt(jnp.mean(x * x, axis=-1, keepdims=True) + EPS_RMS) * g


def _proj_cast_kernel(x_ref, g_ref, w_ref, b_ref, o_ref, wb_ref, xn_scr, *, gap_block):
    j = pl.program_id(0)

    @pl.when(j == 0)
    def _():
        xn_scr[...] = _rms(x_ref[...], g_ref[...]).astype(BF16)

    wb = jnp.where(j != gap_block, w_ref[...], 0.0).astype(BF16)
    wb_ref[...] = wb
    acc = _dot_bf16(xn_scr[...], wb)
    tn = o_ref.shape[1]
    col = lax.broadcasted_iota(jnp.int32, (1, tn), 1) + j * tn
    o_ref[...] = jnp.where(col >= COL_GATE, _sigmoid(acc + b_ref[...]), acc)


def _proj_cast(x, g, w, bias):
    m, k = x.shape
    tn = COL_GATE - COL_GATE_SRC
    assert tn % LANES == 0 and COL_GATE_SRC % tn == 0
    gap_block = COL_GATE_SRC // tn
    return pl.pallas_call(
        functools.partial(_proj_cast_kernel, gap_block=gap_block),
        out_shape=[jax.ShapeDtypeStruct((m, D_IN_PAD), F32), jax.ShapeDtypeStruct((k, D_IN_PAD), BF16)],
        grid=(D_IN_PAD // tn,),
        in_specs=[
            pl.BlockSpec((m, k), lambda j: (0, 0)),
            pl.BlockSpec((1, k), lambda j: (0, 0)),
            pl.BlockSpec((k, tn), lambda j: (0, jnp.where(j < gap_block, j, jnp.maximum(j - 1, 0)))),
            pl.BlockSpec((1, tn), lambda j: (0, j)),
        ],
        out_specs=[pl.BlockSpec((m, tn), lambda j: (0, j)), pl.BlockSpec((k, tn), lambda j: (0, j))],
        scratch_shapes=[pltpu.VMEM((m, k), BF16)],
        compiler_params=_params(("arbitrary",)),
        name="proj_sample_cast_w_in",
    )(x, g.reshape(1, k), w, bias)


def _proj_kernel(x_ref, g_ref, w_ref, b_ref, o_ref, xn_scr):
    @pl.when(pl.program_id(1) == 0)
    def _():
        xn_scr[...] = _rms(x_ref[...], g_ref[...]).astype(BF16)

    acc = _dot_bf16(xn_scr[...], w_ref[...])
    tn = o_ref.shape[1]
    col = lax.broadcasted_iota(jnp.int32, (1, tn), 1) + pl.program_id(1) * tn
    o_ref[...] = jnp.where(col >= COL_GATE, _sigmoid(acc + b_ref[...]), acc)


def _proj(x, g, w, bias, tm, name):
    m, k = x.shape
    n = w.shape[1]
    return pl.pallas_call(
        _proj_kernel,
        out_shape=jax.ShapeDtypeStruct((m, n), F32),
        grid=(m // tm, n // PROJ_TN),
        in_specs=[
            pl.BlockSpec((tm, k), lambda i, j: (i, 0)),
            pl.BlockSpec((1, k), lambda i, j: (0, 0)),
            pl.BlockSpec((k, PROJ_TN), lambda i, j: (0, j)),
            pl.BlockSpec((1, PROJ_TN), lambda i, j: (0, j)),
        ],
        out_specs=pl.BlockSpec((tm, PROJ_TN), lambda i, j: (i, j)),
        scratch_shapes=[pltpu.VMEM((tm, k), BF16)],
        compiler_params=_params(("parallel", "arbitrary")),
        name=name,
    )(x, g.reshape(1, k), w, bias)


def _matmul_kernel(a_ref, w_ref, o_ref):
    o_ref[...] = _dot_bf16(a_ref[...], w_ref[...].astype(BF16))


def _matmul(a, w, tm, tn, name):
    m, k = a.shape
    n = w.shape[1]
    return pl.pallas_call(
        _matmul_kernel,
        out_shape=jax.ShapeDtypeStruct((m, n), F32),
        grid=(m // tm, n // tn),
        in_specs=[pl.BlockSpec((tm, k), lambda i, j: (i, 0)), pl.BlockSpec((k, tn), lambda i, j: (0, j))],
        out_specs=pl.BlockSpec((tm, tn), lambda i, j: (i, j)),
        compiler_params=_params(("parallel", "arbitrary")),
        name=name,
    )(a, w)


def _prep_kernel(p_ref, prev_ref, mu_ref, wl_ref, w0_ref, a0_ref, kk_ref, ka_ref, rk_ref, lmat_ref, sel_ref,
                 hs_ref, hb_ref, rt_ref, kt_ref, qt_ref, pt_ref, v_ref, bonus_ref, g_ref, gc_ref, *, rows_are_time):
    p = p_ref[...]
    tm = p.shape[0]
    if rows_are_time:
        last = jnp.where(pl.program_id(0) == 0, 0.0, prev_ref[7:8, :])
        row = lax.broadcasted_iota(jnp.int32, (tm, 1), 0)
        prev = jnp.where(row == 0, last, pltpu.roll(p, 1, axis=0))
    else:
        prev = prev_ref[...]
    xm = p + mu_ref[...] * (prev - p)
    r = xm[:, 0:D_A]
    k = xm[:, D_A:2 * D_A]
    v = xm[:, 2 * D_A:3 * D_A]
    xl = xm[:, 3 * D_A:D_SHIFT]
    lane = lax.broadcasted_iota(jnp.int32, xl.shape, 1)
    act = jnp.where(lane < D_DECAY_LORA, jnp.tanh(xl),
                    jnp.where(lane < D_DECAY_LORA + D_AAA_LORA, xl, _sigmoid(xl)))
    act_hi, act_lo = _split2(act)
    w_hi, w_lo = wl_ref[0], wl_ref[1]
    lora = _dot_bf16(act_hi, w_hi) + _dot_bf16(act_hi, w_lo) + _dot_bf16(act_lo, w_hi)
    y = -(w0_ref[...] + lora[:, 0:D_A])
    softplus = jnp.maximum(y, 0.0) + jnp.log(1.0 + jnp.exp(-jnp.abs(y)))
    logw = -jnp.exp(-softplus - 0.5)
    a = _sigmoid(a0_ref[...] + lora[:, D_A:2 * D_A])
    g = lora[:, 2 * D_A:3 * D_A]
    kkr = k * kk_ref[...]
    kp = k * (1.0 + (a - 1.0) * ka_ref[...])
    seg = _head_sums(jnp.concatenate([kkr * kkr, r * kp * rk_ref[...]], axis=0), hs_ref[...], hb_ref[...])
    kk = kkr / jnp.maximum(jnp.sqrt(seg[0:tm]), 1e-12)
    cum = _dot_exact_lhs(lmat_ref[...], logw)
    e_out = jnp.exp(-cum)
    rt_ref[...] = r * jnp.exp(cum)
    kt_ref[...] = kp * e_out
    qt_ref[...] = kk * a * e_out
    pt_ref[...] = -kk * jnp.exp(cum - logw)
    v_ref[...] = v
    bonus_ref[...] = seg[tm:2 * tm] * v
    g_ref[...] = g
    gc_ref[...] = jnp.exp(_dot_exact_lhs(sel_ref[...], cum))


def _rwkv_prep(proj, prev, tm, chunk, lw, name):
    m = proj.shape[0]
    rows_are_time = prev is None
    n_tiles = m // tm
    t = np.arange(tm)
    lmat = ((t[:, None] // chunk == t[None, :] // chunk) & (t[None, :] <= t[:, None])).astype(np.float32)
    sel_stride = 1 if chunk == 1 else 8
    n_sel = sel_stride * (tm // chunk)
    sel = np.zeros((n_sel, tm), np.float32)
    for c in range(tm // chunk):
        sel[sel_stride * c, (c + 1) * chunk - 1] = 1.0
    if rows_are_time:
        prev_arr = proj
        prev_spec = pl.BlockSpec((8, D_SHIFT), lambda i: (jnp.maximum(i * (tm // 8) - 1, 0), 0))
    else:
        prev_arr = prev
        prev_spec = pl.BlockSpec((tm, D_SHIFT), lambda i: (i, 0))
    vec = lambda d: pl.BlockSpec((1, d), lambda i: (0, 0))
    full = lambda a: pl.BlockSpec(a.shape, lambda i: (0, 0))
    big = pl.BlockSpec((tm, D_A), lambda i: (i, 0))
    out = jax.ShapeDtypeStruct((m, D_A), F32)
    lmat_b = jnp.asarray(lmat, BF16)
    sel_b = jnp.asarray(sel, BF16)
    return pl.pallas_call(
        functools.partial(_prep_kernel, rows_are_time=rows_are_time),
        out_shape=[out] * 7 + [jax.ShapeDtypeStruct((n_tiles * n_sel, D_A), F32)],
        grid=(n_tiles,),
        in_specs=[pl.BlockSpec((tm, D_SHIFT), lambda i: (i, 0)), prev_spec, vec(D_SHIFT),
                  pl.BlockSpec(lw["w_lora"].shape, lambda i: (0, 0, 0)),
                  vec(D_A), vec(D_A), vec(D_A), vec(D_A), vec(D_A), full(lmat_b), full(sel_b), full(lw["hsum"]),
                  full(lw["hbc"])],
        out_specs=[big] * 7 + [pl.BlockSpec((n_sel, D_A), lambda i: (i, 0))],
        compiler_params=_params(("parallel",)),
        name=name,
    )(proj, prev_arr, lw["mu"], lw["w_lora"], lw["w0"], lw["a0"], lw["k_k"], lw["k_a"], lw["r_k"], lmat_b, sel_b,
      lw["hsum"], lw["hbc"])


def _bdot3(a, b, dims):
    ah, al = _split2(a)
    bh, bl = _split2(b)
    (ca,), (cb,) = dims[0]
    a_cat = jnp.concatenate([ah, ah, al, jnp.zeros_like(al)], axis=ca)
    b_cat = jnp.concatenate([bh, bl, bh, jnp.zeros_like(bl)], axis=cb)
    return lax.dot_general(a_cat, b_cat, dims, preferred_element_type=F32)


_NN = (((2,), (1,)), ((0,), (0,)))
_NT = (((2,), (2,)), ((0,), (0,)))


def _scan_kernel(pt_ref, rt_ref, qt_ref, kt_ref, v_ref, gc_ref, bonus_ref, g_ref, gng_ref, gnb_ref, ya_ref, sfin_ref,
                 s_scr):
    @pl.when(pl.program_id(0) == 0)
    def _():
        s_scr[...] = jnp.zeros_like(s_scr)

    def heads(x):
        return jnp.stack([x[:, h * HEAD_A:(h + 1) * HEAD_A] for h in range(H_A)], axis=0)

    p, r, q, k, v = (heads(ref[...]) for ref in (pt_ref, rt_ref, qt_ref, kt_ref, v_ref))
    gam = heads(gc_ref[0:1, :])
    ri = lax.broadcasted_iota(jnp.int32, (1, CHUNK, CHUNK), 1)
    ci = lax.broadcasted_iota(jnp.int32, (1, CHUNK, CHUNK), 2)
    strict = ri > ci
    incl = ri >= ci
    eye = (ri == ci).astype(F32)

    gram = _bdot3(jnp.concatenate([p, r], axis=1), jnp.concatenate([q, k], axis=1), _NT)
    a_qp = jnp.where(strict, gram[:, 0:CHUNK, 0:CHUNK], 0.0)
    a_kp = jnp.where(strict, gram[:, 0:CHUNK, CHUNK:], 0.0)
    a_rq = jnp.where(incl, gram[:, CHUNK:, 0:CHUNK], 0.0)
    a_rk = jnp.where(incl, gram[:, CHUNK:, CHUNK:], 0.0)
    same = lambda log2_bs: (ri >> log2_bs) == (ci >> log2_bs)
    a_d = jnp.where(same(4), a_qp, 0.0)
    tinv = eye + a_d
    pw = _bdot3(a_d, a_d, _NN)
    for _ in range(2):
        both = _bdot3(jnp.concatenate([pw, tinv], axis=1), pw, _NN)
        pw = both[:, 0:CHUNK]
        tinv = tinv + both[:, CHUNK:]
    tinv = tinv + _bdot3(tinv, pw, _NN)
    for log2_bs in (4, 5):
        off = jnp.where(jnp.logical_and(same(log2_bs + 1), jnp.logical_not(same(log2_bs))), a_qp, 0.0)
        tinv = tinv + _bdot3(tinv, _bdot3(off, tinv, _NN), _NN)
    av = _bdot3(jnp.concatenate([a_kp, a_rk], axis=1), v, _NN)
    pw_hat = _bdot3(tinv, jnp.concatenate([p, av[:, 0:CHUNK]], axis=2), _NN)
    x = _bdot3(a_rq, pw_hat, _NN)
    r_hat = r + x[:, :, 0:HEAD_A]
    o_loc = x[:, :, HEAD_A:] + av[:, CHUNK:]
    y = _bdot3(jnp.swapaxes(pw_hat, 1, 2), q, _NN)
    m_mat = (eye + y[:, 0:HEAD_A]) * gam
    n_mat = (y[:, HEAD_A:] + _bdot3(jnp.swapaxes(v, 1, 2), k, _NN)) * gam

    s0 = s_scr[...]
    o = _bdot3(r_hat, s0, _NT) + o_loc
    s_scr[...] = _bdot3(s0, m_mat, _NN) + n_mat

    mu = jnp.mean(o, axis=2, keepdims=True)
    d = o - mu
    o_n = d * lax.rsqrt(jnp.mean(d * d, axis=2, keepdims=True) + EPS_GN)
    o_n = jnp.concatenate([o_n[h] for h in range(H_A)], axis=1)
    ya_ref[...] = ((o_n * gng_ref[...] + gnb_ref[...] + bonus_ref[...]) * g_ref[...]).astype(ya_ref.dtype)

    @pl.when(pl.program_id(0) == pl.num_programs(0) - 1)
    def _():
        sfin_ref[...] = s_scr[...]


def _wkv_scan(pt, rt, qt, kt, v, gc, bonus, g, lw):
    t = pt.shape[0]
    blk = pl.BlockSpec((CHUNK, D_A), lambda c: (c, 0))
    vec = pl.BlockSpec((1, D_A), lambda c: (0, 0))
    return pl.pallas_call(
        _scan_kernel,
        out_shape=[jax.ShapeDtypeStruct((t, D_A), BF16), jax.ShapeDtypeStruct((H_A, HEAD_A, HEAD_A), F32)],
        grid=(t // CHUNK,),
        in_specs=[blk] * 5 + [pl.BlockSpec((8, D_A), lambda c: (c, 0)), blk, blk, vec, vec],
        out_specs=[blk, pl.BlockSpec((H_A, HEAD_A, HEAD_A), lambda c: (0, 0, 0))],
        scratch_shapes=[pltpu.VMEM((H_A, HEAD_A, HEAD_A), F32)],
        compiler_params=_params(("arbitrary",)),
        name="wkv_scan_prompt",
    )(pt, rt, qt, kt, v, gc, bonus, g, lw["gn_g"], lw["gn_b"])


def _wkv_step_kernel(s_ref, rows_ref, sn_ref, o_ref):
    ri = lax.broadcasted_iota(jnp.int32, (1, 1, HEAD_A, HEAD_A), 2)
    ci = lax.broadcasted_iota(jnp.int32, (1, 1, HEAD_A, HEAD_A), 3)
    eye = ri == ci
    pt, rt, qt, kt, v, gc = (rows_ref[:, n] for n in range(6))
    s = s_ref[0]
    u = jnp.sum(s * pt, axis=3, keepdims=True)
    v_col = jnp.sum(jnp.where(eye, v, 0.0), axis=3, keepdims=True)
    m = s + u * qt + v_col * kt
    sn_ref[0] = m * gc
    o_col = jnp.sum(m * rt, axis=3, keepdims=True)
    o_ref[...] = jnp.sum(jnp.where(eye, o_col, 0.0), axis=2, keepdims=True)


def _wkv_step(state, pt, rt, qt, kt, v, gc, bt):
    b = pt.shape[0]
    rows = jnp.stack([pt, rt, qt, kt, v, gc], axis=1).reshape(b, 6, H_A, 1, HEAD_A)
    st_spec = pl.BlockSpec((1, bt, H_A, HEAD_A, HEAD_A), lambda i: (0, i, 0, 0, 0))
    sn, o = pl.pallas_call(
        _wkv_step_kernel,
        out_shape=[jax.ShapeDtypeStruct(state.shape, F32), jax.ShapeDtypeStruct((b, H_A, 1, HEAD_A), F32)],
        grid=(b // bt,),
        in_specs=[st_spec, pl.BlockSpec((bt, 6, H_A, 1, HEAD_A), lambda i: (i, 0, 0, 0, 0))],
        out_specs=[st_spec, pl.BlockSpec((bt, H_A, 1, HEAD_A), lambda i: (i, 0, 0, 0))],
        compiler_params=_params(("parallel",)),
        name="wkv_step_sample",
    )(state, rows)
    return sn, o.reshape(b, D_A)


def _post_kernel(o_ref, bonus_ref, g_ref, gng_ref, gnb_ref, hs_ref, hb_ref, y_ref):
    o = o_ref[...]
    hsum, hbc = hs_ref[...], hb_ref[...]
    mu = _head_sums(o, hsum, hbc) * (1.0 / HEAD_A)
    d = o - mu
    var = _head_sums(d * d, hsum, hbc) * (1.0 / HEAD_A)
    o_n = d * lax.rsqrt(var + EPS_GN) * gng_ref[...] + gnb_ref[...]
    y_ref[...] = ((o_n + bonus_ref[...]) * g_ref[...]).astype(y_ref.dtype)


def _rwkv_post(o, bonus, g, lw, tm, name):
    m = o.shape[0]
    big = pl.BlockSpec((tm, D_A), lambda i: (i, 0))
    vec = pl.BlockSpec((1, D_A), lambda i: (0, 0))
    return pl.pallas_call(
        _post_kernel,
        out_shape=jax.ShapeDtypeStruct((m, D_A), BF16),
        grid=(m // tm,),
        in_specs=[big, big, big, vec, vec, pl.BlockSpec((D_A, LANES), lambda i: (0, 0)),
                  pl.BlockSpec((LANES, D_A), lambda i: (0, 0))],
        out_specs=big,
        compiler_params=_params(("parallel",)),
        name=name,
    )(o, bonus, g, lw["gn_g"], lw["gn_b"], lw["hsum"], lw["hbc"])


def _attn_kernel(q_ref, kc_ref, vc_ref, kh_ref, vh_ref, bias_ref, o_ref, l_ref, *, dil, m_blocks, tiled):
    scale = HEAD_B ** -0.5
    n_streams = SUBLANES if tiled else dil
    n_units = n_streams * m_blocks

    stride = SUBLANES if tiled else dil
    if tiled:
        flat = lambda ref: ref.reshape(ref.shape[0] * SUBLANES, LANES)
        q_ref, kc_ref, vc_ref, kh_ref, vh_ref, o_ref, l_ref = map(flat, (q_ref, kc_ref, vc_ref, kh_ref, vh_ref,
                                                                          o_ref, l_ref))

    def rows(ref, r, start, size):
        if stride == 1:
            return ref[pl.ds(start, size), :]
        return ref[pl.ds(r + stride * start, size, stride=stride), :]

    def put(ref, r, start, val):
        if stride == 1:
            ref[pl.ds(start, BLK), :] = val
        else:
            ref[pl.ds(r + stride * start, BLK, stride=stride), :] = val

    qs, ks, vs, units = [], [], [], []
    for r in range(n_streams):
        for mb in range(m_blocks):
            units.append((r, mb))
            qs.append(rows(q_ref, r, BLK * mb, BLK))
            if mb == 0:
                ks.append(jnp.concatenate([rows(kh_ref, r, 0, BLK), rows(kc_ref, r, 0, BLK)], axis=0))
                vs.append(jnp.concatenate([rows(vh_ref, r, 0, BLK), rows(vc_ref, r, 0, BLK)], axis=0))
            else:
                ks.append(rows(kc_ref, r, BLK * (mb - 1), 2 * BLK))
                vs.append(rows(vc_ref, r, BLK * (mb - 1), 2 * BLK))

    def batch(xs, mul=None):
        pre = (lambda x: x) if mul is None else (lambda x: x * mul)
        return jnp.stack([pre(x[:, sub * HEAD_B:(sub + 1) * HEAD_B]).astype(BF16) for sub in range(2) for x in xs],
                         axis=0)

    assert math.frexp(scale)[0] == 0.5
    qb, kb, vb = batch(qs, scale), batch(ks), batch(vs)
    s = lax.dot_general(qb, kb, (((2,), (2,)), ((0,), (0,))), preferred_element_type=F32)
    s = s.reshape(2, n_units, BLK, 2 * BLK) + bias_ref[...][:, None]
    unit = lax.broadcasted_iota(jnp.int32, (1, n_units, 1, 2 * BLK), 1)
    col = lax.broadcasted_iota(jnp.int32, (1, n_units, 1, 2 * BLK), 3)
    no_prev = jnp.logical_and(jnp.logical_and(pl.program_id(0) == 0, (unit & (m_blocks - 1)) == 0), col < BLK)
    s = jnp.where(no_prev, -jnp.inf, s)
    m = jnp.max(s, axis=-1, keepdims=True)
    p = jnp.exp(s - m)
    l = jnp.sum(p, axis=-1, keepdims=True)
    pv = lax.dot_general(p.astype(BF16).reshape(2 * n_units, BLK, 2 * BLK), vb, (((2,), (1,)), ((0,), (0,))),
                         preferred_element_type=F32)
    o = pv.reshape(2, n_units, BLK, HEAD_B) / l
    lse = jnp.broadcast_to(m + jnp.log(l), (2, n_units, BLK, HEAD_B))
    for u, (r, mb) in enumerate(units):
        put(o_ref, r, BLK * mb, jnp.concatenate([o[0, u], o[1, u]], axis=1))
        put(l_ref, r, BLK * mb, jnp.concatenate([lse[0, u], lse[1, u]], axis=1))


def _attn_prompt(proj, bias, gi, rows_per_step):
    t, n_cols = proj.shape
    dil = DILATIONS[gi]
    span = BLK * dil
    tiled = dil % SUBLANES == 0
    if tiled:
        rows_per_step *= dil // SUBLANES
    m_blocks = rows_per_step // span
    cq = (COL_QKV + gi * D_G) // LANES
    ck = (COL_QKV + D_B + gi * D_G) // LANES
    cv = (COL_QKV + 2 * D_B + gi * D_G) // LANES
    bias_block = (2, BLK, 2 * BLK)
    if tiled:
        n_hi = dil // SUBLANES
        src = proj.reshape(t // dil, n_hi, SUBLANES, n_cols)
        grid = (t // rows_per_step, n_hi, H_G // 2)
        l_rows = rows_per_step // dil
        cur = lambda c0: pl.BlockSpec((l_rows, None, SUBLANES, LANES), lambda i, rh, hp: (i, rh, 0, c0 + hp))
        halo = lambda c0: pl.BlockSpec((BLK, None, SUBLANES, LANES),
                                       lambda i, rh, hp: (jnp.maximum(i * m_blocks - 1, 0), rh, 0, c0 + hp))
        out_spec = pl.BlockSpec((l_rows, None, SUBLANES, LANES), lambda i, rh, hp: (i, rh, 0, hp))
        out = jax.ShapeDtypeStruct((t // dil, n_hi, SUBLANES, D_G), F32)
        bias_spec = pl.BlockSpec(bias_block, lambda i, rh, hp: (hp, 0, 0))
    else:
        src = proj
        grid = (t // rows_per_step, H_G // 2)
        cur = lambda c0: pl.BlockSpec((rows_per_step, LANES), lambda i, hp: (i, c0 + hp))
        halo = lambda c0: pl.BlockSpec((span, LANES), lambda i, hp: (jnp.maximum(i * m_blocks - 1, 0), c0 + hp))
        out_spec = pl.BlockSpec((rows_per_step, LANES), lambda i, hp: (i, hp))
        out = jax.ShapeDtypeStruct((t, D_G), F32)
        bias_spec = pl.BlockSpec(bias_block, lambda i, hp: (hp, 0, 0))
    o, l = pl.pallas_call(
        functools.partial(_attn_kernel, dil=dil, m_blocks=m_blocks, tiled=tiled),
        out_shape=[out, out],
        grid=grid,
        in_specs=[cur(cq), cur(ck), cur(cv), halo(ck), halo(cv), bias_spec],
        out_specs=[out_spec, out_spec],
        compiler_params=_params(("parallel",) * len(grid)),
        name=f"attn_prompt_g{gi}",
    )(src, src, src, src, src, bias)
    return o.reshape(t, D_G), l.reshape(t, D_G)


def _attn_step_kernel(q_ref, kn_ref, vn_ref, c_ref, bias_ref, bias0_ref, o_ref, l_ref):
    scale = HEAD_B ** -0.5
    ri = lax.broadcasted_iota(jnp.int32, (1, 1, HEAD_B, HEAD_B), 2)
    ci = lax.broadcasted_iota(jnp.int32, (1, 1, HEAD_B, HEAD_B), 3)
    eye = ri == ci
    q = q_ref[...]
    q_col = jnp.sum(jnp.where(eye, q, 0.0), axis=3, keepdims=True)
    s = jnp.sum(c_ref[:, 0] * q_col, axis=2, keepdims=True) * scale + bias_ref[...]
    s_new = jnp.sum(q * kn_ref[...], axis=3, keepdims=True) * scale + bias0_ref[...]
    m = jnp.maximum(jnp.max(s, axis=3, keepdims=True), s_new)
    p = jnp.exp(s - m)
    p_new = jnp.exp(s_new - m)
    l = jnp.sum(p, axis=3, keepdims=True) + p_new
    o_col = jnp.sum(c_ref[:, 1] * p, axis=3, keepdims=True)
    o_row = jnp.sum(jnp.where(eye, o_col, 0.0), axis=2, keepdims=True)
    o_ref[...] = (o_row + p_new * vn_ref[...]) / l
    l_ref[...] = jnp.broadcast_to(m + jnp.log(l), o_ref.shape)


def _attn_step(q, k_new, v_new, cache, bias, bias0, gi):
    b = q.shape[0]
    w = cache.shape[2]
    bt = max(1, min(b, STEP_CACHE_WORDS // (2 * D_G * w)))
    cache_t = jnp.transpose(cache, (0, 1, 3, 4, 5, 2))
    vec = pl.BlockSpec((bt, H_G, 1, HEAD_B), lambda i: (i, 0, 0, 0))
    out = jax.ShapeDtypeStruct((b, H_G, 1, HEAD_B), F32)
    o, l = pl.pallas_call(
        _attn_step_kernel,
        out_shape=[out, out],
        grid=(b // bt,),
        in_specs=[vec, vec, vec,
                  pl.BlockSpec((None, bt, 2, H_G, HEAD_B, w), lambda i: (0, i, 0, 0, 0, 0)),
                  pl.BlockSpec((H_G, 1, w), lambda i: (0, 0, 0)),
                  pl.BlockSpec((H_G, 1, 1), lambda i: (0, 0, 0))],
        out_specs=[vec, vec],
        compiler_params=_params(("parallel",)),
        name=f"attn_step_g{gi}",
    )(q, k_new, v_new, cache_t, bias, bias0)
    return o.reshape(b, D_G), l.reshape(b, D_G)


def _mix_kernel(ya_ref, o1_ref, o2_ref, o3_ref, l1_ref, l2_ref, l3_ref, wa_ref, wb_ref, ga_ref, gb_ref, o_ref):
    l1, l2, l3 = l1_ref[...], l2_ref[...], l3_ref[...]
    m = jnp.maximum(jnp.maximum(l1, l2), l3)
    e1, e2, e3 = jnp.exp(l1 - m), jnp.exp(l2 - m), jnp.exp(l3 - m)
    den = e1 + e2 + e3
    yb = ((e1 / den) * o1_ref[...] + (e2 / den) * o2_ref[...] + (e3 / den) * o3_ref[...]).astype(BF16)
    mixed = ga_ref[...] * _dot_bf16(ya_ref[...], wa_ref[...]) + gb_ref[...] * _dot_bf16(yb, wb_ref[...])
    o_ref[...] = mixed.astype(o_ref.dtype)


def _mix(ya, outs, lses, wa, wb, proj, tm, name):
    m = ya.shape[0]
    n = D_MODEL
    assert COL_GATE % n == 0
    ga0 = COL_GATE // n
    grp = pl.BlockSpec((tm, D_G), lambda i: (i, 0))
    resident = lambda rows: pl.BlockSpec((rows, n), lambda i: (0, 0), pipeline_mode=pl.Buffered(1))
    return pl.pallas_call(
        _mix_kernel,
        out_shape=jax.ShapeDtypeStruct((m, n), BF16),
        grid=(m // tm,),
        in_specs=[pl.BlockSpec((tm, D_A), lambda i: (i, 0))] + [grp] * 6 + [
            resident(D_A), resident(D_G),
            pl.BlockSpec((tm, n), lambda i: (i, ga0)),
            pl.BlockSpec((tm, n), lambda i: (i, ga0 + 1)),
        ],
        out_specs=pl.BlockSpec((tm, n), lambda i: (i, 0)),
        compiler_params=_params(("parallel",)),
        name=name,
    )(ya, *outs, *lses, wa, wb, proj, proj)


def _wo_kernel(a_ref, w_ref, x_ref, g_ref, h_ref, hn_ref):
    h = x_ref[...] + _dot_bf16(a_ref[...], w_ref[...])
    h_ref[...] = h
    hn_ref[...] = _rms(h, g_ref[...]).astype(BF16)


def _wo(a, w, x, g, tm, name):
    m, k = a.shape
    n = w.shape[1]
    row = pl.BlockSpec((tm, n), lambda i: (i, 0))
    return pl.pallas_call(
        _wo_kernel,
        out_shape=[jax.ShapeDtypeStruct((m, n), F32), jax.ShapeDtypeStruct((m, n), BF16)],
        grid=(m // tm,),
        in_specs=[pl.BlockSpec((tm, k), lambda i: (i, 0)), pl.BlockSpec((k, n), lambda i: (0, 0)), row,
                  pl.BlockSpec((1, n), lambda i: (0, 0))],
        out_specs=[row, row],
        compiler_params=_params(("parallel",)),
        name=name,
    )(a, w, x, g.reshape(1, n))


def _gelu(x):
    return 0.5 * x * (1.0 + lax.erf(x * (1.0 / math.sqrt(2.0))))


def _ffn_up_kernel(h_ref, w1_ref, w2_ref, cw1_ref, cw2_ref, cb1_ref, cb2_ref, act_ref, t1_ref, t2_ref,
                   c1_scr, c2_scr, w1_scr, w2_scr):
    tm = h_ref.shape[0]

    @pl.when(pl.program_id(1) == 0)
    def _():
        c1_scr[...] = jnp.zeros_like(c1_scr)
        c2_scr[...] = jnp.zeros_like(c2_scr)
        w1_scr[...] = w1_ref[...].astype(BF16)
        w2_scr[...] = w2_ref[...].astype(BF16)

    a = h_ref[...]
    row = lax.broadcasted_iota(jnp.int32, (tm, 1), 0)

    def conv(u, carry_ref, cw_ref, cb_ref):
        m1 = jnp.where(row == 0, carry_ref[7:8, :], pltpu.roll(u, 1, axis=0))
        m2 = jnp.where(row == 0, carry_ref[6:7, :], jnp.where(row == 1, carry_ref[7:8, :], pltpu.roll(u, 2, axis=0)))
        return cb_ref[...] + cw_ref[0:1, :] * m2 + cw_ref[1:2, :] * m1 + cw_ref[2:3, :] * u

    u1 = _dot_bf16(a, w1_scr[...])
    u2 = _dot_bf16(a, w2_scr[...])
    c1 = conv(u1, c1_scr, cw1_ref, cb1_ref)
    c2 = conv(u2, c2_scr, cw2_ref, cb2_ref)
    act_ref[...] = (_gelu(c1) * c2).astype(act_ref.dtype)
    c1_scr[...] = u1[tm - 8:tm]
    c2_scr[...] = u2[tm - 8:tm]
    t1_ref[...] = u1[tm - 8:tm]
    t2_ref[...] = u2[tm - 8:tm]


def _ffn_up_prompt(hn, w_up, conv_w, conv_b, tm, tn):
    m = hn.shape[0]
    nj = D_FF // tn
    tail = jax.ShapeDtypeStruct((8, D_FF), F32)
    return pl.pallas_call(
        _ffn_up_kernel,
        out_shape=[jax.ShapeDtypeStruct((m, D_FF), BF16), tail, tail],
        grid=(nj, m // tm),
        in_specs=[
            pl.BlockSpec((tm, D_MODEL), lambda j, i: (i, 0)),
            pl.BlockSpec((D_MODEL, tn), lambda j, i: (0, j)),
            pl.BlockSpec((D_MODEL, tn), lambda j, i: (0, nj + j)),
            pl.BlockSpec((CONV_W, tn), lambda j, i: (0, j)),
            pl.BlockSpec((CONV_W, tn), lambda j, i: (0, nj + j)),
            pl.BlockSpec((1, tn), lambda j, i: (0, j)),
            pl.BlockSpec((1, tn), lambda j, i: (0, nj + j)),
        ],
        out_specs=[pl.BlockSpec((tm, tn), lambda j, i: (i, j)),
                   pl.BlockSpec((8, tn), lambda j, i: (0, j)),
                   pl.BlockSpec((8, tn), lambda j, i: (0, j))],
        scratch_shapes=[pltpu.VMEM((8, tn), F32), pltpu.VMEM((8, tn), F32),
                        pltpu.VMEM((D_MODEL, tn), BF16), pltpu.VMEM((D_MODEL, tn), BF16)],
        compiler_params=_params(("arbitrary", "arbitrary")),
        name="ffn_up_prompt",
    )(hn, w_up, w_up, conv_w, conv_w, conv_b, conv_b)


def _ffn_act_step_kernel(up_ref, prev_ref, cw_ref, cb_ref, act_ref):
    up = up_ref[...]
    w = 2 * D_FF
    c = cb_ref[...] + cw_ref[0:1, :] * prev_ref[:, 0:w] + cw_ref[1:2, :] * prev_ref[:, w:2 * w] + cw_ref[2:3, :] * up
    act_ref[...] = (_gelu(c[:, 0:D_FF]) * c[:, D_FF:w]).astype(act_ref.dtype)


def _ffn_act_step(up, conv_prev, conv_w, conv_b):
    b = up.shape[0]
    return pl.pallas_call(
        _ffn_act_step_kernel,
        out_shape=jax.ShapeDtypeStruct((b, D_FF), BF16),
        compiler_params=pltpu.CompilerParams(vmem_limit_bytes=VMEM_LIMIT),
        name="ffn_act_sample",
    )(up, conv_prev.reshape(b, (CONV_W - 1) * 2 * D_FF), conv_w, conv_b)


def _down_kernel(a_ref, w_ref, h_ref, g_ref, o_ref, acc_ref):
    @pl.when(pl.program_id(1) == 0)
    def _():
        acc_ref[...] = h_ref[...]

    out = acc_ref[...] + _dot_bf16(a_ref[...], w_ref[...])
    acc_ref[...] = out
    inv = lax.rsqrt(jnp.mean(out * out, axis=-1, keepdims=True) + EPS_RMS)
    o_ref[...] = out * inv * g_ref[...]


def _ffn_down(act, w_down, h, g, tm, tk, name):
    m = act.shape[0]
    nk = D_FF // tk
    return pl.pallas_call(
        _down_kernel,
        out_shape=jax.ShapeDtypeStruct((m, D_MODEL), F32),
        grid=(m // tm, nk),
        in_specs=[
            pl.BlockSpec((tm, tk), lambda i, k: (i, k)),
            pl.BlockSpec((tk, D_MODEL), lambda i, k: (k, 0)),
            pl.BlockSpec((tm, D_MODEL), lambda i, k: (i, 0)),
            pl.BlockSpec((1, D_MODEL), lambda i, k: (0, 0)),
        ],
        out_specs=pl.BlockSpec((tm, D_MODEL), lambda i, k: (i, 0)),
        scratch_shapes=[pltpu.VMEM((tm, D_MODEL), F32)],
        compiler_params=_params(("parallel", "arbitrary")),
        name=name,
    )(act, w_down, h, g.reshape(1, D_MODEL))


def _rel_bucket(dist):
    max_exact = N_BUCKETS // 2
    d_f = jnp.maximum(dist, 1).astype(F32)
    large = max_exact + (jnp.log(d_f / max_exact) / math.log(MAX_DISTANCE / max_exact)
                         * (N_BUCKETS - max_exact)).astype(jnp.int32)
    large = jnp.minimum(large, N_BUCKETS - 1)
    return jnp.where(dist < max_exact, dist, large)


def _bias_rows(tab, dist):
    onehot = (_rel_bucket(dist)[None, :] == jnp.arange(N_BUCKETS)[:, None]).astype(F32)
    return jnp.dot(tab.T, onehot, precision=HIGHEST)


def _bias_tables(rel_bias, gi):
    dil = DILATIONS[gi]
    win = WINDOWS[gi]
    reach = win // dil
    tab = rel_bias[:, gi * H_G:(gi + 1) * H_G]
    qi = jnp.arange(BLK)[:, None]
    ki = jnp.arange(2 * BLK)[None, :]
    rel = qi + BLK - ki
    blk = _bias_rows(tab, (dil * jnp.maximum(rel, 0)).reshape(-1)).reshape(H_G, BLK, 2 * BLK)
    blk = jnp.where(((rel >= 0) & (rel <= reach))[None], blk, -jnp.inf)
    back = win - jnp.arange(win)
    step = jnp.where((back % dil == 0)[None, :], _bias_rows(tab, back), -jnp.inf)[:, None, :]
    step0 = _bias_rows(tab, jnp.zeros((1,), jnp.int32))[:, None, :]
    return blk, step, step0


def _tiles(m, prompt):
    if prompt:
        return dict(tag="prompt", proj_m=1024, prep_m=2 * CHUNK, attn_rows=2048, mix_m=512, wo_m=512, up_m=1024,
                    up_n=512, down_m=512, down_k=D_FF // 2)
    return dict(tag="sample", proj_m=m, prep_m=m, step_b=4, post_m=m, mix_m=m, wo_m=m, up_n=512, down_m=m,
                down_k=D_FF // 2)


def _layer(x, lw, bias_blk, prompt, state=None, proj=None):
    m = x.shape[0]
    tiles = _tiles(m, prompt)
    if proj is None:
        proj = _proj(x, lw["norm1_g"], lw["w_in"], lw["in_bias"], tiles["proj_m"], "proj_" + tiles["tag"])

    if prompt:
        rt, kt, qt, pt, v, bonus, g, gc = _rwkv_prep(proj, None, tiles["prep_m"], CHUNK, lw, "rwkv_prep_prompt")
        ya, wkv_new = _wkv_scan(pt, rt, qt, kt, v, gc, bonus, g, lw)
        wkv_new = wkv_new[None, None]
    else:
        rt, kt, qt, pt, v, bonus, g, gc = _rwkv_prep(proj, state["shift"], tiles["prep_m"], 1, lw, "rwkv_prep_sample")
        wkv_new, o = _wkv_step(state["wkv"], pt, rt, qt, kt, v, gc, tiles["step_b"])
        ya = _rwkv_post(o, bonus, g, lw, tiles["post_m"], "rwkv_post_" + tiles["tag"])

    outs, lses = [], []
    for gi in range(N_GROUPS):
        if prompt:
            o_g, l_g = _attn_prompt(proj, bias_blk[gi][0], gi, tiles["attn_rows"])
        else:
            sl = lambda c0: proj[:, c0 + gi * D_G:c0 + (gi + 1) * D_G].reshape(m, H_G, 1, HEAD_B)
            o_g, l_g = _attn_step(sl(COL_QKV), sl(COL_QKV + D_B), sl(COL_QKV + 2 * D_B), state["win"][gi],
                                  bias_blk[gi][1], bias_blk[gi][2], gi)
        outs.append(o_g)
        lses.append(l_g)
    mixed = _mix(ya, outs, lses, lw["w_out_a"], lw["w_out_b"], proj, tiles["mix_m"], "mix_" + tiles["tag"])
    h, hn = _wo(mixed, lw["w_o"], x, lw["norm2_g"], tiles["wo_m"], "wo_" + tiles["tag"])

    if prompt:
        act, t1, t2 = _ffn_up_prompt(hn, lw["w_up"], lw["conv_w"], lw["conv_b"], tiles["up_m"], tiles["up_n"])
        conv_new = jnp.concatenate([t1[6:8], t2[6:8]], axis=1)[None, None]
    else:
        up = _matmul(hn, lw["w_up"], m, tiles["up_n"], "up_sample")
        act = _ffn_act_step(up, state["conv"], lw["conv_w"], lw["conv_b"])
        conv_new = jnp.concatenate([state["conv"][:, 1:], up[:, None, :]], axis=1)[None]
    y = _ffn_down(act, lw["w_down"], h, lw["normf_g"], tiles["down_m"], tiles["down_k"], "down_" + tiles["tag"])
    return y, proj, wkv_new, conv_new


def kernel(x_prompt, x_sample, state_wkv, state_shift, state_ffn_conv, cache_win1, cache_win2, cache_win3, rel_bias,
           norm1_g, w_in, gate_b, mu_shift, w0, w_up_decay, a0, w_up_aaa, w_up_gate, k_k, k_a, r_k, gn_g, gn_b,
           w_out_a, w_out_b, w_o, norm2_g, w_up, conv_w, conv_b, w_down, normf_g):
    row = lambda a: a.reshape(1, -1)
    w_lora = jnp.zeros((D_LORA, 3 * D_A), F32)
    w_lora = w_lora.at[0:D_DECAY_LORA, 0:D_A].set(w_up_decay[0])
    w_lora = w_lora.at[D_DECAY_LORA:D_DECAY_LORA + D_AAA_LORA, D_A:2 * D_A].set(w_up_aaa[0])
    w_lora = w_lora.at[D_DECAY_LORA + D_AAA_LORA:, 2 * D_A:].set(w_up_gate[0])
    head = np.arange(D_A) // HEAD_A
    lw = dict(
        norm1_g=norm1_g[0], norm2_g=norm2_g[0], normf_g=normf_g,
        in_bias=jnp.concatenate([jnp.zeros((1, COL_GATE), F32), row(gate_b[0])], axis=1),
        mu=row(mu_shift[0]), w_lora=jnp.stack(_split2(w_lora)), w0=row(w0[0]), a0=row(a0[0]), k_k=row(k_k[0]), k_a=row(k_a[0]),
        r_k=row(r_k[0]), gn_g=row(gn_g[0]), gn_b=row(gn_b[0]),
        hsum=jnp.asarray(head[:, None] == np.arange(LANES)[None, :], BF16),
        hbc=jnp.asarray(np.arange(LANES)[:, None] == head[None, :], BF16),
        w_out_a=w_out_a[0].astype(BF16), w_out_b=w_out_b[0].astype(BF16), w_o=w_o[0].astype(BF16),
        w_up=w_up[0], conv_w=conv_w[0], conv_b=row(conv_b[0]), w_down=w_down[0].astype(BF16),
    )
    bias = [_bias_tables(rel_bias, gi) for gi in range(N_GROUPS)]

    proj_s, lw["w_in"] = _proj_cast(x_sample[:, 0], norm1_g[0], w_in[0], lw["in_bias"])
    y_p, proj_p, wkv_p, conv_p = _layer(x_prompt[0], lw, bias, prompt=True)

    b = DEC_BATCH
    state = dict(wkv=state_wkv, shift=state_shift.reshape(b, D_SHIFT), conv=state_ffn_conv[0],
                 win=(cache_win1, cache_win2, cache_win3))
    y_s, proj_s, wkv_s, conv_s = _layer(x_sample[:, 0], lw, bias, prompt=False, state=state, proj=proj_s)

    def kv_rows(proj, lo, gi):
        k = proj[lo:, COL_QKV + D_B + gi * D_G:COL_QKV + D_B + (gi + 1) * D_G]
        v = proj[lo:, COL_QKV + 2 * D_B + gi * D_G:COL_QKV + 2 * D_B + (gi + 1) * D_G]
        n = k.shape[0]
        return jnp.stack([k.reshape(n, H_G, HEAD_B), v.reshape(n, H_G, HEAD_B)], axis=1)

    win_p = [kv_rows(proj_p, SEQ - min(WINDOWS[gi], SEQ), gi)[None, None] for gi in range(N_GROUPS)]
    win_s = [kv_rows(proj_s, 0, gi)[None, :, None] for gi in range(N_GROUPS)]
    return (y_p[None], y_s[:, None],
            wkv_p, wkv_s,
            proj_p[SEQ - 1:, 0:D_SHIFT][None, None], proj_s[:, 0:D_SHIFT][None, :, None],
            conv_p, conv_s,
            win_p[0], win_s[0], win_p[1], win_s[1], win_p[2], win_s[2])
```

```python
import functools
import math

import numpy as np
import jax
import jax.numpy as jnp
from jax import lax
from jax.experimental import pallas as pl
from jax.experimental.pallas import tpu as pltpu

F32 = jnp.float32
BF16 = jnp.bfloat16
HIGHEST = lax.Precision.HIGHEST

D_MODEL = 2048
SEQ = 8192
DEC_BATCH = 32
HEAD_A = 64
H_A = 16
D_A = H_A * HEAD_A
D_DECAY_LORA = 96
D_AAA_LORA = 96
D_GATE_LORA = 64
D_LORA = D_DECAY_LORA + D_AAA_LORA + D_GATE_LORA
D_SHIFT = 3 * D_A + D_LORA
EPS_GN = 64e-5
HEAD_B = 64
H_G = 8
D_G = H_G * HEAD_B
WINDOWS = (128, 512, 2048)
DILATIONS = (1, 4, 16)
N_GROUPS = 3
D_B = N_GROUPS * D_G
BLK = 128
N_BUCKETS = 32
MAX_DISTANCE = 2048
COL_QKV = D_SHIFT
COL_GATE_SRC = D_SHIFT + 3 * D_B
D_IN = COL_GATE_SRC + 2 * D_MODEL
PROJ_TN = 1024
COL_GATE = -(-COL_GATE_SRC // PROJ_TN) * PROJ_TN
D_IN_PAD = COL_GATE + 2 * D_MODEL
D_FF = 5632
CONV_W = 3
EPS_RMS = 1e-6
CHUNK = 64
F_RT, F_KT, F_QT, F_PT, F_V, F_BONUS, F_G = range(7)
N_FEATS = 7
LANES = 128
SUBLANES = 8
VMEM_LIMIT = 56 * 1024 * 1024
STEP_CACHE_WORDS = 2 * 1024 * 1024


def _params(sem, vmem=VMEM_LIMIT):
    return pltpu.CompilerParams(dimension_semantics=sem, vmem_limit_bytes=vmem)


def _sigmoid(x):
    return 1.0 / (1.0 + jnp.exp(-x))


def _dot_bf16(a, b):
    return jnp.dot(a, b, preferred_element_type=F32)


def _split2(x):
    hi = x.astype(BF16)
    return hi, (x - hi.astype(F32)).astype(BF16)


def _split3(x):
    hi = x.astype(BF16)
    r1 = x - hi.astype(F32)
    mid = r1.astype(BF16)
    lo = (r1 - mid.astype(F32)).astype(BF16)
    return hi, mid, lo


def _dot_exact_rhs(x, m):
    hi, mid, lo = _split3(x)
    return _dot_bf16(hi, m) + _dot_bf16(mid, m) + _dot_bf16(lo, m)


def _head_sums(x, hsum, hbc):
    return _dot_exact_rhs(_dot_exact_rhs(x, hsum), hbc)


def _dot_exact_lhs(m, x):
    hi, mid, lo = _split3(x)
    return _dot_bf16(m, hi) + _dot_bf16(m, mid) + _dot_bf16(m, lo)


def _rms(x, g):
    return x * lax.rsqrt(jnp.mean(x * x, axis=-1, keepdims=True) + EPS_RMS) * g


def _proj_cast_kernel(x_ref, g_ref, w_ref, b_ref, o_ref, wb_ref, xn_scr, *, gap_block):
    j = pl.program_id(0)

    @pl.when(j == 0)
    def _():
        xn_scr[...] = _rms(x_ref[...], g_ref[...]).astype(BF16)

    wb = jnp.where(j != gap_block, w_ref[...], 0.0).astype(BF16)
    wb_ref[...] = wb
    acc = _dot_bf16(xn_scr[...], wb)
    tn = o_ref.shape[1]
    col = lax.broadcasted_iota(jnp.int32, (1, tn), 1) + j * tn
    o_ref[...] = jnp.where(col >= COL_GATE, _sigmoid(acc + b_ref[...]), acc)


def _proj_cast(x, g, w, bias):
    m, k = x.shape
    tn = COL_GATE - COL_GATE_SRC
    assert tn % LANES == 0 and COL_GATE_SRC % tn == 0
    gap_block = COL_GATE_SRC // tn
    return pl.pallas_call(
        functools.partial(_proj_cast_kernel, gap_block=gap_block),
        out_shape=[jax.ShapeDtypeStruct((m, D_IN_PAD), F32), jax.ShapeDtypeStruct((k, D_IN_PAD), BF16)],
        grid=(D_IN_PAD // tn,),
        in_specs=[
            pl.BlockSpec((m, k), lambda j: (0, 0)),
            pl.BlockSpec((1, k), lambda j: (0, 0)),
            pl.BlockSpec((k, tn), lambda j: (0, jnp.where(j < gap_block, j, jnp.maximum(j - 1, 0)))),
            pl.BlockSpec((1, tn), lambda j: (0, j)),
        ],
        out_specs=[pl.BlockSpec((m, tn), lambda j: (0, j)), pl.BlockSpec((k, tn), lambda j: (0, j))],
        scratch_shapes=[pltpu.VMEM((m, k), BF16)],
        compiler_params=_params(("arbitrary",)),
        name="proj_sample_cast_w_in",
    )(x, g.reshape(1, k), w, bias)


def _proj_kernel(x_ref, g_ref, w_ref, b_ref, o_ref, xn_scr):
    @pl.when(pl.program_id(1) == 0)
    def _():
        xn_scr[...] = _rms(x_ref[...], g_ref[...]).astype(BF16)

    acc = _dot_bf16(xn_scr[...], w_ref[...])
    tn = o_ref.shape[1]
    col = lax.broadcasted_iota(jnp.int32, (1, tn), 1) + pl.program_id(1) * tn
    o_ref[...] = jnp.where(col >= COL_GATE, _sigmoid(acc + b_ref[...]), acc)


def _proj(x, g, w, bias, tm, name):
    m, k = x.shape
    n = w.shape[1]
    return pl.pallas_call(
        _proj_kernel,
        out_shape=jax.ShapeDtypeStruct((m, n), F32),
        grid=(m // tm, n // PROJ_TN),
        in_specs=[
            pl.BlockSpec((tm, k), lambda i, j: (i, 0)),
            pl.BlockSpec((1, k), lambda i, j: (0, 0)),
            pl.BlockSpec((k, PROJ_TN), lambda i, j: (0, j)),
            pl.BlockSpec((1, PROJ_TN), lambda i, j: (0, j)),
        ],
        out_specs=pl.BlockSpec((tm, PROJ_TN), lambda i, j: (i, j)),
        scratch_shapes=[pltpu.VMEM((tm, k), BF16)],
        compiler_params=_params(("parallel", "arbitrary")),
        name=name,
    )(x, g.reshape(1, k), w, bias)


def _matmul_kernel(a_ref, w_ref, o_ref):
    o_ref[...] = _dot_bf16(a_ref[...], w_ref[...].astype(BF16))


def _matmul(a, w, tm, tn, name):
    m, k = a.shape
    n = w.shape[1]
    return pl.pallas_call(
        _matmul_kernel,
        out_shape=jax.ShapeDtypeStruct((m, n), F32),
        grid=(m // tm, n // tn),
        in_specs=[pl.BlockSpec((tm, k), lambda i, j: (i, 0)), pl.BlockSpec((k, tn), lambda i, j: (0, j))],
        out_specs=pl.BlockSpec((tm, tn), lambda i, j: (i, j)),
        compiler_params=_params(("parallel", "arbitrary")),
        name=name,
    )(a, w)


def _prep_kernel(p_ref, prev_ref, mu_ref, wl_ref, w0_ref, a0_ref, kk_ref, ka_ref, rk_ref, lmat_ref, sel_ref,
                 hs_ref, hb_ref, f_ref, gc_ref, *, rows_are_time):
    p = p_ref[...]
    tm = p.shape[0]
    if rows_are_time:
        last = jnp.where(pl.program_id(0) == 0, 0.0, prev_ref[7:8, :])
        row = lax.broadcasted_iota(jnp.int32, (tm, 1), 0)
        prev = jnp.where(row == 0, last, pltpu.roll(p, 1, axis=0))
    else:
        prev = prev_ref[...]
    xm = p + mu_ref[...] * (prev - p)
    r = xm[:, 0:D_A]
    k = xm[:, D_A:2 * D_A]
    v = xm[:, 2 * D_A:3 * D_A]
    xl = xm[:, 3 * D_A:D_SHIFT]
    lane = lax.broadcasted_iota(jnp.int32, xl.shape, 1)
    act = jnp.where(lane < D_DECAY_LORA, jnp.tanh(xl),
                    jnp.where(lane < D_DECAY_LORA + D_AAA_LORA, xl, _sigmoid(xl)))
    act_hi, act_lo = _split2(act)
    w_hi, w_lo = wl_ref[0], wl_ref[1]
    lora = _dot_bf16(act_hi, w_hi) + _dot_bf16(act_hi, w_lo) + _dot_bf16(act_lo, w_hi)
    y = -(w0_ref[...] + lora[:, 0:D_A])
    softplus = jnp.maximum(y, 0.0) + jnp.log(1.0 + jnp.exp(-jnp.abs(y)))
    logw = -jnp.exp(-softplus - 0.5)
    a = _sigmoid(a0_ref[...] + lora[:, D_A:2 * D_A])
    g = lora[:, 2 * D_A:3 * D_A]
    kkr = k * kk_ref[...]
    kp = k * (1.0 + (a - 1.0) * ka_ref[...])
    seg = _head_sums(jnp.concatenate([kkr * kkr, r * kp * rk_ref[...]], axis=0), hs_ref[...], hb_ref[...])
    kk = kkr / jnp.maximum(jnp.sqrt(seg[0:tm]), 1e-12)
    cum = _dot_exact_lhs(lmat_ref[...], logw)
    e_out = jnp.exp(-cum)
    feats = {F_RT: r * jnp.exp(cum), F_KT: kp * e_out, F_QT: kk * a * e_out, F_PT: -kk * jnp.exp(cum - logw),
             F_V: v, F_BONUS: seg[tm:2 * tm] * v, F_G: g}
    for n, val in feats.items():
        f_ref[:, n * D_A:(n + 1) * D_A] = val
    gc_ref[...] = jnp.exp(_dot_exact_lhs(sel_ref[...], cum))


def _rwkv_prep(proj, prev, tm, chunk, lw, name):
    m = proj.shape[0]
    rows_are_time = prev is None
    n_tiles = m // tm
    t = np.arange(tm)
    lmat = ((t[:, None] // chunk == t[None, :] // chunk) & (t[None, :] <= t[:, None])).astype(np.float32)
    sel_stride = 1 if chunk == 1 else 8
    n_sel = sel_stride * (tm // chunk)
    sel = np.zeros((n_sel, tm), np.float32)
    for c in range(tm // chunk):
        sel[sel_stride * c, (c + 1) * chunk - 1] = 1.0
    if rows_are_time:
        prev_arr = proj
        prev_spec = pl.BlockSpec((8, D_SHIFT), lambda i: (jnp.maximum(i * (tm // 8) - 1, 0), 0))
    else:
        prev_arr = prev
        prev_spec = pl.BlockSpec((tm, D_SHIFT), lambda i: (i, 0))
    vec = lambda d: pl.BlockSpec((1, d), lambda i: (0, 0))
    full = lambda a: pl.BlockSpec(a.shape, lambda i: (0, 0))
    lmat_b = jnp.asarray(lmat, BF16)
    sel_b = jnp.asarray(sel, BF16)
    return pl.pallas_call(
        functools.partial(_prep_kernel, rows_are_time=rows_are_time),
        out_shape=[jax.ShapeDtypeStruct((m, N_FEATS * D_A), F32),
                   jax.ShapeDtypeStruct((n_tiles * n_sel, D_A), F32)],
        grid=(n_tiles,),
        in_specs=[pl.BlockSpec((tm, D_SHIFT), lambda i: (i, 0)), prev_spec, vec(D_SHIFT),
                  pl.BlockSpec(lw["w_lora"].shape, lambda i: (0, 0, 0)),
                  vec(D_A), vec(D_A), vec(D_A), vec(D_A), vec(D_A), full(lmat_b), full(sel_b), full(lw["hsum"]),
                  full(lw["hbc"])],
        out_specs=[pl.BlockSpec((tm, N_FEATS * D_A), lambda i: (i, 0)), pl.BlockSpec((n_sel, D_A), lambda i: (i, 0))],
        compiler_params=_params(("parallel",)),
        name=name,
    )(proj, prev_arr, lw["mu"], lw["w_lora"], lw["w0"], lw["a0"], lw["k_k"], lw["k_a"], lw["r_k"], lmat_b, sel_b,
      lw["hsum"], lw["hbc"])


def _bdot3(a, b, dims):
    ah, al = _split2(a)
    bh, bl = _split2(b)
    (ca,), (cb,) = dims[0]
    a_cat = jnp.concatenate([ah, ah, al, jnp.zeros_like(al)], axis=ca)
    b_cat = jnp.concatenate([bh, bl, bh, jnp.zeros_like(bl)], axis=cb)
    return lax.dot_general(a_cat, b_cat, dims, preferred_element_type=F32)


_NN = (((2,), (1,)), ((0,), (0,)))
_NT = (((2,), (2,)), ((0,), (0,)))


def _scan_kernel(f_ref, gc_ref, gng_ref, gnb_ref, ya_ref, sfin_ref, s_scr):
    @pl.when(pl.program_id(0) == 0)
    def _():
        s_scr[...] = jnp.zeros_like(s_scr)

    def heads(x):
        return jnp.stack([x[:, h * HEAD_A:(h + 1) * HEAD_A] for h in range(H_A)], axis=0)

    feat = lambda n: f_ref[:, n * D_A:(n + 1) * D_A]
    p, r, q, k, v = (heads(feat(n)) for n in (F_PT, F_RT, F_QT, F_KT, F_V))
    gam = heads(gc_ref[0:1, :])
    ri = lax.broadcasted_iota(jnp.int32, (1, CHUNK, CHUNK), 1)
    ci = lax.broadcasted_iota(jnp.int32, (1, CHUNK, CHUNK), 2)
    strict = ri > ci
    incl = ri >= ci
    eye = (ri == ci).astype(F32)

    gram = _bdot3(jnp.concatenate([p, r], axis=1), jnp.concatenate([q, k], axis=1), _NT)
    a_qp = jnp.where(strict, gram[:, 0:CHUNK, 0:CHUNK], 0.0)
    a_kp = jnp.where(strict, gram[:, 0:CHUNK, CHUNK:], 0.0)
    a_rq = jnp.where(incl, gram[:, CHUNK:, 0:CHUNK], 0.0)
    a_rk = jnp.where(incl, gram[:, CHUNK:, CHUNK:], 0.0)
    same = lambda log2_bs: (ri >> log2_bs) == (ci >> log2_bs)
    a_d = jnp.where(same(4), a_qp, 0.0)
    tinv = eye + a_d
    pw = _bdot3(a_d, a_d, _NN)
    for _ in range(2):
        both = _bdot3(jnp.concatenate([pw, tinv], axis=1), pw, _NN)
        pw = both[:, 0:CHUNK]
        tinv = tinv + both[:, CHUNK:]
    tinv = tinv + _bdot3(tinv, pw, _NN)
    for log2_bs in (4, 5):
        off = jnp.where(jnp.logical_and(same(log2_bs + 1), jnp.logical_not(same(log2_bs))), a_qp, 0.0)
        tinv = tinv + _bdot3(tinv, _bdot3(off, tinv, _NN), _NN)
    av = _bdot3(jnp.concatenate([a_kp, a_rk], axis=1), v, _NN)
    pw_hat = _bdot3(tinv, jnp.concatenate([p, av[:, 0:CHUNK]], axis=2), _NN)
    x = _bdot3(a_rq, pw_hat, _NN)
    r_hat = r + x[:, :, 0:HEAD_A]
    o_loc = x[:, :, HEAD_A:] + av[:, CHUNK:]
    y = _bdot3(jnp.swapaxes(pw_hat, 1, 2), q, _NN)
    m_mat = (eye + y[:, 0:HEAD_A]) * gam
    n_mat = (y[:, HEAD_A:] + _bdot3(jnp.swapaxes(v, 1, 2), k, _NN)) * gam

    s0 = s_scr[...]
    o = _bdot3(r_hat, s0, _NT) + o_loc
    s_scr[...] = _bdot3(s0, m_mat, _NN) + n_mat

    mu = jnp.mean(o, axis=2, keepdims=True)
    d = o - mu
    o_n = d * lax.rsqrt(jnp.mean(d * d, axis=2, keepdims=True) + EPS_GN)
    o_n = jnp.concatenate([o_n[h] for h in range(H_A)], axis=1)
    ya_ref[...] = ((o_n * gng_ref[...] + gnb_ref[...] + feat(F_BONUS)) * feat(F_G)).astype(ya_ref.dtype)

    @pl.when(pl.program_id(0) == pl.num_programs(0) - 1)
    def _():
        sfin_ref[...] = s_scr[...]


def _wkv_scan(feats, gc, lw):
    t = feats.shape[0]
    vec = pl.BlockSpec((1, D_A), lambda c: (0, 0))
    return pl.pallas_call(
        _scan_kernel,
        out_shape=[jax.ShapeDtypeStruct((t, D_A), BF16), jax.ShapeDtypeStruct((H_A, HEAD_A, HEAD_A), F32)],
        grid=(t // CHUNK,),
        in_specs=[pl.BlockSpec((CHUNK, N_FEATS * D_A), lambda c: (c, 0)), pl.BlockSpec((8, D_A), lambda c: (c, 0)),
                  vec, vec],
        out_specs=[pl.BlockSpec((CHUNK, D_A), lambda c: (c, 0)),
                   pl.BlockSpec((H_A, HEAD_A, HEAD_A), lambda c: (0, 0, 0))],
        scratch_shapes=[pltpu.VMEM((H_A, HEAD_A, HEAD_A), F32)],
        compiler_params=_params(("arbitrary",)),
        name="wkv_scan_prompt",
    )(feats, gc, lw["gn_g"], lw["gn_b"])


def _wkv_step_kernel(s_ref, rows_ref, sn_ref, o_ref):
    ri = lax.broadcasted_iota(jnp.int32, (1, 1, HEAD_A, HEAD_A), 2)
    ci = lax.broadcasted_iota(jnp.int32, (1, 1, HEAD_A, HEAD_A), 3)
    eye = ri == ci
    pt, rt, qt, kt, v, gc = (rows_ref[:, n] for n in range(6))
    s = s_ref[0]
    u = jnp.sum(s * pt, axis=3, keepdims=True)
    v_col = jnp.sum(jnp.where(eye, v, 0.0), axis=3, keepdims=True)
    m = s + u * qt + v_col * kt
    sn_ref[0] = m * gc
    o_col = jnp.sum(m * rt, axis=3, keepdims=True)
    o_ref[...] = jnp.sum(jnp.where(eye, o_col, 0.0), axis=2, keepdims=True)


def _wkv_step(state, pt, rt, qt, kt, v, gc, bt):
    b = pt.shape[0]
    rows = jnp.stack([pt, rt, qt, kt, v, gc], axis=1).reshape(b, 6, H_A, 1, HEAD_A)
    st_spec = pl.BlockSpec((1, bt, H_A, HEAD_A, HEAD_A), lambda i: (0, i, 0, 0, 0))
    sn, o = pl.pallas_call(
        _wkv_step_kernel,
        out_shape=[jax.ShapeDtypeStruct(state.shape, F32), jax.ShapeDtypeStruct((b, H_A, 1, HEAD_A), F32)],
        grid=(b // bt,),
        in_specs=[st_spec, pl.BlockSpec((bt, 6, H_A, 1, HEAD_A), lambda i: (i, 0, 0, 0, 0))],
        out_specs=[st_spec, pl.BlockSpec((bt, H_A, 1, HEAD_A), lambda i: (i, 0, 0, 0))],
        compiler_params=_params(("parallel",)),
        name="wkv_step_sample",
    )(state, rows)
    return sn, o.reshape(b, D_A)


def _post_kernel(o_ref, bonus_ref, g_ref, gng_ref, gnb_ref, hs_ref, hb_ref, y_ref):
    o = o_ref[...]
    hsum, hbc = hs_ref[...], hb_ref[...]
    mu = _head_sums(o, hsum, hbc) * (1.0 / HEAD_A)
    d = o - mu
    var = _head_sums(d * d, hsum, hbc) * (1.0 / HEAD_A)
    o_n = d * lax.rsqrt(var + EPS_GN) * gng_ref[...] + gnb_ref[...]
    y_ref[...] = ((o_n + bonus_ref[...]) * g_ref[...]).astype(y_ref.dtype)


def _rwkv_post(o, bonus, g, lw, tm, name):
    m = o.shape[0]
    big = pl.BlockSpec((tm, D_A), lambda i: (i, 0))
    vec = pl.BlockSpec((1, D_A), lambda i: (0, 0))
    return pl.pallas_call(
        _post_kernel,
        out_shape=jax.ShapeDtypeStruct((m, D_A), BF16),
        grid=(m // tm,),
        in_specs=[big, big, big, vec, vec, pl.BlockSpec((D_A, LANES), lambda i: (0, 0)),
                  pl.BlockSpec((LANES, D_A), lambda i: (0, 0))],
        out_specs=big,
        compiler_params=_params(("parallel",)),
        name=name,
    )(o, bonus, g, lw["gn_g"], lw["gn_b"], lw["hsum"], lw["hbc"])


def _attn_kernel(q_ref, kc_ref, vc_ref, kh_ref, vh_ref, bias_ref, o_ref, l_ref, *, dil, m_blocks, tiled):
    scale = HEAD_B ** -0.5
    n_streams = SUBLANES if tiled else dil
    n_units = n_streams * m_blocks

    stride = SUBLANES if tiled else dil
    if tiled:
        flat = lambda ref: ref.reshape(ref.shape[0] * SUBLANES, LANES)
        q_ref, kc_ref, vc_ref, kh_ref, vh_ref, o_ref, l_ref = map(flat, (q_ref, kc_ref, vc_ref, kh_ref, vh_ref,
                                                                          o_ref, l_ref))

    def rows(ref, r, start, size):
        if stride == 1:
            return ref[pl.ds(start, size), :]
        return ref[pl.ds(r + stride * start, size, stride=stride), :]

    def put(ref, r, start, val):
        if stride == 1:
            ref[pl.ds(start, BLK), :] = val
        else:
            ref[pl.ds(r + stride * start, BLK, stride=stride), :] = val

    qs, ks, vs, units = [], [], [], []
    for r in range(n_streams):
        for mb in range(m_blocks):
            units.append((r, mb))
            qs.append(rows(q_ref, r, BLK * mb, BLK))
            if mb == 0:
                ks.append(jnp.concatenate([rows(kh_ref, r, 0, BLK), rows(kc_ref, r, 0, BLK)], axis=0))
                vs.append(jnp.concatenate([rows(vh_ref, r, 0, BLK), rows(vc_ref, r, 0, BLK)], axis=0))
            else:
                ks.append(rows(kc_ref, r, BLK * (mb - 1), 2 * BLK))
                vs.append(rows(vc_ref, r, BLK * (mb - 1), 2 * BLK))

    def batch(xs, mul=None):
        pre = (lambda x: x) if mul is None else (lambda x: x * mul)
        return jnp.stack([pre(x[:, sub * HEAD_B:(sub + 1) * HEAD_B]).astype(BF16) for sub in range(2) for x in xs],
                         axis=0)

    assert math.frexp(scale)[0] == 0.5
    qb, kb, vb = batch(qs, scale), batch(ks), batch(vs)
    s = lax.dot_general(qb, kb, (((2,), (2,)), ((0,), (0,))), preferred_element_type=F32)
    s = s.reshape(2, n_units, BLK, 2 * BLK) + bias_ref[...][:, None]
    unit = lax.broadcasted_iota(jnp.int32, (1, n_units, 1, 2 * BLK), 1)
    col = lax.broadcasted_iota(jnp.int32, (1, n_units, 1, 2 * BLK), 3)
    no_prev = jnp.logical_and(jnp.logical_and(pl.program_id(0) == 0, (unit & (m_blocks - 1)) == 0), col < BLK)
    s = jnp.where(no_prev, -jnp.inf, s)
    m = jnp.max(s, axis=-1, keepdims=True)
    p = jnp.exp(s - m)
    l = jnp.sum(p, axis=-1, keepdims=True)
    pv = lax.dot_general(p.astype(BF16).reshape(2 * n_units, BLK, 2 * BLK), vb, (((2,), (1,)), ((0,), (0,))),
                         preferred_element_type=F32)
    o = pv.reshape(2, n_units, BLK, HEAD_B) / l
    lse = jnp.broadcast_to(m + jnp.log(l), (2, n_units, BLK, HEAD_B))
    for u, (r, mb) in enumerate(units):
        put(o_ref, r, BLK * mb, jnp.concatenate([o[0, u], o[1, u]], axis=1))
        put(l_ref, r, BLK * mb, jnp.concatenate([lse[0, u], lse[1, u]], axis=1))


def _attn_prompt(proj, bias, gi, rows_per_step):
    t, n_cols = proj.shape
    dil = DILATIONS[gi]
    span = BLK * dil
    tiled = dil % SUBLANES == 0
    if tiled:
        rows_per_step *= dil // SUBLANES
    m_blocks = rows_per_step // span
    cq = (COL_QKV + gi * D_G) // LANES
    ck = (COL_QKV + D_B + gi * D_G) // LANES
    cv = (COL_QKV + 2 * D_B + gi * D_G) // LANES
    bias_block = (2, BLK, 2 * BLK)
    if tiled:
        n_hi = dil // SUBLANES
        src = proj.reshape(t // dil, n_hi, SUBLANES, n_cols)
        grid = (t // rows_per_step, n_hi, H_G // 2)
        l_rows = rows_per_step // dil
        cur = lambda c0: pl.BlockSpec((l_rows, None, SUBLANES, LANES), lambda i, rh, hp: (i, rh, 0, c0 + hp))
        halo = lambda c0: pl.BlockSpec((BLK, None, SUBLANES, LANES),
                                       lambda i, rh, hp: (jnp.maximum(i * m_blocks - 1, 0), rh, 0, c0 + hp))
        out_spec = pl.BlockSpec((l_rows, None, SUBLANES, LANES), lambda i, rh, hp: (i, rh, 0, hp))
        out = jax.ShapeDtypeStruct((t // dil, n_hi, SUBLANES, D_G), F32)
        bias_spec = pl.BlockSpec(bias_block, lambda i, rh, hp: (hp, 0, 0))
    else:
        src = proj
        grid = (t // rows_per_step, H_G // 2)
        cur = lambda c0: pl.BlockSpec((rows_per_step, LANES), lambda i, hp: (i, c0 + hp))
        halo = lambda c0: pl.BlockSpec((span, LANES), lambda i, hp: (jnp.maximum(i * m_blocks - 1, 0), c0 + hp))
        out_spec = pl.BlockSpec((rows_per_step, LANES), lambda i, hp: (i, hp))
        out = jax.ShapeDtypeStruct((t, D_G), F32)
        bias_spec = pl.BlockSpec(bias_block, lambda i, hp: (hp, 0, 0))
    o, l = pl.pallas_call(
        functools.partial(_attn_kernel, dil=dil, m_blocks=m_blocks, tiled=tiled),
        out_shape=[out, out],
        grid=grid,
        in_specs=[cur(cq), cur(ck), cur(cv), halo(ck), halo(cv), bias_spec],
        out_specs=[out_spec, out_spec],
        compiler_params=_params(("parallel",) * len(grid)),
        name=f"attn_prompt_g{gi}",
    )(src, src, src, src, src, bias)
    return o.reshape(t, D_G), l.reshape(t, D_G)


def _attn_step_kernel(q_ref, kn_ref, vn_ref, c_ref, bias_ref, bias0_ref, o_ref, l_ref):
    scale = HEAD_B ** -0.5
    ri = lax.broadcasted_iota(jnp.int32, (1, 1, HEAD_B, HEAD_B), 2)
    ci = lax.broadcasted_iota(jnp.int32, (1, 1, HEAD_B, HEAD_B), 3)
    eye = ri == ci
    q = q_ref[...]
    q_col = jnp.sum(jnp.where(eye, q, 0.0), axis=3, keepdims=True)
    s = jnp.sum(c_ref[:, 0] * q_col, axis=2, keepdims=True) * scale + bias_ref[...]
    s_new = jnp.sum(q * kn_ref[...], axis=3, keepdims=True) * scale + bias0_ref[...]
    m = jnp.maximum(jnp.max(s, axis=3, keepdims=True), s_new)
    p = jnp.exp(s - m)
    p_new = jnp.exp(s_new - m)
    l = jnp.sum(p, axis=3, keepdims=True) + p_new
    o_col = jnp.sum(c_ref[:, 1] * p, axis=3, keepdims=True)
    o_row = jnp.sum(jnp.where(eye, o_col, 0.0), axis=2, keepdims=True)
    o_ref[...] = (o_row + p_new * vn_ref[...]) / l
    l_ref[...] = jnp.broadcast_to(m + jnp.log(l), o_ref.shape)


def _attn_step(q, k_new, v_new, cache, bias, bias0, gi):
    b = q.shape[0]
    w = cache.shape[2]
    bt = max(1, min(b, STEP_CACHE_WORDS // (2 * D_G * w)))
    cache_t = jnp.transpose(cache, (0, 1, 3, 4, 5, 2))
    vec = pl.BlockSpec((bt, H_G, 1, HEAD_B), lambda i: (i, 0, 0, 0))
    out = jax.ShapeDtypeStruct((b, H_G, 1, HEAD_B), F32)
    o, l = pl.pallas_call(
        _attn_step_kernel,
        out_shape=[out, out],
        grid=(b // bt,),
        in_specs=[vec, vec, vec,
                  pl.BlockSpec((None, bt, 2, H_G, HEAD_B, w), lambda i: (0, i, 0, 0, 0, 0)),
                  pl.BlockSpec((H_G, 1, w), lambda i: (0, 0, 0)),
                  pl.BlockSpec((H_G, 1, 1), lambda i: (0, 0, 0))],
        out_specs=[vec, vec],
        compiler_params=_params(("parallel",)),
        name=f"attn_step_g{gi}",
    )(q, k_new, v_new, cache_t, bias, bias0)
    return o.reshape(b, D_G), l.reshape(b, D_G)


def _mix_kernel(ya_ref, o1_ref, o2_ref, o3_ref, l1_ref, l2_ref, l3_ref, wa_ref, wb_ref, ga_ref, gb_ref, o_ref):
    l1, l2, l3 = l1_ref[...], l2_ref[...], l3_ref[...]
    m = jnp.maximum(jnp.maximum(l1, l2), l3)
    e1, e2, e3 = jnp.exp(l1 - m), jnp.exp(l2 - m), jnp.exp(l3 - m)
    den = e1 + e2 + e3
    yb = ((e1 / den) * o1_ref[...] + (e2 / den) * o2_ref[...] + (e3 / den) * o3_ref[...]).astype(BF16)
    mixed = ga_ref[...] * _dot_bf16(ya_ref[...], wa_ref[...]) + gb_ref[...] * _dot_bf16(yb, wb_ref[...])
    o_ref[...] = mixed.astype(o_ref.dtype)


def _mix(ya, outs, lses, wa, wb, proj, tm, name):
    m = ya.shape[0]
    n = D_MODEL
    assert COL_GATE % n == 0
    ga0 = COL_GATE // n
    grp = pl.BlockSpec((tm, D_G), lambda i: (i, 0))
    resident = lambda rows: pl.BlockSpec((rows, n), lambda i: (0, 0), pipeline_mode=pl.Buffered(1))
    return pl.pallas_call(
        _mix_kernel,
        out_shape=jax.ShapeDtypeStruct((m, n), BF16),
        grid=(m // tm,),
        in_specs=[pl.BlockSpec((tm, D_A), lambda i: (i, 0))] + [grp] * 6 + [
            resident(D_A), resident(D_G),
            pl.BlockSpec((tm, n), lambda i: (i, ga0)),
            pl.BlockSpec((tm, n), lambda i: (i, ga0 + 1)),
        ],
        out_specs=pl.BlockSpec((tm, n), lambda i: (i, 0)),
        compiler_params=_params(("parallel",)),
        name=name,
    )(ya, *outs, *lses, wa, wb, proj, proj)


def _wo_kernel(a_ref, w_ref, x_ref, g_ref, h_ref, hn_ref):
    h = x_ref[...] + _dot_bf16(a_ref[...], w_ref[...])
    h_ref[...] = h
    hn_ref[...] = _rms(h, g_ref[...]).astype(BF16)


def _wo(a, w, x, g, tm, name):
    m, k = a.shape
    n = w.shape[1]
    row = pl.BlockSpec((tm, n), lambda i: (i, 0))
    return pl.pallas_call(
        _wo_kernel,
        out_shape=[jax.ShapeDtypeStruct((m, n), F32), jax.ShapeDtypeStruct((m, n), BF16)],
        grid=(m // tm,),
        in_specs=[pl.BlockSpec((tm, k), lambda i: (i, 0)), pl.BlockSpec((k, n), lambda i: (0, 0)), row,
                  pl.BlockSpec((1, n), lambda i: (0, 0))],
        out_specs=[row, row],
        compiler_params=_params(("parallel",)),
        name=name,
    )(a, w, x, g.reshape(1, n))


def _gelu(x):
    return 0.5 * x * (1.0 + lax.erf(x * (1.0 / math.sqrt(2.0))))


def _ffn_up_kernel(h_ref, w1_ref, w2_ref, cw1_ref, cw2_ref, cb1_ref, cb2_ref, act_ref, t1_ref, t2_ref,
                   c1_scr, c2_scr, w1_scr, w2_scr):
    tm = h_ref.shape[0]

    @pl.when(pl.program_id(1) == 0)
    def _():
        c1_scr[...] = jnp.zeros_like(c1_scr)
        c2_scr[...] = jnp.zeros_like(c2_scr)
        w1_scr[...] = w1_ref[...].astype(BF16)
        w2_scr[...] = w2_ref[...].astype(BF16)

    a = h_ref[...]
    row = lax.broadcasted_iota(jnp.int32, (tm, 1), 0)

    def conv(u, carry_ref, cw_ref, cb_ref):
        m1 = jnp.where(row == 0, carry_ref[7:8, :], pltpu.roll(u, 1, axis=0))
        m2 = jnp.where(row == 0, carry_ref[6:7, :], jnp.where(row == 1, carry_ref[7:8, :], pltpu.roll(u, 2, axis=0)))
        return cb_ref[...] + cw_ref[0:1, :] * m2 + cw_ref[1:2, :] * m1 + cw_ref[2:3, :] * u

    u1 = _dot_bf16(a, w1_scr[...])
    u2 = _dot_bf16(a, w2_scr[...])
    c1 = conv(u1, c1_scr, cw1_ref, cb1_ref)
    c2 = conv(u2, c2_scr, cw2_ref, cb2_ref)
    act_ref[...] = (_gelu(c1) * c2).astype(act_ref.dtype)
    c1_scr[...] = u1[tm - 8:tm]
    c2_scr[...] = u2[tm - 8:tm]
    t1_ref[...] = u1[tm - 8:tm]
    t2_ref[...] = u2[tm - 8:tm]


def _ffn_up_prompt(hn, w_up, conv_w, conv_b, tm, tn):
    m = hn.shape[0]
    nj = D_FF // tn
    tail = jax.ShapeDtypeStruct((8, D_FF), F32)
    return pl.pallas_call(
        _ffn_up_kernel,
        out_shape=[jax.ShapeDtypeStruct((m, D_FF), BF16), tail, tail],
        grid=(nj, m // tm),
        in_specs=[
            pl.BlockSpec((tm, D_MODEL), lambda j, i: (i, 0)),
            pl.BlockSpec((D_MODEL, tn), lambda j, i: (0, j)),
            pl.BlockSpec((D_MODEL, tn), lambda j, i: (0, nj + j)),
            pl.BlockSpec((CONV_W, tn), lambda j, i: (0, j)),
            pl.BlockSpec((CONV_W, tn), lambda j, i: (0, nj + j)),
            pl.BlockSpec((1, tn), lambda j, i: (0, j)),
            pl.BlockSpec((1, tn), lambda j, i: (0, nj + j)),
        ],
        out_specs=[pl.BlockSpec((tm, tn), lambda j, i: (i, j)),
                   pl.BlockSpec((8, tn), lambda j, i: (0, j)),
                   pl.BlockSpec((8, tn), lambda j, i: (0, j))],
        scratch_shapes=[pltpu.VMEM((8, tn), F32), pltpu.VMEM((8, tn), F32),
                        pltpu.VMEM((D_MODEL, tn), BF16), pltpu.VMEM((D_MODEL, tn), BF16)],
        compiler_params=_params(("arbitrary", "arbitrary")),
        name="ffn_up_prompt",
    )(hn, w_up, w_up, conv_w, conv_w, conv_b, conv_b)


def _ffn_act_step_kernel(up_ref, prev_ref, cw_ref, cb_ref, act_ref):
    up = up_ref[...]
    w = 2 * D_FF
    c = cb_ref[...] + cw_ref[0:1, :] * prev_ref[:, 0:w] + cw_ref[1:2, :] * prev_ref[:, w:2 * w] + cw_ref[2:3, :] * up
    act_ref[...] = (_gelu(c[:, 0:D_FF]) * c[:, D_FF:w]).astype(act_ref.dtype)


def _ffn_act_step(up, conv_prev, conv_w, conv_b):
    b = up.shape[0]
    return pl.pallas_call(
        _ffn_act_step_kernel,
        out_shape=jax.ShapeDtypeStruct((b, D_FF), BF16),
        compiler_params=pltpu.CompilerParams(vmem_limit_bytes=VMEM_LIMIT),
        name="ffn_act_sample",
    )(up, conv_prev.reshape(b, (CONV_W - 1) * 2 * D_FF), conv_w, conv_b)


def _down_kernel(a_ref, w_ref, h_ref, g_ref, o_ref, acc_ref):
    @pl.when(pl.program_id(1) == 0)
    def _():
        acc_ref[...] = h_ref[...]

    out = acc_ref[...] + _dot_bf16(a_ref[...], w_ref[...])
    acc_ref[...] = out
    inv = lax.rsqrt(jnp.mean(out * out, axis=-1, keepdims=True) + EPS_RMS)
    o_ref[...] = out * inv * g_ref[...]


def _ffn_down(act, w_down, h, g, tm, tk, name):
    m = act.shape[0]
    nk = D_FF // tk
    return pl.pallas_call(
        _down_kernel,
        out_shape=jax.ShapeDtypeStruct((m, D_MODEL), F32),
        grid=(m // tm, nk),
        in_specs=[
            pl.BlockSpec((tm, tk), lambda i, k: (i, k)),
            pl.BlockSpec((tk, D_MODEL), lambda i, k: (k, 0)),
            pl.BlockSpec((tm, D_MODEL), lambda i, k: (i, 0)),
            pl.BlockSpec((1, D_MODEL), lambda i, k: (0, 0)),
        ],
        out_specs=pl.BlockSpec((tm, D_MODEL), lambda i, k: (i, 0)),
        scratch_shapes=[pltpu.VMEM((tm, D_MODEL), F32)],
        compiler_params=_params(("parallel", "arbitrary")),
        name=name,
    )(act, w_down, h, g.reshape(1, D_MODEL))


def _rel_bucket(dist):
    max_exact = N_BUCKETS // 2
    d_f = jnp.maximum(dist, 1).astype(F32)
    large = max_exact + (jnp.log(d_f / max_exact) / math.log(MAX_DISTANCE / max_exact)
                         * (N_BUCKETS - max_exact)).astype(jnp.int32)
    large = jnp.minimum(large, N_BUCKETS - 1)
    return jnp.where(dist < max_exact, dist, large)


def _bias_rows(tab, dist):
    onehot = (_rel_bucket(dist)[None, :] == jnp.arange(N_BUCKETS)[:, None]).astype(F32)
    return jnp.dot(tab.T, onehot, precision=HIGHEST)


def _bias_tables(rel_bias, gi):
    dil = DILATIONS[gi]
    win = WINDOWS[gi]
    reach = win // dil
    tab = rel_bias[:, gi * H_G:(gi + 1) * H_G]
    qi = jnp.arange(BLK)[:, None]
    ki = jnp.arange(2 * BLK)[None, :]
    rel = qi + BLK - ki
    blk = _bias_rows(tab, (dil * jnp.maximum(rel, 0)).reshape(-1)).reshape(H_G, BLK, 2 * BLK)
    blk = jnp.where(((rel >= 0) & (rel <= reach))[None], blk, -jnp.inf)
    back = win - jnp.arange(win)
    step = jnp.where((back % dil == 0)[None, :], _bias_rows(tab, back), -jnp.inf)[:, None, :]
    step0 = _bias_rows(tab, jnp.zeros((1,), jnp.int32))[:, None, :]
    return blk, step, step0


def _tiles(m, prompt):
    if prompt:
        return dict(tag="prompt", proj_m=1024, prep_m=2 * CHUNK, attn_rows=2048, mix_m=512, wo_m=512, up_m=1024,
                    up_n=512, down_m=512, down_k=D_FF // 2)
    return dict(tag="sample", proj_m=m, prep_m=m, step_b=4, post_m=m, mix_m=m, wo_m=m, up_n=512, down_m=m,
                down_k=D_FF // 2)


def _layer(x, lw, bias_blk, prompt, state=None, proj=None):
    m = x.shape[0]
    tiles = _tiles(m, prompt)
    if proj is None:
        proj = _proj(x, lw["norm1_g"], lw["w_in"], lw["in_bias"], tiles["proj_m"], "proj_" + tiles["tag"])

    if prompt:
        feats, gc = _rwkv_prep(proj, None, tiles["prep_m"], CHUNK, lw, "rwkv_prep_prompt")
        ya, wkv_new = _wkv_scan(feats, gc, lw)
        wkv_new = wkv_new[None, None]
    else:
        feats, gc = _rwkv_prep(proj, state["shift"], tiles["prep_m"], 1, lw, "rwkv_prep_sample")
        f = lambda n: feats[:, n * D_A:(n + 1) * D_A]
        wkv_new, o = _wkv_step(state["wkv"], f(F_PT), f(F_RT), f(F_QT), f(F_KT), f(F_V), gc, tiles["step_b"])
        ya = _rwkv_post(o, f(F_BONUS), f(F_G), lw, tiles["post_m"], "rwkv_post_" + tiles["tag"])

    outs, lses = [], []
    for gi in range(N_GROUPS):
        if prompt:
            o_g, l_g = _attn_prompt(proj, bias_blk[gi][0], gi, tiles["attn_rows"])
        else:
            sl = lambda c0: proj[:, c0 + gi * D_G:c0 + (gi + 1) * D_G].reshape(m, H_G, 1, HEAD_B)
            o_g, l_g = _attn_step(sl(COL_QKV), sl(COL_QKV + D_B), sl(COL_QKV + 2 * D_B), state["win"][gi],
                                  bias_blk[gi][1], bias_blk[gi][2], gi)
        outs.append(o_g)
        lses.append(l_g)
    mixed = _mix(ya, outs, lses, lw["w_out_a"], lw["w_out_b"], proj, tiles["mix_m"], "mix_" + tiles["tag"])
    h, hn = _wo(mixed, lw["w_o"], x, lw["norm2_g"], tiles["wo_m"], "wo_" + tiles["tag"])

    if prompt:
        act, t1, t2 = _ffn_up_prompt(hn, lw["w_up"], lw["conv_w"], lw["conv_b"], tiles["up_m"], tiles["up_n"])
        conv_new = jnp.concatenate([t1[6:8], t2[6:8]], axis=1)[None, None]
    else:
        up = _matmul(hn, lw["w_up"], m, tiles["up_n"], "up_sample")
        act = _ffn_act_step(up, state["conv"], lw["conv_w"], lw["conv_b"])
        conv_new = jnp.concatenate([state["conv"][:, 1:], up[:, None, :]], axis=1)[None]
    y = _ffn_down(act, lw["w_down"], h, lw["normf_g"], tiles["down_m"], tiles["down_k"], "down_" + tiles["tag"])
    return y, proj, wkv_new, conv_new


def kernel(x_prompt, x_sample, state_wkv, state_shift, state_ffn_conv, cache_win1, cache_win2, cache_win3, rel_bias,
           norm1_g, w_in, gate_b, mu_shift, w0, w_up_decay, a0, w_up_aaa, w_up_gate, k_k, k_a, r_k, gn_g, gn_b,
           w_out_a, w_out_b, w_o, norm2_g, w_up, conv_w, conv_b, w_down, normf_g):
    row = lambda a: a.reshape(1, -1)
    w_lora = jnp.zeros((D_LORA, 3 * D_A), F32)
    w_lora = w_lora.at[0:D_DECAY_LORA, 0:D_A].set(w_up_decay[0])
    w_lora = w_lora.at[D_DECAY_LORA:D_DECAY_LORA + D_AAA_LORA, D_A:2 * D_A].set(w_up_aaa[0])
    w_lora = w_lora.at[D_DECAY_LORA + D_AAA_LORA:, 2 * D_A:].set(w_up_gate[0])
    head = np.arange(D_A) // HEAD_A
    lw = dict(
        norm1_g=norm1_g[0], norm2_g=norm2_g[0], normf_g=normf_g,
        in_bias=jnp.concatenate([jnp.zeros((1, COL_GATE), F32), row(gate_b[0])], axis=1),
        mu=row(mu_shift[0]), w_lora=jnp.stack(_split2(w_lora)), w0=row(w0[0]), a0=row(a0[0]), k_k=row(k_k[0]), k_a=row(k_a[0]),
        r_k=row(r_k[0]), gn_g=row(gn_g[0]), gn_b=row(gn_b[0]),
        hsum=jnp.asarray(head[:, None] == np.arange(LANES)[None, :], BF16),
        hbc=jnp.asarray(np.arange(LANES)[:, None] == head[None, :], BF16),
        w_out_a=w_out_a[0].astype(BF16), w_out_b=w_out_b[0].astype(BF16), w_o=w_o[0].astype(BF16),
        w_up=w_up[0], conv_w=conv_w[0], conv_b=row(conv_b[0]), w_down=w_down[0].astype(BF16),
    )
    bias = [_bias_tables(rel_bias, gi) for gi in range(N_GROUPS)]

    proj_s, lw["w_in"] = _proj_cast(x_sample[:, 0], norm1_g[0], w_in[0], lw["in_bias"])
    y_p, proj_p, wkv_p, conv_p = _layer(x_prompt[0], lw, bias, prompt=True)

    b = DEC_BATCH
    state = dict(wkv=state_wkv, shift=state_shift.reshape(b, D_SHIFT), conv=state_ffn_conv[0],
                 win=(cache_win1, cache_win2, cache_win3))
    y_s, proj_s, wkv_s, conv_s = _layer(x_sample[:, 0], lw, bias, prompt=False, state=state, proj=proj_s)

    def kv_rows(proj, lo, gi):
        k = proj[lo:, COL_QKV + D_B + gi * D_G:COL_QKV + D_B + (gi + 1) * D_G]
        v = proj[lo:, COL_QKV + 2 * D_B + gi * D_G:COL_QKV + 2 * D_B + (gi + 1) * D_G]
        n = k.shape[0]
        return jnp.stack([k.reshape(n, H_G, HEAD_B), v.reshape(n, H_G, HEAD_B)], axis=1)

    win_p = [kv_rows(proj_p, SEQ - min(WINDOWS[gi], SEQ), gi)[None, None] for gi in range(N_GROUPS)]
    win_s = [kv_rows(proj_s, 0, gi)[None, :, None] for gi in range(N_GROUPS)]
    return (y_p[None], y_s[:, None],
            wkv_p, wkv_s,
            proj_p[SEQ - 1:, 0:D_SHIFT][None, None], proj_s[:, 0:D_SHIFT][None, :, None],
            conv_p, conv_s,
            win_p[0], win_s[0], win_p[1], win_s[1], win_p[2], win_s[2])
```

```python
import functools
import math

import numpy as np
import jax
import jax.numpy as jnp
from jax import lax
from jax.experimental import pallas as pl
from jax.experimental.pallas import tpu as pltpu

F32 = jnp.float32
BF16 = jnp.bfloat16
HIGHEST = lax.Precision.HIGHEST

D_MODEL = 2048
SEQ = 8192
DEC_BATCH = 32
HEAD_A = 64
H_A = 16
D_A = H_A * HEAD_A
D_DECAY_LORA = 96
D_AAA_LORA = 96
D_GATE_LORA = 64
D_LORA = D_DECAY_LORA + D_AAA_LORA + D_GATE_LORA
D_SHIFT = 3 * D_A + D_LORA
EPS_GN = 64e-5
HEAD_B = 64
H_G = 8
D_G = H_G * HEAD_B
WINDOWS = (128, 512, 2048)
DILATIONS = (1, 4, 16)
N_GROUPS = 3
D_B = N_GROUPS * D_G
BLK = 128
N_BUCKETS = 32
MAX_DISTANCE = 2048
COL_QKV = D_SHIFT
COL_GATE_SRC = D_SHIFT + 3 * D_B
D_IN = COL_GATE_SRC + 2 * D_MODEL
PROJ_TN = 1024
COL_GATE = -(-COL_GATE_SRC // PROJ_TN) * PROJ_TN
D_IN_PAD = COL_GATE + 2 * D_MODEL
D_FF = 5632
CONV_W = 3
EPS_RMS = 1e-6
CHUNK = 64
F_RT, F_KT, F_QT, F_PT, F_V, F_BONUS, F_G = range(7)
N_FEATS = 7
LANES = 128
SUBLANES = 8
VMEM_LIMIT = 56 * 1024 * 1024
STEP_CACHE_WORDS = 2 * 1024 * 1024


def _params(sem, vmem=VMEM_LIMIT):
    return pltpu.CompilerParams(dimension_semantics=sem, vmem_limit_bytes=vmem)


def _sigmoid(x):
    return 1.0 / (1.0 + jnp.exp(-x))


def _dot_bf16(a, b):
    return jnp.dot(a, b, preferred_element_type=F32)


def _split2(x):
    hi = x.astype(BF16)
    return hi, (x - hi.astype(F32)).astype(BF16)


def _split3(x):
    hi = x.astype(BF16)
    r1 = x - hi.astype(F32)
    mid = r1.astype(BF16)
    lo = (r1 - mid.astype(F32)).astype(BF16)
    return hi, mid, lo


def _dot_exact_rhs(x, m):
    hi, mid, lo = _split3(x)
    return _dot_bf16(hi, m) + _dot_bf16(mid, m) + _dot_bf16(lo, m)


def _head_sums(x, hsum, hbc):
    return _dot_exact_rhs(_dot_exact_rhs(x, hsum), hbc)


def _dot_exact_lhs(m, x):
    hi, mid, lo = _split3(x)
    return _dot_bf16(m, hi) + _dot_bf16(m, mid) + _dot_bf16(m, lo)


def _rms(x, g):
    return x * lax.rsqrt(jnp.mean(x * x, axis=-1, keepdims=True) + EPS_RMS) * g


def _proj_cast_kernel(x_ref, g_ref, w_ref, b_ref, o_ref, wb_ref, xn_scr, *, gap_block):
    j = pl.program_id(0)

    @pl.when(j == 0)
    def _():
        xn_scr[...] = _rms(x_ref[...], g_ref[...]).astype(BF16)

    wb = jnp.where(j != gap_block, w_ref[...], 0.0).astype(BF16)
    wb_ref[...] = wb
    acc = _dot_bf16(xn_scr[...], wb)
    tn = o_ref.shape[1]
    col = lax.broadcasted_iota(jnp.int32, (1, tn), 1) + j * tn
    o_ref[...] = jnp.where(col >= COL_GATE, _sigmoid(acc + b_ref[...]), acc)


def _proj_cast(x, g, w, bias):
    m, k = x.shape
    tn = COL_GATE - COL_GATE_SRC
    assert tn % LANES == 0 and COL_GATE_SRC % tn == 0
    gap_block = COL_GATE_SRC // tn
    return pl.pallas_call(
        functools.partial(_proj_cast_kernel, gap_block=gap_block),
        out_shape=[jax.ShapeDtypeStruct((m, D_IN_PAD), F32), jax.ShapeDtypeStruct((k, D_IN_PAD), BF16)],
        grid=(D_IN_PAD // tn,),
        in_specs=[
            pl.BlockSpec((m, k), lambda j: (0, 0)),
            pl.BlockSpec((1, k), lambda j: (0, 0)),
            pl.BlockSpec((k, tn), lambda j: (0, jnp.where(j < gap_block, j, jnp.maximum(j - 1, 0)))),
            pl.BlockSpec((1, tn), lambda j: (0, j)),
        ],
        out_specs=[pl.BlockSpec((m, tn), lambda j: (0, j)), pl.BlockSpec((k, tn), lambda j: (0, j))],
        scratch_shapes=[pltpu.VMEM((m, k), BF16)],
        compiler_params=_params(("arbitrary",)),
        name="proj_sample_cast_w_in",
    )(x, g.reshape(1, k), w, bias)


def _proj_kernel(x_ref, g_ref, w_ref, b_ref, o_ref, xn_scr):
    @pl.when(pl.program_id(1) == 0)
    def _():
        xn_scr[...] = _rms(x_ref[...], g_ref[...]).astype(BF16)

    acc = _dot_bf16(xn_scr[...], w_ref[...])
    tn = o_ref.shape[1]
    col = lax.broadcasted_iota(jnp.int32, (1, tn), 1) + pl.program_id(1) * tn
    o_ref[...] = jnp.where(col >= COL_GATE, _sigmoid(acc + b_ref[...]), acc)


def _proj(x, g, w, bias, tm, name):
    m, k = x.shape
    n = w.shape[1]
    return pl.pallas_call(
        _proj_kernel,
        out_shape=jax.ShapeDtypeStruct((m, n), F32),
        grid=(m // tm, n // PROJ_TN),
        in_specs=[
            pl.BlockSpec((tm, k), lambda i, j: (i, 0)),
            pl.BlockSpec((1, k), lambda i, j: (0, 0)),
            pl.BlockSpec((k, PROJ_TN), lambda i, j: (0, j)),
            pl.BlockSpec((1, PROJ_TN), lambda i, j: (0, j)),
        ],
        out_specs=pl.BlockSpec((tm, PROJ_TN), lambda i, j: (i, j)),
        scratch_shapes=[pltpu.VMEM((tm, k), BF16)],
        compiler_params=_params(("parallel", "arbitrary")),
        name=name,
    )(x, g.reshape(1, k), w, bias)


def _matmul_kernel(a_ref, w_ref, o_ref):
    o_ref[...] = _dot_bf16(a_ref[...], w_ref[...].astype(BF16))


def _matmul(a, w, tm, tn, name):
    m, k = a.shape
    n = w.shape[1]
    return pl.pallas_call(
        _matmul_kernel,
        out_shape=jax.ShapeDtypeStruct((m, n), F32),
        grid=(m // tm, n // tn),
        in_specs=[pl.BlockSpec((tm, k), lambda i, j: (i, 0)), pl.BlockSpec((k, tn), lambda i, j: (0, j))],
        out_specs=pl.BlockSpec((tm, tn), lambda i, j: (i, j)),
        compiler_params=_params(("parallel", "arbitrary")),
        name=name,
    )(a, w)


def _prep_kernel(p_ref, prev_ref, mu_ref, wl_ref, w0_ref, a0_ref, kk_ref, ka_ref, rk_ref, lmat_ref, sel_ref,
                 hs_ref, hb_ref, f_ref, gc_ref, *, rows_are_time):
    p = p_ref[...]
    tm = p.shape[0]
    if rows_are_time:
        last = jnp.where(pl.program_id(0) == 0, 0.0, prev_ref[7:8, :])
        row = lax.broadcasted_iota(jnp.int32, (tm, 1), 0)
        prev = jnp.where(row == 0, last, pltpu.roll(p, 1, axis=0))
    else:
        prev = prev_ref[...]
    xm = p + mu_ref[...] * (prev - p)
    r = xm[:, 0:D_A]
    k = xm[:, D_A:2 * D_A]
    v = xm[:, 2 * D_A:3 * D_A]
    xl = xm[:, 3 * D_A:D_SHIFT]
    lane = lax.broadcasted_iota(jnp.int32, xl.shape, 1)
    act = jnp.where(lane < D_DECAY_LORA, jnp.tanh(xl),
                    jnp.where(lane < D_DECAY_LORA + D_AAA_LORA, xl, _sigmoid(xl)))
    act_hi, act_lo = _split2(act)
    w_hi, w_lo = wl_ref[0], wl_ref[1]
    lora = _dot_bf16(act_hi, w_hi) + _dot_bf16(act_hi, w_lo) + _dot_bf16(act_lo, w_hi)
    y = -(w0_ref[...] + lora[:, 0:D_A])
    softplus = jnp.maximum(y, 0.0) + jnp.log(1.0 + jnp.exp(-jnp.abs(y)))
    logw = -jnp.exp(-softplus - 0.5)
    a = _sigmoid(a0_ref[...] + lora[:, D_A:2 * D_A])
    g = lora[:, 2 * D_A:3 * D_A]
    kkr = k * kk_ref[...]
    kp = k * (1.0 + (a - 1.0) * ka_ref[...])
    seg = _head_sums(jnp.concatenate([kkr * kkr, r * kp * rk_ref[...]], axis=0), hs_ref[...], hb_ref[...])
    kk = kkr / jnp.maximum(jnp.sqrt(seg[0:tm]), 1e-12)
    cum = _dot_exact_lhs(lmat_ref[...], logw)
    e_out = jnp.exp(-cum)
    feats = {F_RT: r * jnp.exp(cum), F_KT: kp * e_out, F_QT: kk * a * e_out, F_PT: -kk * jnp.exp(cum - logw),
             F_V: v, F_BONUS: seg[tm:2 * tm] * v, F_G: g}
    for n, val in feats.items():
        f_ref[:, n * D_A:(n + 1) * D_A] = val
    gc_ref[...] = jnp.exp(_dot_exact_lhs(sel_ref[...], cum))


def _rwkv_prep(proj, prev, tm, chunk, lw, name):
    m = proj.shape[0]
    rows_are_time = prev is None
    n_tiles = m // tm
    t = np.arange(tm)
    lmat = ((t[:, None] // chunk == t[None, :] // chunk) & (t[None, :] <= t[:, None])).astype(np.float32)
    sel_stride = 1 if chunk == 1 else 8
    n_sel = sel_stride * (tm // chunk)
    sel = np.zeros((n_sel, tm), np.float32)
    for c in range(tm // chunk):
        sel[sel_stride * c, (c + 1) * chunk - 1] = 1.0
    if rows_are_time:
        prev_arr = proj
        prev_spec = pl.BlockSpec((8, D_SHIFT), lambda i: (jnp.maximum(i * (tm // 8) - 1, 0), 0))
    else:
        prev_arr = prev
        prev_spec = pl.BlockSpec((tm, D_SHIFT), lambda i: (i, 0))
    vec = lambda d: pl.BlockSpec((1, d), lambda i: (0, 0))
    full = lambda a: pl.BlockSpec(a.shape, lambda i: (0, 0))
    lmat_b = jnp.asarray(lmat, BF16)
    sel_b = jnp.asarray(sel, BF16)
    return pl.pallas_call(
        functools.partial(_prep_kernel, rows_are_time=rows_are_time),
        out_shape=[jax.ShapeDtypeStruct((m, N_FEATS * D_A), F32),
                   jax.ShapeDtypeStruct((n_tiles * n_sel, D_A), F32)],
        grid=(n_tiles,),
        in_specs=[pl.BlockSpec((tm, D_SHIFT), lambda i: (i, 0)), prev_spec, vec(D_SHIFT),
                  pl.BlockSpec(lw["w_lora"].shape, lambda i: (0, 0, 0)),
                  vec(D_A), vec(D_A), vec(D_A), vec(D_A), vec(D_A), full(lmat_b), full(sel_b), full(lw["hsum"]),
                  full(lw["hbc"])],
        out_specs=[pl.BlockSpec((tm, N_FEATS * D_A), lambda i: (i, 0)), pl.BlockSpec((n_sel, D_A), lambda i: (i, 0))],
        compiler_params=_params(("parallel",)),
        name=name,
    )(proj, prev_arr, lw["mu"], lw["w_lora"], lw["w0"], lw["a0"], lw["k_k"], lw["k_a"], lw["r_k"], lmat_b, sel_b,
      lw["hsum"], lw["hbc"])


_NN = (((2,), (1,)), ((0,), (0,)))
_NT = (((2,), (2,)), ((0,), (0,)))
PAIRS = H_A // 2
LOG2_HEAD = HEAD_A.bit_length() - 1
assert HEAD_A == 1 << LOG2_HEAD and 2 * HEAD_A == LANES


def _lane_head(n):
    return (lax.broadcasted_iota(jnp.int32, (1, 1, n), 2) >> LOG2_HEAD) & 1


def _keep_head(x, head, which):
    return jnp.where(head == which, x, jnp.zeros_like(x))


def _pdot_nn(a, b):
    ah, al = _split2(a)
    bh, bl = _split2(b)
    head = _lane_head(b.shape[2])
    bd = lambda x: jnp.concatenate([_keep_head(x, head, 0), _keep_head(x, head, 1)], axis=1)
    a_cat = jnp.concatenate([ah, ah, al], axis=2)
    b_cat = jnp.concatenate([bd(bh), bd(bl), bd(bh)], axis=1)
    return lax.dot_general(a_cat, b_cat, _NN, preferred_element_type=F32)


def _pdot_nt(a, bs):
    ah, al = _split2(a)
    head = _lane_head(LANES)
    hi, lo = [], []
    for b in bs:
        bh, bl = _split2(b)
        for which in (0, 1):
            hi.append(_keep_head(bh, head, which))
            lo.append(_keep_head(bl, head, which))
    bh, bl = jnp.concatenate(hi, axis=1), jnp.concatenate(lo, axis=1)
    a_cat = jnp.concatenate([ah, ah, al], axis=2)
    b_cat = jnp.concatenate([bh, bl, bh], axis=2)
    return lax.dot_general(a_cat, b_cat, _NT, preferred_element_type=F32)


def _pdot_tn(x, y):
    nx = x.shape[2]
    xh, xl = _split2(jnp.swapaxes(x, 1, 2))
    yh, yl = _split2(y)
    res = lax.dot_general(jnp.concatenate([xh, xl], axis=1), jnp.concatenate([yh, yl], axis=2), _NN,
                          preferred_element_type=F32)
    tot = res[:, 0:nx, 0:LANES] + res[:, 0:nx, LANES:] + res[:, nx:, 0:LANES]
    head = _lane_head(LANES)
    return [jnp.where(head == 0, tot[:, g:g + HEAD_A], tot[:, g + HEAD_A:g + LANES]) for g in range(0, nx, LANES)]


def _scan_kernel(f_ref, gc_ref, gng_ref, gnb_ref, ya_ref, sfin_ref, s_scr):
    @pl.when(pl.program_id(0) == 0)
    def _():
        s_scr[...] = jnp.zeros_like(s_scr)

    def pairs(x):
        return jnp.stack([x[:, i * LANES:(i + 1) * LANES] for i in range(PAIRS)], axis=0)

    feat = lambda n: f_ref[:, n * D_A:(n + 1) * D_A]
    p, r, q, k, v = (pairs(feat(n)) for n in (F_PT, F_RT, F_QT, F_KT, F_V))
    gam = pairs(gc_ref[0:1, :])
    ri = lax.broadcasted_iota(jnp.int32, (1, CHUNK, LANES), 1)
    ci = lax.broadcasted_iota(jnp.int32, (1, CHUNK, LANES), 2) & (HEAD_A - 1)
    strict = ri > ci
    incl = ri >= ci
    eye = (ri == ci).astype(F32)

    gram = _pdot_nt(jnp.concatenate([p, r], axis=1), [q, k])
    a_qp = jnp.where(strict, gram[:, 0:CHUNK, 0:LANES], 0.0)
    a_kp = jnp.where(strict, gram[:, 0:CHUNK, LANES:], 0.0)
    a_rq = jnp.where(incl, gram[:, CHUNK:, 0:LANES], 0.0)
    a_rk = jnp.where(incl, gram[:, CHUNK:, LANES:], 0.0)
    same = lambda log2_bs: (ri >> log2_bs) == (ci >> log2_bs)
    a_d = jnp.where(same(4), a_qp, 0.0)
    tinv = eye + a_d
    pw = _pdot_nn(a_d, a_d)
    for _ in range(2):
        both = _pdot_nn(jnp.concatenate([pw, tinv], axis=1), pw)
        pw = both[:, 0:CHUNK]
        tinv = tinv + both[:, CHUNK:]
    tinv = tinv + _pdot_nn(tinv, pw)
    for log2_bs in (4, 5):
        off = jnp.where(jnp.logical_and(same(log2_bs + 1), jnp.logical_not(same(log2_bs))), a_qp, 0.0)
        tinv = tinv + _pdot_nn(tinv, _pdot_nn(off, tinv))
    av = _pdot_nn(jnp.concatenate([a_kp, a_rk], axis=1), v)
    pw_hat = _pdot_nn(tinv, jnp.concatenate([p, av[:, 0:CHUNK]], axis=2))
    x = _pdot_nn(a_rq, pw_hat)
    r_hat = r + x[:, :, 0:LANES]
    o_loc = x[:, :, LANES:] + av[:, CHUNK:]
    y_p, y_w = _pdot_tn(pw_hat, q)
    (vk,) = _pdot_tn(v, k)
    m_mat = (eye + y_p) * gam
    n_mat = (y_w + vk) * gam

    s0 = s_scr[...]
    o = _pdot_nt(r_hat, [s0]) + o_loc
    s_scr[...] = _pdot_nn(s0, m_mat) + n_mat

    head = _lane_head(LANES)

    def head_mean(z):
        sums = [jnp.sum(_keep_head(z, head, which), axis=2, keepdims=True) for which in (0, 1)]
        return jnp.where(head == 0, sums[0], sums[1]) * (1.0 / HEAD_A)

    d = o - head_mean(o)
    o_n = d * lax.rsqrt(head_mean(d * d) + EPS_GN)
    o_n = jnp.concatenate([o_n[i] for i in range(PAIRS)], axis=1)
    ya_ref[...] = ((o_n * gng_ref[...] + gnb_ref[...] + feat(F_BONUS)) * feat(F_G)).astype(ya_ref.dtype)

    @pl.when(pl.program_id(0) == pl.num_programs(0) - 1)
    def _():
        sfin_ref[...] = s_scr[...]


def _wkv_scan(feats, gc, lw):
    t = feats.shape[0]
    vec = pl.BlockSpec((1, D_A), lambda c: (0, 0))
    pair_state = (PAIRS, HEAD_A, LANES)
    ya, s = pl.pallas_call(
        _scan_kernel,
        out_shape=[jax.ShapeDtypeStruct((t, D_A), BF16), jax.ShapeDtypeStruct(pair_state, F32)],
        grid=(t // CHUNK,),
        in_specs=[pl.BlockSpec((CHUNK, N_FEATS * D_A), lambda c: (c, 0)), pl.BlockSpec((8, D_A), lambda c: (c, 0)),
                  vec, vec],
        out_specs=[pl.BlockSpec((CHUNK, D_A), lambda c: (c, 0)), pl.BlockSpec(pair_state, lambda c: (0, 0, 0))],
        scratch_shapes=[pltpu.VMEM(pair_state, F32)],
        compiler_params=_params(("arbitrary",)),
        name="wkv_scan_prompt",
    )(feats, gc, lw["gn_g"], lw["gn_b"])
    s = s.reshape(PAIRS, HEAD_A, 2, HEAD_A).transpose(0, 2, 1, 3).reshape(H_A, HEAD_A, HEAD_A)
    return ya, s


def _wkv_step_kernel(s_ref, rows_ref, sn_ref, o_ref):
    ri = lax.broadcasted_iota(jnp.int32, (1, 1, HEAD_A, HEAD_A), 2)
    ci = lax.broadcasted_iota(jnp.int32, (1, 1, HEAD_A, HEAD_A), 3)
    eye = ri == ci
    pt, rt, qt, kt, v, gc = (rows_ref[:, n] for n in range(6))
    s = s_ref[0]
    u = jnp.sum(s * pt, axis=3, keepdims=True)
    v_col = jnp.sum(jnp.where(eye, v, 0.0), axis=3, keepdims=True)
    m = s + u * qt + v_col * kt
    sn_ref[0] = m * gc
    o_col = jnp.sum(m * rt, axis=3, keepdims=True)
    o_ref[...] = jnp.sum(jnp.where(eye, o_col, 0.0), axis=2, keepdims=True)


def _wkv_step(state, pt, rt, qt, kt, v, gc, bt):
    b = pt.shape[0]
    rows = jnp.stack([pt, rt, qt, kt, v, gc], axis=1).reshape(b, 6, H_A, 1, HEAD_A)
    st_spec = pl.BlockSpec((1, bt, H_A, HEAD_A, HEAD_A), lambda i: (0, i, 0, 0, 0))
    sn, o = pl.pallas_call(
        _wkv_step_kernel,
        out_shape=[jax.ShapeDtypeStruct(state.shape, F32), jax.ShapeDtypeStruct((b, H_A, 1, HEAD_A), F32)],
        grid=(b // bt,),
        in_specs=[st_spec, pl.BlockSpec((bt, 6, H_A, 1, HEAD_A), lambda i: (i, 0, 0, 0, 0))],
        out_specs=[st_spec, pl.BlockSpec((bt, H_A, 1, HEAD_A), lambda i: (i, 0, 0, 0))],
        compiler_params=_params(("parallel",)),
        name="wkv_step_sample",
    )(state, rows)
    return sn, o.reshape(b, D_A)


def _post_kernel(o_ref, bonus_ref, g_ref, gng_ref, gnb_ref, hs_ref, hb_ref, y_ref):
    o = o_ref[...]
    hsum, hbc = hs_ref[...], hb_ref[...]
    mu = _head_sums(o, hsum, hbc) * (1.0 / HEAD_A)
    d = o - mu
    var = _head_sums(d * d, hsum, hbc) * (1.0 / HEAD_A)
    o_n = d * lax.rsqrt(var + EPS_GN) * gng_ref[...] + gnb_ref[...]
    y_ref[...] = ((o_n + bonus_ref[...]) * g_ref[...]).astype(y_ref.dtype)


def _rwkv_post(o, bonus, g, lw, tm, name):
    m = o.shape[0]
    big = pl.BlockSpec((tm, D_A), lambda i: (i, 0))
    vec = pl.BlockSpec((1, D_A), lambda i: (0, 0))
    return pl.pallas_call(
        _post_kernel,
        out_shape=jax.ShapeDtypeStruct((m, D_A), BF16),
        grid=(m // tm,),
        in_specs=[big, big, big, vec, vec, pl.BlockSpec((D_A, LANES), lambda i: (0, 0)),
                  pl.BlockSpec((LANES, D_A), lambda i: (0, 0))],
        out_specs=big,
        compiler_params=_params(("parallel",)),
        name=name,
    )(o, bonus, g, lw["gn_g"], lw["gn_b"], lw["hsum"], lw["hbc"])


def _attn_kernel(q_ref, kc_ref, vc_ref, kh_ref, vh_ref, bias_ref, o_ref, l_ref, *, dil, m_blocks, tiled):
    scale = HEAD_B ** -0.5
    n_streams = SUBLANES if tiled else dil
    n_units = n_streams * m_blocks

    stride = SUBLANES if tiled else dil
    if tiled:
        flat = lambda ref: ref.reshape(ref.shape[0] * SUBLANES, LANES)
        q_ref, kc_ref, vc_ref, kh_ref, vh_ref, o_ref, l_ref = map(flat, (q_ref, kc_ref, vc_ref, kh_ref, vh_ref,
                                                                          o_ref, l_ref))

    def rows(ref, r, start, size):
        if stride == 1:
            return ref[pl.ds(start, size), :]
        return ref[pl.ds(r + stride * start, size, stride=stride), :]

    def put(ref, r, start, val):
        if stride == 1:
            ref[pl.ds(start, BLK), :] = val
        else:
            ref[pl.ds(r + stride * start, BLK, stride=stride), :] = val

    qs, ks, vs, units = [], [], [], []
    for r in range(n_streams):
        for mb in range(m_blocks):
            units.append((r, mb))
            qs.append(rows(q_ref, r, BLK * mb, BLK))
            if mb == 0:
                ks.append(jnp.concatenate([rows(kh_ref, r, 0, BLK), rows(kc_ref, r, 0, BLK)], axis=0))
                vs.append(jnp.concatenate([rows(vh_ref, r, 0, BLK), rows(vc_ref, r, 0, BLK)], axis=0))
            else:
                ks.append(rows(kc_ref, r, BLK * (mb - 1), 2 * BLK))
                vs.append(rows(vc_ref, r, BLK * (mb - 1), 2 * BLK))

    def batch(xs, mul=None):
        pre = (lambda x: x) if mul is None else (lambda x: x * mul)
        return jnp.stack([pre(x[:, sub * HEAD_B:(sub + 1) * HEAD_B]).astype(BF16) for sub in range(2) for x in xs],
                         axis=0)

    assert math.frexp(scale)[0] == 0.5
    qb, kb, vb = batch(qs, scale), batch(ks), batch(vs)
    s = lax.dot_general(qb, kb, (((2,), (2,)), ((0,), (0,))), preferred_element_type=F32)
    s = s.reshape(2, n_units, BLK, 2 * BLK) + bias_ref[...][:, None]
    unit = lax.broadcasted_iota(jnp.int32, (1, n_units, 1, 2 * BLK), 1)
    col = lax.broadcasted_iota(jnp.int32, (1, n_units, 1, 2 * BLK), 3)
    no_prev = jnp.logical_and(jnp.logical_and(pl.program_id(0) == 0, (unit & (m_blocks - 1)) == 0), col < BLK)
    s = jnp.where(no_prev, -jnp.inf, s)
    m = jnp.max(s, axis=-1, keepdims=True)
    p = jnp.exp(s - m)
    l = jnp.sum(p, axis=-1, keepdims=True)
    pv = lax.dot_general(p.astype(BF16).reshape(2 * n_units, BLK, 2 * BLK), vb, (((2,), (1,)), ((0,), (0,))),
                         preferred_element_type=F32)
    o = pv.reshape(2, n_units, BLK, HEAD_B) / l
    lse = jnp.broadcast_to(m + jnp.log(l), (2, n_units, BLK, HEAD_B))
    for u, (r, mb) in enumerate(units):
        put(o_ref, r, BLK * mb, jnp.concatenate([o[0, u], o[1, u]], axis=1))
        put(l_ref, r, BLK * mb, jnp.concatenate([lse[0, u], lse[1, u]], axis=1))


def _attn_prompt(proj, bias, gi, rows_per_step):
    t, n_cols = proj.shape
    dil = DILATIONS[gi]
    span = BLK * dil
    tiled = dil % SUBLANES == 0
    if tiled:
        rows_per_step *= dil // SUBLANES
    m_blocks = rows_per_step // span
    cq = (COL_QKV + gi * D_G) // LANES
    ck = (COL_QKV + D_B + gi * D_G) // LANES
    cv = (COL_QKV + 2 * D_B + gi * D_G) // LANES
    bias_block = (2, BLK, 2 * BLK)
    if tiled:
        n_hi = dil // SUBLANES
        src = proj.reshape(t // dil, n_hi, SUBLANES, n_cols)
        grid = (t // rows_per_step, n_hi, H_G // 2)
        l_rows = rows_per_step // dil
        cur = lambda c0: pl.BlockSpec((l_rows, None, SUBLANES, LANES), lambda i, rh, hp: (i, rh, 0, c0 + hp))
        halo = lambda c0: pl.BlockSpec((BLK, None, SUBLANES, LANES),
                                       lambda i, rh, hp: (jnp.maximum(i * m_blocks - 1, 0), rh, 0, c0 + hp))
        out_spec = pl.BlockSpec((l_rows, None, SUBLANES, LANES), lambda i, rh, hp: (i, rh, 0, hp))
        out = jax.ShapeDtypeStruct((t // dil, n_hi, SUBLANES, D_G), F32)
        bias_spec = pl.BlockSpec(bias_block, lambda i, rh, hp: (hp, 0, 0))
    else:
        src = proj
        grid = (t // rows_per_step, H_G // 2)
        cur = lambda c0: pl.BlockSpec((rows_per_step, LANES), lambda i, hp: (i, c0 + hp))
        halo = lambda c0: pl.BlockSpec((span, LANES), lambda i, hp: (jnp.maximum(i * m_blocks - 1, 0), c0 + hp))
        out_spec = pl.BlockSpec((rows_per_step, LANES), lambda i, hp: (i, hp))
        out = jax.ShapeDtypeStruct((t, D_G), F32)
        bias_spec = pl.BlockSpec(bias_block, lambda i, hp: (hp, 0, 0))
    o, l = pl.pallas_call(
        functools.partial(_attn_kernel, dil=dil, m_blocks=m_blocks, tiled=tiled),
        out_shape=[out, out],
        grid=grid,
        in_specs=[cur(cq), cur(ck), cur(cv), halo(ck), halo(cv), bias_spec],
        out_specs=[out_spec, out_spec],
        compiler_params=_params(("parallel",) * len(grid)),
        name=f"attn_prompt_g{gi}",
    )(src, src, src, src, src, bias)
    return o.reshape(t, D_G), l.reshape(t, D_G)


def _attn_step_kernel(q_ref, kn_ref, vn_ref, c_ref, bias_ref, bias0_ref, o_ref, l_ref):
    scale = HEAD_B ** -0.5
    ri = lax.broadcasted_iota(jnp.int32, (1, 1, HEAD_B, HEAD_B), 2)
    ci = lax.broadcasted_iota(jnp.int32, (1, 1, HEAD_B, HEAD_B), 3)
    eye = ri == ci
    q = q_ref[...]
    q_col = jnp.sum(jnp.where(eye, q, 0.0), axis=3, keepdims=True)
    s = jnp.sum(c_ref[:, 0] * q_col, axis=2, keepdims=True) * scale + bias_ref[...]
    s_new = jnp.sum(q * kn_ref[...], axis=3, keepdims=True) * scale + bias0_ref[...]
    m = jnp.maximum(jnp.max(s, axis=3, keepdims=True), s_new)
    p = jnp.exp(s - m)
    p_new = jnp.exp(s_new - m)
    l = jnp.sum(p, axis=3, keepdims=True) + p_new
    o_col = jnp.sum(c_ref[:, 1] * p, axis=3, keepdims=True)
    o_row = jnp.sum(jnp.where(eye, o_col, 0.0), axis=2, keepdims=True)
    o_ref[...] = (o_row + p_new * vn_ref[...]) / l
    l_ref[...] = jnp.broadcast_to(m + jnp.log(l), o_ref.shape)


def _attn_step(q, k_new, v_new, cache, bias, bias0, gi):
    b = q.shape[0]
    w = cache.shape[2]
    bt = max(1, min(b, STEP_CACHE_WORDS // (2 * D_G * w)))
    cache_t = jnp.transpose(cache, (0, 1, 3, 4, 5, 2))
    vec = pl.BlockSpec((bt, H_G, 1, HEAD_B), lambda i: (i, 0, 0, 0))
    out = jax.ShapeDtypeStruct((b, H_G, 1, HEAD_B), F32)
    o, l = pl.pallas_call(
        _attn_step_kernel,
        out_shape=[out, out],
        grid=(b // bt,),
        in_specs=[vec, vec, vec,
                  pl.BlockSpec((None, bt, 2, H_G, HEAD_B, w), lambda i: (0, i, 0, 0, 0, 0)),
                  pl.BlockSpec((H_G, 1, w), lambda i: (0, 0, 0)),
                  pl.BlockSpec((H_G, 1, 1), lambda i: (0, 0, 0))],
        out_specs=[vec, vec],
        compiler_params=_params(("parallel",)),
        name=f"attn_step_g{gi}",
    )(q, k_new, v_new, cache_t, bias, bias0)
    return o.reshape(b, D_G), l.reshape(b, D_G)


def _mix_kernel(ya_ref, o1_ref, o2_ref, o3_ref, l1_ref, l2_ref, l3_ref, wa_ref, wb_ref, ga_ref, gb_ref, o_ref):
    l1, l2, l3 = l1_ref[...], l2_ref[...], l3_ref[...]
    m = jnp.maximum(jnp.maximum(l1, l2), l3)
    e1, e2, e3 = jnp.exp(l1 - m), jnp.exp(l2 - m), jnp.exp(l3 - m)
    den = e1 + e2 + e3
    yb = ((e1 / den) * o1_ref[...] + (e2 / den) * o2_ref[...] + (e3 / den) * o3_ref[...]).astype(BF16)
    mixed = ga_ref[...] * _dot_bf16(ya_ref[...], wa_ref[...]) + gb_ref[...] * _dot_bf16(yb, wb_ref[...])
    o_ref[...] = mixed.astype(o_ref.dtype)


def _mix(ya, outs, lses, wa, wb, proj, tm, name):
    m = ya.shape[0]
    n = D_MODEL
    assert COL_GATE % n == 0
    ga0 = COL_GATE // n
    grp = pl.BlockSpec((tm, D_G), lambda i: (i, 0))
    resident = lambda rows: pl.BlockSpec((rows, n), lambda i: (0, 0), pipeline_mode=pl.Buffered(1))
    return pl.pallas_call(
        _mix_kernel,
        out_shape=jax.ShapeDtypeStruct((m, n), BF16),
        grid=(m // tm,),
        in_specs=[pl.BlockSpec((tm, D_A), lambda i: (i, 0))] + [grp] * 6 + [
            resident(D_A), resident(D_G),
            pl.BlockSpec((tm, n), lambda i: (i, ga0)),
            pl.BlockSpec((tm, n), lambda i: (i, ga0 + 1)),
        ],
        out_specs=pl.BlockSpec((tm, n), lambda i: (i, 0)),
        compiler_params=_params(("parallel",)),
        name=name,
    )(ya, *outs, *lses, wa, wb, proj, proj)


def _wo_kernel(a_ref, w_ref, x_ref, g_ref, h_ref, hn_ref):
    h = x_ref[...] + _dot_bf16(a_ref[...], w_ref[...])
    h_ref[...] = h
    hn_ref[...] = _rms(h, g_ref[...]).astype(BF16)


def _wo(a, w, x, g, tm, name):
    m, k = a.shape
    n = w.shape[1]
    row = pl.BlockSpec((tm, n), lambda i: (i, 0))
    return pl.pallas_call(
        _wo_kernel,
        out_shape=[jax.ShapeDtypeStruct((m, n), F32), jax.ShapeDtypeStruct((m, n), BF16)],
        grid=(m // tm,),
        in_specs=[pl.BlockSpec((tm, k), lambda i: (i, 0)), pl.BlockSpec((k, n), lambda i: (0, 0)), row,
                  pl.BlockSpec((1, n), lambda i: (0, 0))],
        out_specs=[row, row],
        compiler_params=_params(("parallel",)),
        name=name,
    )(a, w, x, g.reshape(1, n))


def _gelu(x):
    return 0.5 * x * (1.0 + lax.erf(x * (1.0 / math.sqrt(2.0))))


def _ffn_up_kernel(h_ref, w1_ref, w2_ref, cw1_ref, cw2_ref, cb1_ref, cb2_ref, act_ref, t1_ref, t2_ref,
                   c1_scr, c2_scr, w1_scr, w2_scr):
    tm = h_ref.shape[0]

    @pl.when(pl.program_id(1) == 0)
    def _():
        c1_scr[...] = jnp.zeros_like(c1_scr)
        c2_scr[...] = jnp.zeros_like(c2_scr)
        w1_scr[...] = w1_ref[...].astype(BF16)
        w2_scr[...] = w2_ref[...].astype(BF16)

    a = h_ref[...]
    row = lax.broadcasted_iota(jnp.int32, (tm, 1), 0)

    def conv(u, carry_ref, cw_ref, cb_ref):
        m1 = jnp.where(row == 0, carry_ref[7:8, :], pltpu.roll(u, 1, axis=0))
        m2 = jnp.where(row == 0, carry_ref[6:7, :], jnp.where(row == 1, carry_ref[7:8, :], pltpu.roll(u, 2, axis=0)))
        return cb_ref[...] + cw_ref[0:1, :] * m2 + cw_ref[1:2, :] * m1 + cw_ref[2:3, :] * u

    u1 = _dot_bf16(a, w1_scr[...])
    u2 = _dot_bf16(a, w2_scr[...])
    c1 = conv(u1, c1_scr, cw1_ref, cb1_ref)
    c2 = conv(u2, c2_scr, cw2_ref, cb2_ref)
    act_ref[...] = (_gelu(c1) * c2).astype(act_ref.dtype)
    c1_scr[...] = u1[tm - 8:tm]
    c2_scr[...] = u2[tm - 8:tm]
    t1_ref[...] = u1[tm - 8:tm]
    t2_ref[...] = u2[tm - 8:tm]


def _ffn_up_prompt(hn, w_up, conv_w, conv_b, tm, tn):
    m = hn.shape[0]
    nj = D_FF // tn
    tail = jax.ShapeDtypeStruct((8, D_FF), F32)
    return pl.pallas_call(
        _ffn_up_kernel,
        out_shape=[jax.ShapeDtypeStruct((m, D_FF), BF16), tail, tail],
        grid=(nj, m // tm),
        in_specs=[
            pl.BlockSpec((tm, D_MODEL), lambda j, i: (i, 0)),
            pl.BlockSpec((D_MODEL, tn), lambda j, i: (0, j)),
            pl.BlockSpec((D_MODEL, tn), lambda j, i: (0, nj + j)),
            pl.BlockSpec((CONV_W, tn), lambda j, i: (0, j)),
            pl.BlockSpec((CONV_W, tn), lambda j, i: (0, nj + j)),
            pl.BlockSpec((1, tn), lambda j, i: (0, j)),
            pl.BlockSpec((1, tn), lambda j, i: (0, nj + j)),
        ],
        out_specs=[pl.BlockSpec((tm, tn), lambda j, i: (i, j)),
                   pl.BlockSpec((8, tn), lambda j, i: (0, j)),
                   pl.BlockSpec((8, tn), lambda j, i: (0, j))],
        scratch_shapes=[pltpu.VMEM((8, tn), F32), pltpu.VMEM((8, tn), F32),
                        pltpu.VMEM((D_MODEL, tn), BF16), pltpu.VMEM((D_MODEL, tn), BF16)],
        compiler_params=_params(("arbitrary", "arbitrary")),
        name="ffn_up_prompt",
    )(hn, w_up, w_up, conv_w, conv_w, conv_b, conv_b)


def _ffn_act_step_kernel(up_ref, prev_ref, cw_ref, cb_ref, act_ref):
    up = up_ref[...]
    w = 2 * D_FF
    c = cb_ref[...] + cw_ref[0:1, :] * prev_ref[:, 0:w] + cw_ref[1:2, :] * prev_ref[:, w:2 * w] + cw_ref[2:3, :] * up
    act_ref[...] = (_gelu(c[:, 0:D_FF]) * c[:, D_FF:w]).astype(act_ref.dtype)


def _ffn_act_step(up, conv_prev, conv_w, conv_b):
    b = up.shape[0]
    return pl.pallas_call(
        _ffn_act_step_kernel,
        out_shape=jax.ShapeDtypeStruct((b, D_FF), BF16),
        compiler_params=pltpu.CompilerParams(vmem_limit_bytes=VMEM_LIMIT),
        name="ffn_act_sample",
    )(up, conv_prev.reshape(b, (CONV_W - 1) * 2 * D_FF), conv_w, conv_b)


def _down_kernel(a_ref, w_ref, h_ref, g_ref, o_ref, acc_ref):
    @pl.when(pl.program_id(1) == 0)
    def _():
        acc_ref[...] = h_ref[...]

    out = acc_ref[...] + _dot_bf16(a_ref[...], w_ref[...])
    acc_ref[...] = out
    inv = lax.rsqrt(jnp.mean(out * out, axis=-1, keepdims=True) + EPS_RMS)
    o_ref[...] = out * inv * g_ref[...]


def _ffn_down(act, w_down, h, g, tm, tk, name):
    m = act.shape[0]
    nk = D_FF // tk
    return pl.pallas_call(
        _down_kernel,
        out_shape=jax.ShapeDtypeStruct((m, D_MODEL), F32),
        grid=(m // tm, nk),
        in_specs=[
            pl.BlockSpec((tm, tk), lambda i, k: (i, k)),
            pl.BlockSpec((tk, D_MODEL), lambda i, k: (k, 0)),
            pl.BlockSpec((tm, D_MODEL), lambda i, k: (i, 0)),
            pl.BlockSpec((1, D_MODEL), lambda i, k: (0, 0)),
        ],
        out_specs=pl.BlockSpec((tm, D_MODEL), lambda i, k: (i, 0)),
        scratch_shapes=[pltpu.VMEM((tm, D_MODEL), F32)],
        compiler_params=_params(("parallel", "arbitrary")),
        name=name,
    )(act, w_down, h, g.reshape(1, D_MODEL))


def _rel_bucket(dist):
    max_exact = N_BUCKETS // 2
    d_f = jnp.maximum(dist, 1).astype(F32)
    large = max_exact + (jnp.log(d_f / max_exact) / math.log(MAX_DISTANCE / max_exact)
                         * (N_BUCKETS - max_exact)).astype(jnp.int32)
    large = jnp.minimum(large, N_BUCKETS - 1)
    return jnp.where(dist < max_exact, dist, large)


def _bias_rows(tab, dist):
    onehot = (_rel_bucket(dist)[None, :] == jnp.arange(N_BUCKETS)[:, None]).astype(F32)
    return jnp.dot(tab.T, onehot, precision=HIGHEST)


def _bias_tables(rel_bias, gi):
    dil = DILATIONS[gi]
    win = WINDOWS[gi]
    reach = win // dil
    tab = rel_bias[:, gi * H_G:(gi + 1) * H_G]
    qi = jnp.arange(BLK)[:, None]
    ki = jnp.arange(2 * BLK)[None, :]
    rel = qi + BLK - ki
    blk = _bias_rows(tab, (dil * jnp.maximum(rel, 0)).reshape(-1)).reshape(H_G, BLK, 2 * BLK)
    blk = jnp.where(((rel >= 0) & (rel <= reach))[None], blk, -jnp.inf)
    back = win - jnp.arange(win)
    step = jnp.where((back % dil == 0)[None, :], _bias_rows(tab, back), -jnp.inf)[:, None, :]
    step0 = _bias_rows(tab, jnp.zeros((1,), jnp.int32))[:, None, :]
    return blk, step, step0


def _tiles(m, prompt):
    if prompt:
        return dict(tag="prompt", proj_m=1024, prep_m=2 * CHUNK, attn_rows=2048, mix_m=512, wo_m=512, up_m=1024,
                    up_n=512, down_m=512, down_k=D_FF // 2)
    return dict(tag="sample", proj_m=m, prep_m=m, step_b=4, post_m=m, mix_m=m, wo_m=m, up_n=512, down_m=m,
                down_k=D_FF // 2)


def _layer(x, lw, bias_blk, prompt, state=None, proj=None):
    m = x.shape[0]
    tiles = _tiles(m, prompt)
    if proj is None:
        proj = _proj(x, lw["norm1_g"], lw["w_in"], lw["in_bias"], tiles["proj_m"], "proj_" + tiles["tag"])

    if prompt:
        feats, gc = _rwkv_prep(proj, None, tiles["prep_m"], CHUNK, lw, "rwkv_prep_prompt")
        ya, wkv_new = _wkv_scan(feats, gc, lw)
        wkv_new = wkv_new[None, None]
    else:
        feats, gc = _rwkv_prep(proj, state["shift"], tiles["prep_m"], 1, lw, "rwkv_prep_sample")
        f = lambda n: feats[:, n * D_A:(n + 1) * D_A]
        wkv_new, o = _wkv_step(state["wkv"], f(F_PT), f(F_RT), f(F_QT), f(F_KT), f(F_V), gc, tiles["step_b"])
        ya = _rwkv_post(o, f(F_BONUS), f(F_G), lw, tiles["post_m"], "rwkv_post_" + tiles["tag"])

    outs, lses = [], []
    for gi in range(N_GROUPS):
        if prompt:
            o_g, l_g = _attn_prompt(proj, bias_blk[gi][0], gi, tiles["attn_rows"])
        else:
            sl = lambda c0: proj[:, c0 + gi * D_G:c0 + (gi + 1) * D_G].reshape(m, H_G, 1, HEAD_B)
            o_g, l_g = _attn_step(sl(COL_QKV), sl(COL_QKV + D_B), sl(COL_QKV + 2 * D_B), state["win"][gi],
                                  bias_blk[gi][1], bias_blk[gi][2], gi)
        outs.append(o_g)
        lses.append(l_g)
    mixed = _mix(ya, outs, lses, lw["w_out_a"], lw["w_out_b"], proj, tiles["mix_m"], "mix_" + tiles["tag"])
    h, hn = _wo(mixed, lw["w_o"], x, lw["norm2_g"], tiles["wo_m"], "wo_" + tiles["tag"])

    if prompt:
        act, t1, t2 = _ffn_up_prompt(hn, lw["w_up"], lw["conv_w"], lw["conv_b"], tiles["up_m"], tiles["up_n"])
        conv_new = jnp.concatenate([t1[6:8], t2[6:8]], axis=1)[None, None]
    else:
        up = _matmul(hn, lw["w_up"], m, tiles["up_n"], "up_sample")
        act = _ffn_act_step(up, state["conv"], lw["conv_w"], lw["conv_b"])
        conv_new = jnp.concatenate([state["conv"][:, 1:], up[:, None, :]], axis=1)[None]
    y = _ffn_down(act, lw["w_down"], h, lw["normf_g"], tiles["down_m"], tiles["down_k"], "down_" + tiles["tag"])
    return y, proj, wkv_new, conv_new


def kernel(x_prompt, x_sample, state_wkv, state_shift, state_ffn_conv, cache_win1, cache_win2, cache_win3, rel_bias,
           norm1_g, w_in, gate_b, mu_shift, w0, w_up_decay, a0, w_up_aaa, w_up_gate, k_k, k_a, r_k, gn_g, gn_b,
           w_out_a, w_out_b, w_o, norm2_g, w_up, conv_w, conv_b, w_down, normf_g):
    row = lambda a: a.reshape(1, -1)
    w_lora = jnp.zeros((D_LORA, 3 * D_A), F32)
    w_lora = w_lora.at[0:D_DECAY_LORA, 0:D_A].set(w_up_decay[0])
    w_lora = w_lora.at[D_DECAY_LORA:D_DECAY_LORA + D_AAA_LORA, D_A:2 * D_A].set(w_up_aaa[0])
    w_lora = w_lora.at[D_DECAY_LORA + D_AAA_LORA:, 2 * D_A:].set(w_up_gate[0])
    head = np.arange(D_A) // HEAD_A
    lw = dict(
        norm1_g=norm1_g[0], norm2_g=norm2_g[0], normf_g=normf_g,
        in_bias=jnp.concatenate([jnp.zeros((1, COL_GATE), F32), row(gate_b[0])], axis=1),
        mu=row(mu_shift[0]), w_lora=jnp.stack(_split2(w_lora)), w0=row(w0[0]), a0=row(a0[0]), k_k=row(k_k[0]), k_a=row(k_a[0]),
        r_k=row(r_k[0]), gn_g=row(gn_g[0]), gn_b=row(gn_b[0]),
        hsum=jnp.asarray(head[:, None] == np.arange(LANES)[None, :], BF16),
        hbc=jnp.asarray(np.arange(LANES)[:, None] == head[None, :], BF16),
        w_out_a=w_out_a[0].astype(BF16), w_out_b=w_out_b[0].astype(BF16), w_o=w_o[0].astype(BF16),
        w_up=w_up[0], conv_w=conv_w[0], conv_b=row(conv_b[0]), w_down=w_down[0].astype(BF16),
    )
    bias = [_bias_tables(rel_bias, gi) for gi in range(N_GROUPS)]

    proj_s, lw["w_in"] = _proj_cast(x_sample[:, 0], norm1_g[0], w_in[0], lw["in_bias"])
    y_p, proj_p, wkv_p, conv_p = _layer(x_prompt[0], lw, bias, prompt=True)

    b = DEC_BATCH
    state = dict(wkv=state_wkv, shift=state_shift.reshape(b, D_SHIFT), conv=state_ffn_conv[0],
                 win=(cache_win1, cache_win2, cache_win3))
    y_s, proj_s, wkv_s, conv_s = _layer(x_sample[:, 0], lw, bias, prompt=False, state=state, proj=proj_s)

    def kv_rows(proj, lo, gi):
        k = proj[lo:, COL_QKV + D_B + gi * D_G:COL_QKV + D_B + (gi + 1) * D_G]
        v = proj[lo:, COL_QKV + 2 * D_B + gi * D_G:COL_QKV + 2 * D_B + (gi + 1) * D_G]
        n = k.shape[0]
        return jnp.stack([k.reshape(n, H_G, HEAD_B), v.reshape(n, H_G, HEAD_B)], axis=1)

    win_p = [kv_rows(proj_p, SEQ - min(WINDOWS[gi], SEQ), gi)[None, None] for gi in range(N_GROUPS)]
    win_s = [kv_rows(proj_s, 0, gi)[None, :, None] for gi in range(N_GROUPS)]
    return (y_p[None], y_s[:, None],
            wkv_p, wkv_s,
            proj_p[SEQ - 1:, 0:D_SHIFT][None, None], proj_s[:, 0:D_SHIFT][None, :, None],
            conv_p, conv_s,
            win_p[0], win_s[0], win_p[1], win_s[1], win_p[2], win_s[2])
```

```python
import functools
import math

import numpy as np
import jax
import jax.numpy as jnp
from jax import lax
from jax.experimental import pallas as pl
from jax.experimental.pallas import tpu as pltpu

F32 = jnp.float32
BF16 = jnp.bfloat16
HIGHEST = lax.Precision.HIGHEST

D_MODEL = 2048
SEQ = 8192
DEC_BATCH = 32
HEAD_A = 64
H_A = 16
D_A = H_A * HEAD_A
D_DECAY_LORA = 96
D_AAA_LORA = 96
D_GATE_LORA = 64
D_LORA = D_DECAY_LORA + D_AAA_LORA + D_GATE_LORA
D_SHIFT = 3 * D_A + D_LORA
EPS_GN = 64e-5
HEAD_B = 64
H_G = 8
D_G = H_G * HEAD_B
WINDOWS = (128, 512, 2048)
DILATIONS = (1, 4, 16)
N_GROUPS = 3
D_B = N_GROUPS * D_G
BLK = 128
N_BUCKETS = 32
MAX_DISTANCE = 2048
COL_QKV = D_SHIFT
COL_GATE_SRC = D_SHIFT + 3 * D_B
D_IN = COL_GATE_SRC + 2 * D_MODEL
PROJ_TN = 1024
COL_GATE = -(-COL_GATE_SRC // PROJ_TN) * PROJ_TN
D_IN_PAD = COL_GATE + 2 * D_MODEL
D_FF = 5632
CONV_W = 3
EPS_RMS = 1e-6
CHUNK = 64
SCAN_CHUNKS = 2
F_RT, F_KT, F_QT, F_PT, F_V, F_BONUS, F_G = range(7)
N_FEATS = 7
LANES = 128
SUBLANES = 8
VMEM_LIMIT = 56 * 1024 * 1024
STEP_CACHE_WORDS = 2 * 1024 * 1024


def _params(sem, vmem=VMEM_LIMIT):
    return pltpu.CompilerParams(dimension_semantics=sem, vmem_limit_bytes=vmem)


def _sigmoid(x):
    return 1.0 / (1.0 + jnp.exp(-x))


def _dot_bf16(a, b):
    return jnp.dot(a, b, preferred_element_type=F32)


def _split2(x):
    hi = x.astype(BF16)
    return hi, (x - hi.astype(F32)).astype(BF16)


def _split3(x):
    hi = x.astype(BF16)
    r1 = x - hi.astype(F32)
    mid = r1.astype(BF16)
    lo = (r1 - mid.astype(F32)).astype(BF16)
    return hi, mid, lo


def _dot_exact_rhs(x, m):
    hi, mid, lo = _split3(x)
    return _dot_bf16(hi, m) + _dot_bf16(mid, m) + _dot_bf16(lo, m)


def _head_sums(x, hsum, hbc):
    return _dot_exact_rhs(_dot_exact_rhs(x, hsum), hbc)


def _dot_exact_lhs(m, x):
    hi, mid, lo = _split3(x)
    return _dot_bf16(m, hi) + _dot_bf16(m, mid) + _dot_bf16(m, lo)


def _rms(x, g):
    return x * lax.rsqrt(jnp.mean(x * x, axis=-1, keepdims=True) + EPS_RMS) * g


def _proj_cast_kernel(x_ref, g_ref, w_ref, b_ref, o_ref, wb_ref, xn_scr, *, gap_block):
    j = pl.program_id(0)

    @pl.when(j == 0)
    def _():
        xn_scr[...] = _rms(x_ref[...], g_ref[...]).astype(BF16)

    wb = jnp.where(j != gap_block, w_ref[...], 0.0).astype(BF16)
    wb_ref[...] = wb
    acc = _dot_bf16(xn_scr[...], wb)
    tn = o_ref.shape[1]
    col = lax.broadcasted_iota(jnp.int32, (1, tn), 1) + j * tn
    o_ref[...] = jnp.where(col >= COL_GATE, _sigmoid(acc + b_ref[...]), acc)


def _proj_cast(x, g, w, bias):
    m, k = x.shape
    tn = COL_GATE - COL_GATE_SRC
    assert tn % LANES == 0 and COL_GATE_SRC % tn == 0
    gap_block = COL_GATE_SRC // tn
    return pl.pallas_call(
        functools.partial(_proj_cast_kernel, gap_block=gap_block),
        out_shape=[jax.ShapeDtypeStruct((m, D_IN_PAD), F32), jax.ShapeDtypeStruct((k, D_IN_PAD), BF16)],
        grid=(D_IN_PAD // tn,),
        in_specs=[
            pl.BlockSpec((m, k), lambda j: (0, 0)),
            pl.BlockSpec((1, k), lambda j: (0, 0)),
            pl.BlockSpec((k, tn), lambda j: (0, jnp.where(j < gap_block, j, jnp.maximum(j - 1, 0)))),
            pl.BlockSpec((1, tn), lambda j: (0, j)),
        ],
        out_specs=[pl.BlockSpec((m, tn), lambda j: (0, j)), pl.BlockSpec((k, tn), lambda j: (0, j))],
        scratch_shapes=[pltpu.VMEM((m, k), BF16)],
        compiler_params=_params(("arbitrary",)),
        name="proj_sample_cast_w_in",
    )(x, g.reshape(1, k), w, bias)


def _proj_kernel(x_ref, g_ref, w_ref, b_ref, o_ref, xn_scr):
    @pl.when(pl.program_id(1) == 0)
    def _():
        xn_scr[...] = _rms(x_ref[...], g_ref[...]).astype(BF16)

    acc = _dot_bf16(xn_scr[...], w_ref[...])
    tn = o_ref.shape[1]
    col = lax.broadcasted_iota(jnp.int32, (1, tn), 1) + pl.program_id(1) * tn
    o_ref[...] = jnp.where(col >= COL_GATE, _sigmoid(acc + b_ref[...]), acc)


def _proj(x, g, w, bias, tm, name):
    m, k = x.shape
    n = w.shape[1]
    return pl.pallas_call(
        _proj_kernel,
        out_shape=jax.ShapeDtypeStruct((m, n), F32),
        grid=(m // tm, n // PROJ_TN),
        in_specs=[
            pl.BlockSpec((tm, k), lambda i, j: (i, 0)),
            pl.BlockSpec((1, k), lambda i, j: (0, 0)),
            pl.BlockSpec((k, PROJ_TN), lambda i, j: (0, j)),
            pl.BlockSpec((1, PROJ_TN), lambda i, j: (0, j)),
        ],
        out_specs=pl.BlockSpec((tm, PROJ_TN), lambda i, j: (i, j)),
        scratch_shapes=[pltpu.VMEM((tm, k), BF16)],
        compiler_params=_params(("parallel", "arbitrary")),
        name=name,
    )(x, g.reshape(1, k), w, bias)


def _matmul_kernel(a_ref, w_ref, o_ref):
    o_ref[...] = _dot_bf16(a_ref[...], w_ref[...].astype(BF16))


def _matmul(a, w, tm, tn, name):
    m, k = a.shape
    n = w.shape[1]
    return pl.pallas_call(
        _matmul_kernel,
        out_shape=jax.ShapeDtypeStruct((m, n), F32),
        grid=(m // tm, n // tn),
        in_specs=[pl.BlockSpec((tm, k), lambda i, j: (i, 0)), pl.BlockSpec((k, tn), lambda i, j: (0, j))],
        out_specs=pl.BlockSpec((tm, tn), lambda i, j: (i, j)),
        compiler_params=_params(("parallel", "arbitrary")),
        name=name,
    )(a, w)


def _prep_kernel(p_ref, prev_ref, mu_ref, wl_ref, w0_ref, a0_ref, kk_ref, ka_ref, rk_ref, lmat_ref, sel_ref,
                 hs_ref, hb_ref, f_ref, gc_ref, *, rows_are_time):
    p = p_ref[...]
    tm = p.shape[0]
    if rows_are_time:
        last = jnp.where(pl.program_id(0) == 0, 0.0, prev_ref[7:8, :])
        row = lax.broadcasted_iota(jnp.int32, (tm, 1), 0)
        prev = jnp.where(row == 0, last, pltpu.roll(p, 1, axis=0))
    else:
        prev = prev_ref[...]
    xm = p + mu_ref[...] * (prev - p)
    r = xm[:, 0:D_A]
    k = xm[:, D_A:2 * D_A]
    v = xm[:, 2 * D_A:3 * D_A]
    xl = xm[:, 3 * D_A:D_SHIFT]
    lane = lax.broadcasted_iota(jnp.int32, xl.shape, 1)
    act = jnp.where(lane < D_DECAY_LORA, jnp.tanh(xl),
                    jnp.where(lane < D_DECAY_LORA + D_AAA_LORA, xl, _sigmoid(xl)))
    act_hi, act_lo = _split2(act)
    w_hi, w_lo = wl_ref[0], wl_ref[1]
    lora = _dot_bf16(act_hi, w_hi) + _dot_bf16(act_hi, w_lo) + _dot_bf16(act_lo, w_hi)
    y = -(w0_ref[...] + lora[:, 0:D_A])
    softplus = jnp.maximum(y, 0.0) + jnp.log(1.0 + jnp.exp(-jnp.abs(y)))
    logw = -jnp.exp(-softplus - 0.5)
    a = _sigmoid(a0_ref[...] + lora[:, D_A:2 * D_A])
    g = lora[:, 2 * D_A:3 * D_A]
    kkr = k * kk_ref[...]
    kp = k * (1.0 + (a - 1.0) * ka_ref[...])
    seg = _head_sums(jnp.concatenate([kkr * kkr, r * kp * rk_ref[...]], axis=0), hs_ref[...], hb_ref[...])
    kk = kkr / jnp.maximum(jnp.sqrt(seg[0:tm]), 1e-12)
    cum = _dot_exact_lhs(lmat_ref[...], logw)
    e_out = jnp.exp(-cum)
    feats = {F_RT: r * jnp.exp(cum), F_KT: kp * e_out, F_QT: kk * a * e_out, F_PT: -kk * jnp.exp(cum - logw),
             F_V: v, F_BONUS: seg[tm:2 * tm] * v, F_G: g}
    for n, val in feats.items():
        f_ref[:, n * D_A:(n + 1) * D_A] = val
    gc_ref[...] = jnp.exp(_dot_exact_lhs(sel_ref[...], cum))


def _rwkv_prep(proj, prev, tm, chunk, lw, name):
    m = proj.shape[0]
    rows_are_time = prev is None
    n_tiles = m // tm
    t = np.arange(tm)
    lmat = ((t[:, None] // chunk == t[None, :] // chunk) & (t[None, :] <= t[:, None])).astype(np.float32)
    sel_stride = 1 if chunk == 1 else 8
    n_sel = sel_stride * (tm // chunk)
    sel = np.zeros((n_sel, tm), np.float32)
    for c in range(tm // chunk):
        sel[sel_stride * c, (c + 1) * chunk - 1] = 1.0
    if rows_are_time:
        prev_arr = proj
        prev_spec = pl.BlockSpec((8, D_SHIFT), lambda i: (jnp.maximum(i * (tm // 8) - 1, 0), 0))
    else:
        prev_arr = prev
        prev_spec = pl.BlockSpec((tm, D_SHIFT), lambda i: (i, 0))
    vec = lambda d: pl.BlockSpec((1, d), lambda i: (0, 0))
    full = lambda a: pl.BlockSpec(a.shape, lambda i: (0, 0))
    lmat_b = jnp.asarray(lmat, BF16)
    sel_b = jnp.asarray(sel, BF16)
    return pl.pallas_call(
        functools.partial(_prep_kernel, rows_are_time=rows_are_time),
        out_shape=[jax.ShapeDtypeStruct((m, N_FEATS * D_A), F32),
                   jax.ShapeDtypeStruct((n_tiles * n_sel, D_A), F32)],
        grid=(n_tiles,),
        in_specs=[pl.BlockSpec((tm, D_SHIFT), lambda i: (i, 0)), prev_spec, vec(D_SHIFT),
                  pl.BlockSpec(lw["w_lora"].shape, lambda i: (0, 0, 0)),
                  vec(D_A), vec(D_A), vec(D_A), vec(D_A), vec(D_A), full(lmat_b), full(sel_b), full(lw["hsum"]),
                  full(lw["hbc"])],
        out_specs=[pl.BlockSpec((tm, N_FEATS * D_A), lambda i: (i, 0)), pl.BlockSpec((n_sel, D_A), lambda i: (i, 0))],
        compiler_params=_params(("parallel",)),
        name=name,
    )(proj, prev_arr, lw["mu"], lw["w_lora"], lw["w0"], lw["a0"], lw["k_k"], lw["k_a"], lw["r_k"], lmat_b, sel_b,
      lw["hsum"], lw["hbc"])


_NN = (((2,), (1,)), ((0,), (0,)))
_NT = (((2,), (2,)), ((0,), (0,)))
PAIRS = H_A // 2
LOG2_HEAD = HEAD_A.bit_length() - 1
assert HEAD_A == 1 << LOG2_HEAD and 2 * HEAD_A == LANES


def _lane_head(n):
    return (lax.broadcasted_iota(jnp.int32, (1, 1, n), 2) >> LOG2_HEAD) & 1


def _keep_head(x, head, which):
    return jnp.where(head == which, x, jnp.zeros_like(x))


def _pdot_nn(a, b):
    n = b.shape[2]
    ah, al = _split2(a)
    bh, bl = _split2(b)
    head = _lane_head(n)
    bd = lambda x: jnp.concatenate([_keep_head(x, head, 0), _keep_head(x, head, 1)], axis=1)
    bd_hi = bd(bh)
    a_cat = jnp.concatenate([ah, al], axis=2)
    b_cat = jnp.concatenate([jnp.concatenate([bd_hi, bd(bl)], axis=2),
                             jnp.concatenate([bd_hi, jnp.zeros_like(bd_hi)], axis=2)], axis=1)
    res = lax.dot_general(a_cat, b_cat, _NN, preferred_element_type=F32)
    return res[:, :, 0:n] + res[:, :, n:]


def _pdot_nt(a, bs):
    ah, al = _split2(a)
    head = _lane_head(LANES)
    hi, lo = [], []
    for b in bs:
        bh, bl = _split2(b)
        for which in (0, 1):
            hi.append(_keep_head(bh, head, which))
            lo.append(_keep_head(bl, head, which))
    bh, bl = jnp.concatenate(hi, axis=1), jnp.concatenate(lo, axis=1)
    n_rows = bh.shape[1]
    a_cat = jnp.concatenate([ah, al], axis=2)
    b_cat = jnp.concatenate([jnp.concatenate([bh, bh], axis=2),
                             jnp.concatenate([bl, jnp.zeros_like(bl)], axis=2)], axis=1)
    res = lax.dot_general(a_cat, b_cat, _NT, preferred_element_type=F32)
    return res[:, :, 0:n_rows] + res[:, :, n_rows:]


def _pdot_tn(x, y):
    nx = x.shape[2]
    xh, xl = _split2(jnp.swapaxes(x, 1, 2))
    yh, yl = _split2(y)
    x_cat = jnp.concatenate([xh, xh, xl, jnp.zeros_like(xl)], axis=2)
    y_cat = jnp.concatenate([yh, yl, yh, jnp.zeros_like(yl)], axis=1)
    tot = lax.dot_general(x_cat, y_cat, _NN, preferred_element_type=F32)
    head = _lane_head(LANES)
    return [jnp.where(head == 0, tot[:, g:g + HEAD_A], tot[:, g + HEAD_A:g + LANES]) for g in range(0, nx, LANES)]


def _scan_kernel(f_ref, gc_ref, gng_ref, gnb_ref, ya_ref, sfin_ref, s_scr):
    @pl.when(pl.program_id(0) == 0)
    def _():
        s_scr[...] = jnp.zeros_like(s_scr)

    def pairs(x, rows):
        return jnp.stack([x[c * rows:(c + 1) * rows, i * LANES:(i + 1) * LANES]
                          for c in range(SCAN_CHUNKS) for i in range(PAIRS)], axis=0)

    feat = lambda n: f_ref[:, n * D_A:(n + 1) * D_A]
    p, r, q, k, v = (pairs(feat(n), CHUNK) for n in (F_PT, F_RT, F_QT, F_KT, F_V))
    gam = pairs(gc_ref[...], SUBLANES)[:, 0:1]
    ri = lax.broadcasted_iota(jnp.int32, (1, CHUNK, LANES), 1)
    ci = lax.broadcasted_iota(jnp.int32, (1, CHUNK, LANES), 2) & (HEAD_A - 1)
    strict = ri > ci
    incl = ri >= ci
    eye = (ri == ci).astype(F32)

    gram = _pdot_nt(jnp.concatenate([p, r], axis=1), [q, k])
    a_qp = jnp.where(strict, gram[:, 0:CHUNK, 0:LANES], 0.0)
    a_kp = jnp.where(strict, gram[:, 0:CHUNK, LANES:], 0.0)
    a_rq = jnp.where(incl, gram[:, CHUNK:, 0:LANES], 0.0)
    a_rk = jnp.where(incl, gram[:, CHUNK:, LANES:], 0.0)
    same = lambda log2_bs: (ri >> log2_bs) == (ci >> log2_bs)
    a_d = jnp.where(same(4), a_qp, 0.0)
    tinv = eye + a_d
    pw = _pdot_nn(a_d, a_d)
    for _ in range(2):
        both = _pdot_nn(jnp.concatenate([pw, tinv], axis=1), pw)
        pw = both[:, 0:CHUNK]
        tinv = tinv + both[:, CHUNK:]
    tinv = tinv + _pdot_nn(tinv, pw)
    for log2_bs in (4, 5):
        off = jnp.where(jnp.logical_and(same(log2_bs + 1), jnp.logical_not(same(log2_bs))), a_qp, 0.0)
        tinv = tinv + _pdot_nn(tinv, _pdot_nn(off, tinv))
    av = _pdot_nn(jnp.concatenate([a_kp, a_rk], axis=1), v)
    pw_hat = _pdot_nn(tinv, jnp.concatenate([p, av[:, 0:CHUNK]], axis=2))
    x = _pdot_nn(a_rq, pw_hat)
    r_hat = r + x[:, :, 0:LANES]
    o_loc = x[:, :, LANES:] + av[:, CHUNK:]
    y_p, y_w = _pdot_tn(pw_hat, q)
    (vk,) = _pdot_tn(v, k)
    m_mat = (eye + y_p) * gam
    n_mat = (y_w + vk) * gam

    state = s_scr[...]
    outs = []
    for c in range(SCAN_CHUNKS):
        sl = slice(c * PAIRS, (c + 1) * PAIRS)
        outs.append(_pdot_nt(r_hat[sl], [state]) + o_loc[sl])
        state = _pdot_nn(state, m_mat[sl]) + n_mat[sl]
    s_scr[...] = state
    o = jnp.concatenate(outs, axis=0)

    head = _lane_head(LANES)

    def head_mean(z):
        sums = [jnp.sum(_keep_head(z, head, which), axis=2, keepdims=True) for which in (0, 1)]
        return jnp.where(head == 0, sums[0], sums[1]) * (1.0 / HEAD_A)

    d = o - head_mean(o)
    o_n = d * lax.rsqrt(head_mean(d * d) + EPS_GN)
    o_n = jnp.concatenate([jnp.concatenate([o_n[c * PAIRS + i] for i in range(PAIRS)], axis=1)
                           for c in range(SCAN_CHUNKS)], axis=0)
    ya_ref[...] = ((o_n * gng_ref[...] + gnb_ref[...] + feat(F_BONUS)) * feat(F_G)).astype(ya_ref.dtype)

    @pl.when(pl.program_id(0) == pl.num_programs(0) - 1)
    def _():
        sfin_ref[...] = s_scr[...]


def _wkv_scan(feats, gc, lw):
    t = feats.shape[0]
    vec = pl.BlockSpec((1, D_A), lambda c: (0, 0))
    pair_state = (PAIRS, HEAD_A, LANES)
    rows = SCAN_CHUNKS * CHUNK
    ya, s = pl.pallas_call(
        _scan_kernel,
        out_shape=[jax.ShapeDtypeStruct((t, D_A), BF16), jax.ShapeDtypeStruct(pair_state, F32)],
        grid=(t // rows,),
        in_specs=[pl.BlockSpec((rows, N_FEATS * D_A), lambda c: (c, 0)),
                  pl.BlockSpec((SUBLANES * SCAN_CHUNKS, D_A), lambda c: (c, 0)), vec, vec],
        out_specs=[pl.BlockSpec((rows, D_A), lambda c: (c, 0)), pl.BlockSpec(pair_state, lambda c: (0, 0, 0))],
        scratch_shapes=[pltpu.VMEM(pair_state, F32)],
        compiler_params=_params(("arbitrary",)),
        name="wkv_scan_prompt",
    )(feats, gc, lw["gn_g"], lw["gn_b"])
    s = s.reshape(PAIRS, HEAD_A, 2, HEAD_A).transpose(0, 2, 1, 3).reshape(H_A, HEAD_A, HEAD_A)
    return ya, s


def _wkv_step_kernel(s_ref, rows_ref, sn_ref, o_ref):
    ri = lax.broadcasted_iota(jnp.int32, (1, 1, HEAD_A, HEAD_A), 2)
    ci = lax.broadcasted_iota(jnp.int32, (1, 1, HEAD_A, HEAD_A), 3)
    eye = ri == ci
    pt, rt, qt, kt, v, gc = (rows_ref[:, n] for n in range(6))
    s = s_ref[0]
    u = jnp.sum(s * pt, axis=3, keepdims=True)
    v_col = jnp.sum(jnp.where(eye, v, 0.0), axis=3, keepdims=True)
    m = s + u * qt + v_col * kt
    sn_ref[0] = m * gc
    o_col = jnp.sum(m * rt, axis=3, keepdims=True)
    o_ref[...] = jnp.sum(jnp.where(eye, o_col, 0.0), axis=2, keepdims=True)


def _wkv_step(state, pt, rt, qt, kt, v, gc, bt):
    b = pt.shape[0]
    rows = jnp.stack([pt, rt, qt, kt, v, gc], axis=1).reshape(b, 6, H_A, 1, HEAD_A)
    st_spec = pl.BlockSpec((1, bt, H_A, HEAD_A, HEAD_A), lambda i: (0, i, 0, 0, 0))
    sn, o = pl.pallas_call(
        _wkv_step_kernel,
        out_shape=[jax.ShapeDtypeStruct(state.shape, F32), jax.ShapeDtypeStruct((b, H_A, 1, HEAD_A), F32)],
        grid=(b // bt,),
        in_specs=[st_spec, pl.BlockSpec((bt, 6, H_A, 1, HEAD_A), lambda i: (i, 0, 0, 0, 0))],
        out_specs=[st_spec, pl.BlockSpec((bt, H_A, 1, HEAD_A), lambda i: (i, 0, 0, 0))],
        compiler_params=_params(("parallel",)),
        name="wkv_step_sample",
    )(state, rows)
    return sn, o.reshape(b, D_A)


def _post_kernel(o_ref, bonus_ref, g_ref, gng_ref, gnb_ref, hs_ref, hb_ref, y_ref):
    o = o_ref[...]
    hsum, hbc = hs_ref[...], hb_ref[...]
    mu = _head_sums(o, hsum, hbc) * (1.0 / HEAD_A)
    d = o - mu
    var = _head_sums(d * d, hsum, hbc) * (1.0 / HEAD_A)
    o_n = d * lax.rsqrt(var + EPS_GN) * gng_ref[...] + gnb_ref[...]
    y_ref[...] = ((o_n + bonus_ref[...]) * g_ref[...]).astype(y_ref.dtype)


def _rwkv_post(o, bonus, g, lw, tm, name):
    m = o.shape[0]
    big = pl.BlockSpec((tm, D_A), lambda i: (i, 0))
    vec = pl.BlockSpec((1, D_A), lambda i: (0, 0))
    return pl.pallas_call(
        _post_kernel,
        out_shape=jax.ShapeDtypeStruct((m, D_A), BF16),
        grid=(m // tm,),
        in_specs=[big, big, big, vec, vec, pl.BlockSpec((D_A, LANES), lambda i: (0, 0)),
                  pl.BlockSpec((LANES, D_A), lambda i: (0, 0))],
        out_specs=big,
        compiler_params=_params(("parallel",)),
        name=name,
    )(o, bonus, g, lw["gn_g"], lw["gn_b"], lw["hsum"], lw["hbc"])


def _attn_kernel(q_ref, kc_ref, vc_ref, kh_ref, vh_ref, bias_ref, o_ref, l_ref, *, dil, m_blocks, tiled):
    scale = HEAD_B ** -0.5
    n_streams = SUBLANES if tiled else dil
    n_units = n_streams * m_blocks

    stride = SUBLANES if tiled else dil
    if tiled:
        flat = lambda ref: ref.reshape(ref.shape[0] * SUBLANES, LANES)
        q_ref, kc_ref, vc_ref, kh_ref, vh_ref, o_ref, l_ref = map(flat, (q_ref, kc_ref, vc_ref, kh_ref, vh_ref,
                                                                          o_ref, l_ref))

    def rows(ref, r, start, size):
        if stride == 1:
            return ref[pl.ds(start, size), :]
        return ref[pl.ds(r + stride * start, size, stride=stride), :]

    def put(ref, r, start, val):
        if stride == 1:
            ref[pl.ds(start, BLK), :] = val
        else:
            ref[pl.ds(r + stride * start, BLK, stride=stride), :] = val

    qs, ks, vs, units = [], [], [], []
    for r in range(n_streams):
        for mb in range(m_blocks):
            units.append((r, mb))
            qs.append(rows(q_ref, r, BLK * mb, BLK))
            if mb == 0:
                ks.append(jnp.concatenate([rows(kh_ref, r, 0, BLK), rows(kc_ref, r, 0, BLK)], axis=0))
                vs.append(jnp.concatenate([rows(vh_ref, r, 0, BLK), rows(vc_ref, r, 0, BLK)], axis=0))
            else:
                ks.append(rows(kc_ref, r, BLK * (mb - 1), 2 * BLK))
                vs.append(rows(vc_ref, r, BLK * (mb - 1), 2 * BLK))

    def batch(xs, mul=None):
        pre = (lambda x: x) if mul is None else (lambda x: x * mul)
        return jnp.stack([pre(x[:, sub * HEAD_B:(sub + 1) * HEAD_B]).astype(BF16) for sub in range(2) for x in xs],
                         axis=0)

    assert math.frexp(scale)[0] == 0.5
    qb, kb, vb = batch(qs, scale), batch(ks), batch(vs)
    s = lax.dot_general(qb, kb, (((2,), (2,)), ((0,), (0,))), preferred_element_type=F32)
    s = s.reshape(2, n_units, BLK, 2 * BLK) + bias_ref[...][:, None]
    unit = lax.broadcasted_iota(jnp.int32, (1, n_units, 1, 2 * BLK), 1)
    col = lax.broadcasted_iota(jnp.int32, (1, n_units, 1, 2 * BLK), 3)
    no_prev = jnp.logical_and(jnp.logical_and(pl.program_id(0) == 0, (unit & (m_blocks - 1)) == 0), col < BLK)
    s = jnp.where(no_prev, -jnp.inf, s)
    m = jnp.max(s, axis=-1, keepdims=True)
    p = jnp.exp(s - m)
    l = jnp.sum(p, axis=-1, keepdims=True)
    pv = lax.dot_general(p.astype(BF16).reshape(2 * n_units, BLK, 2 * BLK), vb, (((2,), (1,)), ((0,), (0,))),
                         preferred_element_type=F32)
    o = pv.reshape(2, n_units, BLK, HEAD_B) / l
    lse = jnp.broadcast_to(m + jnp.log(l), (2, n_units, BLK, HEAD_B))
    for u, (r, mb) in enumerate(units):
        put(o_ref, r, BLK * mb, jnp.concatenate([o[0, u], o[1, u]], axis=1))
        put(l_ref, r, BLK * mb, jnp.concatenate([lse[0, u], lse[1, u]], axis=1))


def _attn_prompt(proj, bias, gi, rows_per_step):
    t, n_cols = proj.shape
    dil = DILATIONS[gi]
    span = BLK * dil
    tiled = dil % SUBLANES == 0
    if tiled:
        rows_per_step *= dil // SUBLANES
    m_blocks = rows_per_step // span
    cq = (COL_QKV + gi * D_G) // LANES
    ck = (COL_QKV + D_B + gi * D_G) // LANES
    cv = (COL_QKV + 2 * D_B + gi * D_G) // LANES
    bias_block = (2, BLK, 2 * BLK)
    if tiled:
        n_hi = dil // SUBLANES
        src = proj.reshape(t // dil, n_hi, SUBLANES, n_cols)
        grid = (t // rows_per_step, n_hi, H_G // 2)
        l_rows = rows_per_step // dil
        cur = lambda c0: pl.BlockSpec((l_rows, None, SUBLANES, LANES), lambda i, rh, hp: (i, rh, 0, c0 + hp))
        halo = lambda c0: pl.BlockSpec((BLK, None, SUBLANES, LANES),
                                       lambda i, rh, hp: (jnp.maximum(i * m_blocks - 1, 0), rh, 0, c0 + hp))
        out_spec = pl.BlockSpec((l_rows, None, SUBLANES, LANES), lambda i, rh, hp: (i, rh, 0, hp))
        out = jax.ShapeDtypeStruct((t // dil, n_hi, SUBLANES, D_G), F32)
        bias_spec = pl.BlockSpec(bias_block, lambda i, rh, hp: (hp, 0, 0))
    else:
        src = proj
        grid = (t // rows_per_step, H_G // 2)
        cur = lambda c0: pl.BlockSpec((rows_per_step, LANES), lambda i, hp: (i, c0 + hp))
        halo = lambda c0: pl.BlockSpec((span, LANES), lambda i, hp: (jnp.maximum(i * m_blocks - 1, 0), c0 + hp))
        out_spec = pl.BlockSpec((rows_per_step, LANES), lambda i, hp: (i, hp))
        out = jax.ShapeDtypeStruct((t, D_G), F32)
        bias_spec = pl.BlockSpec(bias_block, lambda i, hp: (hp, 0, 0))
    o, l = pl.pallas_call(
        functools.partial(_attn_kernel, dil=dil, m_blocks=m_blocks, tiled=tiled),
        out_shape=[out, out],
        grid=grid,
        in_specs=[cur(cq), cur(ck), cur(cv), halo(ck), halo(cv), bias_spec],
        out_specs=[out_spec, out_spec],
        compiler_params=_params(("parallel",) * len(grid)),
        name=f"attn_prompt_g{gi}",
    )(src, src, src, src, src, bias)
    return o.reshape(t, D_G), l.reshape(t, D_G)


def _attn_step_kernel(q_ref, kn_ref, vn_ref, c_ref, bias_ref, bias0_ref, o_ref, l_ref):
    scale = HEAD_B ** -0.5
    ri = lax.broadcasted_iota(jnp.int32, (1, 1, HEAD_B, HEAD_B), 2)
    ci = lax.broadcasted_iota(jnp.int32, (1, 1, HEAD_B, HEAD_B), 3)
    eye = ri == ci
    q = q_ref[...]
    q_col = jnp.sum(jnp.where(eye, q, 0.0), axis=3, keepdims=True)
    s = jnp.sum(c_ref[:, 0] * q_col, axis=2, keepdims=True) * scale + bias_ref[...]
    s_new = jnp.sum(q * kn_ref[...], axis=3, keepdims=True) * scale + bias0_ref[...]
    m = jnp.maximum(jnp.max(s, axis=3, keepdims=True), s_new)
    p = jnp.exp(s - m)
    p_new = jnp.exp(s_new - m)
    l = jnp.sum(p, axis=3, keepdims=True) + p_new
    o_col = jnp.sum(c_ref[:, 1] * p, axis=3, keepdims=True)
    o_row = jnp.sum(jnp.where(eye, o_col, 0.0), axis=2, keepdims=True)
    o_ref[...] = (o_row + p_new * vn_ref[...]) / l
    l_ref[...] = jnp.broadcast_to(m + jnp.log(l), o_ref.shape)


def _attn_step(q, k_new, v_new, cache, bias, bias0, gi):
    b = q.shape[0]
    w = cache.shape[2]
    bt = max(1, min(b, STEP_CACHE_WORDS // (2 * D_G * w)))
    cache_t = jnp.transpose(cache, (0, 1, 3, 4, 5, 2))
    vec = pl.BlockSpec((bt, H_G, 1, HEAD_B), lambda i: (i, 0, 0, 0))
    out = jax.ShapeDtypeStruct((b, H_G, 1, HEAD_B), F32)
    o, l = pl.pallas_call(
        _attn_step_kernel,
        out_shape=[out, out],
        grid=(b // bt,),
        in_specs=[vec, vec, vec,
                  pl.BlockSpec((None, bt, 2, H_G, HEAD_B, w), lambda i: (0, i, 0, 0, 0, 0)),
                  pl.BlockSpec((H_G, 1, w), lambda i: (0, 0, 0)),
                  pl.BlockSpec((H_G, 1, 1), lambda i: (0, 0, 0))],
        out_specs=[vec, vec],
        compiler_params=_params(("parallel",)),
        name=f"attn_step_g{gi}",
    )(q, k_new, v_new, cache_t, bias, bias0)
    return o.reshape(b, D_G), l.reshape(b, D_G)


def _mix_kernel(ya_ref, o1_ref, o2_ref, o3_ref, l1_ref, l2_ref, l3_ref, wa_ref, wb_ref, ga_ref, gb_ref, o_ref):
    l1, l2, l3 = l1_ref[...], l2_ref[...], l3_ref[...]
    m = jnp.maximum(jnp.maximum(l1, l2), l3)
    e1, e2, e3 = jnp.exp(l1 - m), jnp.exp(l2 - m), jnp.exp(l3 - m)
    den = e1 + e2 + e3
    yb = ((e1 / den) * o1_ref[...] + (e2 / den) * o2_ref[...] + (e3 / den) * o3_ref[...]).astype(BF16)
    mixed = ga_ref[...] * _dot_bf16(ya_ref[...], wa_ref[...]) + gb_ref[...] * _dot_bf16(yb, wb_ref[...])
    o_ref[...] = mixed.astype(o_ref.dtype)


def _mix(ya, outs, lses, wa, wb, proj, tm, name):
    m = ya.shape[0]
    n = D_MODEL
    assert COL_GATE % n == 0
    ga0 = COL_GATE // n
    grp = pl.BlockSpec((tm, D_G), lambda i: (i, 0))
    resident = lambda rows: pl.BlockSpec((rows, n), lambda i: (0, 0), pipeline_mode=pl.Buffered(1))
    return pl.pallas_call(
        _mix_kernel,
        out_shape=jax.ShapeDtypeStruct((m, n), BF16),
        grid=(m // tm,),
        in_specs=[pl.BlockSpec((tm, D_A), lambda i: (i, 0))] + [grp] * 6 + [
            resident(D_A), resident(D_G),
            pl.BlockSpec((tm, n), lambda i: (i, ga0)),
            pl.BlockSpec((tm, n), lambda i: (i, ga0 + 1)),
        ],
        out_specs=pl.BlockSpec((tm, n), lambda i: (i, 0)),
        compiler_params=_params(("parallel",)),
        name=name,
    )(ya, *outs, *lses, wa, wb, proj, proj)


def _wo_kernel(a_ref, w_ref, x_ref, g_ref, h_ref, hn_ref):
    h = x_ref[...] + _dot_bf16(a_ref[...], w_ref[...])
    h_ref[...] = h
    hn_ref[...] = _rms(h, g_ref[...]).astype(BF16)


def _wo(a, w, x, g, tm, name):
    m, k = a.shape
    n = w.shape[1]
    row = pl.BlockSpec((tm, n), lambda i: (i, 0))
    return pl.pallas_call(
        _wo_kernel,
        out_shape=[jax.ShapeDtypeStruct((m, n), F32), jax.ShapeDtypeStruct((m, n), BF16)],
        grid=(m // tm,),
        in_specs=[pl.BlockSpec((tm, k), lambda i: (i, 0)), pl.BlockSpec((k, n), lambda i: (0, 0)), row,
                  pl.BlockSpec((1, n), lambda i: (0, 0))],
        out_specs=[row, row],
        compiler_params=_params(("parallel",)),
        name=name,
    )(a, w, x, g.reshape(1, n))


def _gelu(x):
    return 0.5 * x * (1.0 + lax.erf(x * (1.0 / math.sqrt(2.0))))


def _ffn_up_kernel(h_ref, w1_ref, w2_ref, cw1_ref, cw2_ref, cb1_ref, cb2_ref, act_ref, t1_ref, t2_ref,
                   c1_scr, c2_scr, w1_scr, w2_scr):
    tm = h_ref.shape[0]

    @pl.when(pl.program_id(1) == 0)
    def _():
        c1_scr[...] = jnp.zeros_like(c1_scr)
        c2_scr[...] = jnp.zeros_like(c2_scr)
        w1_scr[...] = w1_ref[...].astype(BF16)
        w2_scr[...] = w2_ref[...].astype(BF16)

    a = h_ref[...]
    row = lax.broadcasted_iota(jnp.int32, (tm, 1), 0)

    def conv(u, carry_ref, cw_ref, cb_ref):
        m1 = jnp.where(row == 0, carry_ref[7:8, :], pltpu.roll(u, 1, axis=0))
        m2 = jnp.where(row == 0, carry_ref[6:7, :], jnp.where(row == 1, carry_ref[7:8, :], pltpu.roll(u, 2, axis=0)))
        return cb_ref[...] + cw_ref[0:1, :] * m2 + cw_ref[1:2, :] * m1 + cw_ref[2:3, :] * u

    u1 = _dot_bf16(a, w1_scr[...])
    u2 = _dot_bf16(a, w2_scr[...])
    c1 = conv(u1, c1_scr, cw1_ref, cb1_ref)
    c2 = conv(u2, c2_scr, cw2_ref, cb2_ref)
    act_ref[...] = (_gelu(c1) * c2).astype(act_ref.dtype)
    c1_scr[...] = u1[tm - 8:tm]
    c2_scr[...] = u2[tm - 8:tm]
    t1_ref[...] = u1[tm - 8:tm]
    t2_ref[...] = u2[tm - 8:tm]


def _ffn_up_prompt(hn, w_up, conv_w, conv_b, tm, tn):
    m = hn.shape[0]
    nj = D_FF // tn
    tail = jax.ShapeDtypeStruct((8, D_FF), F32)
    return pl.pallas_call(
        _ffn_up_kernel,
        out_shape=[jax.ShapeDtypeStruct((m, D_FF), BF16), tail, tail],
        grid=(nj, m // tm),
        in_specs=[
            pl.BlockSpec((tm, D_MODEL), lambda j, i: (i, 0)),
            pl.BlockSpec((D_MODEL, tn), lambda j, i: (0, j)),
            pl.BlockSpec((D_MODEL, tn), lambda j, i: (0, nj + j)),
            pl.BlockSpec((CONV_W, tn), lambda j, i: (0, j)),
            pl.BlockSpec((CONV_W, tn), lambda j, i: (0, nj + j)),
            pl.BlockSpec((1, tn), lambda j, i: (0, j)),
            pl.BlockSpec((1, tn), lambda j, i: (0, nj + j)),
        ],
        out_specs=[pl.BlockSpec((tm, tn), lambda j, i: (i, j)),
                   pl.BlockSpec((8, tn), lambda j, i: (0, j)),
                   pl.BlockSpec((8, tn), lambda j, i: (0, j))],
        scratch_shapes=[pltpu.VMEM((8, tn), F32), pltpu.VMEM((8, tn), F32),
                        pltpu.VMEM((D_MODEL, tn), BF16), pltpu.VMEM((D_MODEL, tn), BF16)],
        compiler_params=_params(("arbitrary", "arbitrary")),
        name="ffn_up_prompt",
    )(hn, w_up, w_up, conv_w, conv_w, conv_b, conv_b)


def _ffn_act_step_kernel(up_ref, prev_ref, cw_ref, cb_ref, act_ref):
    up = up_ref[...]
    w = 2 * D_FF
    c = cb_ref[...] + cw_ref[0:1, :] * prev_ref[:, 0:w] + cw_ref[1:2, :] * prev_ref[:, w:2 * w] + cw_ref[2:3, :] * up
    act_ref[...] = (_gelu(c[:, 0:D_FF]) * c[:, D_FF:w]).astype(act_ref.dtype)


def _ffn_act_step(up, conv_prev, conv_w, conv_b):
    b = up.shape[0]
    return pl.pallas_call(
        _ffn_act_step_kernel,
        out_shape=jax.ShapeDtypeStruct((b, D_FF), BF16),
        compiler_params=pltpu.CompilerParams(vmem_limit_bytes=VMEM_LIMIT),
        name="ffn_act_sample",
    )(up, conv_prev.reshape(b, (CONV_W - 1) * 2 * D_FF), conv_w, conv_b)


def _down_kernel(a_ref, w_ref, h_ref, g_ref, o_ref, acc_ref):
    @pl.when(pl.program_id(1) == 0)
    def _():
        acc_ref[...] = h_ref[...]

    out = acc_ref[...] + _dot_bf16(a_ref[...], w_ref[...])
    acc_ref[...] = out
    inv = lax.rsqrt(jnp.mean(out * out, axis=-1, keepdims=True) + EPS_RMS)
    o_ref[...] = out * inv * g_ref[...]


def _ffn_down(act, w_down, h, g, tm, tk, name):
    m = act.shape[0]
    nk = D_FF // tk
    return pl.pallas_call(
        _down_kernel,
        out_shape=jax.ShapeDtypeStruct((m, D_MODEL), F32),
        grid=(m // tm, nk),
        in_specs=[
            pl.BlockSpec((tm, tk), lambda i, k: (i, k)),
            pl.BlockSpec((tk, D_MODEL), lambda i, k: (k, 0)),
            pl.BlockSpec((tm, D_MODEL), lambda i, k: (i, 0)),
            pl.BlockSpec((1, D_MODEL), lambda i, k: (0, 0)),
        ],
        out_specs=pl.BlockSpec((tm, D_MODEL), lambda i, k: (i, 0)),
        scratch_shapes=[pltpu.VMEM((tm, D_MODEL), F32)],
        compiler_params=_params(("parallel", "arbitrary")),
        name=name,
    )(act, w_down, h, g.reshape(1, D_MODEL))


def _rel_bucket(dist):
    max_exact = N_BUCKETS // 2
    d_f = np.maximum(dist, 1).astype(np.float64)
    scaled = np.log(d_f / max_exact) / math.log(MAX_DISTANCE / max_exact) * (N_BUCKETS - max_exact)
    large = np.minimum(max_exact + scaled.astype(np.int64), N_BUCKETS - 1)
    return np.where(dist < max_exact, dist, large).astype(np.int32)


def _bias_rows(tab, dist):
    onehot = (jnp.asarray(_rel_bucket(dist))[None, :] == jnp.arange(N_BUCKETS)[:, None]).astype(F32)
    return jnp.dot(tab.T, onehot, precision=HIGHEST)


def _bias_tables(rel_bias, gi):
    dil = DILATIONS[gi]
    win = WINDOWS[gi]
    reach = win // dil
    tab = rel_bias[:, gi * H_G:(gi + 1) * H_G]
    qi = np.arange(BLK)[:, None]
    ki = np.arange(2 * BLK)[None, :]
    rel = qi + BLK - ki
    blk = _bias_rows(tab, (dil * np.maximum(rel, 0)).reshape(-1)).reshape(H_G, BLK, 2 * BLK)
    blk = jnp.where(((rel >= 0) & (rel <= reach))[None], blk, -jnp.inf)
    back = win - np.arange(win)
    step = jnp.where((back % dil == 0)[None, :], _bias_rows(tab, back), -jnp.inf)[:, None, :]
    step0 = _bias_rows(tab, np.zeros((1,), np.int64))[:, None, :]
    return blk, step, step0


def _tiles(m, prompt):
    if prompt:
        return dict(tag="prompt", proj_m=1024, prep_m=2 * CHUNK, attn_rows=2048, mix_m=512, wo_m=512, up_m=1024,
                    up_n=512, down_m=512, down_k=D_FF // 2)
    return dict(tag="sample", proj_m=m, prep_m=m, step_b=4, post_m=m, mix_m=m, wo_m=m, up_n=512, down_m=m,
                down_k=D_FF // 2)


def _layer(x, lw, bias_blk, prompt, state=None, proj=None):
    m = x.shape[0]
    tiles = _tiles(m, prompt)
    if proj is None:
        proj = _proj(x, lw["norm1_g"], lw["w_in"], lw["in_bias"], tiles["proj_m"], "proj_" + tiles["tag"])

    if prompt:
        feats, gc = _rwkv_prep(proj, None, tiles["prep_m"], CHUNK, lw, "rwkv_prep_prompt")
        ya, wkv_new = _wkv_scan(feats, gc, lw)
        wkv_new = wkv_new[None, None]
    else:
        feats, gc = _rwkv_prep(proj, state["shift"], tiles["prep_m"], 1, lw, "rwkv_prep_sample")
        f = lambda n: feats[:, n * D_A:(n + 1) * D_A]
        wkv_new, o = _wkv_step(state["wkv"], f(F_PT), f(F_RT), f(F_QT), f(F_KT), f(F_V), gc, tiles["step_b"])
        ya = _rwkv_post(o, f(F_BONUS), f(F_G), lw, tiles["post_m"], "rwkv_post_" + tiles["tag"])

    outs, lses = [], []
    for gi in range(N_GROUPS):
        if prompt:
            o_g, l_g = _attn_prompt(proj, bias_blk[gi][0], gi, tiles["attn_rows"])
        else:
            sl = lambda c0: proj[:, c0 + gi * D_G:c0 + (gi + 1) * D_G].reshape(m, H_G, 1, HEAD_B)
            o_g, l_g = _attn_step(sl(COL_QKV), sl(COL_QKV + D_B), sl(COL_QKV + 2 * D_B), state["win"][gi],
                                  bias_blk[gi][1], bias_blk[gi][2], gi)
        outs.append(o_g)
        lses.append(l_g)
    mixed = _mix(ya, outs, lses, lw["w_out_a"], lw["w_out_b"], proj, tiles["mix_m"], "mix_" + tiles["tag"])
    h, hn = _wo(mixed, lw["w_o"], x, lw["norm2_g"], tiles["wo_m"], "wo_" + tiles["tag"])

    if prompt:
        act, t1, t2 = _ffn_up_prompt(hn, lw["w_up"], lw["conv_w"], lw["conv_b"], tiles["up_m"], tiles["up_n"])
        conv_new = jnp.concatenate([t1[6:8], t2[6:8]], axis=1)[None, None]
    else:
        up = _matmul(hn, lw["w_up"], m, tiles["up_n"], "up_sample")
        act = _ffn_act_step(up, state["conv"], lw["conv_w"], lw["conv_b"])
        conv_new = jnp.concatenate([state["conv"][:, 1:], up[:, None, :]], axis=1)[None]
    y = _ffn_down(act, lw["w_down"], h, lw["normf_g"], tiles["down_m"], tiles["down_k"], "down_" + tiles["tag"])
    return y, proj, wkv_new, conv_new


def kernel(x_prompt, x_sample, state_wkv, state_shift, state_ffn_conv, cache_win1, cache_win2, cache_win3, rel_bias,
           norm1_g, w_in, gate_b, mu_shift, w0, w_up_decay, a0, w_up_aaa, w_up_gate, k_k, k_a, r_k, gn_g, gn_b,
           w_out_a, w_out_b, w_o, norm2_g, w_up, conv_w, conv_b, w_down, normf_g):
    row = lambda a: a.reshape(1, -1)
    w_lora = jnp.zeros((D_LORA, 3 * D_A), F32)
    w_lora = w_lora.at[0:D_DECAY_LORA, 0:D_A].set(w_up_decay[0])
    w_lora = w_lora.at[D_DECAY_LORA:D_DECAY_LORA + D_AAA_LORA, D_A:2 * D_A].set(w_up_aaa[0])
    w_lora = w_lora.at[D_DECAY_LORA + D_AAA_LORA:, 2 * D_A:].set(w_up_gate[0])
    head = np.arange(D_A) // HEAD_A
    lw = dict(
        norm1_g=norm1_g[0], norm2_g=norm2_g[0], normf_g=normf_g,
        in_bias=jnp.concatenate([jnp.zeros((1, COL_GATE), F32), row(gate_b[0])], axis=1),
        mu=row(mu_shift[0]), w_lora=jnp.stack(_split2(w_lora)), w0=row(w0[0]), a0=row(a0[0]), k_k=row(k_k[0]), k_a=row(k_a[0]),
        r_k=row(r_k[0]), gn_g=row(gn_g[0]), gn_b=row(gn_b[0]),
        hsum=jnp.asarray(head[:, None] == np.arange(LANES)[None, :], BF16),
        hbc=jnp.asarray(np.arange(LANES)[:, None] == head[None, :], BF16),
        w_out_a=w_out_a[0].astype(BF16), w_out_b=w_out_b[0].astype(BF16), w_o=w_o[0].astype(BF16),
        w_up=w_up[0], conv_w=conv_w[0], conv_b=row(conv_b[0]), w_down=w_down[0].astype(BF16),
    )
    bias = [_bias_tables(rel_bias, gi) for gi in range(N_GROUPS)]

    proj_s, lw["w_in"] = _proj_cast(x_sample[:, 0], norm1_g[0], w_in[0], lw["in_bias"])
    y_p, proj_p, wkv_p, conv_p = _layer(x_prompt[0], lw, bias, prompt=True)

    b = DEC_BATCH
    state = dict(wkv=state_wkv, shift=state_shift.reshape(b, D_SHIFT), conv=state_ffn_conv[0],
                 win=(cache_win1, cache_win2, cache_win3))
    y_s, proj_s, wkv_s, conv_s = _layer(x_sample[:, 0], lw, bias, prompt=False, state=state, proj=proj_s)

    def kv_rows(proj, lo, gi):
        k = proj[lo:, COL_QKV + D_B + gi * D_G:COL_QKV + D_B + (gi + 1) * D_G]
        v = proj[lo:, COL_QKV + 2 * D_B + gi * D_G:COL_QKV + 2 * D_B + (gi + 1) * D_G]
        n = k.shape[0]
        return jnp.stack([k.reshape(n, H_G, HEAD_B), v.reshape(n, H_G, HEAD_B)], axis=1)

    win_p = [kv_rows(proj_p, SEQ - min(WINDOWS[gi], SEQ), gi)[None, None] for gi in range(N_GROUPS)]
    win_s = [kv_rows(proj_s, 0, gi)[None, :, None] for gi in range(N_GROUPS)]
    return (y_p[None], y_s[:, None],
            wkv_p, wkv_s,
            proj_p[SEQ - 1:, 0:D_SHIFT][None, None], proj_s[:, 0:D_SHIFT][None, :, None],
            conv_p, conv_s,
            win_p[0], win_s[0], win_p[1], win_s[1], win_p[2], win_s[2])
```

```python
import functools
import math

import numpy as np
import jax
import jax.numpy as jnp
from jax import lax
from jax.experimental import pallas as pl
from jax.experimental.pallas import tpu as pltpu

F32 = jnp.float32
BF16 = jnp.bfloat16
HIGHEST = lax.Precision.HIGHEST

D_MODEL = 2048
SEQ = 8192
DEC_BATCH = 32
HEAD_A = 64
H_A = 16
D_A = H_A * HEAD_A
D_DECAY_LORA = 96
D_AAA_LORA = 96
D_GATE_LORA = 64
D_LORA = D_DECAY_LORA + D_AAA_LORA + D_GATE_LORA
D_SHIFT = 3 * D_A + D_LORA
EPS_GN = 64e-5
HEAD_B = 64
H_G = 8
D_G = H_G * HEAD_B
WINDOWS = (128, 512, 2048)
DILATIONS = (1, 4, 16)
N_GROUPS = 3
D_B = N_GROUPS * D_G
BLK = 128
N_BUCKETS = 32
MAX_DISTANCE = 2048
COL_QKV = D_SHIFT
COL_GATE_SRC = D_SHIFT + 3 * D_B
D_IN = COL_GATE_SRC + 2 * D_MODEL
PROJ_TN = 1024
COL_GATE = -(-COL_GATE_SRC // PROJ_TN) * PROJ_TN
D_IN_PAD = COL_GATE + 2 * D_MODEL
D_FF = 5632
CONV_W = 3
EPS_RMS = 1e-6
CHUNK = 64
SCAN_CHUNKS = 1
F_RT, F_KT, F_QT, F_PT, F_V, F_BONUS, F_G = range(7)
N_FEATS = 7
LANES = 128
SUBLANES = 8
VMEM_LIMIT = 56 * 1024 * 1024
STEP_CACHE_WORDS = 2 * 1024 * 1024


def _params(sem, vmem=VMEM_LIMIT):
    return pltpu.CompilerParams(dimension_semantics=sem, vmem_limit_bytes=vmem)


def _sigmoid(x):
    return 1.0 / (1.0 + jnp.exp(-x))


def _dot_bf16(a, b):
    return jnp.dot(a, b, preferred_element_type=F32)


def _split2(x):
    hi = x.astype(BF16)
    return hi, (x - hi.astype(F32)).astype(BF16)


def _split3(x):
    hi = x.astype(BF16)
    r1 = x - hi.astype(F32)
    mid = r1.astype(BF16)
    lo = (r1 - mid.astype(F32)).astype(BF16)
    return hi, mid, lo


def _dot_exact_rhs(x, m):
    hi, mid, lo = _split3(x)
    return _dot_bf16(hi, m) + _dot_bf16(mid, m) + _dot_bf16(lo, m)


def _head_sums(x, hsum, hbc):
    return _dot_exact_rhs(_dot_exact_rhs(x, hsum), hbc)


def _dot_exact_lhs(m, x):
    hi, mid, lo = _split3(x)
    return _dot_bf16(m, hi) + _dot_bf16(m, mid) + _dot_bf16(m, lo)


def _rms(x, g):
    return x * lax.rsqrt(jnp.mean(x * x, axis=-1, keepdims=True) + EPS_RMS) * g


def _proj_cast_kernel(x_ref, g_ref, w_ref, b_ref, o_ref, wb_ref, xn_scr, *, gap_block):
    j = pl.program_id(0)

    @pl.when(j == 0)
    def _():
        xn_scr[...] = _rms(x_ref[...], g_ref[...]).astype(BF16)

    wb = jnp.where(j != gap_block, w_ref[...], 0.0).astype(BF16)
    wb_ref[...] = wb
    acc = _dot_bf16(xn_scr[...], wb)
    tn = o_ref.shape[1]
    col = lax.broadcasted_iota(jnp.int32, (1, tn), 1) + j * tn
    o_ref[...] = jnp.where(col >= COL_GATE, _sigmoid(acc + b_ref[...]), acc)


def _proj_cast(x, g, w, bias):
    m, k = x.shape
    tn = COL_GATE - COL_GATE_SRC
    assert tn % LANES == 0 and COL_GATE_SRC % tn == 0
    gap_block = COL_GATE_SRC // tn
    return pl.pallas_call(
        functools.partial(_proj_cast_kernel, gap_block=gap_block),
        out_shape=[jax.ShapeDtypeStruct((m, D_IN_PAD), F32), jax.ShapeDtypeStruct((k, D_IN_PAD), BF16)],
        grid=(D_IN_PAD // tn,),
        in_specs=[
            pl.BlockSpec((m, k), lambda j: (0, 0)),
            pl.BlockSpec((1, k), lambda j: (0, 0)),
            pl.BlockSpec((k, tn), lambda j: (0, jnp.where(j < gap_block, j, jnp.maximum(j - 1, 0)))),
            pl.BlockSpec((1, tn), lambda j: (0, j)),
        ],
        out_specs=[pl.BlockSpec((m, tn), lambda j: (0, j)), pl.BlockSpec((k, tn), lambda j: (0, j))],
        scratch_shapes=[pltpu.VMEM((m, k), BF16)],
        compiler_params=_params(("arbitrary",)),
        name="proj_sample_cast_w_in",
    )(x, g.reshape(1, k), w, bias)


def _proj_kernel(x_ref, g_ref, w_ref, b_ref, o_ref, xn_scr):
    @pl.when(pl.program_id(1) == 0)
    def _():
        xn_scr[...] = _rms(x_ref[...], g_ref[...]).astype(BF16)

    acc = _dot_bf16(xn_scr[...], w_ref[...])
    tn = o_ref.shape[1]
    col = lax.broadcasted_iota(jnp.int32, (1, tn), 1) + pl.program_id(1) * tn
    o_ref[...] = jnp.where(col >= COL_GATE, _sigmoid(acc + b_ref[...]), acc)


def _proj(x, g, w, bias, tm, name):
    m, k = x.shape
    n = w.shape[1]
    return pl.pallas_call(
        _proj_kernel,
        out_shape=jax.ShapeDtypeStruct((m, n), F32),
        grid=(m // tm, n // PROJ_TN),
        in_specs=[
            pl.BlockSpec((tm, k), lambda i, j: (i, 0)),
            pl.BlockSpec((1, k), lambda i, j: (0, 0)),
            pl.BlockSpec((k, PROJ_TN), lambda i, j: (0, j)),
            pl.BlockSpec((1, PROJ_TN), lambda i, j: (0, j)),
        ],
        out_specs=pl.BlockSpec((tm, PROJ_TN), lambda i, j: (i, j)),
        scratch_shapes=[pltpu.VMEM((tm, k), BF16)],
        compiler_params=_params(("parallel", "arbitrary")),
        name=name,
    )(x, g.reshape(1, k), w, bias)


def _matmul_kernel(a_ref, w_ref, o_ref):
    o_ref[...] = _dot_bf16(a_ref[...], w_ref[...].astype(BF16))


def _matmul(a, w, tm, tn, name):
    m, k = a.shape
    n = w.shape[1]
    return pl.pallas_call(
        _matmul_kernel,
        out_shape=jax.ShapeDtypeStruct((m, n), F32),
        grid=(m // tm, n // tn),
        in_specs=[pl.BlockSpec((tm, k), lambda i, j: (i, 0)), pl.BlockSpec((k, tn), lambda i, j: (0, j))],
        out_specs=pl.BlockSpec((tm, tn), lambda i, j: (i, j)),
        compiler_params=_params(("parallel", "arbitrary")),
        name=name,
    )(a, w)


def _prep_kernel(p_ref, prev_ref, mu_ref, wl_ref, w0_ref, a0_ref, kk_ref, ka_ref, rk_ref, lmat_ref, sel_ref,
                 hs_ref, hb_ref, f_ref, gc_ref, *, rows_are_time):
    p = p_ref[...]
    tm = p.shape[0]
    if rows_are_time:
        last = jnp.where(pl.program_id(0) == 0, 0.0, prev_ref[7:8, :])
        row = lax.broadcasted_iota(jnp.int32, (tm, 1), 0)
        prev = jnp.where(row == 0, last, pltpu.roll(p, 1, axis=0))
    else:
        prev = prev_ref[...]
    xm = p + mu_ref[...] * (prev - p)
    r = xm[:, 0:D_A]
    k = xm[:, D_A:2 * D_A]
    v = xm[:, 2 * D_A:3 * D_A]
    xl = xm[:, 3 * D_A:D_SHIFT]
    lane = lax.broadcasted_iota(jnp.int32, xl.shape, 1)
    act = jnp.where(lane < D_DECAY_LORA, jnp.tanh(xl),
                    jnp.where(lane < D_DECAY_LORA + D_AAA_LORA, xl, _sigmoid(xl)))
    act_hi, act_lo = _split2(act)
    w_hi, w_lo = wl_ref[0], wl_ref[1]
    lora = _dot_bf16(act_hi, w_hi) + _dot_bf16(act_hi, w_lo) + _dot_bf16(act_lo, w_hi)
    y = -(w0_ref[...] + lora[:, 0:D_A])
    softplus = jnp.maximum(y, 0.0) + jnp.log(1.0 + jnp.exp(-jnp.abs(y)))
    logw = -jnp.exp(-softplus - 0.5)
    a = _sigmoid(a0_ref[...] + lora[:, D_A:2 * D_A])
    g = lora[:, 2 * D_A:3 * D_A]
    kkr = k * kk_ref[...]
    kp = k * (1.0 + (a - 1.0) * ka_ref[...])
    seg = _head_sums(jnp.concatenate([kkr * kkr, r * kp * rk_ref[...]], axis=0), hs_ref[...], hb_ref[...])
    kk = kkr / jnp.maximum(jnp.sqrt(seg[0:tm]), 1e-12)
    cum = _dot_exact_lhs(lmat_ref[...], logw)
    e_out = jnp.exp(-cum)
    feats = {F_RT: r * jnp.exp(cum), F_KT: kp * e_out, F_QT: kk * a * e_out, F_PT: -kk * jnp.exp(cum - logw),
             F_V: v, F_BONUS: seg[tm:2 * tm] * v, F_G: g}
    for n, val in feats.items():
        f_ref[:, n * D_A:(n + 1) * D_A] = val
    gc_ref[...] = jnp.exp(_dot_exact_lhs(sel_ref[...], cum))


def _rwkv_prep(proj, prev, tm, chunk, lw, name):
    m = proj.shape[0]
    rows_are_time = prev is None
    n_tiles = m // tm
    t = np.arange(tm)
    lmat = ((t[:, None] // chunk == t[None, :] // chunk) & (t[None, :] <= t[:, None])).astype(np.float32)
    sel_stride = 1 if chunk == 1 else 8
    n_sel = sel_stride * (tm // chunk)
    sel = np.zeros((n_sel, tm), np.float32)
    for c in range(tm // chunk):
        sel[sel_stride * c, (c + 1) * chunk - 1] = 1.0
    if rows_are_time:
        prev_arr = proj
        prev_spec = pl.BlockSpec((8, D_SHIFT), lambda i: (jnp.maximum(i * (tm // 8) - 1, 0), 0))
    else:
        prev_arr = prev
        prev_spec = pl.BlockSpec((tm, D_SHIFT), lambda i: (i, 0))
    vec = lambda d: pl.BlockSpec((1, d), lambda i: (0, 0))
    full = lambda a: pl.BlockSpec(a.shape, lambda i: (0, 0))
    lmat_b = jnp.asarray(lmat, BF16)
    sel_b = jnp.asarray(sel, BF16)
    return pl.pallas_call(
        functools.partial(_prep_kernel, rows_are_time=rows_are_time),
        out_shape=[jax.ShapeDtypeStruct((m, N_FEATS * D_A), F32),
                   jax.ShapeDtypeStruct((n_tiles * n_sel, D_A), F32)],
        grid=(n_tiles,),
        in_specs=[pl.BlockSpec((tm, D_SHIFT), lambda i: (i, 0)), prev_spec, vec(D_SHIFT),
                  pl.BlockSpec(lw["w_lora"].shape, lambda i: (0, 0, 0)),
                  vec(D_A), vec(D_A), vec(D_A), vec(D_A), vec(D_A), full(lmat_b), full(sel_b), full(lw["hsum"]),
                  full(lw["hbc"])],
        out_specs=[pl.BlockSpec((tm, N_FEATS * D_A), lambda i: (i, 0)), pl.BlockSpec((n_sel, D_A), lambda i: (i, 0))],
        compiler_params=_params(("parallel",)),
        name=name,
    )(proj, prev_arr, lw["mu"], lw["w_lora"], lw["w0"], lw["a0"], lw["k_k"], lw["k_a"], lw["r_k"], lmat_b, sel_b,
      lw["hsum"], lw["hbc"])


_NN = (((2,), (1,)), ((0,), (0,)))
_NT = (((2,), (2,)), ((0,), (0,)))
PAIRS = H_A // 2
LOG2_HEAD = HEAD_A.bit_length() - 1
assert HEAD_A == 1 << LOG2_HEAD and 2 * HEAD_A == LANES


def _lane_head(n):
    return (lax.broadcasted_iota(jnp.int32, (1, 1, n), 2) >> LOG2_HEAD) & 1


def _keep_head(x, head, which):
    return jnp.where(head == which, x, jnp.zeros_like(x))


def _pdot_nn(a, b):
    n = b.shape[2]
    ah, al = _split2(a)
    bh, bl = _split2(b)
    head = _lane_head(n)
    bd = lambda x: jnp.concatenate([_keep_head(x, head, 0), _keep_head(x, head, 1)], axis=1)
    bd_hi = bd(bh)
    a_cat = jnp.concatenate([ah, al], axis=2)
    b_cat = jnp.concatenate([jnp.concatenate([bd_hi, bd(bl)], axis=2),
                             jnp.concatenate([bd_hi, jnp.zeros_like(bd_hi)], axis=2)], axis=1)
    res = lax.dot_general(a_cat, b_cat, _NN, preferred_element_type=F32)
    return res[:, :, 0:n] + res[:, :, n:]


def _pdot_nt(a, bs):
    ah, al = _split2(a)
    head = _lane_head(LANES)
    hi, lo = [], []
    for b in bs:
        bh, bl = _split2(b)
        for which in (0, 1):
            hi.append(_keep_head(bh, head, which))
            lo.append(_keep_head(bl, head, which))
    bh, bl = jnp.concatenate(hi, axis=1), jnp.concatenate(lo, axis=1)
    n_rows = bh.shape[1]
    a_cat = jnp.concatenate([ah, al], axis=2)
    b_cat = jnp.concatenate([jnp.concatenate([bh, bh], axis=2),
                             jnp.concatenate([bl, jnp.zeros_like(bl)], axis=2)], axis=1)
    res = lax.dot_general(a_cat, b_cat, _NT, preferred_element_type=F32)
    return res[:, :, 0:n_rows] + res[:, :, n_rows:]


def _pdot_tn(x, y):
    nx = x.shape[2]
    xh, xl = _split2(jnp.swapaxes(x, 1, 2))
    yh, yl = _split2(y)
    x_cat = jnp.concatenate([xh, xh, xl, jnp.zeros_like(xl)], axis=2)
    y_cat = jnp.concatenate([yh, yl, yh, jnp.zeros_like(yl)], axis=1)
    tot = lax.dot_general(x_cat, y_cat, _NN, preferred_element_type=F32)
    head = _lane_head(LANES)
    return [jnp.where(head == 0, tot[:, g:g + HEAD_A], tot[:, g + HEAD_A:g + LANES]) for g in range(0, nx, LANES)]


def _scan_kernel(f_ref, gc_ref, gng_ref, gnb_ref, ya_ref, sfin_ref, s_scr):
    @pl.when(pl.program_id(0) == 0)
    def _():
        s_scr[...] = jnp.zeros_like(s_scr)

    def pairs(x, rows):
        return jnp.stack([x[c * rows:(c + 1) * rows, i * LANES:(i + 1) * LANES]
                          for c in range(SCAN_CHUNKS) for i in range(PAIRS)], axis=0)

    feat = lambda n: f_ref[:, n * D_A:(n + 1) * D_A]
    p, r, q, k, v = (pairs(feat(n), CHUNK) for n in (F_PT, F_RT, F_QT, F_KT, F_V))
    gam = pairs(gc_ref[...], SUBLANES)[:, 0:1]
    ri = lax.broadcasted_iota(jnp.int32, (1, CHUNK, LANES), 1)
    ci = lax.broadcasted_iota(jnp.int32, (1, CHUNK, LANES), 2) & (HEAD_A - 1)
    strict = ri > ci
    incl = ri >= ci
    eye = (ri == ci).astype(F32)

    gram = _pdot_nt(jnp.concatenate([p, r], axis=1), [q, k])
    a_qp = jnp.where(strict, gram[:, 0:CHUNK, 0:LANES], 0.0)
    a_kp = jnp.where(strict, gram[:, 0:CHUNK, LANES:], 0.0)
    a_rq = jnp.where(incl, gram[:, CHUNK:, 0:LANES], 0.0)
    a_rk = jnp.where(incl, gram[:, CHUNK:, LANES:], 0.0)
    same = lambda log2_bs: (ri >> log2_bs) == (ci >> log2_bs)
    a_d = jnp.where(same(4), a_qp, 0.0)
    tinv = eye + a_d
    pw = _pdot_nn(a_d, a_d)
    for _ in range(2):
        both = _pdot_nn(jnp.concatenate([pw, tinv], axis=1), pw)
        pw = both[:, 0:CHUNK]
        tinv = tinv + both[:, CHUNK:]
    tinv = tinv + _pdot_nn(tinv, pw)
    for log2_bs in (4, 5):
        off = jnp.where(jnp.logical_and(same(log2_bs + 1), jnp.logical_not(same(log2_bs))), a_qp, 0.0)
        tinv = tinv + _pdot_nn(tinv, _pdot_nn(off, tinv))
    av = _pdot_nn(jnp.concatenate([a_kp, a_rk], axis=1), v)
    pw_hat = _pdot_nn(tinv, jnp.concatenate([p, av[:, 0:CHUNK]], axis=2))
    x = _pdot_nn(a_rq, pw_hat)
    r_hat = r + x[:, :, 0:LANES]
    o_loc = x[:, :, LANES:] + av[:, CHUNK:]
    y_p, y_w = _pdot_tn(pw_hat, q)
    (vk,) = _pdot_tn(v, k)
    m_mat = (eye + y_p) * gam
    n_mat = (y_w + vk) * gam

    state = s_scr[...]
    outs = []
    for c in range(SCAN_CHUNKS):
        sl = slice(c * PAIRS, (c + 1) * PAIRS)
        outs.append(_pdot_nt(r_hat[sl], [state]) + o_loc[sl])
        state = _pdot_nn(state, m_mat[sl]) + n_mat[sl]
    s_scr[...] = state
    o = jnp.concatenate(outs, axis=0)

    head = _lane_head(LANES)

    def head_mean(z):
        sums = [jnp.sum(_keep_head(z, head, which), axis=2, keepdims=True) for which in (0, 1)]
        return jnp.where(head == 0, sums[0], sums[1]) * (1.0 / HEAD_A)

    d = o - head_mean(o)
    o_n = d * lax.rsqrt(head_mean(d * d) + EPS_GN)
    o_n = jnp.concatenate([jnp.concatenate([o_n[c * PAIRS + i] for i in range(PAIRS)], axis=1)
                           for c in range(SCAN_CHUNKS)], axis=0)
    ya_ref[...] = ((o_n * gng_ref[...] + gnb_ref[...] + feat(F_BONUS)) * feat(F_G)).astype(ya_ref.dtype)

    @pl.when(pl.program_id(0) == pl.num_programs(0) - 1)
    def _():
        sfin_ref[...] = s_scr[...]


def _wkv_scan(feats, gc, lw):
    t = feats.shape[0]
    vec = pl.BlockSpec((1, D_A), lambda c: (0, 0))
    pair_state = (PAIRS, HEAD_A, LANES)
    rows = SCAN_CHUNKS * CHUNK
    ya, s = pl.pallas_call(
        _scan_kernel,
        out_shape=[jax.ShapeDtypeStruct((t, D_A), BF16), jax.ShapeDtypeStruct(pair_state, F32)],
        grid=(t // rows,),
        in_specs=[pl.BlockSpec((rows, N_FEATS * D_A), lambda c: (c, 0)),
                  pl.BlockSpec((SUBLANES * SCAN_CHUNKS, D_A), lambda c: (c, 0)), vec, vec],
        out_specs=[pl.BlockSpec((rows, D_A), lambda c: (c, 0)), pl.BlockSpec(pair_state, lambda c: (0, 0, 0))],
        scratch_shapes=[pltpu.VMEM(pair_state, F32)],
        compiler_params=_params(("arbitrary",)),
        name="wkv_scan_prompt",
    )(feats, gc, lw["gn_g"], lw["gn_b"])
    s = s.reshape(PAIRS, HEAD_A, 2, HEAD_A).transpose(0, 2, 1, 3).reshape(H_A, HEAD_A, HEAD_A)
    return ya, s


def _wkv_step_kernel(s_ref, rows_ref, sn_ref, o_ref):
    ri = lax.broadcasted_iota(jnp.int32, (1, 1, HEAD_A, HEAD_A), 2)
    ci = lax.broadcasted_iota(jnp.int32, (1, 1, HEAD_A, HEAD_A), 3)
    eye = ri == ci
    pt, rt, qt, kt, v, gc = (rows_ref[:, n] for n in range(6))
    s = s_ref[0]
    u = jnp.sum(s * pt, axis=3, keepdims=True)
    v_col = jnp.sum(jnp.where(eye, v, 0.0), axis=3, keepdims=True)
    m = s + u * qt + v_col * kt
    sn_ref[0] = m * gc
    o_col = jnp.sum(m * rt, axis=3, keepdims=True)
    o_ref[...] = jnp.sum(jnp.where(eye, o_col, 0.0), axis=2, keepdims=True)


def _wkv_step(state, pt, rt, qt, kt, v, gc, bt):
    b = pt.shape[0]
    rows = jnp.stack([pt, rt, qt, kt, v, gc], axis=1).reshape(b, 6, H_A, 1, HEAD_A)
    st_spec = pl.BlockSpec((1, bt, H_A, HEAD_A, HEAD_A), lambda i: (0, i, 0, 0, 0))
    sn, o = pl.pallas_call(
        _wkv_step_kernel,
        out_shape=[jax.ShapeDtypeStruct(state.shape, F32), jax.ShapeDtypeStruct((b, H_A, 1, HEAD_A), F32)],
        grid=(b // bt,),
        in_specs=[st_spec, pl.BlockSpec((bt, 6, H_A, 1, HEAD_A), lambda i: (i, 0, 0, 0, 0))],
        out_specs=[st_spec, pl.BlockSpec((bt, H_A, 1, HEAD_A), lambda i: (i, 0, 0, 0))],
        compiler_params=_params(("parallel",)),
        name="wkv_step_sample",
    )(state, rows)
    return sn, o.reshape(b, D_A)


def _post_kernel(o_ref, bonus_ref, g_ref, gng_ref, gnb_ref, hs_ref, hb_ref, y_ref):
    o = o_ref[...]
    hsum, hbc = hs_ref[...], hb_ref[...]
    mu = _head_sums(o, hsum, hbc) * (1.0 / HEAD_A)
    d = o - mu
    var = _head_sums(d * d, hsum, hbc) * (1.0 / HEAD_A)
    o_n = d * lax.rsqrt(var + EPS_GN) * gng_ref[...] + gnb_ref[...]
    y_ref[...] = ((o_n + bonus_ref[...]) * g_ref[...]).astype(y_ref.dtype)


def _rwkv_post(o, bonus, g, lw, tm, name):
    m = o.shape[0]
    big = pl.BlockSpec((tm, D_A), lambda i: (i, 0))
    vec = pl.BlockSpec((1, D_A), lambda i: (0, 0))
    return pl.pallas_call(
        _post_kernel,
        out_shape=jax.ShapeDtypeStruct((m, D_A), BF16),
        grid=(m // tm,),
        in_specs=[big, big, big, vec, vec, pl.BlockSpec((D_A, LANES), lambda i: (0, 0)),
                  pl.BlockSpec((LANES, D_A), lambda i: (0, 0))],
        out_specs=big,
        compiler_params=_params(("parallel",)),
        name=name,
    )(o, bonus, g, lw["gn_g"], lw["gn_b"], lw["hsum"], lw["hbc"])


def _attn_kernel(q_ref, kc_ref, vc_ref, kh_ref, vh_ref, bias_ref, o_ref, l_ref, *, dil, m_blocks, tiled):
    scale = HEAD_B ** -0.5
    n_streams = SUBLANES if tiled else dil
    n_units = n_streams * m_blocks

    stride = SUBLANES if tiled else dil
    if tiled:
        flat = lambda ref: ref.reshape(ref.shape[0] * SUBLANES, LANES)
        q_ref, kc_ref, vc_ref, kh_ref, vh_ref, o_ref, l_ref = map(flat, (q_ref, kc_ref, vc_ref, kh_ref, vh_ref,
                                                                          o_ref, l_ref))

    def rows(ref, r, start, size):
        if stride == 1:
            return ref[pl.ds(start, size), :]
        return ref[pl.ds(r + stride * start, size, stride=stride), :]

    def put(ref, r, start, val):
        if stride == 1:
            ref[pl.ds(start, BLK), :] = val
        else:
            ref[pl.ds(r + stride * start, BLK, stride=stride), :] = val

    qs, ks, vs, units = [], [], [], []
    for r in range(n_streams):
        for mb in range(m_blocks):
            units.append((r, mb))
            qs.append(rows(q_ref, r, BLK * mb, BLK))
            if mb == 0:
                ks.append(jnp.concatenate([rows(kh_ref, r, 0, BLK), rows(kc_ref, r, 0, BLK)], axis=0))
                vs.append(jnp.concatenate([rows(vh_ref, r, 0, BLK), rows(vc_ref, r, 0, BLK)], axis=0))
            else:
                ks.append(rows(kc_ref, r, BLK * (mb - 1), 2 * BLK))
                vs.append(rows(vc_ref, r, BLK * (mb - 1), 2 * BLK))

    def batch(xs, mul=None):
        pre = (lambda x: x) if mul is None else (lambda x: x * mul)
        return jnp.stack([pre(x[:, sub * HEAD_B:(sub + 1) * HEAD_B]).astype(BF16) for sub in range(2) for x in xs],
                         axis=0)

    assert math.frexp(scale)[0] == 0.5
    qb, kb, vb = batch(qs, scale), batch(ks), batch(vs)
    s = lax.dot_general(qb, kb, (((2,), (2,)), ((0,), (0,))), preferred_element_type=F32)
    s = s.reshape(2, n_units, BLK, 2 * BLK) + bias_ref[...][:, None]
    unit = lax.broadcasted_iota(jnp.int32, (1, n_units, 1, 2 * BLK), 1)
    col = lax.broadcasted_iota(jnp.int32, (1, n_units, 1, 2 * BLK), 3)
    no_prev = jnp.logical_and(jnp.logical_and(pl.program_id(0) == 0, (unit & (m_blocks - 1)) == 0), col < BLK)
    s = jnp.where(no_prev, -jnp.inf, s)
    m = jnp.max(s, axis=-1, keepdims=True)
    p = jnp.exp(s - m)
    l = jnp.sum(p, axis=-1, keepdims=True)
    pv = lax.dot_general(p.astype(BF16).reshape(2 * n_units, BLK, 2 * BLK), vb, (((2,), (1,)), ((0,), (0,))),
                         preferred_element_type=F32)
    o = pv.reshape(2, n_units, BLK, HEAD_B) / l
    lse = jnp.broadcast_to(m + jnp.log(l), (2, n_units, BLK, HEAD_B))
    for u, (r, mb) in enumerate(units):
        put(o_ref, r, BLK * mb, jnp.concatenate([o[0, u], o[1, u]], axis=1))
        put(l_ref, r, BLK * mb, jnp.concatenate([lse[0, u], lse[1, u]], axis=1))


def _attn_prompt(proj, bias, gi, rows_per_step):
    t, n_cols = proj.shape
    dil = DILATIONS[gi]
    span = BLK * dil
    tiled = dil % SUBLANES == 0
    if tiled:
        rows_per_step *= dil // SUBLANES
    m_blocks = rows_per_step // span
    cq = (COL_QKV + gi * D_G) // LANES
    ck = (COL_QKV + D_B + gi * D_G) // LANES
    cv = (COL_QKV + 2 * D_B + gi * D_G) // LANES
    bias_block = (2, BLK, 2 * BLK)
    if tiled:
        n_hi = dil // SUBLANES
        src = proj.reshape(t // dil, n_hi, SUBLANES, n_cols)
        grid = (t // rows_per_step, n_hi, H_G // 2)
        l_rows = rows_per_step // dil
        cur = lambda c0: pl.BlockSpec((l_rows, None, SUBLANES, LANES), lambda i, rh, hp: (i, rh, 0, c0 + hp))
        halo = lambda c0: pl.BlockSpec((BLK, None, SUBLANES, LANES),
                                       lambda i, rh, hp: (jnp.maximum(i * m_blocks - 1, 0), rh, 0, c0 + hp))
        out_spec = pl.BlockSpec((l_rows, None, SUBLANES, LANES), lambda i, rh, hp: (i, rh, 0, hp))
        out = jax.ShapeDtypeStruct((t // dil, n_hi, SUBLANES, D_G), F32)
        bias_spec = pl.BlockSpec(bias_block, lambda i, rh, hp: (hp, 0, 0))
    else:
        src = proj
        grid = (t // rows_per_step, H_G // 2)
        cur = lambda c0: pl.BlockSpec((rows_per_step, LANES), lambda i, hp: (i, c0 + hp))
        halo = lambda c0: pl.BlockSpec((span, LANES), lambda i, hp: (jnp.maximum(i * m_blocks - 1, 0), c0 + hp))
        out_spec = pl.BlockSpec((rows_per_step, LANES), lambda i, hp: (i, hp))
        out = jax.ShapeDtypeStruct((t, D_G), F32)
        bias_spec = pl.BlockSpec(bias_block, lambda i, hp: (hp, 0, 0))
    o, l = pl.pallas_call(
        functools.partial(_attn_kernel, dil=dil, m_blocks=m_blocks, tiled=tiled),
        out_shape=[out, out],
        grid=grid,
        in_specs=[cur(cq), cur(ck), cur(cv), halo(ck), halo(cv), bias_spec],
        out_specs=[out_spec, out_spec],
        compiler_params=_params(("parallel",) * len(grid)),
        name=f"attn_prompt_g{gi}",
    )(src, src, src, src, src, bias)
    return o.reshape(t, D_G), l.reshape(t, D_G)


def _attn_step_kernel(q_ref, kn_ref, vn_ref, c_ref, bias_ref, bias0_ref, o_ref, l_ref):
    scale = HEAD_B ** -0.5
    ri = lax.broadcasted_iota(jnp.int32, (1, 1, HEAD_B, HEAD_B), 2)
    ci = lax.broadcasted_iota(jnp.int32, (1, 1, HEAD_B, HEAD_B), 3)
    eye = ri == ci
    q = q_ref[...]
    q_col = jnp.sum(jnp.where(eye, q, 0.0), axis=3, keepdims=True)
    s = jnp.sum(c_ref[:, 0] * q_col, axis=2, keepdims=True) * scale + bias_ref[...]
    s_new = jnp.sum(q * kn_ref[...], axis=3, keepdims=True) * scale + bias0_ref[...]
    m = jnp.maximum(jnp.max(s, axis=3, keepdims=True), s_new)
    p = jnp.exp(s - m)
    p_new = jnp.exp(s_new - m)
    l = jnp.sum(p, axis=3, keepdims=True) + p_new
    o_col = jnp.sum(c_ref[:, 1] * p, axis=3, keepdims=True)
    o_row = jnp.sum(jnp.where(eye, o_col, 0.0), axis=2, keepdims=True)
    o_ref[...] = (o_row + p_new * vn_ref[...]) / l
    l_ref[...] = jnp.broadcast_to(m + jnp.log(l), o_ref.shape)


def _attn_step(q, k_new, v_new, cache, bias, bias0, gi):
    b = q.shape[0]
    w = cache.shape[2]
    bt = max(1, min(b, STEP_CACHE_WORDS // (2 * D_G * w)))
    cache_t = jnp.transpose(cache, (0, 1, 3, 4, 5, 2))
    vec = pl.BlockSpec((bt, H_G, 1, HEAD_B), lambda i: (i, 0, 0, 0))
    out = jax.ShapeDtypeStruct((b, H_G, 1, HEAD_B), F32)
    o, l = pl.pallas_call(
        _attn_step_kernel,
        out_shape=[out, out],
        grid=(b // bt,),
        in_specs=[vec, vec, vec,
                  pl.BlockSpec((None, bt, 2, H_G, HEAD_B, w), lambda i: (0, i, 0, 0, 0, 0)),
                  pl.BlockSpec((H_G, 1, w), lambda i: (0, 0, 0)),
                  pl.BlockSpec((H_G, 1, 1), lambda i: (0, 0, 0))],
        out_specs=[vec, vec],
        compiler_params=_params(("parallel",)),
        name=f"attn_step_g{gi}",
    )(q, k_new, v_new, cache_t, bias, bias0)
    return o.reshape(b, D_G), l.reshape(b, D_G)


def _mix_kernel(ya_ref, o1_ref, o2_ref, o3_ref, l1_ref, l2_ref, l3_ref, wa_ref, wb_ref, ga_ref, gb_ref, o_ref):
    l1, l2, l3 = l1_ref[...], l2_ref[...], l3_ref[...]
    m = jnp.maximum(jnp.maximum(l1, l2), l3)
    e1, e2, e3 = jnp.exp(l1 - m), jnp.exp(l2 - m), jnp.exp(l3 - m)
    den = e1 + e2 + e3
    yb = ((e1 / den) * o1_ref[...] + (e2 / den) * o2_ref[...] + (e3 / den) * o3_ref[...]).astype(BF16)
    mixed = ga_ref[...] * _dot_bf16(ya_ref[...], wa_ref[...]) + gb_ref[...] * _dot_bf16(yb, wb_ref[...])
    o_ref[...] = mixed.astype(o_ref.dtype)


def _mix(ya, outs, lses, wa, wb, proj, tm, name):
    m = ya.shape[0]
    n = D_MODEL
    assert COL_GATE % n == 0
    ga0 = COL_GATE // n
    grp = pl.BlockSpec((tm, D_G), lambda i: (i, 0))
    resident = lambda rows: pl.BlockSpec((rows, n), lambda i: (0, 0), pipeline_mode=pl.Buffered(1))
    return pl.pallas_call(
        _mix_kernel,
        out_shape=jax.ShapeDtypeStruct((m, n), BF16),
        grid=(m // tm,),
        in_specs=[pl.BlockSpec((tm, D_A), lambda i: (i, 0))] + [grp] * 6 + [
            resident(D_A), resident(D_G),
            pl.BlockSpec((tm, n), lambda i: (i, ga0)),
            pl.BlockSpec((tm, n), lambda i: (i, ga0 + 1)),
        ],
        out_specs=pl.BlockSpec((tm, n), lambda i: (i, 0)),
        compiler_params=_params(("parallel",)),
        name=name,
    )(ya, *outs, *lses, wa, wb, proj, proj)


def _wo_kernel(a_ref, w_ref, x_ref, g_ref, h_ref, hn_ref):
    h = x_ref[...] + _dot_bf16(a_ref[...], w_ref[...])
    h_ref[...] = h
    hn_ref[...] = _rms(h, g_ref[...]).astype(BF16)


def _wo(a, w, x, g, tm, name):
    m, k = a.shape
    n = w.shape[1]
    row = pl.BlockSpec((tm, n), lambda i: (i, 0))
    return pl.pallas_call(
        _wo_kernel,
        out_shape=[jax.ShapeDtypeStruct((m, n), F32), jax.ShapeDtypeStruct((m, n), BF16)],
        grid=(m // tm,),
        in_specs=[pl.BlockSpec((tm, k), lambda i: (i, 0)), pl.BlockSpec((k, n), lambda i: (0, 0)), row,
                  pl.BlockSpec((1, n), lambda i: (0, 0))],
        out_specs=[row, row],
        compiler_params=_params(("parallel",)),
        name=name,
    )(a, w, x, g.reshape(1, n))


def _gelu(x):
    return 0.5 * x * (1.0 + lax.erf(x * (1.0 / math.sqrt(2.0))))


def _ffn_up_kernel(h_ref, w1_ref, w2_ref, cw1_ref, cw2_ref, cb1_ref, cb2_ref, act_ref, t1_ref, t2_ref,
                   c1_scr, c2_scr, w1_scr, w2_scr):
    tm = h_ref.shape[0]

    @pl.when(pl.program_id(1) == 0)
    def _():
        c1_scr[...] = jnp.zeros_like(c1_scr)
        c2_scr[...] = jnp.zeros_like(c2_scr)
        w1_scr[...] = w1_ref[...].astype(BF16)
        w2_scr[...] = w2_ref[...].astype(BF16)

    a = h_ref[...]
    row = lax.broadcasted_iota(jnp.int32, (tm, 1), 0)

    def conv(u, carry_ref, cw_ref, cb_ref):
        m1 = jnp.where(row == 0, carry_ref[7:8, :], pltpu.roll(u, 1, axis=0))
        m2 = jnp.where(row == 0, carry_ref[6:7, :], jnp.where(row == 1, carry_ref[7:8, :], pltpu.roll(u, 2, axis=0)))
        return cb_ref[...] + cw_ref[0:1, :] * m2 + cw_ref[1:2, :] * m1 + cw_ref[2:3, :] * u

    u1 = _dot_bf16(a, w1_scr[...])
    u2 = _dot_bf16(a, w2_scr[...])
    c1 = conv(u1, c1_scr, cw1_ref, cb1_ref)
    c2 = conv(u2, c2_scr, cw2_ref, cb2_ref)
    act_ref[...] = (_gelu(c1) * c2).astype(act_ref.dtype)
    c1_scr[...] = u1[tm - 8:tm]
    c2_scr[...] = u2[tm - 8:tm]
    t1_ref[...] = u1[tm - 8:tm]
    t2_ref[...] = u2[tm - 8:tm]


def _ffn_up_prompt(hn, w_up, conv_w, conv_b, tm, tn):
    m = hn.shape[0]
    nj = D_FF // tn
    tail = jax.ShapeDtypeStruct((8, D_FF), F32)
    return pl.pallas_call(
        _ffn_up_kernel,
        out_shape=[jax.ShapeDtypeStruct((m, D_FF), BF16), tail, tail],
        grid=(nj, m // tm),
        in_specs=[
            pl.BlockSpec((tm, D_MODEL), lambda j, i: (i, 0)),
            pl.BlockSpec((D_MODEL, tn), lambda j, i: (0, j)),
            pl.BlockSpec((D_MODEL, tn), lambda j, i: (0, nj + j)),
            pl.BlockSpec((CONV_W, tn), lambda j, i: (0, j)),
            pl.BlockSpec((CONV_W, tn), lambda j, i: (0, nj + j)),
            pl.BlockSpec((1, tn), lambda j, i: (0, j)),
            pl.BlockSpec((1, tn), lambda j, i: (0, nj + j)),
        ],
        out_specs=[pl.BlockSpec((tm, tn), lambda j, i: (i, j)),
                   pl.BlockSpec((8, tn), lambda j, i: (0, j)),
                   pl.BlockSpec((8, tn), lambda j, i: (0, j))],
        scratch_shapes=[pltpu.VMEM((8, tn), F32), pltpu.VMEM((8, tn), F32),
                        pltpu.VMEM((D_MODEL, tn), BF16), pltpu.VMEM((D_MODEL, tn), BF16)],
        compiler_params=_params(("arbitrary", "arbitrary")),
        name="ffn_up_prompt",
    )(hn, w_up, w_up, conv_w, conv_w, conv_b, conv_b)


def _ffn_act_step_kernel(up_ref, prev_ref, cw_ref, cb_ref, act_ref):
    up = up_ref[...]
    w = 2 * D_FF
    c = cb_ref[...] + cw_ref[0:1, :] * prev_ref[:, 0:w] + cw_ref[1:2, :] * prev_ref[:, w:2 * w] + cw_ref[2:3, :] * up
    act_ref[...] = (_gelu(c[:, 0:D_FF]) * c[:, D_FF:w]).astype(act_ref.dtype)


def _ffn_act_step(up, conv_prev, conv_w, conv_b):
    b = up.shape[0]
    return pl.pallas_call(
        _ffn_act_step_kernel,
        out_shape=jax.ShapeDtypeStruct((b, D_FF), BF16),
        compiler_params=pltpu.CompilerParams(vmem_limit_bytes=VMEM_LIMIT),
        name="ffn_act_sample",
    )(up, conv_prev.reshape(b, (CONV_W - 1) * 2 * D_FF), conv_w, conv_b)


def _down_kernel(a_ref, w_ref, h_ref, g_ref, o_ref, acc_ref):
    @pl.when(pl.program_id(1) == 0)
    def _():
        acc_ref[...] = h_ref[...]

    out = acc_ref[...] + _dot_bf16(a_ref[...], w_ref[...])
    acc_ref[...] = out
    inv = lax.rsqrt(jnp.mean(out * out, axis=-1, keepdims=True) + EPS_RMS)
    o_ref[...] = out * inv * g_ref[...]


def _ffn_down(act, w_down, h, g, tm, tk, name):
    m = act.shape[0]
    nk = D_FF // tk
    return pl.pallas_call(
        _down_kernel,
        out_shape=jax.ShapeDtypeStruct((m, D_MODEL), F32),
        grid=(m // tm, nk),
        in_specs=[
            pl.BlockSpec((tm, tk), lambda i, k: (i, k)),
            pl.BlockSpec((tk, D_MODEL), lambda i, k: (k, 0)),
            pl.BlockSpec((tm, D_MODEL), lambda i, k: (i, 0)),
            pl.BlockSpec((1, D_MODEL), lambda i, k: (0, 0)),
        ],
        out_specs=pl.BlockSpec((tm, D_MODEL), lambda i, k: (i, 0)),
        scratch_shapes=[pltpu.VMEM((tm, D_MODEL), F32)],
        compiler_params=_params(("parallel", "arbitrary")),
        name=name,
    )(act, w_down, h, g.reshape(1, D_MODEL))


def _rel_bucket(dist):
    max_exact = N_BUCKETS // 2
    d_f = np.maximum(dist, 1).astype(np.float64)
    scaled = np.log(d_f / max_exact) / math.log(MAX_DISTANCE / max_exact) * (N_BUCKETS - max_exact)
    large = np.minimum(max_exact + scaled.astype(np.int64), N_BUCKETS - 1)
    return np.where(dist < max_exact, dist, large).astype(np.int32)


def _bias_rows(tab, dist):
    onehot = (jnp.asarray(_rel_bucket(dist))[None, :] == jnp.arange(N_BUCKETS)[:, None]).astype(F32)
    return jnp.dot(tab.T, onehot, precision=HIGHEST)


def _bias_tables(rel_bias, gi):
    dil = DILATIONS[gi]
    win = WINDOWS[gi]
    reach = win // dil
    tab = rel_bias[:, gi * H_G:(gi + 1) * H_G]
    qi = np.arange(BLK)[:, None]
    ki = np.arange(2 * BLK)[None, :]
    rel = qi + BLK - ki
    blk = _bias_rows(tab, (dil * np.maximum(rel, 0)).reshape(-1)).reshape(H_G, BLK, 2 * BLK)
    blk = jnp.where(((rel >= 0) & (rel <= reach))[None], blk, -jnp.inf)
    back = win - np.arange(win)
    step = jnp.where((back % dil == 0)[None, :], _bias_rows(tab, back), -jnp.inf)[:, None, :]
    step0 = _bias_rows(tab, np.zeros((1,), np.int64))[:, None, :]
    return blk, step, step0


def _tiles(m, prompt):
    if prompt:
        return dict(tag="prompt", proj_m=1024, prep_m=2 * CHUNK, attn_rows=2048, mix_m=512, wo_m=512, up_m=1024,
                    up_n=512, down_m=512, down_k=D_FF // 2)
    return dict(tag="sample", proj_m=m, prep_m=m, step_b=4, post_m=m, mix_m=m, wo_m=m, up_n=512, down_m=m,
                down_k=D_FF // 2)


def _layer(x, lw, bias_blk, prompt, state=None, proj=None):
    m = x.shape[0]
    tiles = _tiles(m, prompt)
    if proj is None:
        proj = _proj(x, lw["norm1_g"], lw["w_in"], lw["in_bias"], tiles["proj_m"], "proj_" + tiles["tag"])

    if prompt:
        feats, gc = _rwkv_prep(proj, None, tiles["prep_m"], CHUNK, lw, "rwkv_prep_prompt")
        ya, wkv_new = _wkv_scan(feats, gc, lw)
        wkv_new = wkv_new[None, None]
    else:
        feats, gc = _rwkv_prep(proj, state["shift"], tiles["prep_m"], 1, lw, "rwkv_prep_sample")
        f = lambda n: feats[:, n * D_A:(n + 1) * D_A]
        wkv_new, o = _wkv_step(state["wkv"], f(F_PT), f(F_RT), f(F_QT), f(F_KT), f(F_V), gc, tiles["step_b"])
        ya = _rwkv_post(o, f(F_BONUS), f(F_G), lw, tiles["post_m"], "rwkv_post_" + tiles["tag"])

    outs, lses = [], []
    for gi in range(N_GROUPS):
        if prompt:
            o_g, l_g = _attn_prompt(proj, bias_blk[gi][0], gi, tiles["attn_rows"])
        else:
            sl = lambda c0: proj[:, c0 + gi * D_G:c0 + (gi + 1) * D_G].reshape(m, H_G, 1, HEAD_B)
            o_g, l_g = _attn_step(sl(COL_QKV), sl(COL_QKV + D_B), sl(COL_QKV + 2 * D_B), state["win"][gi],
                                  bias_blk[gi][1], bias_blk[gi][2], gi)
        outs.append(o_g)
        lses.append(l_g)
    mixed = _mix(ya, outs, lses, lw["w_out_a"], lw["w_out_b"], proj, tiles["mix_m"], "mix_" + tiles["tag"])
    h, hn = _wo(mixed, lw["w_o"], x, lw["norm2_g"], tiles["wo_m"], "wo_" + tiles["tag"])

    if prompt:
        act, t1, t2 = _ffn_up_prompt(hn, lw["w_up"], lw["conv_w"], lw["conv_b"], tiles["up_m"], tiles["up_n"])
        conv_new = jnp.concatenate([t1[6:8], t2[6:8]], axis=1)[None, None]
    else:
        up = _matmul(hn, lw["w_up"], m, tiles["up_n"], "up_sample")
        act = _ffn_act_step(up, state["conv"], lw["conv_w"], lw["conv_b"])
        conv_new = jnp.concatenate([state["conv"][:, 1:], up[:, None, :]], axis=1)[None]
    y = _ffn_down(act, lw["w_down"], h, lw["normf_g"], tiles["down_m"], tiles["down_k"], "down_" + tiles["tag"])
    return y, proj, wkv_new, conv_new


def kernel(x_prompt, x_sample, state_wkv, state_shift, state_ffn_conv, cache_win1, cache_win2, cache_win3, rel_bias,
           norm1_g, w_in, gate_b, mu_shift, w0, w_up_decay, a0, w_up_aaa, w_up_gate, k_k, k_a, r_k, gn_g, gn_b,
           w_out_a, w_out_b, w_o, norm2_g, w_up, conv_w, conv_b, w_down, normf_g):
    row = lambda a: a.reshape(1, -1)
    w_lora = jnp.zeros((D_LORA, 3 * D_A), F32)
    w_lora = w_lora.at[0:D_DECAY_LORA, 0:D_A].set(w_up_decay[0])
    w_lora = w_lora.at[D_DECAY_LORA:D_DECAY_LORA + D_AAA_LORA, D_A:2 * D_A].set(w_up_aaa[0])
    w_lora = w_lora.at[D_DECAY_LORA + D_AAA_LORA:, 2 * D_A:].set(w_up_gate[0])
    head = np.arange(D_A) // HEAD_A
    lw = dict(
        norm1_g=norm1_g[0], norm2_g=norm2_g[0], normf_g=normf_g,
        in_bias=jnp.concatenate([jnp.zeros((1, COL_GATE), F32), row(gate_b[0])], axis=1),
        mu=row(mu_shift[0]), w_lora=jnp.stack(_split2(w_lora)), w0=row(w0[0]), a0=row(a0[0]), k_k=row(k_k[0]), k_a=row(k_a[0]),
        r_k=row(r_k[0]), gn_g=row(gn_g[0]), gn_b=row(gn_b[0]),
        hsum=jnp.asarray(head[:, None] == np.arange(LANES)[None, :], BF16),
        hbc=jnp.asarray(np.arange(LANES)[:, None] == head[None, :], BF16),
        w_out_a=w_out_a[0].astype(BF16), w_out_b=w_out_b[0].astype(BF16), w_o=w_o[0].astype(BF16),
        w_up=w_up[0], conv_w=conv_w[0], conv_b=row(conv_b[0]), w_down=w_down[0].astype(BF16),
    )
    bias = [_bias_tables(rel_bias, gi) for gi in range(N_GROUPS)]

    proj_s, lw["w_in"] = _proj_cast(x_sample[:, 0], norm1_g[0], w_in[0], lw["in_bias"])
    y_p, proj_p, wkv_p, conv_p = _layer(x_prompt[0], lw, bias, prompt=True)

    b = DEC_BATCH
    state = dict(wkv=state_wkv, shift=state_shift.reshape(b, D_SHIFT), conv=state_ffn_conv[0],
                 win=(cache_win1, cache_win2, cache_win3))
    y_s, proj_s, wkv_s, conv_s = _layer(x_sample[:, 0], lw, bias, prompt=False, state=state, proj=proj_s)

    def kv_rows(proj, lo, gi):
        k = proj[lo:, COL_QKV + D_B + gi * D_G:COL_QKV + D_B + (gi + 1) * D_G]
        v = proj[lo:, COL_QKV + 2 * D_B + gi * D_G:COL_QKV + 2 * D_B + (gi + 1) * D_G]
        n = k.shape[0]
        return jnp.stack([k.reshape(n, H_G, HEAD_B), v.reshape(n, H_G, HEAD_B)], axis=1)

    win_p = [kv_rows(proj_p, SEQ - min(WINDOWS[gi], SEQ), gi)[None, None] for gi in range(N_GROUPS)]
    win_s = [kv_rows(proj_s, 0, gi)[None, :, None] for gi in range(N_GROUPS)]
    return (y_p[None], y_s[:, None],
            wkv_p, wkv_s,
            proj_p[SEQ - 1:, 0:D_SHIFT][None, None], proj_s[:, 0:D_SHIFT][None, :, None],
            conv_p, conv_s,
            win_p[0], win_s[0], win_p[1], win_s[1], win_p[2], win_s[2])
```

```python
import functools
import math

import numpy as np
import jax
import jax.numpy as jnp
from jax import lax
from jax.experimental import pallas as pl
from jax.experimental.pallas import tpu as pltpu

F32 = jnp.float32
BF16 = jnp.bfloat16
HIGHEST = lax.Precision.HIGHEST

D_MODEL = 2048
SEQ = 8192
DEC_BATCH = 32
HEAD_A = 64
H_A = 16
D_A = H_A * HEAD_A
D_DECAY_LORA = 96
D_AAA_LORA = 96
D_GATE_LORA = 64
D_LORA = D_DECAY_LORA + D_AAA_LORA + D_GATE_LORA
D_SHIFT = 3 * D_A + D_LORA
EPS_GN = 64e-5
HEAD_B = 64
H_G = 8
D_G = H_G * HEAD_B
WINDOWS = (128, 512, 2048)
DILATIONS = (1, 4, 16)
N_GROUPS = 3
D_B = N_GROUPS * D_G
BLK = 128
N_BUCKETS = 32
MAX_DISTANCE = 2048
COL_QKV = D_SHIFT
COL_GATE_SRC = D_SHIFT + 3 * D_B
D_IN = COL_GATE_SRC + 2 * D_MODEL
PROJ_TN = 1024
COL_GATE = -(-COL_GATE_SRC // PROJ_TN) * PROJ_TN
D_IN_PAD = COL_GATE + 2 * D_MODEL
D_FF = 5632
CONV_W = 3
EPS_RMS = 1e-6
CHUNK = 64
SCAN_CHUNKS = 2
F_RT, F_KT, F_QT, F_PT, F_V, F_BONUS, F_G = range(7)
N_FEATS = 7
LANES = 128
SUBLANES = 8
VMEM_LIMIT = 56 * 1024 * 1024
STEP_CACHE_WORDS = 2 * 1024 * 1024


def _params(sem, vmem=VMEM_LIMIT):
    return pltpu.CompilerParams(dimension_semantics=sem, vmem_limit_bytes=vmem)


def _sigmoid(x):
    return 1.0 / (1.0 + jnp.exp(-x))


def _dot_bf16(a, b):
    return jnp.dot(a, b, preferred_element_type=F32)


def _split2(x):
    hi = x.astype(BF16)
    return hi, (x - hi.astype(F32)).astype(BF16)


def _split3(x):
    hi = x.astype(BF16)
    r1 = x - hi.astype(F32)
    mid = r1.astype(BF16)
    lo = (r1 - mid.astype(F32)).astype(BF16)
    return hi, mid, lo


def _dot_exact_rhs(x, m):
    hi, mid, lo = _split3(x)
    return _dot_bf16(hi, m) + _dot_bf16(mid, m) + _dot_bf16(lo, m)


def _head_sums(x, hsum, hbc):
    return _dot_exact_rhs(_dot_exact_rhs(x, hsum), hbc)


def _dot_exact_lhs(m, x):
    hi, mid, lo = _split3(x)
    return _dot_bf16(m, hi) + _dot_bf16(m, mid) + _dot_bf16(m, lo)


def _rms(x, g):
    return x * lax.rsqrt(jnp.mean(x * x, axis=-1, keepdims=True) + EPS_RMS) * g


def _proj_cast_kernel(x_ref, g_ref, w_ref, b_ref, o_ref, wb_ref, xn_scr, *, gap_block):
    j = pl.program_id(0)

    @pl.when(j == 0)
    def _():
        xn_scr[...] = _rms(x_ref[...], g_ref[...]).astype(BF16)

    wb = jnp.where(j != gap_block, w_ref[...], 0.0).astype(BF16)
    wb_ref[...] = wb
    acc = _dot_bf16(xn_scr[...], wb)
    tn = o_ref.shape[1]
    col = lax.broadcasted_iota(jnp.int32, (1, tn), 1) + j * tn
    o_ref[...] = jnp.where(col >= COL_GATE, _sigmoid(acc + b_ref[...]), acc)


def _proj_cast(x, g, w, bias):
    m, k = x.shape
    tn = COL_GATE - COL_GATE_SRC
    assert tn % LANES == 0 and COL_GATE_SRC % tn == 0
    gap_block = COL_GATE_SRC // tn
    return pl.pallas_call(
        functools.partial(_proj_cast_kernel, gap_block=gap_block),
        out_shape=[jax.ShapeDtypeStruct((m, D_IN_PAD), F32), jax.ShapeDtypeStruct((k, D_IN_PAD), BF16)],
        grid=(D_IN_PAD // tn,),
        in_specs=[
            pl.BlockSpec((m, k), lambda j: (0, 0)),
            pl.BlockSpec((1, k), lambda j: (0, 0)),
            pl.BlockSpec((k, tn), lambda j: (0, jnp.where(j < gap_block, j, jnp.maximum(j - 1, 0)))),
            pl.BlockSpec((1, tn), lambda j: (0, j)),
        ],
        out_specs=[pl.BlockSpec((m, tn), lambda j: (0, j)), pl.BlockSpec((k, tn), lambda j: (0, j))],
        scratch_shapes=[pltpu.VMEM((m, k), BF16)],
        compiler_params=_params(("arbitrary",)),
        name="proj_sample_cast_w_in",
    )(x, g.reshape(1, k), w, bias)


def _proj_kernel(x_ref, g_ref, w_ref, b_ref, o_ref, xn_scr):
    @pl.when(pl.program_id(1) == 0)
    def _():
        xn_scr[...] = _rms(x_ref[...], g_ref[...]).astype(BF16)

    acc = _dot_bf16(xn_scr[...], w_ref[...])
    tn = o_ref.shape[1]
    col = lax.broadcasted_iota(jnp.int32, (1, tn), 1) + pl.program_id(1) * tn
    o_ref[...] = jnp.where(col >= COL_GATE, _sigmoid(acc + b_ref[...]), acc)


def _proj(x, g, w, bias, tm, name):
    m, k = x.shape
    n = w.shape[1]
    return pl.pallas_call(
        _proj_kernel,
        out_shape=jax.ShapeDtypeStruct((m, n), F32),
        grid=(m // tm, n // PROJ_TN),
        in_specs=[
            pl.BlockSpec((tm, k), lambda i, j: (i, 0)),
            pl.BlockSpec((1, k), lambda i, j: (0, 0)),
            pl.BlockSpec((k, PROJ_TN), lambda i, j: (0, j)),
            pl.BlockSpec((1, PROJ_TN), lambda i, j: (0, j)),
        ],
        out_specs=pl.BlockSpec((tm, PROJ_TN), lambda i, j: (i, j)),
        scratch_shapes=[pltpu.VMEM((tm, k), BF16)],
        compiler_params=_params(("parallel", "arbitrary")),
        name=name,
    )(x, g.reshape(1, k), w, bias)


def _matmul_kernel(a_ref, w_ref, o_ref):
    o_ref[...] = _dot_bf16(a_ref[...], w_ref[...].astype(BF16))


def _matmul(a, w, tm, tn, name):
    m, k = a.shape
    n = w.shape[1]
    return pl.pallas_call(
        _matmul_kernel,
        out_shape=jax.ShapeDtypeStruct((m, n), F32),
        grid=(m // tm, n // tn),
        in_specs=[pl.BlockSpec((tm, k), lambda i, j: (i, 0)), pl.BlockSpec((k, tn), lambda i, j: (0, j))],
        out_specs=pl.BlockSpec((tm, tn), lambda i, j: (i, j)),
        compiler_params=_params(("parallel", "arbitrary")),
        name=name,
    )(a, w)


def _prep_kernel(p_ref, prev_ref, mu_ref, wl_ref, w0_ref, a0_ref, kk_ref, ka_ref, rk_ref, lmat_ref, sel_ref,
                 hs_ref, hb_ref, f_ref, gc_ref, *, rows_are_time):
    p = p_ref[...]
    tm = p.shape[0]
    if rows_are_time:
        last = jnp.where(pl.program_id(0) == 0, 0.0, prev_ref[7:8, :])
        row = lax.broadcasted_iota(jnp.int32, (tm, 1), 0)
        prev = jnp.where(row == 0, last, pltpu.roll(p, 1, axis=0))
    else:
        prev = prev_ref[...]
    xm = p + mu_ref[...] * (prev - p)
    r = xm[:, 0:D_A]
    k = xm[:, D_A:2 * D_A]
    v = xm[:, 2 * D_A:3 * D_A]
    xl = xm[:, 3 * D_A:D_SHIFT]
    lane = lax.broadcasted_iota(jnp.int32, xl.shape, 1)
    act = jnp.where(lane < D_DECAY_LORA, jnp.tanh(xl),
                    jnp.where(lane < D_DECAY_LORA + D_AAA_LORA, xl, _sigmoid(xl)))
    act_hi, act_lo = _split2(act)
    w_hi, w_lo = wl_ref[0], wl_ref[1]
    lora = _dot_bf16(act_hi, w_hi) + _dot_bf16(act_hi, w_lo) + _dot_bf16(act_lo, w_hi)
    y = -(w0_ref[...] + lora[:, 0:D_A])
    softplus = jnp.maximum(y, 0.0) + jnp.log(1.0 + jnp.exp(-jnp.abs(y)))
    logw = -jnp.exp(-softplus - 0.5)
    a = _sigmoid(a0_ref[...] + lora[:, D_A:2 * D_A])
    g = lora[:, 2 * D_A:3 * D_A]
    kkr = k * kk_ref[...]
    kp = k * (1.0 + (a - 1.0) * ka_ref[...])
    seg = _head_sums(jnp.concatenate([kkr * kkr, r * kp * rk_ref[...]], axis=0), hs_ref[...], hb_ref[...])
    kk = kkr / jnp.maximum(jnp.sqrt(seg[0:tm]), 1e-12)
    cum = _dot_exact_lhs(lmat_ref[...], logw)
    e_out = jnp.exp(-cum)
    feats = {F_RT: r * jnp.exp(cum), F_KT: kp * e_out, F_QT: kk * a * e_out, F_PT: -kk * jnp.exp(cum - logw),
             F_V: v, F_BONUS: seg[tm:2 * tm] * v, F_G: g}
    for n, val in feats.items():
        f_ref[:, n * D_A:(n + 1) * D_A] = val
    gc_ref[...] = jnp.exp(_dot_exact_lhs(sel_ref[...], cum))


def _rwkv_prep(proj, prev, tm, chunk, lw, name):
    m = proj.shape[0]
    rows_are_time = prev is None
    n_tiles = m // tm
    t = np.arange(tm)
    lmat = ((t[:, None] // chunk == t[None, :] // chunk) & (t[None, :] <= t[:, None])).astype(np.float32)
    sel_stride = 1 if chunk == 1 else 8
    n_sel = sel_stride * (tm // chunk)
    sel = np.zeros((n_sel, tm), np.float32)
    for c in range(tm // chunk):
        sel[sel_stride * c, (c + 1) * chunk - 1] = 1.0
    if rows_are_time:
        prev_arr = proj
        prev_spec = pl.BlockSpec((8, D_SHIFT), lambda i: (jnp.maximum(i * (tm // 8) - 1, 0), 0))
    else:
        prev_arr = prev
        prev_spec = pl.BlockSpec((tm, D_SHIFT), lambda i: (i, 0))
    vec = lambda d: pl.BlockSpec((1, d), lambda i: (0, 0))
    full = lambda a: pl.BlockSpec(a.shape, lambda i: (0, 0))
    lmat_b = jnp.asarray(lmat, BF16)
    sel_b = jnp.asarray(sel, BF16)
    return pl.pallas_call(
        functools.partial(_prep_kernel, rows_are_time=rows_are_time),
        out_shape=[jax.ShapeDtypeStruct((m, N_FEATS * D_A), F32),
                   jax.ShapeDtypeStruct((n_tiles * n_sel, D_A), F32)],
        grid=(n_tiles,),
        in_specs=[pl.BlockSpec((tm, D_SHIFT), lambda i: (i, 0)), prev_spec, vec(D_SHIFT),
                  pl.BlockSpec(lw["w_lora"].shape, lambda i: (0, 0, 0)),
                  vec(D_A), vec(D_A), vec(D_A), vec(D_A), vec(D_A), full(lmat_b), full(sel_b), full(lw["hsum"]),
                  full(lw["hbc"])],
        out_specs=[pl.BlockSpec((tm, N_FEATS * D_A), lambda i: (i, 0)), pl.BlockSpec((n_sel, D_A), lambda i: (i, 0))],
        compiler_params=_params(("parallel",)),
        name=name,
    )(proj, prev_arr, lw["mu"], lw["w_lora"], lw["w0"], lw["a0"], lw["k_k"], lw["k_a"], lw["r_k"], lmat_b, sel_b,
      lw["hsum"], lw["hbc"])


_NN = (((2,), (1,)), ((0,), (0,)))
_NT = (((2,), (2,)), ((0,), (0,)))
PAIRS = H_A // 2
LOG2_HEAD = HEAD_A.bit_length() - 1
assert HEAD_A == 1 << LOG2_HEAD and 2 * HEAD_A == LANES


def _lane_head(n):
    return (lax.broadcasted_iota(jnp.int32, (1, 1, n), 2) >> LOG2_HEAD) & 1


def _keep_head(x, head, which):
    return jnp.where(head == which, x, jnp.zeros_like(x))


def _pdot_nn(a, b):
    ah, al = _split2(a)
    bh, bl = _split2(b)
    head = _lane_head(b.shape[2])
    bd = lambda x: jnp.concatenate([_keep_head(x, head, 0), _keep_head(x, head, 1)], axis=1)
    a_cat = jnp.concatenate([ah, ah, al], axis=2)
    b_cat = jnp.concatenate([bd(bh), bd(bl), bd(bh)], axis=1)
    return lax.dot_general(a_cat, b_cat, _NN, preferred_element_type=F32)


def _pdot_nt(a, bs):
    ah, al = _split2(a)
    head = _lane_head(LANES)
    hi, lo = [], []
    for b in bs:
        bh, bl = _split2(b)
        for which in (0, 1):
            hi.append(_keep_head(bh, head, which))
            lo.append(_keep_head(bl, head, which))
    bh, bl = jnp.concatenate(hi, axis=1), jnp.concatenate(lo, axis=1)
    a_cat = jnp.concatenate([ah, ah, al], axis=2)
    b_cat = jnp.concatenate([bh, bl, bh], axis=2)
    return lax.dot_general(a_cat, b_cat, _NT, preferred_element_type=F32)


def _pdot_tn(x, y):
    nx = x.shape[2]
    xh, xl = _split2(jnp.swapaxes(x, 1, 2))
    yh, yl = _split2(y)
    x_cat = jnp.concatenate([xh, xh, xl, jnp.zeros_like(xl)], axis=2)
    y_cat = jnp.concatenate([yh, yl, yh, jnp.zeros_like(yl)], axis=1)
    tot = lax.dot_general(x_cat, y_cat, _NN, preferred_element_type=F32)
    head = _lane_head(LANES)
    return [jnp.where(head == 0, tot[:, g:g + HEAD_A], tot[:, g + HEAD_A:g + LANES]) for g in range(0, nx, LANES)]


def _scan_kernel(f_ref, gc_ref, gng_ref, gnb_ref, ya_ref, sfin_ref, s_scr):
    @pl.when(pl.program_id(0) == 0)
    def _():
        s_scr[...] = jnp.zeros_like(s_scr)

    def pairs(x, rows):
        return jnp.stack([x[c * rows:(c + 1) * rows, i * LANES:(i + 1) * LANES]
                          for c in range(SCAN_CHUNKS) for i in range(PAIRS)], axis=0)

    feat = lambda n: f_ref[:, n * D_A:(n + 1) * D_A]
    p, r, q, k, v = (pairs(feat(n), CHUNK) for n in (F_PT, F_RT, F_QT, F_KT, F_V))
    gam = pairs(gc_ref[...], SUBLANES)[:, 0:1]
    ri = lax.broadcasted_iota(jnp.int32, (1, CHUNK, LANES), 1)
    ci = lax.broadcasted_iota(jnp.int32, (1, CHUNK, LANES), 2) & (HEAD_A - 1)
    strict = ri > ci
    incl = ri >= ci
    eye = (ri == ci).astype(F32)

    gram = _pdot_nt(jnp.concatenate([p, r], axis=1), [q, k])
    a_qp = jnp.where(strict, gram[:, 0:CHUNK, 0:LANES], 0.0)
    a_kp = jnp.where(strict, gram[:, 0:CHUNK, LANES:], 0.0)
    a_rq = jnp.where(incl, gram[:, CHUNK:, 0:LANES], 0.0)
    a_rk = jnp.where(incl, gram[:, CHUNK:, LANES:], 0.0)
    same = lambda log2_bs: (ri >> log2_bs) == (ci >> log2_bs)
    a_d = jnp.where(same(4), a_qp, 0.0)
    tinv = eye + a_d
    pw = _pdot_nn(a_d, a_d)
    for _ in range(2):
        both = _pdot_nn(jnp.concatenate([pw, tinv], axis=1), pw)
        pw = both[:, 0:CHUNK]
        tinv = tinv + both[:, CHUNK:]
    tinv = tinv + _pdot_nn(tinv, pw)
    for log2_bs in (4, 5):
        off = jnp.where(jnp.logical_and(same(log2_bs + 1), jnp.logical_not(same(log2_bs))), a_qp, 0.0)
        tinv = tinv + _pdot_nn(tinv, _pdot_nn(off, tinv))
    av = _pdot_nn(jnp.concatenate([a_kp, a_rk], axis=1), v)
    pw_hat = _pdot_nn(tinv, jnp.concatenate([p, av[:, 0:CHUNK]], axis=2))
    x = _pdot_nn(a_rq, pw_hat)
    r_hat = r + x[:, :, 0:LANES]
    o_loc = x[:, :, LANES:] + av[:, CHUNK:]
    y_p, y_w = _pdot_tn(pw_hat, q)
    (vk,) = _pdot_tn(v, k)
    m_mat = (eye + y_p) * gam
    n_mat = (y_w + vk) * gam

    state = s_scr[...]
    outs = []
    for c in range(SCAN_CHUNKS):
        sl = slice(c * PAIRS, (c + 1) * PAIRS)
        outs.append(_pdot_nt(r_hat[sl], [state]) + o_loc[sl])
        state = _pdot_nn(state, m_mat[sl]) + n_mat[sl]
    s_scr[...] = state
    o = jnp.concatenate(outs, axis=0)

    head = _lane_head(LANES)

    def head_mean(z):
        sums = [jnp.sum(_keep_head(z, head, which), axis=2, keepdims=True) for which in (0, 1)]
        return jnp.where(head == 0, sums[0], sums[1]) * (1.0 / HEAD_A)

    d = o - head_mean(o)
    o_n = d * lax.rsqrt(head_mean(d * d) + EPS_GN)
    o_n = jnp.concatenate([jnp.concatenate([o_n[c * PAIRS + i] for i in range(PAIRS)], axis=1)
                           for c in range(SCAN_CHUNKS)], axis=0)
    ya_ref[...] = ((o_n * gng_ref[...] + gnb_ref[...] + feat(F_BONUS)) * feat(F_G)).astype(ya_ref.dtype)

    @pl.when(pl.program_id(0) == pl.num_programs(0) - 1)
    def _():
        sfin_ref[...] = s_scr[...]


def _wkv_scan(feats, gc, lw):
    t = feats.shape[0]
    vec = pl.BlockSpec((1, D_A), lambda c: (0, 0))
    pair_state = (PAIRS, HEAD_A, LANES)
    rows = SCAN_CHUNKS * CHUNK
    ya, s = pl.pallas_call(
        _scan_kernel,
        out_shape=[jax.ShapeDtypeStruct((t, D_A), BF16), jax.ShapeDtypeStruct(pair_state, F32)],
        grid=(t // rows,),
        in_specs=[pl.BlockSpec((rows, N_FEATS * D_A), lambda c: (c, 0)),
                  pl.BlockSpec((SUBLANES * SCAN_CHUNKS, D_A), lambda c: (c, 0)), vec, vec],
        out_specs=[pl.BlockSpec((rows, D_A), lambda c: (c, 0)), pl.BlockSpec(pair_state, lambda c: (0, 0, 0))],
        scratch_shapes=[pltpu.VMEM(pair_state, F32)],
        compiler_params=_params(("arbitrary",)),
        name="wkv_scan_prompt",
    )(feats, gc, lw["gn_g"], lw["gn_b"])
    s = s.reshape(PAIRS, HEAD_A, 2, HEAD_A).transpose(0, 2, 1, 3).reshape(H_A, HEAD_A, HEAD_A)
    return ya, s


def _wkv_step_kernel(s_ref, rows_ref, sn_ref, o_ref):
    ri = lax.broadcasted_iota(jnp.int32, (1, 1, HEAD_A, HEAD_A), 2)
    ci = lax.broadcasted_iota(jnp.int32, (1, 1, HEAD_A, HEAD_A), 3)
    eye = ri == ci
    pt, rt, qt, kt, v, gc = (rows_ref[:, n] for n in range(6))
    s = s_ref[0]
    u = jnp.sum(s * pt, axis=3, keepdims=True)
    v_col = jnp.sum(jnp.where(eye, v, 0.0), axis=3, keepdims=True)
    m = s + u * qt + v_col * kt
    sn_ref[0] = m * gc
    o_col = jnp.sum(m * rt, axis=3, keepdims=True)
    o_ref[...] = jnp.sum(jnp.where(eye, o_col, 0.0), axis=2, keepdims=True)


def _wkv_step(state, pt, rt, qt, kt, v, gc, bt):
    b = pt.shape[0]
    rows = jnp.stack([pt, rt, qt, kt, v, gc], axis=1).reshape(b, 6, H_A, 1, HEAD_A)
    st_spec = pl.BlockSpec((1, bt, H_A, HEAD_A, HEAD_A), lambda i: (0, i, 0, 0, 0))
    sn, o = pl.pallas_call(
        _wkv_step_kernel,
        out_shape=[jax.ShapeDtypeStruct(state.shape, F32), jax.ShapeDtypeStruct((b, H_A, 1, HEAD_A), F32)],
        grid=(b // bt,),
        in_specs=[st_spec, pl.BlockSpec((bt, 6, H_A, 1, HEAD_A), lambda i: (i, 0, 0, 0, 0))],
        out_specs=[st_spec, pl.BlockSpec((bt, H_A, 1, HEAD_A), lambda i: (i, 0, 0, 0))],
        compiler_params=_params(("parallel",)),
        name="wkv_step_sample",
    )(state, rows)
    return sn, o.reshape(b, D_A)


def _post_kernel(o_ref, bonus_ref, g_ref, gng_ref, gnb_ref, hs_ref, hb_ref, y_ref):
    o = o_ref[...]
    hsum, hbc = hs_ref[...], hb_ref[...]
    mu = _head_sums(o, hsum, hbc) * (1.0 / HEAD_A)
    d = o - mu
    var = _head_sums(d * d, hsum, hbc) * (1.0 / HEAD_A)
    o_n = d * lax.rsqrt(var + EPS_GN) * gng_ref[...] + gnb_ref[...]
    y_ref[...] = ((o_n + bonus_ref[...]) * g_ref[...]).astype(y_ref.dtype)


def _rwkv_post(o, bonus, g, lw, tm, name):
    m = o.shape[0]
    big = pl.BlockSpec((tm, D_A), lambda i: (i, 0))
    vec = pl.BlockSpec((1, D_A), lambda i: (0, 0))
    return pl.pallas_call(
        _post_kernel,
        out_shape=jax.ShapeDtypeStruct((m, D_A), BF16),
        grid=(m // tm,),
        in_specs=[big, big, big, vec, vec, pl.BlockSpec((D_A, LANES), lambda i: (0, 0)),
                  pl.BlockSpec((LANES, D_A), lambda i: (0, 0))],
        out_specs=big,
        compiler_params=_params(("parallel",)),
        name=name,
    )(o, bonus, g, lw["gn_g"], lw["gn_b"], lw["hsum"], lw["hbc"])


def _attn_kernel(q_ref, kc_ref, vc_ref, kh_ref, vh_ref, bias_ref, o_ref, l_ref, *, dil, m_blocks, tiled):
    scale = HEAD_B ** -0.5
    n_streams = SUBLANES if tiled else dil
    n_units = n_streams * m_blocks

    stride = SUBLANES if tiled else dil
    if tiled:
        flat = lambda ref: ref.reshape(ref.shape[0] * SUBLANES, LANES)
        q_ref, kc_ref, vc_ref, kh_ref, vh_ref, o_ref, l_ref = map(flat, (q_ref, kc_ref, vc_ref, kh_ref, vh_ref,
                                                                          o_ref, l_ref))

    def rows(ref, r, start, size):
        if stride == 1:
            return ref[pl.ds(start, size), :]
        return ref[pl.ds(r + stride * start, size, stride=stride), :]

    def put(ref, r, start, val):
        if stride == 1:
            ref[pl.ds(start, BLK), :] = val
        else:
            ref[pl.ds(r + stride * start, BLK, stride=stride), :] = val

    qs, ks, vs, units = [], [], [], []
    for r in range(n_streams):
        for mb in range(m_blocks):
            units.append((r, mb))
            qs.append(rows(q_ref, r, BLK * mb, BLK))
            if mb == 0:
                ks.append(jnp.concatenate([rows(kh_ref, r, 0, BLK), rows(kc_ref, r, 0, BLK)], axis=0))
                vs.append(jnp.concatenate([rows(vh_ref, r, 0, BLK), rows(vc_ref, r, 0, BLK)], axis=0))
            else:
                ks.append(rows(kc_ref, r, BLK * (mb - 1), 2 * BLK))
                vs.append(rows(vc_ref, r, BLK * (mb - 1), 2 * BLK))

    def batch(xs, mul=None):
        pre = (lambda x: x) if mul is None else (lambda x: x * mul)
        return jnp.stack([pre(x[:, sub * HEAD_B:(sub + 1) * HEAD_B]).astype(BF16) for sub in range(2) for x in xs],
                         axis=0)

    assert math.frexp(scale)[0] == 0.5
    qb, kb, vb = batch(qs, scale), batch(ks), batch(vs)
    s = lax.dot_general(qb, kb, (((2,), (2,)), ((0,), (0,))), preferred_element_type=F32)
    s = s.reshape(2, n_units, BLK, 2 * BLK) + bias_ref[...][:, None]
    unit = lax.broadcasted_iota(jnp.int32, (1, n_units, 1, 2 * BLK), 1)
    col = lax.broadcasted_iota(jnp.int32, (1, n_units, 1, 2 * BLK), 3)
    no_prev = jnp.logical_and(jnp.logical_and(pl.program_id(0) == 0, (unit & (m_blocks - 1)) == 0), col < BLK)
    s = jnp.where(no_prev, -jnp.inf, s)
    m = jnp.max(s, axis=-1, keepdims=True)
    p = jnp.exp(s - m)
    l = jnp.sum(p, axis=-1, keepdims=True)
    pv = lax.dot_general(p.astype(BF16).reshape(2 * n_units, BLK, 2 * BLK), vb, (((2,), (1,)), ((0,), (0,))),
                         preferred_element_type=F32)
    o = pv.reshape(2, n_units, BLK, HEAD_B) / l
    lse = jnp.broadcast_to(m + jnp.log(l), (2, n_units, BLK, HEAD_B))
    for u, (r, mb) in enumerate(units):
        put(o_ref, r, BLK * mb, jnp.concatenate([o[0, u], o[1, u]], axis=1))
        put(l_ref, r, BLK * mb, jnp.concatenate([lse[0, u], lse[1, u]], axis=1))


def _attn_prompt(proj, bias, gi, rows_per_step):
    t, n_cols = proj.shape
    dil = DILATIONS[gi]
    span = BLK * dil
    tiled = dil % SUBLANES == 0
    if tiled:
        rows_per_step *= dil // SUBLANES
    m_blocks = rows_per_step // span
    cq = (COL_QKV + gi * D_G) // LANES
    ck = (COL_QKV + D_B + gi * D_G) // LANES
    cv = (COL_QKV + 2 * D_B + gi * D_G) // LANES
    bias_block = (2, BLK, 2 * BLK)
    if tiled:
        n_hi = dil // SUBLANES
        src = proj.reshape(t // dil, n_hi, SUBLANES, n_cols)
        grid = (t // rows_per_step, n_hi, H_G // 2)
        l_rows = rows_per_step // dil
        cur = lambda c0: pl.BlockSpec((l_rows, None, SUBLANES, LANES), lambda i, rh, hp: (i, rh, 0, c0 + hp))
        halo = lambda c0: pl.BlockSpec((BLK, None, SUBLANES, LANES),
                                       lambda i, rh, hp: (jnp.maximum(i * m_blocks - 1, 0), rh, 0, c0 + hp))
        out_spec = pl.BlockSpec((l_rows, None, SUBLANES, LANES), lambda i, rh, hp: (i, rh, 0, hp))
        out = jax.ShapeDtypeStruct((t // dil, n_hi, SUBLANES, D_G), F32)
        bias_spec = pl.BlockSpec(bias_block, lambda i, rh, hp: (hp, 0, 0))
    else:
        src = proj
        grid = (t // rows_per_step, H_G // 2)
        cur = lambda c0: pl.BlockSpec((rows_per_step, LANES), lambda i, hp: (i, c0 + hp))
        halo = lambda c0: pl.BlockSpec((span, LANES), lambda i, hp: (jnp.maximum(i * m_blocks - 1, 0), c0 + hp))
        out_spec = pl.BlockSpec((rows_per_step, LANES), lambda i, hp: (i, hp))
        out = jax.ShapeDtypeStruct((t, D_G), F32)
        bias_spec = pl.BlockSpec(bias_block, lambda i, hp: (hp, 0, 0))
    o, l = pl.pallas_call(
        functools.partial(_attn_kernel, dil=dil, m_blocks=m_blocks, tiled=tiled),
        out_shape=[out, out],
        grid=grid,
        in_specs=[cur(cq), cur(ck), cur(cv), halo(ck), halo(cv), bias_spec],
        out_specs=[out_spec, out_spec],
        compiler_params=_params(("parallel",) * len(grid)),
        name=f"attn_prompt_g{gi}",
    )(src, src, src, src, src, bias)
    return o.reshape(t, D_G), l.reshape(t, D_G)


def _attn_step_kernel(q_ref, kn_ref, vn_ref, c_ref, bias_ref, bias0_ref, o_ref, l_ref):
    scale = HEAD_B ** -0.5
    ri = lax.broadcasted_iota(jnp.int32, (1, 1, HEAD_B, HEAD_B), 2)
    ci = lax.broadcasted_iota(jnp.int32, (1, 1, HEAD_B, HEAD_B), 3)
    eye = ri == ci
    q = q_ref[...]
    q_col = jnp.sum(jnp.where(eye, q, 0.0), axis=3, keepdims=True)
    s = jnp.sum(c_ref[:, 0] * q_col, axis=2, keepdims=True) * scale + bias_ref[...]
    s_new = jnp.sum(q * kn_ref[...], axis=3, keepdims=True) * scale + bias0_ref[...]
    m = jnp.maximum(jnp.max(s, axis=3, keepdims=True), s_new)
    p = jnp.exp(s - m)
    p_new = jnp.exp(s_new - m)
    l = jnp.sum(p, axis=3, keepdims=True) + p_new
    o_col = jnp.sum(c_ref[:, 1] * p, axis=3, keepdims=True)
    o_row = jnp.sum(jnp.where(eye, o_col, 0.0), axis=2, keepdims=True)
    o_ref[...] = (o_row + p_new * vn_ref[...]) / l
    l_ref[...] = jnp.broadcast_to(m + jnp.log(l), o_ref.shape)


def _attn_step(q, k_new, v_new, cache, bias, bias0, gi):
    b = q.shape[0]
    w = cache.shape[2]
    bt = max(1, min(b, STEP_CACHE_WORDS // (2 * D_G * w)))
    cache_t = jnp.transpose(cache, (0, 1, 3, 4, 5, 2))
    vec = pl.BlockSpec((bt, H_G, 1, HEAD_B), lambda i: (i, 0, 0, 0))
    out = jax.ShapeDtypeStruct((b, H_G, 1, HEAD_B), F32)
    o, l = pl.pallas_call(
        _attn_step_kernel,
        out_shape=[out, out],
        grid=(b // bt,),
        in_specs=[vec, vec, vec,
                  pl.BlockSpec((None, bt, 2, H_G, HEAD_B, w), lambda i: (0, i, 0, 0, 0, 0)),
                  pl.BlockSpec((H_G, 1, w), lambda i: (0, 0, 0)),
                  pl.BlockSpec((H_G, 1, 1), lambda i: (0, 0, 0))],
        out_specs=[vec, vec],
        compiler_params=_params(("parallel",)),
        name=f"attn_step_g{gi}",
    )(q, k_new, v_new, cache_t, bias, bias0)
    return o.reshape(b, D_G), l.reshape(b, D_G)


def _mix_kernel(ya_ref, o1_ref, o2_ref, o3_ref, l1_ref, l2_ref, l3_ref, wa_ref, wb_ref, ga_ref, gb_ref, o_ref):
    l1, l2, l3 = l1_ref[...], l2_ref[...], l3_ref[...]
    m = jnp.maximum(jnp.maximum(l1, l2), l3)
    e1, e2, e3 = jnp.exp(l1 - m), jnp.exp(l2 - m), jnp.exp(l3 - m)
    den = e1 + e2 + e3
    yb = ((e1 / den) * o1_ref[...] + (e2 / den) * o2_ref[...] + (e3 / den) * o3_ref[...]).astype(BF16)
    mixed = ga_ref[...] * _dot_bf16(ya_ref[...], wa_ref[...]) + gb_ref[...] * _dot_bf16(yb, wb_ref[...])
    o_ref[...] = mixed.astype(o_ref.dtype)


def _mix(ya, outs, lses, wa, wb, proj, tm, name):
    m = ya.shape[0]
    n = D_MODEL
    assert COL_GATE % n == 0
    ga0 = COL_GATE // n
    grp = pl.BlockSpec((tm, D_G), lambda i: (i, 0))
    resident = lambda rows: pl.BlockSpec((rows, n), lambda i: (0, 0), pipeline_mode=pl.Buffered(1))
    return pl.pallas_call(
        _mix_kernel,
        out_shape=jax.ShapeDtypeStruct((m, n), BF16),
        grid=(m // tm,),
        in_specs=[pl.BlockSpec((tm, D_A), lambda i: (i, 0))] + [grp] * 6 + [
            resident(D_A), resident(D_G),
            pl.BlockSpec((tm, n), lambda i: (i, ga0)),
            pl.BlockSpec((tm, n), lambda i: (i, ga0 + 1)),
        ],
        out_specs=pl.BlockSpec((tm, n), lambda i: (i, 0)),
        compiler_params=_params(("parallel",)),
        name=name,
    )(ya, *outs, *lses, wa, wb, proj, proj)


def _wo_kernel(a_ref, w_ref, x_ref, g_ref, h_ref, hn_ref):
    h = x_ref[...] + _dot_bf16(a_ref[...], w_ref[...])
    h_ref[...] = h
    hn_ref[...] = _rms(h, g_ref[...]).astype(BF16)


def _wo(a, w, x, g, tm, name):
    m, k = a.shape
    n = w.shape[1]
    row = pl.BlockSpec((tm, n), lambda i: (i, 0))
    return pl.pallas_call(
        _wo_kernel,
        out_shape=[jax.ShapeDtypeStruct((m, n), F32), jax.ShapeDtypeStruct((m, n), BF16)],
        grid=(m // tm,),
        in_specs=[pl.BlockSpec((tm, k), lambda i: (i, 0)), pl.BlockSpec((k, n), lambda i: (0, 0)), row,
                  pl.BlockSpec((1, n), lambda i: (0, 0))],
        out_specs=[row, row],
        compiler_params=_params(("parallel",)),
        name=name,
    )(a, w, x, g.reshape(1, n))


def _gelu(x):
    return 0.5 * x * (1.0 + lax.erf(x * (1.0 / math.sqrt(2.0))))


def _ffn_up_kernel(h_ref, w1_ref, w2_ref, cw1_ref, cw2_ref, cb1_ref, cb2_ref, act_ref, t1_ref, t2_ref,
                   c1_scr, c2_scr, w1_scr, w2_scr):
    tm = h_ref.shape[0]

    @pl.when(pl.program_id(1) == 0)
    def _():
        c1_scr[...] = jnp.zeros_like(c1_scr)
        c2_scr[...] = jnp.zeros_like(c2_scr)
        w1_scr[...] = w1_ref[...].astype(BF16)
        w2_scr[...] = w2_ref[...].astype(BF16)

    a = h_ref[...]
    row = lax.broadcasted_iota(jnp.int32, (tm, 1), 0)

    def conv(u, carry_ref, cw_ref, cb_ref):
        m1 = jnp.where(row == 0, carry_ref[7:8, :], pltpu.roll(u, 1, axis=0))
        m2 = jnp.where(row == 0, carry_ref[6:7, :], jnp.where(row == 1, carry_ref[7:8, :], pltpu.roll(u, 2, axis=0)))
        return cb_ref[...] + cw_ref[0:1, :] * m2 + cw_ref[1:2, :] * m1 + cw_ref[2:3, :] * u

    u1 = _dot_bf16(a, w1_scr[...])
    u2 = _dot_bf16(a, w2_scr[...])
    c1 = conv(u1, c1_scr, cw1_ref, cb1_ref)
    c2 = conv(u2, c2_scr, cw2_ref, cb2_ref)
    act_ref[...] = (_gelu(c1) * c2).astype(act_ref.dtype)
    c1_scr[...] = u1[tm - 8:tm]
    c2_scr[...] = u2[tm - 8:tm]
    t1_ref[...] = u1[tm - 8:tm]
    t2_ref[...] = u2[tm - 8:tm]


def _ffn_up_prompt(hn, w_up, conv_w, conv_b, tm, tn):
    m = hn.shape[0]
    nj = D_FF // tn
    tail = jax.ShapeDtypeStruct((8, D_FF), F32)
    return pl.pallas_call(
        _ffn_up_kernel,
        out_shape=[jax.ShapeDtypeStruct((m, D_FF), BF16), tail, tail],
        grid=(nj, m // tm),
        in_specs=[
            pl.BlockSpec((tm, D_MODEL), lambda j, i: (i, 0)),
            pl.BlockSpec((D_MODEL, tn), lambda j, i: (0, j)),
            pl.BlockSpec((D_MODEL, tn), lambda j, i: (0, nj + j)),
            pl.BlockSpec((CONV_W, tn), lambda j, i: (0, j)),
            pl.BlockSpec((CONV_W, tn), lambda j, i: (0, nj + j)),
            pl.BlockSpec((1, tn), lambda j, i: (0, j)),
            pl.BlockSpec((1, tn), lambda j, i: (0, nj + j)),
        ],
        out_specs=[pl.BlockSpec((tm, tn), lambda j, i: (i, j)),
                   pl.BlockSpec((8, tn), lambda j, i: (0, j)),
                   pl.BlockSpec((8, tn), lambda j, i: (0, j))],
        scratch_shapes=[pltpu.VMEM((8, tn), F32), pltpu.VMEM((8, tn), F32),
                        pltpu.VMEM((D_MODEL, tn), BF16), pltpu.VMEM((D_MODEL, tn), BF16)],
        compiler_params=_params(("arbitrary", "arbitrary")),
        name="ffn_up_prompt",
    )(hn, w_up, w_up, conv_w, conv_w, conv_b, conv_b)


def _ffn_act_step_kernel(up_ref, prev_ref, cw_ref, cb_ref, act_ref):
    up = up_ref[...]
    w = 2 * D_FF
    c = cb_ref[...] + cw_ref[0:1, :] * prev_ref[:, 0:w] + cw_ref[1:2, :] * prev_ref[:, w:2 * w] + cw_ref[2:3, :] * up
    act_ref[...] = (_gelu(c[:, 0:D_FF]) * c[:, D_FF:w]).astype(act_ref.dtype)


def _ffn_act_step(up, conv_prev, conv_w, conv_b):
    b = up.shape[0]
    return pl.pallas_call(
        _ffn_act_step_kernel,
        out_shape=jax.ShapeDtypeStruct((b, D_FF), BF16),
        compiler_params=pltpu.CompilerParams(vmem_limit_bytes=VMEM_LIMIT),
        name="ffn_act_sample",
    )(up, conv_prev.reshape(b, (CONV_W - 1) * 2 * D_FF), conv_w, conv_b)


def _down_kernel(a_ref, w_ref, h_ref, g_ref, o_ref, acc_ref):
    @pl.when(pl.program_id(1) == 0)
    def _():
        acc_ref[...] = h_ref[...]

    out = acc_ref[...] + _dot_bf16(a_ref[...], w_ref[...])
    acc_ref[...] = out
    inv = lax.rsqrt(jnp.mean(out * out, axis=-1, keepdims=True) + EPS_RMS)
    o_ref[...] = out * inv * g_ref[...]


def _ffn_down(act, w_down, h, g, tm, tk, name):
    m = act.shape[0]
    nk = D_FF // tk
    return pl.pallas_call(
        _down_kernel,
        out_shape=jax.ShapeDtypeStruct((m, D_MODEL), F32),
        grid=(m // tm, nk),
        in_specs=[
            pl.BlockSpec((tm, tk), lambda i, k: (i, k)),
            pl.BlockSpec((tk, D_MODEL), lambda i, k: (k, 0)),
            pl.BlockSpec((tm, D_MODEL), lambda i, k: (i, 0)),
            pl.BlockSpec((1, D_MODEL), lambda i, k: (0, 0)),
        ],
        out_specs=pl.BlockSpec((tm, D_MODEL), lambda i, k: (i, 0)),
        scratch_shapes=[pltpu.VMEM((tm, D_MODEL), F32)],
        compiler_params=_params(("parallel", "arbitrary")),
        name=name,
    )(act, w_down, h, g.reshape(1, D_MODEL))


def _rel_bucket(dist):
    max_exact = N_BUCKETS // 2
    d_f = np.maximum(dist, 1).astype(np.float64)
    scaled = np.log(d_f / max_exact) / math.log(MAX_DISTANCE / max_exact) * (N_BUCKETS - max_exact)
    large = np.minimum(max_exact + scaled.astype(np.int64), N_BUCKETS - 1)
    return np.where(dist < max_exact, dist, large).astype(np.int32)


def _bias_rows(tab, dist):
    onehot = (jnp.asarray(_rel_bucket(dist))[None, :] == jnp.arange(N_BUCKETS)[:, None]).astype(F32)
    return jnp.dot(tab.T, onehot, precision=HIGHEST)


def _bias_tables(rel_bias, gi):
    dil = DILATIONS[gi]
    win = WINDOWS[gi]
    reach = win // dil
    tab = rel_bias[:, gi * H_G:(gi + 1) * H_G]
    qi = np.arange(BLK)[:, None]
    ki = np.arange(2 * BLK)[None, :]
    rel = qi + BLK - ki
    blk = _bias_rows(tab, (dil * np.maximum(rel, 0)).reshape(-1)).reshape(H_G, BLK, 2 * BLK)
    blk = jnp.where(((rel >= 0) & (rel <= reach))[None], blk, -jnp.inf)
    back = win - np.arange(win)
    step = jnp.where((back % dil == 0)[None, :], _bias_rows(tab, back), -jnp.inf)[:, None, :]
    step0 = _bias_rows(tab, np.zeros((1,), np.int64))[:, None, :]
    return blk, step, step0


def _tiles(m, prompt):
    if prompt:
        return dict(tag="prompt", proj_m=1024, prep_m=2 * CHUNK, attn_rows=2048, mix_m=512, wo_m=512, up_m=1024,
                    up_n=512, down_m=512, down_k=D_FF // 2)
    return dict(tag="sample", proj_m=m, prep_m=m, step_b=4, post_m=m, mix_m=m, wo_m=m, up_n=512, down_m=m,
                down_k=D_FF // 2)


def _layer(x, lw, bias_blk, prompt, state=None, proj=None):
    m = x.shape[0]
    tiles = _tiles(m, prompt)
    if proj is None:
        proj = _proj(x, lw["norm1_g"], lw["w_in"], lw["in_bias"], tiles["proj_m"], "proj_" + tiles["tag"])

    if prompt:
        feats, gc = _rwkv_prep(proj, None, tiles["prep_m"], CHUNK, lw, "rwkv_prep_prompt")
        ya, wkv_new = _wkv_scan(feats, gc, lw)
        wkv_new = wkv_new[None, None]
    else:
        feats, gc = _rwkv_prep(proj, state["shift"], tiles["prep_m"], 1, lw, "rwkv_prep_sample")
        f = lambda n: feats[:, n * D_A:(n + 1) * D_A]
        wkv_new, o = _wkv_step(state["wkv"], f(F_PT), f(F_RT), f(F_QT), f(F_KT), f(F_V), gc, tiles["step_b"])
        ya = _rwkv_post(o, f(F_BONUS), f(F_G), lw, tiles["post_m"], "rwkv_post_" + tiles["tag"])

    outs, lses = [], []
    for gi in range(N_GROUPS):
        if prompt:
            o_g, l_g = _attn_prompt(proj, bias_blk[gi][0], gi, tiles["attn_rows"])
        else:
            sl = lambda c0: proj[:, c0 + gi * D_G:c0 + (gi + 1) * D_G].reshape(m, H_G, 1, HEAD_B)
            o_g, l_g = _attn_step(sl(COL_QKV), sl(COL_QKV + D_B), sl(COL_QKV + 2 * D_B), state["win"][gi],
                                  bias_blk[gi][1], bias_blk[gi][2], gi)
        outs.append(o_g)
        lses.append(l_g)
    mixed = _mix(ya, outs, lses, lw["w_out_a"], lw["w_out_b"], proj, tiles["mix_m"], "mix_" + tiles["tag"])
    h, hn = _wo(mixed, lw["w_o"], x, lw["norm2_g"], tiles["wo_m"], "wo_" + tiles["tag"])

    if prompt:
        act, t1, t2 = _ffn_up_prompt(hn, lw["w_up"], lw["conv_w"], lw["conv_b"], tiles["up_m"], tiles["up_n"])
        conv_new = jnp.concatenate([t1[6:8], t2[6:8]], axis=1)[None, None]
    else:
        up = _matmul(hn, lw["w_up"], m, tiles["up_n"], "up_sample")
        act = _ffn_act_step(up, state["conv"], lw["conv_w"], lw["conv_b"])
        conv_new = jnp.concatenate([state["conv"][:, 1:], up[:, None, :]], axis=1)[None]
    y = _ffn_down(act, lw["w_down"], h, lw["normf_g"], tiles["down_m"], tiles["down_k"], "down_" + tiles["tag"])
    return y, proj, wkv_new, conv_new


def kernel(x_prompt, x_sample, state_wkv, state_shift, state_ffn_conv, cache_win1, cache_win2, cache_win3, rel_bias,
           norm1_g, w_in, gate_b, mu_shift, w0, w_up_decay, a0, w_up_aaa, w_up_gate, k_k, k_a, r_k, gn_g, gn_b,
           w_out_a, w_out_b, w_o, norm2_g, w_up, conv_w, conv_b, w_down, normf_g):
    row = lambda a: a.reshape(1, -1)
    w_lora = jnp.zeros((D_LORA, 3 * D_A), F32)
    w_lora = w_lora.at[0:D_DECAY_LORA, 0:D_A].set(w_up_decay[0])
    w_lora = w_lora.at[D_DECAY_LORA:D_DECAY_LORA + D_AAA_LORA, D_A:2 * D_A].set(w_up_aaa[0])
    w_lora = w_lora.at[D_DECAY_LORA + D_AAA_LORA:, 2 * D_A:].set(w_up_gate[0])
    head = np.arange(D_A) // HEAD_A
    lw = dict(
        norm1_g=norm1_g[0], norm2_g=norm2_g[0], normf_g=normf_g,
        in_bias=jnp.concatenate([jnp.zeros((1, COL_GATE), F32), row(gate_b[0])], axis=1),
        mu=row(mu_shift[0]), w_lora=jnp.stack(_split2(w_lora)), w0=row(w0[0]), a0=row(a0[0]), k_k=row(k_k[0]), k_a=row(k_a[0]),
        r_k=row(r_k[0]), gn_g=row(gn_g[0]), gn_b=row(gn_b[0]),
        hsum=jnp.asarray(head[:, None] == np.arange(LANES)[None, :], BF16),
        hbc=jnp.asarray(np.arange(LANES)[:, None] == head[None, :], BF16),
        w_out_a=w_out_a[0].astype(BF16), w_out_b=w_out_b[0].astype(BF16), w_o=w_o[0].astype(BF16),
        w_up=w_up[0], conv_w=conv_w[0], conv_b=row(conv_b[0]), w_down=w_down[0].astype(BF16),
    )
    bias = [_bias_tables(rel_bias, gi) for gi in range(N_GROUPS)]

    proj_s, lw["w_in"] = _proj_cast(x_sample[:, 0], norm1_g[0], w_in[0], lw["in_bias"])
    y_p, proj_p, wkv_p, conv_p = _layer(x_prompt[0], lw, bias, prompt=True)

    b = DEC_BATCH
    state = dict(wkv=state_wkv, shift=state_shift.reshape(b, D_SHIFT), conv=state_ffn_conv[0],
                 win=(cache_win1, cache_win2, cache_win3))
    y_s, proj_s, wkv_s, conv_s = _layer(x_sample[:, 0], lw, bias, prompt=False, state=state, proj=proj_s)

    def kv_rows(proj, lo, gi):
        k = proj[lo:, COL_QKV + D_B + gi * D_G:COL_QKV + D_B + (gi + 1) * D_G]
        v = proj[lo:, COL_QKV + 2 * D_B + gi * D_G:COL_QKV + 2 * D_B + (gi + 1) * D_G]
        n = k.shape[0]
        return jnp.stack([k.reshape(n, H_G, HEAD_B), v.reshape(n, H_G, HEAD_B)], axis=1)

    win_p = [kv_rows(proj_p, SEQ - min(WINDOWS[gi], SEQ), gi)[None, None] for gi in range(N_GROUPS)]
    win_s = [kv_rows(proj_s, 0, gi)[None, :, None] for gi in range(N_GROUPS)]
    return (y_p[None], y_s[:, None],
            wkv_p, wkv_s,
            proj_p[SEQ - 1:, 0:D_SHIFT][None, None], proj_s[:, 0:D_SHIFT][None, :, None],
            conv_p, conv_s,
            win_p[0], win_s[0], win_p[1], win_s[1], win_p[2], win_s[2])
```

```python
import functools
import math

import numpy as np
import jax
import jax.numpy as jnp
from jax import lax
from jax.experimental import pallas as pl
from jax.experimental.pallas import tpu as pltpu

F32 = jnp.float32
BF16 = jnp.bfloat16
HIGHEST = lax.Precision.HIGHEST

D_MODEL = 2048
SEQ = 8192
DEC_BATCH = 32
HEAD_A = 64
H_A = 16
D_A = H_A * HEAD_A
D_DECAY_LORA = 96
D_AAA_LORA = 96
D_GATE_LORA = 64
D_LORA = D_DECAY_LORA + D_AAA_LORA + D_GATE_LORA
D_SHIFT = 3 * D_A + D_LORA
EPS_GN = 64e-5
HEAD_B = 64
H_G = 8
D_G = H_G * HEAD_B
WINDOWS = (128, 512, 2048)
DILATIONS = (1, 4, 16)
N_GROUPS = 3
D_B = N_GROUPS * D_G
BLK = 128
N_BUCKETS = 32
MAX_DISTANCE = 2048
COL_QKV = D_SHIFT
COL_GATE_SRC = D_SHIFT + 3 * D_B
D_IN = COL_GATE_SRC + 2 * D_MODEL
PROJ_TN = 1024
COL_GATE = -(-COL_GATE_SRC // PROJ_TN) * PROJ_TN
D_IN_PAD = COL_GATE + 2 * D_MODEL
D_FF = 5632
CONV_W = 3
EPS_RMS = 1e-6
CHUNK = 64
SCAN_CHUNKS = 4
F_RT, F_KT, F_QT, F_PT, F_V, F_BONUS, F_G = range(7)
N_FEATS = 7
LANES = 128
SUBLANES = 8
VMEM_LIMIT = 56 * 1024 * 1024
STEP_CACHE_WORDS = 2 * 1024 * 1024


def _params(sem, vmem=VMEM_LIMIT):
    return pltpu.CompilerParams(dimension_semantics=sem, vmem_limit_bytes=vmem)


def _sigmoid(x):
    return 1.0 / (1.0 + jnp.exp(-x))


def _dot_bf16(a, b):
    return jnp.dot(a, b, preferred_element_type=F32)


def _split2(x):
    hi = x.astype(BF16)
    return hi, (x - hi.astype(F32)).astype(BF16)


def _split3(x):
    hi = x.astype(BF16)
    r1 = x - hi.astype(F32)
    mid = r1.astype(BF16)
    lo = (r1 - mid.astype(F32)).astype(BF16)
    return hi, mid, lo


def _dot_exact_rhs(x, m):
    hi, mid, lo = _split3(x)
    return _dot_bf16(hi, m) + _dot_bf16(mid, m) + _dot_bf16(lo, m)


def _head_sums(x, hsum, hbc):
    return _dot_exact_rhs(_dot_exact_rhs(x, hsum), hbc)


def _dot_exact_lhs(m, x):
    hi, mid, lo = _split3(x)
    return _dot_bf16(m, hi) + _dot_bf16(m, mid) + _dot_bf16(m, lo)


def _rms(x, g):
    return x * lax.rsqrt(jnp.mean(x * x, axis=-1, keepdims=True) + EPS_RMS) * g


def _proj_cast_kernel(x_ref, g_ref, w_ref, b_ref, o_ref, wb_ref, xn_scr, *, gap_block):
    j = pl.program_id(0)

    @pl.when(j == 0)
    def _():
        xn_scr[...] = _rms(x_ref[...], g_ref[...]).astype(BF16)

    wb = jnp.where(j != gap_block, w_ref[...], 0.0).astype(BF16)
    wb_ref[...] = wb
    acc = _dot_bf16(xn_scr[...], wb)
    tn = o_ref.shape[1]
    col = lax.broadcasted_iota(jnp.int32, (1, tn), 1) + j * tn
    o_ref[...] = jnp.where(col >= COL_GATE, _sigmoid(acc + b_ref[...]), acc)


def _proj_cast(x, g, w, bias):
    m, k = x.shape
    tn = COL_GATE - COL_GATE_SRC
    assert tn % LANES == 0 and COL_GATE_SRC % tn == 0
    gap_block = COL_GATE_SRC // tn
    return pl.pallas_call(
        functools.partial(_proj_cast_kernel, gap_block=gap_block),
        out_shape=[jax.ShapeDtypeStruct((m, D_IN_PAD), F32), jax.ShapeDtypeStruct((k, D_IN_PAD), BF16)],
        grid=(D_IN_PAD // tn,),
        in_specs=[
            pl.BlockSpec((m, k), lambda j: (0, 0)),
            pl.BlockSpec((1, k), lambda j: (0, 0)),
            pl.BlockSpec((k, tn), lambda j: (0, jnp.where(j < gap_block, j, jnp.maximum(j - 1, 0)))),
            pl.BlockSpec((1, tn), lambda j: (0, j)),
        ],
        out_specs=[pl.BlockSpec((m, tn), lambda j: (0, j)), pl.BlockSpec((k, tn), lambda j: (0, j))],
        scratch_shapes=[pltpu.VMEM((m, k), BF16)],
        compiler_params=_params(("arbitrary",)),
        name="proj_sample_cast_w_in",
    )(x, g.reshape(1, k), w, bias)


def _proj_kernel(x_ref, g_ref, w_ref, b_ref, o_ref, xn_scr):
    @pl.when(pl.program_id(1) == 0)
    def _():
        xn_scr[...] = _rms(x_ref[...], g_ref[...]).astype(BF16)

    acc = _dot_bf16(xn_scr[...], w_ref[...])
    tn = o_ref.shape[1]
    col = lax.broadcasted_iota(jnp.int32, (1, tn), 1) + pl.program_id(1) * tn
    o_ref[...] = jnp.where(col >= COL_GATE, _sigmoid(acc + b_ref[...]), acc)


def _proj(x, g, w, bias, tm, name):
    m, k = x.shape
    n = w.shape[1]
    return pl.pallas_call(
        _proj_kernel,
        out_shape=jax.ShapeDtypeStruct((m, n), F32),
        grid=(m // tm, n // PROJ_TN),
        in_specs=[
            pl.BlockSpec((tm, k), lambda i, j: (i, 0)),
            pl.BlockSpec((1, k), lambda i, j: (0, 0)),
            pl.BlockSpec((k, PROJ_TN), lambda i, j: (0, j)),
            pl.BlockSpec((1, PROJ_TN), lambda i, j: (0, j)),
        ],
        out_specs=pl.BlockSpec((tm, PROJ_TN), lambda i, j: (i, j)),
        scratch_shapes=[pltpu.VMEM((tm, k), BF16)],
        compiler_params=_params(("parallel", "arbitrary")),
        name=name,
    )(x, g.reshape(1, k), w, bias)


def _matmul_kernel(a_ref, w_ref, o_ref):
    o_ref[...] = _dot_bf16(a_ref[...], w_ref[...].astype(BF16))


def _matmul(a, w, tm, tn, name):
    m, k = a.shape
    n = w.shape[1]
    return pl.pallas_call(
        _matmul_kernel,
        out_shape=jax.ShapeDtypeStruct((m, n), F32),
        grid=(m // tm, n // tn),
        in_specs=[pl.BlockSpec((tm, k), lambda i, j: (i, 0)), pl.BlockSpec((k, tn), lambda i, j: (0, j))],
        out_specs=pl.BlockSpec((tm, tn), lambda i, j: (i, j)),
        compiler_params=_params(("parallel", "arbitrary")),
        name=name,
    )(a, w)


def _prep_kernel(p_ref, prev_ref, mu_ref, wl_ref, w0_ref, a0_ref, kk_ref, ka_ref, rk_ref, lmat_ref, sel_ref,
                 hs_ref, hb_ref, f_ref, gc_ref, *, rows_are_time):
    p = p_ref[...]
    tm = p.shape[0]
    if rows_are_time:
        last = jnp.where(pl.program_id(0) == 0, 0.0, prev_ref[7:8, :])
        row = lax.broadcasted_iota(jnp.int32, (tm, 1), 0)
        prev = jnp.where(row == 0, last, pltpu.roll(p, 1, axis=0))
    else:
        prev = prev_ref[...]
    xm = p + mu_ref[...] * (prev - p)
    r = xm[:, 0:D_A]
    k = xm[:, D_A:2 * D_A]
    v = xm[:, 2 * D_A:3 * D_A]
    xl = xm[:, 3 * D_A:D_SHIFT]
    lane = lax.broadcasted_iota(jnp.int32, xl.shape, 1)
    act = jnp.where(lane < D_DECAY_LORA, jnp.tanh(xl),
                    jnp.where(lane < D_DECAY_LORA + D_AAA_LORA, xl, _sigmoid(xl)))
    act_hi, act_lo = _split2(act)
    w_hi, w_lo = wl_ref[0], wl_ref[1]
    lora = _dot_bf16(act_hi, w_hi) + _dot_bf16(act_hi, w_lo) + _dot_bf16(act_lo, w_hi)
    y = -(w0_ref[...] + lora[:, 0:D_A])
    softplus = jnp.maximum(y, 0.0) + jnp.log(1.0 + jnp.exp(-jnp.abs(y)))
    logw = -jnp.exp(-softplus - 0.5)
    a = _sigmoid(a0_ref[...] + lora[:, D_A:2 * D_A])
    g = lora[:, 2 * D_A:3 * D_A]
    kkr = k * kk_ref[...]
    kp = k * (1.0 + (a - 1.0) * ka_ref[...])
    seg = _head_sums(jnp.concatenate([kkr * kkr, r * kp * rk_ref[...]], axis=0), hs_ref[...], hb_ref[...])
    kk = kkr / jnp.maximum(jnp.sqrt(seg[0:tm]), 1e-12)
    cum = _dot_exact_lhs(lmat_ref[...], logw)
    e_out = jnp.exp(-cum)
    feats = {F_RT: r * jnp.exp(cum), F_KT: kp * e_out, F_QT: kk * a * e_out, F_PT: -kk * jnp.exp(cum - logw),
             F_V: v, F_BONUS: seg[tm:2 * tm] * v, F_G: g}
    for n, val in feats.items():
        f_ref[:, n * D_A:(n + 1) * D_A] = val
    gc_ref[...] = jnp.exp(_dot_exact_lhs(sel_ref[...], cum))


def _rwkv_prep(proj, prev, tm, chunk, lw, name):
    m = proj.shape[0]
    rows_are_time = prev is None
    n_tiles = m // tm
    t = np.arange(tm)
    lmat = ((t[:, None] // chunk == t[None, :] // chunk) & (t[None, :] <= t[:, None])).astype(np.float32)
    sel_stride = 1 if chunk == 1 else 8
    n_sel = sel_stride * (tm // chunk)
    sel = np.zeros((n_sel, tm), np.float32)
    for c in range(tm // chunk):
        sel[sel_stride * c, (c + 1) * chunk - 1] = 1.0
    if rows_are_time:
        prev_arr = proj
        prev_spec = pl.BlockSpec((8, D_SHIFT), lambda i: (jnp.maximum(i * (tm // 8) - 1, 0), 0))
    else:
        prev_arr = prev
        prev_spec = pl.BlockSpec((tm, D_SHIFT), lambda i: (i, 0))
    vec = lambda d: pl.BlockSpec((1, d), lambda i: (0, 0))
    full = lambda a: pl.BlockSpec(a.shape, lambda i: (0, 0))
    lmat_b = jnp.asarray(lmat, BF16)
    sel_b = jnp.asarray(sel, BF16)
    return pl.pallas_call(
        functools.partial(_prep_kernel, rows_are_time=rows_are_time),
        out_shape=[jax.ShapeDtypeStruct((m, N_FEATS * D_A), F32),
                   jax.ShapeDtypeStruct((n_tiles * n_sel, D_A), F32)],
        grid=(n_tiles,),
        in_specs=[pl.BlockSpec((tm, D_SHIFT), lambda i: (i, 0)), prev_spec, vec(D_SHIFT),
                  pl.BlockSpec(lw["w_lora"].shape, lambda i: (0, 0, 0)),
                  vec(D_A), vec(D_A), vec(D_A), vec(D_A), vec(D_A), full(lmat_b), full(sel_b), full(lw["hsum"]),
                  full(lw["hbc"])],
        out_specs=[pl.BlockSpec((tm, N_FEATS * D_A), lambda i: (i, 0)), pl.BlockSpec((n_sel, D_A), lambda i: (i, 0))],
        compiler_params=_params(("parallel",)),
        name=name,
    )(proj, prev_arr, lw["mu"], lw["w_lora"], lw["w0"], lw["a0"], lw["k_k"], lw["k_a"], lw["r_k"], lmat_b, sel_b,
      lw["hsum"], lw["hbc"])


_NN = (((2,), (1,)), ((0,), (0,)))
_NT = (((2,), (2,)), ((0,), (0,)))
PAIRS = H_A // 2
LOG2_HEAD = HEAD_A.bit_length() - 1
assert HEAD_A == 1 << LOG2_HEAD and 2 * HEAD_A == LANES


def _lane_head(n):
    return (lax.broadcasted_iota(jnp.int32, (1, 1, n), 2) >> LOG2_HEAD) & 1


def _keep_head(x, head, which):
    return jnp.where(head == which, x, jnp.zeros_like(x))


def _pdot_nn(a, b):
    ah, al = _split2(a)
    bh, bl = _split2(b)
    head = _lane_head(b.shape[2])
    bd = lambda x: jnp.concatenate([_keep_head(x, head, 0), _keep_head(x, head, 1)], axis=1)
    a_cat = jnp.concatenate([ah, ah, al], axis=2)
    b_cat = jnp.concatenate([bd(bh), bd(bl), bd(bh)], axis=1)
    return lax.dot_general(a_cat, b_cat, _NN, preferred_element_type=F32)


def _pdot_nt(a, bs):
    ah, al = _split2(a)
    head = _lane_head(LANES)
    hi, lo = [], []
    for b in bs:
        bh, bl = _split2(b)
        for which in (0, 1):
            hi.append(_keep_head(bh, head, which))
            lo.append(_keep_head(bl, head, which))
    bh, bl = jnp.concatenate(hi, axis=1), jnp.concatenate(lo, axis=1)
    a_cat = jnp.concatenate([ah, ah, al], axis=2)
    b_cat = jnp.concatenate([bh, bl, bh], axis=2)
    return lax.dot_general(a_cat, b_cat, _NT, preferred_element_type=F32)


def _pdot_tn(x, y):
    nx = x.shape[2]
    xh, xl = _split2(jnp.swapaxes(x, 1, 2))
    yh, yl = _split2(y)
    x_cat = jnp.concatenate([xh, xh, xl, jnp.zeros_like(xl)], axis=2)
    y_cat = jnp.concatenate([yh, yl, yh, jnp.zeros_like(yl)], axis=1)
    tot = lax.dot_general(x_cat, y_cat, _NN, preferred_element_type=F32)
    head = _lane_head(LANES)
    return [jnp.where(head == 0, tot[:, g:g + HEAD_A], tot[:, g + HEAD_A:g + LANES]) for g in range(0, nx, LANES)]


def _scan_kernel(f_ref, gc_ref, gng_ref, gnb_ref, ya_ref, sfin_ref, s_scr):
    @pl.when(pl.program_id(0) == 0)
    def _():
        s_scr[...] = jnp.zeros_like(s_scr)

    def pairs(x, rows):
        return jnp.stack([x[c * rows:(c + 1) * rows, i * LANES:(i + 1) * LANES]
                          for c in range(SCAN_CHUNKS) for i in range(PAIRS)], axis=0)

    feat = lambda n: f_ref[:, n * D_A:(n + 1) * D_A]
    p, r, q, k, v = (pairs(feat(n), CHUNK) for n in (F_PT, F_RT, F_QT, F_KT, F_V))
    gam = pairs(gc_ref[...], SUBLANES)[:, 0:1]
    ri = lax.broadcasted_iota(jnp.int32, (1, CHUNK, LANES), 1)
    ci = lax.broadcasted_iota(jnp.int32, (1, CHUNK, LANES), 2) & (HEAD_A - 1)
    strict = ri > ci
    incl = ri >= ci
    eye = (ri == ci).astype(F32)

    gram = _pdot_nt(jnp.concatenate([p, r], axis=1), [q, k])
    a_qp = jnp.where(strict, gram[:, 0:CHUNK, 0:LANES], 0.0)
    a_kp = jnp.where(strict, gram[:, 0:CHUNK, LANES:], 0.0)
    a_rq = jnp.where(incl, gram[:, CHUNK:, 0:LANES], 0.0)
    a_rk = jnp.where(incl, gram[:, CHUNK:, LANES:], 0.0)
    same = lambda log2_bs: (ri >> log2_bs) == (ci >> log2_bs)
    a_d = jnp.where(same(4), a_qp, 0.0)
    tinv = eye + a_d
    pw = _pdot_nn(a_d, a_d)
    for _ in range(2):
        both = _pdot_nn(jnp.concatenate([pw, tinv], axis=1), pw)
        pw = both[:, 0:CHUNK]
        tinv = tinv + both[:, CHUNK:]
    tinv = tinv + _pdot_nn(tinv, pw)
    for log2_bs in (4, 5):
        off = jnp.where(jnp.logical_and(same(log2_bs + 1), jnp.logical_not(same(log2_bs))), a_qp, 0.0)
        tinv = tinv + _pdot_nn(tinv, _pdot_nn(off, tinv))
    av = _pdot_nn(jnp.concatenate([a_kp, a_rk], axis=1), v)
    pw_hat = _pdot_nn(tinv, jnp.concatenate([p, av[:, 0:CHUNK]], axis=2))
    x = _pdot_nn(a_rq, pw_hat)
    r_hat = r + x[:, :, 0:LANES]
    o_loc = x[:, :, LANES:] + av[:, CHUNK:]
    y_p, y_w = _pdot_tn(pw_hat, q)
    (vk,) = _pdot_tn(v, k)
    m_mat = (eye + y_p) * gam
    n_mat = (y_w + vk) * gam

    state = s_scr[...]
    outs = []
    for c in range(SCAN_CHUNKS):
        sl = slice(c * PAIRS, (c + 1) * PAIRS)
        outs.append(_pdot_nt(r_hat[sl], [state]) + o_loc[sl])
        state = _pdot_nn(state, m_mat[sl]) + n_mat[sl]
    s_scr[...] = state
    o = jnp.concatenate(outs, axis=0)

    head = _lane_head(LANES)

    def head_mean(z):
        sums = [jnp.sum(_keep_head(z, head, which), axis=2, keepdims=True) for which in (0, 1)]
        return jnp.where(head == 0, sums[0], sums[1]) * (1.0 / HEAD_A)

    d = o - head_mean(o)
    o_n = d * lax.rsqrt(head_mean(d * d) + EPS_GN)
    o_n = jnp.concatenate([jnp.concatenate([o_n[c * PAIRS + i] for i in range(PAIRS)], axis=1)
                           for c in range(SCAN_CHUNKS)], axis=0)
    ya_ref[...] = ((o_n * gng_ref[...] + gnb_ref[...] + feat(F_BONUS)) * feat(F_G)).astype(ya_ref.dtype)

    @pl.when(pl.program_id(0) == pl.num_programs(0) - 1)
    def _():
        sfin_ref[...] = s_scr[...]


def _wkv_scan(feats, gc, lw):
    t = feats.shape[0]
    vec = pl.BlockSpec((1, D_A), lambda c: (0, 0))
    pair_state = (PAIRS, HEAD_A, LANES)
    rows = SCAN_CHUNKS * CHUNK
    ya, s = pl.pallas_call(
        _scan_kernel,
        out_shape=[jax.ShapeDtypeStruct((t, D_A), BF16), jax.ShapeDtypeStruct(pair_state, F32)],
        grid=(t // rows,),
        in_specs=[pl.BlockSpec((rows, N_FEATS * D_A), lambda c: (c, 0)),
                  pl.BlockSpec((SUBLANES * SCAN_CHUNKS, D_A), lambda c: (c, 0)), vec, vec],
        out_specs=[pl.BlockSpec((rows, D_A), lambda c: (c, 0)), pl.BlockSpec(pair_state, lambda c: (0, 0, 0))],
        scratch_shapes=[pltpu.VMEM(pair_state, F32)],
        compiler_params=_params(("arbitrary",)),
        name="wkv_scan_prompt",
    )(feats, gc, lw["gn_g"], lw["gn_b"])
    s = s.reshape(PAIRS, HEAD_A, 2, HEAD_A).transpose(0, 2, 1, 3).reshape(H_A, HEAD_A, HEAD_A)
    return ya, s


def _wkv_step_kernel(s_ref, rows_ref, sn_ref, o_ref):
    ri = lax.broadcasted_iota(jnp.int32, (1, 1, HEAD_A, HEAD_A), 2)
    ci = lax.broadcasted_iota(jnp.int32, (1, 1, HEAD_A, HEAD_A), 3)
    eye = ri == ci
    pt, rt, qt, kt, v, gc = (rows_ref[:, n] for n in range(6))
    s = s_ref[0]
    u = jnp.sum(s * pt, axis=3, keepdims=True)
    v_col = jnp.sum(jnp.where(eye, v, 0.0), axis=3, keepdims=True)
    m = s + u * qt + v_col * kt
    sn_ref[0] = m * gc
    o_col = jnp.sum(m * rt, axis=3, keepdims=True)
    o_ref[...] = jnp.sum(jnp.where(eye, o_col, 0.0), axis=2, keepdims=True)


def _wkv_step(state, pt, rt, qt, kt, v, gc, bt):
    b = pt.shape[0]
    rows = jnp.stack([pt, rt, qt, kt, v, gc], axis=1).reshape(b, 6, H_A, 1, HEAD_A)
    st_spec = pl.BlockSpec((1, bt, H_A, HEAD_A, HEAD_A), lambda i: (0, i, 0, 0, 0))
    sn, o = pl.pallas_call(
        _wkv_step_kernel,
        out_shape=[jax.ShapeDtypeStruct(state.shape, F32), jax.ShapeDtypeStruct((b, H_A, 1, HEAD_A), F32)],
        grid=(b // bt,),
        in_specs=[st_spec, pl.BlockSpec((bt, 6, H_A, 1, HEAD_A), lambda i: (i, 0, 0, 0, 0))],
        out_specs=[st_spec, pl.BlockSpec((bt, H_A, 1, HEAD_A), lambda i: (i, 0, 0, 0))],
        compiler_params=_params(("parallel",)),
        name="wkv_step_sample",
    )(state, rows)
    return sn, o.reshape(b, D_A)


def _post_kernel(o_ref, bonus_ref, g_ref, gng_ref, gnb_ref, hs_ref, hb_ref, y_ref):
    o = o_ref[...]
    hsum, hbc = hs_ref[...], hb_ref[...]
    mu = _head_sums(o, hsum, hbc) * (1.0 / HEAD_A)
    d = o - mu
    var = _head_sums(d * d, hsum, hbc) * (1.0 / HEAD_A)
    o_n = d * lax.rsqrt(var + EPS_GN) * gng_ref[...] + gnb_ref[...]
    y_ref[...] = ((o_n + bonus_ref[...]) * g_ref[...]).astype(y_ref.dtype)


def _rwkv_post(o, bonus, g, lw, tm, name):
    m = o.shape[0]
    big = pl.BlockSpec((tm, D_A), lambda i: (i, 0))
    vec = pl.BlockSpec((1, D_A), lambda i: (0, 0))
    return pl.pallas_call(
        _post_kernel,
        out_shape=jax.ShapeDtypeStruct((m, D_A), BF16),
        grid=(m // tm,),
        in_specs=[big, big, big, vec, vec, pl.BlockSpec((D_A, LANES), lambda i: (0, 0)),
                  pl.BlockSpec((LANES, D_A), lambda i: (0, 0))],
        out_specs=big,
        compiler_params=_params(("parallel",)),
        name=name,
    )(o, bonus, g, lw["gn_g"], lw["gn_b"], lw["hsum"], lw["hbc"])


def _attn_kernel(q_ref, kc_ref, vc_ref, kh_ref, vh_ref, bias_ref, o_ref, l_ref, *, dil, m_blocks, tiled):
    scale = HEAD_B ** -0.5
    n_streams = SUBLANES if tiled else dil
    n_units = n_streams * m_blocks

    stride = SUBLANES if tiled else dil
    if tiled:
        flat = lambda ref: ref.reshape(ref.shape[0] * SUBLANES, LANES)
        q_ref, kc_ref, vc_ref, kh_ref, vh_ref, o_ref, l_ref = map(flat, (q_ref, kc_ref, vc_ref, kh_ref, vh_ref,
                                                                          o_ref, l_ref))

    def rows(ref, r, start, size):
        if stride == 1:
            return ref[pl.ds(start, size), :]
        return ref[pl.ds(r + stride * start, size, stride=stride), :]

    def put(ref, r, start, val):
        if stride == 1:
            ref[pl.ds(start, BLK), :] = val
        else:
            ref[pl.ds(r + stride * start, BLK, stride=stride), :] = val

    qs, ks, vs, units = [], [], [], []
    for r in range(n_streams):
        for mb in range(m_blocks):
            units.append((r, mb))
            qs.append(rows(q_ref, r, BLK * mb, BLK))
            if mb == 0:
                ks.append(jnp.concatenate([rows(kh_ref, r, 0, BLK), rows(kc_ref, r, 0, BLK)], axis=0))
                vs.append(jnp.concatenate([rows(vh_ref, r, 0, BLK), rows(vc_ref, r, 0, BLK)], axis=0))
            else:
                ks.append(rows(kc_ref, r, BLK * (mb - 1), 2 * BLK))
                vs.append(rows(vc_ref, r, BLK * (mb - 1), 2 * BLK))

    def batch(xs, mul=None):
        pre = (lambda x: x) if mul is None else (lambda x: x * mul)
        return jnp.stack([pre(x[:, sub * HEAD_B:(sub + 1) * HEAD_B]).astype(BF16) for sub in range(2) for x in xs],
                         axis=0)

    assert math.frexp(scale)[0] == 0.5
    qb, kb, vb = batch(qs, scale), batch(ks), batch(vs)
    s = lax.dot_general(qb, kb, (((2,), (2,)), ((0,), (0,))), preferred_element_type=F32)
    s = s.reshape(2, n_units, BLK, 2 * BLK) + bias_ref[...][:, None]
    unit = lax.broadcasted_iota(jnp.int32, (1, n_units, 1, 2 * BLK), 1)
    col = lax.broadcasted_iota(jnp.int32, (1, n_units, 1, 2 * BLK), 3)
    no_prev = jnp.logical_and(jnp.logical_and(pl.program_id(0) == 0, (unit & (m_blocks - 1)) == 0), col < BLK)
    s = jnp.where(no_prev, -jnp.inf, s)
    m = jnp.max(s, axis=-1, keepdims=True)
    p = jnp.exp(s - m)
    l = jnp.sum(p, axis=-1, keepdims=True)
    pv = lax.dot_general(p.astype(BF16).reshape(2 * n_units, BLK, 2 * BLK), vb, (((2,), (1,)), ((0,), (0,))),
                         preferred_element_type=F32)
    o = pv.reshape(2, n_units, BLK, HEAD_B) / l
    lse = jnp.broadcast_to(m + jnp.log(l), (2, n_units, BLK, HEAD_B))
    for u, (r, mb) in enumerate(units):
        put(o_ref, r, BLK * mb, jnp.concatenate([o[0, u], o[1, u]], axis=1))
        put(l_ref, r, BLK * mb, jnp.concatenate([lse[0, u], lse[1, u]], axis=1))


def _attn_prompt(proj, bias, gi, rows_per_step):
    t, n_cols = proj.shape
    dil = DILATIONS[gi]
    span = BLK * dil
    tiled = dil % SUBLANES == 0
    if tiled:
        rows_per_step *= dil // SUBLANES
    m_blocks = rows_per_step // span
    cq = (COL_QKV + gi * D_G) // LANES
    ck = (COL_QKV + D_B + gi * D_G) // LANES
    cv = (COL_QKV + 2 * D_B + gi * D_G) // LANES
    bias_block = (2, BLK, 2 * BLK)
    if tiled:
        n_hi = dil // SUBLANES
        src = proj.reshape(t // dil, n_hi, SUBLANES, n_cols)
        grid = (t // rows_per_step, n_hi, H_G // 2)
        l_rows = rows_per_step // dil
        cur = lambda c0: pl.BlockSpec((l_rows, None, SUBLANES, LANES), lambda i, rh, hp: (i, rh, 0, c0 + hp))
        halo = lambda c0: pl.BlockSpec((BLK, None, SUBLANES, LANES),
                                       lambda i, rh, hp: (jnp.maximum(i * m_blocks - 1, 0), rh, 0, c0 + hp))
        out_spec = pl.BlockSpec((l_rows, None, SUBLANES, LANES), lambda i, rh, hp: (i, rh, 0, hp))
        out = jax.ShapeDtypeStruct((t // dil, n_hi, SUBLANES, D_G), F32)
        bias_spec = pl.BlockSpec(bias_block, lambda i, rh, hp: (hp, 0, 0))
    else:
        src = proj
        grid = (t // rows_per_step, H_G // 2)
        cur = lambda c0: pl.BlockSpec((rows_per_step, LANES), lambda i, hp: (i, c0 + hp))
        halo = lambda c0: pl.BlockSpec((span, LANES), lambda i, hp: (jnp.maximum(i * m_blocks - 1, 0), c0 + hp))
        out_spec = pl.BlockSpec((rows_per_step, LANES), lambda i, hp: (i, hp))
        out = jax.ShapeDtypeStruct((t, D_G), F32)
        bias_spec = pl.BlockSpec(bias_block, lambda i, hp: (hp, 0, 0))
    o, l = pl.pallas_call(
        functools.partial(_attn_kernel, dil=dil, m_blocks=m_blocks, tiled=tiled),
        out_shape=[out, out],
        grid=grid,
        in_specs=[cur(cq), cur(ck), cur(cv), halo(ck), halo(cv), bias_spec],
        out_specs=[out_spec, out_spec],
        compiler_params=_params(("parallel",) * len(grid)),
        name=f"attn_prompt_g{gi}",
    )(src, src, src, src, src, bias)
    return o.reshape(t, D_G), l.reshape(t, D_G)


def _attn_step_kernel(q_ref, kn_ref, vn_ref, c_ref, bias_ref, bias0_ref, o_ref, l_ref):
    scale = HEAD_B ** -0.5
    ri = lax.broadcasted_iota(jnp.int32, (1, 1, HEAD_B, HEAD_B), 2)
    ci = lax.broadcasted_iota(jnp.int32, (1, 1, HEAD_B, HEAD_B), 3)
    eye = ri == ci
    q = q_ref[...]
    q_col = jnp.sum(jnp.where(eye, q, 0.0), axis=3, keepdims=True)
    s = jnp.sum(c_ref[:, 0] * q_col, axis=2, keepdims=True) * scale + bias_ref[...]
    s_new = jnp.sum(q * kn_ref[...], axis=3, keepdims=True) * scale + bias0_ref[...]
    m = jnp.maximum(jnp.max(s, axis=3, keepdims=True), s_new)
    p = jnp.exp(s - m)
    p_new = jnp.exp(s_new - m)
    l = jnp.sum(p, axis=3, keepdims=True) + p_new
    o_col = jnp.sum(c_ref[:, 1] * p, axis=3, keepdims=True)
    o_row = jnp.sum(jnp.where(eye, o_col, 0.0), axis=2, keepdims=True)
    o_ref[...] = (o_row + p_new * vn_ref[...]) / l
    l_ref[...] = jnp.broadcast_to(m + jnp.log(l), o_ref.shape)


def _attn_step(q, k_new, v_new, cache, bias, bias0, gi):
    b = q.shape[0]
    w = cache.shape[2]
    bt = max(1, min(b, STEP_CACHE_WORDS // (2 * D_G * w)))
    cache_t = jnp.transpose(cache, (0, 1, 3, 4, 5, 2))
    vec = pl.BlockSpec((bt, H_G, 1, HEAD_B), lambda i: (i, 0, 0, 0))
    out = jax.ShapeDtypeStruct((b, H_G, 1, HEAD_B), F32)
    o, l = pl.pallas_call(
        _attn_step_kernel,
        out_shape=[out, out],
        grid=(b // bt,),
        in_specs=[vec, vec, vec,
                  pl.BlockSpec((None, bt, 2, H_G, HEAD_B, w), lambda i: (0, i, 0, 0, 0, 0)),
                  pl.BlockSpec((H_G, 1, w), lambda i: (0, 0, 0)),
                  pl.BlockSpec((H_G, 1, 1), lambda i: (0, 0, 0))],
        out_specs=[vec, vec],
        compiler_params=_params(("parallel",)),
        name=f"attn_step_g{gi}",
    )(q, k_new, v_new, cache_t, bias, bias0)
    return o.reshape(b, D_G), l.reshape(b, D_G)


def _mix_kernel(ya_ref, o1_ref, o2_ref, o3_ref, l1_ref, l2_ref, l3_ref, wa_ref, wb_ref, ga_ref, gb_ref, o_ref):
    l1, l2, l3 = l1_ref[...], l2_ref[...], l3_ref[...]
    m = jnp.maximum(jnp.maximum(l1, l2), l3)
    e1, e2, e3 = jnp.exp(l1 - m), jnp.exp(l2 - m), jnp.exp(l3 - m)
    den = e1 + e2 + e3
    yb = ((e1 / den) * o1_ref[...] + (e2 / den) * o2_ref[...] + (e3 / den) * o3_ref[...]).astype(BF16)
    mixed = ga_ref[...] * _dot_bf16(ya_ref[...], wa_ref[...]) + gb_ref[...] * _dot_bf16(yb, wb_ref[...])
    o_ref[...] = mixed.astype(o_ref.dtype)


def _mix(ya, outs, lses, wa, wb, proj, tm, name):
    m = ya.shape[0]
    n = D_MODEL
    assert COL_GATE % n == 0
    ga0 = COL_GATE // n
    grp = pl.BlockSpec((tm, D_G), lambda i: (i, 0))
    resident = lambda rows: pl.BlockSpec((rows, n), lambda i: (0, 0), pipeline_mode=pl.Buffered(1))
    return pl.pallas_call(
        _mix_kernel,
        out_shape=jax.ShapeDtypeStruct((m, n), BF16),
        grid=(m // tm,),
        in_specs=[pl.BlockSpec((tm, D_A), lambda i: (i, 0))] + [grp] * 6 + [
            resident(D_A), resident(D_G),
            pl.BlockSpec((tm, n), lambda i: (i, ga0)),
            pl.BlockSpec((tm, n), lambda i: (i, ga0 + 1)),
        ],
        out_specs=pl.BlockSpec((tm, n), lambda i: (i, 0)),
        compiler_params=_params(("parallel",)),
        name=name,
    )(ya, *outs, *lses, wa, wb, proj, proj)


def _wo_kernel(a_ref, w_ref, x_ref, g_ref, h_ref, hn_ref):
    h = x_ref[...] + _dot_bf16(a_ref[...], w_ref[...])
    h_ref[...] = h
    hn_ref[...] = _rms(h, g_ref[...]).astype(BF16)


def _wo(a, w, x, g, tm, name):
    m, k = a.shape
    n = w.shape[1]
    row = pl.BlockSpec((tm, n), lambda i: (i, 0))
    return pl.pallas_call(
        _wo_kernel,
        out_shape=[jax.ShapeDtypeStruct((m, n), F32), jax.ShapeDtypeStruct((m, n), BF16)],
        grid=(m // tm,),
        in_specs=[pl.BlockSpec((tm, k), lambda i: (i, 0)), pl.BlockSpec((k, n), lambda i: (0, 0)), row,
                  pl.BlockSpec((1, n), lambda i: (0, 0))],
        out_specs=[row, row],
        compiler_params=_params(("parallel",)),
        name=name,
    )(a, w, x, g.reshape(1, n))


def _gelu(x):
    return 0.5 * x * (1.0 + lax.erf(x * (1.0 / math.sqrt(2.0))))


def _ffn_up_kernel(h_ref, w1_ref, w2_ref, cw1_ref, cw2_ref, cb1_ref, cb2_ref, act_ref, t1_ref, t2_ref,
                   c1_scr, c2_scr, w1_scr, w2_scr):
    tm = h_ref.shape[0]

    @pl.when(pl.program_id(1) == 0)
    def _():
        c1_scr[...] = jnp.zeros_like(c1_scr)
        c2_scr[...] = jnp.zeros_like(c2_scr)
        w1_scr[...] = w1_ref[...].astype(BF16)
        w2_scr[...] = w2_ref[...].astype(BF16)

    a = h_ref[...]
    row = lax.broadcasted_iota(jnp.int32, (tm, 1), 0)

    def conv(u, carry_ref, cw_ref, cb_ref):
        m1 = jnp.where(row == 0, carry_ref[7:8, :], pltpu.roll(u, 1, axis=0))
        m2 = jnp.where(row == 0, carry_ref[6:7, :], jnp.where(row == 1, carry_ref[7:8, :], pltpu.roll(u, 2, axis=0)))
        return cb_ref[...] + cw_ref[0:1, :] * m2 + cw_ref[1:2, :] * m1 + cw_ref[2:3, :] * u

    u1 = _dot_bf16(a, w1_scr[...])
    u2 = _dot_bf16(a, w2_scr[...])
    c1 = conv(u1, c1_scr, cw1_ref, cb1_ref)
    c2 = conv(u2, c2_scr, cw2_ref, cb2_ref)
    act_ref[...] = (_gelu(c1) * c2).astype(act_ref.dtype)
    c1_scr[...] = u1[tm - 8:tm]
    c2_scr[...] = u2[tm - 8:tm]
    t1_ref[...] = u1[tm - 8:tm]
    t2_ref[...] = u2[tm - 8:tm]


def _ffn_up_prompt(hn, w_up, conv_w, conv_b, tm, tn):
    m = hn.shape[0]
    nj = D_FF // tn
    tail = jax.ShapeDtypeStruct((8, D_FF), F32)
    return pl.pallas_call(
        _ffn_up_kernel,
        out_shape=[jax.ShapeDtypeStruct((m, D_FF), BF16), tail, tail],
        grid=(nj, m // tm),
        in_specs=[
            pl.BlockSpec((tm, D_MODEL), lambda j, i: (i, 0)),
            pl.BlockSpec((D_MODEL, tn), lambda j, i: (0, j)),
            pl.BlockSpec((D_MODEL, tn), lambda j, i: (0, nj + j)),
            pl.BlockSpec((CONV_W, tn), lambda j, i: (0, j)),
            pl.BlockSpec((CONV_W, tn), lambda j, i: (0, nj + j)),
            pl.BlockSpec((1, tn), lambda j, i: (0, j)),
            pl.BlockSpec((1, tn), lambda j, i: (0, nj + j)),
        ],
        out_specs=[pl.BlockSpec((tm, tn), lambda j, i: (i, j)),
                   pl.BlockSpec((8, tn), lambda j, i: (0, j)),
                   pl.BlockSpec((8, tn), lambda j, i: (0, j))],
        scratch_shapes=[pltpu.VMEM((8, tn), F32), pltpu.VMEM((8, tn), F32),
                        pltpu.VMEM((D_MODEL, tn), BF16), pltpu.VMEM((D_MODEL, tn), BF16)],
        compiler_params=_params(("arbitrary", "arbitrary")),
        name="ffn_up_prompt",
    )(hn, w_up, w_up, conv_w, conv_w, conv_b, conv_b)


def _ffn_act_step_kernel(up_ref, prev_ref, cw_ref, cb_ref, act_ref):
    up = up_ref[...]
    w = 2 * D_FF
    c = cb_ref[...] + cw_ref[0:1, :] * prev_ref[:, 0:w] + cw_ref[1:2, :] * prev_ref[:, w:2 * w] + cw_ref[2:3, :] * up
    act_ref[...] = (_gelu(c[:, 0:D_FF]) * c[:, D_FF:w]).astype(act_ref.dtype)


def _ffn_act_step(up, conv_prev, conv_w, conv_b):
    b = up.shape[0]
    return pl.pallas_call(
        _ffn_act_step_kernel,
        out_shape=jax.ShapeDtypeStruct((b, D_FF), BF16),
        compiler_params=pltpu.CompilerParams(vmem_limit_bytes=VMEM_LIMIT),
        name="ffn_act_sample",
    )(up, conv_prev.reshape(b, (CONV_W - 1) * 2 * D_FF), conv_w, conv_b)


def _down_kernel(a_ref, w_ref, h_ref, g_ref, o_ref, acc_ref):
    @pl.when(pl.program_id(1) == 0)
    def _():
        acc_ref[...] = h_ref[...]

    out = acc_ref[...] + _dot_bf16(a_ref[...], w_ref[...])
    acc_ref[...] = out
    inv = lax.rsqrt(jnp.mean(out * out, axis=-1, keepdims=True) + EPS_RMS)
    o_ref[...] = out * inv * g_ref[...]


def _ffn_down(act, w_down, h, g, tm, tk, name):
    m = act.shape[0]
    nk = D_FF // tk
    return pl.pallas_call(
        _down_kernel,
        out_shape=jax.ShapeDtypeStruct((m, D_MODEL), F32),
        grid=(m // tm, nk),
        in_specs=[
            pl.BlockSpec((tm, tk), lambda i, k: (i, k)),
            pl.BlockSpec((tk, D_MODEL), lambda i, k: (k, 0)),
            pl.BlockSpec((tm, D_MODEL), lambda i, k: (i, 0)),
            pl.BlockSpec((1, D_MODEL), lambda i, k: (0, 0)),
        ],
        out_specs=pl.BlockSpec((tm, D_MODEL), lambda i, k: (i, 0)),
        scratch_shapes=[pltpu.VMEM((tm, D_MODEL), F32)],
        compiler_params=_params(("parallel", "arbitrary")),
        name=name,
    )(act, w_down, h, g.reshape(1, D_MODEL))


def _rel_bucket(dist):
    max_exact = N_BUCKETS // 2
    d_f = np.maximum(dist, 1).astype(np.float64)
    scaled = np.log(d_f / max_exact) / math.log(MAX_DISTANCE / max_exact) * (N_BUCKETS - max_exact)
    large = np.minimum(max_exact + scaled.astype(np.int64), N_BUCKETS - 1)
    return np.where(dist < max_exact, dist, large).astype(np.int32)


def _bias_rows(tab, dist):
    onehot = (jnp.asarray(_rel_bucket(dist))[None, :] == jnp.arange(N_BUCKETS)[:, None]).astype(F32)
    return jnp.dot(tab.T, onehot, precision=HIGHEST)


def _bias_tables(rel_bias, gi):
    dil = DILATIONS[gi]
    win = WINDOWS[gi]
    reach = win // dil
    tab = rel_bias[:, gi * H_G:(gi + 1) * H_G]
    qi = np.arange(BLK)[:, None]
    ki = np.arange(2 * BLK)[None, :]
    rel = qi + BLK - ki
    blk = _bias_rows(tab, (dil * np.maximum(rel, 0)).reshape(-1)).reshape(H_G, BLK, 2 * BLK)
    blk = jnp.where(((rel >= 0) & (rel <= reach))[None], blk, -jnp.inf)
    back = win - np.arange(win)
    step = jnp.where((back % dil == 0)[None, :], _bias_rows(tab, back), -jnp.inf)[:, None, :]
    step0 = _bias_rows(tab, np.zeros((1,), np.int64))[:, None, :]
    return blk, step, step0


def _tiles(m, prompt):
    if prompt:
        return dict(tag="prompt", proj_m=1024, prep_m=2 * CHUNK, attn_rows=2048, mix_m=512, wo_m=512, up_m=1024,
                    up_n=512, down_m=512, down_k=D_FF // 2)
    return dict(tag="sample", proj_m=m, prep_m=m, step_b=4, post_m=m, mix_m=m, wo_m=m, up_n=512, down_m=m,
                down_k=D_FF // 2)


def _layer(x, lw, bias_blk, prompt, state=None, proj=None):
    m = x.shape[0]
    tiles = _tiles(m, prompt)
    if proj is None:
        proj = _proj(x, lw["norm1_g"], lw["w_in"], lw["in_bias"], tiles["proj_m"], "proj_" + tiles["tag"])

    if prompt:
        feats, gc = _rwkv_prep(proj, None, tiles["prep_m"], CHUNK, lw, "rwkv_prep_prompt")
        ya, wkv_new = _wkv_scan(feats, gc, lw)
        wkv_new = wkv_new[None, None]
    else:
        feats, gc = _rwkv_prep(proj, state["shift"], tiles["prep_m"], 1, lw, "rwkv_prep_sample")
        f = lambda n: feats[:, n * D_A:(n + 1) * D_A]
        wkv_new, o = _wkv_step(state["wkv"], f(F_PT), f(F_RT), f(F_QT), f(F_KT), f(F_V), gc, tiles["step_b"])
        ya = _rwkv_post(o, f(F_BONUS), f(F_G), lw, tiles["post_m"], "rwkv_post_" + tiles["tag"])

    outs, lses = [], []
    for gi in range(N_GROUPS):
        if prompt:
            o_g, l_g = _attn_prompt(proj, bias_blk[gi][0], gi, tiles["attn_rows"])
        else:
            sl = lambda c0: proj[:, c0 + gi * D_G:c0 + (gi + 1) * D_G].reshape(m, H_G, 1, HEAD_B)
            o_g, l_g = _attn_step(sl(COL_QKV), sl(COL_QKV + D_B), sl(COL_QKV + 2 * D_B), state["win"][gi],
                                  bias_blk[gi][1], bias_blk[gi][2], gi)
        outs.append(o_g)
        lses.append(l_g)
    mixed = _mix(ya, outs, lses, lw["w_out_a"], lw["w_out_b"], proj, tiles["mix_m"], "mix_" + tiles["tag"])
    h, hn = _wo(mixed, lw["w_o"], x, lw["norm2_g"], tiles["wo_m"], "wo_" + tiles["tag"])

    if prompt:
        act, t1, t2 = _ffn_up_prompt(hn, lw["w_up"], lw["conv_w"], lw["conv_b"], tiles["up_m"], tiles["up_n"])
        conv_new = jnp.concatenate([t1[6:8], t2[6:8]], axis=1)[None, None]
    else:
        up = _matmul(hn, lw["w_up"], m, tiles["up_n"], "up_sample")
        act = _ffn_act_step(up, state["conv"], lw["conv_w"], lw["conv_b"])
        conv_new = jnp.concatenate([state["conv"][:, 1:], up[:, None, :]], axis=1)[None]
    y = _ffn_down(act, lw["w_down"], h, lw["normf_g"], tiles["down_m"], tiles["down_k"], "down_" + tiles["tag"])
    return y, proj, wkv_new, conv_new


def kernel(x_prompt, x_sample, state_wkv, state_shift, state_ffn_conv, cache_win1, cache_win2, cache_win3, rel_bias,
           norm1_g, w_in, gate_b, mu_shift, w0, w_up_decay, a0, w_up_aaa, w_up_gate, k_k, k_a, r_k, gn_g, gn_b,
           w_out_a, w_out_b, w_o, norm2_g, w_up, conv_w, conv_b, w_down, normf_g):
    row = lambda a: a.reshape(1, -1)
    w_lora = jnp.zeros((D_LORA, 3 * D_A), F32)
    w_lora = w_lora.at[0:D_DECAY_LORA, 0:D_A].set(w_up_decay[0])
    w_lora = w_lora.at[D_DECAY_LORA:D_DECAY_LORA + D_AAA_LORA, D_A:2 * D_A].set(w_up_aaa[0])
    w_lora = w_lora.at[D_DECAY_LORA + D_AAA_LORA:, 2 * D_A:].set(w_up_gate[0])
    head = np.arange(D_A) // HEAD_A
    lw = dict(
        norm1_g=norm1_g[0], norm2_g=norm2_g[0], normf_g=normf_g,
        in_bias=jnp.concatenate([jnp.zeros((1, COL_GATE), F32), row(gate_b[0])], axis=1),
        mu=row(mu_shift[0]), w_lora=jnp.stack(_split2(w_lora)), w0=row(w0[0]), a0=row(a0[0]), k_k=row(k_k[0]), k_a=row(k_a[0]),
        r_k=row(r_k[0]), gn_g=row(gn_g[0]), gn_b=row(gn_b[0]),
        hsum=jnp.asarray(head[:, None] == np.arange(LANES)[None, :], BF16),
        hbc=jnp.asarray(np.arange(LANES)[:, None] == head[None, :], BF16),
        w_out_a=w_out_a[0].astype(BF16), w_out_b=w_out_b[0].astype(BF16), w_o=w_o[0].astype(BF16),
        w_up=w_up[0], conv_w=conv_w[0], conv_b=row(conv_b[0]), w_down=w_down[0].astype(BF16),
    )
    bias = [_bias_tables(rel_bias, gi) for gi in range(N_GROUPS)]

    proj_s, lw["w_in"] = _proj_cast(x_sample[:, 0], norm1_g[0], w_in[0], lw["in_bias"])
    y_p, proj_p, wkv_p, conv_p = _layer(x_prompt[0], lw, bias, prompt=True)

    b = DEC_BATCH
    state = dict(wkv=state_wkv, shift=state_shift.reshape(b, D_SHIFT), conv=state_ffn_conv[0],
                 win=(cache_win1, cache_win2, cache_win3))
    y_s, proj_s, wkv_s, conv_s = _layer(x_sample[:, 0], lw, bias, prompt=False, state=state, proj=proj_s)

    def kv_rows(proj, lo, gi):
        k = proj[lo:, COL_QKV + D_B + gi * D_G:COL_QKV + D_B + (gi + 1) * D_G]
        v = proj[lo:, COL_QKV + 2 * D_B + gi * D_G:COL_QKV + 2 * D_B + (gi + 1) * D_G]
        n = k.shape[0]
        return jnp.stack([k.reshape(n, H_G, HEAD_B), v.reshape(n, H_G, HEAD_B)], axis=1)

    win_p = [kv_rows(proj_p, SEQ - min(WINDOWS[gi], SEQ), gi)[None, None] for gi in range(N_GROUPS)]
    win_s = [kv_rows(proj_s, 0, gi)[None, :, None] for gi in range(N_GROUPS)]
    return (y_p[None], y_s[:, None],
            wkv_p, wkv_s,
            proj_p[SEQ - 1:, 0:D_SHIFT][None, None], proj_s[:, 0:D_SHIFT][None, :, None],
            conv_p, conv_s,
            win_p[0], win_s[0], win_p[1], win_s[1], win_p[2], win_s[2])
```

```python
import functools
import math

import numpy as np
import jax
import jax.numpy as jnp
from jax import lax
from jax.experimental import pallas as pl
from jax.experimental.pallas import tpu as pltpu

F32 = jnp.float32
BF16 = jnp.bfloat16
HIGHEST = lax.Precision.HIGHEST

D_MODEL = 2048
SEQ = 8192
DEC_BATCH = 32
HEAD_A = 64
H_A = 16
D_A = H_A * HEAD_A
D_DECAY_LORA = 96
D_AAA_LORA = 96
D_GATE_LORA = 64
D_LORA = D_DECAY_LORA + D_AAA_LORA + D_GATE_LORA
D_SHIFT = 3 * D_A + D_LORA
EPS_GN = 64e-5
HEAD_B = 64
H_G = 8
D_G = H_G * HEAD_B
WINDOWS = (128, 512, 2048)
DILATIONS = (1, 4, 16)
N_GROUPS = 3
D_B = N_GROUPS * D_G
BLK = 128
N_BUCKETS = 32
MAX_DISTANCE = 2048
COL_QKV = D_SHIFT
COL_GATE_SRC = D_SHIFT + 3 * D_B
D_IN = COL_GATE_SRC + 2 * D_MODEL
PROJ_TN = 1024
COL_GATE = -(-COL_GATE_SRC // PROJ_TN) * PROJ_TN
D_IN_PAD = COL_GATE + 2 * D_MODEL
D_FF = 5632
CONV_W = 3
EPS_RMS = 1e-6
CHUNK = 64
SCAN_CHUNKS = 4
F_RT, F_KT, F_QT, F_PT, F_V, F_BONUS, F_G = range(7)
N_FEATS = 7
LANES = 128
SUBLANES = 8
VMEM_LIMIT = 56 * 1024 * 1024
STEP_CACHE_WORDS = 2 * 1024 * 1024


def _params(sem, vmem=VMEM_LIMIT):
    return pltpu.CompilerParams(dimension_semantics=sem, vmem_limit_bytes=vmem)


def _sigmoid(x):
    return 1.0 / (1.0 + jnp.exp(-x))


def _dot_bf16(a, b):
    return jnp.dot(a, b, preferred_element_type=F32)


def _split2(x):
    hi = x.astype(BF16)
    return hi, (x - hi.astype(F32)).astype(BF16)


def _split3(x):
    hi = x.astype(BF16)
    r1 = x - hi.astype(F32)
    mid = r1.astype(BF16)
    lo = (r1 - mid.astype(F32)).astype(BF16)
    return hi, mid, lo


def _dot_exact_rhs(x, m):
    hi, mid, lo = _split3(x)
    return _dot_bf16(hi, m) + _dot_bf16(mid, m) + _dot_bf16(lo, m)


def _head_sums(x, hsum, hbc):
    return _dot_exact_rhs(_dot_exact_rhs(x, hsum), hbc)


def _dot_exact_lhs(m, x):
    hi, mid, lo = _split3(x)
    return _dot_bf16(m, hi) + _dot_bf16(m, mid) + _dot_bf16(m, lo)


def _rms(x, g):
    return x * lax.rsqrt(jnp.mean(x * x, axis=-1, keepdims=True) + EPS_RMS) * g


def _proj_cast_kernel(x_ref, g_ref, w_ref, b_ref, o_ref, wb_ref, xn_scr, *, gap_block):
    j = pl.program_id(0)

    @pl.when(j == 0)
    def _():
        xn_scr[...] = _rms(x_ref[...], g_ref[...]).astype(BF16)

    wb = jnp.where(j != gap_block, w_ref[...], 0.0).astype(BF16)
    wb_ref[...] = wb
    acc = _dot_bf16(xn_scr[...], wb)
    tn = o_ref.shape[1]
    col = lax.broadcasted_iota(jnp.int32, (1, tn), 1) + j * tn
    o_ref[...] = jnp.where(col >= COL_GATE, _sigmoid(acc + b_ref[...]), acc)


def _proj_cast(x, g, w, bias):
    m, k = x.shape
    tn = COL_GATE - COL_GATE_SRC
    assert tn % LANES == 0 and COL_GATE_SRC % tn == 0
    gap_block = COL_GATE_SRC // tn
    return pl.pallas_call(
        functools.partial(_proj_cast_kernel, gap_block=gap_block),
        out_shape=[jax.ShapeDtypeStruct((m, D_IN_PAD), F32), jax.ShapeDtypeStruct((k, D_IN_PAD), BF16)],
        grid=(D_IN_PAD // tn,),
        in_specs=[
            pl.BlockSpec((m, k), lambda j: (0, 0)),
            pl.BlockSpec((1, k), lambda j: (0, 0)),
            pl.BlockSpec((k, tn), lambda j: (0, jnp.where(j < gap_block, j, jnp.maximum(j - 1, 0)))),
            pl.BlockSpec((1, tn), lambda j: (0, j)),
        ],
        out_specs=[pl.BlockSpec((m, tn), lambda j: (0, j)), pl.BlockSpec((k, tn), lambda j: (0, j))],
        scratch_shapes=[pltpu.VMEM((m, k), BF16)],
        compiler_params=_params(("arbitrary",)),
        name="proj_sample_cast_w_in",
    )(x, g.reshape(1, k), w, bias)


def _proj_kernel(x_ref, g_ref, w_ref, b_ref, o_ref, xn_scr):
    @pl.when(pl.program_id(1) == 0)
    def _():
        xn_scr[...] = _rms(x_ref[...], g_ref[...]).astype(BF16)

    acc = _dot_bf16(xn_scr[...], w_ref[...])
    tn = o_ref.shape[1]
    col = lax.broadcasted_iota(jnp.int32, (1, tn), 1) + pl.program_id(1) * tn
    o_ref[...] = jnp.where(col >= COL_GATE, _sigmoid(acc + b_ref[...]), acc)


def _proj(x, g, w, bias, tm, name):
    m, k = x.shape
    n = w.shape[1]
    return pl.pallas_call(
        _proj_kernel,
        out_shape=jax.ShapeDtypeStruct((m, n), F32),
        grid=(m // tm, n // PROJ_TN),
        in_specs=[
            pl.BlockSpec((tm, k), lambda i, j: (i, 0)),
            pl.BlockSpec((1, k), lambda i, j: (0, 0)),
            pl.BlockSpec((k, PROJ_TN), lambda i, j: (0, j)),
            pl.BlockSpec((1, PROJ_TN), lambda i, j: (0, j)),
        ],
        out_specs=pl.BlockSpec((tm, PROJ_TN), lambda i, j: (i, j)),
        scratch_shapes=[pltpu.VMEM((tm, k), BF16)],
        compiler_params=_params(("parallel", "arbitrary")),
        name=name,
    )(x, g.reshape(1, k), w, bias)


def _matmul_kernel(a_ref, w_ref, o_ref):
    o_ref[...] = _dot_bf16(a_ref[...], w_ref[...].astype(BF16))


def _matmul(a, w, tm, tn, name):
    m, k = a.shape
    n = w.shape[1]
    return pl.pallas_call(
        _matmul_kernel,
        out_shape=jax.ShapeDtypeStruct((m, n), F32),
        grid=(m // tm, n // tn),
        in_specs=[pl.BlockSpec((tm, k), lambda i, j: (i, 0)), pl.BlockSpec((k, tn), lambda i, j: (0, j))],
        out_specs=pl.BlockSpec((tm, tn), lambda i, j: (i, j)),
        compiler_params=_params(("parallel", "arbitrary")),
        name=name,
    )(a, w)


def _prep_kernel(p_ref, prev_ref, mu_ref, wl_ref, w0_ref, a0_ref, kk_ref, ka_ref, rk_ref, lmat_ref, sel_ref,
                 hs_ref, hb_ref, f_ref, gc_ref, *, rows_are_time):
    p = p_ref[...]
    tm = p.shape[0]
    if rows_are_time:
        last = jnp.where(pl.program_id(0) == 0, 0.0, prev_ref[7:8, :])
        row = lax.broadcasted_iota(jnp.int32, (tm, 1), 0)
        prev = jnp.where(row == 0, last, pltpu.roll(p, 1, axis=0))
    else:
        prev = prev_ref[...]
    xm = p + mu_ref[...] * (prev - p)
    r = xm[:, 0:D_A]
    k = xm[:, D_A:2 * D_A]
    v = xm[:, 2 * D_A:3 * D_A]
    xl = xm[:, 3 * D_A:D_SHIFT]
    lane = lax.broadcasted_iota(jnp.int32, xl.shape, 1)
    act = jnp.where(lane < D_DECAY_LORA, jnp.tanh(xl),
                    jnp.where(lane < D_DECAY_LORA + D_AAA_LORA, xl, _sigmoid(xl)))
    act_hi, act_lo = _split2(act)
    w_hi, w_lo = wl_ref[0], wl_ref[1]
    lora = _dot_bf16(act_hi, w_hi) + _dot_bf16(act_hi, w_lo) + _dot_bf16(act_lo, w_hi)
    y = -(w0_ref[...] + lora[:, 0:D_A])
    softplus = jnp.maximum(y, 0.0) + jnp.log(1.0 + jnp.exp(-jnp.abs(y)))
    logw = -jnp.exp(-softplus - 0.5)
    a = _sigmoid(a0_ref[...] + lora[:, D_A:2 * D_A])
    g = lora[:, 2 * D_A:3 * D_A]
    kkr = k * kk_ref[...]
    kp = k * (1.0 + (a - 1.0) * ka_ref[...])
    seg = _head_sums(jnp.concatenate([kkr * kkr, r * kp * rk_ref[...]], axis=0), hs_ref[...], hb_ref[...])
    kk = kkr / jnp.maximum(jnp.sqrt(seg[0:tm]), 1e-12)
    cum = _dot_exact_lhs(lmat_ref[...], logw)
    e_out = jnp.exp(-cum)
    feats = {F_RT: r * jnp.exp(cum), F_KT: kp * e_out, F_QT: kk * a * e_out, F_PT: -kk * jnp.exp(cum - logw),
             F_V: v, F_BONUS: seg[tm:2 * tm] * v, F_G: g}
    for n, val in feats.items():
        f_ref[:, n * D_A:(n + 1) * D_A] = val
    gc_ref[...] = jnp.exp(_dot_exact_lhs(sel_ref[...], cum))


def _rwkv_prep(proj, prev, tm, chunk, lw, name):
    m = proj.shape[0]
    rows_are_time = prev is None
    n_tiles = m // tm
    t = np.arange(tm)
    lmat = ((t[:, None] // chunk == t[None, :] // chunk) & (t[None, :] <= t[:, None])).astype(np.float32)
    sel_stride = 1 if chunk == 1 else 8
    n_sel = sel_stride * (tm // chunk)
    sel = np.zeros((n_sel, tm), np.float32)
    for c in range(tm // chunk):
        sel[sel_stride * c, (c + 1) * chunk - 1] = 1.0
    if rows_are_time:
        prev_arr = proj
        prev_spec = pl.BlockSpec((8, D_SHIFT), lambda i: (jnp.maximum(i * (tm // 8) - 1, 0), 0))
    else:
        prev_arr = prev
        prev_spec = pl.BlockSpec((tm, D_SHIFT), lambda i: (i, 0))
    vec = lambda d: pl.BlockSpec((1, d), lambda i: (0, 0))
    full = lambda a: pl.BlockSpec(a.shape, lambda i: (0, 0))
    lmat_b = jnp.asarray(lmat, BF16)
    sel_b = jnp.asarray(sel, BF16)
    return pl.pallas_call(
        functools.partial(_prep_kernel, rows_are_time=rows_are_time),
        out_shape=[jax.ShapeDtypeStruct((m, N_FEATS * D_A), F32),
                   jax.ShapeDtypeStruct((n_tiles * n_sel, D_A), F32)],
        grid=(n_tiles,),
        in_specs=[pl.BlockSpec((tm, D_SHIFT), lambda i: (i, 0)), prev_spec, vec(D_SHIFT),
                  pl.BlockSpec(lw["w_lora"].shape, lambda i: (0, 0, 0)),
                  vec(D_A), vec(D_A), vec(D_A), vec(D_A), vec(D_A), full(lmat_b), full(sel_b), full(lw["hsum"]),
                  full(lw["hbc"])],
        out_specs=[pl.BlockSpec((tm, N_FEATS * D_A), lambda i: (i, 0)), pl.BlockSpec((n_sel, D_A), lambda i: (i, 0))],
        compiler_params=_params(("parallel",)),
        name=name,
    )(proj, prev_arr, lw["mu"], lw["w_lora"], lw["w0"], lw["a0"], lw["k_k"], lw["k_a"], lw["r_k"], lmat_b, sel_b,
      lw["hsum"], lw["hbc"])


_NN = (((2,), (1,)), ((0,), (0,)))
_NT = (((2,), (2,)), ((0,), (0,)))
PAIRS = H_A // 2
LOG2_HEAD = HEAD_A.bit_length() - 1
assert HEAD_A == 1 << LOG2_HEAD and 2 * HEAD_A == LANES


def _lane_head(n):
    return (lax.broadcasted_iota(jnp.int32, (1, 1, n), 2) >> LOG2_HEAD) & 1


def _keep_head(x, head, which):
    return jnp.where(head == which, x, jnp.zeros_like(x))


def _pdot_nn(a, b):
    ah, al = _split2(a)
    bh, bl = _split2(b)
    head = _lane_head(b.shape[2])
    bd = lambda x: jnp.concatenate([_keep_head(x, head, 0), _keep_head(x, head, 1)], axis=1)
    a_cat = jnp.concatenate([ah, ah, al], axis=2)
    b_cat = jnp.concatenate([bd(bh), bd(bl), bd(bh)], axis=1)
    return lax.dot_general(a_cat, b_cat, _NN, preferred_element_type=F32)


def _pdot_nt(a, bs):
    ah, al = _split2(a)
    head = _lane_head(LANES)
    hi, lo = [], []
    for b in bs:
        bh, bl = _split2(b)
        for which in (0, 1):
            hi.append(_keep_head(bh, head, which))
            lo.append(_keep_head(bl, head, which))
    bh, bl = jnp.concatenate(hi, axis=1), jnp.concatenate(lo, axis=1)
    a_cat = jnp.concatenate([ah, ah, al], axis=2)
    b_cat = jnp.concatenate([bh, bl, bh], axis=2)
    return lax.dot_general(a_cat, b_cat, _NT, preferred_element_type=F32)


def _pdot_tn(x, y):
    nx = x.shape[2]
    xh, xl = _split2(jnp.swapaxes(x, 1, 2))
    yh, yl = _split2(y)
    res = lax.dot_general(jnp.concatenate([xh, xl], axis=1), jnp.concatenate([yh, yl], axis=2), _NN,
                          preferred_element_type=F32)
    tot = res[:, 0:nx, 0:LANES] + res[:, 0:nx, LANES:] + res[:, nx:, 0:LANES]
    head = _lane_head(LANES)
    return [jnp.where(head == 0, tot[:, g:g + HEAD_A], tot[:, g + HEAD_A:g + LANES]) for g in range(0, nx, LANES)]


def _scan_kernel(f_ref, gc_ref, gng_ref, gnb_ref, ya_ref, sfin_ref, s_scr):
    @pl.when(pl.program_id(0) == 0)
    def _():
        s_scr[...] = jnp.zeros_like(s_scr)

    def pairs(x, rows):
        return jnp.stack([x[c * rows:(c + 1) * rows, i * LANES:(i + 1) * LANES]
                          for c in range(SCAN_CHUNKS) for i in range(PAIRS)], axis=0)

    feat = lambda n: f_ref[:, n * D_A:(n + 1) * D_A]
    p, r, q, k, v = (pairs(feat(n), CHUNK) for n in (F_PT, F_RT, F_QT, F_KT, F_V))
    gam = pairs(gc_ref[...], SUBLANES)[:, 0:1]
    ri = lax.broadcasted_iota(jnp.int32, (1, CHUNK, LANES), 1)
    ci = lax.broadcasted_iota(jnp.int32, (1, CHUNK, LANES), 2) & (HEAD_A - 1)
    strict = ri > ci
    incl = ri >= ci
    eye = (ri == ci).astype(F32)

    gram = _pdot_nt(jnp.concatenate([p, r], axis=1), [q, k])
    a_qp = jnp.where(strict, gram[:, 0:CHUNK, 0:LANES], 0.0)
    a_kp = jnp.where(strict, gram[:, 0:CHUNK, LANES:], 0.0)
    a_rq = jnp.where(incl, gram[:, CHUNK:, 0:LANES], 0.0)
    a_rk = jnp.where(incl, gram[:, CHUNK:, LANES:], 0.0)
    same = lambda log2_bs: (ri >> log2_bs) == (ci >> log2_bs)
    a_d = jnp.where(same(4), a_qp, 0.0)
    tinv = eye + a_d
    pw = _pdot_nn(a_d, a_d)
    for _ in range(2):
        both = _pdot_nn(jnp.concatenate([pw, tinv], axis=1), pw)
        pw = both[:, 0:CHUNK]
        tinv = tinv + both[:, CHUNK:]
    tinv = tinv + _pdot_nn(tinv, pw)
    for log2_bs in (4, 5):
        off = jnp.where(jnp.logical_and(same(log2_bs + 1), jnp.logical_not(same(log2_bs))), a_qp, 0.0)
        tinv = tinv + _pdot_nn(tinv, _pdot_nn(off, tinv))
    av = _pdot_nn(jnp.concatenate([a_kp, a_rk], axis=1), v)
    pw_hat = _pdot_nn(tinv, jnp.concatenate([p, av[:, 0:CHUNK]], axis=2))
    x = _pdot_nn(a_rq, pw_hat)
    r_hat = r + x[:, :, 0:LANES]
    o_loc = x[:, :, LANES:] + av[:, CHUNK:]
    y_p, y_w = _pdot_tn(pw_hat, q)
    (vk,) = _pdot_tn(v, k)
    m_mat = (eye + y_p) * gam
    n_mat = (y_w + vk) * gam

    state = s_scr[...]
    outs = []
    for c in range(SCAN_CHUNKS):
        sl = slice(c * PAIRS, (c + 1) * PAIRS)
        outs.append(_pdot_nt(r_hat[sl], [state]) + o_loc[sl])
        state = _pdot_nn(state, m_mat[sl]) + n_mat[sl]
    s_scr[...] = state
    o = jnp.concatenate(outs, axis=0)

    head = _lane_head(LANES)

    def head_mean(z):
        sums = [jnp.sum(_keep_head(z, head, which), axis=2, keepdims=True) for which in (0, 1)]
        return jnp.where(head == 0, sums[0], sums[1]) * (1.0 / HEAD_A)

    d = o - head_mean(o)
    o_n = d * lax.rsqrt(head_mean(d * d) + EPS_GN)
    o_n = jnp.concatenate([jnp.concatenate([o_n[c * PAIRS + i] for i in range(PAIRS)], axis=1)
                           for c in range(SCAN_CHUNKS)], axis=0)
    ya_ref[...] = ((o_n * gng_ref[...] + gnb_ref[...] + feat(F_BONUS)) * feat(F_G)).astype(ya_ref.dtype)

    @pl.when(pl.program_id(0) == pl.num_programs(0) - 1)
    def _():
        sfin_ref[...] = s_scr[...]


def _wkv_scan(feats, gc, lw):
    t = feats.shape[0]
    vec = pl.BlockSpec((1, D_A), lambda c: (0, 0))
    pair_state = (PAIRS, HEAD_A, LANES)
    rows = SCAN_CHUNKS * CHUNK
    ya, s = pl.pallas_call(
        _scan_kernel,
        out_shape=[jax.ShapeDtypeStruct((t, D_A), BF16), jax.ShapeDtypeStruct(pair_state, F32)],
        grid=(t // rows,),
        in_specs=[pl.BlockSpec((rows, N_FEATS * D_A), lambda c: (c, 0)),
                  pl.BlockSpec((SUBLANES * SCAN_CHUNKS, D_A), lambda c: (c, 0)), vec, vec],
        out_specs=[pl.BlockSpec((rows, D_A), lambda c: (c, 0)), pl.BlockSpec(pair_state, lambda c: (0, 0, 0))],
        scratch_shapes=[pltpu.VMEM(pair_state, F32)],
        compiler_params=_params(("arbitrary",)),
        name="wkv_scan_prompt",
    )(feats, gc, lw["gn_g"], lw["gn_b"])
    s = s.reshape(PAIRS, HEAD_A, 2, HEAD_A).transpose(0, 2, 1, 3).reshape(H_A, HEAD_A, HEAD_A)
    return ya, s


def _wkv_step_kernel(s_ref, rows_ref, sn_ref, o_ref):
    ri = lax.broadcasted_iota(jnp.int32, (1, 1, HEAD_A, HEAD_A), 2)
    ci = lax.broadcasted_iota(jnp.int32, (1, 1, HEAD_A, HEAD_A), 3)
    eye = ri == ci
    pt, rt, qt, kt, v, gc = (rows_ref[:, n] for n in range(6))
    s = s_ref[0]
    u = jnp.sum(s * pt, axis=3, keepdims=True)
    v_col = jnp.sum(jnp.where(eye, v, 0.0), axis=3, keepdims=True)
    m = s + u * qt + v_col * kt
    sn_ref[0] = m * gc
    o_col = jnp.sum(m * rt, axis=3, keepdims=True)
    o_ref[...] = jnp.sum(jnp.where(eye, o_col, 0.0), axis=2, keepdims=True)


def _wkv_step(state, pt, rt, qt, kt, v, gc, bt):
    b = pt.shape[0]
    rows = jnp.stack([pt, rt, qt, kt, v, gc], axis=1).reshape(b, 6, H_A, 1, HEAD_A)
    st_spec = pl.BlockSpec((1, bt, H_A, HEAD_A, HEAD_A), lambda i: (0, i, 0, 0, 0))
    sn, o = pl.pallas_call(
        _wkv_step_kernel,
        out_shape=[jax.ShapeDtypeStruct(state.shape, F32), jax.ShapeDtypeStruct((b, H_A, 1, HEAD_A), F32)],
        grid=(b // bt,),
        in_specs=[st_spec, pl.BlockSpec((bt, 6, H_A, 1, HEAD_A), lambda i: (i, 0, 0, 0, 0))],
        out_specs=[st_spec, pl.BlockSpec((bt, H_A, 1, HEAD_A), lambda i: (i, 0, 0, 0))],
        compiler_params=_params(("parallel",)),
        name="wkv_step_sample",
    )(state, rows)
    return sn, o.reshape(b, D_A)


def _post_kernel(o_ref, bonus_ref, g_ref, gng_ref, gnb_ref, hs_ref, hb_ref, y_ref):
    o = o_ref[...]
    hsum, hbc = hs_ref[...], hb_ref[...]
    mu = _head_sums(o, hsum, hbc) * (1.0 / HEAD_A)
    d = o - mu
    var = _head_sums(d * d, hsum, hbc) * (1.0 / HEAD_A)
    o_n = d * lax.rsqrt(var + EPS_GN) * gng_ref[...] + gnb_ref[...]
    y_ref[...] = ((o_n + bonus_ref[...]) * g_ref[...]).astype(y_ref.dtype)


def _rwkv_post(o, bonus, g, lw, tm, name):
    m = o.shape[0]
    big = pl.BlockSpec((tm, D_A), lambda i: (i, 0))
    vec = pl.BlockSpec((1, D_A), lambda i: (0, 0))
    return pl.pallas_call(
        _post_kernel,
        out_shape=jax.ShapeDtypeStruct((m, D_A), BF16),
        grid=(m // tm,),
        in_specs=[big, big, big, vec, vec, pl.BlockSpec((D_A, LANES), lambda i: (0, 0)),
                  pl.BlockSpec((LANES, D_A), lambda i: (0, 0))],
        out_specs=big,
        compiler_params=_params(("parallel",)),
        name=name,
    )(o, bonus, g, lw["gn_g"], lw["gn_b"], lw["hsum"], lw["hbc"])


def _attn_kernel(q_ref, kc_ref, vc_ref, kh_ref, vh_ref, bias_ref, o_ref, l_ref, *, dil, m_blocks, tiled):
    scale = HEAD_B ** -0.5
    n_streams = SUBLANES if tiled else dil
    n_units = n_streams * m_blocks

    stride = SUBLANES if tiled else dil
    if tiled:
        flat = lambda ref: ref.reshape(ref.shape[0] * SUBLANES, LANES)
        q_ref, kc_ref, vc_ref, kh_ref, vh_ref, o_ref, l_ref = map(flat, (q_ref, kc_ref, vc_ref, kh_ref, vh_ref,
                                                                          o_ref, l_ref))

    def rows(ref, r, start, size):
        if stride == 1:
            return ref[pl.ds(start, size), :]
        return ref[pl.ds(r + stride * start, size, stride=stride), :]

    def put(ref, r, start, val):
        if stride == 1:
            ref[pl.ds(start, BLK), :] = val
        else:
            ref[pl.ds(r + stride * start, BLK, stride=stride), :] = val

    qs, ks, vs, units = [], [], [], []
    for r in range(n_streams):
        for mb in range(m_blocks):
            units.append((r, mb))
            qs.append(rows(q_ref, r, BLK * mb, BLK))
            if mb == 0:
                ks.append(jnp.concatenate([rows(kh_ref, r, 0, BLK), rows(kc_ref, r, 0, BLK)], axis=0))
                vs.append(jnp.concatenate([rows(vh_ref, r, 0, BLK), rows(vc_ref, r, 0, BLK)], axis=0))
            else:
                ks.append(rows(kc_ref, r, BLK * (mb - 1), 2 * BLK))
                vs.append(rows(vc_ref, r, BLK * (mb - 1), 2 * BLK))

    def batch(xs, mul=None):
        pre = (lambda x: x) if mul is None else (lambda x: x * mul)
        return jnp.stack([pre(x[:, sub * HEAD_B:(sub + 1) * HEAD_B]).astype(BF16) for sub in range(2) for x in xs],
                         axis=0)

    assert math.frexp(scale)[0] == 0.5
    qb, kb, vb = batch(qs, scale), batch(ks), batch(vs)
    s = lax.dot_general(qb, kb, (((2,), (2,)), ((0,), (0,))), preferred_element_type=F32)
    s = s.reshape(2, n_units, BLK, 2 * BLK) + bias_ref[...][:, None]
    unit = lax.broadcasted_iota(jnp.int32, (1, n_units, 1, 2 * BLK), 1)
    col = lax.broadcasted_iota(jnp.int32, (1, n_units, 1, 2 * BLK), 3)
    no_prev = jnp.logical_and(jnp.logical_and(pl.program_id(0) == 0, (unit & (m_blocks - 1)) == 0), col < BLK)
    s = jnp.where(no_prev, -jnp.inf, s)
    m = jnp.max(s, axis=-1, keepdims=True)
    p = jnp.exp(s - m)
    l = jnp.sum(p, axis=-1, keepdims=True)
    pv = lax.dot_general(p.astype(BF16).reshape(2 * n_units, BLK, 2 * BLK), vb, (((2,), (1,)), ((0,), (0,))),
                         preferred_element_type=F32)
    o = pv.reshape(2, n_units, BLK, HEAD_B) / l
    lse = jnp.broadcast_to(m + jnp.log(l), (2, n_units, BLK, HEAD_B))
    for u, (r, mb) in enumerate(units):
        put(o_ref, r, BLK * mb, jnp.concatenate([o[0, u], o[1, u]], axis=1))
        put(l_ref, r, BLK * mb, jnp.concatenate([lse[0, u], lse[1, u]], axis=1))


def _attn_prompt(proj, bias, gi, rows_per_step):
    t, n_cols = proj.shape
    dil = DILATIONS[gi]
    span = BLK * dil
    tiled = dil % SUBLANES == 0
    if tiled:
        rows_per_step *= dil // SUBLANES
    m_blocks = rows_per_step // span
    cq = (COL_QKV + gi * D_G) // LANES
    ck = (COL_QKV + D_B + gi * D_G) // LANES
    cv = (COL_QKV + 2 * D_B + gi * D_G) // LANES
    bias_block = (2, BLK, 2 * BLK)
    if tiled:
        n_hi = dil // SUBLANES
        src = proj.reshape(t // dil, n_hi, SUBLANES, n_cols)
        grid = (t // rows_per_step, n_hi, H_G // 2)
        l_rows = rows_per_step // dil
        cur = lambda c0: pl.BlockSpec((l_rows, None, SUBLANES, LANES), lambda i, rh, hp: (i, rh, 0, c0 + hp))
        halo = lambda c0: pl.BlockSpec((BLK, None, SUBLANES, LANES),
                                       lambda i, rh, hp: (jnp.maximum(i * m_blocks - 1, 0), rh, 0, c0 + hp))
        out_spec = pl.BlockSpec((l_rows, None, SUBLANES, LANES), lambda i, rh, hp: (i, rh, 0, hp))
        out = jax.ShapeDtypeStruct((t // dil, n_hi, SUBLANES, D_G), F32)
        bias_spec = pl.BlockSpec(bias_block, lambda i, rh, hp: (hp, 0, 0))
    else:
        src = proj
        grid = (t // rows_per_step, H_G // 2)
        cur = lambda c0: pl.BlockSpec((rows_per_step, LANES), lambda i, hp: (i, c0 + hp))
        halo = lambda c0: pl.BlockSpec((span, LANES), lambda i, hp: (jnp.maximum(i * m_blocks - 1, 0), c0 + hp))
        out_spec = pl.BlockSpec((rows_per_step, LANES), lambda i, hp: (i, hp))
        out = jax.ShapeDtypeStruct((t, D_G), F32)
        bias_spec = pl.BlockSpec(bias_block, lambda i, hp: (hp, 0, 0))
    o, l = pl.pallas_call(
        functools.partial(_attn_kernel, dil=dil, m_blocks=m_blocks, tiled=tiled),
        out_shape=[out, out],
        grid=grid,
        in_specs=[cur(cq), cur(ck), cur(cv), halo(ck), halo(cv), bias_spec],
        out_specs=[out_spec, out_spec],
        compiler_params=_params(("parallel",) * len(grid)),
        name=f"attn_prompt_g{gi}",
    )(src, src, src, src, src, bias)
    return o.reshape(t, D_G), l.reshape(t, D_G)


def _attn_step_kernel(q_ref, kn_ref, vn_ref, c_ref, bias_ref, bias0_ref, o_ref, l_ref):
    scale = HEAD_B ** -0.5
    ri = lax.broadcasted_iota(jnp.int32, (1, 1, HEAD_B, HEAD_B), 2)
    ci = lax.broadcasted_iota(jnp.int32, (1, 1, HEAD_B, HEAD_B), 3)
    eye = ri == ci
    q = q_ref[...]
    q_col = jnp.sum(jnp.where(eye, q, 0.0), axis=3, keepdims=True)
    s = jnp.sum(c_ref[:, 0] * q_col, axis=2, keepdims=True) * scale + bias_ref[...]
    s_new = jnp.sum(q * kn_ref[...], axis=3, keepdims=True) * scale + bias0_ref[...]
    m = jnp.maximum(jnp.max(s, axis=3, keepdims=True), s_new)
    p = jnp.exp(s - m)
    p_new = jnp.exp(s_new - m)
    l = jnp.sum(p, axis=3, keepdims=True) + p_new
    o_col = jnp.sum(c_ref[:, 1] * p, axis=3, keepdims=True)
    o_row = jnp.sum(jnp.where(eye, o_col, 0.0), axis=2, keepdims=True)
    o_ref[...] = (o_row + p_new * vn_ref[...]) / l
    l_ref[...] = jnp.broadcast_to(m + jnp.log(l), o_ref.shape)


def _attn_step(q, k_new, v_new, cache, bias, bias0, gi):
    b = q.shape[0]
    w = cache.shape[2]
    bt = max(1, min(b, STEP_CACHE_WORDS // (2 * D_G * w)))
    cache_t = jnp.transpose(cache, (0, 1, 3, 4, 5, 2))
    vec = pl.BlockSpec((bt, H_G, 1, HEAD_B), lambda i: (i, 0, 0, 0))
    out = jax.ShapeDtypeStruct((b, H_G, 1, HEAD_B), F32)
    o, l = pl.pallas_call(
        _attn_step_kernel,
        out_shape=[out, out],
        grid=(b // bt,),
        in_specs=[vec, vec, vec,
                  pl.BlockSpec((None, bt, 2, H_G, HEAD_B, w), lambda i: (0, i, 0, 0, 0, 0)),
                  pl.BlockSpec((H_G, 1, w), lambda i: (0, 0, 0)),
                  pl.BlockSpec((H_G, 1, 1), lambda i: (0, 0, 0))],
        out_specs=[vec, vec],
        compiler_params=_params(("parallel",)),
        name=f"attn_step_g{gi}",
    )(q, k_new, v_new, cache_t, bias, bias0)
    return o.reshape(b, D_G), l.reshape(b, D_G)


def _mix_kernel(ya_ref, o1_ref, o2_ref, o3_ref, l1_ref, l2_ref, l3_ref, wa_ref, wb_ref, ga_ref, gb_ref, o_ref):
    l1, l2, l3 = l1_ref[...], l2_ref[...], l3_ref[...]
    m = jnp.maximum(jnp.maximum(l1, l2), l3)
    e1, e2, e3 = jnp.exp(l1 - m), jnp.exp(l2 - m), jnp.exp(l3 - m)
    den = e1 + e2 + e3
    yb = ((e1 / den) * o1_ref[...] + (e2 / den) * o2_ref[...] + (e3 / den) * o3_ref[...]).astype(BF16)
    mixed = ga_ref[...] * _dot_bf16(ya_ref[...], wa_ref[...]) + gb_ref[...] * _dot_bf16(yb, wb_ref[...])
    o_ref[...] = mixed.astype(o_ref.dtype)


def _mix(ya, outs, lses, wa, wb, proj, tm, name):
    m = ya.shape[0]
    n = D_MODEL
    assert COL_GATE % n == 0
    ga0 = COL_GATE // n
    grp = pl.BlockSpec((tm, D_G), lambda i: (i, 0))
    resident = lambda rows: pl.BlockSpec((rows, n), lambda i: (0, 0), pipeline_mode=pl.Buffered(1))
    return pl.pallas_call(
        _mix_kernel,
        out_shape=jax.ShapeDtypeStruct((m, n), BF16),
        grid=(m // tm,),
        in_specs=[pl.BlockSpec((tm, D_A), lambda i: (i, 0))] + [grp] * 6 + [
            resident(D_A), resident(D_G),
            pl.BlockSpec((tm, n), lambda i: (i, ga0)),
            pl.BlockSpec((tm, n), lambda i: (i, ga0 + 1)),
        ],
        out_specs=pl.BlockSpec((tm, n), lambda i: (i, 0)),
        compiler_params=_params(("parallel",)),
        name=name,
    )(ya, *outs, *lses, wa, wb, proj, proj)


def _wo_kernel(a_ref, w_ref, x_ref, g_ref, h_ref, hn_ref):
    h = x_ref[...] + _dot_bf16(a_ref[...], w_ref[...])
    h_ref[...] = h
    hn_ref[...] = _rms(h, g_ref[...]).astype(BF16)


def _wo(a, w, x, g, tm, name):
    m, k = a.shape
    n = w.shape[1]
    row = pl.BlockSpec((tm, n), lambda i: (i, 0))
    return pl.pallas_call(
        _wo_kernel,
        out_shape=[jax.ShapeDtypeStruct((m, n), F32), jax.ShapeDtypeStruct((m, n), BF16)],
        grid=(m // tm,),
        in_specs=[pl.BlockSpec((tm, k), lambda i: (i, 0)), pl.BlockSpec((k, n), lambda i: (0, 0)), row,
                  pl.BlockSpec((1, n), lambda i: (0, 0))],
        out_specs=[row, row],
        compiler_params=_params(("parallel",)),
        name=name,
    )(a, w, x, g.reshape(1, n))


def _gelu(x):
    return 0.5 * x * (1.0 + lax.erf(x * (1.0 / math.sqrt(2.0))))


def _ffn_up_kernel(h_ref, w1_ref, w2_ref, cw1_ref, cw2_ref, cb1_ref, cb2_ref, act_ref, t1_ref, t2_ref,
                   c1_scr, c2_scr, w1_scr, w2_scr):
    tm = h_ref.shape[0]

    @pl.when(pl.program_id(1) == 0)
    def _():
        c1_scr[...] = jnp.zeros_like(c1_scr)
        c2_scr[...] = jnp.zeros_like(c2_scr)
        w1_scr[...] = w1_ref[...].astype(BF16)
        w2_scr[...] = w2_ref[...].astype(BF16)

    a = h_ref[...]
    row = lax.broadcasted_iota(jnp.int32, (tm, 1), 0)

    def conv(u, carry_ref, cw_ref, cb_ref):
        m1 = jnp.where(row == 0, carry_ref[7:8, :], pltpu.roll(u, 1, axis=0))
        m2 = jnp.where(row == 0, carry_ref[6:7, :], jnp.where(row == 1, carry_ref[7:8, :], pltpu.roll(u, 2, axis=0)))
        return cb_ref[...] + cw_ref[0:1, :] * m2 + cw_ref[1:2, :] * m1 + cw_ref[2:3, :] * u

    u1 = _dot_bf16(a, w1_scr[...])
    u2 = _dot_bf16(a, w2_scr[...])
    c1 = conv(u1, c1_scr, cw1_ref, cb1_ref)
    c2 = conv(u2, c2_scr, cw2_ref, cb2_ref)
    act_ref[...] = (_gelu(c1) * c2).astype(act_ref.dtype)
    c1_scr[...] = u1[tm - 8:tm]
    c2_scr[...] = u2[tm - 8:tm]
    t1_ref[...] = u1[tm - 8:tm]
    t2_ref[...] = u2[tm - 8:tm]


def _ffn_up_prompt(hn, w_up, conv_w, conv_b, tm, tn):
    m = hn.shape[0]
    nj = D_FF // tn
    tail = jax.ShapeDtypeStruct((8, D_FF), F32)
    return pl.pallas_call(
        _ffn_up_kernel,
        out_shape=[jax.ShapeDtypeStruct((m, D_FF), BF16), tail, tail],
        grid=(nj, m // tm),
        in_specs=[
            pl.BlockSpec((tm, D_MODEL), lambda j, i: (i, 0)),
            pl.BlockSpec((D_MODEL, tn), lambda j, i: (0, j)),
            pl.BlockSpec((D_MODEL, tn), lambda j, i: (0, nj + j)),
            pl.BlockSpec((CONV_W, tn), lambda j, i: (0, j)),
            pl.BlockSpec((CONV_W, tn), lambda j, i: (0, nj + j)),
            pl.BlockSpec((1, tn), lambda j, i: (0, j)),
            pl.BlockSpec((1, tn), lambda j, i: (0, nj + j)),
        ],
        out_specs=[pl.BlockSpec((tm, tn), lambda j, i: (i, j)),
                   pl.BlockSpec((8, tn), lambda j, i: (0, j)),
                   pl.BlockSpec((8, tn), lambda j, i: (0, j))],
        scratch_shapes=[pltpu.VMEM((8, tn), F32), pltpu.VMEM((8, tn), F32),
                        pltpu.VMEM((D_MODEL, tn), BF16), pltpu.VMEM((D_MODEL, tn), BF16)],
        compiler_params=_params(("arbitrary", "arbitrary")),
        name="ffn_up_prompt",
    )(hn, w_up, w_up, conv_w, conv_w, conv_b, conv_b)


def _ffn_act_step_kernel(up_ref, prev_ref, cw_ref, cb_ref, act_ref):
    up = up_ref[...]
    w = 2 * D_FF
    c = cb_ref[...] + cw_ref[0:1, :] * prev_ref[:, 0:w] + cw_ref[1:2, :] * prev_ref[:, w:2 * w] + cw_ref[2:3, :] * up
    act_ref[...] = (_gelu(c[:, 0:D_FF]) * c[:, D_FF:w]).astype(act_ref.dtype)


def _ffn_act_step(up, conv_prev, conv_w, conv_b):
    b = up.shape[0]
    return pl.pallas_call(
        _ffn_act_step_kernel,
        out_shape=jax.ShapeDtypeStruct((b, D_FF), BF16),
        compiler_params=pltpu.CompilerParams(vmem_limit_bytes=VMEM_LIMIT),
        name="ffn_act_sample",
    )(up, conv_prev.reshape(b, (CONV_W - 1) * 2 * D_FF), conv_w, conv_b)


def _down_kernel(a_ref, w_ref, h_ref, g_ref, o_ref, acc_ref):
    @pl.when(pl.program_id(1) == 0)
    def _():
        acc_ref[...] = h_ref[...]

    out = acc_ref[...] + _dot_bf16(a_ref[...], w_ref[...])
    acc_ref[...] = out
    inv = lax.rsqrt(jnp.mean(out * out, axis=-1, keepdims=True) + EPS_RMS)
    o_ref[...] = out * inv * g_ref[...]


def _ffn_down(act, w_down, h, g, tm, tk, name):
    m = act.shape[0]
    nk = D_FF // tk
    return pl.pallas_call(
        _down_kernel,
        out_shape=jax.ShapeDtypeStruct((m, D_MODEL), F32),
        grid=(m // tm, nk),
        in_specs=[
            pl.BlockSpec((tm, tk), lambda i, k: (i, k)),
            pl.BlockSpec((tk, D_MODEL), lambda i, k: (k, 0)),
            pl.BlockSpec((tm, D_MODEL), lambda i, k: (i, 0)),
            pl.BlockSpec((1, D_MODEL), lambda i, k: (0, 0)),
        ],
        out_specs=pl.BlockSpec((tm, D_MODEL), lambda i, k: (i, 0)),
        scratch_shapes=[pltpu.VMEM((tm, D_MODEL), F32)],
        compiler_params=_params(("parallel", "arbitrary")),
        name=name,
    )(act, w_down, h, g.reshape(1, D_MODEL))


def _rel_bucket(dist):
    max_exact = N_BUCKETS // 2
    d_f = np.maximum(dist, 1).astype(np.float64)
    scaled = np.log(d_f / max_exact) / math.log(MAX_DISTANCE / max_exact) * (N_BUCKETS - max_exact)
    large = np.minimum(max_exact + scaled.astype(np.int64), N_BUCKETS - 1)
    return np.where(dist < max_exact, dist, large).astype(np.int32)


def _bias_rows(tab, dist):
    onehot = (jnp.asarray(_rel_bucket(dist))[None, :] == jnp.arange(N_BUCKETS)[:, None]).astype(F32)
    return jnp.dot(tab.T, onehot, precision=HIGHEST)


def _bias_tables(rel_bias, gi):
    dil = DILATIONS[gi]
    win = WINDOWS[gi]
    reach = win // dil
    tab = rel_bias[:, gi * H_G:(gi + 1) * H_G]
    qi = np.arange(BLK)[:, None]
    ki = np.arange(2 * BLK)[None, :]
    rel = qi + BLK - ki
    blk = _bias_rows(tab, (dil * np.maximum(rel, 0)).reshape(-1)).reshape(H_G, BLK, 2 * BLK)
    blk = jnp.where(((rel >= 0) & (rel <= reach))[None], blk, -jnp.inf)
    back = win - np.arange(win)
    step = jnp.where((back % dil == 0)[None, :], _bias_rows(tab, back), -jnp.inf)[:, None, :]
    step0 = _bias_rows(tab, np.zeros((1,), np.int64))[:, None, :]
    return blk, step, step0


def _tiles(m, prompt):
    if prompt:
        return dict(tag="prompt", proj_m=1024, prep_m=2 * CHUNK, attn_rows=2048, mix_m=512, wo_m=512, up_m=1024,
                    up_n=512, down_m=512, down_k=D_FF // 2)
    return dict(tag="sample", proj_m=m, prep_m=m, step_b=4, post_m=m, mix_m=m, wo_m=m, up_n=512, down_m=m,
                down_k=D_FF // 2)


def _layer(x, lw, bias_blk, prompt, state=None, proj=None):
    m = x.shape[0]
    tiles = _tiles(m, prompt)
    if proj is None:
        proj = _proj(x, lw["norm1_g"], lw["w_in"], lw["in_bias"], tiles["proj_m"], "proj_" + tiles["tag"])

    if prompt:
        feats, gc = _rwkv_prep(proj, None, tiles["prep_m"], CHUNK, lw, "rwkv_prep_prompt")
        ya, wkv_new = _wkv_scan(feats, gc, lw)
        wkv_new = wkv_new[None, None]
    else:
        feats, gc = _rwkv_prep(proj, state["shift"], tiles["prep_m"], 1, lw, "rwkv_prep_sample")
        f = lambda n: feats[:, n * D_A:(n + 1) * D_A]
        wkv_new, o = _wkv_step(state["wkv"], f(F_PT), f(F_RT), f(F_QT), f(F_KT), f(F_V), gc, tiles["step_b"])
        ya = _rwkv_post(o, f(F_BONUS), f(F_G), lw, tiles["post_m"], "rwkv_post_" + tiles["tag"])

    outs, lses = [], []
    for gi in range(N_GROUPS):
        if prompt:
            o_g, l_g = _attn_prompt(proj, bias_blk[gi][0], gi, tiles["attn_rows"])
        else:
            sl = lambda c0: proj[:, c0 + gi * D_G:c0 + (gi + 1) * D_G].reshape(m, H_G, 1, HEAD_B)
            o_g, l_g = _attn_step(sl(COL_QKV), sl(COL_QKV + D_B), sl(COL_QKV + 2 * D_B), state["win"][gi],
                                  bias_blk[gi][1], bias_blk[gi][2], gi)
        outs.append(o_g)
        lses.append(l_g)
    mixed = _mix(ya, outs, lses, lw["w_out_a"], lw["w_out_b"], proj, tiles["mix_m"], "mix_" + tiles["tag"])
    h, hn = _wo(mixed, lw["w_o"], x, lw["norm2_g"], tiles["wo_m"], "wo_" + tiles["tag"])

    if prompt:
        act, t1, t2 = _ffn_up_prompt(hn, lw["w_up"], lw["conv_w"], lw["conv_b"], tiles["up_m"], tiles["up_n"])
        conv_new = jnp.concatenate([t1[6:8], t2[6:8]], axis=1)[None, None]
    else:
        up = _matmul(hn, lw["w_up"], m, tiles["up_n"], "up_sample")
        act = _ffn_act_step(up, state["conv"], lw["conv_w"], lw["conv_b"])
        conv_new = jnp.concatenate([state["conv"][:, 1:], up[:, None, :]], axis=1)[None]
    y = _ffn_down(act, lw["w_down"], h, lw["normf_g"], tiles["down_m"], tiles["down_k"], "down_" + tiles["tag"])
    return y, proj, wkv_new, conv_new


def kernel(x_prompt, x_sample, state_wkv, state_shift, state_ffn_conv, cache_win1, cache_win2, cache_win3, rel_bias,
           norm1_g, w_in, gate_b, mu_shift, w0, w_up_decay, a0, w_up_aaa, w_up_gate, k_k, k_a, r_k, gn_g, gn_b,
           w_out_a, w_out_b, w_o, norm2_g, w_up, conv_w, conv_b, w_down, normf_g):
    row = lambda a: a.reshape(1, -1)
    w_lora = jnp.zeros((D_LORA, 3 * D_A), F32)
    w_lora = w_lora.at[0:D_DECAY_LORA, 0:D_A].set(w_up_decay[0])
    w_lora = w_lora.at[D_DECAY_LORA:D_DECAY_LORA + D_AAA_LORA, D_A:2 * D_A].set(w_up_aaa[0])
    w_lora = w_lora.at[D_DECAY_LORA + D_AAA_LORA:, 2 * D_A:].set(w_up_gate[0])
    head = np.arange(D_A) // HEAD_A
    lw = dict(
        norm1_g=norm1_g[0], norm2_g=norm2_g[0], normf_g=normf_g,
        in_bias=jnp.concatenate([jnp.zeros((1, COL_GATE), F32), row(gate_b[0])], axis=1),
        mu=row(mu_shift[0]), w_lora=jnp.stack(_split2(w_lora)), w0=row(w0[0]), a0=row(a0[0]), k_k=row(k_k[0]), k_a=row(k_a[0]),
        r_k=row(r_k[0]), gn_g=row(gn_g[0]), gn_b=row(gn_b[0]),
        hsum=jnp.asarray(head[:, None] == np.arange(LANES)[None, :], BF16),
        hbc=jnp.asarray(np.arange(LANES)[:, None] == head[None, :], BF16),
        w_out_a=w_out_a[0].astype(BF16), w_out_b=w_out_b[0].astype(BF16), w_o=w_o[0].astype(BF16),
        w_up=w_up[0], conv_w=conv_w[0], conv_b=row(conv_b[0]), w_down=w_down[0].astype(BF16),
    )
    bias = [_bias_tables(rel_bias, gi) for gi in range(N_GROUPS)]

    proj_s, lw["w_in"] = _proj_cast(x_sample[:, 0], norm1_g[0], w_in[0], lw["in_bias"])
    y_p, proj_p, wkv_p, conv_p = _layer(x_prompt[0], lw, bias, prompt=True)

    b = DEC_BATCH
    state = dict(wkv=state_wkv, shift=state_shift.reshape(b, D_SHIFT), conv=state_ffn_conv[0],
                 win=(cache_win1, cache_win2, cache_win3))
    y_s, proj_s, wkv_s, conv_s = _layer(x_sample[:, 0], lw, bias, prompt=False, state=state, proj=proj_s)

    def kv_rows(proj, lo, gi):
        k = proj[lo:, COL_QKV + D_B + gi * D_G:COL_QKV + D_B + (gi + 1) * D_G]
        v = proj[lo:, COL_QKV + 2 * D_B + gi * D_G:COL_QKV + 2 * D_B + (gi + 1) * D_G]
        n = k.shape[0]
        return jnp.stack([k.reshape(n, H_G, HEAD_B), v.reshape(n, H_G, HEAD_B)], axis=1)

    win_p = [kv_rows(proj_p, SEQ - min(WINDOWS[gi], SEQ), gi)[None, None] for gi in range(N_GROUPS)]
    win_s = [kv_rows(proj_s, 0, gi)[None, :, None] for gi in range(N_GROUPS)]
    return (y_p[None], y_s[:, None],
            wkv_p, wkv_s,
            proj_p[SEQ - 1:, 0:D_SHIFT][None, None], proj_s[:, 0:D_SHIFT][None, :, None],
            conv_p, conv_s,
            win_p[0], win_s[0], win_p[1], win_s[1], win_p[2], win_s[2])
```

```python
import functools
import math

import numpy as np
import jax
import jax.numpy as jnp
from jax import lax
from jax.experimental import pallas as pl
from jax.experimental.pallas import tpu as pltpu

F32 = jnp.float32
BF16 = jnp.bfloat16
HIGHEST = lax.Precision.HIGHEST

D_MODEL = 2048
SEQ = 8192
DEC_BATCH = 32
HEAD_A = 64
H_A = 16
D_A = H_A * HEAD_A
D_DECAY_LORA = 96
D_AAA_LORA = 96
D_GATE_LORA = 64
D_LORA = D_DECAY_LORA + D_AAA_LORA + D_GATE_LORA
D_SHIFT = 3 * D_A + D_LORA
EPS_GN = 64e-5
HEAD_B = 64
H_G = 8
D_G = H_G * HEAD_B
WINDOWS = (128, 512, 2048)
DILATIONS = (1, 4, 16)
N_GROUPS = 3
D_B = N_GROUPS * D_G
BLK = 128
N_BUCKETS = 32
MAX_DISTANCE = 2048
COL_QKV = D_SHIFT
COL_GATE_SRC = D_SHIFT + 3 * D_B
D_IN = COL_GATE_SRC + 2 * D_MODEL
PROJ_TN = 1024
COL_GATE = -(-COL_GATE_SRC // PROJ_TN) * PROJ_TN
D_IN_PAD = COL_GATE + 2 * D_MODEL
D_FF = 5632
CONV_W = 3
EPS_RMS = 1e-6
CHUNK = 64
SCAN_CHUNKS = 4
F_RT, F_KT, F_QT, F_PT, F_V, F_BONUS, F_G = range(7)
N_FEATS = 7
LANES = 128
SUBLANES = 8
VMEM_LIMIT = 56 * 1024 * 1024
STEP_CACHE_WORDS = 2 * 1024 * 1024


def _params(sem, vmem=VMEM_LIMIT):
    return pltpu.CompilerParams(dimension_semantics=sem, vmem_limit_bytes=vmem)


def _sigmoid(x):
    return 1.0 / (1.0 + jnp.exp(-x))


def _dot_bf16(a, b):
    return jnp.dot(a, b, preferred_element_type=F32)


def _split2(x):
    hi = x.astype(BF16)
    return hi, (x - hi.astype(F32)).astype(BF16)


def _split3(x):
    hi = x.astype(BF16)
    r1 = x - hi.astype(F32)
    mid = r1.astype(BF16)
    lo = (r1 - mid.astype(F32)).astype(BF16)
    return hi, mid, lo


def _dot_exact_rhs(x, m):
    hi, mid, lo = _split3(x)
    return _dot_bf16(hi, m) + _dot_bf16(mid, m) + _dot_bf16(lo, m)


def _head_sums(x, hsum, hbc):
    return _dot_exact_rhs(_dot_exact_rhs(x, hsum), hbc)


def _dot_exact_lhs(m, x):
    hi, mid, lo = _split3(x)
    return _dot_bf16(m, hi) + _dot_bf16(m, mid) + _dot_bf16(m, lo)


def _rms(x, g):
    return x * lax.rsqrt(jnp.mean(x * x, axis=-1, keepdims=True) + EPS_RMS) * g


def _proj_cast_kernel(x_ref, g_ref, w_ref, b_ref, o_ref, wb_ref, xn_scr, *, gap_block):
    j = pl.program_id(0)

    @pl.when(j == 0)
    def _():
        xn_scr[...] = _rms(x_ref[...], g_ref[...]).astype(BF16)

    wb = jnp.where(j != gap_block, w_ref[...], 0.0).astype(BF16)
    wb_ref[...] = wb
    acc = _dot_bf16(xn_scr[...], wb)
    tn = o_ref.shape[1]
    col = lax.broadcasted_iota(jnp.int32, (1, tn), 1) + j * tn
    o_ref[...] = jnp.where(col >= COL_GATE, _sigmoid(acc + b_ref[...]), acc)


def _proj_cast(x, g, w, bias):
    m, k = x.shape
    tn = COL_GATE - COL_GATE_SRC
    assert tn % LANES == 0 and COL_GATE_SRC % tn == 0
    gap_block = COL_GATE_SRC // tn
    return pl.pallas_call(
        functools.partial(_proj_cast_kernel, gap_block=gap_block),
        out_shape=[jax.ShapeDtypeStruct((m, D_IN_PAD), F32), jax.ShapeDtypeStruct((k, D_IN_PAD), BF16)],
        grid=(D_IN_PAD // tn,),
        in_specs=[
            pl.BlockSpec((m, k), lambda j: (0, 0)),
            pl.BlockSpec((1, k), lambda j: (0, 0)),
            pl.BlockSpec((k, tn), lambda j: (0, jnp.where(j < gap_block, j, jnp.maximum(j - 1, 0)))),
            pl.BlockSpec((1, tn), lambda j: (0, j)),
        ],
        out_specs=[pl.BlockSpec((m, tn), lambda j: (0, j)), pl.BlockSpec((k, tn), lambda j: (0, j))],
        scratch_shapes=[pltpu.VMEM((m, k), BF16)],
        compiler_params=_params(("arbitrary",)),
        name="proj_sample_cast_w_in",
    )(x, g.reshape(1, k), w, bias)


def _proj_kernel(x_ref, g_ref, w_ref, b_ref, o_ref, xn_scr):
    @pl.when(pl.program_id(1) == 0)
    def _():
        xn_scr[...] = _rms(x_ref[...], g_ref[...]).astype(BF16)

    acc = _dot_bf16(xn_scr[...], w_ref[...])
    tn = o_ref.shape[1]
    col = lax.broadcasted_iota(jnp.int32, (1, tn), 1) + pl.program_id(1) * tn
    o_ref[...] = jnp.where(col >= COL_GATE, _sigmoid(acc + b_ref[...]), acc)


def _proj(x, g, w, bias, tm, name):
    m, k = x.shape
    n = w.shape[1]
    return pl.pallas_call(
        _proj_kernel,
        out_shape=jax.ShapeDtypeStruct((m, n), F32),
        grid=(m // tm, n // PROJ_TN),
        in_specs=[
            pl.BlockSpec((tm, k), lambda i, j: (i, 0)),
            pl.BlockSpec((1, k), lambda i, j: (0, 0)),
            pl.BlockSpec((k, PROJ_TN), lambda i, j: (0, j)),
            pl.BlockSpec((1, PROJ_TN), lambda i, j: (0, j)),
        ],
        out_specs=pl.BlockSpec((tm, PROJ_TN), lambda i, j: (i, j)),
        scratch_shapes=[pltpu.VMEM((tm, k), BF16)],
        compiler_params=_params(("parallel", "arbitrary")),
        name=name,
    )(x, g.reshape(1, k), w, bias)


def _matmul_kernel(a_ref, w_ref, o_ref):
    o_ref[...] = _dot_bf16(a_ref[...], w_ref[...].astype(BF16))


def _matmul(a, w, tm, tn, name):
    m, k = a.shape
    n = w.shape[1]
    return pl.pallas_call(
        _matmul_kernel,
        out_shape=jax.ShapeDtypeStruct((m, n), F32),
        grid=(m // tm, n // tn),
        in_specs=[pl.BlockSpec((tm, k), lambda i, j: (i, 0)), pl.BlockSpec((k, tn), lambda i, j: (0, j))],
        out_specs=pl.BlockSpec((tm, tn), lambda i, j: (i, j)),
        compiler_params=_params(("parallel", "arbitrary")),
        name=name,
    )(a, w)


def _prep_kernel(p_ref, prev_ref, mu_ref, wl_ref, w0_ref, a0_ref, kk_ref, ka_ref, rk_ref, lmat_ref, sel_ref,
                 hs_ref, hb_ref, f_ref, gc_ref, *, rows_are_time):
    p = p_ref[...]
    tm = p.shape[0]
    if rows_are_time:
        last = jnp.where(pl.program_id(0) == 0, 0.0, prev_ref[7:8, :])
        row = lax.broadcasted_iota(jnp.int32, (tm, 1), 0)
        prev = jnp.where(row == 0, last, pltpu.roll(p, 1, axis=0))
    else:
        prev = prev_ref[...]
    xm = p + mu_ref[...] * (prev - p)
    r = xm[:, 0:D_A]
    k = xm[:, D_A:2 * D_A]
    v = xm[:, 2 * D_A:3 * D_A]
    xl = xm[:, 3 * D_A:D_SHIFT]
    lane = lax.broadcasted_iota(jnp.int32, xl.shape, 1)
    act = jnp.where(lane < D_DECAY_LORA, jnp.tanh(xl),
                    jnp.where(lane < D_DECAY_LORA + D_AAA_LORA, xl, _sigmoid(xl)))
    act_hi, act_lo = _split2(act)
    w_hi, w_lo = wl_ref[0], wl_ref[1]
    lora = _dot_bf16(act_hi, w_hi) + _dot_bf16(act_hi, w_lo) + _dot_bf16(act_lo, w_hi)
    y = -(w0_ref[...] + lora[:, 0:D_A])
    softplus = jnp.maximum(y, 0.0) + jnp.log(1.0 + jnp.exp(-jnp.abs(y)))
    logw = -jnp.exp(-softplus - 0.5)
    a = _sigmoid(a0_ref[...] + lora[:, D_A:2 * D_A])
    g = lora[:, 2 * D_A:3 * D_A]
    kkr = k * kk_ref[...]
    kp = k * (1.0 + (a - 1.0) * ka_ref[...])
    seg = _head_sums(jnp.concatenate([kkr * kkr, r * kp * rk_ref[...]], axis=0), hs_ref[...], hb_ref[...])
    kk = kkr / jnp.maximum(jnp.sqrt(seg[0:tm]), 1e-12)
    cum = _dot_exact_lhs(lmat_ref[...], logw)
    e_out = jnp.exp(-cum)
    feats = {F_RT: r * jnp.exp(cum), F_KT: kp * e_out, F_QT: kk * a * e_out, F_PT: -kk * jnp.exp(cum - logw),
             F_V: v, F_BONUS: seg[tm:2 * tm] * v, F_G: g}
    for n, val in feats.items():
        f_ref[:, n * D_A:(n + 1) * D_A] = val
    gc_ref[...] = jnp.exp(_dot_exact_lhs(sel_ref[...], cum))


def _rwkv_prep(proj, prev, tm, chunk, lw, name):
    m = proj.shape[0]
    rows_are_time = prev is None
    n_tiles = m // tm
    t = np.arange(tm)
    lmat = ((t[:, None] // chunk == t[None, :] // chunk) & (t[None, :] <= t[:, None])).astype(np.float32)
    sel_stride = 1 if chunk == 1 else 8
    n_sel = sel_stride * (tm // chunk)
    sel = np.zeros((n_sel, tm), np.float32)
    for c in range(tm // chunk):
        sel[sel_stride * c, (c + 1) * chunk - 1] = 1.0
    if rows_are_time:
        prev_arr = proj
        prev_spec = pl.BlockSpec((8, D_SHIFT), lambda i: (jnp.maximum(i * (tm // 8) - 1, 0), 0))
    else:
        prev_arr = prev
        prev_spec = pl.BlockSpec((tm, D_SHIFT), lambda i: (i, 0))
    vec = lambda d: pl.BlockSpec((1, d), lambda i: (0, 0))
    full = lambda a: pl.BlockSpec(a.shape, lambda i: (0, 0))
    lmat_b = jnp.asarray(lmat, BF16)
    sel_b = jnp.asarray(sel, BF16)
    return pl.pallas_call(
        functools.partial(_prep_kernel, rows_are_time=rows_are_time),
        out_shape=[jax.ShapeDtypeStruct((m, N_FEATS * D_A), F32),
                   jax.ShapeDtypeStruct((n_tiles * n_sel, D_A), F32)],
        grid=(n_tiles,),
        in_specs=[pl.BlockSpec((tm, D_SHIFT), lambda i: (i, 0)), prev_spec, vec(D_SHIFT),
                  pl.BlockSpec(lw["w_lora"].shape, lambda i: (0, 0, 0)),
                  vec(D_A), vec(D_A), vec(D_A), vec(D_A), vec(D_A), full(lmat_b), full(sel_b), full(lw["hsum"]),
                  full(lw["hbc"])],
        out_specs=[pl.BlockSpec((tm, N_FEATS * D_A), lambda i: (i, 0)), pl.BlockSpec((n_sel, D_A), lambda i: (i, 0))],
        compiler_params=_params(("parallel",)),
        name=name,
    )(proj, prev_arr, lw["mu"], lw["w_lora"], lw["w0"], lw["a0"], lw["k_k"], lw["k_a"], lw["r_k"], lmat_b, sel_b,
      lw["hsum"], lw["hbc"])


_NN = (((2,), (1,)), ((0,), (0,)))
_NT = (((2,), (2,)), ((0,), (0,)))
PAIRS = H_A // 2
LOG2_HEAD = HEAD_A.bit_length() - 1
assert HEAD_A == 1 << LOG2_HEAD and 2 * HEAD_A == LANES


def _lane_head(n):
    return (lax.broadcasted_iota(jnp.int32, (1, 1, n), 2) >> LOG2_HEAD) & 1


def _keep_head(x, head, which):
    return jnp.where(head == which, x, jnp.zeros_like(x))


def _pdot_nn(a, b):
    ah, al = _split2(a)
    bh, bl = _split2(b)
    head = _lane_head(b.shape[2])
    bd = lambda x: jnp.concatenate([_keep_head(x, head, 0), _keep_head(x, head, 1)], axis=1)
    a_cat = jnp.concatenate([ah, ah, al], axis=2)
    b_cat = jnp.concatenate([bd(bh), bd(bl), bd(bh)], axis=1)
    return lax.dot_general(a_cat, b_cat, _NN, preferred_element_type=F32)


def _pdot_nt(a, bs):
    ah, al = _split2(a)
    head = _lane_head(LANES)
    hi, lo = [], []
    for b in bs:
        bh, bl = _split2(b)
        for which in (0, 1):
            hi.append(_keep_head(bh, head, which))
            lo.append(_keep_head(bl, head, which))
    bh, bl = jnp.concatenate(hi, axis=1), jnp.concatenate(lo, axis=1)
    a_cat = jnp.concatenate([ah, ah, al], axis=2)
    b_cat = jnp.concatenate([bh, bl, bh], axis=2)
    return lax.dot_general(a_cat, b_cat, _NT, preferred_element_type=F32)


def _pdot_tn(x, y):
    nx = x.shape[2]
    xh, xl = _split2(jnp.swapaxes(x, 1, 2))
    yh, yl = _split2(y)
    x_cat = jnp.concatenate([xh, xh, xl, jnp.zeros_like(xl)], axis=2)
    y_cat = jnp.concatenate([yh, yl, yh, jnp.zeros_like(yl)], axis=1)
    tot = lax.dot_general(x_cat, y_cat, _NN, preferred_element_type=F32)
    head = _lane_head(LANES)
    return [jnp.where(head == 0, tot[:, g:g + HEAD_A], tot[:, g + HEAD_A:g + LANES]) for g in range(0, nx, LANES)]


def _scan_kernel(f_ref, gc_ref, gng_ref, gnb_ref, ya_ref, sfin_ref, s_scr):
    @pl.when(pl.program_id(0) == 0)
    def _():
        s_scr[...] = jnp.zeros_like(s_scr)

    def pairs(x, rows):
        return jnp.stack([x[c * rows:(c + 1) * rows, i * LANES:(i + 1) * LANES]
                          for c in range(SCAN_CHUNKS) for i in range(PAIRS)], axis=0)

    feat = lambda n: f_ref[:, n * D_A:(n + 1) * D_A]
    p, r, q, k, v = (pairs(feat(n), CHUNK) for n in (F_PT, F_RT, F_QT, F_KT, F_V))
    gam = pairs(gc_ref[...], SUBLANES)[:, 0:1]
    ri = lax.broadcasted_iota(jnp.int32, (1, CHUNK, LANES), 1)
    ci = lax.broadcasted_iota(jnp.int32, (1, CHUNK, LANES), 2) & (HEAD_A - 1)
    strict = ri > ci
    incl = ri >= ci
    eye = (ri == ci).astype(F32)

    gram = _pdot_nt(jnp.concatenate([p, r], axis=1), [q, k])
    a_qp = jnp.where(strict, gram[:, 0:CHUNK, 0:LANES], 0.0)
    a_kp = jnp.where(strict, gram[:, 0:CHUNK, LANES:], 0.0)
    a_rq = jnp.where(incl, gram[:, CHUNK:, 0:LANES], 0.0)
    a_rk = jnp.where(incl, gram[:, CHUNK:, LANES:], 0.0)
    same = lambda log2_bs: (ri >> log2_bs) == (ci >> log2_bs)
    a_d = jnp.where(same(4), a_qp, 0.0)
    tinv = eye + a_d
    pw = _pdot_nn(a_d, a_d)
    for _ in range(2):
        both = _pdot_nn(jnp.concatenate([pw, tinv], axis=1), pw)
        pw = both[:, 0:CHUNK]
        tinv = tinv + both[:, CHUNK:]
    tinv = tinv + _pdot_nn(tinv, pw)
    for log2_bs in (4, 5):
        off = jnp.where(jnp.logical_and(same(log2_bs + 1), jnp.logical_not(same(log2_bs))), a_qp, 0.0)
        tinv = tinv + _pdot_nn(tinv, _pdot_nn(off, tinv))
    av = _pdot_nn(jnp.concatenate([a_kp, a_rk], axis=1), v)
    pw_hat = _pdot_nn(tinv, jnp.concatenate([p, av[:, 0:CHUNK]], axis=2))
    x = _pdot_nn(a_rq, pw_hat)
    r_hat = r + x[:, :, 0:LANES]
    o_loc = x[:, :, LANES:] + av[:, CHUNK:]
    y_p, y_w = _pdot_tn(pw_hat, q)
    (vk,) = _pdot_tn(v, k)
    m_mat = (eye + y_p) * gam
    n_mat = (y_w + vk) * gam

    state = s_scr[...]
    outs = []
    for c in range(SCAN_CHUNKS):
        sl = slice(c * PAIRS, (c + 1) * PAIRS)
        outs.append(_pdot_nt(r_hat[sl], [state]) + o_loc[sl])
        state = _pdot_nn(state, m_mat[sl]) + n_mat[sl]
    s_scr[...] = state
    o = jnp.concatenate(outs, axis=0)

    head = _lane_head(LANES)

    def head_mean(z):
        sums = [jnp.sum(_keep_head(z, head, which), axis=2, keepdims=True) for which in (0, 1)]
        return jnp.where(head == 0, sums[0], sums[1]) * (1.0 / HEAD_A)

    d = o - head_mean(o)
    o_n = d * lax.rsqrt(head_mean(d * d) + EPS_GN)
    o_n = jnp.concatenate([jnp.concatenate([o_n[c * PAIRS + i] for i in range(PAIRS)], axis=1)
                           for c in range(SCAN_CHUNKS)], axis=0)
    ya_ref[...] = ((o_n * gng_ref[...] + gnb_ref[...] + feat(F_BONUS)) * feat(F_G)).astype(ya_ref.dtype)

    @pl.when(pl.program_id(0) == pl.num_programs(0) - 1)
    def _():
        sfin_ref[...] = s_scr[...]


def _wkv_scan(feats, gc, lw):
    t = feats.shape[0]
    vec = pl.BlockSpec((1, D_A), lambda c: (0, 0))
    pair_state = (PAIRS, HEAD_A, LANES)
    rows = SCAN_CHUNKS * CHUNK
    ya, s = pl.pallas_call(
        _scan_kernel,
        out_shape=[jax.ShapeDtypeStruct((t, D_A), BF16), jax.ShapeDtypeStruct(pair_state, F32)],
        grid=(t // rows,),
        in_specs=[pl.BlockSpec((rows, N_FEATS * D_A), lambda c: (c, 0)),
                  pl.BlockSpec((SUBLANES * SCAN_CHUNKS, D_A), lambda c: (c, 0)), vec, vec],
        out_specs=[pl.BlockSpec((rows, D_A), lambda c: (c, 0)), pl.BlockSpec(pair_state, lambda c: (0, 0, 0))],
        scratch_shapes=[pltpu.VMEM(pair_state, F32)],
        compiler_params=_params(("arbitrary",)),
        name="wkv_scan_prompt",
    )(feats, gc, lw["gn_g"], lw["gn_b"])
    s = s.reshape(PAIRS, HEAD_A, 2, HEAD_A).transpose(0, 2, 1, 3).reshape(H_A, HEAD_A, HEAD_A)
    return ya, s


def _wkv_step_kernel(s_ref, rows_ref, sn_ref, o_ref):
    ri = lax.broadcasted_iota(jnp.int32, (1, 1, HEAD_A, HEAD_A), 2)
    ci = lax.broadcasted_iota(jnp.int32, (1, 1, HEAD_A, HEAD_A), 3)
    eye = ri == ci
    pt, rt, qt, kt, v, gc = (rows_ref[:, n] for n in range(6))
    s = s_ref[0]
    u = jnp.sum(s * pt, axis=3, keepdims=True)
    v_col = jnp.sum(jnp.where(eye, v, 0.0), axis=3, keepdims=True)
    m = s + u * qt + v_col * kt
    sn_ref[0] = m * gc
    o_col = jnp.sum(m * rt, axis=3, keepdims=True)
    o_ref[...] = jnp.sum(jnp.where(eye, o_col, 0.0), axis=2, keepdims=True)


def _wkv_step(state, pt, rt, qt, kt, v, gc, bt):
    b = pt.shape[0]
    rows = jnp.stack([pt, rt, qt, kt, v, gc], axis=1).reshape(b, 6, H_A, 1, HEAD_A)
    st_spec = pl.BlockSpec((1, bt, H_A, HEAD_A, HEAD_A), lambda i: (0, i, 0, 0, 0))
    sn, o = pl.pallas_call(
        _wkv_step_kernel,
        out_shape=[jax.ShapeDtypeStruct(state.shape, F32), jax.ShapeDtypeStruct((b, H_A, 1, HEAD_A), F32)],
        grid=(b // bt,),
        in_specs=[st_spec, pl.BlockSpec((bt, 6, H_A, 1, HEAD_A), lambda i: (i, 0, 0, 0, 0))],
        out_specs=[st_spec, pl.BlockSpec((bt, H_A, 1, HEAD_A), lambda i: (i, 0, 0, 0))],
        compiler_params=_params(("parallel",)),
        name="wkv_step_sample",
    )(state, rows)
    return sn, o.reshape(b, D_A)


def _post_kernel(o_ref, bonus_ref, g_ref, gng_ref, gnb_ref, hs_ref, hb_ref, y_ref):
    o = o_ref[...]
    hsum, hbc = hs_ref[...], hb_ref[...]
    mu = _head_sums(o, hsum, hbc) * (1.0 / HEAD_A)
    d = o - mu
    var = _head_sums(d * d, hsum, hbc) * (1.0 / HEAD_A)
    o_n = d * lax.rsqrt(var + EPS_GN) * gng_ref[...] + gnb_ref[...]
    y_ref[...] = ((o_n + bonus_ref[...]) * g_ref[...]).astype(y_ref.dtype)


def _rwkv_post(o, bonus, g, lw, tm, name):
    m = o.shape[0]
    big = pl.BlockSpec((tm, D_A), lambda i: (i, 0))
    vec = pl.BlockSpec((1, D_A), lambda i: (0, 0))
    return pl.pallas_call(
        _post_kernel,
        out_shape=jax.ShapeDtypeStruct((m, D_A), BF16),
        grid=(m // tm,),
        in_specs=[big, big, big, vec, vec, pl.BlockSpec((D_A, LANES), lambda i: (0, 0)),
                  pl.BlockSpec((LANES, D_A), lambda i: (0, 0))],
        out_specs=big,
        compiler_params=_params(("parallel",)),
        name=name,
    )(o, bonus, g, lw["gn_g"], lw["gn_b"], lw["hsum"], lw["hbc"])


def _attn_kernel(q_ref, kc_ref, vc_ref, kh_ref, vh_ref, bias_ref, o_ref, l_ref, *, dil, m_blocks, tiled):
    scale = HEAD_B ** -0.5
    n_streams = SUBLANES if tiled else dil
    n_units = n_streams * m_blocks

    stride = SUBLANES if tiled else dil
    if tiled:
        flat = lambda ref: ref.reshape(ref.shape[0] * SUBLANES, LANES)
        q_ref, kc_ref, vc_ref, kh_ref, vh_ref, o_ref, l_ref = map(flat, (q_ref, kc_ref, vc_ref, kh_ref, vh_ref,
                                                                          o_ref, l_ref))

    def rows(ref, r, start, size):
        if stride == 1:
            return ref[pl.ds(start, size), :]
        return ref[pl.ds(r + stride * start, size, stride=stride), :]

    def put(ref, r, start, val):
        if stride == 1:
            ref[pl.ds(start, BLK), :] = val
        else:
            ref[pl.ds(r + stride * start, BLK, stride=stride), :] = val

    qs, ks, vs, units = [], [], [], []
    for r in range(n_streams):
        for mb in range(m_blocks):
            units.append((r, mb))
            qs.append(rows(q_ref, r, BLK * mb, BLK))
            if mb == 0:
                ks.append(jnp.concatenate([rows(kh_ref, r, 0, BLK), rows(kc_ref, r, 0, BLK)], axis=0))
                vs.append(jnp.concatenate([rows(vh_ref, r, 0, BLK), rows(vc_ref, r, 0, BLK)], axis=0))
            else:
                ks.append(rows(kc_ref, r, BLK * (mb - 1), 2 * BLK))
                vs.append(rows(vc_ref, r, BLK * (mb - 1), 2 * BLK))

    def batch(xs, mul=None):
        pre = (lambda x: x) if mul is None else (lambda x: x * mul)
        return jnp.stack([pre(x[:, sub * HEAD_B:(sub + 1) * HEAD_B]).astype(BF16) for sub in range(2) for x in xs],
                         axis=0)

    assert math.frexp(scale)[0] == 0.5
    qb, kb, vb = batch(qs, scale), batch(ks), batch(vs)
    s = lax.dot_general(qb, kb, (((2,), (2,)), ((0,), (0,))), preferred_element_type=F32)
    s = s.reshape(2, n_units, BLK, 2 * BLK) + bias_ref[...][:, None]
    unit = lax.broadcasted_iota(jnp.int32, (1, n_units, 1, 2 * BLK), 1)
    col = lax.broadcasted_iota(jnp.int32, (1, n_units, 1, 2 * BLK), 3)
    no_prev = jnp.logical_and(jnp.logical_and(pl.program_id(0) == 0, (unit & (m_blocks - 1)) == 0), col < BLK)
    s = jnp.where(no_prev, -jnp.inf, s)
    m = jnp.max(s, axis=-1, keepdims=True)
    p = jnp.exp(s - m)
    l = jnp.sum(p, axis=-1, keepdims=True)
    pv = lax.dot_general(p.astype(BF16).reshape(2 * n_units, BLK, 2 * BLK), vb, (((2,), (1,)), ((0,), (0,))),
                         preferred_element_type=F32)
    o = pv.reshape(2, n_units, BLK, HEAD_B) / l
    lse = jnp.broadcast_to(m + jnp.log(l), (2, n_units, BLK, HEAD_B))
    for u, (r, mb) in enumerate(units):
        put(o_ref, r, BLK * mb, jnp.concatenate([o[0, u], o[1, u]], axis=1))
        put(l_ref, r, BLK * mb, jnp.concatenate([lse[0, u], lse[1, u]], axis=1))


def _attn_prompt(proj, bias, gi, rows_per_step):
    t, n_cols = proj.shape
    dil = DILATIONS[gi]
    span = BLK * dil
    tiled = dil % SUBLANES == 0
    if tiled:
        rows_per_step *= dil // SUBLANES
    m_blocks = rows_per_step // span
    cq = (COL_QKV + gi * D_G) // LANES
    ck = (COL_QKV + D_B + gi * D_G) // LANES
    cv = (COL_QKV + 2 * D_B + gi * D_G) // LANES
    bias_block = (2, BLK, 2 * BLK)
    if tiled:
        n_hi = dil // SUBLANES
        src = proj.reshape(t // dil, n_hi, SUBLANES, n_cols)
        grid = (t // rows_per_step, n_hi, H_G // 2)
        l_rows = rows_per_step // dil
        cur = lambda c0: pl.BlockSpec((l_rows, None, SUBLANES, LANES), lambda i, rh, hp: (i, rh, 0, c0 + hp))
        halo = lambda c0: pl.BlockSpec((BLK, None, SUBLANES, LANES),
                                       lambda i, rh, hp: (jnp.maximum(i * m_blocks - 1, 0), rh, 0, c0 + hp))
        out_spec = pl.BlockSpec((l_rows, None, SUBLANES, LANES), lambda i, rh, hp: (i, rh, 0, hp))
        out = jax.ShapeDtypeStruct((t // dil, n_hi, SUBLANES, D_G), F32)
        bias_spec = pl.BlockSpec(bias_block, lambda i, rh, hp: (hp, 0, 0))
    else:
        src = proj
        grid = (t // rows_per_step, H_G // 2)
        cur = lambda c0: pl.BlockSpec((rows_per_step, LANES), lambda i, hp: (i, c0 + hp))
        halo = lambda c0: pl.BlockSpec((span, LANES), lambda i, hp: (jnp.maximum(i * m_blocks - 1, 0), c0 + hp))
        out_spec = pl.BlockSpec((rows_per_step, LANES), lambda i, hp: (i, hp))
        out = jax.ShapeDtypeStruct((t, D_G), F32)
        bias_spec = pl.BlockSpec(bias_block, lambda i, hp: (hp, 0, 0))
    o, l = pl.pallas_call(
        functools.partial(_attn_kernel, dil=dil, m_blocks=m_blocks, tiled=tiled),
        out_shape=[out, out],
        grid=grid,
        in_specs=[cur(cq), cur(ck), cur(cv), halo(ck), halo(cv), bias_spec],
        out_specs=[out_spec, out_spec],
        compiler_params=_params(("parallel",) * len(grid)),
        name=f"attn_prompt_g{gi}",
    )(src, src, src, src, src, bias)
    return o.reshape(t, D_G), l.reshape(t, D_G)


def _attn_step_kernel(q_ref, kn_ref, vn_ref, c_ref, bias_ref, bias0_ref, o_ref, l_ref):
    scale = HEAD_B ** -0.5
    ri = lax.broadcasted_iota(jnp.int32, (1, 1, HEAD_B, HEAD_B), 2)
    ci = lax.broadcasted_iota(jnp.int32, (1, 1, HEAD_B, HEAD_B), 3)
    eye = ri == ci
    q = q_ref[...]
    q_col = jnp.sum(jnp.where(eye, q, 0.0), axis=3, keepdims=True)
    s = jnp.sum(c_ref[:, 0] * q_col, axis=2, keepdims=True) * scale + bias_ref[...]
    s_new = jnp.sum(q * kn_ref[...], axis=3, keepdims=True) * scale + bias0_ref[...]
    m = jnp.maximum(jnp.max(s, axis=3, keepdims=True), s_new)
    p = jnp.exp(s - m)
    p_new = jnp.exp(s_new - m)
    l = jnp.sum(p, axis=3, keepdims=True) + p_new
    o_col = jnp.sum(c_ref[:, 1] * p, axis=3, keepdims=True)
    o_row = jnp.sum(jnp.where(eye, o_col, 0.0), axis=2, keepdims=True)
    o_ref[...] = (o_row + p_new * vn_ref[...]) / l
    l_ref[...] = jnp.broadcast_to(m + jnp.log(l), o_ref.shape)


def _attn_step(q, k_new, v_new, cache, bias, bias0, gi):
    b = q.shape[0]
    w = cache.shape[2]
    bt = max(1, min(b, STEP_CACHE_WORDS // (2 * D_G * w)))
    cache_t = jnp.transpose(cache, (0, 1, 3, 4, 5, 2))
    vec = pl.BlockSpec((bt, H_G, 1, HEAD_B), lambda i: (i, 0, 0, 0))
    out = jax.ShapeDtypeStruct((b, H_G, 1, HEAD_B), F32)
    o, l = pl.pallas_call(
        _attn_step_kernel,
        out_shape=[out, out],
        grid=(b // bt,),
        in_specs=[vec, vec, vec,
                  pl.BlockSpec((None, bt, 2, H_G, HEAD_B, w), lambda i: (0, i, 0, 0, 0, 0)),
                  pl.BlockSpec((H_G, 1, w), lambda i: (0, 0, 0)),
                  pl.BlockSpec((H_G, 1, 1), lambda i: (0, 0, 0))],
        out_specs=[vec, vec],
        compiler_params=_params(("parallel",)),
        name=f"attn_step_g{gi}",
    )(q, k_new, v_new, cache_t, bias, bias0)
    return o.reshape(b, D_G), l.reshape(b, D_G)


def _mix_kernel(ya_ref, o1_ref, o2_ref, o3_ref, l1_ref, l2_ref, l3_ref, wa_ref, wb_ref, ga_ref, gb_ref, o_ref):
    l1, l2, l3 = l1_ref[...], l2_ref[...], l3_ref[...]
    m = jnp.maximum(jnp.maximum(l1, l2), l3)
    e1, e2, e3 = jnp.exp(l1 - m), jnp.exp(l2 - m), jnp.exp(l3 - m)
    den = e1 + e2 + e3
    yb = ((e1 / den) * o1_ref[...] + (e2 / den) * o2_ref[...] + (e3 / den) * o3_ref[...]).astype(BF16)
    mixed = ga_ref[...] * _dot_bf16(ya_ref[...], wa_ref[...]) + gb_ref[...] * _dot_bf16(yb, wb_ref[...])
    o_ref[...] = mixed.astype(o_ref.dtype)


def _mix(ya, outs, lses, wa, wb, proj, tm, name):
    m = ya.shape[0]
    n = D_MODEL
    assert COL_GATE % n == 0
    ga0 = COL_GATE // n
    grp = pl.BlockSpec((tm, D_G), lambda i: (i, 0))
    resident = lambda rows: pl.BlockSpec((rows, n), lambda i: (0, 0), pipeline_mode=pl.Buffered(1))
    return pl.pallas_call(
        _mix_kernel,
        out_shape=jax.ShapeDtypeStruct((m, n), BF16),
        grid=(m // tm,),
        in_specs=[pl.BlockSpec((tm, D_A), lambda i: (i, 0))] + [grp] * 6 + [
            resident(D_A), resident(D_G),
            pl.BlockSpec((tm, n), lambda i: (i, ga0)),
            pl.BlockSpec((tm, n), lambda i: (i, ga0 + 1)),
        ],
        out_specs=pl.BlockSpec((tm, n), lambda i: (i, 0)),
        compiler_params=_params(("parallel",)),
        name=name,
    )(ya, *outs, *lses, wa, wb, proj, proj)


def _wo_kernel(a_ref, w_ref, x_ref, g_ref, h_ref, hn_ref):
    h = x_ref[...] + _dot_bf16(a_ref[...], w_ref[...])
    h_ref[...] = h
    hn_ref[...] = _rms(h, g_ref[...]).astype(BF16)


def _wo(a, w, x, g, tm, name):
    m, k = a.shape
    n = w.shape[1]
    row = pl.BlockSpec((tm, n), lambda i: (i, 0))
    return pl.pallas_call(
        _wo_kernel,
        out_shape=[jax.ShapeDtypeStruct((m, n), F32), jax.ShapeDtypeStruct((m, n), BF16)],
        grid=(m // tm,),
        in_specs=[pl.BlockSpec((tm, k), lambda i: (i, 0)), pl.BlockSpec((k, n), lambda i: (0, 0)), row,
                  pl.BlockSpec((1, n), lambda i: (0, 0))],
        out_specs=[row, row],
        compiler_params=_params(("parallel",)),
        name=name,
    )(a, w, x, g.reshape(1, n))


def _gelu(x):
    return 0.5 * x * (1.0 + lax.erf(x * (1.0 / math.sqrt(2.0))))


def _ffn_up_kernel(h_ref, w1_ref, w2_ref, cw1_ref, cw2_ref, cb1_ref, cb2_ref, act_ref, t1_ref, t2_ref,
                   c1_scr, c2_scr, w1_scr, w2_scr):
    tm = h_ref.shape[0]

    @pl.when(pl.program_id(1) == 0)
    def _():
        c1_scr[...] = jnp.zeros_like(c1_scr)
        c2_scr[...] = jnp.zeros_like(c2_scr)
        w1_scr[...] = w1_ref[...].astype(BF16)
        w2_scr[...] = w2_ref[...].astype(BF16)

    a = h_ref[...]
    row = lax.broadcasted_iota(jnp.int32, (tm, 1), 0)

    def conv(u, carry_ref, cw_ref, cb_ref):
        m1 = jnp.where(row == 0, carry_ref[7:8, :], pltpu.roll(u, 1, axis=0))
        m2 = jnp.where(row == 0, carry_ref[6:7, :], jnp.where(row == 1, carry_ref[7:8, :], pltpu.roll(u, 2, axis=0)))
        return cb_ref[...] + cw_ref[0:1, :] * m2 + cw_ref[1:2, :] * m1 + cw_ref[2:3, :] * u

    u1 = _dot_bf16(a, w1_scr[...])
    u2 = _dot_bf16(a, w2_scr[...])
    c1 = conv(u1, c1_scr, cw1_ref, cb1_ref)
    c2 = conv(u2, c2_scr, cw2_ref, cb2_ref)
    act_ref[...] = (_gelu(c1) * c2).astype(act_ref.dtype)
    c1_scr[...] = u1[tm - 8:tm]
    c2_scr[...] = u2[tm - 8:tm]
    t1_ref[...] = u1[tm - 8:tm]
    t2_ref[...] = u2[tm - 8:tm]


def _ffn_up_prompt(hn, w_up, conv_w, conv_b, tm, tn):
    m = hn.shape[0]
    nj = D_FF // tn
    tail = jax.ShapeDtypeStruct((8, D_FF), F32)
    return pl.pallas_call(
        _ffn_up_kernel,
        out_shape=[jax.ShapeDtypeStruct((m, D_FF), BF16), tail, tail],
        grid=(nj, m // tm),
        in_specs=[
            pl.BlockSpec((tm, D_MODEL), lambda j, i: (i, 0)),
            pl.BlockSpec((D_MODEL, tn), lambda j, i: (0, j)),
            pl.BlockSpec((D_MODEL, tn), lambda j, i: (0, nj + j)),
            pl.BlockSpec((CONV_W, tn), lambda j, i: (0, j)),
            pl.BlockSpec((CONV_W, tn), lambda j, i: (0, nj + j)),
            pl.BlockSpec((1, tn), lambda j, i: (0, j)),
            pl.BlockSpec((1, tn), lambda j, i: (0, nj + j)),
        ],
        out_specs=[pl.BlockSpec((tm, tn), lambda j, i: (i, j)),
                   pl.BlockSpec((8, tn), lambda j, i: (0, j)),
                   pl.BlockSpec((8, tn), lambda j, i: (0, j))],
        scratch_shapes=[pltpu.VMEM((8, tn), F32), pltpu.VMEM((8, tn), F32),
                        pltpu.VMEM((D_MODEL, tn), BF16), pltpu.VMEM((D_MODEL, tn), BF16)],
        compiler_params=_params(("arbitrary", "arbitrary")),
        name="ffn_up_prompt",
    )(hn, w_up, w_up, conv_w, conv_w, conv_b, conv_b)


def _ffn_act_step_kernel(up_ref, prev_ref, cw_ref, cb_ref, act_ref):
    up = up_ref[...]
    w = 2 * D_FF
    c = cb_ref[...] + cw_ref[0:1, :] * prev_ref[:, 0:w] + cw_ref[1:2, :] * prev_ref[:, w:2 * w] + cw_ref[2:3, :] * up
    act_ref[...] = (_gelu(c[:, 0:D_FF]) * c[:, D_FF:w]).astype(act_ref.dtype)


def _ffn_act_step(up, conv_prev, conv_w, conv_b):
    b = up.shape[0]
    return pl.pallas_call(
        _ffn_act_step_kernel,
        out_shape=jax.ShapeDtypeStruct((b, D_FF), BF16),
        compiler_params=pltpu.CompilerParams(vmem_limit_bytes=VMEM_LIMIT),
        name="ffn_act_sample",
    )(up, conv_prev.reshape(b, (CONV_W - 1) * 2 * D_FF), conv_w, conv_b)


def _down_kernel(a_ref, w_ref, h_ref, g_ref, o_ref, acc_ref):
    @pl.when(pl.program_id(1) == 0)
    def _():
        acc_ref[...] = h_ref[...]

    out = acc_ref[...] + _dot_bf16(a_ref[...], w_ref[...])
    acc_ref[...] = out
    inv = lax.rsqrt(jnp.mean(out * out, axis=-1, keepdims=True) + EPS_RMS)
    o_ref[...] = out * inv * g_ref[...]


def _ffn_down(act, w_down, h, g, tm, tk, name):
    m = act.shape[0]
    nk = D_FF // tk
    return pl.pallas_call(
        _down_kernel,
        out_shape=jax.ShapeDtypeStruct((m, D_MODEL), F32),
        grid=(m // tm, nk),
        in_specs=[
            pl.BlockSpec((tm, tk), lambda i, k: (i, k)),
            pl.BlockSpec((tk, D_MODEL), lambda i, k: (k, 0)),
            pl.BlockSpec((tm, D_MODEL), lambda i, k: (i, 0)),
            pl.BlockSpec((1, D_MODEL), lambda i, k: (0, 0)),
        ],
        out_specs=pl.BlockSpec((tm, D_MODEL), lambda i, k: (i, 0)),
        scratch_shapes=[pltpu.VMEM((tm, D_MODEL), F32)],
        compiler_params=_params(("parallel", "arbitrary")),
        name=name,
    )(act, w_down, h, g.reshape(1, D_MODEL))


def _rel_bucket(dist):
    max_exact = N_BUCKETS // 2
    d_f = np.maximum(dist, 1).astype(np.float64)
    scaled = np.log(d_f / max_exact) / math.log(MAX_DISTANCE / max_exact) * (N_BUCKETS - max_exact)
    large = np.minimum(max_exact + scaled.astype(np.int64), N_BUCKETS - 1)
    return np.where(dist < max_exact, dist, large).astype(np.int32)


def _bias_rows(tab, dist):
    onehot = (jnp.asarray(_rel_bucket(dist))[None, :] == jnp.arange(N_BUCKETS)[:, None]).astype(F32)
    return jnp.dot(tab.T, onehot, precision=HIGHEST)


def _bias_tables(rel_bias, gi):
    dil = DILATIONS[gi]
    win = WINDOWS[gi]
    reach = win // dil
    tab = rel_bias[:, gi * H_G:(gi + 1) * H_G]
    qi = np.arange(BLK)[:, None]
    ki = np.arange(2 * BLK)[None, :]
    rel = qi + BLK - ki
    blk = _bias_rows(tab, (dil * np.maximum(rel, 0)).reshape(-1)).reshape(H_G, BLK, 2 * BLK)
    blk = jnp.where(((rel >= 0) & (rel <= reach))[None], blk, -jnp.inf)
    back = win - np.arange(win)
    step = jnp.where((back % dil == 0)[None, :], _bias_rows(tab, back), -jnp.inf)[:, None, :]
    step0 = _bias_rows(tab, np.zeros((1,), np.int64))[:, None, :]
    return blk, step, step0


def _tiles(m, prompt):
    if prompt:
        return dict(tag="prompt", proj_m=1024, prep_m=2 * CHUNK, attn_rows=4096, mix_m=512, wo_m=512, up_m=1024,
                    up_n=512, down_m=512, down_k=D_FF // 2)
    return dict(tag="sample", proj_m=m, prep_m=m, step_b=4, post_m=m, mix_m=m, wo_m=m, up_n=512, down_m=m,
                down_k=D_FF // 2)


def _layer(x, lw, bias_blk, prompt, state=None, proj=None):
    m = x.shape[0]
    tiles = _tiles(m, prompt)
    if proj is None:
        proj = _proj(x, lw["norm1_g"], lw["w_in"], lw["in_bias"], tiles["proj_m"], "proj_" + tiles["tag"])

    if prompt:
        feats, gc = _rwkv_prep(proj, None, tiles["prep_m"], CHUNK, lw, "rwkv_prep_prompt")
        ya, wkv_new = _wkv_scan(feats, gc, lw)
        wkv_new = wkv_new[None, None]
    else:
        feats, gc = _rwkv_prep(proj, state["shift"], tiles["prep_m"], 1, lw, "rwkv_prep_sample")
        f = lambda n: feats[:, n * D_A:(n + 1) * D_A]
        wkv_new, o = _wkv_step(state["wkv"], f(F_PT), f(F_RT), f(F_QT), f(F_KT), f(F_V), gc, tiles["step_b"])
        ya = _rwkv_post(o, f(F_BONUS), f(F_G), lw, tiles["post_m"], "rwkv_post_" + tiles["tag"])

    outs, lses = [], []
    for gi in range(N_GROUPS):
        if prompt:
            o_g, l_g = _attn_prompt(proj, bias_blk[gi][0], gi, tiles["attn_rows"])
        else:
            sl = lambda c0: proj[:, c0 + gi * D_G:c0 + (gi + 1) * D_G].reshape(m, H_G, 1, HEAD_B)
            o_g, l_g = _attn_step(sl(COL_QKV), sl(COL_QKV + D_B), sl(COL_QKV + 2 * D_B), state["win"][gi],
                                  bias_blk[gi][1], bias_blk[gi][2], gi)
        outs.append(o_g)
        lses.append(l_g)
    mixed = _mix(ya, outs, lses, lw["w_out_a"], lw["w_out_b"], proj, tiles["mix_m"], "mix_" + tiles["tag"])
    h, hn = _wo(mixed, lw["w_o"], x, lw["norm2_g"], tiles["wo_m"], "wo_" + tiles["tag"])

    if prompt:
        act, t1, t2 = _ffn_up_prompt(hn, lw["w_up"], lw["conv_w"], lw["conv_b"], tiles["up_m"], tiles["up_n"])
        conv_new = jnp.concatenate([t1[6:8], t2[6:8]], axis=1)[None, None]
    else:
        up = _matmul(hn, lw["w_up"], m, tiles["up_n"], "up_sample")
        act = _ffn_act_step(up, state["conv"], lw["conv_w"], lw["conv_b"])
        conv_new = jnp.concatenate([state["conv"][:, 1:], up[:, None, :]], axis=1)[None]
    y = _ffn_down(act, lw["w_down"], h, lw["normf_g"], tiles["down_m"], tiles["down_k"], "down_" + tiles["tag"])
    return y, proj, wkv_new, conv_new


def kernel(x_prompt, x_sample, state_wkv, state_shift, state_ffn_conv, cache_win1, cache_win2, cache_win3, rel_bias,
           norm1_g, w_in, gate_b, mu_shift, w0, w_up_decay, a0, w_up_aaa, w_up_gate, k_k, k_a, r_k, gn_g, gn_b,
           w_out_a, w_out_b, w_o, norm2_g, w_up, conv_w, conv_b, w_down, normf_g):
    row = lambda a: a.reshape(1, -1)
    w_lora = jnp.zeros((D_LORA, 3 * D_A), F32)
    w_lora = w_lora.at[0:D_DECAY_LORA, 0:D_A].set(w_up_decay[0])
    w_lora = w_lora.at[D_DECAY_LORA:D_DECAY_LORA + D_AAA_LORA, D_A:2 * D_A].set(w_up_aaa[0])
    w_lora = w_lora.at[D_DECAY_LORA + D_AAA_LORA:, 2 * D_A:].set(w_up_gate[0])
    head = np.arange(D_A) // HEAD_A
    lw = dict(
        norm1_g=norm1_g[0], norm2_g=norm2_g[0], normf_g=normf_g,
        in_bias=jnp.concatenate([jnp.zeros((1, COL_GATE), F32), row(gate_b[0])], axis=1),
        mu=row(mu_shift[0]), w_lora=jnp.stack(_split2(w_lora)), w0=row(w0[0]), a0=row(a0[0]), k_k=row(k_k[0]), k_a=row(k_a[0]),
        r_k=row(r_k[0]), gn_g=row(gn_g[0]), gn_b=row(gn_b[0]),
        hsum=jnp.asarray(head[:, None] == np.arange(LANES)[None, :], BF16),
        hbc=jnp.asarray(np.arange(LANES)[:, None] == head[None, :], BF16),
        w_out_a=w_out_a[0].astype(BF16), w_out_b=w_out_b[0].astype(BF16), w_o=w_o[0].astype(BF16),
        w_up=w_up[0], conv_w=conv_w[0], conv_b=row(conv_b[0]), w_down=w_down[0].astype(BF16),
    )
    bias = [_bias_tables(rel_bias, gi) for gi in range(N_GROUPS)]

    proj_s, lw["w_in"] = _proj_cast(x_sample[:, 0], norm1_g[0], w_in[0], lw["in_bias"])
    y_p, proj_p, wkv_p, conv_p = _layer(x_prompt[0], lw, bias, prompt=True)

    b = DEC_BATCH
    state = dict(wkv=state_wkv, shift=state_shift.reshape(b, D_SHIFT), conv=state_ffn_conv[0],
                 win=(cache_win1, cache_win2, cache_win3))
    y_s, proj_s, wkv_s, conv_s = _layer(x_sample[:, 0], lw, bias, prompt=False, state=state, proj=proj_s)

    def kv_rows(proj, lo, gi):
        k = proj[lo:, COL_QKV + D_B + gi * D_G:COL_QKV + D_B + (gi + 1) * D_G]
        v = proj[lo:, COL_QKV + 2 * D_B + gi * D_G:COL_QKV + 2 * D_B + (gi + 1) * D_G]
        n = k.shape[0]
        return jnp.stack([k.reshape(n, H_G, HEAD_B), v.reshape(n, H_G, HEAD_B)], axis=1)

    win_p = [kv_rows(proj_p, SEQ - min(WINDOWS[gi], SEQ), gi)[None, None] for gi in range(N_GROUPS)]
    win_s = [kv_rows(proj_s, 0, gi)[None, :, None] for gi in range(N_GROUPS)]
    return (y_p[None], y_s[:, None],
            wkv_p, wkv_s,
            proj_p[SEQ - 1:, 0:D_SHIFT][None, None], proj_s[:, 0:D_SHIFT][None, :, None],
            conv_p, conv_s,
            win_p[0], win_s[0], win_p[1], win_s[1], win_p[2], win_s[2])
```

```python
import functools
import math

import numpy as np
import jax
import jax.numpy as jnp
from jax import lax
from jax.experimental import pallas as pl
from jax.experimental.pallas import tpu as pltpu

F32 = jnp.float32
BF16 = jnp.bfloat16
HIGHEST = lax.Precision.HIGHEST

D_MODEL = 2048
SEQ = 8192
DEC_BATCH = 32
HEAD_A = 64
H_A = 16
D_A = H_A * HEAD_A
D_DECAY_LORA = 96
D_AAA_LORA = 96
D_GATE_LORA = 64
D_LORA = D_DECAY_LORA + D_AAA_LORA + D_GATE_LORA
D_SHIFT = 3 * D_A + D_LORA
EPS_GN = 64e-5
HEAD_B = 64
H_G = 8
D_G = H_G * HEAD_B
WINDOWS = (128, 512, 2048)
DILATIONS = (1, 4, 16)
N_GROUPS = 3
D_B = N_GROUPS * D_G
BLK = 128
N_BUCKETS = 32
MAX_DISTANCE = 2048
COL_QKV = D_SHIFT
COL_GATE_SRC = D_SHIFT + 3 * D_B
D_IN = COL_GATE_SRC + 2 * D_MODEL
PROJ_TN = 1024
COL_GATE = -(-COL_GATE_SRC // PROJ_TN) * PROJ_TN
D_IN_PAD = COL_GATE + 2 * D_MODEL
D_FF = 5632
CONV_W = 3
EPS_RMS = 1e-6
CHUNK = 64
SCAN_CHUNKS = 4
F_RT, F_KT, F_QT, F_PT, F_V, F_BONUS, F_G = range(7)
N_FEATS = 7
LANES = 128
SUBLANES = 8
VMEM_LIMIT = 56 * 1024 * 1024
STEP_CACHE_WORDS = 2 * 1024 * 1024


def _params(sem, vmem=VMEM_LIMIT):
    return pltpu.CompilerParams(dimension_semantics=sem, vmem_limit_bytes=vmem)


def _sigmoid(x):
    return 1.0 / (1.0 + jnp.exp(-x))


def _dot_bf16(a, b):
    return jnp.dot(a, b, preferred_element_type=F32)


def _split2(x):
    hi = x.astype(BF16)
    return hi, (x - hi.astype(F32)).astype(BF16)


def _split3(x):
    hi = x.astype(BF16)
    r1 = x - hi.astype(F32)
    mid = r1.astype(BF16)
    lo = (r1 - mid.astype(F32)).astype(BF16)
    return hi, mid, lo


def _dot_exact_rhs(x, m):
    hi, mid, lo = _split3(x)
    return _dot_bf16(hi, m) + _dot_bf16(mid, m) + _dot_bf16(lo, m)


def _head_sums(x, hsum, hbc):
    return _dot_exact_rhs(_dot_exact_rhs(x, hsum), hbc)


def _dot_exact_lhs(m, x):
    hi, mid, lo = _split3(x)
    return _dot_bf16(m, hi) + _dot_bf16(m, mid) + _dot_bf16(m, lo)


def _rms(x, g):
    return x * lax.rsqrt(jnp.mean(x * x, axis=-1, keepdims=True) + EPS_RMS) * g


def _proj_cast_kernel(x_ref, g_ref, w_ref, b_ref, o_ref, wb_ref, xn_scr, *, gap_block):
    j = pl.program_id(0)

    @pl.when(j == 0)
    def _():
        xn_scr[...] = _rms(x_ref[...], g_ref[...]).astype(BF16)

    wb = jnp.where(j != gap_block, w_ref[...], 0.0).astype(BF16)
    wb_ref[...] = wb
    acc = _dot_bf16(xn_scr[...], wb)
    tn = o_ref.shape[1]
    col = lax.broadcasted_iota(jnp.int32, (1, tn), 1) + j * tn
    o_ref[...] = jnp.where(col >= COL_GATE, _sigmoid(acc + b_ref[...]), acc)


def _proj_cast(x, g, w, bias):
    m, k = x.shape
    tn = COL_GATE - COL_GATE_SRC
    assert tn % LANES == 0 and COL_GATE_SRC % tn == 0
    gap_block = COL_GATE_SRC // tn
    return pl.pallas_call(
        functools.partial(_proj_cast_kernel, gap_block=gap_block),
        out_shape=[jax.ShapeDtypeStruct((m, D_IN_PAD), F32), jax.ShapeDtypeStruct((k, D_IN_PAD), BF16)],
        grid=(D_IN_PAD // tn,),
        in_specs=[
            pl.BlockSpec((m, k), lambda j: (0, 0)),
            pl.BlockSpec((1, k), lambda j: (0, 0)),
            pl.BlockSpec((k, tn), lambda j: (0, jnp.where(j < gap_block, j, jnp.maximum(j - 1, 0)))),
            pl.BlockSpec((1, tn), lambda j: (0, j)),
        ],
        out_specs=[pl.BlockSpec((m, tn), lambda j: (0, j)), pl.BlockSpec((k, tn), lambda j: (0, j))],
        scratch_shapes=[pltpu.VMEM((m, k), BF16)],
        compiler_params=_params(("arbitrary",)),
        name="proj_sample_cast_w_in",
    )(x, g.reshape(1, k), w, bias)


def _proj_kernel(x_ref, g_ref, w_ref, b_ref, o_ref, xn_scr):
    @pl.when(pl.program_id(1) == 0)
    def _():
        xn_scr[...] = _rms(x_ref[...], g_ref[...]).astype(BF16)

    acc = _dot_bf16(xn_scr[...], w_ref[...])
    tn = o_ref.shape[1]
    col = lax.broadcasted_iota(jnp.int32, (1, tn), 1) + pl.program_id(1) * tn
    o_ref[...] = jnp.where(col >= COL_GATE, _sigmoid(acc + b_ref[...]), acc)


def _proj(x, g, w, bias, tm, name):
    m, k = x.shape
    n = w.shape[1]
    return pl.pallas_call(
        _proj_kernel,
        out_shape=jax.ShapeDtypeStruct((m, n), F32),
        grid=(m // tm, n // PROJ_TN),
        in_specs=[
            pl.BlockSpec((tm, k), lambda i, j: (i, 0)),
            pl.BlockSpec((1, k), lambda i, j: (0, 0)),
            pl.BlockSpec((k, PROJ_TN), lambda i, j: (0, j)),
            pl.BlockSpec((1, PROJ_TN), lambda i, j: (0, j)),
        ],
        out_specs=pl.BlockSpec((tm, PROJ_TN), lambda i, j: (i, j)),
        scratch_shapes=[pltpu.VMEM((tm, k), BF16)],
        compiler_params=_params(("parallel", "arbitrary")),
        name=name,
    )(x, g.reshape(1, k), w, bias)


def _matmul_kernel(a_ref, w_ref, o_ref):
    o_ref[...] = _dot_bf16(a_ref[...], w_ref[...].astype(BF16))


def _matmul(a, w, tm, tn, name):
    m, k = a.shape
    n = w.shape[1]
    return pl.pallas_call(
        _matmul_kernel,
        out_shape=jax.ShapeDtypeStruct((m, n), F32),
        grid=(m // tm, n // tn),
        in_specs=[pl.BlockSpec((tm, k), lambda i, j: (i, 0)), pl.BlockSpec((k, tn), lambda i, j: (0, j))],
        out_specs=pl.BlockSpec((tm, tn), lambda i, j: (i, j)),
        compiler_params=_params(("parallel", "arbitrary")),
        name=name,
    )(a, w)


def _prep_kernel(p_ref, prev_ref, mu_ref, wl_ref, w0_ref, a0_ref, kk_ref, ka_ref, rk_ref, lmat_ref, sel_ref,
                 hs_ref, hb_ref, f_ref, gc_ref, *, rows_are_time):
    p = p_ref[...]
    tm = p.shape[0]
    if rows_are_time:
        last = jnp.where(pl.program_id(0) == 0, 0.0, prev_ref[7:8, :])
        row = lax.broadcasted_iota(jnp.int32, (tm, 1), 0)
        prev = jnp.where(row == 0, last, pltpu.roll(p, 1, axis=0))
    else:
        prev = prev_ref[...]
    xm = p + mu_ref[...] * (prev - p)
    r = xm[:, 0:D_A]
    k = xm[:, D_A:2 * D_A]
    v = xm[:, 2 * D_A:3 * D_A]
    xl = xm[:, 3 * D_A:D_SHIFT]
    lane = lax.broadcasted_iota(jnp.int32, xl.shape, 1)
    act = jnp.where(lane < D_DECAY_LORA, jnp.tanh(xl),
                    jnp.where(lane < D_DECAY_LORA + D_AAA_LORA, xl, _sigmoid(xl)))
    act_hi, act_lo = _split2(act)
    w_hi, w_lo = wl_ref[0], wl_ref[1]
    lora = _dot_bf16(act_hi, w_hi) + _dot_bf16(act_hi, w_lo) + _dot_bf16(act_lo, w_hi)
    y = -(w0_ref[...] + lora[:, 0:D_A])
    softplus = jnp.maximum(y, 0.0) + jnp.log(1.0 + jnp.exp(-jnp.abs(y)))
    logw = -jnp.exp(-softplus - 0.5)
    a = _sigmoid(a0_ref[...] + lora[:, D_A:2 * D_A])
    g = lora[:, 2 * D_A:3 * D_A]
    kkr = k * kk_ref[...]
    kp = k * (1.0 + (a - 1.0) * ka_ref[...])
    seg = _head_sums(jnp.concatenate([kkr * kkr, r * kp * rk_ref[...]], axis=0), hs_ref[...], hb_ref[...])
    kk = kkr / jnp.maximum(jnp.sqrt(seg[0:tm]), 1e-12)
    cum = _dot_exact_lhs(lmat_ref[...], logw)
    e_out = jnp.exp(-cum)
    feats = {F_RT: r * jnp.exp(cum), F_KT: kp * e_out, F_QT: kk * a * e_out, F_PT: -kk * jnp.exp(cum - logw),
             F_V: v, F_BONUS: seg[tm:2 * tm] * v, F_G: g}
    for n, val in feats.items():
        f_ref[:, n * D_A:(n + 1) * D_A] = val
    gc_ref[...] = jnp.exp(_dot_exact_lhs(sel_ref[...], cum))


def _rwkv_prep(proj, prev, tm, chunk, lw, name):
    m = proj.shape[0]
    rows_are_time = prev is None
    n_tiles = m // tm
    t = np.arange(tm)
    lmat = ((t[:, None] // chunk == t[None, :] // chunk) & (t[None, :] <= t[:, None])).astype(np.float32)
    sel_stride = 1 if chunk == 1 else 8
    n_sel = sel_stride * (tm // chunk)
    sel = np.zeros((n_sel, tm), np.float32)
    for c in range(tm // chunk):
        sel[sel_stride * c, (c + 1) * chunk - 1] = 1.0
    if rows_are_time:
        prev_arr = proj
        prev_spec = pl.BlockSpec((8, D_SHIFT), lambda i: (jnp.maximum(i * (tm // 8) - 1, 0), 0))
    else:
        prev_arr = prev
        prev_spec = pl.BlockSpec((tm, D_SHIFT), lambda i: (i, 0))
    vec = lambda d: pl.BlockSpec((1, d), lambda i: (0, 0))
    full = lambda a: pl.BlockSpec(a.shape, lambda i: (0, 0))
    lmat_b = jnp.asarray(lmat, BF16)
    sel_b = jnp.asarray(sel, BF16)
    return pl.pallas_call(
        functools.partial(_prep_kernel, rows_are_time=rows_are_time),
        out_shape=[jax.ShapeDtypeStruct((m, N_FEATS * D_A), F32),
                   jax.ShapeDtypeStruct((n_tiles * n_sel, D_A), F32)],
        grid=(n_tiles,),
        in_specs=[pl.BlockSpec((tm, D_SHIFT), lambda i: (i, 0)), prev_spec, vec(D_SHIFT),
                  pl.BlockSpec(lw["w_lora"].shape, lambda i: (0, 0, 0)),
                  vec(D_A), vec(D_A), vec(D_A), vec(D_A), vec(D_A), full(lmat_b), full(sel_b), full(lw["hsum"]),
                  full(lw["hbc"])],
        out_specs=[pl.BlockSpec((tm, N_FEATS * D_A), lambda i: (i, 0)), pl.BlockSpec((n_sel, D_A), lambda i: (i, 0))],
        compiler_params=_params(("parallel",)),
        name=name,
    )(proj, prev_arr, lw["mu"], lw["w_lora"], lw["w0"], lw["a0"], lw["k_k"], lw["k_a"], lw["r_k"], lmat_b, sel_b,
      lw["hsum"], lw["hbc"])


_NN = (((2,), (1,)), ((0,), (0,)))
_NT = (((2,), (2,)), ((0,), (0,)))
PAIRS = H_A // 2
LOG2_HEAD = HEAD_A.bit_length() - 1
assert HEAD_A == 1 << LOG2_HEAD and 2 * HEAD_A == LANES


def _lane_head(n):
    return (lax.broadcasted_iota(jnp.int32, (1, 1, n), 2) >> LOG2_HEAD) & 1


def _keep_head(x, head, which):
    return jnp.where(head == which, x, jnp.zeros_like(x))


def _pdot_nn(a, b):
    ah, al = _split2(a)
    bh, bl = _split2(b)
    head = _lane_head(b.shape[2])
    bd = lambda x: jnp.concatenate([_keep_head(x, head, 0), _keep_head(x, head, 1)], axis=1)
    a_cat = jnp.concatenate([ah, ah, al], axis=2)
    b_cat = jnp.concatenate([bd(bh), bd(bl), bd(bh)], axis=1)
    return lax.dot_general(a_cat, b_cat, _NN, preferred_element_type=F32)


def _pdot_nt(a, bs):
    ah, al = _split2(a)
    head = _lane_head(LANES)
    hi, lo = [], []
    for b in bs:
        bh, bl = _split2(b)
        for which in (0, 1):
            hi.append(_keep_head(bh, head, which))
            lo.append(_keep_head(bl, head, which))
    bh, bl = jnp.concatenate(hi, axis=1), jnp.concatenate(lo, axis=1)
    a_cat = jnp.concatenate([ah, ah, al], axis=2)
    b_cat = jnp.concatenate([bh, bl, bh], axis=2)
    return lax.dot_general(a_cat, b_cat, _NT, preferred_element_type=F32)


def _pdot_tn(x, y):
    nx = x.shape[2]
    xh, xl = _split2(jnp.swapaxes(x, 1, 2))
    yh, yl = _split2(y)
    x_cat = jnp.concatenate([xh, xh, xl, jnp.zeros_like(xl)], axis=2)
    y_cat = jnp.concatenate([yh, yl, yh, jnp.zeros_like(yl)], axis=1)
    tot = lax.dot_general(x_cat, y_cat, _NN, preferred_element_type=F32)
    head = _lane_head(LANES)
    return [jnp.where(head == 0, tot[:, g:g + HEAD_A], tot[:, g + HEAD_A:g + LANES]) for g in range(0, nx, LANES)]


def _scan_kernel(f_ref, gc_ref, gng_ref, gnb_ref, ya_ref, sfin_ref, s_scr):
    @pl.when(pl.program_id(0) == 0)
    def _():
        s_scr[...] = jnp.zeros_like(s_scr)

    def pairs(x, rows):
        return jnp.stack([x[c * rows:(c + 1) * rows, i * LANES:(i + 1) * LANES]
                          for c in range(SCAN_CHUNKS) for i in range(PAIRS)], axis=0)

    feat = lambda n: f_ref[:, n * D_A:(n + 1) * D_A]
    p, r, q, k, v = (pairs(feat(n), CHUNK) for n in (F_PT, F_RT, F_QT, F_KT, F_V))
    gam = pairs(gc_ref[...], SUBLANES)[:, 0:1]
    ri = lax.broadcasted_iota(jnp.int32, (1, CHUNK, LANES), 1)
    ci = lax.broadcasted_iota(jnp.int32, (1, CHUNK, LANES), 2) & (HEAD_A - 1)
    strict = ri > ci
    incl = ri >= ci
    eye = (ri == ci).astype(F32)

    gram = _pdot_nt(jnp.concatenate([p, r], axis=1), [q, k])
    a_qp = jnp.where(strict, gram[:, 0:CHUNK, 0:LANES], 0.0)
    a_kp = jnp.where(strict, gram[:, 0:CHUNK, LANES:], 0.0)
    a_rq = jnp.where(incl, gram[:, CHUNK:, 0:LANES], 0.0)
    a_rk = jnp.where(incl, gram[:, CHUNK:, LANES:], 0.0)
    same = lambda log2_bs: (ri >> log2_bs) == (ci >> log2_bs)
    a_d = jnp.where(same(4), a_qp, 0.0)
    tinv = eye + a_d
    pw = _pdot_nn(a_d, a_d)
    for _ in range(2):
        both = _pdot_nn(jnp.concatenate([pw, tinv], axis=1), pw)
        pw = both[:, 0:CHUNK]
        tinv = tinv + both[:, CHUNK:]
    tinv = tinv + _pdot_nn(tinv, pw)
    for log2_bs in (4, 5):
        off = jnp.where(jnp.logical_and(same(log2_bs + 1), jnp.logical_not(same(log2_bs))), a_qp, 0.0)
        tinv = tinv + _pdot_nn(tinv, _pdot_nn(off, tinv))
    av = _pdot_nn(jnp.concatenate([a_kp, a_rk], axis=1), v)
    pw_hat = _pdot_nn(tinv, jnp.concatenate([p, av[:, 0:CHUNK]], axis=2))
    x = _pdot_nn(a_rq, pw_hat)
    r_hat = r + x[:, :, 0:LANES]
    o_loc = x[:, :, LANES:] + av[:, CHUNK:]
    y_p, y_w = _pdot_tn(pw_hat, q)
    (vk,) = _pdot_tn(v, k)
    m_mat = (eye + y_p) * gam
    n_mat = (y_w + vk) * gam

    state = s_scr[...]
    outs = []
    for c in range(SCAN_CHUNKS):
        sl = slice(c * PAIRS, (c + 1) * PAIRS)
        outs.append(_pdot_nt(r_hat[sl], [state]) + o_loc[sl])
        state = _pdot_nn(state, m_mat[sl]) + n_mat[sl]
    s_scr[...] = state
    o = jnp.concatenate(outs, axis=0)

    head = _lane_head(LANES)

    def head_mean(z):
        sums = [jnp.sum(_keep_head(z, head, which), axis=2, keepdims=True) for which in (0, 1)]
        return jnp.where(head == 0, sums[0], sums[1]) * (1.0 / HEAD_A)

    d = o - head_mean(o)
    o_n = d * lax.rsqrt(head_mean(d * d) + EPS_GN)
    o_n = jnp.concatenate([jnp.concatenate([o_n[c * PAIRS + i] for i in range(PAIRS)], axis=1)
                           for c in range(SCAN_CHUNKS)], axis=0)
    ya_ref[...] = ((o_n * gng_ref[...] + gnb_ref[...] + feat(F_BONUS)) * feat(F_G)).astype(ya_ref.dtype)

    @pl.when(pl.program_id(0) == pl.num_programs(0) - 1)
    def _():
        sfin_ref[...] = s_scr[...]


def _wkv_scan(feats, gc, lw):
    t = feats.shape[0]
    vec = pl.BlockSpec((1, D_A), lambda c: (0, 0))
    pair_state = (PAIRS, HEAD_A, LANES)
    rows = SCAN_CHUNKS * CHUNK
    ya, s = pl.pallas_call(
        _scan_kernel,
        out_shape=[jax.ShapeDtypeStruct((t, D_A), BF16), jax.ShapeDtypeStruct(pair_state, F32)],
        grid=(t // rows,),
        in_specs=[pl.BlockSpec((rows, N_FEATS * D_A), lambda c: (c, 0)),
                  pl.BlockSpec((SUBLANES * SCAN_CHUNKS, D_A), lambda c: (c, 0)), vec, vec],
        out_specs=[pl.BlockSpec((rows, D_A), lambda c: (c, 0)), pl.BlockSpec(pair_state, lambda c: (0, 0, 0))],
        scratch_shapes=[pltpu.VMEM(pair_state, F32)],
        compiler_params=_params(("arbitrary",)),
        name="wkv_scan_prompt",
    )(feats, gc, lw["gn_g"], lw["gn_b"])
    s = s.reshape(PAIRS, HEAD_A, 2, HEAD_A).transpose(0, 2, 1, 3).reshape(H_A, HEAD_A, HEAD_A)
    return ya, s


def _wkv_step_kernel(s_ref, rows_ref, sn_ref, o_ref):
    ri = lax.broadcasted_iota(jnp.int32, (1, 1, HEAD_A, HEAD_A), 2)
    ci = lax.broadcasted_iota(jnp.int32, (1, 1, HEAD_A, HEAD_A), 3)
    eye = ri == ci
    pt, rt, qt, kt, v, gc = (rows_ref[:, n] for n in range(6))
    s = s_ref[0]
    u = jnp.sum(s * pt, axis=3, keepdims=True)
    v_col = jnp.sum(jnp.where(eye, v, 0.0), axis=3, keepdims=True)
    m = s + u * qt + v_col * kt
    sn_ref[0] = m * gc
    o_col = jnp.sum(m * rt, axis=3, keepdims=True)
    o_ref[...] = jnp.sum(jnp.where(eye, o_col, 0.0), axis=2, keepdims=True)


def _wkv_step(state, pt, rt, qt, kt, v, gc, bt):
    b = pt.shape[0]
    rows = jnp.stack([pt, rt, qt, kt, v, gc], axis=1).reshape(b, 6, H_A, 1, HEAD_A)
    st_spec = pl.BlockSpec((1, bt, H_A, HEAD_A, HEAD_A), lambda i: (0, i, 0, 0, 0))
    sn, o = pl.pallas_call(
        _wkv_step_kernel,
        out_shape=[jax.ShapeDtypeStruct(state.shape, F32), jax.ShapeDtypeStruct((b, H_A, 1, HEAD_A), F32)],
        grid=(b // bt,),
        in_specs=[st_spec, pl.BlockSpec((bt, 6, H_A, 1, HEAD_A), lambda i: (i, 0, 0, 0, 0))],
        out_specs=[st_spec, pl.BlockSpec((bt, H_A, 1, HEAD_A), lambda i: (i, 0, 0, 0))],
        compiler_params=_params(("parallel",)),
        name="wkv_step_sample",
    )(state, rows)
    return sn, o.reshape(b, D_A)


def _post_kernel(o_ref, bonus_ref, g_ref, gng_ref, gnb_ref, hs_ref, hb_ref, y_ref):
    o = o_ref[...]
    hsum, hbc = hs_ref[...], hb_ref[...]
    mu = _head_sums(o, hsum, hbc) * (1.0 / HEAD_A)
    d = o - mu
    var = _head_sums(d * d, hsum, hbc) * (1.0 / HEAD_A)
    o_n = d * lax.rsqrt(var + EPS_GN) * gng_ref[...] + gnb_ref[...]
    y_ref[...] = ((o_n + bonus_ref[...]) * g_ref[...]).astype(y_ref.dtype)


def _rwkv_post(o, bonus, g, lw, tm, name):
    m = o.shape[0]
    big = pl.BlockSpec((tm, D_A), lambda i: (i, 0))
    vec = pl.BlockSpec((1, D_A), lambda i: (0, 0))
    return pl.pallas_call(
        _post_kernel,
        out_shape=jax.ShapeDtypeStruct((m, D_A), BF16),
        grid=(m // tm,),
        in_specs=[big, big, big, vec, vec, pl.BlockSpec((D_A, LANES), lambda i: (0, 0)),
                  pl.BlockSpec((LANES, D_A), lambda i: (0, 0))],
        out_specs=big,
        compiler_params=_params(("parallel",)),
        name=name,
    )(o, bonus, g, lw["gn_g"], lw["gn_b"], lw["hsum"], lw["hbc"])


def _attn_kernel(q_ref, kc_ref, vc_ref, kh_ref, vh_ref, bias_ref, o_ref, l_ref, *, dil, m_blocks, tiled):
    scale = HEAD_B ** -0.5
    n_streams = SUBLANES if tiled else dil
    n_units = n_streams * m_blocks

    stride = SUBLANES if tiled else dil
    if tiled:
        flat = lambda ref: ref.reshape(ref.shape[0] * SUBLANES, LANES)
        q_ref, kc_ref, vc_ref, kh_ref, vh_ref, o_ref, l_ref = map(flat, (q_ref, kc_ref, vc_ref, kh_ref, vh_ref,
                                                                          o_ref, l_ref))

    def rows(ref, r, start, size):
        if stride == 1:
            return ref[pl.ds(start, size), :]
        return ref[pl.ds(r + stride * start, size, stride=stride), :]

    def put(ref, r, start, val):
        if stride == 1:
            ref[pl.ds(start, BLK), :] = val
        else:
            ref[pl.ds(r + stride * start, BLK, stride=stride), :] = val

    qs, ks, vs, units = [], [], [], []
    for r in range(n_streams):
        for mb in range(m_blocks):
            units.append((r, mb))
            qs.append(rows(q_ref, r, BLK * mb, BLK))
            if mb == 0:
                ks.append(jnp.concatenate([rows(kh_ref, r, 0, BLK), rows(kc_ref, r, 0, BLK)], axis=0))
                vs.append(jnp.concatenate([rows(vh_ref, r, 0, BLK), rows(vc_ref, r, 0, BLK)], axis=0))
            else:
                ks.append(rows(kc_ref, r, BLK * (mb - 1), 2 * BLK))
                vs.append(rows(vc_ref, r, BLK * (mb - 1), 2 * BLK))

    def batch(xs, mul=None):
        pre = (lambda x: x) if mul is None else (lambda x: x * mul)
        return jnp.stack([pre(x[:, sub * HEAD_B:(sub + 1) * HEAD_B]).astype(BF16) for sub in range(2) for x in xs],
                         axis=0)

    assert math.frexp(scale)[0] == 0.5
    qb, kb, vb = batch(qs, scale), batch(ks), batch(vs)
    s = lax.dot_general(qb, kb, (((2,), (2,)), ((0,), (0,))), preferred_element_type=F32)
    s = s.reshape(2, n_units, BLK, 2 * BLK) + bias_ref[...][:, None]
    unit = lax.broadcasted_iota(jnp.int32, (1, n_units, 1, 2 * BLK), 1)
    col = lax.broadcasted_iota(jnp.int32, (1, n_units, 1, 2 * BLK), 3)
    no_prev = jnp.logical_and(jnp.logical_and(pl.program_id(0) == 0, (unit & (m_blocks - 1)) == 0), col < BLK)
    s = jnp.where(no_prev, -jnp.inf, s)
    m = jnp.max(s, axis=-1, keepdims=True)
    p = jnp.exp(s - m)
    l = jnp.sum(p, axis=-1, keepdims=True)
    pv = lax.dot_general(p.astype(BF16).reshape(2 * n_units, BLK, 2 * BLK), vb, (((2,), (1,)), ((0,), (0,))),
                         preferred_element_type=F32)
    o = pv.reshape(2, n_units, BLK, HEAD_B) / l
    lse = jnp.broadcast_to(m + jnp.log(l), (2, n_units, BLK, HEAD_B))
    for u, (r, mb) in enumerate(units):
        put(o_ref, r, BLK * mb, jnp.concatenate([o[0, u], o[1, u]], axis=1))
        put(l_ref, r, BLK * mb, jnp.concatenate([lse[0, u], lse[1, u]], axis=1))


def _attn_prompt(proj, bias, gi, rows_per_step):
    t, n_cols = proj.shape
    dil = DILATIONS[gi]
    span = BLK * dil
    tiled = dil % SUBLANES == 0
    if tiled:
        rows_per_step *= dil // SUBLANES
    m_blocks = rows_per_step // span
    cq = (COL_QKV + gi * D_G) // LANES
    ck = (COL_QKV + D_B + gi * D_G) // LANES
    cv = (COL_QKV + 2 * D_B + gi * D_G) // LANES
    bias_block = (2, BLK, 2 * BLK)
    if tiled:
        n_hi = dil // SUBLANES
        src = proj.reshape(t // dil, n_hi, SUBLANES, n_cols)
        grid = (t // rows_per_step, n_hi, H_G // 2)
        l_rows = rows_per_step // dil
        cur = lambda c0: pl.BlockSpec((l_rows, None, SUBLANES, LANES), lambda i, rh, hp: (i, rh, 0, c0 + hp))
        halo = lambda c0: pl.BlockSpec((BLK, None, SUBLANES, LANES),
                                       lambda i, rh, hp: (jnp.maximum(i * m_blocks - 1, 0), rh, 0, c0 + hp))
        out_spec = pl.BlockSpec((l_rows, None, SUBLANES, LANES), lambda i, rh, hp: (i, rh, 0, hp))
        out = jax.ShapeDtypeStruct((t // dil, n_hi, SUBLANES, D_G), F32)
        bias_spec = pl.BlockSpec(bias_block, lambda i, rh, hp: (hp, 0, 0))
    else:
        src = proj
        grid = (t // rows_per_step, H_G // 2)
        cur = lambda c0: pl.BlockSpec((rows_per_step, LANES), lambda i, hp: (i, c0 + hp))
        halo = lambda c0: pl.BlockSpec((span, LANES), lambda i, hp: (jnp.maximum(i * m_blocks - 1, 0), c0 + hp))
        out_spec = pl.BlockSpec((rows_per_step, LANES), lambda i, hp: (i, hp))
        out = jax.ShapeDtypeStruct((t, D_G), F32)
        bias_spec = pl.BlockSpec(bias_block, lambda i, hp: (hp, 0, 0))
    o, l = pl.pallas_call(
        functools.partial(_attn_kernel, dil=dil, m_blocks=m_blocks, tiled=tiled),
        out_shape=[out, out],
        grid=grid,
        in_specs=[cur(cq), cur(ck), cur(cv), halo(ck), halo(cv), bias_spec],
        out_specs=[out_spec, out_spec],
        compiler_params=_params(("parallel",) * len(grid)),
        name=f"attn_prompt_g{gi}",
    )(src, src, src, src, src, bias)
    return o.reshape(t, D_G), l.reshape(t, D_G)


def _attn_step_kernel(q_ref, kn_ref, vn_ref, c_ref, bias_ref, bias0_ref, o_ref, l_ref):
    scale = HEAD_B ** -0.5
    ri = lax.broadcasted_iota(jnp.int32, (1, 1, HEAD_B, HEAD_B), 2)
    ci = lax.broadcasted_iota(jnp.int32, (1, 1, HEAD_B, HEAD_B), 3)
    eye = ri == ci
    q = q_ref[...]
    q_col = jnp.sum(jnp.where(eye, q, 0.0), axis=3, keepdims=True)
    s = jnp.sum(c_ref[:, 0] * q_col, axis=2, keepdims=True) * scale + bias_ref[...]
    s_new = jnp.sum(q * kn_ref[...], axis=3, keepdims=True) * scale + bias0_ref[...]
    m = jnp.maximum(jnp.max(s, axis=3, keepdims=True), s_new)
    p = jnp.exp(s - m)
    p_new = jnp.exp(s_new - m)
    l = jnp.sum(p, axis=3, keepdims=True) + p_new
    o_col = jnp.sum(c_ref[:, 1] * p, axis=3, keepdims=True)
    o_row = jnp.sum(jnp.where(eye, o_col, 0.0), axis=2, keepdims=True)
    o_ref[...] = (o_row + p_new * vn_ref[...]) / l
    l_ref[...] = jnp.broadcast_to(m + jnp.log(l), o_ref.shape)


def _attn_step(q, k_new, v_new, cache, bias, bias0, gi):
    b = q.shape[0]
    w = cache.shape[2]
    bt = max(1, min(b, STEP_CACHE_WORDS // (2 * D_G * w)))
    cache_t = jnp.transpose(cache, (0, 1, 3, 4, 5, 2))
    vec = pl.BlockSpec((bt, H_G, 1, HEAD_B), lambda i: (i, 0, 0, 0))
    out = jax.ShapeDtypeStruct((b, H_G, 1, HEAD_B), F32)
    o, l = pl.pallas_call(
        _attn_step_kernel,
        out_shape=[out, out],
        grid=(b // bt,),
        in_specs=[vec, vec, vec,
                  pl.BlockSpec((None, bt, 2, H_G, HEAD_B, w), lambda i: (0, i, 0, 0, 0, 0)),
                  pl.BlockSpec((H_G, 1, w), lambda i: (0, 0, 0)),
                  pl.BlockSpec((H_G, 1, 1), lambda i: (0, 0, 0))],
        out_specs=[vec, vec],
        compiler_params=_params(("parallel",)),
        name=f"attn_step_g{gi}",
    )(q, k_new, v_new, cache_t, bias, bias0)
    return o.reshape(b, D_G), l.reshape(b, D_G)


def _mix_kernel(ya_ref, o1_ref, o2_ref, o3_ref, l1_ref, l2_ref, l3_ref, wa_ref, wb_ref, ga_ref, gb_ref, o_ref):
    l1, l2, l3 = l1_ref[...], l2_ref[...], l3_ref[...]
    m = jnp.maximum(jnp.maximum(l1, l2), l3)
    e1, e2, e3 = jnp.exp(l1 - m), jnp.exp(l2 - m), jnp.exp(l3 - m)
    den = e1 + e2 + e3
    yb = ((e1 / den) * o1_ref[...] + (e2 / den) * o2_ref[...] + (e3 / den) * o3_ref[...]).astype(BF16)
    mixed = ga_ref[...] * _dot_bf16(ya_ref[...], wa_ref[...]) + gb_ref[...] * _dot_bf16(yb, wb_ref[...])
    o_ref[...] = mixed.astype(o_ref.dtype)


def _mix(ya, outs, lses, wa, wb, proj, tm, name):
    m = ya.shape[0]
    n = D_MODEL
    assert COL_GATE % n == 0
    ga0 = COL_GATE // n
    grp = pl.BlockSpec((tm, D_G), lambda i: (i, 0))
    resident = lambda rows: pl.BlockSpec((rows, n), lambda i: (0, 0), pipeline_mode=pl.Buffered(1))
    return pl.pallas_call(
        _mix_kernel,
        out_shape=jax.ShapeDtypeStruct((m, n), BF16),
        grid=(m // tm,),
        in_specs=[pl.BlockSpec((tm, D_A), lambda i: (i, 0))] + [grp] * 6 + [
            resident(D_A), resident(D_G),
            pl.BlockSpec((tm, n), lambda i: (i, ga0)),
            pl.BlockSpec((tm, n), lambda i: (i, ga0 + 1)),
        ],
        out_specs=pl.BlockSpec((tm, n), lambda i: (i, 0)),
        compiler_params=_params(("parallel",)),
        name=name,
    )(ya, *outs, *lses, wa, wb, proj, proj)


def _wo_kernel(a_ref, w_ref, x_ref, g_ref, h_ref, hn_ref):
    h = x_ref[...] + _dot_bf16(a_ref[...], w_ref[...])
    h_ref[...] = h
    hn_ref[...] = _rms(h, g_ref[...]).astype(BF16)


def _wo(a, w, x, g, tm, name):
    m, k = a.shape
    n = w.shape[1]
    row = pl.BlockSpec((tm, n), lambda i: (i, 0))
    return pl.pallas_call(
        _wo_kernel,
        out_shape=[jax.ShapeDtypeStruct((m, n), F32), jax.ShapeDtypeStruct((m, n), BF16)],
        grid=(m // tm,),
        in_specs=[pl.BlockSpec((tm, k), lambda i: (i, 0)), pl.BlockSpec((k, n), lambda i: (0, 0)), row,
                  pl.BlockSpec((1, n), lambda i: (0, 0))],
        out_specs=[row, row],
        compiler_params=_params(("parallel",)),
        name=name,
    )(a, w, x, g.reshape(1, n))


def _gelu(x):
    return 0.5 * x * (1.0 + lax.erf(x * (1.0 / math.sqrt(2.0))))


def _ffn_up_kernel(h_ref, w1_ref, w2_ref, cw1_ref, cw2_ref, cb1_ref, cb2_ref, act_ref, t1_ref, t2_ref,
                   c1_scr, c2_scr, w1_scr, w2_scr):
    tm = h_ref.shape[0]

    @pl.when(pl.program_id(1) == 0)
    def _():
        c1_scr[...] = jnp.zeros_like(c1_scr)
        c2_scr[...] = jnp.zeros_like(c2_scr)
        w1_scr[...] = w1_ref[...].astype(BF16)
        w2_scr[...] = w2_ref[...].astype(BF16)

    a = h_ref[...]
    row = lax.broadcasted_iota(jnp.int32, (tm, 1), 0)

    def conv(u, carry_ref, cw_ref, cb_ref):
        m1 = jnp.where(row == 0, carry_ref[7:8, :], pltpu.roll(u, 1, axis=0))
        m2 = jnp.where(row == 0, carry_ref[6:7, :], jnp.where(row == 1, carry_ref[7:8, :], pltpu.roll(u, 2, axis=0)))
        return cb_ref[...] + cw_ref[0:1, :] * m2 + cw_ref[1:2, :] * m1 + cw_ref[2:3, :] * u

    u1 = _dot_bf16(a, w1_scr[...])
    u2 = _dot_bf16(a, w2_scr[...])
    c1 = conv(u1, c1_scr, cw1_ref, cb1_ref)
    c2 = conv(u2, c2_scr, cw2_ref, cb2_ref)
    act_ref[...] = (_gelu(c1) * c2).astype(act_ref.dtype)
    c1_scr[...] = u1[tm - 8:tm]
    c2_scr[...] = u2[tm - 8:tm]
    t1_ref[...] = u1[tm - 8:tm]
    t2_ref[...] = u2[tm - 8:tm]


def _ffn_up_prompt(hn, w_up, conv_w, conv_b, tm, tn):
    m = hn.shape[0]
    nj = D_FF // tn
    tail = jax.ShapeDtypeStruct((8, D_FF), F32)
    return pl.pallas_call(
        _ffn_up_kernel,
        out_shape=[jax.ShapeDtypeStruct((m, D_FF), BF16), tail, tail],
        grid=(nj, m // tm),
        in_specs=[
            pl.BlockSpec((tm, D_MODEL), lambda j, i: (i, 0)),
            pl.BlockSpec((D_MODEL, tn), lambda j, i: (0, j)),
            pl.BlockSpec((D_MODEL, tn), lambda j, i: (0, nj + j)),
            pl.BlockSpec((CONV_W, tn), lambda j, i: (0, j)),
            pl.BlockSpec((CONV_W, tn), lambda j, i: (0, nj + j)),
            pl.BlockSpec((1, tn), lambda j, i: (0, j)),
            pl.BlockSpec((1, tn), lambda j, i: (0, nj + j)),
        ],
        out_specs=[pl.BlockSpec((tm, tn), lambda j, i: (i, j)),
                   pl.BlockSpec((8, tn), lambda j, i: (0, j)),
                   pl.BlockSpec((8, tn), lambda j, i: (0, j))],
        scratch_shapes=[pltpu.VMEM((8, tn), F32), pltpu.VMEM((8, tn), F32),
                        pltpu.VMEM((D_MODEL, tn), BF16), pltpu.VMEM((D_MODEL, tn), BF16)],
        compiler_params=_params(("arbitrary", "arbitrary")),
        name="ffn_up_prompt",
    )(hn, w_up, w_up, conv_w, conv_w, conv_b, conv_b)


def _ffn_act_step_kernel(up_ref, prev_ref, cw_ref, cb_ref, act_ref):
    up = up_ref[...]
    w = 2 * D_FF
    c = cb_ref[...] + cw_ref[0:1, :] * prev_ref[:, 0:w] + cw_ref[1:2, :] * prev_ref[:, w:2 * w] + cw_ref[2:3, :] * up
    act_ref[...] = (_gelu(c[:, 0:D_FF]) * c[:, D_FF:w]).astype(act_ref.dtype)


def _ffn_act_step(up, conv_prev, conv_w, conv_b):
    b = up.shape[0]
    return pl.pallas_call(
        _ffn_act_step_kernel,
        out_shape=jax.ShapeDtypeStruct((b, D_FF), BF16),
        compiler_params=pltpu.CompilerParams(vmem_limit_bytes=VMEM_LIMIT),
        name="ffn_act_sample",
    )(up, conv_prev.reshape(b, (CONV_W - 1) * 2 * D_FF), conv_w, conv_b)


def _down_kernel(a_ref, w_ref, h_ref, g_ref, o_ref):
    o_ref[...] = _rms(h_ref[...] + _dot_bf16(a_ref[...], w_ref[...]), g_ref[...])


def _ffn_down(act, w_down, h, g, tm, name):
    m = act.shape[0]
    return pl.pallas_call(
        _down_kernel,
        out_shape=jax.ShapeDtypeStruct((m, D_MODEL), F32),
        grid=(m // tm,),
        in_specs=[
            pl.BlockSpec((tm, D_FF), lambda i: (i, 0)),
            pl.BlockSpec((D_FF, D_MODEL), lambda i: (0, 0), pipeline_mode=pl.Buffered(1)),
            pl.BlockSpec((tm, D_MODEL), lambda i: (i, 0)),
            pl.BlockSpec((1, D_MODEL), lambda i: (0, 0)),
        ],
        out_specs=pl.BlockSpec((tm, D_MODEL), lambda i: (i, 0)),
        compiler_params=_params(("parallel",)),
        name=name,
    )(act, w_down, h, g.reshape(1, D_MODEL))


def _rel_bucket(dist):
    max_exact = N_BUCKETS // 2
    d_f = np.maximum(dist, 1).astype(np.float64)
    scaled = np.log(d_f / max_exact) / math.log(MAX_DISTANCE / max_exact) * (N_BUCKETS - max_exact)
    large = np.minimum(max_exact + scaled.astype(np.int64), N_BUCKETS - 1)
    return np.where(dist < max_exact, dist, large).astype(np.int32)


def _bias_rows(tab, dist):
    onehot = (jnp.asarray(_rel_bucket(dist))[None, :] == jnp.arange(N_BUCKETS)[:, None]).astype(F32)
    return jnp.dot(tab.T, onehot, precision=HIGHEST)


def _bias_tables(rel_bias, gi):
    dil = DILATIONS[gi]
    win = WINDOWS[gi]
    reach = win // dil
    tab = rel_bias[:, gi * H_G:(gi + 1) * H_G]
    qi = np.arange(BLK)[:, None]
    ki = np.arange(2 * BLK)[None, :]
    rel = qi + BLK - ki
    blk = _bias_rows(tab, (dil * np.maximum(rel, 0)).reshape(-1)).reshape(H_G, BLK, 2 * BLK)
    blk = jnp.where(((rel >= 0) & (rel <= reach))[None], blk, -jnp.inf)
    back = win - np.arange(win)
    step = jnp.where((back % dil == 0)[None, :], _bias_rows(tab, back), -jnp.inf)[:, None, :]
    step0 = _bias_rows(tab, np.zeros((1,), np.int64))[:, None, :]
    return blk, step, step0


def _tiles(m, prompt):
    if prompt:
        return dict(tag="prompt", proj_m=1024, prep_m=2 * CHUNK, attn_rows=2048, mix_m=512, wo_m=512, up_m=1024,
                    up_n=512, down_m=256)
    return dict(tag="sample", proj_m=m, prep_m=m, step_b=4, post_m=m, mix_m=m, wo_m=m, up_n=512, down_m=m)


def _layer(x, lw, bias_blk, prompt, state=None, proj=None):
    m = x.shape[0]
    tiles = _tiles(m, prompt)
    if proj is None:
        proj = _proj(x, lw["norm1_g"], lw["w_in"], lw["in_bias"], tiles["proj_m"], "proj_" + tiles["tag"])

    if prompt:
        feats, gc = _rwkv_prep(proj, None, tiles["prep_m"], CHUNK, lw, "rwkv_prep_prompt")
        ya, wkv_new = _wkv_scan(feats, gc, lw)
        wkv_new = wkv_new[None, None]
    else:
        feats, gc = _rwkv_prep(proj, state["shift"], tiles["prep_m"], 1, lw, "rwkv_prep_sample")
        f = lambda n: feats[:, n * D_A:(n + 1) * D_A]
        wkv_new, o = _wkv_step(state["wkv"], f(F_PT), f(F_RT), f(F_QT), f(F_KT), f(F_V), gc, tiles["step_b"])
        ya = _rwkv_post(o, f(F_BONUS), f(F_G), lw, tiles["post_m"], "rwkv_post_" + tiles["tag"])

    outs, lses = [], []
    for gi in range(N_GROUPS):
        if prompt:
            o_g, l_g = _attn_prompt(proj, bias_blk[gi][0], gi, tiles["attn_rows"])
        else:
            sl = lambda c0: proj[:, c0 + gi * D_G:c0 + (gi + 1) * D_G].reshape(m, H_G, 1, HEAD_B)
            o_g, l_g = _attn_step(sl(COL_QKV), sl(COL_QKV + D_B), sl(COL_QKV + 2 * D_B), state["win"][gi],
                                  bias_blk[gi][1], bias_blk[gi][2], gi)
        outs.append(o_g)
        lses.append(l_g)
    mixed = _mix(ya, outs, lses, lw["w_out_a"], lw["w_out_b"], proj, tiles["mix_m"], "mix_" + tiles["tag"])
    h, hn = _wo(mixed, lw["w_o"], x, lw["norm2_g"], tiles["wo_m"], "wo_" + tiles["tag"])

    if prompt:
        act, t1, t2 = _ffn_up_prompt(hn, lw["w_up"], lw["conv_w"], lw["conv_b"], tiles["up_m"], tiles["up_n"])
        conv_new = jnp.concatenate([t1[6:8], t2[6:8]], axis=1)[None, None]
    else:
        up = _matmul(hn, lw["w_up"], m, tiles["up_n"], "up_sample")
        act = _ffn_act_step(up, state["conv"], lw["conv_w"], lw["conv_b"])
        conv_new = jnp.concatenate([state["conv"][:, 1:], up[:, None, :]], axis=1)[None]
    y = _ffn_down(act, lw["w_down"], h, lw["normf_g"], tiles["down_m"], "down_" + tiles["tag"])
    return y, proj, wkv_new, conv_new


def kernel(x_prompt, x_sample, state_wkv, state_shift, state_ffn_conv, cache_win1, cache_win2, cache_win3, rel_bias,
           norm1_g, w_in, gate_b, mu_shift, w0, w_up_decay, a0, w_up_aaa, w_up_gate, k_k, k_a, r_k, gn_g, gn_b,
           w_out_a, w_out_b, w_o, norm2_g, w_up, conv_w, conv_b, w_down, normf_g):
    row = lambda a: a.reshape(1, -1)
    w_lora = jnp.zeros((D_LORA, 3 * D_A), F32)
    w_lora = w_lora.at[0:D_DECAY_LORA, 0:D_A].set(w_up_decay[0])
    w_lora = w_lora.at[D_DECAY_LORA:D_DECAY_LORA + D_AAA_LORA, D_A:2 * D_A].set(w_up_aaa[0])
    w_lora = w_lora.at[D_DECAY_LORA + D_AAA_LORA:, 2 * D_A:].set(w_up_gate[0])
    head = np.arange(D_A) // HEAD_A
    lw = dict(
        norm1_g=norm1_g[0], norm2_g=norm2_g[0], normf_g=normf_g,
        in_bias=jnp.concatenate([jnp.zeros((1, COL_GATE), F32), row(gate_b[0])], axis=1),
        mu=row(mu_shift[0]), w_lora=jnp.stack(_split2(w_lora)), w0=row(w0[0]), a0=row(a0[0]), k_k=row(k_k[0]), k_a=row(k_a[0]),
        r_k=row(r_k[0]), gn_g=row(gn_g[0]), gn_b=row(gn_b[0]),
        hsum=jnp.asarray(head[:, None] == np.arange(LANES)[None, :], BF16),
        hbc=jnp.asarray(np.arange(LANES)[:, None] == head[None, :], BF16),
        w_out_a=w_out_a[0].astype(BF16), w_out_b=w_out_b[0].astype(BF16), w_o=w_o[0].astype(BF16),
        w_up=w_up[0], conv_w=conv_w[0], conv_b=row(conv_b[0]), w_down=w_down[0].astype(BF16),
    )
    bias = [_bias_tables(rel_bias, gi) for gi in range(N_GROUPS)]

    proj_s, lw["w_in"] = _proj_cast(x_sample[:, 0], norm1_g[0], w_in[0], lw["in_bias"])
    y_p, proj_p, wkv_p, conv_p = _layer(x_prompt[0], lw, bias, prompt=True)

    b = DEC_BATCH
    state = dict(wkv=state_wkv, shift=state_shift.reshape(b, D_SHIFT), conv=state_ffn_conv[0],
                 win=(cache_win1, cache_win2, cache_win3))
    y_s, proj_s, wkv_s, conv_s = _layer(x_sample[:, 0], lw, bias, prompt=False, state=state, proj=proj_s)

    def kv_rows(proj, lo, gi):
        k = proj[lo:, COL_QKV + D_B + gi * D_G:COL_QKV + D_B + (gi + 1) * D_G]
        v = proj[lo:, COL_QKV + 2 * D_B + gi * D_G:COL_QKV + 2 * D_B + (gi + 1) * D_G]
        n = k.shape[0]
        return jnp.stack([k.reshape(n, H_G, HEAD_B), v.reshape(n, H_G, HEAD_B)], axis=1)

    win_p = [kv_rows(proj_p, SEQ - min(WINDOWS[gi], SEQ), gi)[None, None] for gi in range(N_GROUPS)]
    win_s = [kv_rows(proj_s, 0, gi)[None, :, None] for gi in range(N_GROUPS)]
    return (y_p[None], y_s[:, None],
            wkv_p, wkv_s,
            proj_p[SEQ - 1:, 0:D_SHIFT][None, None], proj_s[:, 0:D_SHIFT][None, :, None],
            conv_p, conv_s,
            win_p[0], win_s[0], win_p[1], win_s[1], win_p[2], win_s[2])
```

```python
import functools
import math

import numpy as np
import jax
import jax.numpy as jnp
from jax import lax
from jax.experimental import pallas as pl
from jax.experimental.pallas import tpu as pltpu

F32 = jnp.float32
BF16 = jnp.bfloat16
HIGHEST = lax.Precision.HIGHEST

D_MODEL = 2048
SEQ = 8192
DEC_BATCH = 32
HEAD_A = 64
H_A = 16
D_A = H_A * HEAD_A
D_DECAY_LORA = 96
D_AAA_LORA = 96
D_GATE_LORA = 64
D_LORA = D_DECAY_LORA + D_AAA_LORA + D_GATE_LORA
D_SHIFT = 3 * D_A + D_LORA
EPS_GN = 64e-5
HEAD_B = 64
H_G = 8
D_G = H_G * HEAD_B
WINDOWS = (128, 512, 2048)
DILATIONS = (1, 4, 16)
N_GROUPS = 3
D_B = N_GROUPS * D_G
BLK = 128
N_BUCKETS = 32
MAX_DISTANCE = 2048
COL_QKV = D_SHIFT
COL_GATE_SRC = D_SHIFT + 3 * D_B
D_IN = COL_GATE_SRC + 2 * D_MODEL
PROJ_TN = 1024
COL_GATE = -(-COL_GATE_SRC // PROJ_TN) * PROJ_TN
D_IN_PAD = COL_GATE + 2 * D_MODEL
D_FF = 5632
CONV_W = 3
EPS_RMS = 1e-6
CHUNK = 64
SCAN_CHUNKS = 4
F_RT, F_KT, F_QT, F_PT, F_V, F_BONUS, F_G = range(7)
N_FEATS = 7
LANES = 128
SUBLANES = 8
VMEM_LIMIT = 56 * 1024 * 1024
STEP_CACHE_WORDS = 2 * 1024 * 1024


def _params(sem, vmem=VMEM_LIMIT):
    return pltpu.CompilerParams(dimension_semantics=sem, vmem_limit_bytes=vmem)


def _sigmoid(x):
    return 1.0 / (1.0 + jnp.exp(-x))


def _dot_bf16(a, b):
    return jnp.dot(a, b, preferred_element_type=F32)


def _split2(x):
    hi = x.astype(BF16)
    return hi, (x - hi.astype(F32)).astype(BF16)


def _split3(x):
    hi = x.astype(BF16)
    r1 = x - hi.astype(F32)
    mid = r1.astype(BF16)
    lo = (r1 - mid.astype(F32)).astype(BF16)
    return hi, mid, lo


def _dot_exact_rhs(x, m):
    hi, mid, lo = _split3(x)
    return _dot_bf16(hi, m) + _dot_bf16(mid, m) + _dot_bf16(lo, m)


def _head_sums(x, hsum, hbc):
    return _dot_exact_rhs(_dot_exact_rhs(x, hsum), hbc)


def _dot_exact_lhs(m, x):
    hi, mid, lo = _split3(x)
    return _dot_bf16(m, hi) + _dot_bf16(m, mid) + _dot_bf16(m, lo)


def _rms(x, g):
    return x * lax.rsqrt(jnp.mean(x * x, axis=-1, keepdims=True) + EPS_RMS) * g


def _proj_cast_kernel(x_ref, g_ref, w_ref, b_ref, o_ref, wb_ref, xn_scr, *, gap_block):
    j = pl.program_id(0)

    @pl.when(j == 0)
    def _():
        xn_scr[...] = _rms(x_ref[...], g_ref[...]).astype(BF16)

    wb = jnp.where(j != gap_block, w_ref[...], 0.0).astype(BF16)
    wb_ref[...] = wb
    acc = _dot_bf16(xn_scr[...], wb)
    tn = o_ref.shape[1]
    col = lax.broadcasted_iota(jnp.int32, (1, tn), 1) + j * tn
    o_ref[...] = jnp.where(col >= COL_GATE, _sigmoid(acc + b_ref[...]), acc)


def _proj_cast(x, g, w, bias):
    m, k = x.shape
    tn = COL_GATE - COL_GATE_SRC
    assert tn % LANES == 0 and COL_GATE_SRC % tn == 0
    gap_block = COL_GATE_SRC // tn
    return pl.pallas_call(
        functools.partial(_proj_cast_kernel, gap_block=gap_block),
        out_shape=[jax.ShapeDtypeStruct((m, D_IN_PAD), F32), jax.ShapeDtypeStruct((k, D_IN_PAD), BF16)],
        grid=(D_IN_PAD // tn,),
        in_specs=[
            pl.BlockSpec((m, k), lambda j: (0, 0)),
            pl.BlockSpec((1, k), lambda j: (0, 0)),
            pl.BlockSpec((k, tn), lambda j: (0, jnp.where(j < gap_block, j, jnp.maximum(j - 1, 0)))),
            pl.BlockSpec((1, tn), lambda j: (0, j)),
        ],
        out_specs=[pl.BlockSpec((m, tn), lambda j: (0, j)), pl.BlockSpec((k, tn), lambda j: (0, j))],
        scratch_shapes=[pltpu.VMEM((m, k), BF16)],
        compiler_params=_params(("arbitrary",)),
        name="proj_sample_cast_w_in",
    )(x, g.reshape(1, k), w, bias)


def _proj_kernel(x_ref, g_ref, w_ref, b_ref, o_ref, xn_scr):
    @pl.when(pl.program_id(1) == 0)
    def _():
        xn_scr[...] = _rms(x_ref[...], g_ref[...]).astype(BF16)

    acc = _dot_bf16(xn_scr[...], w_ref[...])
    tn = o_ref.shape[1]
    col = lax.broadcasted_iota(jnp.int32, (1, tn), 1) + pl.program_id(1) * tn
    o_ref[...] = jnp.where(col >= COL_GATE, _sigmoid(acc + b_ref[...]), acc)


def _proj(x, g, w, bias, tm, name):
    m, k = x.shape
    n = w.shape[1]
    return pl.pallas_call(
        _proj_kernel,
        out_shape=jax.ShapeDtypeStruct((m, n), F32),
        grid=(m // tm, n // PROJ_TN),
        in_specs=[
            pl.BlockSpec((tm, k), lambda i, j: (i, 0)),
            pl.BlockSpec((1, k), lambda i, j: (0, 0)),
            pl.BlockSpec((k, PROJ_TN), lambda i, j: (0, j)),
            pl.BlockSpec((1, PROJ_TN), lambda i, j: (0, j)),
        ],
        out_specs=pl.BlockSpec((tm, PROJ_TN), lambda i, j: (i, j)),
        scratch_shapes=[pltpu.VMEM((tm, k), BF16)],
        compiler_params=_params(("parallel", "arbitrary")),
        name=name,
    )(x, g.reshape(1, k), w, bias)


def _matmul_kernel(a_ref, w_ref, o_ref):
    o_ref[...] = _dot_bf16(a_ref[...], w_ref[...].astype(BF16))


def _matmul(a, w, tm, tn, name):
    m, k = a.shape
    n = w.shape[1]
    return pl.pallas_call(
        _matmul_kernel,
        out_shape=jax.ShapeDtypeStruct((m, n), F32),
        grid=(m // tm, n // tn),
        in_specs=[pl.BlockSpec((tm, k), lambda i, j: (i, 0)), pl.BlockSpec((k, tn), lambda i, j: (0, j))],
        out_specs=pl.BlockSpec((tm, tn), lambda i, j: (i, j)),
        compiler_params=_params(("parallel", "arbitrary")),
        name=name,
    )(a, w)


def _prep_kernel(p_ref, prev_ref, mu_ref, wl_ref, w0_ref, a0_ref, kk_ref, ka_ref, rk_ref, lmat_ref, sel_ref,
                 hs_ref, hb_ref, f_ref, gc_ref, *, rows_are_time):
    p = p_ref[...]
    tm = p.shape[0]
    if rows_are_time:
        last = jnp.where(pl.program_id(0) == 0, 0.0, prev_ref[7:8, :])
        row = lax.broadcasted_iota(jnp.int32, (tm, 1), 0)
        prev = jnp.where(row == 0, last, pltpu.roll(p, 1, axis=0))
    else:
        prev = prev_ref[...]
    xm = p + mu_ref[...] * (prev - p)
    r = xm[:, 0:D_A]
    k = xm[:, D_A:2 * D_A]
    v = xm[:, 2 * D_A:3 * D_A]
    xl = xm[:, 3 * D_A:D_SHIFT]
    lane = lax.broadcasted_iota(jnp.int32, xl.shape, 1)
    act = jnp.where(lane < D_DECAY_LORA, jnp.tanh(xl),
                    jnp.where(lane < D_DECAY_LORA + D_AAA_LORA, xl, _sigmoid(xl)))
    act_hi, act_lo = _split2(act)
    w_hi, w_lo = wl_ref[0], wl_ref[1]
    lora = _dot_bf16(act_hi, w_hi) + _dot_bf16(act_hi, w_lo) + _dot_bf16(act_lo, w_hi)
    y = -(w0_ref[...] + lora[:, 0:D_A])
    softplus = jnp.maximum(y, 0.0) + jnp.log(1.0 + jnp.exp(-jnp.abs(y)))
    logw = -jnp.exp(-softplus - 0.5)
    a = _sigmoid(a0_ref[...] + lora[:, D_A:2 * D_A])
    g = lora[:, 2 * D_A:3 * D_A]
    kkr = k * kk_ref[...]
    kp = k * (1.0 + (a - 1.0) * ka_ref[...])
    seg = _head_sums(jnp.concatenate([kkr * kkr, r * kp * rk_ref[...]], axis=0), hs_ref[...], hb_ref[...])
    kk = kkr / jnp.maximum(jnp.sqrt(seg[0:tm]), 1e-12)
    cum = _dot_exact_lhs(lmat_ref[...], logw)
    e_out = jnp.exp(-cum)
    feats = {F_RT: r * jnp.exp(cum), F_KT: kp * e_out, F_QT: kk * a * e_out, F_PT: -kk * jnp.exp(cum - logw),
             F_V: v, F_BONUS: seg[tm:2 * tm] * v, F_G: g}
    for n, val in feats.items():
        f_ref[:, n * D_A:(n + 1) * D_A] = val
    gc_ref[...] = jnp.exp(_dot_exact_lhs(sel_ref[...], cum))


def _rwkv_prep(proj, prev, tm, chunk, lw, name):
    m = proj.shape[0]
    rows_are_time = prev is None
    n_tiles = m // tm
    t = np.arange(tm)
    lmat = ((t[:, None] // chunk == t[None, :] // chunk) & (t[None, :] <= t[:, None])).astype(np.float32)
    sel_stride = 1 if chunk == 1 else 8
    n_sel = sel_stride * (tm // chunk)
    sel = np.zeros((n_sel, tm), np.float32)
    for c in range(tm // chunk):
        sel[sel_stride * c, (c + 1) * chunk - 1] = 1.0
    if rows_are_time:
        prev_arr = proj
        prev_spec = pl.BlockSpec((8, D_SHIFT), lambda i: (jnp.maximum(i * (tm // 8) - 1, 0), 0))
    else:
        prev_arr = prev
        prev_spec = pl.BlockSpec((tm, D_SHIFT), lambda i: (i, 0))
    vec = lambda d: pl.BlockSpec((1, d), lambda i: (0, 0))
    full = lambda a: pl.BlockSpec(a.shape, lambda i: (0, 0))
    lmat_b = jnp.asarray(lmat, BF16)
    sel_b = jnp.asarray(sel, BF16)
    return pl.pallas_call(
        functools.partial(_prep_kernel, rows_are_time=rows_are_time),
        out_shape=[jax.ShapeDtypeStruct((m, N_FEATS * D_A), F32),
                   jax.ShapeDtypeStruct((n_tiles * n_sel, D_A), F32)],
        grid=(n_tiles,),
        in_specs=[pl.BlockSpec((tm, D_SHIFT), lambda i: (i, 0)), prev_spec, vec(D_SHIFT),
                  pl.BlockSpec(lw["w_lora"].shape, lambda i: (0, 0, 0)),
                  vec(D_A), vec(D_A), vec(D_A), vec(D_A), vec(D_A), full(lmat_b), full(sel_b), full(lw["hsum"]),
                  full(lw["hbc"])],
        out_specs=[pl.BlockSpec((tm, N_FEATS * D_A), lambda i: (i, 0)), pl.BlockSpec((n_sel, D_A), lambda i: (i, 0))],
        compiler_params=_params(("parallel",)),
        name=name,
    )(proj, prev_arr, lw["mu"], lw["w_lora"], lw["w0"], lw["a0"], lw["k_k"], lw["k_a"], lw["r_k"], lmat_b, sel_b,
      lw["hsum"], lw["hbc"])


_NN = (((2,), (1,)), ((0,), (0,)))
_NT = (((2,), (2,)), ((0,), (0,)))
PAIRS = H_A // 2
LOG2_HEAD = HEAD_A.bit_length() - 1
assert HEAD_A == 1 << LOG2_HEAD and 2 * HEAD_A == LANES


def _lane_head(n):
    return (lax.broadcasted_iota(jnp.int32, (1, 1, n), 2) >> LOG2_HEAD) & 1


def _keep_head(x, head, which):
    return jnp.where(head == which, x, jnp.zeros_like(x))


def _pdot_nn(a, b):
    ah, al = _split2(a)
    bh, bl = _split2(b)
    head = _lane_head(b.shape[2])
    bd = lambda x: jnp.concatenate([_keep_head(x, head, 0), _keep_head(x, head, 1)], axis=1)
    a_cat = jnp.concatenate([ah, ah, al], axis=2)
    b_cat = jnp.concatenate([bd(bh), bd(bl), bd(bh)], axis=1)
    return lax.dot_general(a_cat, b_cat, _NN, preferred_element_type=F32)


def _pdot_nt(a, bs):
    ah, al = _split2(a)
    head = _lane_head(LANES)
    hi, lo = [], []
    for b in bs:
        bh, bl = _split2(b)
        for which in (0, 1):
            hi.append(_keep_head(bh, head, which))
            lo.append(_keep_head(bl, head, which))
    bh, bl = jnp.concatenate(hi, axis=1), jnp.concatenate(lo, axis=1)
    a_cat = jnp.concatenate([ah, ah, al], axis=2)
    b_cat = jnp.concatenate([bh, bl, bh], axis=2)
    return lax.dot_general(a_cat, b_cat, _NT, preferred_element_type=F32)


def _pdot_tn(x, y):
    nx = x.shape[2]
    xh, xl = _split2(jnp.swapaxes(x, 1, 2))
    yh, yl = _split2(y)
    x_cat = jnp.concatenate([xh, xh, xl, jnp.zeros_like(xl)], axis=2)
    y_cat = jnp.concatenate([yh, yl, yh, jnp.zeros_like(yl)], axis=1)
    tot = lax.dot_general(x_cat, y_cat, _NN, preferred_element_type=F32)
    head = _lane_head(LANES)
    return [jnp.where(head == 0, tot[:, g:g + HEAD_A], tot[:, g + HEAD_A:g + LANES]) for g in range(0, nx, LANES)]


def _scan_kernel(f_ref, gc_ref, gng_ref, gnb_ref, ya_ref, sfin_ref, s_scr):
    @pl.when(pl.program_id(0) == 0)
    def _():
        s_scr[...] = jnp.zeros_like(s_scr)

    def pairs(x, rows):
        return jnp.stack([x[c * rows:(c + 1) * rows, i * LANES:(i + 1) * LANES]
                          for c in range(SCAN_CHUNKS) for i in range(PAIRS)], axis=0)

    feat = lambda n: f_ref[:, n * D_A:(n + 1) * D_A]
    p, r, q, k, v = (pairs(feat(n), CHUNK) for n in (F_PT, F_RT, F_QT, F_KT, F_V))
    gam = pairs(gc_ref[...], SUBLANES)[:, 0:1]
    ri = lax.broadcasted_iota(jnp.int32, (1, CHUNK, LANES), 1)
    ci = lax.broadcasted_iota(jnp.int32, (1, CHUNK, LANES), 2) & (HEAD_A - 1)
    strict = ri > ci
    incl = ri >= ci
    eye = (ri == ci).astype(F32)

    gram = _pdot_nt(jnp.concatenate([p, r], axis=1), [q, k])
    a_qp = jnp.where(strict, gram[:, 0:CHUNK, 0:LANES], 0.0)
    a_kp = jnp.where(strict, gram[:, 0:CHUNK, LANES:], 0.0)
    a_rq = jnp.where(incl, gram[:, CHUNK:, 0:LANES], 0.0)
    a_rk = jnp.where(incl, gram[:, CHUNK:, LANES:], 0.0)
    same = lambda log2_bs: (ri >> log2_bs) == (ci >> log2_bs)
    a_d = jnp.where(same(4), a_qp, 0.0)
    tinv = eye + a_d
    pw = _pdot_nn(a_d, a_d)
    for _ in range(2):
        both = _pdot_nn(jnp.concatenate([pw, tinv], axis=1), pw)
        pw = both[:, 0:CHUNK]
        tinv = tinv + both[:, CHUNK:]
    tinv = tinv + _pdot_nn(tinv, pw)
    for log2_bs in (4, 5):
        off = jnp.where(jnp.logical_and(same(log2_bs + 1), jnp.logical_not(same(log2_bs))), a_qp, 0.0)
        tinv = tinv + _pdot_nn(tinv, _pdot_nn(off, tinv))
    av = _pdot_nn(jnp.concatenate([a_kp, a_rk], axis=1), v)
    pw_hat = _pdot_nn(tinv, jnp.concatenate([p, av[:, 0:CHUNK]], axis=2))
    x = _pdot_nn(a_rq, pw_hat)
    r_hat = r + x[:, :, 0:LANES]
    o_loc = x[:, :, LANES:] + av[:, CHUNK:]
    y_p, y_w = _pdot_tn(pw_hat, q)
    (vk,) = _pdot_tn(v, k)
    m_mat = (eye + y_p) * gam
    n_mat = (y_w + vk) * gam

    state = s_scr[...]
    outs = []
    for c in range(SCAN_CHUNKS):
        sl = slice(c * PAIRS, (c + 1) * PAIRS)
        outs.append(_pdot_nt(r_hat[sl], [state]) + o_loc[sl])
        state = _pdot_nn(state, m_mat[sl]) + n_mat[sl]
    s_scr[...] = state
    o = jnp.concatenate(outs, axis=0)

    head = _lane_head(LANES)

    def head_mean(z):
        sums = [jnp.sum(_keep_head(z, head, which), axis=2, keepdims=True) for which in (0, 1)]
        return jnp.where(head == 0, sums[0], sums[1]) * (1.0 / HEAD_A)

    d = o - head_mean(o)
    o_n = d * lax.rsqrt(head_mean(d * d) + EPS_GN)
    o_n = jnp.concatenate([jnp.concatenate([o_n[c * PAIRS + i] for i in range(PAIRS)], axis=1)
                           for c in range(SCAN_CHUNKS)], axis=0)
    ya_ref[...] = ((o_n * gng_ref[...] + gnb_ref[...] + feat(F_BONUS)) * feat(F_G)).astype(ya_ref.dtype)

    @pl.when(pl.program_id(0) == pl.num_programs(0) - 1)
    def _():
        sfin_ref[...] = s_scr[...]


def _wkv_scan(feats, gc, lw):
    t = feats.shape[0]
    vec = pl.BlockSpec((1, D_A), lambda c: (0, 0))
    pair_state = (PAIRS, HEAD_A, LANES)
    rows = SCAN_CHUNKS * CHUNK
    ya, s = pl.pallas_call(
        _scan_kernel,
        out_shape=[jax.ShapeDtypeStruct((t, D_A), BF16), jax.ShapeDtypeStruct(pair_state, F32)],
        grid=(t // rows,),
        in_specs=[pl.BlockSpec((rows, N_FEATS * D_A), lambda c: (c, 0)),
                  pl.BlockSpec((SUBLANES * SCAN_CHUNKS, D_A), lambda c: (c, 0)), vec, vec],
        out_specs=[pl.BlockSpec((rows, D_A), lambda c: (c, 0)), pl.BlockSpec(pair_state, lambda c: (0, 0, 0))],
        scratch_shapes=[pltpu.VMEM(pair_state, F32)],
        compiler_params=_params(("arbitrary",)),
        name="wkv_scan_prompt",
    )(feats, gc, lw["gn_g"], lw["gn_b"])
    s = s.reshape(PAIRS, HEAD_A, 2, HEAD_A).transpose(0, 2, 1, 3).reshape(H_A, HEAD_A, HEAD_A)
    return ya, s


def _wkv_step_kernel(s_ref, rows_ref, sn_ref, o_ref):
    ri = lax.broadcasted_iota(jnp.int32, (1, 1, HEAD_A, HEAD_A), 2)
    ci = lax.broadcasted_iota(jnp.int32, (1, 1, HEAD_A, HEAD_A), 3)
    eye = ri == ci
    pt, rt, qt, kt, v, gc = (rows_ref[:, n] for n in range(6))
    s = s_ref[0]
    u = jnp.sum(s * pt, axis=3, keepdims=True)
    v_col = jnp.sum(jnp.where(eye, v, 0.0), axis=3, keepdims=True)
    m = s + u * qt + v_col * kt
    sn_ref[0] = m * gc
    o_col = jnp.sum(m * rt, axis=3, keepdims=True)
    o_ref[...] = jnp.sum(jnp.where(eye, o_col, 0.0), axis=2, keepdims=True)


def _wkv_step(state, pt, rt, qt, kt, v, gc, bt):
    b = pt.shape[0]
    rows = jnp.stack([pt, rt, qt, kt, v, gc], axis=1).reshape(b, 6, H_A, 1, HEAD_A)
    st_spec = pl.BlockSpec((1, bt, H_A, HEAD_A, HEAD_A), lambda i: (0, i, 0, 0, 0))
    sn, o = pl.pallas_call(
        _wkv_step_kernel,
        out_shape=[jax.ShapeDtypeStruct(state.shape, F32), jax.ShapeDtypeStruct((b, H_A, 1, HEAD_A), F32)],
        grid=(b // bt,),
        in_specs=[st_spec, pl.BlockSpec((bt, 6, H_A, 1, HEAD_A), lambda i: (i, 0, 0, 0, 0))],
        out_specs=[st_spec, pl.BlockSpec((bt, H_A, 1, HEAD_A), lambda i: (i, 0, 0, 0))],
        compiler_params=_params(("parallel",)),
        name="wkv_step_sample",
    )(state, rows)
    return sn, o.reshape(b, D_A)


def _post_kernel(o_ref, bonus_ref, g_ref, gng_ref, gnb_ref, hs_ref, hb_ref, y_ref):
    o = o_ref[...]
    hsum, hbc = hs_ref[...], hb_ref[...]
    mu = _head_sums(o, hsum, hbc) * (1.0 / HEAD_A)
    d = o - mu
    var = _head_sums(d * d, hsum, hbc) * (1.0 / HEAD_A)
    o_n = d * lax.rsqrt(var + EPS_GN) * gng_ref[...] + gnb_ref[...]
    y_ref[...] = ((o_n + bonus_ref[...]) * g_ref[...]).astype(y_ref.dtype)


def _rwkv_post(o, bonus, g, lw, tm, name):
    m = o.shape[0]
    big = pl.BlockSpec((tm, D_A), lambda i: (i, 0))
    vec = pl.BlockSpec((1, D_A), lambda i: (0, 0))
    return pl.pallas_call(
        _post_kernel,
        out_shape=jax.ShapeDtypeStruct((m, D_A), BF16),
        grid=(m // tm,),
        in_specs=[big, big, big, vec, vec, pl.BlockSpec((D_A, LANES), lambda i: (0, 0)),
                  pl.BlockSpec((LANES, D_A), lambda i: (0, 0))],
        out_specs=big,
        compiler_params=_params(("parallel",)),
        name=name,
    )(o, bonus, g, lw["gn_g"], lw["gn_b"], lw["hsum"], lw["hbc"])


def _attn_kernel(q_ref, kc_ref, vc_ref, kh_ref, vh_ref, bias_ref, o_ref, l_ref, *, dil, m_blocks, tiled):
    scale = HEAD_B ** -0.5
    n_streams = SUBLANES if tiled else dil
    n_units = n_streams * m_blocks

    stride = SUBLANES if tiled else dil
    if tiled:
        flat = lambda ref: ref.reshape(ref.shape[0] * SUBLANES, LANES)
        q_ref, kc_ref, vc_ref, kh_ref, vh_ref, o_ref, l_ref = map(flat, (q_ref, kc_ref, vc_ref, kh_ref, vh_ref,
                                                                          o_ref, l_ref))

    def rows(ref, r, start, size):
        if stride == 1:
            return ref[pl.ds(start, size), :]
        return ref[pl.ds(r + stride * start, size, stride=stride), :]

    def put(ref, r, start, val):
        if stride == 1:
            ref[pl.ds(start, BLK), :] = val
        else:
            ref[pl.ds(r + stride * start, BLK, stride=stride), :] = val

    qs, ks, vs, units = [], [], [], []
    for r in range(n_streams):
        for mb in range(m_blocks):
            units.append((r, mb))
            qs.append(rows(q_ref, r, BLK * mb, BLK))
            if mb == 0:
                ks.append(jnp.concatenate([rows(kh_ref, r, 0, BLK), rows(kc_ref, r, 0, BLK)], axis=0))
                vs.append(jnp.concatenate([rows(vh_ref, r, 0, BLK), rows(vc_ref, r, 0, BLK)], axis=0))
            else:
                ks.append(rows(kc_ref, r, BLK * (mb - 1), 2 * BLK))
                vs.append(rows(vc_ref, r, BLK * (mb - 1), 2 * BLK))

    def batch(xs, mul=None):
        pre = (lambda x: x) if mul is None else (lambda x: x * mul)
        return jnp.stack([pre(x[:, sub * HEAD_B:(sub + 1) * HEAD_B]).astype(BF16) for sub in range(2) for x in xs],
                         axis=0)

    assert math.frexp(scale)[0] == 0.5
    qb, kb, vb = batch(qs, scale), batch(ks), batch(vs)
    s = lax.dot_general(qb, kb, (((2,), (2,)), ((0,), (0,))), preferred_element_type=F32)
    s = s.reshape(2, n_units, BLK, 2 * BLK) + bias_ref[...][:, None]
    unit = lax.broadcasted_iota(jnp.int32, (1, n_units, 1, 2 * BLK), 1)
    col = lax.broadcasted_iota(jnp.int32, (1, n_units, 1, 2 * BLK), 3)
    no_prev = jnp.logical_and(jnp.logical_and(pl.program_id(0) == 0, (unit & (m_blocks - 1)) == 0), col < BLK)
    s = jnp.where(no_prev, -jnp.inf, s)
    m = jnp.max(s, axis=-1, keepdims=True)
    p = jnp.exp(s - m)
    l = jnp.sum(p, axis=-1, keepdims=True)
    pv = lax.dot_general(p.astype(BF16).reshape(2 * n_units, BLK, 2 * BLK), vb, (((2,), (1,)), ((0,), (0,))),
                         preferred_element_type=F32)
    o = pv.reshape(2, n_units, BLK, HEAD_B) / l
    lse = jnp.broadcast_to(m + jnp.log(l), (2, n_units, BLK, HEAD_B))
    for u, (r, mb) in enumerate(units):
        put(o_ref, r, BLK * mb, jnp.concatenate([o[0, u], o[1, u]], axis=1))
        put(l_ref, r, BLK * mb, jnp.concatenate([lse[0, u], lse[1, u]], axis=1))


def _attn_prompt(proj, bias, gi, rows_per_step):
    t, n_cols = proj.shape
    dil = DILATIONS[gi]
    span = BLK * dil
    tiled = dil % SUBLANES == 0
    if tiled:
        rows_per_step *= dil // SUBLANES
    m_blocks = rows_per_step // span
    cq = (COL_QKV + gi * D_G) // LANES
    ck = (COL_QKV + D_B + gi * D_G) // LANES
    cv = (COL_QKV + 2 * D_B + gi * D_G) // LANES
    bias_block = (2, BLK, 2 * BLK)
    if tiled:
        n_hi = dil // SUBLANES
        src = proj.reshape(t // dil, n_hi, SUBLANES, n_cols)
        grid = (t // rows_per_step, n_hi, H_G // 2)
        l_rows = rows_per_step // dil
        cur = lambda c0: pl.BlockSpec((l_rows, None, SUBLANES, LANES), lambda i, rh, hp: (i, rh, 0, c0 + hp))
        halo = lambda c0: pl.BlockSpec((BLK, None, SUBLANES, LANES),
                                       lambda i, rh, hp: (jnp.maximum(i * m_blocks - 1, 0), rh, 0, c0 + hp))
        out_spec = pl.BlockSpec((l_rows, None, SUBLANES, LANES), lambda i, rh, hp: (i, rh, 0, hp))
        out = jax.ShapeDtypeStruct((t // dil, n_hi, SUBLANES, D_G), F32)
        bias_spec = pl.BlockSpec(bias_block, lambda i, rh, hp: (hp, 0, 0))
    else:
        src = proj
        grid = (t // rows_per_step, H_G // 2)
        cur = lambda c0: pl.BlockSpec((rows_per_step, LANES), lambda i, hp: (i, c0 + hp))
        halo = lambda c0: pl.BlockSpec((span, LANES), lambda i, hp: (jnp.maximum(i * m_blocks - 1, 0), c0 + hp))
        out_spec = pl.BlockSpec((rows_per_step, LANES), lambda i, hp: (i, hp))
        out = jax.ShapeDtypeStruct((t, D_G), F32)
        bias_spec = pl.BlockSpec(bias_block, lambda i, hp: (hp, 0, 0))
    o, l = pl.pallas_call(
        functools.partial(_attn_kernel, dil=dil, m_blocks=m_blocks, tiled=tiled),
        out_shape=[out, out],
        grid=grid,
        in_specs=[cur(cq), cur(ck), cur(cv), halo(ck), halo(cv), bias_spec],
        out_specs=[out_spec, out_spec],
        compiler_params=_params(("parallel",) * len(grid)),
        name=f"attn_prompt_g{gi}",
    )(src, src, src, src, src, bias)
    return o.reshape(t, D_G), l.reshape(t, D_G)


def _attn_step_kernel(q_ref, kn_ref, vn_ref, c_ref, bias_ref, bias0_ref, o_ref, l_ref):
    scale = HEAD_B ** -0.5
    ri = lax.broadcasted_iota(jnp.int32, (1, 1, HEAD_B, HEAD_B), 2)
    ci = lax.broadcasted_iota(jnp.int32, (1, 1, HEAD_B, HEAD_B), 3)
    eye = ri == ci
    q = q_ref[...]
    q_col = jnp.sum(jnp.where(eye, q, 0.0), axis=3, keepdims=True)
    s = jnp.sum(c_ref[:, 0] * q_col, axis=2, keepdims=True) * scale + bias_ref[...]
    s_new = jnp.sum(q * kn_ref[...], axis=3, keepdims=True) * scale + bias0_ref[...]
    m = jnp.maximum(jnp.max(s, axis=3, keepdims=True), s_new)
    p = jnp.exp(s - m)
    p_new = jnp.exp(s_new - m)
    l = jnp.sum(p, axis=3, keepdims=True) + p_new
    o_col = jnp.sum(c_ref[:, 1] * p, axis=3, keepdims=True)
    o_row = jnp.sum(jnp.where(eye, o_col, 0.0), axis=2, keepdims=True)
    o_ref[...] = (o_row + p_new * vn_ref[...]) / l
    l_ref[...] = jnp.broadcast_to(m + jnp.log(l), o_ref.shape)


def _attn_step(q, k_new, v_new, cache, bias, bias0, gi):
    b = q.shape[0]
    w = cache.shape[2]
    bt = max(1, min(b, STEP_CACHE_WORDS // (2 * D_G * w)))
    cache_t = jnp.transpose(cache, (0, 1, 3, 4, 5, 2))
    vec = pl.BlockSpec((bt, H_G, 1, HEAD_B), lambda i: (i, 0, 0, 0))
    out = jax.ShapeDtypeStruct((b, H_G, 1, HEAD_B), F32)
    o, l = pl.pallas_call(
        _attn_step_kernel,
        out_shape=[out, out],
        grid=(b // bt,),
        in_specs=[vec, vec, vec,
                  pl.BlockSpec((None, bt, 2, H_G, HEAD_B, w), lambda i: (0, i, 0, 0, 0, 0)),
                  pl.BlockSpec((H_G, 1, w), lambda i: (0, 0, 0)),
                  pl.BlockSpec((H_G, 1, 1), lambda i: (0, 0, 0))],
        out_specs=[vec, vec],
        compiler_params=_params(("parallel",)),
        name=f"attn_step_g{gi}",
    )(q, k_new, v_new, cache_t, bias, bias0)
    return o.reshape(b, D_G), l.reshape(b, D_G)


def _mix_kernel(ya_ref, o1_ref, o2_ref, o3_ref, l1_ref, l2_ref, l3_ref, wa_ref, wb_ref, ga_ref, gb_ref, wo_ref, x_ref,
                g_ref, h_ref, hn_ref):
    l1, l2, l3 = l1_ref[...], l2_ref[...], l3_ref[...]
    m = jnp.maximum(jnp.maximum(l1, l2), l3)
    e1, e2, e3 = jnp.exp(l1 - m), jnp.exp(l2 - m), jnp.exp(l3 - m)
    den = e1 + e2 + e3
    yb = ((e1 / den) * o1_ref[...] + (e2 / den) * o2_ref[...] + (e3 / den) * o3_ref[...]).astype(BF16)
    mixed = ga_ref[...] * _dot_bf16(ya_ref[...], wa_ref[...]) + gb_ref[...] * _dot_bf16(yb, wb_ref[...])
    h = x_ref[...] + _dot_bf16(mixed.astype(BF16), wo_ref[...])
    h_ref[...] = h
    hn_ref[...] = _rms(h, g_ref[...]).astype(BF16)


def _mix(ya, outs, lses, wa, wb, proj, w_o, x, g, tm, name):
    m = ya.shape[0]
    n = D_MODEL
    assert COL_GATE % n == 0
    ga0 = COL_GATE // n
    grp = pl.BlockSpec((tm, D_G), lambda i: (i, 0))
    row = pl.BlockSpec((tm, n), lambda i: (i, 0))
    resident = lambda rows: pl.BlockSpec((rows, n), lambda i: (0, 0), pipeline_mode=pl.Buffered(1))
    return pl.pallas_call(
        _mix_kernel,
        out_shape=[jax.ShapeDtypeStruct((m, n), F32), jax.ShapeDtypeStruct((m, n), BF16)],
        grid=(m // tm,),
        in_specs=[pl.BlockSpec((tm, D_A), lambda i: (i, 0))] + [grp] * 6 + [
            resident(D_A), resident(D_G),
            pl.BlockSpec((tm, n), lambda i: (i, ga0)),
            pl.BlockSpec((tm, n), lambda i: (i, ga0 + 1)),
            resident(n), row, pl.BlockSpec((1, n), lambda i: (0, 0)),
        ],
        out_specs=[row, row],
        compiler_params=_params(("parallel",)),
        name=name,
    )(ya, *outs, *lses, wa, wb, proj, proj, w_o, x, g.reshape(1, n))


def _wo_kernel(a_ref, w_ref, x_ref, g_ref, h_ref, hn_ref):
    h = x_ref[...] + _dot_bf16(a_ref[...], w_ref[...])
    h_ref[...] = h
    hn_ref[...] = _rms(h, g_ref[...]).astype(BF16)


def _wo(a, w, x, g, tm, name):
    m, k = a.shape
    n = w.shape[1]
    row = pl.BlockSpec((tm, n), lambda i: (i, 0))
    return pl.pallas_call(
        _wo_kernel,
        out_shape=[jax.ShapeDtypeStruct((m, n), F32), jax.ShapeDtypeStruct((m, n), BF16)],
        grid=(m // tm,),
        in_specs=[pl.BlockSpec((tm, k), lambda i: (i, 0)), pl.BlockSpec((k, n), lambda i: (0, 0)), row,
                  pl.BlockSpec((1, n), lambda i: (0, 0))],
        out_specs=[row, row],
        compiler_params=_params(("parallel",)),
        name=name,
    )(a, w, x, g.reshape(1, n))


def _gelu(x):
    return 0.5 * x * (1.0 + lax.erf(x * (1.0 / math.sqrt(2.0))))


def _ffn_up_kernel(h_ref, w1_ref, w2_ref, cw1_ref, cw2_ref, cb1_ref, cb2_ref, act_ref, t1_ref, t2_ref,
                   c1_scr, c2_scr, w1_scr, w2_scr):
    tm = h_ref.shape[0]

    @pl.when(pl.program_id(1) == 0)
    def _():
        c1_scr[...] = jnp.zeros_like(c1_scr)
        c2_scr[...] = jnp.zeros_like(c2_scr)
        w1_scr[...] = w1_ref[...].astype(BF16)
        w2_scr[...] = w2_ref[...].astype(BF16)

    a = h_ref[...]
    row = lax.broadcasted_iota(jnp.int32, (tm, 1), 0)

    def conv(u, carry_ref, cw_ref, cb_ref):
        m1 = jnp.where(row == 0, carry_ref[7:8, :], pltpu.roll(u, 1, axis=0))
        m2 = jnp.where(row == 0, carry_ref[6:7, :], jnp.where(row == 1, carry_ref[7:8, :], pltpu.roll(u, 2, axis=0)))
        return cb_ref[...] + cw_ref[0:1, :] * m2 + cw_ref[1:2, :] * m1 + cw_ref[2:3, :] * u

    u1 = _dot_bf16(a, w1_scr[...])
    u2 = _dot_bf16(a, w2_scr[...])
    c1 = conv(u1, c1_scr, cw1_ref, cb1_ref)
    c2 = conv(u2, c2_scr, cw2_ref, cb2_ref)
    act_ref[...] = (_gelu(c1) * c2).astype(act_ref.dtype)
    c1_scr[...] = u1[tm - 8:tm]
    c2_scr[...] = u2[tm - 8:tm]
    t1_ref[...] = u1[tm - 8:tm]
    t2_ref[...] = u2[tm - 8:tm]


def _ffn_up_prompt(hn, w_up, conv_w, conv_b, tm, tn):
    m = hn.shape[0]
    nj = D_FF // tn
    tail = jax.ShapeDtypeStruct((8, D_FF), F32)
    return pl.pallas_call(
        _ffn_up_kernel,
        out_shape=[jax.ShapeDtypeStruct((m, D_FF), BF16), tail, tail],
        grid=(nj, m // tm),
        in_specs=[
            pl.BlockSpec((tm, D_MODEL), lambda j, i: (i, 0)),
            pl.BlockSpec((D_MODEL, tn), lambda j, i: (0, j)),
            pl.BlockSpec((D_MODEL, tn), lambda j, i: (0, nj + j)),
            pl.BlockSpec((CONV_W, tn), lambda j, i: (0, j)),
            pl.BlockSpec((CONV_W, tn), lambda j, i: (0, nj + j)),
            pl.BlockSpec((1, tn), lambda j, i: (0, j)),
            pl.BlockSpec((1, tn), lambda j, i: (0, nj + j)),
        ],
        out_specs=[pl.BlockSpec((tm, tn), lambda j, i: (i, j)),
                   pl.BlockSpec((8, tn), lambda j, i: (0, j)),
                   pl.BlockSpec((8, tn), lambda j, i: (0, j))],
        scratch_shapes=[pltpu.VMEM((8, tn), F32), pltpu.VMEM((8, tn), F32),
                        pltpu.VMEM((D_MODEL, tn), BF16), pltpu.VMEM((D_MODEL, tn), BF16)],
        compiler_params=_params(("arbitrary", "arbitrary")),
        name="ffn_up_prompt",
    )(hn, w_up, w_up, conv_w, conv_w, conv_b, conv_b)


def _ffn_act_step_kernel(up_ref, prev_ref, cw_ref, cb_ref, act_ref):
    up = up_ref[...]
    w = 2 * D_FF
    c = cb_ref[...] + cw_ref[0:1, :] * prev_ref[:, 0:w] + cw_ref[1:2, :] * prev_ref[:, w:2 * w] + cw_ref[2:3, :] * up
    act_ref[...] = (_gelu(c[:, 0:D_FF]) * c[:, D_FF:w]).astype(act_ref.dtype)


def _ffn_act_step(up, conv_prev, conv_w, conv_b):
    b = up.shape[0]
    return pl.pallas_call(
        _ffn_act_step_kernel,
        out_shape=jax.ShapeDtypeStruct((b, D_FF), BF16),
        compiler_params=pltpu.CompilerParams(vmem_limit_bytes=VMEM_LIMIT),
        name="ffn_act_sample",
    )(up, conv_prev.reshape(b, (CONV_W - 1) * 2 * D_FF), conv_w, conv_b)


def _down_kernel(a_ref, w_ref, h_ref, g_ref, o_ref):
    o_ref[...] = _rms(h_ref[...] + _dot_bf16(a_ref[...], w_ref[...]), g_ref[...])


def _ffn_down(act, w_down, h, g, tm, name):
    m = act.shape[0]
    return pl.pallas_call(
        _down_kernel,
        out_shape=jax.ShapeDtypeStruct((m, D_MODEL), F32),
        grid=(m // tm,),
        in_specs=[
            pl.BlockSpec((tm, D_FF), lambda i: (i, 0)),
            pl.BlockSpec((D_FF, D_MODEL), lambda i: (0, 0), pipeline_mode=pl.Buffered(1)),
            pl.BlockSpec((tm, D_MODEL), lambda i: (i, 0)),
            pl.BlockSpec((1, D_MODEL), lambda i: (0, 0)),
        ],
        out_specs=pl.BlockSpec((tm, D_MODEL), lambda i: (i, 0)),
        compiler_params=_params(("parallel",)),
        name=name,
    )(act, w_down, h, g.reshape(1, D_MODEL))


def _rel_bucket(dist):
    max_exact = N_BUCKETS // 2
    d_f = np.maximum(dist, 1).astype(np.float64)
    scaled = np.log(d_f / max_exact) / math.log(MAX_DISTANCE / max_exact) * (N_BUCKETS - max_exact)
    large = np.minimum(max_exact + scaled.astype(np.int64), N_BUCKETS - 1)
    return np.where(dist < max_exact, dist, large).astype(np.int32)


def _bias_rows(tab, dist):
    onehot = (jnp.asarray(_rel_bucket(dist))[None, :] == jnp.arange(N_BUCKETS)[:, None]).astype(F32)
    return jnp.dot(tab.T, onehot, precision=HIGHEST)


def _bias_tables(rel_bias, gi):
    dil = DILATIONS[gi]
    win = WINDOWS[gi]
    reach = win // dil
    tab = rel_bias[:, gi * H_G:(gi + 1) * H_G]
    qi = np.arange(BLK)[:, None]
    ki = np.arange(2 * BLK)[None, :]
    rel = qi + BLK - ki
    blk = _bias_rows(tab, (dil * np.maximum(rel, 0)).reshape(-1)).reshape(H_G, BLK, 2 * BLK)
    blk = jnp.where(((rel >= 0) & (rel <= reach))[None], blk, -jnp.inf)
    back = win - np.arange(win)
    step = jnp.where((back % dil == 0)[None, :], _bias_rows(tab, back), -jnp.inf)[:, None, :]
    step0 = _bias_rows(tab, np.zeros((1,), np.int64))[:, None, :]
    return blk, step, step0


def _tiles(m, prompt):
    if prompt:
        return dict(tag="prompt", proj_m=1024, prep_m=2 * CHUNK, attn_rows=2048, mix_m=256, wo_m=512, up_m=1024,
                    up_n=512, down_m=256)
    return dict(tag="sample", proj_m=m, prep_m=m, step_b=4, post_m=m, mix_m=m, wo_m=m, up_n=512, down_m=m)


def _layer(x, lw, bias_blk, prompt, state=None, proj=None):
    m = x.shape[0]
    tiles = _tiles(m, prompt)
    if proj is None:
        proj = _proj(x, lw["norm1_g"], lw["w_in"], lw["in_bias"], tiles["proj_m"], "proj_" + tiles["tag"])

    if prompt:
        feats, gc = _rwkv_prep(proj, None, tiles["prep_m"], CHUNK, lw, "rwkv_prep_prompt")
        ya, wkv_new = _wkv_scan(feats, gc, lw)
        wkv_new = wkv_new[None, None]
    else:
        feats, gc = _rwkv_prep(proj, state["shift"], tiles["prep_m"], 1, lw, "rwkv_prep_sample")
        f = lambda n: feats[:, n * D_A:(n + 1) * D_A]
        wkv_new, o = _wkv_step(state["wkv"], f(F_PT), f(F_RT), f(F_QT), f(F_KT), f(F_V), gc, tiles["step_b"])
        ya = _rwkv_post(o, f(F_BONUS), f(F_G), lw, tiles["post_m"], "rwkv_post_" + tiles["tag"])

    outs, lses = [], []
    for gi in range(N_GROUPS):
        if prompt:
            o_g, l_g = _attn_prompt(proj, bias_blk[gi][0], gi, tiles["attn_rows"])
        else:
            sl = lambda c0: proj[:, c0 + gi * D_G:c0 + (gi + 1) * D_G].reshape(m, H_G, 1, HEAD_B)
            o_g, l_g = _attn_step(sl(COL_QKV), sl(COL_QKV + D_B), sl(COL_QKV + 2 * D_B), state["win"][gi],
                                  bias_blk[gi][1], bias_blk[gi][2], gi)
        outs.append(o_g)
        lses.append(l_g)
    h, hn = _mix(ya, outs, lses, lw["w_out_a"], lw["w_out_b"], proj, lw["w_o"], x, lw["norm2_g"], tiles["mix_m"],
                 "mix_" + tiles["tag"])

    if prompt:
        act, t1, t2 = _ffn_up_prompt(hn, lw["w_up"], lw["conv_w"], lw["conv_b"], tiles["up_m"], tiles["up_n"])
        conv_new = jnp.concatenate([t1[6:8], t2[6:8]], axis=1)[None, None]
    else:
        up = _matmul(hn, lw["w_up"], m, tiles["up_n"], "up_sample")
        act = _ffn_act_step(up, state["conv"], lw["conv_w"], lw["conv_b"])
        conv_new = jnp.concatenate([state["conv"][:, 1:], up[:, None, :]], axis=1)[None]
    y = _ffn_down(act, lw["w_down"], h, lw["normf_g"], tiles["down_m"], "down_" + tiles["tag"])
    return y, proj, wkv_new, conv_new


def kernel(x_prompt, x_sample, state_wkv, state_shift, state_ffn_conv, cache_win1, cache_win2, cache_win3, rel_bias,
           norm1_g, w_in, gate_b, mu_shift, w0, w_up_decay, a0, w_up_aaa, w_up_gate, k_k, k_a, r_k, gn_g, gn_b,
           w_out_a, w_out_b, w_o, norm2_g, w_up, conv_w, conv_b, w_down, normf_g):
    row = lambda a: a.reshape(1, -1)
    w_lora = jnp.zeros((D_LORA, 3 * D_A), F32)
    w_lora = w_lora.at[0:D_DECAY_LORA, 0:D_A].set(w_up_decay[0])
    w_lora = w_lora.at[D_DECAY_LORA:D_DECAY_LORA + D_AAA_LORA, D_A:2 * D_A].set(w_up_aaa[0])
    w_lora = w_lora.at[D_DECAY_LORA + D_AAA_LORA:, 2 * D_A:].set(w_up_gate[0])
    head = np.arange(D_A) // HEAD_A
    lw = dict(
        norm1_g=norm1_g[0], norm2_g=norm2_g[0], normf_g=normf_g,
        in_bias=jnp.concatenate([jnp.zeros((1, COL_GATE), F32), row(gate_b[0])], axis=1),
        mu=row(mu_shift[0]), w_lora=jnp.stack(_split2(w_lora)), w0=row(w0[0]), a0=row(a0[0]), k_k=row(k_k[0]), k_a=row(k_a[0]),
        r_k=row(r_k[0]), gn_g=row(gn_g[0]), gn_b=row(gn_b[0]),
        hsum=jnp.asarray(head[:, None] == np.arange(LANES)[None, :], BF16),
        hbc=jnp.asarray(np.arange(LANES)[:, None] == head[None, :], BF16),
        w_out_a=w_out_a[0].astype(BF16), w_out_b=w_out_b[0].astype(BF16), w_o=w_o[0].astype(BF16),
        w_up=w_up[0], conv_w=conv_w[0], conv_b=row(conv_b[0]), w_down=w_down[0].astype(BF16),
    )
    bias = [_bias_tables(rel_bias, gi) for gi in range(N_GROUPS)]

    proj_s, lw["w_in"] = _proj_cast(x_sample[:, 0], norm1_g[0], w_in[0], lw["in_bias"])
    y_p, proj_p, wkv_p, conv_p = _layer(x_prompt[0], lw, bias, prompt=True)

    b = DEC_BATCH
    state = dict(wkv=state_wkv, shift=state_shift.reshape(b, D_SHIFT), conv=state_ffn_conv[0],
                 win=(cache_win1, cache_win2, cache_win3))
    y_s, proj_s, wkv_s, conv_s = _layer(x_sample[:, 0], lw, bias, prompt=False, state=state, proj=proj_s)

    def kv_rows(proj, lo, gi):
        k = proj[lo:, COL_QKV + D_B + gi * D_G:COL_QKV + D_B + (gi + 1) * D_G]
        v = proj[lo:, COL_QKV + 2 * D_B + gi * D_G:COL_QKV + 2 * D_B + (gi + 1) * D_G]
        n = k.shape[0]
        return jnp.stack([k.reshape(n, H_G, HEAD_B), v.reshape(n, H_G, HEAD_B)], axis=1)

    win_p = [kv_rows(proj_p, SEQ - min(WINDOWS[gi], SEQ), gi)[None, None] for gi in range(N_GROUPS)]
    win_s = [kv_rows(proj_s, 0, gi)[None, :, None] for gi in range(N_GROUPS)]
    return (y_p[None], y_s[:, None],
            wkv_p, wkv_s,
            proj_p[SEQ - 1:, 0:D_SHIFT][None, None], proj_s[:, 0:D_SHIFT][None, :, None],
            conv_p, conv_s,
            win_p[0], win_s[0], win_p[1], win_s[1], win_p[2], win_s[2])
```
